```python
import jax, jax.numpy as jnp
from jax import lax
import numpy as np

D_MODEL = 1024
BATCH = 8
SEQ = 8192
DEPTH = 1

HEAD_DIM = 64
N_Q_HEADS = 16
N_KV_HEADS = 4
GROUP = N_Q_HEADS // N_KV_HEADS
ATTN_WIDTH = N_Q_HEADS * HEAD_DIM
KV_WIDTH = N_KV_HEADS * HEAD_DIM
WINDOW = 128
BLOCK = 128
CONV_CH = D_MODEL
CONV_WIDTH = 31
D_FF = 4 * D_MODEL
N_BUCKETS = 32
MAX_DISTANCE = 128
EPS = 1e-6
NEG = -1e30
Q_END = ATTN_WIDTH
K_END = Q_END + KV_WIDTH
V_END = K_END + KV_WIDTH
GLU_END = V_END + 2 * CONV_CH
IN_WIDTH = GLU_END + 2 * D_MODEL

kernel_name = "hybrid_swa_sink_conformer_gated_block"


def rms_norm(x, g):
    xf = x.astype(jnp.float32)
    y = xf * lax.rsqrt(jnp.mean(xf * xf, axis=-1, keepdims=True) + EPS)
    return (y * g.astype(jnp.float32)).astype(x.dtype)


def layer_norm(x, g, b):
    xf = x.astype(jnp.float32)
    mu = jnp.mean(xf, axis=-1, keepdims=True)
    xc = xf - mu
    var = jnp.mean(xc * xc, axis=-1, keepdims=True)
    y = xc * lax.rsqrt(var + EPS) * g.astype(jnp.float32) + b.astype(jnp.float32)
    return y.astype(x.dtype)


def t5_causal_bucket(dist):
    n = jnp.maximum(dist, 0)
    max_exact = N_BUCKETS // 2
    nf = jnp.maximum(n, 1).astype(jnp.float32)
    large = max_exact + (jnp.log(nf / max_exact) / np.float32(np.log(MAX_DISTANCE / max_exact))
                         * (N_BUCKETS - max_exact)).astype(jnp.int32)
    large = jnp.minimum(large, N_BUCKETS - 1)
    return jnp.where(n < max_exact, n, large)


def band_blocks(t, nb):
    b = t.shape[0]
    tp = jnp.pad(t, ((0, 0), (BLOCK, 0), (0, 0), (0, 0))).reshape(b, nb + 1, BLOCK, t.shape[2], t.shape[3])
    return jnp.concatenate([tp[:, :-1], tp[:, 1:]], axis=2)


def sliding_window_attention(q, k, v, sinks, rel_bias):
    b, s = q.shape[0], q.shape[1]
    nb = s // BLOCK
    qb = q.reshape(b, nb, BLOCK, N_KV_HEADS, GROUP, HEAD_DIM)
    kb = band_blocks(k, nb)
    vb = band_blocks(v, nb)
    scores = jnp.einsum('bnqhgd,bnkhd->bnhgqk', qb, kb,
                        preferred_element_type=jnp.float32)
    qi = jnp.arange(BLOCK, dtype=jnp.int32)[:, None]
    kj = jnp.arange(2 * BLOCK, dtype=jnp.int32)[None, :]
    dist = qi + BLOCK - kj
    bias = rel_bias[t5_causal_bucket(dist)].astype(jnp.float32)
    bias = jnp.transpose(bias, (2, 0, 1)).reshape(N_KV_HEADS, GROUP, BLOCK, 2 * BLOCK)
    scores = scores + bias
    key_pos = jnp.arange(nb, dtype=jnp.int32)[:, None] * BLOCK - BLOCK + kj
    valid = ((dist >= 0) & (dist < WINDOW))[None] & (key_pos >= 0)[:, None, :]
    scores = jnp.where(valid[None, :, None, None], scores, NEG)
    sink = sinks.astype(jnp.float32).reshape(N_KV_HEADS, GROUP)[None, None, :, :, None, None]
    sink = jnp.broadcast_to(sink, scores.shape[:-1] + (1,))
    probs = jax.nn.softmax(jnp.concatenate([scores, sink], axis=-1), axis=-1)[..., :-1]
    o = jnp.einsum('bnhgqk,bnkhd->bnqhgd', probs.astype(v.dtype), vb)
    return o.reshape(b, s, ATTN_WIDTH)


def conformer_conv(glu_in, w_dw, b_dw, ln_g, ln_b, w_conv_out):
    a, gate = jnp.split(glu_in, 2, axis=-1)
    h = a * jax.nn.sigmoid(gate)
    h = lax.conv_general_dilated(
        h, w_dw[:, None, :].astype(h.dtype), window_strides=(1,),
        padding=[(CONV_WIDTH - 1, 0)],
        dimension_numbers=('NWC', 'WIO', 'NWC'),
        feature_group_count=CONV_CH) + b_dw
    h = jax.nn.silu(layer_norm(h, ln_g, ln_b))
    return h @ w_conv_out


def _fwd_setup_inputs(seed: int = 0) -> dict:
    key = jax.random.key(seed)
    ks = jax.random.split(key, 20)
    f32 = jnp.float32

    def nrm(k, shape, scale):
        return jax.random.normal(k, shape, f32) * scale

    L = DEPTH
    return {
        "x": nrm(ks[0], (BATCH, SEQ, D_MODEL), 1.0),
        "norm_mix_g": 1.0 + nrm(ks[1], (L, D_MODEL), 0.02),
        "w_in": nrm(ks[2], (L, D_MODEL, IN_WIDTH), D_MODEL ** -0.5),
        "q_norm_g": 1.0 + nrm(ks[3], (L, HEAD_DIM), 0.02),
        "k_norm_g": 1.0 + nrm(ks[4], (L, HEAD_DIM), 0.02),
        "attn_sinks": nrm(ks[5], (L, N_Q_HEADS), 0.5),
        "rel_bias": nrm(ks[6], (N_BUCKETS, N_Q_HEADS), 0.5),
        "w_attn_o": nrm(ks[7], (L, ATTN_WIDTH, D_MODEL), ATTN_WIDTH ** -0.5),
        "w_dw": nrm(ks[8], (L, CONV_WIDTH, CONV_CH), CONV_WIDTH ** -0.5),
        "b_dw": nrm(ks[9], (L, CONV_CH), 0.02),
        "conv_ln_g": 1.0 + nrm(ks[10], (L, CONV_CH), 0.02),
        "conv_ln_b": nrm(ks[11], (L, CONV_CH), 0.02),
        "w_conv_out": nrm(ks[12], (L, CONV_CH, D_MODEL), CONV_CH ** -0.5),
        "w_out": nrm(ks[13], (L, D_MODEL, D_MODEL), D_MODEL ** -0.5),
        "norm_mlp_g": 1.0 + nrm(ks[14], (L, D_MODEL), 0.02),
        "w_ff1": nrm(ks[15], (L, D_MODEL, D_FF), D_MODEL ** -0.5),
        "w_ff2": nrm(ks[16], (L, D_FF, D_MODEL), D_FF ** -0.5),
    }


def _fwd_reference(x, norm_mix_g, w_in, q_norm_g, k_norm_g, attn_sinks, rel_bias, w_attn_o,
              w_dw, b_dw, conv_ln_g, conv_ln_b, w_conv_out, w_out, norm_mlp_g, w_ff1, w_ff2):
    b, s, _ = x.shape
    for l in range(DEPTH):
        u = rms_norm(x, norm_mix_g[l])
        proj = u @ w_in[l]
        q = proj[..., :Q_END].reshape(b, s, N_Q_HEADS, HEAD_DIM)
        k = proj[..., Q_END:K_END].reshape(b, s, N_KV_HEADS, HEAD_DIM)
        v = proj[..., K_END:V_END].reshape(b, s, N_KV_HEADS, HEAD_DIM)
        glu_in = proj[..., V_END:GLU_END]
        gate_attn, gate_conv = jnp.split(proj[..., GLU_END:], 2, axis=-1)

        q = rms_norm(q, q_norm_g[l]) * (HEAD_DIM ** -0.5)
        k = rms_norm(k, k_norm_g[l])
        attn = sliding_window_attention(q, k, v, attn_sinks[l], rel_bias) @ w_attn_o[l]
        conv = conformer_conv(glu_in, w_dw[l], b_dw[l], conv_ln_g[l], conv_ln_b[l], w_conv_out[l])

        merged = jax.nn.sigmoid(gate_attn) * attn + jax.nn.sigmoid(gate_conv) * conv
        x = x + merged @ w_out[l]

        hmid = jnp.square(jax.nn.relu(rms_norm(x, norm_mlp_g[l]) @ w_ff1[l]))
        x = x + hmid @ w_ff2[l]
    return x


import jax as _jax
import jax.numpy as _jnp

TWIN_FORMAT = 'train_step'
FWD_PARAMS = ['x', 'norm_mix_g', 'w_in', 'q_norm_g', 'k_norm_g', 'attn_sinks', 'rel_bias', 'w_attn_o', 'w_dw', 'b_dw', 'conv_ln_g', 'conv_ln_b', 'w_conv_out', 'w_out', 'norm_mlp_g', 'w_ff1', 'w_ff2']
TWIN_WEIGHTS = ['norm_mix_g', 'w_in', 'q_norm_g', 'k_norm_g', 'attn_sinks', 'rel_bias', 'w_attn_o', 'w_dw', 'b_dw', 'conv_ln_g', 'conv_ln_b', 'w_conv_out', 'w_out', 'norm_mlp_g', 'w_ff1', 'w_ff2']
TWIN_DIFF_INPUT = 'x'
TWIN_INPUTS = ['x', 'norm_mix_g', 'w_in', 'q_norm_g', 'k_norm_g', 'attn_sinks', 'rel_bias', 'w_attn_o', 'w_dw', 'b_dw', 'conv_ln_g', 'conv_ln_b', 'w_conv_out', 'w_out', 'norm_mlp_g', 'w_ff1', 'w_ff2', 'loss_target', 'm_norm_mix_g', 'm_w_in', 'm_q_norm_g', 'm_k_norm_g', 'm_attn_sinks', 'm_rel_bias', 'm_w_attn_o', 'm_w_dw', 'm_b_dw', 'm_conv_ln_g', 'm_conv_ln_b', 'm_w_conv_out', 'm_w_out', 'm_norm_mlp_g', 'm_w_ff1', 'm_w_ff2', 'v_norm_mix_g', 'v_w_in', 'v_q_norm_g', 'v_k_norm_g', 'v_attn_sinks', 'v_rel_bias', 'v_w_attn_o', 'v_w_dw', 'v_b_dw', 'v_conv_ln_g', 'v_conv_ln_b', 'v_w_conv_out', 'v_w_out', 'v_norm_mlp_g', 'v_w_ff1', 'v_w_ff2']
TWIN_OUTPUTS = ['loss', 'grad_x', 'grad_norm_mix_g', 'grad_w_in', 'grad_q_norm_g', 'grad_k_norm_g', 'grad_attn_sinks', 'grad_rel_bias', 'grad_w_attn_o', 'grad_w_dw', 'grad_b_dw', 'grad_conv_ln_g', 'grad_conv_ln_b', 'grad_w_conv_out', 'grad_w_out', 'grad_norm_mlp_g', 'grad_w_ff1', 'grad_w_ff2', 'delta_norm_mix_g', 'delta_w_in', 'delta_q_norm_g', 'delta_k_norm_g', 'delta_attn_sinks', 'delta_rel_bias', 'delta_w_attn_o', 'delta_w_dw', 'delta_b_dw', 'delta_conv_ln_g', 'delta_conv_ln_b', 'delta_w_conv_out', 'delta_w_out', 'delta_norm_mlp_g', 'delta_w_ff1', 'delta_w_ff2', 'new_m_norm_mix_g', 'new_m_w_in', 'new_m_q_norm_g', 'new_m_k_norm_g', 'new_m_attn_sinks', 'new_m_rel_bias', 'new_m_w_attn_o', 'new_m_w_dw', 'new_m_b_dw', 'new_m_conv_ln_g', 'new_m_conv_ln_b', 'new_m_w_conv_out', 'new_m_w_out', 'new_m_norm_mlp_g', 'new_m_w_ff1', 'new_m_w_ff2', 'new_v_norm_mix_g', 'new_v_w_in', 'new_v_q_norm_g', 'new_v_k_norm_g', 'new_v_attn_sinks', 'new_v_rel_bias', 'new_v_w_attn_o', 'new_v_w_dw', 'new_v_b_dw', 'new_v_conv_ln_g', 'new_v_conv_ln_b', 'new_v_w_conv_out', 'new_v_w_out', 'new_v_norm_mlp_g', 'new_v_w_ff1', 'new_v_w_ff2']
TWIN_LEAF_KINDS = {'loss': 'loss', 'grad_x': 'grad_x', 'grad_norm_mix_g': 'grad_w', 'grad_w_in': 'grad_w', 'grad_q_norm_g': 'grad_w', 'grad_k_norm_g': 'grad_w', 'grad_attn_sinks': 'grad_w', 'grad_rel_bias': 'grad_w', 'grad_w_attn_o': 'grad_w', 'grad_w_dw': 'grad_w', 'grad_b_dw': 'grad_w', 'grad_conv_ln_g': 'grad_w', 'grad_conv_ln_b': 'grad_w', 'grad_w_conv_out': 'grad_w', 'grad_w_out': 'grad_w', 'grad_norm_mlp_g': 'grad_w', 'grad_w_ff1': 'grad_w', 'grad_w_ff2': 'grad_w', 'delta_norm_mix_g': 'delta_w', 'delta_w_in': 'delta_w', 'delta_q_norm_g': 'delta_w', 'delta_k_norm_g': 'delta_w', 'delta_attn_sinks': 'delta_w', 'delta_rel_bias': 'delta_w', 'delta_w_attn_o': 'delta_w', 'delta_w_dw': 'delta_w', 'delta_b_dw': 'delta_w', 'delta_conv_ln_g': 'delta_w', 'delta_conv_ln_b': 'delta_w', 'delta_w_conv_out': 'delta_w', 'delta_w_out': 'delta_w', 'delta_norm_mlp_g': 'delta_w', 'delta_w_ff1': 'delta_w', 'delta_w_ff2': 'delta_w', 'new_m_norm_mix_g': 'new_m', 'new_m_w_in': 'new_m', 'new_m_q_norm_g': 'new_m', 'new_m_k_norm_g': 'new_m', 'new_m_attn_sinks': 'new_m', 'new_m_rel_bias': 'new_m', 'new_m_w_attn_o': 'new_m', 'new_m_w_dw': 'new_m', 'new_m_b_dw': 'new_m', 'new_m_conv_ln_g': 'new_m', 'new_m_conv_ln_b': 'new_m', 'new_m_w_conv_out': 'new_m', 'new_m_w_out': 'new_m', 'new_m_norm_mlp_g': 'new_m', 'new_m_w_ff1': 'new_m', 'new_m_w_ff2': 'new_m', 'new_v_norm_mix_g': 'new_v', 'new_v_w_in': 'new_v', 'new_v_q_norm_g': 'new_v', 'new_v_k_norm_g': 'new_v', 'new_v_attn_sinks': 'new_v', 'new_v_rel_bias': 'new_v', 'new_v_w_attn_o': 'new_v', 'new_v_w_dw': 'new_v', 'new_v_b_dw': 'new_v', 'new_v_conv_ln_g': 'new_v', 'new_v_conv_ln_b': 'new_v', 'new_v_w_conv_out': 'new_v', 'new_v_w_out': 'new_v', 'new_v_norm_mlp_g': 'new_v', 'new_v_w_ff1': 'new_v', 'new_v_w_ff2': 'new_v'}


def _forward(args):
    return _fwd_reference(*[args[k] for k in FWD_PARAMS])


def _output_shape():
    out = _jax.eval_shape(lambda: _forward(_fwd_setup_inputs(0)))
    return out.shape, out.dtype

N_MICROBATCH = 1
ADAM_LR = 0.001
ADAM_B1 = 0.9
ADAM_B2 = 0.999
ADAM_EPS = 1e-08
ADAM_WD = 0.01
ADAM_STEP = 10
PER_EXAMPLE_BATCH_AXIS = {'x': 0, 'loss_target': 0}
SHARED_INPUTS = []
_WEIGHT_DTYPES = {'norm_mix_g': _jnp.float32, 'w_in': _jnp.float32, 'q_norm_g': _jnp.float32, 'k_norm_g': _jnp.float32, 'attn_sinks': _jnp.float32, 'rel_bias': _jnp.float32, 'w_attn_o': _jnp.float32, 'w_dw': _jnp.float32, 'b_dw': _jnp.float32, 'conv_ln_g': _jnp.float32, 'conv_ln_b': _jnp.float32, 'w_conv_out': _jnp.float32, 'w_out': _jnp.float32, 'norm_mlp_g': _jnp.float32, 'w_ff1': _jnp.float32, 'w_ff2': _jnp.float32}
MOMENT_SCALE = {'norm_mix_g': 7.721714e-01, 'w_in': 2.088166e-01, 'q_norm_g': 3.048470e+00, 'k_norm_g': 3.035640e+00, 'attn_sinks': 3.492187e-01, 'rel_bias': 2.027210e-01, 'w_attn_o': 2.171116e-01, 'w_dw': 1.360934e+00, 'b_dw': 2.842538e+01, 'conv_ln_g': 1.464716e+01, 'conv_ln_b': 1.798026e+01, 'w_conv_out': 6.637231e+00, 'w_out': 6.618209e+00, 'norm_mlp_g': 1.916451e+02, 'w_ff1': 2.461718e+00, 'w_ff2': 1.670132e+01}


def _to_microbatches(a, axis):
    t = _jnp.moveaxis(a, axis, 0)
    t = t.reshape((N_MICROBATCH, t.shape[0] // N_MICROBATCH) + t.shape[1:])
    return _jnp.moveaxis(t, 1, axis + 1)


def setup_inputs(seed: int = 0) -> dict:
    inp = _fwd_setup_inputs(seed)
    key = _jax.random.fold_in(_jax.random.key(seed), 7919)
    shape, _ = _output_shape()
    out = dict(inp)
    out["loss_target"] = _jax.random.normal(_jax.random.fold_in(key, 0), shape, _jnp.float32)
    for i, name in enumerate(TWIN_WEIGHTS):
        w = inp[name].astype(_jnp.float32)
        if MOMENT_SCALE is None:
            s = _jnp.sqrt(_jnp.mean(_jnp.square(w)) + 1e-30)
        else:
            s = MOMENT_SCALE[name]
        km, kv = _jax.random.split(_jax.random.fold_in(key, i + 1))
        out[name] = w
        out["m_" + name] = s * _jax.random.normal(km, w.shape, _jnp.float32)
        out["v_" + name] = (s * s) * _jax.random.uniform(kv, w.shape, _jnp.float32, 0.5, 1.5)
    if N_MICROBATCH > 1:
        for name, axis in PER_EXAMPLE_BATCH_AXIS.items():
            out[name] = _to_microbatches(out[name], axis)
    return {'x': out['x'], 'norm_mix_g': out['norm_mix_g'], 'w_in': out['w_in'], 'q_norm_g': out['q_norm_g'], 'k_norm_g': out['k_norm_g'], 'attn_sinks': out['attn_sinks'], 'rel_bias': out['rel_bias'], 'w_attn_o': out['w_attn_o'], 'w_dw': out['w_dw'], 'b_dw': out['b_dw'], 'conv_ln_g': out['conv_ln_g'], 'conv_ln_b': out['conv_ln_b'], 'w_conv_out': out['w_conv_out'], 'w_out': out['w_out'], 'norm_mlp_g': out['norm_mlp_g'], 'w_ff1': out['w_ff1'], 'w_ff2': out['w_ff2'], 'loss_target': out['loss_target'], 'm_norm_mix_g': out['m_norm_mix_g'], 'm_w_in': out['m_w_in'], 'm_q_norm_g': out['m_q_norm_g'], 'm_k_norm_g': out['m_k_norm_g'], 'm_attn_sinks': out['m_attn_sinks'], 'm_rel_bias': out['m_rel_bias'], 'm_w_attn_o': out['m_w_attn_o'], 'm_w_dw': out['m_w_dw'], 'm_b_dw': out['m_b_dw'], 'm_conv_ln_g': out['m_conv_ln_g'], 'm_conv_ln_b': out['m_conv_ln_b'], 'm_w_conv_out': out['m_w_conv_out'], 'm_w_out': out['m_w_out'], 'm_norm_mlp_g': out['m_norm_mlp_g'], 'm_w_ff1': out['m_w_ff1'], 'm_w_ff2': out['m_w_ff2'], 'v_norm_mix_g': out['v_norm_mix_g'], 'v_w_in': out['v_w_in'], 'v_q_norm_g': out['v_q_norm_g'], 'v_k_norm_g': out['v_k_norm_g'], 'v_attn_sinks': out['v_attn_sinks'], 'v_rel_bias': out['v_rel_bias'], 'v_w_attn_o': out['v_w_attn_o'], 'v_w_dw': out['v_w_dw'], 'v_b_dw': out['v_b_dw'], 'v_conv_ln_g': out['v_conv_ln_g'], 'v_conv_ln_b': out['v_conv_ln_b'], 'v_w_conv_out': out['v_w_conv_out'], 'v_w_out': out['v_w_out'], 'v_norm_mlp_g': out['v_norm_mlp_g'], 'v_w_ff1': out['v_w_ff1'], 'v_w_ff2': out['v_w_ff2']}


def _loss(weights, diff, rest, loss_target):
    with _jax.named_scope("forward"):
        args = {**rest, TWIN_DIFF_INPUT: diff, **{k: w.astype(_WEIGHT_DTYPES[k]) for k, w in weights.items()}}
        y = _forward(args)
    with _jax.named_scope("loss_head"):
        err = _jnp.square(y.astype(_jnp.float32) - loss_target)
        return 0.5 * _jnp.sum(_jnp.mean(err, axis=-1)) if err.ndim else 0.5 * err


def _adamw(w, g, m, v):
    m = ADAM_B1 * m + (1.0 - ADAM_B1) * g
    v = ADAM_B2 * v + (1.0 - ADAM_B2) * _jnp.square(g)
    m_hat = m / (1.0 - ADAM_B1 ** ADAM_STEP)
    v_hat = v / (1.0 - ADAM_B2 ** ADAM_STEP)
    delta = -ADAM_LR * (m_hat / (_jnp.sqrt(v_hat) + ADAM_EPS) + ADAM_WD * w)
    return delta, m, v


def reference(x, norm_mix_g, w_in, q_norm_g, k_norm_g, attn_sinks, rel_bias, w_attn_o, w_dw, b_dw, conv_ln_g, conv_ln_b, w_conv_out, w_out, norm_mlp_g, w_ff1, w_ff2, loss_target, m_norm_mix_g, m_w_in, m_q_norm_g, m_k_norm_g, m_attn_sinks, m_rel_bias, m_w_attn_o, m_w_dw, m_b_dw, m_conv_ln_g, m_conv_ln_b, m_w_conv_out, m_w_out, m_norm_mlp_g, m_w_ff1, m_w_ff2, v_norm_mix_g, v_w_in, v_q_norm_g, v_k_norm_g, v_attn_sinks, v_rel_bias, v_w_attn_o, v_w_dw, v_b_dw, v_conv_ln_g, v_conv_ln_b, v_w_conv_out, v_w_out, v_norm_mlp_g, v_w_ff1, v_w_ff2):
    given = dict(x=x, norm_mix_g=norm_mix_g, w_in=w_in, q_norm_g=q_norm_g, k_norm_g=k_norm_g, attn_sinks=attn_sinks, rel_bias=rel_bias, w_attn_o=w_attn_o, w_dw=w_dw, b_dw=b_dw, conv_ln_g=conv_ln_g, conv_ln_b=conv_ln_b, w_conv_out=w_conv_out, w_out=w_out, norm_mlp_g=norm_mlp_g, w_ff1=w_ff1, w_ff2=w_ff2, loss_target=loss_target, m_norm_mix_g=m_norm_mix_g, m_w_in=m_w_in, m_q_norm_g=m_q_norm_g, m_k_norm_g=m_k_norm_g, m_attn_sinks=m_attn_sinks, m_rel_bias=m_rel_bias, m_w_attn_o=m_w_attn_o, m_w_dw=m_w_dw, m_b_dw=m_b_dw, m_conv_ln_g=m_conv_ln_g, m_conv_ln_b=m_conv_ln_b, m_w_conv_out=m_w_conv_out, m_w_out=m_w_out, m_norm_mlp_g=m_norm_mlp_g, m_w_ff1=m_w_ff1, m_w_ff2=m_w_ff2, v_norm_mix_g=v_norm_mix_g, v_w_in=v_w_in, v_q_norm_g=v_q_norm_g, v_k_norm_g=v_k_norm_g, v_attn_sinks=v_attn_sinks, v_rel_bias=v_rel_bias, v_w_attn_o=v_w_attn_o, v_w_dw=v_w_dw, v_b_dw=v_b_dw, v_conv_ln_g=v_conv_ln_g, v_conv_ln_b=v_conv_ln_b, v_w_conv_out=v_w_conv_out, v_w_out=v_w_out, v_norm_mlp_g=v_norm_mlp_g, v_w_ff1=v_w_ff1, v_w_ff2=v_w_ff2)
    weights = {n: given[n] for n in TWIN_WEIGHTS}
    shared = {n: given[n] for n in SHARED_INPUTS}
    per_example = {n: given[n] for n in ['x']}
    grad_fn = _jax.value_and_grad(_loss, argnums=(0, 1))

    def one_microbatch(ex, loss_target):
        ex = dict(ex)
        diff = ex.pop(TWIN_DIFF_INPUT)
        return grad_fn(weights, diff, {**shared, **ex}, loss_target)

    if N_MICROBATCH == 1:
        loss, (grad_w, grad_x) = one_microbatch(per_example, given["loss_target"])
    else:
        def body(carry, xs):
            loss_sum, grad_sum = carry
            l_k, (gw_k, gx_k) = one_microbatch(xs[0], xs[1])
            with _jax.named_scope("update"):
                return (loss_sum + l_k, _jax.tree.map(_jnp.add, grad_sum, gw_k)), gx_k

        init = (_jnp.zeros((), _jnp.float32), _jax.tree.map(_jnp.zeros_like, weights))
        (loss, grad_w), grad_x = _jax.lax.scan(body, init, (per_example, given["loss_target"]))
    with _jax.named_scope("update"):
        delta_w, new_m, new_v = {}, {}, {}
        for n in TWIN_WEIGHTS:
            delta_w[n], new_m[n], new_v[n] = _adamw(weights[n], grad_w[n], given["m_" + n], given["v_" + n])
    return (loss, grad_x, *[grad_w[n] for n in TWIN_WEIGHTS], *[delta_w[n] for n in TWIN_WEIGHTS],
            *[new_m[n] for n in TWIN_WEIGHTS], *[new_v[n] for n in TWIN_WEIGHTS])
```

```python
import functools

import numpy as np
import jax
import jax.numpy as jnp
from jax import lax
from jax.experimental import pallas as pl
from jax.experimental.pallas import tpu as pltpu

f32 = jnp.float32
bf16 = jnp.bfloat16
SDS = jax.ShapeDtypeStruct
MESH = pl.DeviceIdType.MESH

D = 1024
HD = 64
NQ = 16
NKV = 4
BLK = 128
CW = 31
HALO = 32
DFF = 4096
NBUCKET = 32
EPS = 1e-6
NEG = -1e30
INW = 5632
C_Q, C_A, C_G, C_GA, C_GC = 0, 1, 2, 3, 4
C_KV = 10

ADAM_LR = 0.001
ADAM_B1 = 0.9
ADAM_B2 = 0.999
ADAM_EPS = 1e-08
ADAM_WD = 0.01
ADAM_STEP = 10

VMEM_BYTES_V7X = 64 << 20


def _cparams(sem, vmem_mb):
    assert (vmem_mb << 20) < VMEM_BYTES_V7X
    return pltpu.CompilerParams(dimension_semantics=sem, vmem_limit_bytes=vmem_mb << 20)


def _dot(a, b):
    return jnp.dot(a, b, preferred_element_type=f32)


def _dot_nt(a, b):
    return lax.dot_general(a, b, (((1,), (1,)), ((), ())), preferred_element_type=f32)


def _dot_tn(a, b):
    return lax.dot_general(a, b, (((0,), (0,)), ((), ())), preferred_element_type=f32)


def _sigmoid(x):
    return 1.0 / (1.0 + jnp.exp(-x))


def _low_head_lanes():
    return lax.broadcasted_iota(jnp.int32, (1, 2 * HD), 1) < HD


def _head_blockdiag():
    r = lax.broadcasted_iota(jnp.int32, (2 * HD, 2 * HD), 0) // HD
    c = lax.broadcasted_iota(jnp.int32, (2 * HD, 2 * HD), 1) // HD
    return jnp.where(r == c, 1.0, 0.0).astype(bf16)


def _head_sums(z, bd):
    hi = z.astype(bf16)
    lo = (z - hi.astype(f32)).astype(bf16)
    return _dot(hi, bd) + _dot(lo, bd)


def _rms_inproj(x, g, w):
    T, N = x.shape[0], w.shape[1]
    tm, tn = 512, 512

    def body(x_ref, g_ref, w_ref, p_ref, u_ref):
        @pl.when(pl.program_id(1) == 0)
        def _():
            xv = x_ref[...]
            r = lax.rsqrt(jnp.mean(xv * xv, axis=-1, keepdims=True) + EPS)
            u_ref[...] = (xv * r * g_ref[...]).astype(bf16)
        p_ref[...] = _dot(u_ref[...], w_ref[...])

    return pl.pallas_call(
        body, grid=(T // tm, N // tn),
        in_specs=[pl.BlockSpec((tm, D), lambda i, j: (i, 0)),
                  pl.BlockSpec((1, D), lambda i, j: (0, 0)),
                  pl.BlockSpec((D, tn), lambda i, j: (0, j))],
        out_specs=[pl.BlockSpec((tm, tn), lambda i, j: (i, j)),
                   pl.BlockSpec((tm, D), lambda i, j: (i, 0))],
        out_shape=[SDS((T, N), f32), SDS((T, D), bf16)],
        name="rms_inproj", compiler_params=_cparams(("parallel", "arbitrary"), 32))(x, g, w)


def _split_pair(pair, out_ref, p, lo):
    rolled = pltpu.roll(pair, HD, 1)
    zero = jnp.zeros_like(pair)
    c = 512 * p
    out_ref[:, c:c + 128] = jnp.where(lo, pair, zero).astype(bf16)
    out_ref[:, c + 128:c + 256] = jnp.where(lo, zero, rolled).astype(bf16)
    out_ref[:, c + 256:c + 384] = jnp.where(lo, rolled, zero).astype(bf16)
    out_ref[:, c + 384:c + 512] = jnp.where(lo, zero, pair).astype(bf16)


def _qk_prep(proj, gq2, gk2):
    T = proj.shape[0]
    tm = 512

    def body(q_ref, kv_ref, gq_ref, gk_ref, qn_ref, kk_ref, vv_ref):
        bd = _head_blockdiag()
        lo = _low_head_lanes()
        for p in range(NQ // 2):
            z = q_ref[:, 128 * p:128 * p + 128]
            r = lax.rsqrt(_head_sums(z * z, bd) * (1.0 / HD) + EPS)
            qn_ref[:, 128 * p:128 * p + 128] = (z * r * gq_ref[...] * (HD ** -0.5)).astype(bf16)
        for p in range(NKV // 2):
            z = kv_ref[:, 128 * p:128 * p + 128]
            r = lax.rsqrt(_head_sums(z * z, bd) * (1.0 / HD) + EPS)
            _split_pair(z * r * gk_ref[...], kk_ref, p, lo)
            _split_pair(kv_ref[:, 256 + 128 * p:256 + 128 * p + 128], vv_ref, p, lo)

    return pl.pallas_call(
        body, grid=(T // tm,),
        in_specs=[pl.BlockSpec((tm, D), lambda i: (i, C_Q)),
                  pl.BlockSpec((tm, 512), lambda i: (i, C_KV)),
                  pl.BlockSpec((1, 128), lambda i: (0, 0)),
                  pl.BlockSpec((1, 128), lambda i: (0, 0))],
        out_specs=[pl.BlockSpec((tm, D), lambda i: (i, 0))] * 3,
        out_shape=[SDS((T, D), bf16)] * 3,
        name="qk_prep", compiler_params=_cparams(("parallel",), 32))(proj, proj, gq2, gk2)


def _bucket_tile():
    qi = np.arange(BLK)[:, None]
    kj = np.arange(2 * BLK)[None, :]
    dist = qi + BLK - kj
    n = np.maximum(dist, 0)
    max_exact = NBUCKET // 2
    nf = np.maximum(n, 1).astype(np.float32)
    large = max_exact + (np.log(nf / max_exact) / np.float32(np.log(128 / max_exact))
                         * (NBUCKET - max_exact)).astype(np.int32)
    large = np.minimum(large, NBUCKET - 1)
    bucket = np.where(n < max_exact, n, large)
    valid = (dist >= 0) & (dist < BLK)
    return np.where(valid, bucket, -1).astype(np.int32)


def _bias_tiles(rel_bias):
    def body(rb_ref, bk_ref, out_ref):
        bk = bk_ref[...]
        for h in range(NQ):
            acc = jnp.full((BLK, 2 * BLK), NEG, f32)
            for b in range(NBUCKET):
                acc = jnp.where(bk == b, rb_ref[b, h], acc)
            out_ref[h] = acc

    return pl.pallas_call(
        body,
        in_specs=[pl.BlockSpec(memory_space=pltpu.SMEM), pl.BlockSpec(memory_space=pltpu.VMEM)],
        out_specs=pl.BlockSpec(memory_space=pltpu.VMEM),
        out_shape=SDS((NQ, BLK, 2 * BLK), f32),
        name="bias_tiles")(rel_bias, jnp.asarray(_bucket_tile()))


def _rows2(ref, c):
    return jnp.concatenate([ref[:, c:c + 128], ref[:, c + 128:c + 256]], axis=0)


def _attn_fwd(qn, kk, vv, bias, sinks):
    T = qn.shape[0]
    nb = T // BLK

    def body(s_ref, q_ref, kc_ref, kp_ref, vc_ref, vp_ref, b_ref, o_ref, lse_ref):
        has_prev = pl.program_id(0) > 0
        for h in range(NKV):
            c = 256 * h
            qs = _rows2(q_ref, c)
            sc = _dot_nt(qs, _rows2(kc_ref, c))
            sp = _dot_nt(qs, _rows2(kp_ref, c))
            vstack = jnp.concatenate([vp_ref[:, c:c + 128], vc_ref[:, c:c + 128],
                                      vp_ref[:, c + 128:c + 256], vc_ref[:, c + 128:c + 256]], axis=0)
            for pr in range(2):
                ps = []
                for e in range(2):
                    hq = 4 * h + 2 * pr + e
                    rows, cols = slice(128 * pr, 128 * pr + 128), slice(128 * e, 128 * e + 128)
                    s_p = jnp.where(has_prev, sp[rows, cols] + b_ref[hq, :, 0:BLK], NEG)
                    s_c = sc[rows, cols] + b_ref[hq, :, BLK:2 * BLK]
                    sink = s_ref[0, hq]
                    m = jnp.maximum(jnp.maximum(jnp.max(s_p, axis=-1, keepdims=True),
                                                jnp.max(s_c, axis=-1, keepdims=True)), sink)
                    e_p = jnp.exp(s_p - m)
                    e_c = jnp.exp(s_c - m)
                    l = (jnp.sum(e_p, axis=-1, keepdims=True) + jnp.sum(e_c, axis=-1, keepdims=True)
                         + jnp.exp(sink - m))
                    inv = 1.0 / l
                    ps += [(e_p * inv).astype(bf16), (e_c * inv).astype(bf16)]
                    lse_ref[:, hq:hq + 1] = m + jnp.log(l)
                o_ref[:, c + 128 * pr:c + 128 * pr + 128] = _dot(jnp.concatenate(ps, axis=1), vstack).astype(bf16)

    blk = lambda f: pl.BlockSpec((BLK, D), f)
    cur = lambda n: (n, 0)
    prev = lambda n: (jnp.maximum(n - 1, 0), 0)
    return pl.pallas_call(
        body, grid=(nb,),
        in_specs=[pl.BlockSpec(memory_space=pltpu.SMEM), blk(cur), blk(cur), blk(prev), blk(cur), blk(prev),
                  pl.BlockSpec((NQ, BLK, 2 * BLK), lambda n: (0, 0, 0))],
        out_specs=[blk(cur), pl.BlockSpec((BLK, NQ), cur)],
        out_shape=[SDS((T, D), bf16), SDS((T, NQ), f32)],
        name="attn_fwd", compiler_params=_cparams(("parallel",), 32))(sinks, qn, kk, kk, vv, vv, bias)


def _conv_taps(stage_ref, w_ref, rows, offset_of_tap, init):
    halves = []
    for hf in range(2):
        ln = slice(512 * hf, 512 * hf + 512)
        acc = init(ln)
        for j in range(CW):
            acc = acc + stage_ref[pl.ds(offset_of_tap(j), rows), ln] * w_ref[j:j + 1, ln]
        halves.append(acc)
    return halves


def _glu_conv_fwd(proj, w_dw, b_dw, ln_g, ln_b):
    T = proj.shape[0]
    tt, ch = 256, 32

    def body(a_ref, g_ref, ah_ref, gh_ref, w_ref, b_ref, lg_ref, lb_ref, h1_ref, h3_ref, slab, stage):
        i = pl.program_id(0)
        halo = ah_ref[...] * _sigmoid(gh_ref[...])
        slab[0:HALO, :] = jnp.where(i > 0, halo, 0.0)
        slab[HALO:HALO + tt, :] = a_ref[...] * _sigmoid(g_ref[...])

        def chunk(c, carry):
            r0 = pl.multiple_of(c * ch, ch)
            stage[...] = slab[pl.ds(r0, 2 * ch), :]
            lo, hi = _conv_taps(stage, w_ref, ch, lambda j: HALO - (CW - 1) + j,
                                lambda ln: jnp.broadcast_to(b_ref[:, ln], (ch, 512)))
            h1_ref[pl.ds(r0, ch), 0:512] = lo
            h1_ref[pl.ds(r0, ch), 512:1024] = hi
            return carry

        lax.fori_loop(0, tt // ch, chunk, 0)
        h1 = h1_ref[...]
        mu = jnp.mean(h1, axis=-1, keepdims=True)
        xc = h1 - mu
        var = jnp.mean(xc * xc, axis=-1, keepdims=True)
        h2 = xc * lax.rsqrt(var + EPS) * lg_ref[...] + lb_ref[...]
        h3_ref[...] = (h2 * _sigmoid(h2)).astype(bf16)

    hpt = tt // HALO
    tile = lambda cb: pl.BlockSpec((tt, D), lambda i: (i, cb))
    halo = lambda cb: pl.BlockSpec((HALO, D), lambda i: (jnp.maximum(i * hpt - 1, 0), cb))
    vec = pl.BlockSpec((1, D), lambda i: (0, 0))
    return pl.pallas_call(
        body, grid=(T // tt,),
        in_specs=[tile(C_A), tile(C_G), halo(C_A), halo(C_G), pl.BlockSpec((HALO, D), lambda i: (0, 0)), vec, vec, vec],
        out_specs=[pl.BlockSpec((tt, D), lambda i: (i, 0))] * 2,
        out_shape=[SDS((T, D), f32), SDS((T, D), bf16)],
        scratch_shapes=[pltpu.VMEM((HALO + tt, D), f32), pltpu.VMEM((2 * ch, D), f32)],
        name="glu_conv_fwd", compiler_params=_cparams(("parallel",), 32))(proj, proj, proj, proj, w_dw, b_dw, ln_g, ln_b)


def _mix_out(o, h3, proj, x, w_attn_o, w_conv_out, w_out, g_mlp):
    T = x.shape[0]
    tm = 512

    def body(o_ref, h3_ref, ga_ref, gc_ref, x_ref, wa_ref, wc_ref, wo_ref, g_ref,
             attn_ref, conv_ref, mg_ref, x1_ref, n2_ref):
        attn = _dot(o_ref[...], wa_ref[...])
        conv = _dot(h3_ref[...], wc_ref[...])
        attn_ref[...] = attn
        conv_ref[...] = conv
        mg = (_sigmoid(ga_ref[...]) * attn + _sigmoid(gc_ref[...]) * conv).astype(bf16)
        mg_ref[...] = mg
        x1 = x_ref[...] + _dot(mg, wo_ref[...])
        x1_ref[...] = x1
        r = lax.rsqrt(jnp.mean(x1 * x1, axis=-1, keepdims=True) + EPS)
        n2_ref[...] = (x1 * r * g_ref[...]).astype(bf16)

    tile = lambda cb=0: pl.BlockSpec((tm, D), lambda i: (i, cb))
    wfull = pl.BlockSpec((D, D), lambda i: (0, 0))
    return pl.pallas_call(
        body, grid=(T // tm,),
        in_specs=[tile(), tile(), tile(C_GA), tile(C_GC), tile(), wfull, wfull, wfull,
                  pl.BlockSpec((1, D), lambda i: (0, 0))],
        out_specs=[tile()] * 5,
        out_shape=[SDS((T, D), f32), SDS((T, D), f32), SDS((T, D), bf16), SDS((T, D), f32), SDS((T, D), bf16)],
        name="mix_out", compiler_params=_cparams(("parallel",), 48))(o, h3, proj, proj, x, w_attn_o, w_conv_out, w_out, g_mlp)


def _mlp_fwd(n2, w1, w2, x1, tgt):
    T = n2.shape[0]
    tm, tf = 512, 1024
    nk = DFF // tf

    def body(n2_ref, w1_ref, w2_ref, x1_ref, t_ref, hm_ref, dy_ref, dyb_ref, loss_ref, acc):
        i, k = pl.program_id(0), pl.program_id(1)

        @pl.when((i == 0) & (k == 0))
        def _():
            loss_ref[...] = jnp.zeros_like(loss_ref)

        r = jnp.maximum(_dot(n2_ref[...], w1_ref[...]), 0.0)
        hm = (r * r).astype(bf16)
        hm_ref[...] = hm
        part = _dot(hm, w2_ref[...])

        @pl.when(k == 0)
        def _():
            acc[...] = part

        @pl.when(k > 0)
        def _():
            acc[...] += part

        @pl.when(k == nk - 1)
        def _():
            e = x1_ref[...] + acc[...] - t_ref[...]
            dy = e * (1.0 / D)
            dy_ref[...] = dy
            dyb_ref[...] = dy.astype(bf16)
            loss_ref[...] += 0.5 * jnp.sum(jnp.sum(e * e, axis=-1, keepdims=True) * (1.0 / D))

    row = pl.BlockSpec((tm, D), lambda i, k: (i, 0))
    return pl.pallas_call(
        body, grid=(T // tm, nk),
        in_specs=[row, pl.BlockSpec((D, tf), lambda i, k: (0, k)), pl.BlockSpec((tf, D), lambda i, k: (k, 0)), row, row],
        out_specs=[pl.BlockSpec((tm, tf), lambda i, k: (i, k)), row, row, pl.BlockSpec((8, 128), lambda i, k: (0, 0))],
        out_shape=[SDS((T, DFF), bf16), SDS((T, D), f32), SDS((T, D), bf16), SDS((8, 128), f32)],
        scratch_shapes=[pltpu.VMEM((tm, D), f32)],
        name="mlp_fwd", compiler_params=_cparams(("arbitrary", "arbitrary"), 48))(n2, w1, w2, x1, tgt)


def _rms_bwd(xv, g, dn, dres):
    r = lax.rsqrt(jnp.mean(xv * xv, axis=-1, keepdims=True) + EPS)
    gd = dn * g
    dx = dres + r * gd - xv * (r * r * r) * jnp.mean(xv * gd, axis=-1, keepdims=True)
    dg = jnp.sum(dn * xv * r, axis=0, keepdims=True)
    return dx, dg


def _mlp_bwd(dy, dyb, hmid, w1, w2, x1, g_mlp):
    T = dy.shape[0]
    tm, tf = 512, 1024
    nk = DFF // tf

    def body(dy_ref, dyb_ref, hm_ref, w1_ref, w2_ref, x1_ref, g_ref, df_ref, dx_ref, dxb_ref, dg_ref, acc):
        i, k = pl.program_id(0), pl.program_id(1)

        @pl.when((i == 0) & (k == 0))
        def _():
            dg_ref[...] = jnp.zeros_like(dg_ref)

        d_hm = _dot_nt(dyb_ref[...], w2_ref[...])
        df = (d_hm * (2.0 * jnp.sqrt(hm_ref[...].astype(f32)))).astype(bf16)
        df_ref[...] = df
        part = _dot_nt(df, w1_ref[...])

        @pl.when(k == 0)
        def _():
            acc[...] = part

        @pl.when(k > 0)
        def _():
            acc[...] += part

        @pl.when(k == nk - 1)
        def _():
            dx, dg = _rms_bwd(x1_ref[...], g_ref[...], acc[...], dy_ref[...])
            dx_ref[...] = dx
            dxb_ref[...] = dx.astype(bf16)
            dg_ref[...] += dg

    row = pl.BlockSpec((tm, D), lambda i, k: (i, 0))
    vec = pl.BlockSpec((1, D), lambda i, k: (0, 0))
    return pl.pallas_call(
        body, grid=(T // tm, nk),
        in_specs=[row, row, pl.BlockSpec((tm, tf), lambda i, k: (i, k)), pl.BlockSpec((D, tf), lambda i, k: (0, k)),
                  pl.BlockSpec((tf, D), lambda i, k: (k, 0)), row, vec],
        out_specs=[pl.BlockSpec((tm, tf), lambda i, k: (i, k)), row, row, vec],
        out_shape=[SDS((T, DFF), bf16), SDS((T, D), f32), SDS((T, D), bf16), SDS((1, D), f32)],
        scratch_shapes=[pltpu.VMEM((tm, D), f32)],
        name="mlp_bwd", compiler_params=_cparams(("arbitrary", "arbitrary"), 48))(dy, dyb, hmid, w1, w2, x1, g_mlp)


def _wgrad(a, b, name, tn=1024):
    T, M = a.shape
    N = b.shape[1]
    tmm, tk = min(M, 1024), 512

    def body(a_ref, b_ref, o_ref):
        part = _dot_tn(a_ref[...], b_ref[...])

        @pl.when(pl.program_id(2) == 0)
        def _():
            o_ref[...] = part

        @pl.when(pl.program_id(2) > 0)
        def _():
            o_ref[...] += part

    return pl.pallas_call(
        body, grid=(M // tmm, N // tn, T // tk),
        in_specs=[pl.BlockSpec((tk, tmm), lambda m, n, t: (t, m)), pl.BlockSpec((tk, tn), lambda m, n, t: (t, n))],
        out_specs=pl.BlockSpec((tmm, tn), lambda m, n, t: (m, n)),
        out_shape=SDS((M, N), f32),
        name=name, compiler_params=_cparams(("parallel", "parallel", "arbitrary"), 40))(a, b)


def _mix_bwd(dx1b, proj, attn, conv, w_attn_o, w_conv_out, w_out):
    T = dx1b.shape[0]
    tm = 512

    def body(dx_ref, ga_ref, gc_ref, attn_ref, conv_ref, wa_ref, wc_ref, wo_ref,
             dat_ref, dcv_ref, do_ref, dh3_ref, dga_ref, dgc_ref):
        dm = _dot_nt(dx_ref[...], wo_ref[...])
        sa = _sigmoid(ga_ref[...])
        sc = _sigmoid(gc_ref[...])
        dat = (dm * sa).astype(bf16)
        dcv = (dm * sc).astype(bf16)
        dat_ref[...] = dat
        dcv_ref[...] = dcv
        dga_ref[...] = (dm * attn_ref[...] * sa * (1.0 - sa)).astype(bf16)
        dgc_ref[...] = (dm * conv_ref[...] * sc * (1.0 - sc)).astype(bf16)
        do_ref[...] = _dot_nt(dat, wa_ref[...]).astype(bf16)
        dh3_ref[...] = _dot_nt(dcv, wc_ref[...])

    tile = lambda cb=0: pl.BlockSpec((tm, D), lambda i: (i, cb))
    wfull = pl.BlockSpec((D, D), lambda i: (0, 0))
    return pl.pallas_call(
        body, grid=(T // tm,),
        in_specs=[tile(), tile(C_GA), tile(C_GC), tile(), tile(), wfull, wfull, wfull],
        out_specs=[tile()] * 6,
        out_shape=[SDS((T, D), bf16), SDS((T, D), bf16), SDS((T, D), bf16), SDS((T, D), f32),
                   SDS((T, D), bf16), SDS((T, D), bf16)],
        name="mix_bwd", compiler_params=_cparams(("parallel",), 48))(dx1b, proj, proj, attn, conv, w_attn_o, w_conv_out, w_out)


def _conv_ln_bwd(dh3, h1, ln_g, ln_b):
    T = dh3.shape[0]
    tt = 256

    def body(d_ref, h1_ref, lg_ref, lb_ref, dh1_ref, acc_ref):
        @pl.when(pl.program_id(0) == 0)
        def _():
            acc_ref[...] = jnp.zeros_like(acc_ref)

        h1 = h1_ref[...]
        mu = jnp.mean(h1, axis=-1, keepdims=True)
        xc = h1 - mu
        rstd = lax.rsqrt(jnp.mean(xc * xc, axis=-1, keepdims=True) + EPS)
        xh = xc * rstd
        h2 = xh * lg_ref[...] + lb_ref[...]
        sg = _sigmoid(h2)
        dh2 = d_ref[...] * (sg * (1.0 + h2 * (1.0 - sg)))
        dxh = dh2 * lg_ref[...]
        dh1 = rstd * (dxh - jnp.mean(dxh, axis=-1, keepdims=True) - xh * jnp.mean(dxh * xh, axis=-1, keepdims=True))
        dh1_ref[...] = dh1
        acc_ref[0:1, :] += jnp.sum(dh2 * xh, axis=0, keepdims=True)
        acc_ref[1:2, :] += jnp.sum(dh2, axis=0, keepdims=True)
        acc_ref[2:3, :] += jnp.sum(dh1, axis=0, keepdims=True)

    tile = pl.BlockSpec((tt, D), lambda i: (i, 0))
    vec = pl.BlockSpec((1, D), lambda i: (0, 0))
    return pl.pallas_call(
        body, grid=(T // tt,),
        in_specs=[tile, tile, vec, vec],
        out_specs=[tile, pl.BlockSpec((8, D), lambda i: (0, 0))],
        out_shape=[SDS((T, D), f32), SDS((8, D), f32)],
        name="conv_ln_bwd", compiler_params=_cparams(("arbitrary",), 32))(dh3, h1, ln_g, ln_b)


def _conv_bwd(dh1, proj, w_dw):
    T = dh1.shape[0]
    tt, ch = 256, 32
    nt = T // tt

    def body(d_ref, dn_ref, a_ref, g_ref, ah_ref, gh_ref, w_ref, da_ref, dg_ref, gw_ref,
             dslab, hslab, dstage, hstage, dh0, gacc):
        i = pl.program_id(0)

        @pl.when(i == 0)
        def _():
            gacc[...] = jnp.zeros_like(gacc)

        dslab[0:tt, :] = d_ref[...]
        dslab[tt:tt + HALO, :] = jnp.where(i < nt - 1, dn_ref[...], 0.0)
        hslab[0:HALO, :] = jnp.where(i > 0, ah_ref[...] * _sigmoid(gh_ref[...]), 0.0)
        hslab[HALO:HALO + tt, :] = a_ref[...] * _sigmoid(g_ref[...])

        def chunk(c, carry):
            r0 = pl.multiple_of(c * ch, ch)
            dstage[...] = dslab[pl.ds(r0, 2 * ch), :]
            hstage[...] = hslab[pl.ds(r0, 2 * ch), :]
            lo, hi = _conv_taps(dstage, w_ref, ch, lambda j: (CW - 1) - j, lambda ln: jnp.zeros((ch, 512), f32))
            dh0[pl.ds(r0, ch), 0:512] = lo
            dh0[pl.ds(r0, ch), 512:1024] = hi
            for hf in range(2):
                ln = slice(512 * hf, 512 * hf + 512)
                dv = dstage[0:ch, ln]
                for j in range(CW):
                    pr = dv * hstage[pl.ds(HALO - (CW - 1) + j, ch), ln]
                    gacc[8 * j:8 * j + 8, ln] += pr[0:8] + pr[8:16] + pr[16:24] + pr[24:32]
            return carry

        lax.fori_loop(0, tt // ch, chunk, 0)
        a = a_ref[...]
        sg = _sigmoid(g_ref[...])
        d0 = dh0[...]
        da_ref[...] = (d0 * sg).astype(bf16)
        dg_ref[...] = (d0 * a * sg * (1.0 - sg)).astype(bf16)

        @pl.when(i == nt - 1)
        def _():
            gw_ref[...] = jnp.zeros_like(gw_ref)
            for j in range(CW):
                gw_ref[j:j + 1, :] = jnp.sum(gacc[8 * j:8 * j + 8, :], axis=0, keepdims=True)

    hpt = tt // HALO
    tile = lambda cb=0: pl.BlockSpec((tt, D), lambda i: (i, cb))
    halo_prev = lambda cb: pl.BlockSpec((HALO, D), lambda i: (jnp.maximum(i * hpt - 1, 0), cb))
    halo_next = pl.BlockSpec((HALO, D), lambda i: (jnp.minimum((i + 1) * hpt, T // HALO - 1), 0))
    wspec = pl.BlockSpec((HALO, D), lambda i: (0, 0))
    return pl.pallas_call(
        body, grid=(nt,),
        in_specs=[tile(), halo_next, tile(C_A), tile(C_G), halo_prev(C_A), halo_prev(C_G), wspec],
        out_specs=[tile(), tile(), wspec],
        out_shape=[SDS((T, D), bf16), SDS((T, D), bf16), SDS((HALO, D), f32)],
        scratch_shapes=[pltpu.VMEM((tt + HALO, D), f32), pltpu.VMEM((HALO + tt, D), f32),
                        pltpu.VMEM((2 * ch, D), f32), pltpu.VMEM((2 * ch, D), f32),
                        pltpu.VMEM((tt, D), f32), pltpu.VMEM((8 * HALO, D), f32)],
        name="conv_bwd", compiler_params=_cparams(("arbitrary",), 32))(dh1, dh1, proj, proj, proj, proj, w_dw)


def _attn_bwd(qn, kk, vv, bias, sinks, o, do, lse):
    T = qn.shape[0]
    nb = T // BLK

    def body(s_ref, q_ref, kc_ref, kp_ref, vc_ref, vp_ref, b_ref, o_ref, do_ref, lse_ref,
             dq_ref, dkc_ref, dkp_ref, dvc_ref, dvp_ref, dsk_ref, dsa_ref):
        n = pl.program_id(0)

        @pl.when(n == 0)
        def _():
            dsk_ref[...] = jnp.zeros_like(dsk_ref)
            dsa_ref[...] = jnp.zeros_like(dsa_ref)

        @pl.when(n == nb)
        def _():
            dkp_ref[...] = jnp.zeros_like(dkp_ref)
            dvp_ref[...] = jnp.zeros_like(dvp_ref)

        @pl.when(n < nb)
        def _():
            has_prev = n > 0
            lo = _low_head_lanes()
            dups = {"kc": [], "kp": [], "vc": [], "vp": []}
            for h in range(NKV):
                c = 256 * h
                qs = _rows2(q_ref, c)
                dos = _rows2(do_ref, c)
                sc = _dot_nt(qs, _rows2(kc_ref, c))
                sp = _dot_nt(qs, _rows2(kp_ref, c))
                dpc = _dot_nt(dos, _rows2(vc_ref, c))
                dpp = _dot_nt(dos, _rows2(vp_ref, c))
                kstack = jnp.concatenate([kp_ref[:, c:c + 128], kc_ref[:, c:c + 128],
                                          kp_ref[:, c + 128:c + 256], kc_ref[:, c + 128:c + 256]], axis=0)
                p_c, p_p, ds_c, ds_p = [], [], [], []
                for pr in range(2):
                    cc = c + 128 * pr
                    prod = do_ref[:, cc:cc + 128].astype(f32) * o_ref[:, cc:cc + 128].astype(f32)
                    d_lo = jnp.sum(jnp.where(lo, prod, 0.0), axis=-1, keepdims=True)
                    d_hi = jnp.sum(prod, axis=-1, keepdims=True) - d_lo
                    row_pc, row_pp, row_dc, row_dp = [], [], [], []
                    for e in range(2):
                        hq = 4 * h + 2 * pr + e
                        rows, cols = slice(128 * pr, 128 * pr + 128), slice(128 * e, 128 * e + 128)
                        delta = d_lo if e == 0 else d_hi
                        lse = lse_ref[:, hq:hq + 1]
                        pp = jnp.where(has_prev, jnp.exp(sp[rows, cols] + b_ref[hq, :, 0:BLK] - lse), 0.0)
                        pc = jnp.exp(sc[rows, cols] + b_ref[hq, :, BLK:2 * BLK] - lse)
                        dsp = pp * (dpp[rows, cols] - delta)
                        dsc = pc * (dpc[rows, cols] - delta)
                        dsa_ref[hq, :, 0:BLK] += dsp
                        dsa_ref[hq, :, BLK:2 * BLK] += dsc
                        dsk_ref[hq] += jnp.broadcast_to(-jnp.sum(jnp.exp(s_ref[0, hq] - lse) * delta), (8, 128))
                        row_pc.append(pc.astype(bf16))
                        row_pp.append(pp.astype(bf16))
                        row_dc.append(dsc.astype(bf16))
                        row_dp.append(dsp.astype(bf16))
                    dq_ref[:, cc:cc + 128] = _dot(jnp.concatenate([row_dp[0], row_dc[0], row_dp[1], row_dc[1]], axis=1), kstack)
                    p_c.append(jnp.concatenate(row_pc, axis=1))
                    p_p.append(jnp.concatenate(row_pp, axis=1))
                    ds_c.append(jnp.concatenate(row_dc, axis=1))
                    ds_p.append(jnp.concatenate(row_dp, axis=1))

                def to_keys(m2, rhs):
                    x2 = _dot_tn(jnp.concatenate(m2, axis=0), rhs)
                    x = jnp.where(lo, x2[0:128], x2[128:256])
                    return x + pltpu.roll(x, HD, 1)

                dups["kc"].append(to_keys(ds_c, qs))
                dups["kp"].append(to_keys(ds_p, qs))
                dups["vc"].append(to_keys(p_c, dos))
                dups["vp"].append(to_keys(p_p, dos))
            for key, ref in (("kc", dkc_ref), ("kp", dkp_ref), ("vc", dvc_ref), ("vp", dvp_ref)):
                d = dups[key]
                ref[:, 0:128] = jnp.where(lo, d[0], d[1])
                ref[:, 128:256] = jnp.where(lo, d[2], d[3])

    clamp = lambda n: jnp.minimum(n, nb - 1)
    blk = lambda f: pl.BlockSpec((BLK, D), f)
    cur = lambda n: (clamp(n), 0)
    prev = lambda n: (jnp.maximum(clamp(n) - 1, 0), 0)
    back = lambda n: (jnp.maximum(n - 1, 0), 0)
    kvb = lambda f: pl.BlockSpec((BLK, NKV * HD), f)
    return pl.pallas_call(
        body, grid=(nb + 1,),
        in_specs=[pl.BlockSpec(memory_space=pltpu.SMEM), blk(cur), blk(cur), blk(prev), blk(cur), blk(prev),
                  pl.BlockSpec((NQ, BLK, 2 * BLK), lambda n: (0, 0, 0)), blk(cur), blk(cur),
                  pl.BlockSpec((BLK, NQ), cur)],
        out_specs=[blk(cur), kvb(cur), kvb(back), kvb(cur), kvb(back),
                   pl.BlockSpec((NQ, 8, 128), lambda n: (0, 0, 0)),
                   pl.BlockSpec((NQ, BLK, 2 * BLK), lambda n: (0, 0, 0))],
        out_shape=[SDS((T, D), f32)] + [SDS((T, NKV * HD), f32)] * 4 + [SDS((NQ, 8, 128), f32), SDS((NQ, BLK, 2 * BLK), f32)],
        name="attn_bwd", compiler_params=_cparams(("arbitrary",), 40))(sinks, qn, kk, kk, vv, vv, bias, o, do, lse)


def _bias_bwd(dsa):
    def body(bk_ref, ds_ref, out_ref):
        bk = bk_ref[...]
        lane = lax.broadcasted_iota(jnp.int32, (1, 128), 1)
        for h in range(NQ):
            ds = ds_ref[h]
            row = jnp.zeros((1, 128), f32)
            for b in range(NBUCKET):
                row = jnp.where(lane == b, jnp.sum(jnp.where(bk == b, ds, 0.0)), row)
            out_ref[h:h + 1, :] = row

    return pl.pallas_call(body, out_shape=SDS((NQ, 128), f32), name="bias_bwd")(jnp.asarray(_bucket_tile()), dsa)


def _qkv_bwd(proj, gq2, gk2, dqn, dkc, dkp, dvc, dvp):
    T = proj.shape[0]
    tm = 512

    def body(q_ref, kv_ref, gq_ref, gk_ref, dq_ref, dkc_ref, dkp_ref, dvc_ref, dvp_ref,
             oq_ref, okv_ref, ggq_ref, ggk_ref):
        @pl.when(pl.program_id(0) == 0)
        def _():
            ggq_ref[...] = jnp.zeros_like(ggq_ref)
            ggk_ref[...] = jnp.zeros_like(ggk_ref)

        bd = _head_blockdiag()

        def norm_bwd(z, dy, g, scale):
            r = lax.rsqrt(_head_sums(z * z, bd) * (1.0 / HD) + EPS)
            gd = dy * g * scale
            dz = r * gd - z * (r * r * r) * _head_sums(z * gd, bd) * (1.0 / HD)
            return dz, jnp.sum(dy * scale * z * r, axis=0, keepdims=True)

        gq = jnp.zeros((1, 128), f32)
        for p in range(NQ // 2):
            ln = slice(128 * p, 128 * p + 128)
            dz, dg = norm_bwd(q_ref[:, ln], dq_ref[:, ln], gq_ref[...], HD ** -0.5)
            oq_ref[:, ln] = dz.astype(bf16)
            gq = gq + dg
        ggq_ref[...] += gq + pltpu.roll(gq, HD, 1)
        gk = jnp.zeros((1, 128), f32)
        for p in range(NKV // 2):
            ln = slice(128 * p, 128 * p + 128)
            dz, dg = norm_bwd(kv_ref[:, ln], dkc_ref[:, ln] + dkp_ref[:, ln], gk_ref[...], 1.0)
            okv_ref[:, ln] = dz.astype(bf16)
            gk = gk + dg
        ggk_ref[...] += gk + pltpu.roll(gk, HD, 1)
        okv_ref[:, 256:512] = (dvc_ref[...] + dvp_ref[...]).astype(bf16)

    vec = pl.BlockSpec((1, 128), lambda i: (0, 0))
    kvb = pl.BlockSpec((tm, NKV * HD), lambda i: (i, 0))
    return pl.pallas_call(
        body, grid=(T // tm,),
        in_specs=[pl.BlockSpec((tm, D), lambda i: (i, C_Q)), pl.BlockSpec((tm, 512), lambda i: (i, C_KV)), vec, vec,
                  pl.BlockSpec((tm, D), lambda i: (i, 0)), kvb, kvb, kvb, kvb],
        out_specs=[pl.BlockSpec((tm, D), lambda i: (i, 0)), pl.BlockSpec((tm, 512), lambda i: (i, 0)), vec, vec],
        out_shape=[SDS((T, D), bf16), SDS((T, 512), bf16), SDS((1, 128), f32), SDS((1, 128), f32)],
        name="qkv_bwd", compiler_params=_cparams(("arbitrary",), 32))(proj, proj, gq2, gk2, dqn, dkc, dkp, dvc, dvp)


def _inproj_bwd(pieces, w_in, x, dx1, g_mix):
    T = x.shape[0]
    tm = 256
    widths = [p.shape[1] for p in pieces]
    offs = [sum(widths[:i]) for i in range(len(widths))]
    assert sum(widths) == INW

    def body(*refs):
        p_refs, (w_ref, x_ref, dx1_ref, g_ref, dx_ref, dg_ref) = refs[:len(pieces)], refs[len(pieces):]

        @pl.when(pl.program_id(0) == 0)
        def _():
            dg_ref[...] = jnp.zeros_like(dg_ref)

        du = None
        for p_ref, off, wd in zip(p_refs, offs, widths):
            part = _dot_nt(p_ref[...], w_ref[:, off:off + wd])
            du = part if du is None else du + part
        dx, dg = _rms_bwd(x_ref[...], g_ref[...], du, dx1_ref[...])
        dx_ref[...] = dx
        dg_ref[...] += dg

    row = pl.BlockSpec((tm, D), lambda i: (i, 0))
    vec = pl.BlockSpec((1, D), lambda i: (0, 0))
    return pl.pallas_call(
        body, grid=(T // tm,),
        in_specs=[pl.BlockSpec((tm, wd), lambda i: (i, 0)) for wd in widths]
        + [pl.BlockSpec((D, INW), lambda i: (0, 0)), row, row, vec],
        out_specs=[row, vec],
        out_shape=[SDS((T, D), f32), SDS((1, D), f32)],
        name="inproj_bwd", compiler_params=_cparams(("arbitrary",), 48))(*pieces, w_in, x, dx1, g_mix)


def _to_internal_cols(w):
    return jnp.concatenate([w[..., 0:1024], w[..., 1536:INW], w[..., 1024:1536]], axis=-1)


def _local_step(x, tgt, w):
    gq2 = jnp.tile(w["q_norm_g"], (1, 2))
    gk2 = jnp.tile(w["k_norm_g"], (1, 2))
    proj, u = _rms_inproj(x, w["norm_mix_g"], w["w_in"])
    qn, kk, vv = _qk_prep(proj, gq2, gk2)
    bias = _bias_tiles(w["rel_bias"])
    o, lse = _attn_fwd(qn, kk, vv, bias, w["attn_sinks"])
    h1, h3 = _glu_conv_fwd(proj, w["w_dw"], w["b_dw"], w["conv_ln_g"], w["conv_ln_b"])
    attn, conv, merged, x1, n2 = _mix_out(o, h3, proj, x, w["w_attn_o"], w["w_conv_out"], w["w_out"], w["norm_mlp_g"])
    hmid, dy, dyb, loss = _mlp_fwd(n2, w["w_ff1"], w["w_ff2"], x1, tgt)

    g = {}
    df1, dx1, dx1b, g["norm_mlp_g"] = _mlp_bwd(dy, dyb, hmid, w["w_ff1"], w["w_ff2"], x1, w["norm_mlp_g"])
    g["w_ff2"] = _wgrad(hmid, dyb, "wgrad_ff2")
    g["w_ff1"] = _wgrad(n2, df1, "wgrad_ff1")
    dat, dcv, do, dh3, dga, dgc = _mix_bwd(dx1b, proj, attn, conv, w["w_attn_o"], w["w_conv_out"], w["w_out"])
    g["w_out"] = _wgrad(merged, dx1b, "wgrad_out")
    g["w_attn_o"] = _wgrad(o, dat, "wgrad_attn_o")
    g["w_conv_out"] = _wgrad(h3, dcv, "wgrad_conv_out")
    dh1, lnacc = _conv_ln_bwd(dh3, h1, w["conv_ln_g"], w["conv_ln_b"])
    g["conv_ln_g"], g["conv_ln_b"], g["b_dw"] = lnacc[0:1], lnacc[1:2], lnacc[2:3]
    da, dg, gw_dw = _conv_bwd(dh1, proj, w["w_dw"])
    g["w_dw"] = gw_dw
    dqn, dkc, dkp, dvc, dvp, dsk, dsa = _attn_bwd(qn, kk, vv, bias, w["attn_sinks"], o, do, lse)
    g["attn_sinks"] = dsk[:, 0, 0].reshape(1, NQ)
    g["rel_bias"] = _bias_bwd(dsa)[:, 0:NBUCKET].T
    dq, dkv, ggq, ggk = _qkv_bwd(proj, gq2, gk2, dqn, dkc, dkp, dvc, dvp)
    g["q_norm_g"], g["k_norm_g"] = ggq[:, 0:HD], ggk[:, 0:HD]
    pieces = [dq, da, dg, dga, dgc, dkv]
    names = ["q", "a", "g", "ga", "gc", "kv"]
    gw = {nm: _wgrad(u, p, "wgrad_in_" + nm, tn=p.shape[1] if p.shape[1] < 1024 else 1024) for nm, p in zip(names, pieces)}
    g["w_in"] = jnp.concatenate([gw["q"], gw["kv"], gw["a"], gw["g"], gw["ga"], gw["gc"]], axis=1)
    grad_x, g["norm_mix_g"] = _inproj_bwd(pieces, w["w_in"], x, dx1, w["norm_mix_g"])
    return loss[0, 0], grad_x, g


ANY = pl.BlockSpec(memory_space=pl.ANY)
BIG = ["w_in", "w_attn_o", "w_conv_out", "w_out", "w_ff1", "w_ff2"]
SHARD_AXIS = {"w_in": 1, "w_attn_o": 0, "w_conv_out": 0, "w_out": 0, "w_ff1": 1, "w_ff2": 0, "w_dw": 1}
SHARD_SHAPE = {"w_in": (D, INW // 4), "w_attn_o": (D // 4, D), "w_conv_out": (D // 4, D), "w_out": (D // 4, D),
               "w_ff1": (D, DFF // 4), "w_ff2": (DFF // 4, D), "w_dw": (HALO, D // 4)}
PACK_ROWS = {"w_attn_o": (0, 128), "w_conv_out": (128, 128), "w_out": (256, 128), "w_ff1": (384, 512), "w_ff2": (896, 512)}
PACK_O = 1408
HALF_IN = (D // 2, INW // 4)


def _position():
    x, y, c = lax.axis_index("x"), lax.axis_index("y"), lax.axis_index("c")
    other_chips = [(1 - x, y), (x, 1 - y), (1 - x, 1 - y)]
    return x, y, c, 2 * x + y, other_chips


def _shard_window(name, full_ref, s, half=None):
    R, C = SHARD_SHAPE[name]
    r0, nr = (0, R) if half is None else (half * (R // 2), R // 2)
    if SHARD_AXIS[name] == 1:
        return full_ref.at[pl.ds(r0, nr), pl.ds(s * C, C)]
    return full_ref.at[pl.ds(s * R + r0, nr), :]


def _remote(src, dst, send_sems, recv_sems, k, device):
    return pltpu.make_async_remote_copy(src_ref=src, dst_ref=dst, send_sem=send_sems.at[k], recv_sem=recv_sems.at[k],
                                        device_id=device, device_id_type=MESH)


def _gather_weights(shards):
    names = list(shards)
    n = len(names)

    def body(*refs):
        srcs = dict(zip(names, refs[:n]))
        dsts = dict(zip(names, refs[n:2 * n]))
        send_sems, recv_sems, local_sems = refs[2 * n:]
        x, y, c, chip, chips = _position()
        sibling = (x, y, 1 - c)
        local, sent = [], []
        for a, nm in enumerate(names):
            cp = pltpu.make_async_copy(srcs[nm], _shard_window(nm, dsts[nm], chip), local_sems.at[a])
            cp.start()
            local.append(cp)
            R = SHARD_SHAPE[nm][0]
            mine = srcs[nm].at[pl.ds(c * (R // 2), R // 2), :]
            for j, (cx, cy) in enumerate(chips):
                cp = _remote(mine, _shard_window(nm, dsts[nm], chip, c), send_sems, recv_sems, 6 * a + j, (cx, cy, c))
                cp.start()
                sent.append(cp)
        for a, nm in enumerate(names):
            for j, (cx, cy) in enumerate(chips):
                w = _shard_window(nm, dsts[nm], 2 * cx + cy, c)
                _remote(w, w, send_sems, recv_sems, 6 * a + j, (cx, cy, c)).wait_recv()
                cp = _remote(w, w, send_sems, recv_sems, 6 * a + 3 + j, sibling)
                cp.start()
                sent.append(cp)
        for a, nm in enumerate(names):
            for j, (cx, cy) in enumerate(chips):
                w = _shard_window(nm, dsts[nm], 2 * cx + cy, 1 - c)
                _remote(w, w, send_sems, recv_sems, 6 * a + 3 + j, sibling).wait_recv()
        for cp in sent:
            cp.wait_send()
        for cp in local:
            cp.wait()

    def full_shape(nm):
        R, C = SHARD_SHAPE[nm]
        return SDS((R, 4 * C) if SHARD_AXIS[nm] == 1 else (4 * R, C), shards[nm].dtype)

    out = pl.pallas_call(
        body, in_specs=[ANY] * n, out_specs=[ANY] * n, out_shape=[full_shape(nm) for nm in names],
        scratch_shapes=[pltpu.SemaphoreType.DMA((6 * n,)), pltpu.SemaphoreType.DMA((6 * n,)), pltpu.SemaphoreType.DMA((n,))],
        name="gather_weights")(*[shards[nm] for nm in names])
    return dict(zip(names, out))


def _packed_view(nm, buf_w, buf_o, slot):
    if nm == "w_in":
        return buf_w if slot is None else buf_w.at[slot]
    r0, nr = PACK_ROWS[nm]
    return buf_o.at[pl.ds(r0, nr), :] if slot is None else buf_o.at[slot, pl.ds(r0, nr), :]


def _pair_exchange(grads):
    def body(*refs):
        g = dict(zip(BIG, refs[:6]))
        own_w, own_o, got_w, got_o, send_sems, recv_sems, local_sems = refs[6:]
        x, y, c, chip, chips = _position()
        cps = []
        for a, nm in enumerate(BIG):
            for s in range(4):
                k = 4 * a + s
                lc = pltpu.make_async_copy(_shard_window(nm, g[nm], s, c), _packed_view(nm, own_w, own_o, s), local_sems.at[k])
                rc = _remote(_shard_window(nm, g[nm], s, 1 - c), _packed_view(nm, got_w, got_o, s),
                             send_sems, recv_sems, k, (x, y, 1 - c))
                lc.start()
                rc.start()
                cps += [lc, rc]
        for cp in cps:
            cp.wait()

    shp_w, shp_o = SDS((4,) + HALF_IN, f32), SDS((4, PACK_O, D), f32)
    return pl.pallas_call(
        body, in_specs=[ANY] * 6, out_specs=[ANY] * 4, out_shape=[shp_w, shp_o, shp_w, shp_o],
        scratch_shapes=[pltpu.SemaphoreType.DMA((24,)), pltpu.SemaphoreType.DMA((24,)), pltpu.SemaphoreType.DMA((24,))],
        name="rs_pair_exchange")(*[grads[nm] for nm in BIG])


def _chip_exchange(cp_w, cp_o):
    def body(w_ref, o_ref, gw_ref, go_ref, send_sems, recv_sems):
        x, y, c, chip, chips = _position()
        cps = []
        for j, (cx, cy) in enumerate(chips):
            s = 2 * cx + cy
            cps.append(_remote(w_ref.at[s], gw_ref.at[j], send_sems, recv_sems, 2 * j, (cx, cy, c)))
            cps.append(_remote(o_ref.at[s], go_ref.at[j], send_sems, recv_sems, 2 * j + 1, (cx, cy, c)))
        for cp in cps:
            cp.start()
        for cp in cps:
            cp.wait()

    return pl.pallas_call(
        body, in_specs=[ANY] * 2, out_specs=[ANY] * 2, out_shape=[SDS((3,) + HALF_IN, f32), SDS((3, PACK_O, D), f32)],
        scratch_shapes=[pltpu.SemaphoreType.DMA((6,)), pltpu.SemaphoreType.DMA((6,))],
        name="rs_chip_exchange")(cp_w, cp_o)


def _pair_share(tot_w, tot_o):
    def body(w_ref, o_ref, *refs):
        outs = dict(zip(BIG, refs[:6]))
        send_sems, recv_sems, local_sems = refs[6:]
        x, y, c, chip, chips = _position()
        cps = []
        for a, nm in enumerate(BIG):
            R = SHARD_SHAPE[nm][0]
            src = _packed_view(nm, w_ref, o_ref, None)
            dst = outs[nm].at[pl.ds(c * (R // 2), R // 2), :]
            lc = pltpu.make_async_copy(src, dst, local_sems.at[a])
            rc = _remote(src, dst, send_sems, recv_sems, a, (x, y, 1 - c))
            lc.start()
            rc.start()
            cps += [lc, rc]
        for cp in cps:
            cp.wait()

    out = pl.pallas_call(
        body, in_specs=[ANY] * 2, out_specs=[ANY] * 6, out_shape=[SDS(SHARD_SHAPE[nm], f32) for nm in BIG],
        scratch_shapes=[pltpu.SemaphoreType.DMA((6,)), pltpu.SemaphoreType.DMA((6,)), pltpu.SemaphoreType.DMA((6,))],
        name="rs_pair_share")(tot_w, tot_o)
    return dict(zip(BIG, out))


def _sum_arrays(arrs, name):
    shape = arrs[0].shape
    flat = [a.reshape(-1, shape[-1]) for a in arrs]
    rows, cols = flat[0].shape
    tr = 128
    assert rows % tr == 0

    def body(*refs):
        acc = refs[0][...]
        for r in refs[1:-1]:
            acc = acc + r[...]
        refs[-1][...] = acc

    spec = pl.BlockSpec((tr, cols), lambda i: (i, 0))
    out = pl.pallas_call(body, grid=(rows // tr,), in_specs=[spec] * len(flat), out_specs=spec,
                         out_shape=SDS((rows, cols), f32), name=name,
                         compiler_params=_cparams(("parallel",), 32))(*flat)
    return out.reshape(shape)


def _reduce_scatter(grads, chip):
    own_w, own_o, got_w, got_o = _pair_exchange(grads)
    cp_w = _sum_arrays([own_w, got_w], "rs_pair_sum_in")
    cp_o = _sum_arrays([own_o, got_o], "rs_pair_sum_rest")
    rc_w, rc_o = _chip_exchange(cp_w, cp_o)
    mine_w = lax.dynamic_index_in_dim(cp_w, chip, 0, keepdims=False)
    mine_o = lax.dynamic_index_in_dim(cp_o, chip, 0, keepdims=False)
    tot_w = _sum_arrays([mine_w, rc_w[0], rc_w[1], rc_w[2]], "rs_chip_sum_in")
    tot_o = _sum_arrays([mine_o, rc_o[0], rc_o[1], rc_o[2]], "rs_chip_sum_rest")
    return _pair_share(tot_w, tot_o)


SMALL_ROWS = 40


def _allreduce_small(block):
    def body(x_ref, out_ref, buf, send_sems, recv_sems, local_sem):
        x, y, c, chip, chips = _position()
        me, sibling = (x, y, c), (x, y, 1 - c)

        def slot(px, py, pc):
            return buf.at[4 * px + 2 * py + pc]

        def copy(k, block_of, to, src=None):
            return _remote(slot(*block_of) if src is None else src, slot(*block_of), send_sems, recv_sems, k, to)

        mine = pltpu.make_async_copy(x_ref, slot(*me), local_sem)
        mine.start()
        first = [copy(0, me, sibling, src=x_ref)] + [copy(1 + j, me, (*ch, c), src=x_ref) for j, ch in enumerate(chips)]
        for cp in first:
            cp.start()
        passed = [copy(4 + j, (*ch, c), sibling) for j, ch in enumerate(chips)]
        for j, ch in enumerate(chips):
            copy(1 + j, (*ch, c), me).wait_recv()
            passed[j].start()
        copy(0, sibling, me).wait_recv()
        for j, ch in enumerate(chips):
            copy(4 + j, (*ch, 1 - c), me).wait_recv()
        for cp in first + passed:
            cp.wait_send()
        mine.wait()
        acc = buf[0]
        for d in range(1, 8):
            acc = acc + buf[d]
        out_ref[...] = acc

    vm = pl.BlockSpec(memory_space=pltpu.VMEM)
    return pl.pallas_call(
        body, in_specs=[vm], out_specs=vm, out_shape=SDS((SMALL_ROWS, D), f32),
        scratch_shapes=[pltpu.VMEM((8, SMALL_ROWS, D), f32), pltpu.SemaphoreType.DMA((7,)), pltpu.SemaphoreType.DMA((7,)),
                        pltpu.SemaphoreType.DMA],
        name="allreduce_small")(block)


def _adamw(w, g, m, v, name):
    rows, cols = w.shape
    tr = 256 if rows % 256 == 0 else rows

    def body(w_ref, g_ref, m_ref, v_ref, d_ref, nm_ref, nv_ref):
        gv = g_ref[...]
        m2 = ADAM_B1 * m_ref[...] + (1.0 - ADAM_B1) * gv
        v2 = ADAM_B2 * v_ref[...] + (1.0 - ADAM_B2) * jnp.square(gv)
        m_hat = m2 / (1.0 - ADAM_B1 ** ADAM_STEP)
        v_hat = v2 / (1.0 - ADAM_B2 ** ADAM_STEP)
        d_ref[...] = -ADAM_LR * (m_hat / (jnp.sqrt(v_hat) + ADAM_EPS) + ADAM_WD * w_ref[...])
        nm_ref[...] = m2
        nv_ref[...] = v2

    spec = pl.BlockSpec((tr, cols), lambda i: (i, 0))
    return pl.pallas_call(body, grid=(rows // tr,), in_specs=[spec] * 4, out_specs=[spec] * 3,
                          out_shape=[SDS((rows, cols), f32)] * 3, name=name,
                          compiler_params=_cparams(("parallel",), 40))(w, g, m, v)


WEIGHTS = ["norm_mix_g", "w_in", "q_norm_g", "k_norm_g", "attn_sinks", "rel_bias", "w_attn_o", "w_dw", "b_dw",
           "conv_ln_g", "conv_ln_b", "w_conv_out", "w_out", "norm_mlp_g", "w_ff1", "w_ff2"]
ROW_VECS = ["norm_mix_g", "b_dw", "conv_ln_g", "conv_ln_b", "norm_mlp_g"]
MISC_ROW = 5
W_DW_ROW = 8


def _pack_small(vals, loss=None):
    misc = [vals["q_norm_g"].reshape(1, HD), vals["k_norm_g"].reshape(1, HD), vals["attn_sinks"].reshape(1, NQ),
            jnp.zeros((1, 1), f32) if loss is None else loss.reshape(1, 1), jnp.zeros((1, 111), f32),
            vals["rel_bias"].reshape(1, NBUCKET * NQ), jnp.zeros((1, 256), f32)]
    rows = [vals[nm].reshape(1, D) for nm in ROW_VECS] + [jnp.concatenate(misc, axis=1), jnp.zeros((2, D), f32)]
    return jnp.concatenate(rows, axis=0)


def _unpack_small(block):
    out = {nm: block[i:i + 1] for i, nm in enumerate(ROW_VECS)}
    misc = block[MISC_ROW]
    out["q_norm_g"] = misc[0:64].reshape(1, HD)
    out["k_norm_g"] = misc[64:128].reshape(1, HD)
    out["attn_sinks"] = misc[128:144].reshape(1, NQ)
    out["rel_bias"] = misc[256:768].reshape(NBUCKET, NQ)
    return out, misc[144]


def kernel(x, norm_mix_g, w_in, q_norm_g, k_norm_g, attn_sinks, rel_bias, w_attn_o, w_dw, b_dw, conv_ln_g, conv_ln_b, w_conv_out, w_out, norm_mlp_g, w_ff1, w_ff2, loss_target, m_norm_mix_g, m_w_in, m_q_norm_g, m_k_norm_g, m_attn_sinks, m_rel_bias, m_w_attn_o, m_w_dw, m_b_dw, m_conv_ln_g, m_conv_ln_b, m_w_conv_out, m_w_out, m_norm_mlp_g, m_w_ff1, m_w_ff2, v_norm_mix_g, v_w_in, v_q_norm_g, v_k_norm_g, v_attn_sinks, v_rel_bias, v_w_attn_o, v_w_dw, v_b_dw, v_conv_ln_g, v_conv_ln_b, v_w_conv_out, v_w_out, v_norm_mlp_g, v_w_ff1, v_w_ff2):
    args = dict(locals())
    wts = {nm: args[nm] for nm in WEIGHTS}
    mom = {nm: args["m_" + nm] for nm in WEIGHTS}
    var = {nm: args["v_" + nm] for nm in WEIGHTS}
    chip = 2 * lax.axis_index("x") + lax.axis_index("y")

    shards = {nm: wts[nm][0].astype(bf16) for nm in BIG}
    shards["w_dw"] = jnp.pad(w_dw[0], ((0, 1), (0, 0)))
    full = _gather_weights(shards)
    full["w_in"] = _to_internal_cols(full["w_in"])
    for nm in WEIGHTS:
        if nm not in full:
            full[nm] = wts[nm]

    loss_part, grad_x, g = _local_step(x[0], loss_target[0], full)

    small = jnp.concatenate([_pack_small(g, loss_part), g["w_dw"]], axis=0)
    small = _allreduce_small(small)
    grads, loss = _unpack_small(small)
    grads["w_dw"] = lax.dynamic_slice(small[W_DW_ROW:W_DW_ROW + CW], (0, chip * (D // 4)), (CW, D // 4))
    grads.update(_reduce_scatter(g, chip))

    delta, new_m, new_v = {}, {}, {}
    sd, sm, sv = _adamw(_pack_small(wts), small[0:8], _pack_small(mom), _pack_small(var), "adamw_small")
    for res, blk in ((delta, sd), (new_m, sm), (new_v, sv)):
        res.update(_unpack_small(blk)[0])
    for nm in BIG + ["w_dw"]:
        shp = wts[nm].shape
        two_d = lambda a: a.reshape(shp[-2], shp[-1])
        delta[nm], new_m[nm], new_v[nm] = _adamw(two_d(wts[nm]), grads[nm], two_d(mom[nm]), two_d(var[nm]), "adamw_" + nm)

    def shaped(vals):
        return [vals[nm].reshape(wts[nm].shape) for nm in WEIGHTS]

    return (loss, grad_x[None], *shaped(grads), *shaped(delta), *shaped(new_m), *shaped(new_v))
```

```python
import functools

import numpy as np
import jax
import jax.numpy as jnp
from jax import lax
from jax.experimental import pallas as pl
from jax.experimental.pallas import tpu as pltpu

f32 = jnp.float32
bf16 = jnp.bfloat16
SDS = jax.ShapeDtypeStruct
MESH = pl.DeviceIdType.MESH

D = 1024
HD = 64
NQ = 16
NKV = 4
BLK = 128
CW = 31
HALO = 32
DFF = 4096
NBUCKET = 32
EPS = 1e-6
NEG = -1e30
INW = 5632
C_Q, C_A, C_G, C_GA, C_GC = 0, 1, 2, 3, 4
C_KV = 10

ADAM_LR = 0.001
ADAM_B1 = 0.9
ADAM_B2 = 0.999
ADAM_EPS = 1e-08
ADAM_WD = 0.01
ADAM_STEP = 10

VMEM_BYTES_V7X = 64 << 20


def _cparams(sem, vmem_mb):
    assert (vmem_mb << 20) < VMEM_BYTES_V7X
    return pltpu.CompilerParams(dimension_semantics=sem, vmem_limit_bytes=vmem_mb << 20)


def _dot(a, b):
    return jnp.dot(a, b, preferred_element_type=f32)


def _dot_nt(a, b):
    return lax.dot_general(a, b, (((1,), (1,)), ((), ())), preferred_element_type=f32)


def _dot_tn(a, b):
    return lax.dot_general(a, b, (((0,), (0,)), ((), ())), preferred_element_type=f32)


def _sigmoid(x):
    return 1.0 / (1.0 + jnp.exp(-x))


def _low_head_lanes():
    return lax.broadcasted_iota(jnp.int32, (1, 2 * HD), 1) < HD


def _head_blockdiag():
    r = lax.broadcasted_iota(jnp.int32, (2 * HD, 2 * HD), 0) // HD
    c = lax.broadcasted_iota(jnp.int32, (2 * HD, 2 * HD), 1) // HD
    return jnp.where(r == c, 1.0, 0.0).astype(bf16)


def _head_sums(z, bd):
    hi = z.astype(bf16)
    lo = (z - hi.astype(f32)).astype(bf16)
    return _dot(hi, bd) + _dot(lo, bd)


def _rms_inproj(x, g, w):
    T, N = x.shape[0], w.shape[1]
    tm, tn = 512, 512

    def body(x_ref, g_ref, w_ref, p_ref, u_ref):
        @pl.when(pl.program_id(1) == 0)
        def _():
            xv = x_ref[...]
            r = lax.rsqrt(jnp.mean(xv * xv, axis=-1, keepdims=True) + EPS)
            u_ref[...] = (xv * r * g_ref[...]).astype(bf16)
        p_ref[...] = _dot(u_ref[...], w_ref[...])

    return pl.pallas_call(
        body, grid=(T // tm, N // tn),
        in_specs=[pl.BlockSpec((tm, D), lambda i, j: (i, 0)),
                  pl.BlockSpec((1, D), lambda i, j: (0, 0)),
                  pl.BlockSpec((D, tn), lambda i, j: (0, j))],
        out_specs=[pl.BlockSpec((tm, tn), lambda i, j: (i, j)),
                   pl.BlockSpec((tm, D), lambda i, j: (i, 0))],
        out_shape=[SDS((T, N), f32), SDS((T, D), bf16)],
        name="rms_inproj", compiler_params=_cparams(("parallel", "arbitrary"), 32))(x, g, w)


def _split_pair(pair, out_ref, p, lo):
    rolled = pltpu.roll(pair, HD, 1)
    zero = jnp.zeros_like(pair)
    c = 512 * p
    out_ref[:, c:c + 128] = jnp.where(lo, pair, zero).astype(bf16)
    out_ref[:, c + 128:c + 256] = jnp.where(lo, zero, rolled).astype(bf16)
    out_ref[:, c + 256:c + 384] = jnp.where(lo, rolled, zero).astype(bf16)
    out_ref[:, c + 384:c + 512] = jnp.where(lo, zero, pair).astype(bf16)


def _qk_prep(proj, gq2, gk2):
    T = proj.shape[0]
    tm = 512

    def body(q_ref, kv_ref, gq_ref, gk_ref, qn_ref, kk_ref, vv_ref):
        bd = _head_blockdiag()
        lo = _low_head_lanes()
        for p in range(NQ // 2):
            z = q_ref[:, 128 * p:128 * p + 128]
            r = lax.rsqrt(_head_sums(z * z, bd) * (1.0 / HD) + EPS)
            qn_ref[:, 128 * p:128 * p + 128] = (z * r * gq_ref[...] * (HD ** -0.5)).astype(bf16)
        for p in range(NKV // 2):
            z = kv_ref[:, 128 * p:128 * p + 128]
            r = lax.rsqrt(_head_sums(z * z, bd) * (1.0 / HD) + EPS)
            _split_pair(z * r * gk_ref[...], kk_ref, p, lo)
            _split_pair(kv_ref[:, 256 + 128 * p:256 + 128 * p + 128], vv_ref, p, lo)

    return pl.pallas_call(
        body, grid=(T // tm,),
        in_specs=[pl.BlockSpec((tm, D), lambda i: (i, C_Q)),
                  pl.BlockSpec((tm, 512), lambda i: (i, C_KV)),
                  pl.BlockSpec((1, 128), lambda i: (0, 0)),
                  pl.BlockSpec((1, 128), lambda i: (0, 0))],
        out_specs=[pl.BlockSpec((tm, D), lambda i: (i, 0))] * 3,
        out_shape=[SDS((T, D), bf16)] * 3,
        name="qk_prep", compiler_params=_cparams(("parallel",), 32))(proj, proj, gq2, gk2)


def _bucket_tile():
    qi = np.arange(BLK)[:, None]
    kj = np.arange(2 * BLK)[None, :]
    dist = qi + BLK - kj
    n = np.maximum(dist, 0)
    max_exact = NBUCKET // 2
    nf = np.maximum(n, 1).astype(np.float32)
    large = max_exact + (np.log(nf / max_exact) / np.float32(np.log(128 / max_exact))
                         * (NBUCKET - max_exact)).astype(np.int32)
    large = np.minimum(large, NBUCKET - 1)
    bucket = np.where(n < max_exact, n, large)
    valid = (dist >= 0) & (dist < BLK)
    return np.where(valid, bucket, -1).astype(np.int32)


def _bias_tiles(rel_bias):
    def body(rb_ref, bk_ref, out_ref):
        bk = bk_ref[...]
        for h in range(NQ):
            acc = jnp.full((BLK, 2 * BLK), NEG, f32)
            for b in range(NBUCKET):
                acc = jnp.where(bk == b, rb_ref[b, h], acc)
            out_ref[h] = acc

    return pl.pallas_call(
        body,
        in_specs=[pl.BlockSpec(memory_space=pltpu.SMEM), pl.BlockSpec(memory_space=pltpu.VMEM)],
        out_specs=pl.BlockSpec(memory_space=pltpu.VMEM),
        out_shape=SDS((NQ, BLK, 2 * BLK), f32),
        name="bias_tiles")(rel_bias, jnp.asarray(_bucket_tile()))


def _rows2(ref, c):
    return jnp.concatenate([ref[:, c:c + 128], ref[:, c + 128:c + 256]], axis=0)


def _attn_fwd(qn, kk, vv, bias, sinks):
    T = qn.shape[0]
    nb = T // BLK

    def body(s_ref, q_ref, kc_ref, kp_ref, vc_ref, vp_ref, b_ref, o_ref, lse_ref):
        has_prev = pl.program_id(0) > 0
        for h in range(NKV):
            c = 256 * h
            qs = _rows2(q_ref, c)
            sc = _dot_nt(qs, _rows2(kc_ref, c))
            sp = _dot_nt(qs, _rows2(kp_ref, c))
            vstack = jnp.concatenate([vp_ref[:, c:c + 128], vc_ref[:, c:c + 128],
                                      vp_ref[:, c + 128:c + 256], vc_ref[:, c + 128:c + 256]], axis=0)
            for pr in range(2):
                ps = []
                for e in range(2):
                    hq = 4 * h + 2 * pr + e
                    rows, cols = slice(128 * pr, 128 * pr + 128), slice(128 * e, 128 * e + 128)
                    s_p = jnp.where(has_prev, sp[rows, cols] + b_ref[hq, :, 0:BLK], NEG)
                    s_c = sc[rows, cols] + b_ref[hq, :, BLK:2 * BLK]
                    sink = s_ref[0, hq]
                    m = jnp.maximum(jnp.maximum(jnp.max(s_p, axis=-1, keepdims=True),
                                                jnp.max(s_c, axis=-1, keepdims=True)), sink)
                    e_p = jnp.exp(s_p - m)
                    e_c = jnp.exp(s_c - m)
                    l = (jnp.sum(e_p, axis=-1, keepdims=True) + jnp.sum(e_c, axis=-1, keepdims=True)
                         + jnp.exp(sink - m))
                    inv = 1.0 / l
                    ps += [(e_p * inv).astype(bf16), (e_c * inv).astype(bf16)]
                    lse_ref[:, hq:hq + 1] = m + jnp.log(l)
                o_ref[:, c + 128 * pr:c + 128 * pr + 128] = _dot(jnp.concatenate(ps, axis=1), vstack).astype(bf16)

    blk = lambda f: pl.BlockSpec((BLK, D), f)
    cur = lambda n: (n, 0)
    prev = lambda n: (jnp.maximum(n - 1, 0), 0)
    return pl.pallas_call(
        body, grid=(nb,),
        in_specs=[pl.BlockSpec(memory_space=pltpu.SMEM), blk(cur), blk(cur), blk(prev), blk(cur), blk(prev),
                  pl.BlockSpec((NQ, BLK, 2 * BLK), lambda n: (0, 0, 0))],
        out_specs=[blk(cur), pl.BlockSpec((BLK, NQ), cur)],
        out_shape=[SDS((T, D), bf16), SDS((T, NQ), f32)],
        name="attn_fwd", compiler_params=_cparams(("parallel",), 32))(sinks, qn, kk, kk, vv, vv, bias)


def _conv_taps(stage_ref, w_ref, rows, offset_of_tap, init):
    halves = []
    for hf in range(2):
        ln = slice(512 * hf, 512 * hf + 512)
        acc = init(ln)
        for j in range(CW):
            acc = acc + stage_ref[pl.ds(offset_of_tap(j), rows), ln] * w_ref[j:j + 1, ln]
        halves.append(acc)
    return halves


def _glu_conv_fwd(proj, w_dw, b_dw, ln_g, ln_b):
    T = proj.shape[0]
    tt, ch = 256, 32

    def body(a_ref, g_ref, ah_ref, gh_ref, w_ref, b_ref, lg_ref, lb_ref, h1_ref, h3_ref, slab, stage):
        i = pl.program_id(0)
        halo = ah_ref[...] * _sigmoid(gh_ref[...])
        slab[0:HALO, :] = jnp.where(i > 0, halo, 0.0)
        slab[HALO:HALO + tt, :] = a_ref[...] * _sigmoid(g_ref[...])

        def chunk(c, carry):
            r0 = pl.multiple_of(c * ch, ch)
            stage[...] = slab[pl.ds(r0, 2 * ch), :]
            lo, hi = _conv_taps(stage, w_ref, ch, lambda j: HALO - (CW - 1) + j,
                                lambda ln: jnp.broadcast_to(b_ref[:, ln], (ch, 512)))
            h1_ref[pl.ds(r0, ch), 0:512] = lo
            h1_ref[pl.ds(r0, ch), 512:1024] = hi
            return carry

        lax.fori_loop(0, tt // ch, chunk, 0)
        h1 = h1_ref[...]
        mu = jnp.mean(h1, axis=-1, keepdims=True)
        xc = h1 - mu
        var = jnp.mean(xc * xc, axis=-1, keepdims=True)
        h2 = xc * lax.rsqrt(var + EPS) * lg_ref[...] + lb_ref[...]
        h3_ref[...] = (h2 * _sigmoid(h2)).astype(bf16)

    hpt = tt // HALO
    tile = lambda cb: pl.BlockSpec((tt, D), lambda i: (i, cb))
    halo = lambda cb: pl.BlockSpec((HALO, D), lambda i: (jnp.maximum(i * hpt - 1, 0), cb))
    vec = pl.BlockSpec((1, D), lambda i: (0, 0))
    return pl.pallas_call(
        body, grid=(T // tt,),
        in_specs=[tile(C_A), tile(C_G), halo(C_A), halo(C_G), pl.BlockSpec((HALO, D), lambda i: (0, 0)), vec, vec, vec],
        out_specs=[pl.BlockSpec((tt, D), lambda i: (i, 0))] * 2,
        out_shape=[SDS((T, D), f32), SDS((T, D), bf16)],
        scratch_shapes=[pltpu.VMEM((HALO + tt, D), f32), pltpu.VMEM((2 * ch, D), f32)],
        name="glu_conv_fwd", compiler_params=_cparams(("parallel",), 32))(proj, proj, proj, proj, w_dw, b_dw, ln_g, ln_b)


def _mix_out(o, h3, proj, x, w_attn_o, w_conv_out, w_out, g_mlp):
    T = x.shape[0]
    tm = 512

    def body(o_ref, h3_ref, ga_ref, gc_ref, x_ref, wa_ref, wc_ref, wo_ref, g_ref,
             attn_ref, conv_ref, mg_ref, x1_ref, n2_ref):
        attn = _dot(o_ref[...], wa_ref[...])
        conv = _dot(h3_ref[...], wc_ref[...])
        attn_ref[...] = attn
        conv_ref[...] = conv
        mg = (_sigmoid(ga_ref[...]) * attn + _sigmoid(gc_ref[...]) * conv).astype(bf16)
        mg_ref[...] = mg
        x1 = x_ref[...] + _dot(mg, wo_ref[...])
        x1_ref[...] = x1
        r = lax.rsqrt(jnp.mean(x1 * x1, axis=-1, keepdims=True) + EPS)
        n2_ref[...] = (x1 * r * g_ref[...]).astype(bf16)

    tile = lambda cb=0: pl.BlockSpec((tm, D), lambda i: (i, cb))
    wfull = pl.BlockSpec((D, D), lambda i: (0, 0))
    return pl.pallas_call(
        body, grid=(T // tm,),
        in_specs=[tile(), tile(), tile(C_GA), tile(C_GC), tile(), wfull, wfull, wfull,
                  pl.BlockSpec((1, D), lambda i: (0, 0))],
        out_specs=[tile()] * 5,
        out_shape=[SDS((T, D), f32), SDS((T, D), f32), SDS((T, D), bf16), SDS((T, D), f32), SDS((T, D), bf16)],
        name="mix_out", compiler_params=_cparams(("parallel",), 48))(o, h3, proj, proj, x, w_attn_o, w_conv_out, w_out, g_mlp)


def _mlp_fwd(n2, w1, w2, x1, tgt):
    T = n2.shape[0]
    tm, tf = 512, 1024
    nk = DFF // tf

    def body(n2_ref, w1_ref, w2_ref, x1_ref, t_ref, hm_ref, dy_ref, dyb_ref, loss_ref, acc):
        i, k = pl.program_id(0), pl.program_id(1)

        @pl.when((i == 0) & (k == 0))
        def _():
            loss_ref[...] = jnp.zeros_like(loss_ref)

        r = jnp.maximum(_dot(n2_ref[...], w1_ref[...]), 0.0)
        hm = (r * r).astype(bf16)
        hm_ref[...] = hm
        part = _dot(hm, w2_ref[...])

        @pl.when(k == 0)
        def _():
            acc[...] = part

        @pl.when(k > 0)
        def _():
            acc[...] += part

        @pl.when(k == nk - 1)
        def _():
            e = x1_ref[...] + acc[...] - t_ref[...]
            dy = e * (1.0 / D)
            dy_ref[...] = dy
            dyb_ref[...] = dy.astype(bf16)
            loss_ref[...] += 0.5 * jnp.sum(jnp.sum(e * e, axis=-1, keepdims=True) * (1.0 / D))

    row = pl.BlockSpec((tm, D), lambda i, k: (i, 0))
    return pl.pallas_call(
        body, grid=(T // tm, nk),
        in_specs=[row, pl.BlockSpec((D, tf), lambda i, k: (0, k)), pl.BlockSpec((tf, D), lambda i, k: (k, 0)), row, row],
        out_specs=[pl.BlockSpec((tm, tf), lambda i, k: (i, k)), row, row, pl.BlockSpec((8, 128), lambda i, k: (0, 0))],
        out_shape=[SDS((T, DFF), bf16), SDS((T, D), f32), SDS((T, D), bf16), SDS((8, 128), f32)],
        scratch_shapes=[pltpu.VMEM((tm, D), f32)],
        name="mlp_fwd", compiler_params=_cparams(("arbitrary", "arbitrary"), 48))(n2, w1, w2, x1, tgt)


def _rms_bwd(xv, g, dn, dres):
    r = lax.rsqrt(jnp.mean(xv * xv, axis=-1, keepdims=True) + EPS)
    gd = dn * g
    dx = dres + r * gd - xv * (r * r * r) * jnp.mean(xv * gd, axis=-1, keepdims=True)
    dg = jnp.sum(dn * xv * r, axis=0, keepdims=True)
    return dx, dg


def _mlp_bwd(dy, dyb, hmid, w1, w2, x1, g_mlp):
    T = dy.shape[0]
    tm, tf = 512, 1024
    nk = DFF // tf

    def body(dy_ref, dyb_ref, hm_ref, w1_ref, w2_ref, x1_ref, g_ref, df_ref, dx_ref, dxb_ref, dg_ref, acc):
        i, k = pl.program_id(0), pl.program_id(1)

        @pl.when((i == 0) & (k == 0))
        def _():
            dg_ref[...] = jnp.zeros_like(dg_ref)

        d_hm = _dot_nt(dyb_ref[...], w2_ref[...])
        df = (d_hm * (2.0 * jnp.sqrt(hm_ref[...].astype(f32)))).astype(bf16)
        df_ref[...] = df
        part = _dot_nt(df, w1_ref[...])

        @pl.when(k == 0)
        def _():
            acc[...] = part

        @pl.when(k > 0)
        def _():
            acc[...] += part

        @pl.when(k == nk - 1)
        def _():
            dx, dg = _rms_bwd(x1_ref[...], g_ref[...], acc[...], dy_ref[...])
            dx_ref[...] = dx
            dxb_ref[...] = dx.astype(bf16)
            dg_ref[...] += dg

    row = pl.BlockSpec((tm, D), lambda i, k: (i, 0))
    vec = pl.BlockSpec((1, D), lambda i, k: (0, 0))
    return pl.pallas_call(
        body, grid=(T // tm, nk),
        in_specs=[row, row, pl.BlockSpec((tm, tf), lambda i, k: (i, k)), pl.BlockSpec((D, tf), lambda i, k: (0, k)),
                  pl.BlockSpec((tf, D), lambda i, k: (k, 0)), row, vec],
        out_specs=[pl.BlockSpec((tm, tf), lambda i, k: (i, k)), row, row, vec],
        out_shape=[SDS((T, DFF), bf16), SDS((T, D), f32), SDS((T, D), bf16), SDS((1, D), f32)],
        scratch_shapes=[pltpu.VMEM((tm, D), f32)],
        name="mlp_bwd", compiler_params=_cparams(("arbitrary", "arbitrary"), 48))(dy, dyb, hmid, w1, w2, x1, g_mlp)


def _wgrad(a, b, name, tn=1024):
    T, M = a.shape
    N = b.shape[1]
    tmm, tk = min(M, 1024), 512

    def body(a_ref, b_ref, o_ref):
        part = _dot_tn(a_ref[...], b_ref[...])

        @pl.when(pl.program_id(2) == 0)
        def _():
            o_ref[...] = part

        @pl.when(pl.program_id(2) > 0)
        def _():
            o_ref[...] += part

    return pl.pallas_call(
        body, grid=(M // tmm, N // tn, T // tk),
        in_specs=[pl.BlockSpec((tk, tmm), lambda m, n, t: (t, m)), pl.BlockSpec((tk, tn), lambda m, n, t: (t, n))],
        out_specs=pl.BlockSpec((tmm, tn), lambda m, n, t: (m, n)),
        out_shape=SDS((M, N), f32),
        name=name, compiler_params=_cparams(("parallel", "parallel", "arbitrary"), 40))(a, b)


def _mix_bwd(dx1b, proj, attn, conv, w_attn_o, w_conv_out, w_out):
    T = dx1b.shape[0]
    tm = 512

    def body(dx_ref, ga_ref, gc_ref, attn_ref, conv_ref, wa_ref, wc_ref, wo_ref,
             dat_ref, dcv_ref, do_ref, dh3_ref, dga_ref, dgc_ref):
        dm = _dot_nt(dx_ref[...], wo_ref[...])
        sa = _sigmoid(ga_ref[...])
        sc = _sigmoid(gc_ref[...])
        dat = (dm * sa).astype(bf16)
        dcv = (dm * sc).astype(bf16)
        dat_ref[...] = dat
        dcv_ref[...] = dcv
        dga_ref[...] = (dm * attn_ref[...] * sa * (1.0 - sa)).astype(bf16)
        dgc_ref[...] = (dm * conv_ref[...] * sc * (1.0 - sc)).astype(bf16)
        do_ref[...] = _dot_nt(dat, wa_ref[...]).astype(bf16)
        dh3_ref[...] = _dot_nt(dcv, wc_ref[...])

    tile = lambda cb=0: pl.BlockSpec((tm, D), lambda i: (i, cb))
    wfull = pl.BlockSpec((D, D), lambda i: (0, 0))
    return pl.pallas_call(
        body, grid=(T // tm,),
        in_specs=[tile(), tile(C_GA), tile(C_GC), tile(), tile(), wfull, wfull, wfull],
        out_specs=[tile()] * 6,
        out_shape=[SDS((T, D), bf16), SDS((T, D), bf16), SDS((T, D), bf16), SDS((T, D), f32),
                   SDS((T, D), bf16), SDS((T, D), bf16)],
        name="mix_bwd", compiler_params=_cparams(("parallel",), 48))(dx1b, proj, proj, attn, conv, w_attn_o, w_conv_out, w_out)


def _conv_ln_bwd(dh3, h1, ln_g, ln_b):
    T = dh3.shape[0]
    tt = 256

    def body(d_ref, h1_ref, lg_ref, lb_ref, dh1_ref, acc_ref):
        @pl.when(pl.program_id(0) == 0)
        def _():
            acc_ref[...] = jnp.zeros_like(acc_ref)

        h1 = h1_ref[...]
        mu = jnp.mean(h1, axis=-1, keepdims=True)
        xc = h1 - mu
        rstd = lax.rsqrt(jnp.mean(xc * xc, axis=-1, keepdims=True) + EPS)
        xh = xc * rstd
        h2 = xh * lg_ref[...] + lb_ref[...]
        sg = _sigmoid(h2)
        dh2 = d_ref[...] * (sg * (1.0 + h2 * (1.0 - sg)))
        dxh = dh2 * lg_ref[...]
        dh1 = rstd * (dxh - jnp.mean(dxh, axis=-1, keepdims=True) - xh * jnp.mean(dxh * xh, axis=-1, keepdims=True))
        dh1_ref[...] = dh1
        acc_ref[0:1, :] += jnp.sum(dh2 * xh, axis=0, keepdims=True)
        acc_ref[1:2, :] += jnp.sum(dh2, axis=0, keepdims=True)
        acc_ref[2:3, :] += jnp.sum(dh1, axis=0, keepdims=True)

    tile = pl.BlockSpec((tt, D), lambda i: (i, 0))
    vec = pl.BlockSpec((1, D), lambda i: (0, 0))
    return pl.pallas_call(
        body, grid=(T // tt,),
        in_specs=[tile, tile, vec, vec],
        out_specs=[tile, pl.BlockSpec((8, D), lambda i: (0, 0))],
        out_shape=[SDS((T, D), f32), SDS((8, D), f32)],
        name="conv_ln_bwd", compiler_params=_cparams(("arbitrary",), 32))(dh3, h1, ln_g, ln_b)


def _conv_bwd(dh1, proj, w_dw):
    T = dh1.shape[0]
    tt, ch = 256, 32
    nt = T // tt

    def body(d_ref, dn_ref, a_ref, g_ref, ah_ref, gh_ref, w_ref, da_ref, dg_ref, gw_ref,
             dslab, hslab, dstage, hstage, dh0, gacc):
        i = pl.program_id(0)

        @pl.when(i == 0)
        def _():
            gacc[...] = jnp.zeros_like(gacc)

        dslab[0:tt, :] = d_ref[...]
        dslab[tt:tt + HALO, :] = jnp.where(i < nt - 1, dn_ref[...], 0.0)
        hslab[0:HALO, :] = jnp.where(i > 0, ah_ref[...] * _sigmoid(gh_ref[...]), 0.0)
        hslab[HALO:HALO + tt, :] = a_ref[...] * _sigmoid(g_ref[...])

        def chunk(c, carry):
            r0 = pl.multiple_of(c * ch, ch)
            dstage[...] = dslab[pl.ds(r0, 2 * ch), :]
            hstage[...] = hslab[pl.ds(r0, 2 * ch), :]
            lo, hi = _conv_taps(dstage, w_ref, ch, lambda j: (CW - 1) - j, lambda ln: jnp.zeros((ch, 512), f32))
            dh0[pl.ds(r0, ch), 0:512] = lo
            dh0[pl.ds(r0, ch), 512:1024] = hi
            for hf in range(2):
                ln = slice(512 * hf, 512 * hf + 512)
                dv = dstage[0:ch, ln]
                for j in range(CW):
                    pr = dv * hstage[pl.ds(HALO - (CW - 1) + j, ch), ln]
                    gacc[8 * j:8 * j + 8, ln] += pr[0:8] + pr[8:16] + pr[16:24] + pr[24:32]
            return carry

        lax.fori_loop(0, tt // ch, chunk, 0)
        a = a_ref[...]
        sg = _sigmoid(g_ref[...])
        d0 = dh0[...]
        da_ref[...] = (d0 * sg).astype(bf16)
        dg_ref[...] = (d0 * a * sg * (1.0 - sg)).astype(bf16)

        @pl.when(i == nt - 1)
        def _():
            gw_ref[...] = jnp.zeros_like(gw_ref)
            for j in range(CW):
                gw_ref[j:j + 1, :] = jnp.sum(gacc[8 * j:8 * j + 8, :], axis=0, keepdims=True)

    hpt = tt // HALO
    tile = lambda cb=0: pl.BlockSpec((tt, D), lambda i: (i, cb))
    halo_prev = lambda cb: pl.BlockSpec((HALO, D), lambda i: (jnp.maximum(i * hpt - 1, 0), cb))
    halo_next = pl.BlockSpec((HALO, D), lambda i: (jnp.minimum((i + 1) * hpt, T // HALO - 1), 0))
    wspec = pl.BlockSpec((HALO, D), lambda i: (0, 0))
    return pl.pallas_call(
        body, grid=(nt,),
        in_specs=[tile(), halo_next, tile(C_A), tile(C_G), halo_prev(C_A), halo_prev(C_G), wspec],
        out_specs=[tile(), tile(), wspec],
        out_shape=[SDS((T, D), bf16), SDS((T, D), bf16), SDS((HALO, D), f32)],
        scratch_shapes=[pltpu.VMEM((tt + HALO, D), f32), pltpu.VMEM((HALO + tt, D), f32),
                        pltpu.VMEM((2 * ch, D), f32), pltpu.VMEM((2 * ch, D), f32),
                        pltpu.VMEM((tt, D), f32), pltpu.VMEM((8 * HALO, D), f32)],
        name="conv_bwd", compiler_params=_cparams(("arbitrary",), 32))(dh1, dh1, proj, proj, proj, proj, w_dw)


def _attn_bwd(qn, kk, vv, bias, sinks, o, do, lse):
    T = qn.shape[0]
    nb = T // BLK

    def body(s_ref, q_ref, kc_ref, kp_ref, vc_ref, vp_ref, b_ref, o_ref, do_ref, lse_ref,
             dq_ref, dkc_ref, dkp_ref, dvc_ref, dvp_ref, dsk_ref, dsa_ref):
        n = pl.program_id(0)

        @pl.when(n == 0)
        def _():
            dsk_ref[...] = jnp.zeros_like(dsk_ref)
            dsa_ref[...] = jnp.zeros_like(dsa_ref)

        @pl.when(n == nb)
        def _():
            dkp_ref[...] = jnp.zeros_like(dkp_ref)
            dvp_ref[...] = jnp.zeros_like(dvp_ref)

        @pl.when(n < nb)
        def _():
            has_prev = n > 0
            lo = _low_head_lanes()
            dups = {"kc": [], "kp": [], "vc": [], "vp": []}
            for h in range(NKV):
                c = 256 * h
                qs = _rows2(q_ref, c)
                dos = _rows2(do_ref, c)
                sc = _dot_nt(qs, _rows2(kc_ref, c))
                sp = _dot_nt(qs, _rows2(kp_ref, c))
                dpc = _dot_nt(dos, _rows2(vc_ref, c))
                dpp = _dot_nt(dos, _rows2(vp_ref, c))
                kstack = jnp.concatenate([kp_ref[:, c:c + 128], kc_ref[:, c:c + 128],
                                          kp_ref[:, c + 128:c + 256], kc_ref[:, c + 128:c + 256]], axis=0)
                p_c, p_p, ds_c, ds_p = [], [], [], []
                for pr in range(2):
                    cc = c + 128 * pr
                    prod = do_ref[:, cc:cc + 128].astype(f32) * o_ref[:, cc:cc + 128].astype(f32)
                    d_lo = jnp.sum(jnp.where(lo, prod, 0.0), axis=-1, keepdims=True)
                    d_hi = jnp.sum(prod, axis=-1, keepdims=True) - d_lo
                    row_pc, row_pp, row_dc, row_dp = [], [], [], []
                    for e in range(2):
                        hq = 4 * h + 2 * pr + e
                        rows, cols = slice(128 * pr, 128 * pr + 128), slice(128 * e, 128 * e + 128)
                        delta = d_lo if e == 0 else d_hi
                        lse = lse_ref[:, hq:hq + 1]
                        pp = jnp.where(has_prev, jnp.exp(sp[rows, cols] + b_ref[hq, :, 0:BLK] - lse), 0.0)
                        pc = jnp.exp(sc[rows, cols] + b_ref[hq, :, BLK:2 * BLK] - lse)
                        dsp = pp * (dpp[rows, cols] - delta)
                        dsc = pc * (dpc[rows, cols] - delta)
                        dsa_ref[hq, :, 0:BLK] += dsp
                        dsa_ref[hq, :, BLK:2 * BLK] += dsc
                        dsk_ref[hq] += jnp.broadcast_to(-jnp.sum(jnp.exp(s_ref[0, hq] - lse) * delta), (8, 128))
                        row_pc.append(pc.astype(bf16))
                        row_pp.append(pp.astype(bf16))
                        row_dc.append(dsc.astype(bf16))
                        row_dp.append(dsp.astype(bf16))
                    dq_ref[:, cc:cc + 128] = _dot(jnp.concatenate([row_dp[0], row_dc[0], row_dp[1], row_dc[1]], axis=1), kstack)
                    p_c.append(jnp.concatenate(row_pc, axis=1))
                    p_p.append(jnp.concatenate(row_pp, axis=1))
                    ds_c.append(jnp.concatenate(row_dc, axis=1))
                    ds_p.append(jnp.concatenate(row_dp, axis=1))

                def to_keys(m2, rhs):
                    x2 = _dot_tn(jnp.concatenate(m2, axis=0), rhs)
                    x = jnp.where(lo, x2[0:128], x2[128:256])
                    return x + pltpu.roll(x, HD, 1)

                dups["kc"].append(to_keys(ds_c, qs))
                dups["kp"].append(to_keys(ds_p, qs))
                dups["vc"].append(to_keys(p_c, dos))
                dups["vp"].append(to_keys(p_p, dos))
            for key, ref in (("kc", dkc_ref), ("kp", dkp_ref), ("vc", dvc_ref), ("vp", dvp_ref)):
                d = dups[key]
                ref[:, 0:128] = jnp.where(lo, d[0], d[1])
                ref[:, 128:256] = jnp.where(lo, d[2], d[3])

    clamp = lambda n: jnp.minimum(n, nb - 1)
    blk = lambda f: pl.BlockSpec((BLK, D), f)
    cur = lambda n: (clamp(n), 0)
    prev = lambda n: (jnp.maximum(clamp(n) - 1, 0), 0)
    back = lambda n: (jnp.maximum(n - 1, 0), 0)
    kvb = lambda f: pl.BlockSpec((BLK, NKV * HD), f)
    return pl.pallas_call(
        body, grid=(nb + 1,),
        in_specs=[pl.BlockSpec(memory_space=pltpu.SMEM), blk(cur), blk(cur), blk(prev), blk(cur), blk(prev),
                  pl.BlockSpec((NQ, BLK, 2 * BLK), lambda n: (0, 0, 0)), blk(cur), blk(cur),
                  pl.BlockSpec((BLK, NQ), cur)],
        out_specs=[blk(cur), kvb(cur), kvb(back), kvb(cur), kvb(back),
                   pl.BlockSpec((NQ, 8, 128), lambda n: (0, 0, 0)),
                   pl.BlockSpec((NQ, BLK, 2 * BLK), lambda n: (0, 0, 0))],
        out_shape=[SDS((T, D), f32)] + [SDS((T, NKV * HD), f32)] * 4 + [SDS((NQ, 8, 128), f32), SDS((NQ, BLK, 2 * BLK), f32)],
        name="attn_bwd", compiler_params=_cparams(("arbitrary",), 40))(sinks, qn, kk, kk, vv, vv, bias, o, do, lse)


def _bias_bwd(dsa):
    def body(bk_ref, ds_ref, out_ref):
        bk = bk_ref[...]
        lane = lax.broadcasted_iota(jnp.int32, (1, 128), 1)
        for h in range(NQ):
            ds = ds_ref[h]
            row = jnp.zeros((1, 128), f32)
            for b in range(NBUCKET):
                row = jnp.where(lane == b, jnp.sum(jnp.where(bk == b, ds, 0.0)), row)
            out_ref[h:h + 1, :] = row

    return pl.pallas_call(body, out_shape=SDS((NQ, 128), f32), name="bias_bwd")(jnp.asarray(_bucket_tile()), dsa)


def _qkv_bwd(proj, gq2, gk2, dqn, dkc, dkp, dvc, dvp):
    T = proj.shape[0]
    tm = 512

    def body(q_ref, kv_ref, gq_ref, gk_ref, dq_ref, dkc_ref, dkp_ref, dvc_ref, dvp_ref,
             oq_ref, okv_ref, ggq_ref, ggk_ref):
        @pl.when(pl.program_id(0) == 0)
        def _():
            ggq_ref[...] = jnp.zeros_like(ggq_ref)
            ggk_ref[...] = jnp.zeros_like(ggk_ref)

        bd = _head_blockdiag()

        def norm_bwd(z, dy, g, scale):
            r = lax.rsqrt(_head_sums(z * z, bd) * (1.0 / HD) + EPS)
            gd = dy * g * scale
            dz = r * gd - z * (r * r * r) * _head_sums(z * gd, bd) * (1.0 / HD)
            return dz, jnp.sum(dy * scale * z * r, axis=0, keepdims=True)

        gq = jnp.zeros((1, 128), f32)
        for p in range(NQ // 2):
            ln = slice(128 * p, 128 * p + 128)
            dz, dg = norm_bwd(q_ref[:, ln], dq_ref[:, ln], gq_ref[...], HD ** -0.5)
            oq_ref[:, ln] = dz.astype(bf16)
            gq = gq + dg
        ggq_ref[...] += gq + pltpu.roll(gq, HD, 1)
        gk = jnp.zeros((1, 128), f32)
        for p in range(NKV // 2):
            ln = slice(128 * p, 128 * p + 128)
            dz, dg = norm_bwd(kv_ref[:, ln], dkc_ref[:, ln] + dkp_ref[:, ln], gk_ref[...], 1.0)
            okv_ref[:, ln] = dz.astype(bf16)
            gk = gk + dg
        ggk_ref[...] += gk + pltpu.roll(gk, HD, 1)
        okv_ref[:, 256:512] = (dvc_ref[...] + dvp_ref[...]).astype(bf16)

    vec = pl.BlockSpec((1, 128), lambda i: (0, 0))
    kvb = pl.BlockSpec((tm, NKV * HD), lambda i: (i, 0))
    return pl.pallas_call(
        body, grid=(T // tm,),
        in_specs=[pl.BlockSpec((tm, D), lambda i: (i, C_Q)), pl.BlockSpec((tm, 512), lambda i: (i, C_KV)), vec, vec,
                  pl.BlockSpec((tm, D), lambda i: (i, 0)), kvb, kvb, kvb, kvb],
        out_specs=[pl.BlockSpec((tm, D), lambda i: (i, 0)), pl.BlockSpec((tm, 512), lambda i: (i, 0)), vec, vec],
        out_shape=[SDS((T, D), bf16), SDS((T, 512), bf16), SDS((1, 128), f32), SDS((1, 128), f32)],
        name="qkv_bwd", compiler_params=_cparams(("arbitrary",), 32))(proj, proj, gq2, gk2, dqn, dkc, dkp, dvc, dvp)


def _inproj_bwd(pieces, w_in, x, dx1, g_mix):
    T = x.shape[0]
    tm = 256
    widths = [p.shape[1] for p in pieces]
    offs = [sum(widths[:i]) for i in range(len(widths))]
    assert sum(widths) == INW

    def body(*refs):
        p_refs, (w_ref, x_ref, dx1_ref, g_ref, dx_ref, dg_ref) = refs[:len(pieces)], refs[len(pieces):]

        @pl.when(pl.program_id(0) == 0)
        def _():
            dg_ref[...] = jnp.zeros_like(dg_ref)

        du = None
        for p_ref, off, wd in zip(p_refs, offs, widths):
            part = _dot_nt(p_ref[...], w_ref[:, off:off + wd])
            du = part if du is None else du + part
        dx, dg = _rms_bwd(x_ref[...], g_ref[...], du, dx1_ref[...])
        dx_ref[...] = dx
        dg_ref[...] += dg

    row = pl.BlockSpec((tm, D), lambda i: (i, 0))
    vec = pl.BlockSpec((1, D), lambda i: (0, 0))
    return pl.pallas_call(
        body, grid=(T // tm,),
        in_specs=[pl.BlockSpec((tm, wd), lambda i: (i, 0)) for wd in widths]
        + [pl.BlockSpec((D, INW), lambda i: (0, 0)), row, row, vec],
        out_specs=[row, vec],
        out_shape=[SDS((T, D), f32), SDS((1, D), f32)],
        name="inproj_bwd", compiler_params=_cparams(("arbitrary",), 48))(*pieces, w_in, x, dx1, g_mix)


def _to_internal_cols(w):
    return jnp.concatenate([w[..., 0:1024], w[..., 1536:INW], w[..., 1024:1536]], axis=-1)


def _local_step(x, tgt, w):
    gq2 = jnp.tile(w["q_norm_g"], (1, 2))
    gk2 = jnp.tile(w["k_norm_g"], (1, 2))
    proj, u = _rms_inproj(x, w["norm_mix_g"], w["w_in"])
    qn, kk, vv = _qk_prep(proj, gq2, gk2)
    bias = _bias_tiles(w["rel_bias"])
    o, lse = _attn_fwd(qn, kk, vv, bias, w["attn_sinks"])
    h1, h3 = _glu_conv_fwd(proj, w["w_dw"], w["b_dw"], w["conv_ln_g"], w["conv_ln_b"])
    attn, conv, merged, x1, n2 = _mix_out(o, h3, proj, x, w["w_attn_o"], w["w_conv_out"], w["w_out"], w["norm_mlp_g"])
    hmid, dy, dyb, loss = _mlp_fwd(n2, w["w_ff1"], w["w_ff2"], x1, tgt)

    g = {}
    df1, dx1, dx1b, g["norm_mlp_g"] = _mlp_bwd(dy, dyb, hmid, w["w_ff1"], w["w_ff2"], x1, w["norm_mlp_g"])
    g["w_ff2"] = _wgrad(hmid, dyb, "wgrad_ff2")
    g["w_ff1"] = _wgrad(n2, df1, "wgrad_ff1")
    dat, dcv, do, dh3, dga, dgc = _mix_bwd(dx1b, proj, attn, conv, w["w_attn_o"], w["w_conv_out"], w["w_out"])
    g["w_out"] = _wgrad(merged, dx1b, "wgrad_out")
    g["w_attn_o"] = _wgrad(o, dat, "wgrad_attn_o")
    g["w_conv_out"] = _wgrad(h3, dcv, "wgrad_conv_out")
    dh1, lnacc = _conv_ln_bwd(dh3, h1, w["conv_ln_g"], w["conv_ln_b"])
    g["conv_ln_g"], g["conv_ln_b"], g["b_dw"] = lnacc[0:1], lnacc[1:2], lnacc[2:3]
    da, dg, gw_dw = _conv_bwd(dh1, proj, w["w_dw"])
    g["w_dw"] = gw_dw
    dqn, dkc, dkp, dvc, dvp, dsk, dsa = _attn_bwd(qn, kk, vv, bias, w["attn_sinks"], o, do, lse)
    g["attn_sinks"] = dsk[:, 0, 0].reshape(1, NQ)
    g["rel_bias"] = _bias_bwd(dsa)[:, 0:NBUCKET].T
    dq, dkv, ggq, ggk = _qkv_bwd(proj, gq2, gk2, dqn, dkc, dkp, dvc, dvp)
    g["q_norm_g"], g["k_norm_g"] = ggq[:, 0:HD], ggk[:, 0:HD]
    pieces = [dq, da, dg, dga, dgc, dkv]
    names = ["q", "a", "g", "ga", "gc", "kv"]
    gw = {nm: _wgrad(u, p, "wgrad_in_" + nm, tn=p.shape[1] if p.shape[1] < 1024 else 1024) for nm, p in zip(names, pieces)}
    g["w_in"] = jnp.concatenate([gw["q"], gw["kv"], gw["a"], gw["g"], gw["ga"], gw["gc"]], axis=1)
    grad_x, g["norm_mix_g"] = _inproj_bwd(pieces, w["w_in"], x, dx1, w["norm_mix_g"])
    return loss[0, 0], grad_x, g


ANY = pl.BlockSpec(memory_space=pl.ANY)
BIG = ["w_in", "w_attn_o", "w_conv_out", "w_out", "w_ff1", "w_ff2"]
SHARD_AXIS = {"w_in": 1, "w_attn_o": 0, "w_conv_out": 0, "w_out": 0, "w_ff1": 1, "w_ff2": 0, "w_dw": 1}
SHARD_SHAPE = {"w_in": (D, INW // 4), "w_attn_o": (D // 4, D), "w_conv_out": (D // 4, D), "w_out": (D // 4, D),
               "w_ff1": (D, DFF // 4), "w_ff2": (DFF // 4, D), "w_dw": (HALO, D // 4)}


def _position():
    x, y, c = lax.axis_index("x"), lax.axis_index("y"), lax.axis_index("c")
    other_chips = [(1 - x, y), (x, 1 - y), (1 - x, 1 - y)]
    return x, y, c, 2 * x + y, other_chips


def _shard_window(name, full_ref, s, half=None):
    R, C = SHARD_SHAPE[name]
    r0, nr = (0, R) if half is None else (half * (R // 2), R // 2)
    if SHARD_AXIS[name] == 1:
        return full_ref.at[pl.ds(r0, nr), pl.ds(s * C, C)]
    return full_ref.at[pl.ds(s * R + r0, nr), :]


def _remote(src, dst, send_sems, recv_sems, k, device):
    return pltpu.make_async_remote_copy(src_ref=src, dst_ref=dst, send_sem=send_sems.at[k], recv_sem=recv_sems.at[k],
                                        device_id=device, device_id_type=MESH)


def _gather_weights(shards):
    names = list(shards)
    n = len(names)

    def body(*refs):
        srcs = dict(zip(names, refs[:n]))
        dsts = dict(zip(names, refs[2 * n:3 * n]))
        send_sems, recv_sems = refs[3 * n:]
        x, y, c, chip, chips = _position()
        sibling = (x, y, 1 - c)
        sent = []
        for a, nm in enumerate(names):
            R = SHARD_SHAPE[nm][0]
            mine = srcs[nm].at[pl.ds(c * (R // 2), R // 2), :]
            for j, (cx, cy) in enumerate(chips):
                cp = _remote(mine, _shard_window(nm, dsts[nm], chip, c), send_sems, recv_sems, 6 * a + j, (cx, cy, c))
                cp.start()
                sent.append(cp)
        for a, nm in enumerate(names):
            for j, (cx, cy) in enumerate(chips):
                w = _shard_window(nm, dsts[nm], 2 * cx + cy, c)
                _remote(w, w, send_sems, recv_sems, 6 * a + j, (cx, cy, c)).wait_recv()
                cp = _remote(w, w, send_sems, recv_sems, 6 * a + 3 + j, sibling)
                cp.start()
                sent.append(cp)
        for a, nm in enumerate(names):
            for j, (cx, cy) in enumerate(chips):
                w = _shard_window(nm, dsts[nm], 2 * cx + cy, 1 - c)
                _remote(w, w, send_sems, recv_sems, 6 * a + 3 + j, sibling).wait_recv()
        for cp in sent:
            cp.wait_send()

    def full_shape(nm):
        R, C = SHARD_SHAPE[nm]
        return (R, 4 * C) if SHARD_AXIS[nm] == 1 else (4 * R, C)

    def prefilled(nm, chip):
        R, C = SHARD_SHAPE[nm]
        start = (0, chip * C) if SHARD_AXIS[nm] == 1 else (chip * R, 0)
        return lax.dynamic_update_slice(jnp.zeros(full_shape(nm), shards[nm].dtype), shards[nm], start)

    chip = 2 * lax.axis_index("x") + lax.axis_index("y")
    out = pl.pallas_call(
        body, in_specs=[ANY] * (2 * n), out_specs=[ANY] * n,
        out_shape=[SDS(full_shape(nm), shards[nm].dtype) for nm in names],
        input_output_aliases={n + a: a for a in range(n)},
        scratch_shapes=[pltpu.SemaphoreType.DMA((6 * n,)), pltpu.SemaphoreType.DMA((6 * n,))],
        name="gather_weights")(*[shards[nm] for nm in names], *[prefilled(nm, chip) for nm in names])
    return dict(zip(names, out))


def _half_rows(nm):
    return SHARD_SHAPE[nm][0] // 2


RS_TILE = 128


def _pair_exchange(grads):
    def body(*refs):
        g = dict(zip(BIG, refs[:6]))
        got = dict(zip(BIG, refs[6:12]))
        send_sems, recv_sems = refs[12:]
        x, y, c, chip, chips = _position()
        cps = []
        for a, nm in enumerate(BIG):
            for s in range(4):
                cps.append(_remote(_shard_window(nm, g[nm], s, 1 - c), got[nm].at[s], send_sems, recv_sems, 4 * a + s,
                                   (x, y, 1 - c)))
        for cp in cps:
            cp.start()
        for cp in cps:
            cp.wait()

    out = pl.pallas_call(
        body, in_specs=[ANY] * 6, out_specs=[ANY] * 6,
        out_shape=[SDS((4, _half_rows(nm), SHARD_SHAPE[nm][1]), f32) for nm in BIG],
        scratch_shapes=[pltpu.SemaphoreType.DMA((24,)), pltpu.SemaphoreType.DMA((24,))],
        name="rs_pair_exchange")(*[grads[nm] for nm in BIG])
    return dict(zip(BIG, out))


def _pair_sum(nm, g, got, core):
    R, C = SHARD_SHAPE[nm]
    hr = R // 2
    nt = hr // RS_TILE
    if SHARD_AXIS[nm] == 1:
        g_map = lambda s, i, c_ref: (c_ref[0] * nt + i, s)
    else:
        g_map = lambda s, i, c_ref: (s * (R // RS_TILE) + c_ref[0] * nt + i, 0)

    def body(c_ref, g_ref, got_ref, o_ref):
        o_ref[0] = g_ref[...] + got_ref[0]

    blk3 = pl.BlockSpec((1, RS_TILE, C), lambda s, i, c_ref: (s, i, 0))
    return pl.pallas_call(
        body,
        grid_spec=pltpu.PrefetchScalarGridSpec(
            num_scalar_prefetch=1, grid=(4, nt),
            in_specs=[pl.BlockSpec((RS_TILE, C), g_map), blk3], out_specs=blk3),
        out_shape=SDS((4, hr, C), f32), name="rs_pair_sum_" + nm,
        compiler_params=_cparams(("parallel", "parallel"), 32))(core, g, got)


def _chip_exchange(cp):
    def body(*refs):
        src = dict(zip(BIG, refs[:6]))
        dst = dict(zip(BIG, refs[6:12]))
        send_sems, recv_sems = refs[12:]
        x, y, c, chip, chips = _position()
        cps = []
        for a, nm in enumerate(BIG):
            for j, (cx, cy) in enumerate(chips):
                cps.append(_remote(src[nm].at[2 * cx + cy], dst[nm].at[j], send_sems, recv_sems, 3 * a + j, (cx, cy, c)))
        for cp_ in cps:
            cp_.start()
        for cp_ in cps:
            cp_.wait()

    out = pl.pallas_call(
        body, in_specs=[ANY] * 6, out_specs=[ANY] * 6,
        out_shape=[SDS((3, _half_rows(nm), SHARD_SHAPE[nm][1]), f32) for nm in BIG],
        scratch_shapes=[pltpu.SemaphoreType.DMA((18,)), pltpu.SemaphoreType.DMA((18,))],
        name="rs_chip_exchange")(*[cp[nm] for nm in BIG])
    return dict(zip(BIG, out))


def _chip_sum(nm, cp, rc, chip_core):
    R, C = SHARD_SHAPE[nm]
    nt = (R // 2) // RS_TILE

    def body(sc_ref, cp_ref, rc_ref, o_ref):
        o_ref[...] = cp_ref[0] + rc_ref[0] + rc_ref[1] + rc_ref[2]

    return pl.pallas_call(
        body,
        grid_spec=pltpu.PrefetchScalarGridSpec(
            num_scalar_prefetch=1, grid=(nt,),
            in_specs=[pl.BlockSpec((1, RS_TILE, C), lambda i, sc: (sc[0], i, 0)),
                      pl.BlockSpec((3, RS_TILE, C), lambda i, sc: (0, i, 0))],
            out_specs=pl.BlockSpec((RS_TILE, C), lambda i, sc: (sc[1] * nt + i, 0))),
        out_shape=SDS((R, C), f32), name="rs_chip_sum_" + nm,
        compiler_params=_cparams(("parallel",), 32))(chip_core, cp, rc)


def _pair_share(tot):
    def body(*refs):
        outs = dict(zip(BIG, refs[6:12]))
        send_sems, recv_sems = refs[12:]
        x, y, c, chip, chips = _position()
        cps = []
        for a, nm in enumerate(BIG):
            hr = _half_rows(nm)
            mine = outs[nm].at[pl.ds(c * hr, hr), :]
            cps.append(_remote(mine, mine, send_sems, recv_sems, a, (x, y, 1 - c)))
        for cp in cps:
            cp.start()
        for cp in cps:
            cp.wait()

    out = pl.pallas_call(
        body, in_specs=[ANY] * 6, out_specs=[ANY] * 6, out_shape=[SDS(SHARD_SHAPE[nm], f32) for nm in BIG],
        input_output_aliases={a: a for a in range(6)},
        scratch_shapes=[pltpu.SemaphoreType.DMA((6,)), pltpu.SemaphoreType.DMA((6,))],
        name="rs_pair_share")(*[tot[nm] for nm in BIG])
    return dict(zip(BIG, out))


def _reduce_scatter(grads, chip):
    core = lax.axis_index("c")
    core_arr = jnp.reshape(core, (1,)).astype(jnp.int32)
    chip_core = jnp.stack([chip, core]).astype(jnp.int32)
    got = _pair_exchange(grads)
    cp = {nm: _pair_sum(nm, grads[nm], got[nm], core_arr) for nm in BIG}
    rc = _chip_exchange(cp)
    tot = {nm: _chip_sum(nm, cp[nm], rc[nm], chip_core) for nm in BIG}
    return _pair_share(tot)


SMALL_ROWS = 40


def _allreduce_small(block):
    def body(x_ref, out_ref, buf, send_sems, recv_sems, local_sem):
        x, y, c, chip, chips = _position()
        me, sibling = (x, y, c), (x, y, 1 - c)

        def slot(px, py, pc):
            return buf.at[4 * px + 2 * py + pc]

        def copy(k, block_of, to, src=None):
            return _remote(slot(*block_of) if src is None else src, slot(*block_of), send_sems, recv_sems, k, to)

        mine = pltpu.make_async_copy(x_ref, slot(*me), local_sem)
        mine.start()
        first = [copy(0, me, sibling, src=x_ref)] + [copy(1 + j, me, (*ch, c), src=x_ref) for j, ch in enumerate(chips)]
        for cp in first:
            cp.start()
        passed = [copy(4 + j, (*ch, c), sibling) for j, ch in enumerate(chips)]
        for j, ch in enumerate(chips):
            copy(1 + j, (*ch, c), me).wait_recv()
            passed[j].start()
        copy(0, sibling, me).wait_recv()
        for j, ch in enumerate(chips):
            copy(4 + j, (*ch, 1 - c), me).wait_recv()
        for cp in first + passed:
            cp.wait_send()
        mine.wait()
        acc = buf[0]
        for d in range(1, 8):
            acc = acc + buf[d]
        out_ref[...] = acc

    vm = pl.BlockSpec(memory_space=pltpu.VMEM)
    return pl.pallas_call(
        body, in_specs=[vm], out_specs=vm, out_shape=SDS((SMALL_ROWS, D), f32),
        scratch_shapes=[pltpu.VMEM((8, SMALL_ROWS, D), f32), pltpu.SemaphoreType.DMA((7,)), pltpu.SemaphoreType.DMA((7,)),
                        pltpu.SemaphoreType.DMA],
        name="allreduce_small")(block)


def _adamw(w, g, m, v, name):
    rows, cols = w.shape
    tr = 256 if rows % 256 == 0 else rows

    def body(w_ref, g_ref, m_ref, v_ref, d_ref, nm_ref, nv_ref):
        gv = g_ref[...]
        m2 = ADAM_B1 * m_ref[...] + (1.0 - ADAM_B1) * gv
        v2 = ADAM_B2 * v_ref[...] + (1.0 - ADAM_B2) * jnp.square(gv)
        m_hat = m2 / (1.0 - ADAM_B1 ** ADAM_STEP)
        v_hat = v2 / (1.0 - ADAM_B2 ** ADAM_STEP)
        d_ref[...] = -ADAM_LR * (m_hat / (jnp.sqrt(v_hat) + ADAM_EPS) + ADAM_WD * w_ref[...])
        nm_ref[...] = m2
        nv_ref[...] = v2

    spec = pl.BlockSpec((tr, cols), lambda i: (i, 0))
    return pl.pallas_call(body, grid=(rows // tr,), in_specs=[spec] * 4, out_specs=[spec] * 3,
                          out_shape=[SDS((rows, cols), f32)] * 3, name=name,
                          compiler_params=_cparams(("parallel",), 40))(w, g, m, v)


WEIGHTS = ["norm_mix_g", "w_in", "q_norm_g", "k_norm_g", "attn_sinks", "rel_bias", "w_attn_o", "w_dw", "b_dw",
           "conv_ln_g", "conv_ln_b", "w_conv_out", "w_out", "norm_mlp_g", "w_ff1", "w_ff2"]
ROW_VECS = ["norm_mix_g", "b_dw", "conv_ln_g", "conv_ln_b", "norm_mlp_g"]
MISC_ROW = 5
W_DW_ROW = 8


def _pack_small(vals, loss=None):
    misc = [vals["q_norm_g"].reshape(1, HD), vals["k_norm_g"].reshape(1, HD), vals["attn_sinks"].reshape(1, NQ),
            jnp.zeros((1, 1), f32) if loss is None else loss.reshape(1, 1), jnp.zeros((1, 111), f32),
            vals["rel_bias"].reshape(1, NBUCKET * NQ), jnp.zeros((1, 256), f32)]
    rows = [vals[nm].reshape(1, D) for nm in ROW_VECS] + [jnp.concatenate(misc, axis=1), jnp.zeros((2, D), f32)]
    return jnp.concatenate(rows, axis=0)


def _unpack_small(block):
    out = {nm: block[i:i + 1] for i, nm in enumerate(ROW_VECS)}
    misc = block[MISC_ROW]
    out["q_norm_g"] = misc[0:64].reshape(1, HD)
    out["k_norm_g"] = misc[64:128].reshape(1, HD)
    out["attn_sinks"] = misc[128:144].reshape(1, NQ)
    out["rel_bias"] = misc[256:768].reshape(NBUCKET, NQ)
    return out, misc[144]


def kernel(x, norm_mix_g, w_in, q_norm_g, k_norm_g, attn_sinks, rel_bias, w_attn_o, w_dw, b_dw, conv_ln_g, conv_ln_b, w_conv_out, w_out, norm_mlp_g, w_ff1, w_ff2, loss_target, m_norm_mix_g, m_w_in, m_q_norm_g, m_k_norm_g, m_attn_sinks, m_rel_bias, m_w_attn_o, m_w_dw, m_b_dw, m_conv_ln_g, m_conv_ln_b, m_w_conv_out, m_w_out, m_norm_mlp_g, m_w_ff1, m_w_ff2, v_norm_mix_g, v_w_in, v_q_norm_g, v_k_norm_g, v_attn_sinks, v_rel_bias, v_w_attn_o, v_w_dw, v_b_dw, v_conv_ln_g, v_conv_ln_b, v_w_conv_out, v_w_out, v_norm_mlp_g, v_w_ff1, v_w_ff2):
    args = dict(locals())
    wts = {nm: args[nm] for nm in WEIGHTS}
    mom = {nm: args["m_" + nm] for nm in WEIGHTS}
    var = {nm: args["v_" + nm] for nm in WEIGHTS}
    chip = 2 * lax.axis_index("x") + lax.axis_index("y")

    shards = {nm: wts[nm][0].astype(bf16) for nm in BIG}
    shards["w_dw"] = jnp.pad(w_dw[0], ((0, 1), (0, 0)))
    full = _gather_weights(shards)
    full["w_in"] = _to_internal_cols(full["w_in"])
    for nm in WEIGHTS:
        if nm not in full:
            full[nm] = wts[nm]

    loss_part, grad_x, g = _local_step(x[0], loss_target[0], full)

    small = jnp.concatenate([_pack_small(g, loss_part), g["w_dw"]], axis=0)
    small = _allreduce_small(small)
    grads, loss = _unpack_small(small)
    grads["w_dw"] = lax.dynamic_slice(small[W_DW_ROW:W_DW_ROW + CW], (0, chip * (D // 4)), (CW, D // 4))
    grads.update(_reduce_scatter(g, chip))

    delta, new_m, new_v = {}, {}, {}
    sd, sm, sv = _adamw(_pack_small(wts), small[0:8], _pack_small(mom), _pack_small(var), "adamw_small")
    for res, blk in ((delta, sd), (new_m, sm), (new_v, sv)):
        res.update(_unpack_small(blk)[0])
    for nm in BIG + ["w_dw"]:
        shp = wts[nm].shape
        two_d = lambda a: a.reshape(shp[-2], shp[-1])
        delta[nm], new_m[nm], new_v[nm] = _adamw(two_d(wts[nm]), grads[nm], two_d(mom[nm]), two_d(var[nm]), "adamw_" + nm)

    def shaped(vals):
        return [vals[nm].reshape(wts[nm].shape) for nm in WEIGHTS]

    return (loss, grad_x[None], *shaped(grads), *shaped(delta), *shaped(new_m), *shaped(new_v))
```

```python
import functools

import numpy as np
import jax
import jax.numpy as jnp
from jax import lax
from jax.experimental import pallas as pl
from jax.experimental.pallas import tpu as pltpu

f32 = jnp.float32
bf16 = jnp.bfloat16
SDS = jax.ShapeDtypeStruct
MESH = pl.DeviceIdType.MESH

D = 1024
HD = 64
NQ = 16
NKV = 4
BLK = 128
CW = 31
HALO = 32
DFF = 4096
NBUCKET = 32
EPS = 1e-6
NEG = -1e30
INW = 5632
C_Q, C_A, C_G, C_GA, C_GC = 0, 1, 2, 3, 4
C_KV = 10

ADAM_LR = 0.001
ADAM_B1 = 0.9
ADAM_B2 = 0.999
ADAM_EPS = 1e-08
ADAM_WD = 0.01
ADAM_STEP = 10

VMEM_BYTES_V7X = 64 << 20


def _cparams(sem, vmem_mb):
    assert (vmem_mb << 20) < VMEM_BYTES_V7X
    return pltpu.CompilerParams(dimension_semantics=sem, vmem_limit_bytes=vmem_mb << 20)


def _dot(a, b):
    return jnp.dot(a, b, preferred_element_type=f32)


def _dot_nt(a, b):
    return lax.dot_general(a, b, (((1,), (1,)), ((), ())), preferred_element_type=f32)


def _dot_tn(a, b):
    return lax.dot_general(a, b, (((0,), (0,)), ((), ())), preferred_element_type=f32)


def _sigmoid(x):
    return 1.0 / (1.0 + jnp.exp(-x))


def _low_head_lanes():
    return lax.broadcasted_iota(jnp.int32, (1, 2 * HD), 1) < HD


def _head_blockdiag():
    r = lax.broadcasted_iota(jnp.int32, (2 * HD, 2 * HD), 0) // HD
    c = lax.broadcasted_iota(jnp.int32, (2 * HD, 2 * HD), 1) // HD
    return jnp.where(r == c, 1.0, 0.0).astype(bf16)


def _head_sums(z, bd):
    hi = z.astype(bf16)
    lo = (z - hi.astype(f32)).astype(bf16)
    return _dot(hi, bd) + _dot(lo, bd)


def _rms_inproj(x, g, w):
    T, N = x.shape[0], w.shape[1]
    tm, tn = 512, 512

    def body(x_ref, g_ref, w_ref, p_ref, u_ref):
        @pl.when(pl.program_id(1) == 0)
        def _():
            xv = x_ref[...]
            r = lax.rsqrt(jnp.mean(xv * xv, axis=-1, keepdims=True) + EPS)
            u_ref[...] = (xv * r * g_ref[...]).astype(bf16)
        p_ref[...] = _dot(u_ref[...], w_ref[...])

    return pl.pallas_call(
        body, grid=(T // tm, N // tn),
        in_specs=[pl.BlockSpec((tm, D), lambda i, j: (i, 0)),
                  pl.BlockSpec((1, D), lambda i, j: (0, 0)),
                  pl.BlockSpec((D, tn), lambda i, j: (0, j))],
        out_specs=[pl.BlockSpec((tm, tn), lambda i, j: (i, j)),
                   pl.BlockSpec((tm, D), lambda i, j: (i, 0))],
        out_shape=[SDS((T, N), f32), SDS((T, D), bf16)],
        name="rms_inproj", compiler_params=_cparams(("parallel", "arbitrary"), 32))(x, g, w)


def _split_pair(pair, out_ref, p, lo):
    rolled = pltpu.roll(pair, HD, 1)
    zero = jnp.zeros_like(pair)
    c = 512 * p
    out_ref[:, c:c + 128] = jnp.where(lo, pair, zero).astype(bf16)
    out_ref[:, c + 128:c + 256] = jnp.where(lo, zero, rolled).astype(bf16)
    out_ref[:, c + 256:c + 384] = jnp.where(lo, rolled, zero).astype(bf16)
    out_ref[:, c + 384:c + 512] = jnp.where(lo, zero, pair).astype(bf16)


def _qk_prep(proj, gq2, gk2):
    T = proj.shape[0]
    tm = 512

    def body(q_ref, kv_ref, gq_ref, gk_ref, qn_ref, kk_ref, vv_ref):
        bd = _head_blockdiag()
        lo = _low_head_lanes()
        for p in range(NQ // 2):
            z = q_ref[:, 128 * p:128 * p + 128]
            r = lax.rsqrt(_head_sums(z * z, bd) * (1.0 / HD) + EPS)
            qn_ref[:, 128 * p:128 * p + 128] = (z * r * gq_ref[...] * (HD ** -0.5)).astype(bf16)
        for p in range(NKV // 2):
            z = kv_ref[:, 128 * p:128 * p + 128]
            r = lax.rsqrt(_head_sums(z * z, bd) * (1.0 / HD) + EPS)
            _split_pair(z * r * gk_ref[...], kk_ref, p, lo)
            _split_pair(kv_ref[:, 256 + 128 * p:256 + 128 * p + 128], vv_ref, p, lo)

    return pl.pallas_call(
        body, grid=(T // tm,),
        in_specs=[pl.BlockSpec((tm, D), lambda i: (i, C_Q)),
                  pl.BlockSpec((tm, 512), lambda i: (i, C_KV)),
                  pl.BlockSpec((1, 128), lambda i: (0, 0)),
                  pl.BlockSpec((1, 128), lambda i: (0, 0))],
        out_specs=[pl.BlockSpec((tm, D), lambda i: (i, 0))] * 3,
        out_shape=[SDS((T, D), bf16)] * 3,
        name="qk_prep", compiler_params=_cparams(("parallel",), 32))(proj, proj, gq2, gk2)


def _bucket_tile():
    qi = np.arange(BLK)[:, None]
    kj = np.arange(2 * BLK)[None, :]
    dist = qi + BLK - kj
    n = np.maximum(dist, 0)
    max_exact = NBUCKET // 2
    nf = np.maximum(n, 1).astype(np.float32)
    large = max_exact + (np.log(nf / max_exact) / np.float32(np.log(128 / max_exact))
                         * (NBUCKET - max_exact)).astype(np.int32)
    large = np.minimum(large, NBUCKET - 1)
    bucket = np.where(n < max_exact, n, large)
    valid = (dist >= 0) & (dist < BLK)
    return np.where(valid, bucket, -1).astype(np.int32)


def _bias_tiles(rel_bias):
    def body(rb_ref, bk_ref, out_ref):
        bk = bk_ref[...]
        for h in range(NQ):
            acc = jnp.full((BLK, 2 * BLK), NEG, f32)
            for b in range(NBUCKET):
                acc = jnp.where(bk == b, rb_ref[b, h], acc)
            out_ref[h] = acc

    return pl.pallas_call(
        body,
        in_specs=[pl.BlockSpec(memory_space=pltpu.SMEM), pl.BlockSpec(memory_space=pltpu.VMEM)],
        out_specs=pl.BlockSpec(memory_space=pltpu.VMEM),
        out_shape=SDS((NQ, BLK, 2 * BLK), f32),
        name="bias_tiles")(rel_bias, jnp.asarray(_bucket_tile()))


def _rows2(ref, c):
    return jnp.concatenate([ref[:, c:c + 128], ref[:, c + 128:c + 256]], axis=0)


def _attn_fwd(qn, kk, vv, bias, sinks):
    T = qn.shape[0]
    nb = T // BLK

    def body(s_ref, q_ref, kc_ref, kp_ref, vc_ref, vp_ref, b_ref, o_ref, lse_ref):
        has_prev = pl.program_id(0) > 0
        for h in range(NKV):
            c = 256 * h
            qs = _rows2(q_ref, c)
            sc = _dot_nt(qs, _rows2(kc_ref, c))
            sp = _dot_nt(qs, _rows2(kp_ref, c))
            vstack = jnp.concatenate([vp_ref[:, c:c + 128], vc_ref[:, c:c + 128],
                                      vp_ref[:, c + 128:c + 256], vc_ref[:, c + 128:c + 256]], axis=0)
            for pr in range(2):
                ps = []
                for e in range(2):
                    hq = 4 * h + 2 * pr + e
                    rows, cols = slice(128 * pr, 128 * pr + 128), slice(128 * e, 128 * e + 128)
                    s_p = jnp.where(has_prev, sp[rows, cols] + b_ref[hq, :, 0:BLK], NEG)
                    s_c = sc[rows, cols] + b_ref[hq, :, BLK:2 * BLK]
                    sink = s_ref[0, hq]
                    m = jnp.maximum(jnp.maximum(jnp.max(s_p, axis=-1, keepdims=True),
                                                jnp.max(s_c, axis=-1, keepdims=True)), sink)
                    e_p = jnp.exp(s_p - m)
                    e_c = jnp.exp(s_c - m)
                    l = (jnp.sum(e_p, axis=-1, keepdims=True) + jnp.sum(e_c, axis=-1, keepdims=True)
                         + jnp.exp(sink - m))
                    inv = 1.0 / l
                    ps += [(e_p * inv).astype(bf16), (e_c * inv).astype(bf16)]
                    lse_ref[:, hq:hq + 1] = m + jnp.log(l)
                o_ref[:, c + 128 * pr:c + 128 * pr + 128] = _dot(jnp.concatenate(ps, axis=1), vstack).astype(bf16)

    blk = lambda f: pl.BlockSpec((BLK, D), f)
    cur = lambda n: (n, 0)
    prev = lambda n: (jnp.maximum(n - 1, 0), 0)
    return pl.pallas_call(
        body, grid=(nb,),
        in_specs=[pl.BlockSpec(memory_space=pltpu.SMEM), blk(cur), blk(cur), blk(prev), blk(cur), blk(prev),
                  pl.BlockSpec((NQ, BLK, 2 * BLK), lambda n: (0, 0, 0))],
        out_specs=[blk(cur), pl.BlockSpec((BLK, NQ), cur)],
        out_shape=[SDS((T, D), bf16), SDS((T, NQ), f32)],
        name="attn_fwd", compiler_params=_cparams(("parallel",), 32))(sinks, qn, kk, kk, vv, vv, bias)


def _conv_taps(stage_ref, w_ref, rows, offset_of_tap, init):
    halves = []
    for hf in range(2):
        ln = slice(512 * hf, 512 * hf + 512)
        acc = init(ln)
        for j in range(CW):
            acc = acc + stage_ref[pl.ds(offset_of_tap(j), rows), ln] * w_ref[j:j + 1, ln]
        halves.append(acc)
    return halves


def _glu_conv_fwd(proj, w_dw, b_dw, ln_g, ln_b):
    T = proj.shape[0]
    tt, ch = 256, 32

    def body(a_ref, g_ref, ah_ref, gh_ref, w_ref, b_ref, lg_ref, lb_ref, h1_ref, h3_ref, slab, stage):
        i = pl.program_id(0)
        halo = ah_ref[...] * _sigmoid(gh_ref[...])
        slab[0:HALO, :] = jnp.where(i > 0, halo, 0.0)
        slab[HALO:HALO + tt, :] = a_ref[...] * _sigmoid(g_ref[...])

        def chunk(c, carry):
            r0 = pl.multiple_of(c * ch, ch)
            stage[...] = slab[pl.ds(r0, 2 * ch), :]
            lo, hi = _conv_taps(stage, w_ref, ch, lambda j: HALO - (CW - 1) + j,
                                lambda ln: jnp.broadcast_to(b_ref[:, ln], (ch, 512)))
            h1_ref[pl.ds(r0, ch), 0:512] = lo
            h1_ref[pl.ds(r0, ch), 512:1024] = hi
            return carry

        lax.fori_loop(0, tt // ch, chunk, 0)
        h1 = h1_ref[...]
        mu = jnp.mean(h1, axis=-1, keepdims=True)
        xc = h1 - mu
        var = jnp.mean(xc * xc, axis=-1, keepdims=True)
        h2 = xc * lax.rsqrt(var + EPS) * lg_ref[...] + lb_ref[...]
        h3_ref[...] = (h2 * _sigmoid(h2)).astype(bf16)

    hpt = tt // HALO
    tile = lambda cb: pl.BlockSpec((tt, D), lambda i: (i, cb))
    halo = lambda cb: pl.BlockSpec((HALO, D), lambda i: (jnp.maximum(i * hpt - 1, 0), cb))
    vec = pl.BlockSpec((1, D), lambda i: (0, 0))
    return pl.pallas_call(
        body, grid=(T // tt,),
        in_specs=[tile(C_A), tile(C_G), halo(C_A), halo(C_G), pl.BlockSpec((HALO, D), lambda i: (0, 0)), vec, vec, vec],
        out_specs=[pl.BlockSpec((tt, D), lambda i: (i, 0))] * 2,
        out_shape=[SDS((T, D), f32), SDS((T, D), bf16)],
        scratch_shapes=[pltpu.VMEM((HALO + tt, D), f32), pltpu.VMEM((2 * ch, D), f32)],
        name="glu_conv_fwd", compiler_params=_cparams(("parallel",), 32))(proj, proj, proj, proj, w_dw, b_dw, ln_g, ln_b)


def _mix_out(o, h3, proj, x, w_attn_o, w_conv_out, w_out, g_mlp):
    T = x.shape[0]
    tm = 512

    def body(o_ref, h3_ref, ga_ref, gc_ref, x_ref, wa_ref, wc_ref, wo_ref, g_ref,
             attn_ref, conv_ref, mg_ref, x1_ref, n2_ref):
        attn = _dot(o_ref[...], wa_ref[...])
        conv = _dot(h3_ref[...], wc_ref[...])
        attn_ref[...] = attn
        conv_ref[...] = conv
        mg = (_sigmoid(ga_ref[...]) * attn + _sigmoid(gc_ref[...]) * conv).astype(bf16)
        mg_ref[...] = mg
        x1 = x_ref[...] + _dot(mg, wo_ref[...])
        x1_ref[...] = x1
        r = lax.rsqrt(jnp.mean(x1 * x1, axis=-1, keepdims=True) + EPS)
        n2_ref[...] = (x1 * r * g_ref[...]).astype(bf16)

    tile = lambda cb=0: pl.BlockSpec((tm, D), lambda i: (i, cb))
    wfull = pl.BlockSpec((D, D), lambda i: (0, 0))
    return pl.pallas_call(
        body, grid=(T // tm,),
        in_specs=[tile(), tile(), tile(C_GA), tile(C_GC), tile(), wfull, wfull, wfull,
                  pl.BlockSpec((1, D), lambda i: (0, 0))],
        out_specs=[tile()] * 5,
        out_shape=[SDS((T, D), f32), SDS((T, D), f32), SDS((T, D), bf16), SDS((T, D), f32), SDS((T, D), bf16)],
        name="mix_out", compiler_params=_cparams(("parallel",), 48))(o, h3, proj, proj, x, w_attn_o, w_conv_out, w_out, g_mlp)


def _mlp_fwd(n2, w1, w2, x1, tgt):
    T = n2.shape[0]
    tm, tf = 512, 1024
    nk = DFF // tf

    def body(n2_ref, w1_ref, w2_ref, x1_ref, t_ref, hm_ref, dy_ref, dyb_ref, loss_ref, acc):
        i, k = pl.program_id(0), pl.program_id(1)

        @pl.when((i == 0) & (k == 0))
        def _():
            loss_ref[...] = jnp.zeros_like(loss_ref)

        r = jnp.maximum(_dot(n2_ref[...], w1_ref[...]), 0.0)
        hm = (r * r).astype(bf16)
        hm_ref[...] = hm
        part = _dot(hm, w2_ref[...])

        @pl.when(k == 0)
        def _():
            acc[...] = part

        @pl.when(k > 0)
        def _():
            acc[...] += part

        @pl.when(k == nk - 1)
        def _():
            e = x1_ref[...] + acc[...] - t_ref[...]
            dy = e * (1.0 / D)
            dy_ref[...] = dy
            dyb_ref[...] = dy.astype(bf16)
            loss_ref[...] += 0.5 * jnp.sum(jnp.sum(e * e, axis=-1, keepdims=True) * (1.0 / D))

    row = pl.BlockSpec((tm, D), lambda i, k: (i, 0))
    return pl.pallas_call(
        body, grid=(T // tm, nk),
        in_specs=[row, pl.BlockSpec((D, tf), lambda i, k: (0, k)), pl.BlockSpec((tf, D), lambda i, k: (k, 0)), row, row],
        out_specs=[pl.BlockSpec((tm, tf), lambda i, k: (i, k)), row, row, pl.BlockSpec((8, 128), lambda i, k: (0, 0))],
        out_shape=[SDS((T, DFF), bf16), SDS((T, D), f32), SDS((T, D), bf16), SDS((8, 128), f32)],
        scratch_shapes=[pltpu.VMEM((tm, D), f32)],
        name="mlp_fwd", compiler_params=_cparams(("arbitrary", "arbitrary"), 48))(n2, w1, w2, x1, tgt)


def _rms_bwd(xv, g, dn, dres):
    r = lax.rsqrt(jnp.mean(xv * xv, axis=-1, keepdims=True) + EPS)
    gd = dn * g
    dx = dres + r * gd - xv * (r * r * r) * jnp.mean(xv * gd, axis=-1, keepdims=True)
    dg = jnp.sum(dn * xv * r, axis=0, keepdims=True)
    return dx, dg


def _mlp_bwd(dy, dyb, hmid, w1, w2, x1, g_mlp):
    T = dy.shape[0]
    tm, tf = 512, 1024
    nk = DFF // tf

    def body(dy_ref, dyb_ref, hm_ref, w1_ref, w2_ref, x1_ref, g_ref, df_ref, dx_ref, dxb_ref, dg_ref, acc):
        i, k = pl.program_id(0), pl.program_id(1)

        @pl.when((i == 0) & (k == 0))
        def _():
            dg_ref[...] = jnp.zeros_like(dg_ref)

        d_hm = _dot_nt(dyb_ref[...], w2_ref[...])
        df = (d_hm * (2.0 * jnp.sqrt(hm_ref[...].astype(f32)))).astype(bf16)
        df_ref[...] = df
        part = _dot_nt(df, w1_ref[...])

        @pl.when(k == 0)
        def _():
            acc[...] = part

        @pl.when(k > 0)
        def _():
            acc[...] += part

        @pl.when(k == nk - 1)
        def _():
            dx, dg = _rms_bwd(x1_ref[...], g_ref[...], acc[...], dy_ref[...])
            dx_ref[...] = dx
            dxb_ref[...] = dx.astype(bf16)
            dg_ref[...] += dg

    row = pl.BlockSpec((tm, D), lambda i, k: (i, 0))
    vec = pl.BlockSpec((1, D), lambda i, k: (0, 0))
    return pl.pallas_call(
        body, grid=(T // tm, nk),
        in_specs=[row, row, pl.BlockSpec((tm, tf), lambda i, k: (i, k)), pl.BlockSpec((D, tf), lambda i, k: (0, k)),
                  pl.BlockSpec((tf, D), lambda i, k: (k, 0)), row, vec],
        out_specs=[pl.BlockSpec((tm, tf), lambda i, k: (i, k)), row, row, vec],
        out_shape=[SDS((T, DFF), bf16), SDS((T, D), f32), SDS((T, D), bf16), SDS((1, D), f32)],
        scratch_shapes=[pltpu.VMEM((tm, D), f32)],
        name="mlp_bwd", compiler_params=_cparams(("arbitrary", "arbitrary"), 48))(dy, dyb, hmid, w1, w2, x1, g_mlp)


def _wgrad(a, b, name, tn=1024):
    T, M = a.shape
    N = b.shape[1]
    tmm, tk = min(M, 1024), 512

    def body(a_ref, b_ref, o_ref):
        part = _dot_tn(a_ref[...], b_ref[...])

        @pl.when(pl.program_id(2) == 0)
        def _():
            o_ref[...] = part

        @pl.when(pl.program_id(2) > 0)
        def _():
            o_ref[...] += part

    return pl.pallas_call(
        body, grid=(M // tmm, N // tn, T // tk),
        in_specs=[pl.BlockSpec((tk, tmm), lambda m, n, t: (t, m)), pl.BlockSpec((tk, tn), lambda m, n, t: (t, n))],
        out_specs=pl.BlockSpec((tmm, tn), lambda m, n, t: (m, n)),
        out_shape=SDS((M, N), f32),
        name=name, compiler_params=_cparams(("parallel", "parallel", "arbitrary"), 40))(a, b)


def _mix_bwd(dx1b, proj, attn, conv, w_attn_o, w_conv_out, w_out):
    T = dx1b.shape[0]
    tm = 512

    def body(dx_ref, ga_ref, gc_ref, attn_ref, conv_ref, wa_ref, wc_ref, wo_ref,
             dat_ref, dcv_ref, do_ref, dh3_ref, dga_ref, dgc_ref):
        dm = _dot_nt(dx_ref[...], wo_ref[...])
        sa = _sigmoid(ga_ref[...])
        sc = _sigmoid(gc_ref[...])
        dat = (dm * sa).astype(bf16)
        dcv = (dm * sc).astype(bf16)
        dat_ref[...] = dat
        dcv_ref[...] = dcv
        dga_ref[...] = (dm * attn_ref[...] * sa * (1.0 - sa)).astype(bf16)
        dgc_ref[...] = (dm * conv_ref[...] * sc * (1.0 - sc)).astype(bf16)
        do_ref[...] = _dot_nt(dat, wa_ref[...]).astype(bf16)
        dh3_ref[...] = _dot_nt(dcv, wc_ref[...])

    tile = lambda cb=0: pl.BlockSpec((tm, D), lambda i: (i, cb))
    wfull = pl.BlockSpec((D, D), lambda i: (0, 0))
    return pl.pallas_call(
        body, grid=(T // tm,),
        in_specs=[tile(), tile(C_GA), tile(C_GC), tile(), tile(), wfull, wfull, wfull],
        out_specs=[tile()] * 6,
        out_shape=[SDS((T, D), bf16), SDS((T, D), bf16), SDS((T, D), bf16), SDS((T, D), f32),
                   SDS((T, D), bf16), SDS((T, D), bf16)],
        name="mix_bwd", compiler_params=_cparams(("parallel",), 48))(dx1b, proj, proj, attn, conv, w_attn_o, w_conv_out, w_out)


def _conv_ln_bwd(dh3, h1, ln_g, ln_b):
    T = dh3.shape[0]
    tt = 256

    def body(d_ref, h1_ref, lg_ref, lb_ref, dh1_ref, acc_ref):
        @pl.when(pl.program_id(0) == 0)
        def _():
            acc_ref[...] = jnp.zeros_like(acc_ref)

        h1 = h1_ref[...]
        mu = jnp.mean(h1, axis=-1, keepdims=True)
        xc = h1 - mu
        rstd = lax.rsqrt(jnp.mean(xc * xc, axis=-1, keepdims=True) + EPS)
        xh = xc * rstd
        h2 = xh * lg_ref[...] + lb_ref[...]
        sg = _sigmoid(h2)
        dh2 = d_ref[...] * (sg * (1.0 + h2 * (1.0 - sg)))
        dxh = dh2 * lg_ref[...]
        dh1 = rstd * (dxh - jnp.mean(dxh, axis=-1, keepdims=True) - xh * jnp.mean(dxh * xh, axis=-1, keepdims=True))
        dh1_ref[...] = dh1
        acc_ref[0:1, :] += jnp.sum(dh2 * xh, axis=0, keepdims=True)
        acc_ref[1:2, :] += jnp.sum(dh2, axis=0, keepdims=True)
        acc_ref[2:3, :] += jnp.sum(dh1, axis=0, keepdims=True)

    tile = pl.BlockSpec((tt, D), lambda i: (i, 0))
    vec = pl.BlockSpec((1, D), lambda i: (0, 0))
    return pl.pallas_call(
        body, grid=(T // tt,),
        in_specs=[tile, tile, vec, vec],
        out_specs=[tile, pl.BlockSpec((8, D), lambda i: (0, 0))],
        out_shape=[SDS((T, D), f32), SDS((8, D), f32)],
        name="conv_ln_bwd", compiler_params=_cparams(("arbitrary",), 32))(dh3, h1, ln_g, ln_b)


def _conv_bwd(dh1, proj, w_dw):
    T = dh1.shape[0]
    tt, ch = 256, 32
    nt = T // tt

    def body(d_ref, dn_ref, a_ref, g_ref, ah_ref, gh_ref, w_ref, da_ref, dg_ref, gw_ref,
             dslab, hslab, dstage, hstage, dh0, gacc):
        i = pl.program_id(0)

        @pl.when(i == 0)
        def _():
            gacc[...] = jnp.zeros_like(gacc)

        dslab[0:tt, :] = d_ref[...]
        dslab[tt:tt + HALO, :] = jnp.where(i < nt - 1, dn_ref[...], 0.0)
        hslab[0:HALO, :] = jnp.where(i > 0, ah_ref[...] * _sigmoid(gh_ref[...]), 0.0)
        hslab[HALO:HALO + tt, :] = a_ref[...] * _sigmoid(g_ref[...])

        def chunk(c, carry):
            r0 = pl.multiple_of(c * ch, ch)
            dstage[...] = dslab[pl.ds(r0, 2 * ch), :]
            hstage[...] = hslab[pl.ds(r0, 2 * ch), :]
            lo, hi = _conv_taps(dstage, w_ref, ch, lambda j: (CW - 1) - j, lambda ln: jnp.zeros((ch, 512), f32))
            dh0[pl.ds(r0, ch), 0:512] = lo
            dh0[pl.ds(r0, ch), 512:1024] = hi
            for hf in range(2):
                ln = slice(512 * hf, 512 * hf + 512)
                dv = dstage[0:ch, ln]
                for j in range(CW):
                    pr = dv * hstage[pl.ds(HALO - (CW - 1) + j, ch), ln]
                    gacc[8 * j:8 * j + 8, ln] += pr[0:8] + pr[8:16] + pr[16:24] + pr[24:32]
            return carry

        lax.fori_loop(0, tt // ch, chunk, 0)
        a = a_ref[...]
        sg = _sigmoid(g_ref[...])
        d0 = dh0[...]
        da_ref[...] = (d0 * sg).astype(bf16)
        dg_ref[...] = (d0 * a * sg * (1.0 - sg)).astype(bf16)

        @pl.when(i == nt - 1)
        def _():
            gw_ref[...] = jnp.zeros_like(gw_ref)
            for j in range(CW):
                gw_ref[j:j + 1, :] = jnp.sum(gacc[8 * j:8 * j + 8, :], axis=0, keepdims=True)

    hpt = tt // HALO
    tile = lambda cb=0: pl.BlockSpec((tt, D), lambda i: (i, cb))
    halo_prev = lambda cb: pl.BlockSpec((HALO, D), lambda i: (jnp.maximum(i * hpt - 1, 0), cb))
    halo_next = pl.BlockSpec((HALO, D), lambda i: (jnp.minimum((i + 1) * hpt, T // HALO - 1), 0))
    wspec = pl.BlockSpec((HALO, D), lambda i: (0, 0))
    return pl.pallas_call(
        body, grid=(nt,),
        in_specs=[tile(), halo_next, tile(C_A), tile(C_G), halo_prev(C_A), halo_prev(C_G), wspec],
        out_specs=[tile(), tile(), wspec],
        out_shape=[SDS((T, D), bf16), SDS((T, D), bf16), SDS((HALO, D), f32)],
        scratch_shapes=[pltpu.VMEM((tt + HALO, D), f32), pltpu.VMEM((HALO + tt, D), f32),
                        pltpu.VMEM((2 * ch, D), f32), pltpu.VMEM((2 * ch, D), f32),
                        pltpu.VMEM((tt, D), f32), pltpu.VMEM((8 * HALO, D), f32)],
        name="conv_bwd", compiler_params=_cparams(("arbitrary",), 32))(dh1, dh1, proj, proj, proj, proj, w_dw)


def _attn_bwd(qn, kk, vv, bias, sinks, o, do, lse):
    T = qn.shape[0]
    nb = T // BLK

    def body(s_ref, q_ref, kc_ref, kp_ref, vc_ref, vp_ref, b_ref, o_ref, do_ref, lse_ref,
             dq_ref, dkc_ref, dkp_ref, dvc_ref, dvp_ref, dsk_ref, dsa_ref):
        n = pl.program_id(0)

        @pl.when(n == 0)
        def _():
            dsk_ref[...] = jnp.zeros_like(dsk_ref)
            dsa_ref[...] = jnp.zeros_like(dsa_ref)

        @pl.when(n == nb)
        def _():
            dkp_ref[...] = jnp.zeros_like(dkp_ref)
            dvp_ref[...] = jnp.zeros_like(dvp_ref)

        @pl.when(n < nb)
        def _():
            has_prev = n > 0
            lo = _low_head_lanes()
            dups = {"kc": [], "kp": [], "vc": [], "vp": []}
            for h in range(NKV):
                c = 256 * h
                qs = _rows2(q_ref, c)
                dos = _rows2(do_ref, c)
                sc = _dot_nt(qs, _rows2(kc_ref, c))
                sp = _dot_nt(qs, _rows2(kp_ref, c))
                dpc = _dot_nt(dos, _rows2(vc_ref, c))
                dpp = _dot_nt(dos, _rows2(vp_ref, c))
                kstack = jnp.concatenate([kp_ref[:, c:c + 128], kc_ref[:, c:c + 128],
                                          kp_ref[:, c + 128:c + 256], kc_ref[:, c + 128:c + 256]], axis=0)
                p_c, p_p, ds_c, ds_p = [], [], [], []
                for pr in range(2):
                    cc = c + 128 * pr
                    prod = do_ref[:, cc:cc + 128].astype(f32) * o_ref[:, cc:cc + 128].astype(f32)
                    d_lo = jnp.sum(jnp.where(lo, prod, 0.0), axis=-1, keepdims=True)
                    d_hi = jnp.sum(prod, axis=-1, keepdims=True) - d_lo
                    row_pc, row_pp, row_dc, row_dp = [], [], [], []
                    for e in range(2):
                        hq = 4 * h + 2 * pr + e
                        rows, cols = slice(128 * pr, 128 * pr + 128), slice(128 * e, 128 * e + 128)
                        delta = d_lo if e == 0 else d_hi
                        lse = lse_ref[:, hq:hq + 1]
                        pp = jnp.where(has_prev, jnp.exp(sp[rows, cols] + b_ref[hq, :, 0:BLK] - lse), 0.0)
                        pc = jnp.exp(sc[rows, cols] + b_ref[hq, :, BLK:2 * BLK] - lse)
                        dsp = pp * (dpp[rows, cols] - delta)
                        dsc = pc * (dpc[rows, cols] - delta)
                        dsa_ref[hq, :, 0:BLK] += dsp
                        dsa_ref[hq, :, BLK:2 * BLK] += dsc
                        dsk_ref[hq] += jnp.broadcast_to(-jnp.sum(jnp.exp(s_ref[0, hq] - lse) * delta), (8, 128))
                        row_pc.append(pc.astype(bf16))
                        row_pp.append(pp.astype(bf16))
                        row_dc.append(dsc.astype(bf16))
                        row_dp.append(dsp.astype(bf16))
                    dq_ref[:, cc:cc + 128] = _dot(jnp.concatenate([row_dp[0], row_dc[0], row_dp[1], row_dc[1]], axis=1), kstack)
                    p_c.append(jnp.concatenate(row_pc, axis=1))
                    p_p.append(jnp.concatenate(row_pp, axis=1))
                    ds_c.append(jnp.concatenate(row_dc, axis=1))
                    ds_p.append(jnp.concatenate(row_dp, axis=1))

                def to_keys(m2, rhs):
                    x2 = _dot_tn(jnp.concatenate(m2, axis=0), rhs)
                    x = jnp.where(lo, x2[0:128], x2[128:256])
                    return x + pltpu.roll(x, HD, 1)

                dups["kc"].append(to_keys(ds_c, qs))
                dups["kp"].append(to_keys(ds_p, qs))
                dups["vc"].append(to_keys(p_c, dos))
                dups["vp"].append(to_keys(p_p, dos))
            for key, ref in (("kc", dkc_ref), ("kp", dkp_ref), ("vc", dvc_ref), ("vp", dvp_ref)):
                d = dups[key]
                ref[:, 0:128] = jnp.where(lo, d[0], d[1])
                ref[:, 128:256] = jnp.where(lo, d[2], d[3])

    clamp = lambda n: jnp.minimum(n, nb - 1)
    blk = lambda f: pl.BlockSpec((BLK, D), f)
    cur = lambda n: (clamp(n), 0)
    prev = lambda n: (jnp.maximum(clamp(n) - 1, 0), 0)
    back = lambda n: (jnp.maximum(n - 1, 0), 0)
    kvb = lambda f: pl.BlockSpec((BLK, NKV * HD), f)
    return pl.pallas_call(
        body, grid=(nb + 1,),
        in_specs=[pl.BlockSpec(memory_space=pltpu.SMEM), blk(cur), blk(cur), blk(prev), blk(cur), blk(prev),
                  pl.BlockSpec((NQ, BLK, 2 * BLK), lambda n: (0, 0, 0)), blk(cur), blk(cur),
                  pl.BlockSpec((BLK, NQ), cur)],
        out_specs=[blk(cur), kvb(cur), kvb(back), kvb(cur), kvb(back),
                   pl.BlockSpec((NQ, 8, 128), lambda n: (0, 0, 0)),
                   pl.BlockSpec((NQ, BLK, 2 * BLK), lambda n: (0, 0, 0))],
        out_shape=[SDS((T, D), f32)] + [SDS((T, NKV * HD), f32)] * 4 + [SDS((NQ, 8, 128), f32), SDS((NQ, BLK, 2 * BLK), f32)],
        name="attn_bwd", compiler_params=_cparams(("arbitrary",), 40))(sinks, qn, kk, kk, vv, vv, bias, o, do, lse)


def _bias_bwd(dsa):
    def body(bk_ref, ds_ref, out_ref):
        bk = bk_ref[...]
        lane = lax.broadcasted_iota(jnp.int32, (1, 128), 1)
        for h in range(NQ):
            ds = ds_ref[h]
            row = jnp.zeros((1, 128), f32)
            for b in range(NBUCKET):
                row = jnp.where(lane == b, jnp.sum(jnp.where(bk == b, ds, 0.0)), row)
            out_ref[h:h + 1, :] = row

    return pl.pallas_call(body, out_shape=SDS((NQ, 128), f32), name="bias_bwd")(jnp.asarray(_bucket_tile()), dsa)


def _qkv_bwd(proj, gq2, gk2, dqn, dkc, dkp, dvc, dvp):
    T = proj.shape[0]
    tm = 512

    def body(q_ref, kv_ref, gq_ref, gk_ref, dq_ref, dkc_ref, dkp_ref, dvc_ref, dvp_ref,
             oq_ref, okv_ref, ggq_ref, ggk_ref):
        @pl.when(pl.program_id(0) == 0)
        def _():
            ggq_ref[...] = jnp.zeros_like(ggq_ref)
            ggk_ref[...] = jnp.zeros_like(ggk_ref)

        bd = _head_blockdiag()

        def norm_bwd(z, dy, g, scale):
            r = lax.rsqrt(_head_sums(z * z, bd) * (1.0 / HD) + EPS)
            gd = dy * g * scale
            dz = r * gd - z * (r * r * r) * _head_sums(z * gd, bd) * (1.0 / HD)
            return dz, jnp.sum(dy * scale * z * r, axis=0, keepdims=True)

        gq = jnp.zeros((1, 128), f32)
        for p in range(NQ // 2):
            ln = slice(128 * p, 128 * p + 128)
            dz, dg = norm_bwd(q_ref[:, ln], dq_ref[:, ln], gq_ref[...], HD ** -0.5)
            oq_ref[:, ln] = dz.astype(bf16)
            gq = gq + dg
        ggq_ref[...] += gq + pltpu.roll(gq, HD, 1)
        gk = jnp.zeros((1, 128), f32)
        for p in range(NKV // 2):
            ln = slice(128 * p, 128 * p + 128)
            dz, dg = norm_bwd(kv_ref[:, ln], dkc_ref[:, ln] + dkp_ref[:, ln], gk_ref[...], 1.0)
            okv_ref[:, ln] = dz.astype(bf16)
            gk = gk + dg
        ggk_ref[...] += gk + pltpu.roll(gk, HD, 1)
        okv_ref[:, 256:512] = (dvc_ref[...] + dvp_ref[...]).astype(bf16)

    vec = pl.BlockSpec((1, 128), lambda i: (0, 0))
    kvb = pl.BlockSpec((tm, NKV * HD), lambda i: (i, 0))
    return pl.pallas_call(
        body, grid=(T // tm,),
        in_specs=[pl.BlockSpec((tm, D), lambda i: (i, C_Q)), pl.BlockSpec((tm, 512), lambda i: (i, C_KV)), vec, vec,
                  pl.BlockSpec((tm, D), lambda i: (i, 0)), kvb, kvb, kvb, kvb],
        out_specs=[pl.BlockSpec((tm, D), lambda i: (i, 0)), pl.BlockSpec((tm, 512), lambda i: (i, 0)), vec, vec],
        out_shape=[SDS((T, D), bf16), SDS((T, 512), bf16), SDS((1, 128), f32), SDS((1, 128), f32)],
        name="qkv_bwd", compiler_params=_cparams(("arbitrary",), 32))(proj, proj, gq2, gk2, dqn, dkc, dkp, dvc, dvp)


def _inproj_bwd(pieces, w_in, x, dx1, g_mix):
    T = x.shape[0]
    tm = 256
    widths = [p.shape[1] for p in pieces]
    offs = [sum(widths[:i]) for i in range(len(widths))]
    assert sum(widths) == INW

    def body(*refs):
        p_refs, (w_ref, x_ref, dx1_ref, g_ref, dx_ref, dg_ref) = refs[:len(pieces)], refs[len(pieces):]

        @pl.when(pl.program_id(0) == 0)
        def _():
            dg_ref[...] = jnp.zeros_like(dg_ref)

        du = None
        for p_ref, off, wd in zip(p_refs, offs, widths):
            part = _dot_nt(p_ref[...], w_ref[:, off:off + wd])
            du = part if du is None else du + part
        dx, dg = _rms_bwd(x_ref[...], g_ref[...], du, dx1_ref[...])
        dx_ref[...] = dx
        dg_ref[...] += dg

    row = pl.BlockSpec((tm, D), lambda i: (i, 0))
    vec = pl.BlockSpec((1, D), lambda i: (0, 0))
    return pl.pallas_call(
        body, grid=(T // tm,),
        in_specs=[pl.BlockSpec((tm, wd), lambda i: (i, 0)) for wd in widths]
        + [pl.BlockSpec((D, INW), lambda i: (0, 0)), row, row, vec],
        out_specs=[row, vec],
        out_shape=[SDS((T, D), f32), SDS((1, D), f32)],
        name="inproj_bwd", compiler_params=_cparams(("arbitrary",), 48))(*pieces, w_in, x, dx1, g_mix)


def _to_internal_cols(w):
    return jnp.concatenate([w[..., 0:1024], w[..., 1536:INW], w[..., 1024:1536]], axis=-1)


def _local_step(x, tgt, w):
    gq2 = jnp.tile(w["q_norm_g"], (1, 2))
    gk2 = jnp.tile(w["k_norm_g"], (1, 2))
    proj, u = _rms_inproj(x, w["norm_mix_g"], w["w_in"])
    qn, kk, vv = _qk_prep(proj, gq2, gk2)
    bias = _bias_tiles(w["rel_bias"])
    o, lse = _attn_fwd(qn, kk, vv, bias, w["attn_sinks"])
    h1, h3 = _glu_conv_fwd(proj, w["w_dw"], w["b_dw"], w["conv_ln_g"], w["conv_ln_b"])
    attn, conv, merged, x1, n2 = _mix_out(o, h3, proj, x, w["w_attn_o"], w["w_conv_out"], w["w_out"], w["norm_mlp_g"])
    hmid, dy, dyb, loss = _mlp_fwd(n2, w["w_ff1"], w["w_ff2"], x1, tgt)

    g = {}
    df1, dx1, dx1b, g["norm_mlp_g"] = _mlp_bwd(dy, dyb, hmid, w["w_ff1"], w["w_ff2"], x1, w["norm_mlp_g"])
    g["w_ff2"] = _wgrad(hmid, dyb, "wgrad_ff2")
    g["w_ff1"] = _wgrad(n2, df1, "wgrad_ff1")
    dat, dcv, do, dh3, dga, dgc = _mix_bwd(dx1b, proj, attn, conv, w["w_attn_o"], w["w_conv_out"], w["w_out"])
    g["w_out"] = _wgrad(merged, dx1b, "wgrad_out")
    g["w_attn_o"] = _wgrad(o, dat, "wgrad_attn_o")
    g["w_conv_out"] = _wgrad(h3, dcv, "wgrad_conv_out")
    dh1, lnacc = _conv_ln_bwd(dh3, h1, w["conv_ln_g"], w["conv_ln_b"])
    g["conv_ln_g"], g["conv_ln_b"], g["b_dw"] = lnacc[0:1], lnacc[1:2], lnacc[2:3]
    da, dg, gw_dw = _conv_bwd(dh1, proj, w["w_dw"])
    g["w_dw"] = gw_dw
    dqn, dkc, dkp, dvc, dvp, dsk, dsa = _attn_bwd(qn, kk, vv, bias, w["attn_sinks"], o, do, lse)
    g["attn_sinks"] = dsk[:, 0, 0].reshape(1, NQ)
    g["rel_bias"] = _bias_bwd(dsa)[:, 0:NBUCKET].T
    dq, dkv, ggq, ggk = _qkv_bwd(proj, gq2, gk2, dqn, dkc, dkp, dvc, dvp)
    g["q_norm_g"], g["k_norm_g"] = ggq[:, 0:HD], ggk[:, 0:HD]
    pieces = [dq, da, dg, dga, dgc, dkv]
    names = ["q", "a", "g", "ga", "gc", "kv"]
    gw = {nm: _wgrad(u, p, "wgrad_in_" + nm, tn=p.shape[1] if p.shape[1] < 1024 else 1024) for nm, p in zip(names, pieces)}
    g["w_in"] = jnp.concatenate([gw["q"], gw["kv"], gw["a"], gw["g"], gw["ga"], gw["gc"]], axis=1)
    grad_x, g["norm_mix_g"] = _inproj_bwd(pieces, w["w_in"], x, dx1, w["norm_mix_g"])
    return loss[0, 0], grad_x, g


ANY = pl.BlockSpec(memory_space=pl.ANY)
BIG = ["w_in", "w_attn_o", "w_conv_out", "w_out", "w_ff1", "w_ff2"]
SHARD_AXIS = {"w_in": 1, "w_attn_o": 0, "w_conv_out": 0, "w_out": 0, "w_ff1": 1, "w_ff2": 0, "w_dw": 1}
SHARD_SHAPE = {"w_in": (D, INW // 4), "w_attn_o": (D // 4, D), "w_conv_out": (D // 4, D), "w_out": (D // 4, D),
               "w_ff1": (D, DFF // 4), "w_ff2": (DFF // 4, D), "w_dw": (HALO, D // 4)}


def _position():
    x, y, c = lax.axis_index("x"), lax.axis_index("y"), lax.axis_index("c")
    other_chips = [(1 - x, y), (x, 1 - y), (1 - x, 1 - y)]
    return x, y, c, 2 * x + y, other_chips


def _shard_window(name, full_ref, s, half=None):
    R, C = SHARD_SHAPE[name]
    r0, nr = (0, R) if half is None else (half * (R // 2), R // 2)
    if SHARD_AXIS[name] == 1:
        return full_ref.at[pl.ds(r0, nr), pl.ds(s * C, C)]
    return full_ref.at[pl.ds(s * R + r0, nr), :]


def _remote(src, dst, send_sems, recv_sems, k, device):
    return pltpu.make_async_remote_copy(src_ref=src, dst_ref=dst, send_sem=send_sems.at[k], recv_sem=recv_sems.at[k],
                                        device_id=device, device_id_type=MESH)


def _full_shape(nm):
    R, C = SHARD_SHAPE[nm]
    return (R, 4 * C) if SHARD_AXIS[nm] == 1 else (4 * R, C)


def _place_shard(nm, shard, chip_arr, dtype):
    R, C = SHARD_SHAPE[nm]
    tr = min(R, 256)
    if SHARD_AXIS[nm] == 1:
        o_map = lambda i, ch: (i, ch[0])
    else:
        o_map = lambda i, ch: (ch[0] * (R // tr) + i, 0)

    def body(ch_ref, s_ref, o_ref):
        o_ref[...] = s_ref[...].astype(dtype)

    return pl.pallas_call(
        body,
        grid_spec=pltpu.PrefetchScalarGridSpec(
            num_scalar_prefetch=1, grid=(R // tr,),
            in_specs=[pl.BlockSpec((tr, C), lambda i, ch: (i, 0))], out_specs=pl.BlockSpec((tr, C), o_map)),
        out_shape=SDS(_full_shape(nm), dtype), name="place_" + nm,
        compiler_params=_cparams(("parallel",), 32))(chip_arr, shard)


def _gather_weights(placed):
    names = list(placed)
    n = len(names)

    def body(*refs):
        dsts = dict(zip(names, refs[n:2 * n]))
        send_sems, recv_sems = refs[2 * n:]
        x, y, c, chip, chips = _position()
        sibling = (x, y, 1 - c)
        sent = []
        for a, nm in enumerate(names):
            mine = _shard_window(nm, dsts[nm], chip, c)
            for j, (cx, cy) in enumerate(chips):
                cp = _remote(mine, mine, send_sems, recv_sems, 6 * a + j, (cx, cy, c))
                cp.start()
                sent.append(cp)
        for a, nm in enumerate(names):
            for j, (cx, cy) in enumerate(chips):
                w = _shard_window(nm, dsts[nm], 2 * cx + cy, c)
                _remote(w, w, send_sems, recv_sems, 6 * a + j, (cx, cy, c)).wait_recv()
                cp = _remote(w, w, send_sems, recv_sems, 6 * a + 3 + j, sibling)
                cp.start()
                sent.append(cp)
        for a, nm in enumerate(names):
            for j, (cx, cy) in enumerate(chips):
                w = _shard_window(nm, dsts[nm], 2 * cx + cy, 1 - c)
                _remote(w, w, send_sems, recv_sems, 6 * a + 3 + j, sibling).wait_recv()
        for cp in sent:
            cp.wait_send()

    out = pl.pallas_call(
        body, in_specs=[ANY] * n, out_specs=[ANY] * n,
        out_shape=[SDS(placed[nm].shape, placed[nm].dtype) for nm in names],
        input_output_aliases={a: a for a in range(n)},
        scratch_shapes=[pltpu.SemaphoreType.DMA((6 * n,)), pltpu.SemaphoreType.DMA((6 * n,))],
        name="gather_weights")(*[placed[nm] for nm in names])
    return dict(zip(names, out))


def _half_rows(nm):
    return SHARD_SHAPE[nm][0] // 2


RS_TILE = 128


def _pair_exchange(grads):
    def body(*refs):
        g = dict(zip(BIG, refs[:6]))
        got = dict(zip(BIG, refs[6:12]))
        send_sems, recv_sems = refs[12:]
        x, y, c, chip, chips = _position()
        cps = []
        for a, nm in enumerate(BIG):
            for s in range(4):
                cps.append(_remote(_shard_window(nm, g[nm], s, 1 - c), got[nm].at[s], send_sems, recv_sems, 4 * a + s,
                                   (x, y, 1 - c)))
        for cp in cps:
            cp.start()
        for cp in cps:
            cp.wait()

    out = pl.pallas_call(
        body, in_specs=[ANY] * 6, out_specs=[ANY] * 6,
        out_shape=[SDS((4, _half_rows(nm), SHARD_SHAPE[nm][1]), f32) for nm in BIG],
        scratch_shapes=[pltpu.SemaphoreType.DMA((24,)), pltpu.SemaphoreType.DMA((24,))],
        name="rs_pair_exchange")(*[grads[nm] for nm in BIG])
    return dict(zip(BIG, out))


def _pair_sum(nm, g, got, chip_core):
    R, C = SHARD_SHAPE[nm]
    hr = R // 2
    nt = hr // RS_TILE
    if SHARD_AXIS[nm] == 1:
        g_map = lambda i, s, sc: (sc[1] * nt + i, s)
    else:
        g_map = lambda i, s, sc: (s * (R // RS_TILE) + sc[1] * nt + i, 0)

    def body(sc_ref, g_ref, got_ref, o16_ref, own_ref):
        v = g_ref[...] + got_ref[0]
        o16_ref[0] = v.astype(bf16)

        @pl.when(pl.program_id(1) == sc_ref[0])
        def _():
            own_ref[...] = v

    blk3 = pl.BlockSpec((1, RS_TILE, C), lambda i, s, sc: (s, i, 0))
    return pl.pallas_call(
        body,
        grid_spec=pltpu.PrefetchScalarGridSpec(
            num_scalar_prefetch=1, grid=(nt, 4),
            in_specs=[pl.BlockSpec((RS_TILE, C), g_map), blk3],
            out_specs=[blk3, pl.BlockSpec((RS_TILE, C), lambda i, s, sc: (i, 0))]),
        out_shape=[SDS((4, hr, C), bf16), SDS((hr, C), f32)], name="rs_pair_sum_" + nm,
        compiler_params=_cparams(("parallel", "arbitrary"), 32))(chip_core, g, got)


def _chip_exchange(cp):
    def body(*refs):
        src = dict(zip(BIG, refs[:6]))
        dst = dict(zip(BIG, refs[6:12]))
        send_sems, recv_sems = refs[12:]
        x, y, c, chip, chips = _position()
        cps = []
        for a, nm in enumerate(BIG):
            for j, (cx, cy) in enumerate(chips):
                cps.append(_remote(src[nm].at[2 * cx + cy], dst[nm].at[j], send_sems, recv_sems, 3 * a + j, (cx, cy, c)))
        for cp_ in cps:
            cp_.start()
        for cp_ in cps:
            cp_.wait()

    out = pl.pallas_call(
        body, in_specs=[ANY] * 6, out_specs=[ANY] * 6,
        out_shape=[SDS((3, _half_rows(nm), SHARD_SHAPE[nm][1]), bf16) for nm in BIG],
        scratch_shapes=[pltpu.SemaphoreType.DMA((18,)), pltpu.SemaphoreType.DMA((18,))],
        name="rs_chip_exchange")(*[cp[nm] for nm in BIG])
    return dict(zip(BIG, out))


def _chip_sum(nm, own, rc, chip_core):
    R, C = SHARD_SHAPE[nm]
    nt = (R // 2) // RS_TILE

    def body(sc_ref, own_ref, rc_ref, o_ref):
        o_ref[...] = own_ref[...] + rc_ref[0].astype(f32) + rc_ref[1].astype(f32) + rc_ref[2].astype(f32)

    return pl.pallas_call(
        body,
        grid_spec=pltpu.PrefetchScalarGridSpec(
            num_scalar_prefetch=1, grid=(nt,),
            in_specs=[pl.BlockSpec((RS_TILE, C), lambda i, sc: (i, 0)),
                      pl.BlockSpec((3, RS_TILE, C), lambda i, sc: (0, i, 0))],
            out_specs=pl.BlockSpec((RS_TILE, C), lambda i, sc: (sc[1] * nt + i, 0))),
        out_shape=SDS((R, C), f32), name="rs_chip_sum_" + nm,
        compiler_params=_cparams(("parallel",), 32))(chip_core, own, rc)


def _pair_share(tot):
    def body(*refs):
        outs = dict(zip(BIG, refs[6:12]))
        send_sems, recv_sems = refs[12:]
        x, y, c, chip, chips = _position()
        cps = []
        for a, nm in enumerate(BIG):
            hr = _half_rows(nm)
            mine = outs[nm].at[pl.ds(c * hr, hr), :]
            cps.append(_remote(mine, mine, send_sems, recv_sems, a, (x, y, 1 - c)))
        for cp in cps:
            cp.start()
        for cp in cps:
            cp.wait()

    out = pl.pallas_call(
        body, in_specs=[ANY] * 6, out_specs=[ANY] * 6, out_shape=[SDS(SHARD_SHAPE[nm], f32) for nm in BIG],
        input_output_aliases={a: a for a in range(6)},
        scratch_shapes=[pltpu.SemaphoreType.DMA((6,)), pltpu.SemaphoreType.DMA((6,))],
        name="rs_pair_share")(*[tot[nm] for nm in BIG])
    return dict(zip(BIG, out))


def _reduce_scatter(grads, chip):
    chip_core = jnp.stack([chip, lax.axis_index("c")]).astype(jnp.int32)
    got = _pair_exchange(grads)
    cp, own = {}, {}
    for nm in BIG:
        cp[nm], own[nm] = _pair_sum(nm, grads[nm], got[nm], chip_core)
    rc = _chip_exchange(cp)
    tot = {nm: _chip_sum(nm, own[nm], rc[nm], chip_core) for nm in BIG}
    return _pair_share(tot)


SMALL_ROWS = 40


def _allreduce_small(block):
    def body(x_ref, out_ref, buf, send_sems, recv_sems, local_sem):
        x, y, c, chip, chips = _position()
        me, sibling = (x, y, c), (x, y, 1 - c)

        def slot(px, py, pc):
            return buf.at[4 * px + 2 * py + pc]

        def copy(k, block_of, to, src=None):
            return _remote(slot(*block_of) if src is None else src, slot(*block_of), send_sems, recv_sems, k, to)

        mine = pltpu.make_async_copy(x_ref, slot(*me), local_sem)
        mine.start()
        first = [copy(0, me, sibling, src=x_ref)] + [copy(1 + j, me, (*ch, c), src=x_ref) for j, ch in enumerate(chips)]
        for cp in first:
            cp.start()
        passed = [copy(4 + j, (*ch, c), sibling) for j, ch in enumerate(chips)]
        for j, ch in enumerate(chips):
            copy(1 + j, (*ch, c), me).wait_recv()
            passed[j].start()
        copy(0, sibling, me).wait_recv()
        for j, ch in enumerate(chips):
            copy(4 + j, (*ch, 1 - c), me).wait_recv()
        for cp in first + passed:
            cp.wait_send()
        mine.wait()
        acc = buf[0]
        for d in range(1, 8):
            acc = acc + buf[d]
        out_ref[...] = acc

    vm = pl.BlockSpec(memory_space=pltpu.VMEM)
    return pl.pallas_call(
        body, in_specs=[vm], out_specs=vm, out_shape=SDS((SMALL_ROWS, D), f32),
        scratch_shapes=[pltpu.VMEM((8, SMALL_ROWS, D), f32), pltpu.SemaphoreType.DMA((7,)), pltpu.SemaphoreType.DMA((7,)),
                        pltpu.SemaphoreType.DMA],
        name="allreduce_small")(block)


def _adamw(w, g, m, v, name):
    rows, cols = w.shape
    tr = 256 if rows % 256 == 0 else rows

    def body(w_ref, g_ref, m_ref, v_ref, d_ref, nm_ref, nv_ref):
        gv = g_ref[...]
        m2 = ADAM_B1 * m_ref[...] + (1.0 - ADAM_B1) * gv
        v2 = ADAM_B2 * v_ref[...] + (1.0 - ADAM_B2) * jnp.square(gv)
        m_hat = m2 / (1.0 - ADAM_B1 ** ADAM_STEP)
        v_hat = v2 / (1.0 - ADAM_B2 ** ADAM_STEP)
        d_ref[...] = -ADAM_LR * (m_hat / (jnp.sqrt(v_hat) + ADAM_EPS) + ADAM_WD * w_ref[...])
        nm_ref[...] = m2
        nv_ref[...] = v2

    spec = pl.BlockSpec((tr, cols), lambda i: (i, 0))
    return pl.pallas_call(body, grid=(rows // tr,), in_specs=[spec] * 4, out_specs=[spec] * 3,
                          out_shape=[SDS((rows, cols), f32)] * 3, name=name,
                          compiler_params=_cparams(("parallel",), 40))(w, g, m, v)


WEIGHTS = ["norm_mix_g", "w_in", "q_norm_g", "k_norm_g", "attn_sinks", "rel_bias", "w_attn_o", "w_dw", "b_dw",
           "conv_ln_g", "conv_ln_b", "w_conv_out", "w_out", "norm_mlp_g", "w_ff1", "w_ff2"]
ROW_VECS = ["norm_mix_g", "b_dw", "conv_ln_g", "conv_ln_b", "norm_mlp_g"]
MISC_ROW = 5
W_DW_ROW = 8


def _pack_small(vals, loss=None):
    misc = [vals["q_norm_g"].reshape(1, HD), vals["k_norm_g"].reshape(1, HD), vals["attn_sinks"].reshape(1, NQ),
            jnp.zeros((1, 1), f32) if loss is None else loss.reshape(1, 1), jnp.zeros((1, 111), f32),
            vals["rel_bias"].reshape(1, NBUCKET * NQ), jnp.zeros((1, 256), f32)]
    rows = [vals[nm].reshape(1, D) for nm in ROW_VECS] + [jnp.concatenate(misc, axis=1), jnp.zeros((2, D), f32)]
    return jnp.concatenate(rows, axis=0)


def _unpack_small(block):
    out = {nm: block[i:i + 1] for i, nm in enumerate(ROW_VECS)}
    misc = block[MISC_ROW]
    out["q_norm_g"] = misc[0:64].reshape(1, HD)
    out["k_norm_g"] = misc[64:128].reshape(1, HD)
    out["attn_sinks"] = misc[128:144].reshape(1, NQ)
    out["rel_bias"] = misc[256:768].reshape(NBUCKET, NQ)
    return out, misc[144]


def kernel(x, norm_mix_g, w_in, q_norm_g, k_norm_g, attn_sinks, rel_bias, w_attn_o, w_dw, b_dw, conv_ln_g, conv_ln_b, w_conv_out, w_out, norm_mlp_g, w_ff1, w_ff2, loss_target, m_norm_mix_g, m_w_in, m_q_norm_g, m_k_norm_g, m_attn_sinks, m_rel_bias, m_w_attn_o, m_w_dw, m_b_dw, m_conv_ln_g, m_conv_ln_b, m_w_conv_out, m_w_out, m_norm_mlp_g, m_w_ff1, m_w_ff2, v_norm_mix_g, v_w_in, v_q_norm_g, v_k_norm_g, v_attn_sinks, v_rel_bias, v_w_attn_o, v_w_dw, v_b_dw, v_conv_ln_g, v_conv_ln_b, v_w_conv_out, v_w_out, v_norm_mlp_g, v_w_ff1, v_w_ff2):
    args = dict(locals())
    wts = {nm: args[nm] for nm in WEIGHTS}
    mom = {nm: args["m_" + nm] for nm in WEIGHTS}
    var = {nm: args["v_" + nm] for nm in WEIGHTS}
    chip = 2 * lax.axis_index("x") + lax.axis_index("y")

    chip_arr = jnp.reshape(chip, (1,)).astype(jnp.int32)
    placed = {nm: _place_shard(nm, wts[nm][0], chip_arr, bf16) for nm in BIG}
    placed["w_dw"] = _place_shard("w_dw", jnp.pad(w_dw[0], ((0, 1), (0, 0))), chip_arr, f32)
    full = _gather_weights(placed)
    full["w_in"] = _to_internal_cols(full["w_in"])
    for nm in WEIGHTS:
        if nm not in full:
            full[nm] = wts[nm]

    loss_part, grad_x, g = _local_step(x[0], loss_target[0], full)

    small = jnp.concatenate([_pack_small(g, loss_part), g["w_dw"]], axis=0)
    small = _allreduce_small(small)
    grads, loss = _unpack_small(small)
    grads["w_dw"] = lax.dynamic_slice(small[W_DW_ROW:W_DW_ROW + CW], (0, chip * (D // 4)), (CW, D // 4))
    grads.update(_reduce_scatter(g, chip))

    delta, new_m, new_v = {}, {}, {}
    sd, sm, sv = _adamw(_pack_small(wts), small[0:8], _pack_small(mom), _pack_small(var), "adamw_small")
    for res, blk in ((delta, sd), (new_m, sm), (new_v, sv)):
        res.update(_unpack_small(blk)[0])
    for nm in BIG + ["w_dw"]:
        shp = wts[nm].shape
        two_d = lambda a: a.reshape(shp[-2], shp[-1])
        delta[nm], new_m[nm], new_v[nm] = _adamw(two_d(wts[nm]), grads[nm], two_d(mom[nm]), two_d(var[nm]), "adamw_" + nm)

    def shaped(vals):
        return [vals[nm].reshape(wts[nm].shape) for nm in WEIGHTS]

    return (loss, grad_x[None], *shaped(grads), *shaped(delta), *shaped(new_m), *shaped(new_v))
```

```python
import functools

import numpy as np
import jax
import jax.numpy as jnp
from jax import lax
from jax.experimental import pallas as pl
from jax.experimental.pallas import tpu as pltpu

f32 = jnp.float32
bf16 = jnp.bfloat16
SDS = jax.ShapeDtypeStruct
MESH = pl.DeviceIdType.MESH

D = 1024
HD = 64
NQ = 16
NKV = 4
BLK = 128
CW = 31
HALO = 32
DFF = 4096
NBUCKET = 32
EPS = 1e-6
NEG = -1e30
INW = 5632
C_Q, C_A, C_G, C_GA, C_GC = 0, 1, 2, 3, 4
C_KV = 10

ADAM_LR = 0.001
ADAM_B1 = 0.9
ADAM_B2 = 0.999
ADAM_EPS = 1e-08
ADAM_WD = 0.01
ADAM_STEP = 10

VMEM_BYTES_V7X = 64 << 20


def _cparams(sem, vmem_mb):
    assert (vmem_mb << 20) < VMEM_BYTES_V7X
    return pltpu.CompilerParams(dimension_semantics=sem, vmem_limit_bytes=vmem_mb << 20)


def _dot(a, b):
    return jnp.dot(a, b, preferred_element_type=f32)


def _dot_nt(a, b):
    return lax.dot_general(a, b, (((1,), (1,)), ((), ())), preferred_element_type=f32)


def _dot_tn(a, b):
    return lax.dot_general(a, b, (((0,), (0,)), ((), ())), preferred_element_type=f32)


def _sigmoid(x):
    return 1.0 / (1.0 + jnp.exp(-x))


def _low_head_lanes():
    return lax.broadcasted_iota(jnp.int32, (1, 2 * HD), 1) < HD


def _head_blockdiag():
    r = lax.broadcasted_iota(jnp.int32, (2 * HD, 2 * HD), 0) // HD
    c = lax.broadcasted_iota(jnp.int32, (2 * HD, 2 * HD), 1) // HD
    return jnp.where(r == c, 1.0, 0.0).astype(bf16)


def _head_sums(z, bd):
    hi = z.astype(bf16)
    lo = (z - hi.astype(f32)).astype(bf16)
    return _dot(hi, bd) + _dot(lo, bd)


def _rms_inproj(x, g, w):
    T, N = x.shape[0], w.shape[1]
    tm, tn = 512, 512

    def body(x_ref, g_ref, w_ref, p_ref, u_ref):
        xv = x_ref[...]
        r = lax.rsqrt(jnp.mean(xv * xv, axis=-1, keepdims=True) + EPS)
        u = (xv * r * g_ref[...]).astype(bf16)
        u_ref[...] = u
        for c in range(N // tn):
            p_ref[:, c * tn:(c + 1) * tn] = _dot(u, w_ref[:, c * tn:(c + 1) * tn])

    return pl.pallas_call(
        body, grid=(T // tm,),
        in_specs=[pl.BlockSpec((tm, D), lambda i: (i, 0)),
                  pl.BlockSpec((1, D), lambda i: (0, 0)),
                  pl.BlockSpec((D, N), lambda i: (0, 0), pipeline_mode=pl.Buffered(1))],
        out_specs=[pl.BlockSpec((tm, N), lambda i: (i, 0)),
                   pl.BlockSpec((tm, D), lambda i: (i, 0))],
        out_shape=[SDS((T, N), f32), SDS((T, D), bf16)],
        name="rms_inproj", compiler_params=_cparams(("parallel",), 48))(x, g, w)


def _split_pair(pair, out_ref, p, lo):
    rolled = pltpu.roll(pair, HD, 1)
    zero = jnp.zeros_like(pair)
    c = 512 * p
    out_ref[:, c:c + 128] = jnp.where(lo, pair, zero).astype(bf16)
    out_ref[:, c + 128:c + 256] = jnp.where(lo, zero, rolled).astype(bf16)
    out_ref[:, c + 256:c + 384] = jnp.where(lo, rolled, zero).astype(bf16)
    out_ref[:, c + 384:c + 512] = jnp.where(lo, zero, pair).astype(bf16)


def _qk_prep(proj, gq2, gk2):
    T = proj.shape[0]
    tm = 512

    def body(q_ref, kv_ref, gq_ref, gk_ref, qn_ref, kk_ref, vv_ref):
        bd = _head_blockdiag()
        lo = _low_head_lanes()
        for p in range(NQ // 2):
            z = q_ref[:, 128 * p:128 * p + 128]
            r = lax.rsqrt(_head_sums(z * z, bd) * (1.0 / HD) + EPS)
            qn_ref[:, 128 * p:128 * p + 128] = (z * r * gq_ref[...] * (HD ** -0.5)).astype(bf16)
        for p in range(NKV // 2):
            z = kv_ref[:, 128 * p:128 * p + 128]
            r = lax.rsqrt(_head_sums(z * z, bd) * (1.0 / HD) + EPS)
            _split_pair(z * r * gk_ref[...], kk_ref, p, lo)
            _split_pair(kv_ref[:, 256 + 128 * p:256 + 128 * p + 128], vv_ref, p, lo)

    return pl.pallas_call(
        body, grid=(T // tm,),
        in_specs=[pl.BlockSpec((tm, D), lambda i: (i, C_Q)),
                  pl.BlockSpec((tm, 512), lambda i: (i, C_KV)),
                  pl.BlockSpec((1, 128), lambda i: (0, 0)),
                  pl.BlockSpec((1, 128), lambda i: (0, 0))],
        out_specs=[pl.BlockSpec((tm, D), lambda i: (i, 0))] * 3,
        out_shape=[SDS((T, D), bf16)] * 3,
        name="qk_prep", compiler_params=_cparams(("parallel",), 32))(proj, proj, gq2, gk2)


def _bucket_tile():
    qi = np.arange(BLK)[:, None]
    kj = np.arange(2 * BLK)[None, :]
    dist = qi + BLK - kj
    n = np.maximum(dist, 0)
    max_exact = NBUCKET // 2
    nf = np.maximum(n, 1).astype(np.float32)
    large = max_exact + (np.log(nf / max_exact) / np.float32(np.log(128 / max_exact))
                         * (NBUCKET - max_exact)).astype(np.int32)
    large = np.minimum(large, NBUCKET - 1)
    bucket = np.where(n < max_exact, n, large)
    valid = (dist >= 0) & (dist < BLK)
    return np.where(valid, bucket, -1).astype(np.int32)


def _bias_tiles(rel_bias):
    def body(rb_ref, bk_ref, out_ref):
        bk = bk_ref[...]
        for h in range(NQ):
            acc = jnp.full((BLK, 2 * BLK), NEG, f32)
            for b in range(NBUCKET):
                acc = jnp.where(bk == b, rb_ref[b, h], acc)
            out_ref[h] = acc

    return pl.pallas_call(
        body,
        in_specs=[pl.BlockSpec(memory_space=pltpu.SMEM), pl.BlockSpec(memory_space=pltpu.VMEM)],
        out_specs=pl.BlockSpec(memory_space=pltpu.VMEM),
        out_shape=SDS((NQ, BLK, 2 * BLK), f32),
        name="bias_tiles")(rel_bias, jnp.asarray(_bucket_tile()))


def _rows2(ref, c):
    return jnp.concatenate([ref[:, c:c + 128], ref[:, c + 128:c + 256]], axis=0)


def _attn_fwd(qn, kk, vv, bias, sinks):
    T = qn.shape[0]
    nb = T // BLK

    def body(s_ref, q_ref, kc_ref, kp_ref, vc_ref, vp_ref, b_ref, o_ref, lse_ref):
        has_prev = pl.program_id(0) > 0
        for h in range(NKV):
            c = 256 * h
            qs = _rows2(q_ref, c)
            sc = _dot_nt(qs, _rows2(kc_ref, c))
            sp = _dot_nt(qs, _rows2(kp_ref, c))
            vstack = jnp.concatenate([vp_ref[:, c:c + 128], vc_ref[:, c:c + 128],
                                      vp_ref[:, c + 128:c + 256], vc_ref[:, c + 128:c + 256]], axis=0)
            for pr in range(2):
                ps = []
                for e in range(2):
                    hq = 4 * h + 2 * pr + e
                    rows, cols = slice(128 * pr, 128 * pr + 128), slice(128 * e, 128 * e + 128)
                    s_p = jnp.where(has_prev, sp[rows, cols] + b_ref[hq, :, 0:BLK], NEG)
                    s_c = sc[rows, cols] + b_ref[hq, :, BLK:2 * BLK]
                    sink = s_ref[0, hq]
                    m = jnp.maximum(jnp.maximum(jnp.max(s_p, axis=-1, keepdims=True),
                                                jnp.max(s_c, axis=-1, keepdims=True)), sink)
                    e_p = jnp.exp(s_p - m)
                    e_c = jnp.exp(s_c - m)
                    l = (jnp.sum(e_p, axis=-1, keepdims=True) + jnp.sum(e_c, axis=-1, keepdims=True)
                         + jnp.exp(sink - m))
                    inv = 1.0 / l
                    ps += [(e_p * inv).astype(bf16), (e_c * inv).astype(bf16)]
                    lse_ref[:, hq:hq + 1] = m + jnp.log(l)
                o_ref[:, c + 128 * pr:c + 128 * pr + 128] = _dot(jnp.concatenate(ps, axis=1), vstack).astype(bf16)

    blk = lambda f: pl.BlockSpec((BLK, D), f)
    cur = lambda n: (n, 0)
    prev = lambda n: (jnp.maximum(n - 1, 0), 0)
    return pl.pallas_call(
        body, grid=(nb,),
        in_specs=[pl.BlockSpec(memory_space=pltpu.SMEM), blk(cur), blk(cur), blk(prev), blk(cur), blk(prev),
                  pl.BlockSpec((NQ, BLK, 2 * BLK), lambda n: (0, 0, 0))],
        out_specs=[blk(cur), pl.BlockSpec((BLK, NQ), cur)],
        out_shape=[SDS((T, D), bf16), SDS((T, NQ), f32)],
        name="attn_fwd", compiler_params=_cparams(("parallel",), 32))(sinks, qn, kk, kk, vv, vv, bias)


SUB = 8


def _shifted_copies(sh_ref, rows):
    for b in range(1, SUB):
        sh_ref[b, 0:rows, :] = sh_ref[0, pl.ds(b, rows), :]


def _shifted_rows(sh_ref, r0, offset, rows, ln):
    a, b = divmod(offset, SUB)
    return sh_ref[b, pl.ds(pl.multiple_of(r0 + SUB * a, SUB), rows), ln]


def _conv_taps(sh_ref, w_ref, r0, rows, offset_of_tap, init):
    halves = []
    for hf in range(2):
        ln = slice(512 * hf, 512 * hf + 512)
        acc = init(ln)
        for j in range(CW):
            acc = acc + _shifted_rows(sh_ref, r0, offset_of_tap(j), rows, ln) * w_ref[j:j + 1, ln]
        halves.append(acc)
    return halves


def _glu_conv_fwd(proj, w_dw, b_dw, ln_g, ln_b):
    T = proj.shape[0]
    tt, ch = 256, 32

    def body(a_ref, g_ref, ah_ref, gh_ref, w_ref, b_ref, lg_ref, lb_ref, h1_ref, h3_ref, sh):
        i = pl.program_id(0)
        halo = ah_ref[...] * _sigmoid(gh_ref[...])
        sh[0, 0:HALO, :] = jnp.where(i > 0, halo, 0.0)
        sh[0, HALO:HALO + tt, :] = a_ref[...] * _sigmoid(g_ref[...])
        _shifted_copies(sh, tt + HALO - SUB)

        def chunk(c, carry):
            r0 = pl.multiple_of(c * ch, ch)
            lo, hi = _conv_taps(sh, w_ref, r0, ch, lambda j: HALO - (CW - 1) + j,
                                lambda ln: jnp.broadcast_to(b_ref[:, ln], (ch, 512)))
            h1_ref[pl.ds(r0, ch), 0:512] = lo
            h1_ref[pl.ds(r0, ch), 512:1024] = hi
            return carry

        lax.fori_loop(0, tt // ch, chunk, 0)
        h1 = h1_ref[...]
        mu = jnp.mean(h1, axis=-1, keepdims=True)
        xc = h1 - mu
        var = jnp.mean(xc * xc, axis=-1, keepdims=True)
        h2 = xc * lax.rsqrt(var + EPS) * lg_ref[...] + lb_ref[...]
        h3_ref[...] = (h2 * _sigmoid(h2)).astype(bf16)

    hpt = tt // HALO
    tile = lambda cb: pl.BlockSpec((tt, D), lambda i: (i, cb))
    halo = lambda cb: pl.BlockSpec((HALO, D), lambda i: (jnp.maximum(i * hpt - 1, 0), cb))
    vec = pl.BlockSpec((1, D), lambda i: (0, 0))
    return pl.pallas_call(
        body, grid=(T // tt,),
        in_specs=[tile(C_A), tile(C_G), halo(C_A), halo(C_G), pl.BlockSpec((HALO, D), lambda i: (0, 0)), vec, vec, vec],
        out_specs=[pl.BlockSpec((tt, D), lambda i: (i, 0))] * 2,
        out_shape=[SDS((T, D), f32), SDS((T, D), bf16)],
        scratch_shapes=[pltpu.VMEM((SUB, HALO + tt, D), f32)],
        name="glu_conv_fwd", compiler_params=_cparams(("parallel",), 32))(proj, proj, proj, proj, w_dw, b_dw, ln_g, ln_b)


def _mix_out(o, h3, proj, x, w_attn_o, w_conv_out, w_out, g_mlp):
    T = x.shape[0]
    tm = 512

    def body(o_ref, h3_ref, ga_ref, gc_ref, x_ref, wa_ref, wc_ref, wo_ref, g_ref,
             attn_ref, conv_ref, mg_ref, x1_ref, n2_ref):
        attn = _dot(o_ref[...], wa_ref[...])
        conv = _dot(h3_ref[...], wc_ref[...])
        attn_ref[...] = attn
        conv_ref[...] = conv
        mg = (_sigmoid(ga_ref[...]) * attn + _sigmoid(gc_ref[...]) * conv).astype(bf16)
        mg_ref[...] = mg
        x1 = x_ref[...] + _dot(mg, wo_ref[...])
        x1_ref[...] = x1
        r = lax.rsqrt(jnp.mean(x1 * x1, axis=-1, keepdims=True) + EPS)
        n2_ref[...] = (x1 * r * g_ref[...]).astype(bf16)

    tile = lambda cb=0: pl.BlockSpec((tm, D), lambda i: (i, cb))
    wfull = pl.BlockSpec((D, D), lambda i: (0, 0))
    return pl.pallas_call(
        body, grid=(T // tm,),
        in_specs=[tile(), tile(), tile(C_GA), tile(C_GC), tile(), wfull, wfull, wfull,
                  pl.BlockSpec((1, D), lambda i: (0, 0))],
        out_specs=[tile()] * 5,
        out_shape=[SDS((T, D), f32), SDS((T, D), f32), SDS((T, D), bf16), SDS((T, D), f32), SDS((T, D), bf16)],
        name="mix_out", compiler_params=_cparams(("parallel",), 48))(o, h3, proj, proj, x, w_attn_o, w_conv_out, w_out, g_mlp)


def _mlp_fwd(n2, w1, w2, x1, tgt):
    T = n2.shape[0]
    tm, tf = 512, 1024

    def body(n2_ref, w1_ref, w2_ref, x1_ref, t_ref, hm_ref, dy_ref, dyb_ref, loss_ref):
        @pl.when(pl.program_id(0) == 0)
        def _():
            loss_ref[...] = jnp.zeros_like(loss_ref)

        n2v = n2_ref[...]
        for c in range(DFF // tf):
            r = jnp.maximum(_dot(n2v, w1_ref[:, c * tf:(c + 1) * tf]), 0.0)
            hm_ref[:, c * tf:(c + 1) * tf] = (r * r).astype(bf16)
        e = x1_ref[...] + _dot(hm_ref[...], w2_ref[...]) - t_ref[...]
        dy = e * (1.0 / D)
        dy_ref[...] = dy
        dyb_ref[...] = dy.astype(bf16)
        loss_ref[...] += 0.5 * jnp.sum(jnp.sum(e * e, axis=-1, keepdims=True) * (1.0 / D))

    row = pl.BlockSpec((tm, D), lambda i: (i, 0))
    once = pl.Buffered(1)
    return pl.pallas_call(
        body, grid=(T // tm,),
        in_specs=[row, pl.BlockSpec((D, DFF), lambda i: (0, 0), pipeline_mode=once),
                  pl.BlockSpec((DFF, D), lambda i: (0, 0), pipeline_mode=once), row, row],
        out_specs=[pl.BlockSpec((tm, DFF), lambda i: (i, 0)), row, row, pl.BlockSpec((8, 128), lambda i: (0, 0))],
        out_shape=[SDS((T, DFF), bf16), SDS((T, D), f32), SDS((T, D), bf16), SDS((8, 128), f32)],
        name="mlp_fwd", compiler_params=_cparams(("arbitrary",), 56))(n2, w1, w2, x1, tgt)


def _rms_bwd(xv, g, dn, dres):
    r = lax.rsqrt(jnp.mean(xv * xv, axis=-1, keepdims=True) + EPS)
    gd = dn * g
    dx = dres + r * gd - xv * (r * r * r) * jnp.mean(xv * gd, axis=-1, keepdims=True)
    dg = jnp.sum(dn * xv * r, axis=0, keepdims=True)
    return dx, dg


def _mlp_bwd(dy, dyb, hmid, w1, w2, x1, g_mlp):
    T = dy.shape[0]
    tm, tf = 512, 1024

    def body(dy_ref, dyb_ref, hm_ref, w1_ref, w2_ref, x1_ref, g_ref, df_ref, dx_ref, dxb_ref, dg_ref):
        @pl.when(pl.program_id(0) == 0)
        def _():
            dg_ref[...] = jnp.zeros_like(dg_ref)

        dyb = dyb_ref[...]
        for c in range(DFF // tf):
            cols = slice(c * tf, (c + 1) * tf)
            d_hm = _dot_nt(dyb, w2_ref[cols, :])
            df_ref[:, cols] = (d_hm * (2.0 * jnp.sqrt(hm_ref[:, cols].astype(f32)))).astype(bf16)
        dn = _dot_nt(df_ref[...], w1_ref[...])
        dx, dg = _rms_bwd(x1_ref[...], g_ref[...], dn, dy_ref[...])
        dx_ref[...] = dx
        dxb_ref[...] = dx.astype(bf16)
        dg_ref[...] += dg

    row = pl.BlockSpec((tm, D), lambda i: (i, 0))
    wide = pl.BlockSpec((tm, DFF), lambda i: (i, 0))
    vec = pl.BlockSpec((1, D), lambda i: (0, 0))
    once = pl.Buffered(1)
    return pl.pallas_call(
        body, grid=(T // tm,),
        in_specs=[row, row, wide, pl.BlockSpec((D, DFF), lambda i: (0, 0), pipeline_mode=once),
                  pl.BlockSpec((DFF, D), lambda i: (0, 0), pipeline_mode=once), row, vec],
        out_specs=[wide, row, row, vec],
        out_shape=[SDS((T, DFF), bf16), SDS((T, D), f32), SDS((T, D), bf16), SDS((1, D), f32)],
        name="mlp_bwd", compiler_params=_cparams(("arbitrary",), 56))(dy, dyb, hmid, w1, w2, x1, g_mlp)


def _wgrad(a, b, name, tn=1024):
    T, M = a.shape
    N = b.shape[1]
    tmm, tk = min(M, 1024), min(T, 2048)

    def body(a_ref, b_ref, o_ref):
        @pl.when(pl.program_id(2) == 0)
        def _():
            o_ref[...] = jnp.zeros_like(o_ref)

        o_ref[...] += _dot_tn(a_ref[...], b_ref[...])

    return pl.pallas_call(
        body, grid=(M // tmm, N // tn, T // tk),
        in_specs=[pl.BlockSpec((tk, tmm), lambda m, n, t: (t, m)), pl.BlockSpec((tk, tn), lambda m, n, t: (t, n))],
        out_specs=pl.BlockSpec((tmm, tn), lambda m, n, t: (m, n)),
        out_shape=SDS((M, N), f32),
        name=name, compiler_params=_cparams(("parallel", "parallel", "arbitrary"), 40))(a, b)


def _mix_bwd(dx1b, proj, attn, conv, w_attn_o, w_conv_out, w_out):
    T = dx1b.shape[0]
    tm = 512

    def body(dx_ref, ga_ref, gc_ref, attn_ref, conv_ref, wa_ref, wc_ref, wo_ref,
             dat_ref, dcv_ref, do_ref, dh3_ref, dga_ref, dgc_ref):
        dm = _dot_nt(dx_ref[...], wo_ref[...])
        sa = _sigmoid(ga_ref[...])
        sc = _sigmoid(gc_ref[...])
        dat = (dm * sa).astype(bf16)
        dcv = (dm * sc).astype(bf16)
        dat_ref[...] = dat
        dcv_ref[...] = dcv
        dga_ref[...] = (dm * attn_ref[...] * sa * (1.0 - sa)).astype(bf16)
        dgc_ref[...] = (dm * conv_ref[...] * sc * (1.0 - sc)).astype(bf16)
        do_ref[...] = _dot_nt(dat, wa_ref[...]).astype(bf16)
        dh3_ref[...] = _dot_nt(dcv, wc_ref[...])

    tile = lambda cb=0: pl.BlockSpec((tm, D), lambda i: (i, cb))
    wfull = pl.BlockSpec((D, D), lambda i: (0, 0))
    return pl.pallas_call(
        body, grid=(T // tm,),
        in_specs=[tile(), tile(C_GA), tile(C_GC), tile(), tile(), wfull, wfull, wfull],
        out_specs=[tile()] * 6,
        out_shape=[SDS((T, D), bf16), SDS((T, D), bf16), SDS((T, D), bf16), SDS((T, D), f32),
                   SDS((T, D), bf16), SDS((T, D), bf16)],
        name="mix_bwd", compiler_params=_cparams(("parallel",), 48))(dx1b, proj, proj, attn, conv, w_attn_o, w_conv_out, w_out)


def _conv_ln_bwd(dh3, h1, ln_g, ln_b):
    T = dh3.shape[0]
    tt = 256

    def body(d_ref, h1_ref, lg_ref, lb_ref, dh1_ref, acc_ref):
        @pl.when(pl.program_id(0) == 0)
        def _():
            acc_ref[...] = jnp.zeros_like(acc_ref)

        h1 = h1_ref[...]
        mu = jnp.mean(h1, axis=-1, keepdims=True)
        xc = h1 - mu
        rstd = lax.rsqrt(jnp.mean(xc * xc, axis=-1, keepdims=True) + EPS)
        xh = xc * rstd
        h2 = xh * lg_ref[...] + lb_ref[...]
        sg = _sigmoid(h2)
        dh2 = d_ref[...] * (sg * (1.0 + h2 * (1.0 - sg)))
        dxh = dh2 * lg_ref[...]
        dh1 = rstd * (dxh - jnp.mean(dxh, axis=-1, keepdims=True) - xh * jnp.mean(dxh * xh, axis=-1, keepdims=True))
        dh1_ref[...] = dh1
        acc_ref[0:1, :] += jnp.sum(dh2 * xh, axis=0, keepdims=True)
        acc_ref[1:2, :] += jnp.sum(dh2, axis=0, keepdims=True)
        acc_ref[2:3, :] += jnp.sum(dh1, axis=0, keepdims=True)

    tile = pl.BlockSpec((tt, D), lambda i: (i, 0))
    vec = pl.BlockSpec((1, D), lambda i: (0, 0))
    return pl.pallas_call(
        body, grid=(T // tt,),
        in_specs=[tile, tile, vec, vec],
        out_specs=[tile, pl.BlockSpec((8, D), lambda i: (0, 0))],
        out_shape=[SDS((T, D), f32), SDS((8, D), f32)],
        name="conv_ln_bwd", compiler_params=_cparams(("arbitrary",), 32))(dh3, h1, ln_g, ln_b)


def _conv_bwd(dh1, proj, w_dw):
    T = dh1.shape[0]
    tt, ch = 256, 32
    nt = T // tt

    def body(d_ref, dn_ref, a_ref, g_ref, ah_ref, gh_ref, w_ref, da_ref, dg_ref, gw_ref, dsh, hsh, dh0, gacc):
        i = pl.program_id(0)

        @pl.when(i == 0)
        def _():
            gacc[...] = jnp.zeros_like(gacc)

        dsh[0, 0:tt, :] = d_ref[...]
        dsh[0, tt:tt + HALO, :] = jnp.where(i < nt - 1, dn_ref[...], 0.0)
        hsh[0, 0:HALO, :] = jnp.where(i > 0, ah_ref[...] * _sigmoid(gh_ref[...]), 0.0)
        hsh[0, HALO:HALO + tt, :] = a_ref[...] * _sigmoid(g_ref[...])
        _shifted_copies(dsh, tt + HALO - SUB)
        _shifted_copies(hsh, tt + HALO - SUB)

        def chunk(c, carry):
            r0 = pl.multiple_of(c * ch, ch)
            lo, hi = _conv_taps(dsh, w_ref, r0, ch, lambda j: (CW - 1) - j, lambda ln: jnp.zeros((ch, 512), f32))
            dh0[pl.ds(r0, ch), 0:512] = lo
            dh0[pl.ds(r0, ch), 512:1024] = hi
            for hf in range(2):
                ln = slice(512 * hf, 512 * hf + 512)
                dv = dsh[0, pl.ds(r0, ch), ln]
                for j in range(CW):
                    pr = dv * _shifted_rows(hsh, r0, HALO - (CW - 1) + j, ch, ln)
                    gacc[8 * j:8 * j + 8, ln] += pr[0:8] + pr[8:16] + pr[16:24] + pr[24:32]
            return carry

        lax.fori_loop(0, tt // ch, chunk, 0)
        a = a_ref[...]
        sg = _sigmoid(g_ref[...])
        d0 = dh0[...]
        da_ref[...] = (d0 * sg).astype(bf16)
        dg_ref[...] = (d0 * a * sg * (1.0 - sg)).astype(bf16)

        @pl.when(i == nt - 1)
        def _():
            gw_ref[...] = jnp.zeros_like(gw_ref)
            for j in range(CW):
                gw_ref[j:j + 1, :] = jnp.sum(gacc[8 * j:8 * j + 8, :], axis=0, keepdims=True)

    hpt = tt // HALO
    tile = lambda cb=0: pl.BlockSpec((tt, D), lambda i: (i, cb))
    halo_prev = lambda cb: pl.BlockSpec((HALO, D), lambda i: (jnp.maximum(i * hpt - 1, 0), cb))
    halo_next = pl.BlockSpec((HALO, D), lambda i: (jnp.minimum((i + 1) * hpt, T // HALO - 1), 0))
    wspec = pl.BlockSpec((HALO, D), lambda i: (0, 0))
    return pl.pallas_call(
        body, grid=(nt,),
        in_specs=[tile(), halo_next, tile(C_A), tile(C_G), halo_prev(C_A), halo_prev(C_G), wspec],
        out_specs=[tile(), tile(), wspec],
        out_shape=[SDS((T, D), bf16), SDS((T, D), bf16), SDS((HALO, D), f32)],
        scratch_shapes=[pltpu.VMEM((SUB, tt + HALO, D), f32), pltpu.VMEM((SUB, HALO + tt, D), f32),
                        pltpu.VMEM((tt, D), f32), pltpu.VMEM((8 * HALO, D), f32)],
        name="conv_bwd", compiler_params=_cparams(("arbitrary",), 48))(dh1, dh1, proj, proj, proj, proj, w_dw)


def _attn_bwd(qn, kk, vv, bias, sinks, o, do, lse):
    T = qn.shape[0]
    nb = T // BLK

    def body(s_ref, q_ref, kc_ref, kp_ref, vc_ref, vp_ref, b_ref, o_ref, do_ref, lse_ref,
             dq_ref, dkc_ref, dkp_ref, dvc_ref, dvp_ref, dsk_ref, dsa_ref):
        n = pl.program_id(0)

        @pl.when(n == 0)
        def _():
            dsk_ref[...] = jnp.zeros_like(dsk_ref)
            dsa_ref[...] = jnp.zeros_like(dsa_ref)

        @pl.when(n == nb)
        def _():
            dkp_ref[...] = jnp.zeros_like(dkp_ref)
            dvp_ref[...] = jnp.zeros_like(dvp_ref)

        @pl.when(n < nb)
        def _():
            has_prev = n > 0
            lo = _low_head_lanes()
            dups = {"kc": [], "kp": [], "vc": [], "vp": []}
            for h in range(NKV):
                c = 256 * h
                qs = _rows2(q_ref, c)
                dos = _rows2(do_ref, c)
                sc = _dot_nt(qs, _rows2(kc_ref, c))
                sp = _dot_nt(qs, _rows2(kp_ref, c))
                dpc = _dot_nt(dos, _rows2(vc_ref, c))
                dpp = _dot_nt(dos, _rows2(vp_ref, c))
                kstack = jnp.concatenate([kp_ref[:, c:c + 128], kc_ref[:, c:c + 128],
                                          kp_ref[:, c + 128:c + 256], kc_ref[:, c + 128:c + 256]], axis=0)
                p_c, p_p, ds_c, ds_p = [], [], [], []
                for pr in range(2):
                    cc = c + 128 * pr
                    prod = do_ref[:, cc:cc + 128].astype(f32) * o_ref[:, cc:cc + 128].astype(f32)
                    d_lo = jnp.sum(jnp.where(lo, prod, 0.0), axis=-1, keepdims=True)
                    d_hi = jnp.sum(prod, axis=-1, keepdims=True) - d_lo
                    row_pc, row_pp, row_dc, row_dp = [], [], [], []
                    for e in range(2):
                        hq = 4 * h + 2 * pr + e
                        rows, cols = slice(128 * pr, 128 * pr + 128), slice(128 * e, 128 * e + 128)
                        delta = d_lo if e == 0 else d_hi
                        lse = lse_ref[:, hq:hq + 1]
                        pp = jnp.where(has_prev, jnp.exp(sp[rows, cols] + b_ref[hq, :, 0:BLK] - lse), 0.0)
                        pc = jnp.exp(sc[rows, cols] + b_ref[hq, :, BLK:2 * BLK] - lse)
                        dsp = pp * (dpp[rows, cols] - delta)
                        dsc = pc * (dpc[rows, cols] - delta)
                        dsa_ref[hq, :, 0:BLK] += dsp
                        dsa_ref[hq, :, BLK:2 * BLK] += dsc
                        dsk_ref[hq] += jnp.broadcast_to(-jnp.sum(jnp.exp(s_ref[0, hq] - lse) * delta), (8, 128))
                        row_pc.append(pc.astype(bf16))
                        row_pp.append(pp.astype(bf16))
                        row_dc.append(dsc.astype(bf16))
                        row_dp.append(dsp.astype(bf16))
                    dq_ref[:, cc:cc + 128] = _dot(jnp.concatenate([row_dp[0], row_dc[0], row_dp[1], row_dc[1]], axis=1), kstack)
                    p_c.append(jnp.concatenate(row_pc, axis=1))
                    p_p.append(jnp.concatenate(row_pp, axis=1))
                    ds_c.append(jnp.concatenate(row_dc, axis=1))
                    ds_p.append(jnp.concatenate(row_dp, axis=1))

                def to_keys(m2, rhs):
                    x2 = _dot_tn(jnp.concatenate(m2, axis=0), rhs)
                    x = jnp.where(lo, x2[0:128], x2[128:256])
                    return x + pltpu.roll(x, HD, 1)

                dups["kc"].append(to_keys(ds_c, qs))
                dups["kp"].append(to_keys(ds_p, qs))
                dups["vc"].append(to_keys(p_c, dos))
                dups["vp"].append(to_keys(p_p, dos))
            for key, ref in (("kc", dkc_ref), ("kp", dkp_ref), ("vc", dvc_ref), ("vp", dvp_ref)):
                d = dups[key]
                ref[:, 0:128] = jnp.where(lo, d[0], d[1])
                ref[:, 128:256] = jnp.where(lo, d[2], d[3])

    clamp = lambda n: jnp.minimum(n, nb - 1)
    blk = lambda f: pl.BlockSpec((BLK, D), f)
    cur = lambda n: (clamp(n), 0)
    prev = lambda n: (jnp.maximum(clamp(n) - 1, 0), 0)
    back = lambda n: (jnp.maximum(n - 1, 0), 0)
    kvb = lambda f: pl.BlockSpec((BLK, NKV * HD), f)
    return pl.pallas_call(
        body, grid=(nb + 1,),
        in_specs=[pl.BlockSpec(memory_space=pltpu.SMEM), blk(cur), blk(cur), blk(prev), blk(cur), blk(prev),
                  pl.BlockSpec((NQ, BLK, 2 * BLK), lambda n: (0, 0, 0)), blk(cur), blk(cur),
                  pl.BlockSpec((BLK, NQ), cur)],
        out_specs=[blk(cur), kvb(cur), kvb(back), kvb(cur), kvb(back),
                   pl.BlockSpec((NQ, 8, 128), lambda n: (0, 0, 0)),
                   pl.BlockSpec((NQ, BLK, 2 * BLK), lambda n: (0, 0, 0))],
        out_shape=[SDS((T, D), f32)] + [SDS((T, NKV * HD), f32)] * 4 + [SDS((NQ, 8, 128), f32), SDS((NQ, BLK, 2 * BLK), f32)],
        name="attn_bwd", compiler_params=_cparams(("arbitrary",), 40))(sinks, qn, kk, kk, vv, vv, bias, o, do, lse)


def _bias_bwd(dsa):
    def body(bk_ref, ds_ref, out_ref):
        bk = bk_ref[...]
        lane = lax.broadcasted_iota(jnp.int32, (1, 128), 1)
        for h in range(NQ):
            ds = ds_ref[h]
            row = jnp.zeros((1, 128), f32)
            for b in range(NBUCKET):
                row = jnp.where(lane == b, jnp.sum(jnp.where(bk == b, ds, 0.0)), row)
            out_ref[h:h + 1, :] = row

    return pl.pallas_call(body, out_shape=SDS((NQ, 128), f32), name="bias_bwd")(jnp.asarray(_bucket_tile()), dsa)


def _qkv_bwd(proj, gq2, gk2, dqn, dkc, dkp, dvc, dvp):
    T = proj.shape[0]
    tm = 512

    def body(q_ref, kv_ref, gq_ref, gk_ref, dq_ref, dkc_ref, dkp_ref, dvc_ref, dvp_ref,
             oq_ref, okv_ref, ggq_ref, ggk_ref):
        @pl.when(pl.program_id(0) == 0)
        def _():
            ggq_ref[...] = jnp.zeros_like(ggq_ref)
            ggk_ref[...] = jnp.zeros_like(ggk_ref)

        bd = _head_blockdiag()

        def norm_bwd(z, dy, g, scale):
            r = lax.rsqrt(_head_sums(z * z, bd) * (1.0 / HD) + EPS)
            gd = dy * g * scale
            dz = r * gd - z * (r * r * r) * _head_sums(z * gd, bd) * (1.0 / HD)
            return dz, jnp.sum(dy * scale * z * r, axis=0, keepdims=True)

        gq = jnp.zeros((1, 128), f32)
        for p in range(NQ // 2):
            ln = slice(128 * p, 128 * p + 128)
            dz, dg = norm_bwd(q_ref[:, ln], dq_ref[:, ln], gq_ref[...], HD ** -0.5)
            oq_ref[:, ln] = dz.astype(bf16)
            gq = gq + dg
        ggq_ref[...] += gq + pltpu.roll(gq, HD, 1)
        gk = jnp.zeros((1, 128), f32)
        for p in range(NKV // 2):
            ln = slice(128 * p, 128 * p + 128)
            dz, dg = norm_bwd(kv_ref[:, ln], dkc_ref[:, ln] + dkp_ref[:, ln], gk_ref[...], 1.0)
            okv_ref[:, ln] = dz.astype(bf16)
            gk = gk + dg
        ggk_ref[...] += gk + pltpu.roll(gk, HD, 1)
        okv_ref[:, 256:512] = (dvc_ref[...] + dvp_ref[...]).astype(bf16)

    vec = pl.BlockSpec((1, 128), lambda i: (0, 0))
    kvb = pl.BlockSpec((tm, NKV * HD), lambda i: (i, 0))
    return pl.pallas_call(
        body, grid=(T // tm,),
        in_specs=[pl.BlockSpec((tm, D), lambda i: (i, C_Q)), pl.BlockSpec((tm, 512), lambda i: (i, C_KV)), vec, vec,
                  pl.BlockSpec((tm, D), lambda i: (i, 0)), kvb, kvb, kvb, kvb],
        out_specs=[pl.BlockSpec((tm, D), lambda i: (i, 0)), pl.BlockSpec((tm, 512), lambda i: (i, 0)), vec, vec],
        out_shape=[SDS((T, D), bf16), SDS((T, 512), bf16), SDS((1, 128), f32), SDS((1, 128), f32)],
        name="qkv_bwd", compiler_params=_cparams(("arbitrary",), 32))(proj, proj, gq2, gk2, dqn, dkc, dkp, dvc, dvp)


def _inproj_bwd(pieces, w_in, x, dx1, g_mix):
    T = x.shape[0]
    tm = 256
    widths = [p.shape[1] for p in pieces]
    offs = [sum(widths[:i]) for i in range(len(widths))]
    assert sum(widths) == INW

    def body(*refs):
        p_refs, (w_ref, x_ref, dx1_ref, g_ref, dx_ref, dg_ref) = refs[:len(pieces)], refs[len(pieces):]

        @pl.when(pl.program_id(0) == 0)
        def _():
            dg_ref[...] = jnp.zeros_like(dg_ref)

        du = None
        for p_ref, off, wd in zip(p_refs, offs, widths):
            part = _dot_nt(p_ref[...], w_ref[:, off:off + wd])
            du = part if du is None else du + part
        dx, dg = _rms_bwd(x_ref[...], g_ref[...], du, dx1_ref[...])
        dx_ref[...] = dx
        dg_ref[...] += dg

    row = pl.BlockSpec((tm, D), lambda i: (i, 0))
    vec = pl.BlockSpec((1, D), lambda i: (0, 0))
    return pl.pallas_call(
        body, grid=(T // tm,),
        in_specs=[pl.BlockSpec((tm, wd), lambda i: (i, 0)) for wd in widths]
        + [pl.BlockSpec((D, INW), lambda i: (0, 0)), row, row, vec],
        out_specs=[row, vec],
        out_shape=[SDS((T, D), f32), SDS((1, D), f32)],
        name="inproj_bwd", compiler_params=_cparams(("arbitrary",), 48))(*pieces, w_in, x, dx1, g_mix)


def _to_internal_cols(w):
    return jnp.concatenate([w[..., 0:1024], w[..., 1536:INW], w[..., 1024:1536]], axis=-1)


def _local_step(x, tgt, w):
    gq2 = jnp.tile(w["q_norm_g"], (1, 2))
    gk2 = jnp.tile(w["k_norm_g"], (1, 2))
    proj, u = _rms_inproj(x, w["norm_mix_g"], w["w_in"])
    qn, kk, vv = _qk_prep(proj, gq2, gk2)
    bias = _bias_tiles(w["rel_bias"])
    o, lse = _attn_fwd(qn, kk, vv, bias, w["attn_sinks"])
    h1, h3 = _glu_conv_fwd(proj, w["w_dw"], w["b_dw"], w["conv_ln_g"], w["conv_ln_b"])
    attn, conv, merged, x1, n2 = _mix_out(o, h3, proj, x, w["w_attn_o"], w["w_conv_out"], w["w_out"], w["norm_mlp_g"])
    hmid, dy, dyb, loss = _mlp_fwd(n2, w["w_ff1"], w["w_ff2"], x1, tgt)

    g = {}
    df1, dx1, dx1b, g["norm_mlp_g"] = _mlp_bwd(dy, dyb, hmid, w["w_ff1"], w["w_ff2"], x1, w["norm_mlp_g"])
    g["w_ff2"] = _wgrad(hmid, dyb, "wgrad_ff2")
    g["w_ff1"] = _wgrad(n2, df1, "wgrad_ff1")
    dat, dcv, do, dh3, dga, dgc = _mix_bwd(dx1b, proj, attn, conv, w["w_attn_o"], w["w_conv_out"], w["w_out"])
    g["w_out"] = _wgrad(merged, dx1b, "wgrad_out")
    g["w_attn_o"] = _wgrad(o, dat, "wgrad_attn_o")
    g["w_conv_out"] = _wgrad(h3, dcv, "wgrad_conv_out")
    dh1, lnacc = _conv_ln_bwd(dh3, h1, w["conv_ln_g"], w["conv_ln_b"])
    g["conv_ln_g"], g["conv_ln_b"], g["b_dw"] = lnacc[0:1], lnacc[1:2], lnacc[2:3]
    da, dg, gw_dw = _conv_bwd(dh1, proj, w["w_dw"])
    g["w_dw"] = gw_dw
    dqn, dkc, dkp, dvc, dvp, dsk, dsa = _attn_bwd(qn, kk, vv, bias, w["attn_sinks"], o, do, lse)
    g["attn_sinks"] = dsk[:, 0, 0].reshape(1, NQ)
    g["rel_bias"] = _bias_bwd(dsa)[:, 0:NBUCKET].T
    dq, dkv, ggq, ggk = _qkv_bwd(proj, gq2, gk2, dqn, dkc, dkp, dvc, dvp)
    g["q_norm_g"], g["k_norm_g"] = ggq[:, 0:HD], ggk[:, 0:HD]
    pieces = [dq, da, dg, dga, dgc, dkv]
    names = ["q", "a", "g", "ga", "gc", "kv"]
    gw = {nm: _wgrad(u, p, "wgrad_in_" + nm, tn=p.shape[1] if p.shape[1] < 1024 else 1024) for nm, p in zip(names, pieces)}
    g["w_in"] = jnp.concatenate([gw["q"], gw["kv"], gw["a"], gw["g"], gw["ga"], gw["gc"]], axis=1)
    grad_x, g["norm_mix_g"] = _inproj_bwd(pieces, w["w_in"], x, dx1, w["norm_mix_g"])
    return loss[0, 0], grad_x, g


ANY = pl.BlockSpec(memory_space=pl.ANY)
BIG = ["w_in", "w_attn_o", "w_conv_out", "w_out", "w_ff1", "w_ff2"]
SHARD_AXIS = {"w_in": 1, "w_attn_o": 0, "w_conv_out": 0, "w_out": 0, "w_ff1": 1, "w_ff2": 0, "w_dw": 1}
SHARD_SHAPE = {"w_in": (D, INW // 4), "w_attn_o": (D // 4, D), "w_conv_out": (D // 4, D), "w_out": (D // 4, D),
               "w_ff1": (D, DFF // 4), "w_ff2": (DFF // 4, D), "w_dw": (HALO, D // 4)}


def _position():
    x, y, c = lax.axis_index("x"), lax.axis_index("y"), lax.axis_index("c")
    other_chips = [(1 - x, y), (x, 1 - y), (1 - x, 1 - y)]
    return x, y, c, 2 * x + y, other_chips


def _shard_window(name, full_ref, s, half=None):
    R, C = SHARD_SHAPE[name]
    r0, nr = (0, R) if half is None else (half * (R // 2), R // 2)
    if SHARD_AXIS[name] == 1:
        return full_ref.at[pl.ds(r0, nr), pl.ds(s * C, C)]
    return full_ref.at[pl.ds(s * R + r0, nr), :]


def _remote(src, dst, send_sems, recv_sems, k, device):
    return pltpu.make_async_remote_copy(src_ref=src, dst_ref=dst, send_sem=send_sems.at[k], recv_sem=recv_sems.at[k],
                                        device_id=device, device_id_type=MESH)


def _full_shape(nm):
    R, C = SHARD_SHAPE[nm]
    return (R, 4 * C) if SHARD_AXIS[nm] == 1 else (4 * R, C)


def _place_shard(nm, shard, chip_arr, dtype):
    R, C = SHARD_SHAPE[nm]
    tr = min(R, 256)
    if SHARD_AXIS[nm] == 1:
        o_map = lambda i, ch: (i, ch[0])
    else:
        o_map = lambda i, ch: (ch[0] * (R // tr) + i, 0)

    def body(ch_ref, s_ref, o_ref):
        o_ref[...] = s_ref[...].astype(dtype)

    return pl.pallas_call(
        body,
        grid_spec=pltpu.PrefetchScalarGridSpec(
            num_scalar_prefetch=1, grid=(R // tr,),
            in_specs=[pl.BlockSpec((tr, C), lambda i, ch: (i, 0))], out_specs=pl.BlockSpec((tr, C), o_map)),
        out_shape=SDS(_full_shape(nm), dtype), name="place_" + nm,
        compiler_params=_cparams(("parallel",), 32))(chip_arr, shard)


def _gather_weights(placed):
    names = list(placed)
    n = len(names)

    def body(*refs):
        dsts = dict(zip(names, refs[n:2 * n]))
        send_sems, recv_sems = refs[2 * n:]
        x, y, c, chip, chips = _position()
        sibling = (x, y, 1 - c)
        sent = []
        for a, nm in enumerate(names):
            mine = _shard_window(nm, dsts[nm], chip, c)
            for j, (cx, cy) in enumerate(chips):
                cp = _remote(mine, mine, send_sems, recv_sems, 6 * a + j, (cx, cy, c))
                cp.start()
                sent.append(cp)
        for a, nm in enumerate(names):
            for j, (cx, cy) in enumerate(chips):
                w = _shard_window(nm, dsts[nm], 2 * cx + cy, c)
                _remote(w, w, send_sems, recv_sems, 6 * a + j, (cx, cy, c)).wait_recv()
                cp = _remote(w, w, send_sems, recv_sems, 6 * a + 3 + j, sibling)
                cp.start()
                sent.append(cp)
        for a, nm in enumerate(names):
            for j, (cx, cy) in enumerate(chips):
                w = _shard_window(nm, dsts[nm], 2 * cx + cy, 1 - c)
                _remote(w, w, send_sems, recv_sems, 6 * a + 3 + j, sibling).wait_recv()
        for cp in sent:
            cp.wait_send()

    out = pl.pallas_call(
        body, in_specs=[ANY] * n, out_specs=[ANY] * n,
        out_shape=[SDS(placed[nm].shape, placed[nm].dtype) for nm in names],
        input_output_aliases={a: a for a in range(n)},
        scratch_shapes=[pltpu.SemaphoreType.DMA((6 * n,)), pltpu.SemaphoreType.DMA((6 * n,))],
        name="gather_weights")(*[placed[nm] for nm in names])
    return dict(zip(names, out))


def _half_rows(nm):
    return SHARD_SHAPE[nm][0] // 2


RS_TILE = 128


def _pair_exchange(grads):
    def body(*refs):
        g = dict(zip(BIG, refs[:6]))
        got = dict(zip(BIG, refs[6:12]))
        send_sems, recv_sems = refs[12:]
        x, y, c, chip, chips = _position()
        cps = []
        for a, nm in enumerate(BIG):
            for s in range(4):
                cps.append(_remote(_shard_window(nm, g[nm], s, 1 - c), got[nm].at[s], send_sems, recv_sems, 4 * a + s,
                                   (x, y, 1 - c)))
        for cp in cps:
            cp.start()
        for cp in cps:
            cp.wait()

    out = pl.pallas_call(
        body, in_specs=[ANY] * 6, out_specs=[ANY] * 6,
        out_shape=[SDS((4, _half_rows(nm), SHARD_SHAPE[nm][1]), f32) for nm in BIG],
        scratch_shapes=[pltpu.SemaphoreType.DMA((24,)), pltpu.SemaphoreType.DMA((24,))],
        name="rs_pair_exchange")(*[grads[nm] for nm in BIG])
    return dict(zip(BIG, out))


def _pair_sum(nm, g, got, chip_core):
    R, C = SHARD_SHAPE[nm]
    hr = R // 2
    nt = hr // RS_TILE
    if SHARD_AXIS[nm] == 1:
        g_map = lambda i, s, sc: (sc[1] * nt + i, s)
    else:
        g_map = lambda i, s, sc: (s * (R // RS_TILE) + sc[1] * nt + i, 0)

    def body(sc_ref, g_ref, got_ref, o16_ref, own_ref):
        v = g_ref[...] + got_ref[0]
        o16_ref[0] = v.astype(bf16)

        @pl.when(pl.program_id(1) == sc_ref[0])
        def _():
            own_ref[...] = v

    blk3 = pl.BlockSpec((1, RS_TILE, C), lambda i, s, sc: (s, i, 0))
    return pl.pallas_call(
        body,
        grid_spec=pltpu.PrefetchScalarGridSpec(
            num_scalar_prefetch=1, grid=(nt, 4),
            in_specs=[pl.BlockSpec((RS_TILE, C), g_map), blk3],
            out_specs=[blk3, pl.BlockSpec((RS_TILE, C), lambda i, s, sc: (i, 0))]),
        out_shape=[SDS((4, hr, C), bf16), SDS((hr, C), f32)], name="rs_pair_sum_" + nm,
        compiler_params=_cparams(("parallel", "arbitrary"), 32))(chip_core, g, got)


def _chip_exchange(cp):
    def body(*refs):
        src = dict(zip(BIG, refs[:6]))
        dst = dict(zip(BIG, refs[6:12]))
        send_sems, recv_sems = refs[12:]
        x, y, c, chip, chips = _position()
        cps = []
        for a, nm in enumerate(BIG):
            for j, (cx, cy) in enumerate(chips):
                cps.append(_remote(src[nm].at[2 * cx + cy], dst[nm].at[j], send_sems, recv_sems, 3 * a + j, (cx, cy, c)))
        for cp_ in cps:
            cp_.start()
        for cp_ in cps:
            cp_.wait()

    out = pl.pallas_call(
        body, in_specs=[ANY] * 6, out_specs=[ANY] * 6,
        out_shape=[SDS((3, _half_rows(nm), SHARD_SHAPE[nm][1]), bf16) for nm in BIG],
        scratch_shapes=[pltpu.SemaphoreType.DMA((18,)), pltpu.SemaphoreType.DMA((18,))],
        name="rs_chip_exchange")(*[cp[nm] for nm in BIG])
    return dict(zip(BIG, out))


def _chip_sum(nm, own, rc, chip_core):
    R, C = SHARD_SHAPE[nm]
    nt = (R // 2) // RS_TILE

    def body(sc_ref, own_ref, rc_ref, o_ref):
        o_ref[...] = own_ref[...] + rc_ref[0].astype(f32) + rc_ref[1].astype(f32) + rc_ref[2].astype(f32)

    return pl.pallas_call(
        body,
        grid_spec=pltpu.PrefetchScalarGridSpec(
            num_scalar_prefetch=1, grid=(nt,),
            in_specs=[pl.BlockSpec((RS_TILE, C), lambda i, sc: (i, 0)),
                      pl.BlockSpec((3, RS_TILE, C), lambda i, sc: (0, i, 0))],
            out_specs=pl.BlockSpec((RS_TILE, C), lambda i, sc: (sc[1] * nt + i, 0))),
        out_shape=SDS((R, C), f32), name="rs_chip_sum_" + nm,
        compiler_params=_cparams(("parallel",), 32))(chip_core, own, rc)


def _pair_share(tot):
    def body(*refs):
        outs = dict(zip(BIG, refs[6:12]))
        send_sems, recv_sems = refs[12:]
        x, y, c, chip, chips = _position()
        cps = []
        for a, nm in enumerate(BIG):
            hr = _half_rows(nm)
            mine = outs[nm].at[pl.ds(c * hr, hr), :]
            cps.append(_remote(mine, mine, send_sems, recv_sems, a, (x, y, 1 - c)))
        for cp in cps:
            cp.start()
        for cp in cps:
            cp.wait()

    out = pl.pallas_call(
        body, in_specs=[ANY] * 6, out_specs=[ANY] * 6, out_shape=[SDS(SHARD_SHAPE[nm], f32) for nm in BIG],
        input_output_aliases={a: a for a in range(6)},
        scratch_shapes=[pltpu.SemaphoreType.DMA((6,)), pltpu.SemaphoreType.DMA((6,))],
        name="rs_pair_share")(*[tot[nm] for nm in BIG])
    return dict(zip(BIG, out))


def _reduce_scatter(grads, chip):
    chip_core = jnp.stack([chip, lax.axis_index("c")]).astype(jnp.int32)
    got = _pair_exchange(grads)
    cp, own = {}, {}
    for nm in BIG:
        cp[nm], own[nm] = _pair_sum(nm, grads[nm], got[nm], chip_core)
    rc = _chip_exchange(cp)
    tot = {nm: _chip_sum(nm, own[nm], rc[nm], chip_core) for nm in BIG}
    return _pair_share(tot)


SMALL_ROWS = 40


def _allreduce_small(block):
    def body(x_ref, out_ref, buf, send_sems, recv_sems, local_sem):
        x, y, c, chip, chips = _position()
        me, sibling = (x, y, c), (x, y, 1 - c)

        def slot(px, py, pc):
            return buf.at[4 * px + 2 * py + pc]

        def copy(k, block_of, to, src=None):
            return _remote(slot(*block_of) if src is None else src, slot(*block_of), send_sems, recv_sems, k, to)

        mine = pltpu.make_async_copy(x_ref, slot(*me), local_sem)
        mine.start()
        first = [copy(0, me, sibling, src=x_ref)] + [copy(1 + j, me, (*ch, c), src=x_ref) for j, ch in enumerate(chips)]
        for cp in first:
            cp.start()
        passed = [copy(4 + j, (*ch, c), sibling) for j, ch in enumerate(chips)]
        for j, ch in enumerate(chips):
            copy(1 + j, (*ch, c), me).wait_recv()
            passed[j].start()
        copy(0, sibling, me).wait_recv()
        for j, ch in enumerate(chips):
            copy(4 + j, (*ch, 1 - c), me).wait_recv()
        for cp in first + passed:
            cp.wait_send()
        mine.wait()
        acc = buf[0]
        for d in range(1, 8):
            acc = acc + buf[d]
        out_ref[...] = acc

    vm = pl.BlockSpec(memory_space=pltpu.VMEM)
    return pl.pallas_call(
        body, in_specs=[vm], out_specs=vm, out_shape=SDS((SMALL_ROWS, D), f32),
        scratch_shapes=[pltpu.VMEM((8, SMALL_ROWS, D), f32), pltpu.SemaphoreType.DMA((7,)), pltpu.SemaphoreType.DMA((7,)),
                        pltpu.SemaphoreType.DMA],
        name="allreduce_small")(block)


def _adamw(w, g, m, v, name):
    rows, cols = w.shape
    tr = 256 if rows % 256 == 0 else rows

    def body(w_ref, g_ref, m_ref, v_ref, d_ref, nm_ref, nv_ref):
        gv = g_ref[...]
        m2 = ADAM_B1 * m_ref[...] + (1.0 - ADAM_B1) * gv
        v2 = ADAM_B2 * v_ref[...] + (1.0 - ADAM_B2) * jnp.square(gv)
        m_hat = m2 / (1.0 - ADAM_B1 ** ADAM_STEP)
        v_hat = v2 / (1.0 - ADAM_B2 ** ADAM_STEP)
        d_ref[...] = -ADAM_LR * (m_hat / (jnp.sqrt(v_hat) + ADAM_EPS) + ADAM_WD * w_ref[...])
        nm_ref[...] = m2
        nv_ref[...] = v2

    spec = pl.BlockSpec((tr, cols), lambda i: (i, 0))
    return pl.pallas_call(body, grid=(rows // tr,), in_specs=[spec] * 4, out_specs=[spec] * 3,
                          out_shape=[SDS((rows, cols), f32)] * 3, name=name,
                          compiler_params=_cparams(("parallel",), 40))(w, g, m, v)


WEIGHTS = ["norm_mix_g", "w_in", "q_norm_g", "k_norm_g", "attn_sinks", "rel_bias", "w_attn_o", "w_dw", "b_dw",
           "conv_ln_g", "conv_ln_b", "w_conv_out", "w_out", "norm_mlp_g", "w_ff1", "w_ff2"]
ROW_VECS = ["norm_mix_g", "b_dw", "conv_ln_g", "conv_ln_b", "norm_mlp_g"]
MISC_ROW = 5
W_DW_ROW = 8


def _pack_small(vals, loss=None):
    misc = [vals["q_norm_g"].reshape(1, HD), vals["k_norm_g"].reshape(1, HD), vals["attn_sinks"].reshape(1, NQ),
            jnp.zeros((1, 1), f32) if loss is None else loss.reshape(1, 1), jnp.zeros((1, 111), f32),
            vals["rel_bias"].reshape(1, NBUCKET * NQ), jnp.zeros((1, 256), f32)]
    rows = [vals[nm].reshape(1, D) for nm in ROW_VECS] + [jnp.concatenate(misc, axis=1), jnp.zeros((2, D), f32)]
    return jnp.concatenate(rows, axis=0)


def _unpack_small(block):
    out = {nm: block[i:i + 1] for i, nm in enumerate(ROW_VECS)}
    misc = block[MISC_ROW]
    out["q_norm_g"] = misc[0:64].reshape(1, HD)
    out["k_norm_g"] = misc[64:128].reshape(1, HD)
    out["attn_sinks"] = misc[128:144].reshape(1, NQ)
    out["rel_bias"] = misc[256:768].reshape(NBUCKET, NQ)
    return out, misc[144]


def kernel(x, norm_mix_g, w_in, q_norm_g, k_norm_g, attn_sinks, rel_bias, w_attn_o, w_dw, b_dw, conv_ln_g, conv_ln_b, w_conv_out, w_out, norm_mlp_g, w_ff1, w_ff2, loss_target, m_norm_mix_g, m_w_in, m_q_norm_g, m_k_norm_g, m_attn_sinks, m_rel_bias, m_w_attn_o, m_w_dw, m_b_dw, m_conv_ln_g, m_conv_ln_b, m_w_conv_out, m_w_out, m_norm_mlp_g, m_w_ff1, m_w_ff2, v_norm_mix_g, v_w_in, v_q_norm_g, v_k_norm_g, v_attn_sinks, v_rel_bias, v_w_attn_o, v_w_dw, v_b_dw, v_conv_ln_g, v_conv_ln_b, v_w_conv_out, v_w_out, v_norm_mlp_g, v_w_ff1, v_w_ff2):
    args = dict(locals())
    wts = {nm: args[nm] for nm in WEIGHTS}
    mom = {nm: args["m_" + nm] for nm in WEIGHTS}
    var = {nm: args["v_" + nm] for nm in WEIGHTS}
    chip = 2 * lax.axis_index("x") + lax.axis_index("y")

    chip_arr = jnp.reshape(chip, (1,)).astype(jnp.int32)
    placed = {nm: _place_shard(nm, wts[nm][0], chip_arr, bf16) for nm in BIG}
    placed["w_dw"] = _place_shard("w_dw", jnp.pad(w_dw[0], ((0, 1), (0, 0))), chip_arr, f32)
    full = _gather_weights(placed)
    full["w_in"] = _to_internal_cols(full["w_in"])
    for nm in WEIGHTS:
        if nm not in full:
            full[nm] = wts[nm]

    loss_part, grad_x, g = _local_step(x[0], loss_target[0], full)

    small = jnp.concatenate([_pack_small(g, loss_part), g["w_dw"]], axis=0)
    small = _allreduce_small(small)
    grads, loss = _unpack_small(small)
    grads["w_dw"] = lax.dynamic_slice(small[W_DW_ROW:W_DW_ROW + CW], (0, chip * (D // 4)), (CW, D // 4))
    grads.update(_reduce_scatter(g, chip))

    delta, new_m, new_v = {}, {}, {}
    sd, sm, sv = _adamw(_pack_small(wts), small[0:8], _pack_small(mom), _pack_small(var), "adamw_small")
    for res, blk in ((delta, sd), (new_m, sm), (new_v, sv)):
        res.update(_unpack_small(blk)[0])
    for nm in BIG + ["w_dw"]:
        shp = wts[nm].shape
        two_d = lambda a: a.reshape(shp[-2], shp[-1])
        delta[nm], new_m[nm], new_v[nm] = _adamw(two_d(wts[nm]), grads[nm], two_d(mom[nm]), two_d(var[nm]), "adamw_" + nm)

    def shaped(vals):
        return [vals[nm].reshape(wts[nm].shape) for nm in WEIGHTS]

    return (loss, grad_x[None], *shaped(grads), *shaped(delta), *shaped(new_m), *shaped(new_v))
```

```python
import functools

import numpy as np
import jax
import jax.numpy as jnp
from jax import lax
from jax.experimental import pallas as pl
from jax.experimental.pallas import tpu as pltpu

f32 = jnp.float32
bf16 = jnp.bfloat16
SDS = jax.ShapeDtypeStruct
MESH = pl.DeviceIdType.MESH

D = 1024
HD = 64
NQ = 16
NKV = 4
BLK = 128
CW = 31
HALO = 32
DFF = 4096
NBUCKET = 32
EPS = 1e-6
NEG = -1e30
INW = 5632
C_Q, C_A, C_G, C_GA, C_GC = 0, 1, 2, 3, 4
C_KV = 10

ADAM_LR = 0.001
ADAM_B1 = 0.9
ADAM_B2 = 0.999
ADAM_EPS = 1e-08
ADAM_WD = 0.01
ADAM_STEP = 10

VMEM_BYTES_V7X = 64 << 20


def _cparams(sem, vmem_mb):
    assert (vmem_mb << 20) < VMEM_BYTES_V7X
    return pltpu.CompilerParams(dimension_semantics=sem, vmem_limit_bytes=vmem_mb << 20)


ANY = pl.BlockSpec(memory_space=pl.ANY)


class _Comm:
    def __init__(self, ins, out_shapes, n_sems, start, finish, mid=None, aliases=None):
        self.ins, self.out_shapes, self.n_sems = list(ins), list(out_shapes), n_sems
        self.start, self.finish, self.mid, self.aliases = start, finish, mid, dict(aliases or {})


def _call(body, args, *, grid, in_specs, out_specs, out_shape, name, sem, vmem_mb, scratch_shapes=(), comm=None,
          mid_step=None):
    n_in, n_out, n_scr = len(in_specs), len(out_specs), len(scratch_shapes)
    if comm is None:
        outs = pl.pallas_call(body, grid=grid, in_specs=list(in_specs), out_specs=list(out_specs),
                              out_shape=list(out_shape), scratch_shapes=list(scratch_shapes), name=name,
                              compiler_params=_cparams(sem, vmem_mb))(*args)
        return list(outs), []
    ci, co = len(comm.ins), len(comm.out_shapes)
    last = grid[0] - 1

    def wrapped(*refs):
        ins, cin = refs[:n_in], refs[n_in:n_in + ci]
        outs = refs[n_in + ci:n_in + ci + n_out]
        cout = refs[n_in + ci + n_out:n_in + ci + n_out + co]
        scr = refs[n_in + ci + n_out + co:]
        send, recv = scr[n_scr], scr[n_scr + 1]
        step = pl.program_id(0)

        @pl.when(step == 0)
        def _():
            comm.start(cin, cout, send, recv)

        body(*ins, *outs, *scr[:n_scr])
        if comm.mid is not None:
            @pl.when(step == mid_step)
            def _():
                comm.mid(cin, cout, send, recv)

        @pl.when(step == last)
        def _():
            comm.finish(cin, cout, send, recv)

    res = pl.pallas_call(
        wrapped, grid=grid, in_specs=list(in_specs) + [ANY] * ci, out_specs=list(out_specs) + [ANY] * co,
        out_shape=list(out_shape) + comm.out_shapes,
        input_output_aliases={n_in + k: n_out + v for k, v in comm.aliases.items()},
        scratch_shapes=list(scratch_shapes) + [pltpu.SemaphoreType.DMA((comm.n_sems,))] * 2,
        name=name, compiler_params=_cparams(("arbitrary",), vmem_mb))(*args, *comm.ins)
    return list(res[:n_out]), list(res[n_out:])


def _run_comm(comm, name):
    ci, co = len(comm.ins), len(comm.out_shapes)

    def body(*refs):
        cin, cout, (send, recv) = refs[:ci], refs[ci:ci + co], refs[ci + co:]
        comm.start(cin, cout, send, recv)
        if comm.mid is not None:
            comm.mid(cin, cout, send, recv)
        comm.finish(cin, cout, send, recv)

    return pl.pallas_call(
        body, in_specs=[ANY] * ci, out_specs=[ANY] * co, out_shape=comm.out_shapes, input_output_aliases=comm.aliases,
        scratch_shapes=[pltpu.SemaphoreType.DMA((comm.n_sems,))] * 2, name=name)(*comm.ins)


def _dot(a, b):
    return jnp.dot(a, b, preferred_element_type=f32)


def _dot_nt(a, b):
    return lax.dot_general(a, b, (((1,), (1,)), ((), ())), preferred_element_type=f32)


def _dot_tn(a, b):
    return lax.dot_general(a, b, (((0,), (0,)), ((), ())), preferred_element_type=f32)


def _sigmoid(x):
    return 1.0 / (1.0 + jnp.exp(-x))


def _low_head_lanes():
    return lax.broadcasted_iota(jnp.int32, (1, 2 * HD), 1) < HD


def _head_blockdiag():
    r = lax.broadcasted_iota(jnp.int32, (2 * HD, 2 * HD), 0) // HD
    c = lax.broadcasted_iota(jnp.int32, (2 * HD, 2 * HD), 1) // HD
    return jnp.where(r == c, 1.0, 0.0).astype(bf16)


def _head_sums(z, bd):
    hi = z.astype(bf16)
    lo = (z - hi.astype(f32)).astype(bf16)
    return _dot(hi, bd) + _dot(lo, bd)


def _rms_inproj(x, g, w):
    T, N = x.shape[0], w.shape[1]
    tm, tn = 512, 512

    def body(x_ref, g_ref, w_ref, p_ref, u_ref):
        xv = x_ref[...]
        r = lax.rsqrt(jnp.mean(xv * xv, axis=-1, keepdims=True) + EPS)
        u = (xv * r * g_ref[...]).astype(bf16)
        u_ref[...] = u
        for c in range(N // tn):
            p_ref[:, c * tn:(c + 1) * tn] = _dot(u, w_ref[:, c * tn:(c + 1) * tn])

    return pl.pallas_call(
        body, grid=(T // tm,),
        in_specs=[pl.BlockSpec((tm, D), lambda i: (i, 0)),
                  pl.BlockSpec((1, D), lambda i: (0, 0)),
                  pl.BlockSpec((D, N), lambda i: (0, 0), pipeline_mode=pl.Buffered(1))],
        out_specs=[pl.BlockSpec((tm, N), lambda i: (i, 0)),
                   pl.BlockSpec((tm, D), lambda i: (i, 0))],
        out_shape=[SDS((T, N), f32), SDS((T, D), bf16)],
        name="rms_inproj", compiler_params=_cparams(("parallel",), 48))(x, g, w)


def _split_pair(pair, out_ref, p, lo):
    rolled = pltpu.roll(pair, HD, 1)
    zero = jnp.zeros_like(pair)
    c = 512 * p
    out_ref[:, c:c + 128] = jnp.where(lo, pair, zero).astype(bf16)
    out_ref[:, c + 128:c + 256] = jnp.where(lo, zero, rolled).astype(bf16)
    out_ref[:, c + 256:c + 384] = jnp.where(lo, rolled, zero).astype(bf16)
    out_ref[:, c + 384:c + 512] = jnp.where(lo, zero, pair).astype(bf16)


def _qk_prep(proj, gq2, gk2):
    T = proj.shape[0]
    tm = 512

    def body(q_ref, kv_ref, gq_ref, gk_ref, qn_ref, kk_ref, vv_ref):
        bd = _head_blockdiag()
        lo = _low_head_lanes()
        for p in range(NQ // 2):
            z = q_ref[:, 128 * p:128 * p + 128]
            r = lax.rsqrt(_head_sums(z * z, bd) * (1.0 / HD) + EPS)
            qn_ref[:, 128 * p:128 * p + 128] = (z * r * gq_ref[...] * (HD ** -0.5)).astype(bf16)
        for p in range(NKV // 2):
            z = kv_ref[:, 128 * p:128 * p + 128]
            r = lax.rsqrt(_head_sums(z * z, bd) * (1.0 / HD) + EPS)
            _split_pair(z * r * gk_ref[...], kk_ref, p, lo)
            _split_pair(kv_ref[:, 256 + 128 * p:256 + 128 * p + 128], vv_ref, p, lo)

    return pl.pallas_call(
        body, grid=(T // tm,),
        in_specs=[pl.BlockSpec((tm, D), lambda i: (i, C_Q)),
                  pl.BlockSpec((tm, 512), lambda i: (i, C_KV)),
                  pl.BlockSpec((1, 128), lambda i: (0, 0)),
                  pl.BlockSpec((1, 128), lambda i: (0, 0))],
        out_specs=[pl.BlockSpec((tm, D), lambda i: (i, 0))] * 3,
        out_shape=[SDS((T, D), bf16)] * 3,
        name="qk_prep", compiler_params=_cparams(("parallel",), 32))(proj, proj, gq2, gk2)


def _bucket_tile():
    qi = np.arange(BLK)[:, None]
    kj = np.arange(2 * BLK)[None, :]
    dist = qi + BLK - kj
    n = np.maximum(dist, 0)
    max_exact = NBUCKET // 2
    nf = np.maximum(n, 1).astype(np.float32)
    large = max_exact + (np.log(nf / max_exact) / np.float32(np.log(128 / max_exact))
                         * (NBUCKET - max_exact)).astype(np.int32)
    large = np.minimum(large, NBUCKET - 1)
    bucket = np.where(n < max_exact, n, large)
    valid = (dist >= 0) & (dist < BLK)
    return np.where(valid, bucket, -1).astype(np.int32)


def _bias_tiles(rel_bias):
    def body(rb_ref, bk_ref, out_ref):
        bk = bk_ref[...]
        for h in range(NQ):
            acc = jnp.full((BLK, 2 * BLK), NEG, f32)
            for b in range(NBUCKET):
                acc = jnp.where(bk == b, rb_ref[b, h], acc)
            out_ref[h] = acc

    return pl.pallas_call(
        body,
        in_specs=[pl.BlockSpec(memory_space=pltpu.SMEM), pl.BlockSpec(memory_space=pltpu.VMEM)],
        out_specs=pl.BlockSpec(memory_space=pltpu.VMEM),
        out_shape=SDS((NQ, BLK, 2 * BLK), f32),
        name="bias_tiles")(rel_bias, jnp.asarray(_bucket_tile()))


def _rows2(ref, c):
    return jnp.concatenate([ref[:, c:c + 128], ref[:, c + 128:c + 256]], axis=0)


def _attn_fwd(qn, kk, vv, bias, sinks, comm=None):
    T = qn.shape[0]
    nb = T // BLK

    def body(s_ref, q_ref, kc_ref, kp_ref, vc_ref, vp_ref, b_ref, o_ref, lse_ref):
        has_prev = pl.program_id(0) > 0
        for h in range(NKV):
            c = 256 * h
            qs = _rows2(q_ref, c)
            sc = _dot_nt(qs, _rows2(kc_ref, c))
            sp = _dot_nt(qs, _rows2(kp_ref, c))
            vstack = jnp.concatenate([vp_ref[:, c:c + 128], vc_ref[:, c:c + 128],
                                      vp_ref[:, c + 128:c + 256], vc_ref[:, c + 128:c + 256]], axis=0)
            for pr in range(2):
                ps = []
                for e in range(2):
                    hq = 4 * h + 2 * pr + e
                    rows, cols = slice(128 * pr, 128 * pr + 128), slice(128 * e, 128 * e + 128)
                    s_p = jnp.where(has_prev, sp[rows, cols] + b_ref[hq, :, 0:BLK], NEG)
                    s_c = sc[rows, cols] + b_ref[hq, :, BLK:2 * BLK]
                    sink = s_ref[0, hq]
                    m = jnp.maximum(jnp.maximum(jnp.max(s_p, axis=-1, keepdims=True),
                                                jnp.max(s_c, axis=-1, keepdims=True)), sink)
                    e_p = jnp.exp(s_p - m)
                    e_c = jnp.exp(s_c - m)
                    l = (jnp.sum(e_p, axis=-1, keepdims=True) + jnp.sum(e_c, axis=-1, keepdims=True)
                         + jnp.exp(sink - m))
                    inv = 1.0 / l
                    ps += [(e_p * inv).astype(bf16), (e_c * inv).astype(bf16)]
                    lse_ref[:, hq:hq + 1] = m + jnp.log(l)
                o_ref[:, c + 128 * pr:c + 128 * pr + 128] = _dot(jnp.concatenate(ps, axis=1), vstack).astype(bf16)

    blk = lambda f: pl.BlockSpec((BLK, D), f)
    cur = lambda n: (n, 0)
    prev = lambda n: (jnp.maximum(n - 1, 0), 0)
    return _call(
        body, (sinks, qn, kk, kk, vv, vv, bias), grid=(nb,),
        in_specs=[pl.BlockSpec(memory_space=pltpu.SMEM), blk(cur), blk(cur), blk(prev), blk(cur), blk(prev),
                  pl.BlockSpec((NQ, BLK, 2 * BLK), lambda n: (0, 0, 0))],
        out_specs=[blk(cur), pl.BlockSpec((BLK, NQ), cur)],
        out_shape=[SDS((T, D), bf16), SDS((T, NQ), f32)],
        name="attn_fwd", sem=("parallel",), vmem_mb=32, comm=comm, mid_step=(3 * nb) // 4)


SUB = 8


def _shifted_copies(sh_ref, rows):
    for b in range(1, SUB):
        sh_ref[b, 0:rows, :] = sh_ref[0, pl.ds(b, rows), :]


def _shifted_rows(sh_ref, r0, offset, rows, ln):
    a, b = divmod(offset, SUB)
    return sh_ref[b, pl.ds(pl.multiple_of(r0 + SUB * a, SUB), rows), ln]


def _conv_taps(sh_ref, w_ref, r0, rows, offset_of_tap, init):
    halves = []
    for hf in range(2):
        ln = slice(512 * hf, 512 * hf + 512)
        acc = init(ln)
        for j in range(CW):
            acc = acc + _shifted_rows(sh_ref, r0, offset_of_tap(j), rows, ln) * w_ref[j:j + 1, ln]
        halves.append(acc)
    return halves


def _glu_conv_fwd(proj, w_dw, b_dw, ln_g, ln_b):
    T = proj.shape[0]
    tt, ch = 256, 32

    def body(a_ref, g_ref, ah_ref, gh_ref, w_ref, b_ref, lg_ref, lb_ref, h1_ref, h3_ref, sh):
        i = pl.program_id(0)
        halo = ah_ref[...] * _sigmoid(gh_ref[...])
        sh[0, 0:HALO, :] = jnp.where(i > 0, halo, 0.0)
        sh[0, HALO:HALO + tt, :] = a_ref[...] * _sigmoid(g_ref[...])
        _shifted_copies(sh, tt + HALO - SUB)

        def chunk(c, carry):
            r0 = pl.multiple_of(c * ch, ch)
            lo, hi = _conv_taps(sh, w_ref, r0, ch, lambda j: HALO - (CW - 1) + j,
                                lambda ln: jnp.broadcast_to(b_ref[:, ln], (ch, 512)))
            h1_ref[pl.ds(r0, ch), 0:512] = lo
            h1_ref[pl.ds(r0, ch), 512:1024] = hi
            return carry

        lax.fori_loop(0, tt // ch, chunk, 0)
        h1 = h1_ref[...]
        mu = jnp.mean(h1, axis=-1, keepdims=True)
        xc = h1 - mu
        var = jnp.mean(xc * xc, axis=-1, keepdims=True)
        h2 = xc * lax.rsqrt(var + EPS) * lg_ref[...] + lb_ref[...]
        h3_ref[...] = (h2 * _sigmoid(h2)).astype(bf16)

    hpt = tt // HALO
    tile = lambda cb: pl.BlockSpec((tt, D), lambda i: (i, cb))
    halo = lambda cb: pl.BlockSpec((HALO, D), lambda i: (jnp.maximum(i * hpt - 1, 0), cb))
    vec = pl.BlockSpec((1, D), lambda i: (0, 0))
    return pl.pallas_call(
        body, grid=(T // tt,),
        in_specs=[tile(C_A), tile(C_G), halo(C_A), halo(C_G), pl.BlockSpec((HALO, D), lambda i: (0, 0)), vec, vec, vec],
        out_specs=[pl.BlockSpec((tt, D), lambda i: (i, 0))] * 2,
        out_shape=[SDS((T, D), f32), SDS((T, D), bf16)],
        scratch_shapes=[pltpu.VMEM((SUB, HALO + tt, D), f32)],
        name="glu_conv_fwd", compiler_params=_cparams(("parallel",), 32))(proj, proj, proj, proj, w_dw, b_dw, ln_g, ln_b)


def _mix_out(o, h3, proj, x, w_attn_o, w_conv_out, w_out, g_mlp):
    T = x.shape[0]
    tm = 512

    def body(o_ref, h3_ref, ga_ref, gc_ref, x_ref, wa_ref, wc_ref, wo_ref, g_ref,
             attn_ref, conv_ref, mg_ref, x1_ref, n2_ref):
        attn = _dot(o_ref[...], wa_ref[...])
        conv = _dot(h3_ref[...], wc_ref[...])
        attn_ref[...] = attn
        conv_ref[...] = conv
        mg = (_sigmoid(ga_ref[...]) * attn + _sigmoid(gc_ref[...]) * conv).astype(bf16)
        mg_ref[...] = mg
        x1 = x_ref[...] + _dot(mg, wo_ref[...])
        x1_ref[...] = x1
        r = lax.rsqrt(jnp.mean(x1 * x1, axis=-1, keepdims=True) + EPS)
        n2_ref[...] = (x1 * r * g_ref[...]).astype(bf16)

    tile = lambda cb=0: pl.BlockSpec((tm, D), lambda i: (i, cb))
    wfull = pl.BlockSpec((D, D), lambda i: (0, 0))
    return pl.pallas_call(
        body, grid=(T // tm,),
        in_specs=[tile(), tile(), tile(C_GA), tile(C_GC), tile(), wfull, wfull, wfull,
                  pl.BlockSpec((1, D), lambda i: (0, 0))],
        out_specs=[tile()] * 5,
        out_shape=[SDS((T, D), f32), SDS((T, D), f32), SDS((T, D), bf16), SDS((T, D), f32), SDS((T, D), bf16)],
        name="mix_out", compiler_params=_cparams(("parallel",), 48))(o, h3, proj, proj, x, w_attn_o, w_conv_out, w_out, g_mlp)


def _mlp_fwd(n2, w1, w2, x1, tgt):
    T = n2.shape[0]
    tm, tf = 512, 1024

    def body(n2_ref, w1_ref, w2_ref, x1_ref, t_ref, hm_ref, dy_ref, dyb_ref, loss_ref):
        @pl.when(pl.program_id(0) == 0)
        def _():
            loss_ref[...] = jnp.zeros_like(loss_ref)

        n2v = n2_ref[...]
        for c in range(DFF // tf):
            r = jnp.maximum(_dot(n2v, w1_ref[:, c * tf:(c + 1) * tf]), 0.0)
            hm_ref[:, c * tf:(c + 1) * tf] = (r * r).astype(bf16)
        e = x1_ref[...] + _dot(hm_ref[...], w2_ref[...]) - t_ref[...]
        dy = e * (1.0 / D)
        dy_ref[...] = dy
        dyb_ref[...] = dy.astype(bf16)
        loss_ref[...] += 0.5 * jnp.sum(jnp.sum(e * e, axis=-1, keepdims=True) * (1.0 / D))

    row = pl.BlockSpec((tm, D), lambda i: (i, 0))
    once = pl.Buffered(1)
    return pl.pallas_call(
        body, grid=(T // tm,),
        in_specs=[row, pl.BlockSpec((D, DFF), lambda i: (0, 0), pipeline_mode=once),
                  pl.BlockSpec((DFF, D), lambda i: (0, 0), pipeline_mode=once), row, row],
        out_specs=[pl.BlockSpec((tm, DFF), lambda i: (i, 0)), row, row, pl.BlockSpec((8, 128), lambda i: (0, 0))],
        out_shape=[SDS((T, DFF), bf16), SDS((T, D), f32), SDS((T, D), bf16), SDS((8, 128), f32)],
        name="mlp_fwd", compiler_params=_cparams(("arbitrary",), 56))(n2, w1, w2, x1, tgt)


def _rms_bwd(xv, g, dn, dres):
    r = lax.rsqrt(jnp.mean(xv * xv, axis=-1, keepdims=True) + EPS)
    gd = dn * g
    dx = dres + r * gd - xv * (r * r * r) * jnp.mean(xv * gd, axis=-1, keepdims=True)
    dg = jnp.sum(dn * xv * r, axis=0, keepdims=True)
    return dx, dg


def _mlp_bwd(dy, dyb, hmid, w1, w2, x1, g_mlp):
    T = dy.shape[0]
    tm, tf = 512, 1024

    def body(dy_ref, dyb_ref, hm_ref, w1_ref, w2_ref, x1_ref, g_ref, df_ref, dx_ref, dxb_ref, dg_ref):
        @pl.when(pl.program_id(0) == 0)
        def _():
            dg_ref[...] = jnp.zeros_like(dg_ref)

        dyb = dyb_ref[...]
        for c in range(DFF // tf):
            cols = slice(c * tf, (c + 1) * tf)
            d_hm = _dot_nt(dyb, w2_ref[cols, :])
            df_ref[:, cols] = (d_hm * (2.0 * jnp.sqrt(hm_ref[:, cols].astype(f32)))).astype(bf16)
        dn = _dot_nt(df_ref[...], w1_ref[...])
        dx, dg = _rms_bwd(x1_ref[...], g_ref[...], dn, dy_ref[...])
        dx_ref[...] = dx
        dxb_ref[...] = dx.astype(bf16)
        dg_ref[...] += dg

    row = pl.BlockSpec((tm, D), lambda i: (i, 0))
    wide = pl.BlockSpec((tm, DFF), lambda i: (i, 0))
    vec = pl.BlockSpec((1, D), lambda i: (0, 0))
    once = pl.Buffered(1)
    return pl.pallas_call(
        body, grid=(T // tm,),
        in_specs=[row, row, wide, pl.BlockSpec((D, DFF), lambda i: (0, 0), pipeline_mode=once),
                  pl.BlockSpec((DFF, D), lambda i: (0, 0), pipeline_mode=once), row, vec],
        out_specs=[wide, row, row, vec],
        out_shape=[SDS((T, DFF), bf16), SDS((T, D), f32), SDS((T, D), bf16), SDS((1, D), f32)],
        name="mlp_bwd", compiler_params=_cparams(("arbitrary",), 56))(dy, dyb, hmid, w1, w2, x1, g_mlp)


def _wgrad(a, b, name, tn=1024):
    T, M = a.shape
    N = b.shape[1]
    tmm, tk = min(M, 1024), min(T, 2048)

    def body(a_ref, b_ref, o_ref):
        @pl.when(pl.program_id(2) == 0)
        def _():
            o_ref[...] = jnp.zeros_like(o_ref)

        o_ref[...] += _dot_tn(a_ref[...], b_ref[...])

    return pl.pallas_call(
        body, grid=(M // tmm, N // tn, T // tk),
        in_specs=[pl.BlockSpec((tk, tmm), lambda m, n, t: (t, m)), pl.BlockSpec((tk, tn), lambda m, n, t: (t, n))],
        out_specs=pl.BlockSpec((tmm, tn), lambda m, n, t: (m, n)),
        out_shape=SDS((M, N), f32),
        name=name, compiler_params=_cparams(("parallel", "parallel", "arbitrary"), 40))(a, b)


def _mix_bwd(dx1b, proj, attn, conv, w_attn_o, w_conv_out, w_out, comm=None):
    T = dx1b.shape[0]
    tm = 512

    def body(dx_ref, ga_ref, gc_ref, attn_ref, conv_ref, wa_ref, wc_ref, wo_ref,
             dat_ref, dcv_ref, do_ref, dh3_ref, dga_ref, dgc_ref):
        dm = _dot_nt(dx_ref[...], wo_ref[...])
        sa = _sigmoid(ga_ref[...])
        sc = _sigmoid(gc_ref[...])
        dat = (dm * sa).astype(bf16)
        dcv = (dm * sc).astype(bf16)
        dat_ref[...] = dat
        dcv_ref[...] = dcv
        dga_ref[...] = (dm * attn_ref[...] * sa * (1.0 - sa)).astype(bf16)
        dgc_ref[...] = (dm * conv_ref[...] * sc * (1.0 - sc)).astype(bf16)
        do_ref[...] = _dot_nt(dat, wa_ref[...]).astype(bf16)
        dh3_ref[...] = _dot_nt(dcv, wc_ref[...])

    tile = lambda cb=0: pl.BlockSpec((tm, D), lambda i: (i, cb))
    wfull = pl.BlockSpec((D, D), lambda i: (0, 0))
    return _call(
        body, (dx1b, proj, proj, attn, conv, w_attn_o, w_conv_out, w_out), grid=(T // tm,),
        in_specs=[tile(), tile(C_GA), tile(C_GC), tile(), tile(), wfull, wfull, wfull],
        out_specs=[tile()] * 6,
        out_shape=[SDS((T, D), bf16), SDS((T, D), bf16), SDS((T, D), bf16), SDS((T, D), f32),
                   SDS((T, D), bf16), SDS((T, D), bf16)],
        name="mix_bwd", sem=("parallel",), vmem_mb=48, comm=comm)


def _conv_ln_bwd(dh3, h1, ln_g, ln_b, comm=None):
    T = dh3.shape[0]
    tt = 256

    def body(d_ref, h1_ref, lg_ref, lb_ref, dh1_ref, acc_ref):
        @pl.when(pl.program_id(0) == 0)
        def _():
            acc_ref[...] = jnp.zeros_like(acc_ref)

        h1 = h1_ref[...]
        mu = jnp.mean(h1, axis=-1, keepdims=True)
        xc = h1 - mu
        rstd = lax.rsqrt(jnp.mean(xc * xc, axis=-1, keepdims=True) + EPS)
        xh = xc * rstd
        h2 = xh * lg_ref[...] + lb_ref[...]
        sg = _sigmoid(h2)
        dh2 = d_ref[...] * (sg * (1.0 + h2 * (1.0 - sg)))
        dxh = dh2 * lg_ref[...]
        dh1 = rstd * (dxh - jnp.mean(dxh, axis=-1, keepdims=True) - xh * jnp.mean(dxh * xh, axis=-1, keepdims=True))
        dh1_ref[...] = dh1
        acc_ref[0:1, :] += jnp.sum(dh2 * xh, axis=0, keepdims=True)
        acc_ref[1:2, :] += jnp.sum(dh2, axis=0, keepdims=True)
        acc_ref[2:3, :] += jnp.sum(dh1, axis=0, keepdims=True)

    tile = pl.BlockSpec((tt, D), lambda i: (i, 0))
    vec = pl.BlockSpec((1, D), lambda i: (0, 0))
    return _call(
        body, (dh3, h1, ln_g, ln_b), grid=(T // tt,),
        in_specs=[tile, tile, vec, vec],
        out_specs=[tile, pl.BlockSpec((8, D), lambda i: (0, 0))],
        out_shape=[SDS((T, D), f32), SDS((8, D), f32)],
        name="conv_ln_bwd", sem=("arbitrary",), vmem_mb=32, comm=comm)


def _conv_bwd(dh1, proj, w_dw, comm=None):
    T = dh1.shape[0]
    tt, ch = 256, 32
    nt = T // tt

    def body(d_ref, dn_ref, a_ref, g_ref, ah_ref, gh_ref, w_ref, da_ref, dg_ref, gw_ref, dsh, hsh, dh0, gacc):
        i = pl.program_id(0)

        @pl.when(i == 0)
        def _():
            gacc[...] = jnp.zeros_like(gacc)

        dsh[0, 0:tt, :] = d_ref[...]
        dsh[0, tt:tt + HALO, :] = jnp.where(i < nt - 1, dn_ref[...], 0.0)
        hsh[0, 0:HALO, :] = jnp.where(i > 0, ah_ref[...] * _sigmoid(gh_ref[...]), 0.0)
        hsh[0, HALO:HALO + tt, :] = a_ref[...] * _sigmoid(g_ref[...])
        _shifted_copies(dsh, tt + HALO - SUB)
        _shifted_copies(hsh, tt + HALO - SUB)

        def chunk(c, carry):
            r0 = pl.multiple_of(c * ch, ch)
            lo, hi = _conv_taps(dsh, w_ref, r0, ch, lambda j: (CW - 1) - j, lambda ln: jnp.zeros((ch, 512), f32))
            dh0[pl.ds(r0, ch), 0:512] = lo
            dh0[pl.ds(r0, ch), 512:1024] = hi
            for hf in range(2):
                ln = slice(512 * hf, 512 * hf + 512)
                dv = dsh[0, pl.ds(r0, ch), ln]
                for j in range(CW):
                    pr = dv * _shifted_rows(hsh, r0, HALO - (CW - 1) + j, ch, ln)
                    gacc[8 * j:8 * j + 8, ln] += pr[0:8] + pr[8:16] + pr[16:24] + pr[24:32]
            return carry

        lax.fori_loop(0, tt // ch, chunk, 0)
        a = a_ref[...]
        sg = _sigmoid(g_ref[...])
        d0 = dh0[...]
        da_ref[...] = (d0 * sg).astype(bf16)
        dg_ref[...] = (d0 * a * sg * (1.0 - sg)).astype(bf16)

        @pl.when(i == nt - 1)
        def _():
            gw_ref[...] = jnp.zeros_like(gw_ref)
            for j in range(CW):
                gw_ref[j:j + 1, :] = jnp.sum(gacc[8 * j:8 * j + 8, :], axis=0, keepdims=True)

    hpt = tt // HALO
    tile = lambda cb=0: pl.BlockSpec((tt, D), lambda i: (i, cb))
    halo_prev = lambda cb: pl.BlockSpec((HALO, D), lambda i: (jnp.maximum(i * hpt - 1, 0), cb))
    halo_next = pl.BlockSpec((HALO, D), lambda i: (jnp.minimum((i + 1) * hpt, T // HALO - 1), 0))
    wspec = pl.BlockSpec((HALO, D), lambda i: (0, 0))
    return _call(
        body, (dh1, dh1, proj, proj, proj, proj, w_dw), grid=(nt,),
        in_specs=[tile(), halo_next, tile(C_A), tile(C_G), halo_prev(C_A), halo_prev(C_G), wspec],
        out_specs=[tile(), tile(), wspec],
        out_shape=[SDS((T, D), bf16), SDS((T, D), bf16), SDS((HALO, D), f32)],
        scratch_shapes=[pltpu.VMEM((SUB, tt + HALO, D), f32), pltpu.VMEM((SUB, HALO + tt, D), f32),
                        pltpu.VMEM((tt, D), f32), pltpu.VMEM((8 * HALO, D), f32)],
        name="conv_bwd", sem=("arbitrary",), vmem_mb=48, comm=comm)


def _attn_bwd(qn, kk, vv, bias, sinks, o, do, lse, comm=None):
    T = qn.shape[0]
    nb = T // BLK

    def body(s_ref, q_ref, kc_ref, kp_ref, vc_ref, vp_ref, b_ref, o_ref, do_ref, lse_ref,
             dq_ref, dkc_ref, dkp_ref, dvc_ref, dvp_ref, dsk_ref, dsa_ref):
        n = pl.program_id(0)

        @pl.when(n == 0)
        def _():
            dsk_ref[...] = jnp.zeros_like(dsk_ref)
            dsa_ref[...] = jnp.zeros_like(dsa_ref)

        @pl.when(n == nb)
        def _():
            dkp_ref[...] = jnp.zeros_like(dkp_ref)
            dvp_ref[...] = jnp.zeros_like(dvp_ref)

        @pl.when(n < nb)
        def _():
            has_prev = n > 0
            lo = _low_head_lanes()
            dups = {"kc": [], "kp": [], "vc": [], "vp": []}
            for h in range(NKV):
                c = 256 * h
                qs = _rows2(q_ref, c)
                dos = _rows2(do_ref, c)
                sc = _dot_nt(qs, _rows2(kc_ref, c))
                sp = _dot_nt(qs, _rows2(kp_ref, c))
                dpc = _dot_nt(dos, _rows2(vc_ref, c))
                dpp = _dot_nt(dos, _rows2(vp_ref, c))
                kstack = jnp.concatenate([kp_ref[:, c:c + 128], kc_ref[:, c:c + 128],
                                          kp_ref[:, c + 128:c + 256], kc_ref[:, c + 128:c + 256]], axis=0)
                p_c, p_p, ds_c, ds_p = [], [], [], []
                for pr in range(2):
                    cc = c + 128 * pr
                    prod = do_ref[:, cc:cc + 128].astype(f32) * o_ref[:, cc:cc + 128].astype(f32)
                    d_lo = jnp.sum(jnp.where(lo, prod, 0.0), axis=-1, keepdims=True)
                    d_hi = jnp.sum(prod, axis=-1, keepdims=True) - d_lo
                    row_pc, row_pp, row_dc, row_dp = [], [], [], []
                    for e in range(2):
                        hq = 4 * h + 2 * pr + e
                        rows, cols = slice(128 * pr, 128 * pr + 128), slice(128 * e, 128 * e + 128)
                        delta = d_lo if e == 0 else d_hi
                        lse = lse_ref[:, hq:hq + 1]
                        pp = jnp.where(has_prev, jnp.exp(sp[rows, cols] + b_ref[hq, :, 0:BLK] - lse), 0.0)
                        pc = jnp.exp(sc[rows, cols] + b_ref[hq, :, BLK:2 * BLK] - lse)
                        dsp = pp * (dpp[rows, cols] - delta)
                        dsc = pc * (dpc[rows, cols] - delta)
                        dsa_ref[hq, :, 0:BLK] += dsp
                        dsa_ref[hq, :, BLK:2 * BLK] += dsc
                        dsk_ref[hq] += jnp.broadcast_to(-jnp.sum(jnp.exp(s_ref[0, hq] - lse) * delta), (8, 128))
                        row_pc.append(pc.astype(bf16))
                        row_pp.append(pp.astype(bf16))
                        row_dc.append(dsc.astype(bf16))
                        row_dp.append(dsp.astype(bf16))
                    dq_ref[:, cc:cc + 128] = _dot(jnp.concatenate([row_dp[0], row_dc[0], row_dp[1], row_dc[1]], axis=1), kstack)
                    p_c.append(jnp.concatenate(row_pc, axis=1))
                    p_p.append(jnp.concatenate(row_pp, axis=1))
                    ds_c.append(jnp.concatenate(row_dc, axis=1))
                    ds_p.append(jnp.concatenate(row_dp, axis=1))

                def to_keys(m2, rhs):
                    x2 = _dot_tn(jnp.concatenate(m2, axis=0), rhs)
                    x = jnp.where(lo, x2[0:128], x2[128:256])
                    return x + pltpu.roll(x, HD, 1)

                dups["kc"].append(to_keys(ds_c, qs))
                dups["kp"].append(to_keys(ds_p, qs))
                dups["vc"].append(to_keys(p_c, dos))
                dups["vp"].append(to_keys(p_p, dos))
            for key, ref in (("kc", dkc_ref), ("kp", dkp_ref), ("vc", dvc_ref), ("vp", dvp_ref)):
                d = dups[key]
                ref[:, 0:128] = jnp.where(lo, d[0], d[1])
                ref[:, 128:256] = jnp.where(lo, d[2], d[3])

    clamp = lambda n: jnp.minimum(n, nb - 1)
    blk = lambda f: pl.BlockSpec((BLK, D), f)
    cur = lambda n: (clamp(n), 0)
    prev = lambda n: (jnp.maximum(clamp(n) - 1, 0), 0)
    back = lambda n: (jnp.maximum(n - 1, 0), 0)
    kvb = lambda f: pl.BlockSpec((BLK, NKV * HD), f)
    return _call(
        body, (sinks, qn, kk, kk, vv, vv, bias, o, do, lse), grid=(nb + 1,),
        in_specs=[pl.BlockSpec(memory_space=pltpu.SMEM), blk(cur), blk(cur), blk(prev), blk(cur), blk(prev),
                  pl.BlockSpec((NQ, BLK, 2 * BLK), lambda n: (0, 0, 0)), blk(cur), blk(cur),
                  pl.BlockSpec((BLK, NQ), cur)],
        out_specs=[blk(cur), kvb(cur), kvb(back), kvb(cur), kvb(back),
                   pl.BlockSpec((NQ, 8, 128), lambda n: (0, 0, 0)),
                   pl.BlockSpec((NQ, BLK, 2 * BLK), lambda n: (0, 0, 0))],
        out_shape=[SDS((T, D), f32)] + [SDS((T, NKV * HD), f32)] * 4 + [SDS((NQ, 8, 128), f32), SDS((NQ, BLK, 2 * BLK), f32)],
        name="attn_bwd", sem=("arbitrary",), vmem_mb=40, comm=comm)


def _bias_bwd(dsa):
    def body(bk_ref, ds_ref, out_ref):
        bk = bk_ref[...]
        lane = lax.broadcasted_iota(jnp.int32, (1, 128), 1)
        for h in range(NQ):
            ds = ds_ref[h]
            row = jnp.zeros((1, 128), f32)
            for b in range(NBUCKET):
                row = jnp.where(lane == b, jnp.sum(jnp.where(bk == b, ds, 0.0)), row)
            out_ref[h:h + 1, :] = row

    return pl.pallas_call(body, out_shape=SDS((NQ, 128), f32), name="bias_bwd")(jnp.asarray(_bucket_tile()), dsa)


def _qkv_bwd(proj, gq2, gk2, dqn, dkc, dkp, dvc, dvp):
    T = proj.shape[0]
    tm = 512

    def body(q_ref, kv_ref, gq_ref, gk_ref, dq_ref, dkc_ref, dkp_ref, dvc_ref, dvp_ref,
             oq_ref, okv_ref, ggq_ref, ggk_ref):
        @pl.when(pl.program_id(0) == 0)
        def _():
            ggq_ref[...] = jnp.zeros_like(ggq_ref)
            ggk_ref[...] = jnp.zeros_like(ggk_ref)

        bd = _head_blockdiag()

        def norm_bwd(z, dy, g, scale):
            r = lax.rsqrt(_head_sums(z * z, bd) * (1.0 / HD) + EPS)
            gd = dy * g * scale
            dz = r * gd - z * (r * r * r) * _head_sums(z * gd, bd) * (1.0 / HD)
            return dz, jnp.sum(dy * scale * z * r, axis=0, keepdims=True)

        gq = jnp.zeros((1, 128), f32)
        for p in range(NQ // 2):
            ln = slice(128 * p, 128 * p + 128)
            dz, dg = norm_bwd(q_ref[:, ln], dq_ref[:, ln], gq_ref[...], HD ** -0.5)
            oq_ref[:, ln] = dz.astype(bf16)
            gq = gq + dg
        ggq_ref[...] += gq + pltpu.roll(gq, HD, 1)
        gk = jnp.zeros((1, 128), f32)
        for p in range(NKV // 2):
            ln = slice(128 * p, 128 * p + 128)
            dz, dg = norm_bwd(kv_ref[:, ln], dkc_ref[:, ln] + dkp_ref[:, ln], gk_ref[...], 1.0)
            okv_ref[:, ln] = dz.astype(bf16)
            gk = gk + dg
        ggk_ref[...] += gk + pltpu.roll(gk, HD, 1)
        okv_ref[:, 256:512] = (dvc_ref[...] + dvp_ref[...]).astype(bf16)

    vec = pl.BlockSpec((1, 128), lambda i: (0, 0))
    kvb = pl.BlockSpec((tm, NKV * HD), lambda i: (i, 0))
    return pl.pallas_call(
        body, grid=(T // tm,),
        in_specs=[pl.BlockSpec((tm, D), lambda i: (i, C_Q)), pl.BlockSpec((tm, 512), lambda i: (i, C_KV)), vec, vec,
                  pl.BlockSpec((tm, D), lambda i: (i, 0)), kvb, kvb, kvb, kvb],
        out_specs=[pl.BlockSpec((tm, D), lambda i: (i, 0)), pl.BlockSpec((tm, 512), lambda i: (i, 0)), vec, vec],
        out_shape=[SDS((T, D), bf16), SDS((T, 512), bf16), SDS((1, 128), f32), SDS((1, 128), f32)],
        name="qkv_bwd", compiler_params=_cparams(("arbitrary",), 32))(proj, proj, gq2, gk2, dqn, dkc, dkp, dvc, dvp)


def _inproj_bwd(pieces, w_in, x, dx1, g_mix, comm=None):
    T = x.shape[0]
    tm = 256
    widths = [p.shape[1] for p in pieces]
    offs = [sum(widths[:i]) for i in range(len(widths))]
    assert sum(widths) == INW

    def body(*refs):
        p_refs, (w_ref, x_ref, dx1_ref, g_ref, dx_ref, dg_ref) = refs[:len(pieces)], refs[len(pieces):]

        @pl.when(pl.program_id(0) == 0)
        def _():
            dg_ref[...] = jnp.zeros_like(dg_ref)

        du = None
        for p_ref, off, wd in zip(p_refs, offs, widths):
            part = _dot_nt(p_ref[...], w_ref[:, off:off + wd])
            du = part if du is None else du + part
        dx, dg = _rms_bwd(x_ref[...], g_ref[...], du, dx1_ref[...])
        dx_ref[...] = dx
        dg_ref[...] += dg

    row = pl.BlockSpec((tm, D), lambda i: (i, 0))
    vec = pl.BlockSpec((1, D), lambda i: (0, 0))
    return _call(
        body, (*pieces, w_in, x, dx1, g_mix), grid=(T // tm,),
        in_specs=[pl.BlockSpec((tm, wd), lambda i: (i, 0)) for wd in widths]
        + [pl.BlockSpec((D, INW), lambda i: (0, 0)), row, row, vec],
        out_specs=[row, vec],
        out_shape=[SDS((T, D), f32), SDS((1, D), f32)],
        name="inproj_bwd", sem=("arbitrary",), vmem_mb=48, comm=comm)


def _to_internal_cols(w):
    return jnp.concatenate([w[..., 0:1024], w[..., 1536:INW], w[..., 1024:1536]], axis=-1)


def _forward_backward(x, tgt, w, placed, chip_core):
    def sums(names, grads, got):
        res = [_pair_sum(nm, grads[nm], got_nm, chip_core) for nm, got_nm in zip(names, got)]
        return {nm: r[0] for nm, r in zip(names, res)}, {nm: r[1] for nm, r in zip(names, res)}

    first = ["w_in", "w_dw"]
    w_in, w_dw = _run_comm(_gather_comm({nm: placed[nm] for nm in first}), "gather_first")
    w_in = _to_internal_cols(w_in)
    gq2 = jnp.tile(w["q_norm_g"], (1, 2))
    gk2 = jnp.tile(w["k_norm_g"], (1, 2))
    proj, u = _rms_inproj(x, w["norm_mix_g"], w_in)
    qn, kk, vv = _qk_prep(proj, gq2, gk2)
    bias = _bias_tiles(w["rel_bias"])
    rest = [nm for nm in BIG if nm != "w_in"]
    (o, lse), full = _attn_fwd(qn, kk, vv, bias, w["attn_sinks"], comm=_gather_comm({nm: placed[nm] for nm in rest}))
    full = dict(zip(rest, full))
    h1, h3 = _glu_conv_fwd(proj, w_dw, w["b_dw"], w["conv_ln_g"], w["conv_ln_b"])
    attn, conv, merged, x1, n2 = _mix_out(o, h3, proj, x, full["w_attn_o"], full["w_conv_out"], full["w_out"],
                                          w["norm_mlp_g"])
    hmid, dy, dyb, loss = _mlp_fwd(n2, full["w_ff1"], full["w_ff2"], x1, tgt)

    g, cp, own = {}, {}, {}
    df1, dx1, dx1b, g["norm_mlp_g"] = _mlp_bwd(dy, dyb, hmid, full["w_ff1"], full["w_ff2"], x1, w["norm_mlp_g"])
    ff = ["w_ff1", "w_ff2"]
    gff = {"w_ff2": _wgrad(hmid, dyb, "wgrad_ff2"), "w_ff1": _wgrad(n2, df1, "wgrad_ff1")}
    (dat, dcv, do, dh3, dga, dgc), got = _mix_bwd(dx1b, proj, attn, conv, full["w_attn_o"], full["w_conv_out"],
                                                  full["w_out"], comm=_pair_exchange_comm(gff, ff))
    cp_ff, own_ff = sums(ff, gff, got)
    sq = ["w_out", "w_attn_o", "w_conv_out"]
    gsq = {"w_out": _wgrad(merged, dx1b, "wgrad_out"), "w_attn_o": _wgrad(o, dat, "wgrad_attn_o"),
           "w_conv_out": _wgrad(h3, dcv, "wgrad_conv_out")}
    (dh1, lnacc), got = _conv_ln_bwd(dh3, h1, w["conv_ln_g"], w["conv_ln_b"], comm=_pair_exchange_comm(gsq, sq))
    cp_sq, own_sq = sums(sq, gsq, got)
    cp, own = {**cp_ff, **cp_sq}, {**own_ff, **own_sq}
    g["conv_ln_g"], g["conv_ln_b"], g["b_dw"] = lnacc[0:1], lnacc[1:2], lnacc[2:3]
    five = ff + sq
    (da, dg, g["w_dw"]), rc = _conv_bwd(dh1, proj, w_dw, comm=_chip_exchange_comm(cp, five))
    tot = {nm: _chip_sum(nm, own[nm], rc_nm, chip_core) for nm, rc_nm in zip(five, rc)}
    (dqn, dkc, dkp, dvc, dvp, dsk, dsa), shards = _attn_bwd(qn, kk, vv, bias, w["attn_sinks"], o, do, lse,
                                                            comm=_pair_share_comm(tot, five))
    shards = dict(zip(five, shards))
    g["attn_sinks"] = dsk[:, 0, 0].reshape(1, NQ)
    g["rel_bias"] = _bias_bwd(dsa)[:, 0:NBUCKET].T
    dq, dkv, ggq, ggk = _qkv_bwd(proj, gq2, gk2, dqn, dkc, dkp, dvc, dvp)
    g["q_norm_g"], g["k_norm_g"] = ggq[:, 0:HD], ggk[:, 0:HD]
    pieces = [dq, da, dg, dga, dgc, dkv]
    names = ["q", "a", "g", "ga", "gc", "kv"]
    gw = {nm: _wgrad(u, p, "wgrad_in_" + nm, tn=p.shape[1] if p.shape[1] < 1024 else 1024) for nm, p in zip(names, pieces)}
    gin = {"w_in": jnp.concatenate([gw["q"], gw["kv"], gw["a"], gw["g"], gw["ga"], gw["gc"]], axis=1)}
    got = _run_comm(_pair_exchange_comm(gin, ["w_in"]), "rs_pair_exchange_in")
    cp_in, own_in = sums(["w_in"], gin, got)
    (grad_x, g["norm_mix_g"]), rc = _inproj_bwd(pieces, w_in, x, dx1, w["norm_mix_g"],
                                                comm=_chip_exchange_comm(cp_in, ["w_in"]))
    tot = {"w_in": _chip_sum("w_in", own_in["w_in"], rc[0], chip_core)}
    shards["w_in"] = _run_comm(_pair_share_comm(tot, ["w_in"]), "rs_pair_share_in")[0]
    return loss[0, 0], grad_x, g, shards


BIG = ["w_in", "w_attn_o", "w_conv_out", "w_out", "w_ff1", "w_ff2"]
SHARD_AXIS = {"w_in": 1, "w_attn_o": 0, "w_conv_out": 0, "w_out": 0, "w_ff1": 1, "w_ff2": 0, "w_dw": 1}
SHARD_SHAPE = {"w_in": (D, INW // 4), "w_attn_o": (D // 4, D), "w_conv_out": (D // 4, D), "w_out": (D // 4, D),
               "w_ff1": (D, DFF // 4), "w_ff2": (DFF // 4, D), "w_dw": (HALO, D // 4)}


def _position():
    x, y, c = lax.axis_index("x"), lax.axis_index("y"), lax.axis_index("c")
    other_chips = [(1 - x, y), (x, 1 - y), (1 - x, 1 - y)]
    return x, y, c, 2 * x + y, other_chips


def _shard_window(name, full_ref, s, half=None):
    R, C = SHARD_SHAPE[name]
    r0, nr = (0, R) if half is None else (half * (R // 2), R // 2)
    if SHARD_AXIS[name] == 1:
        return full_ref.at[pl.ds(r0, nr), pl.ds(s * C, C)]
    return full_ref.at[pl.ds(s * R + r0, nr), :]


def _remote(src, dst, send_sems, recv_sems, k, device):
    return pltpu.make_async_remote_copy(src_ref=src, dst_ref=dst, send_sem=send_sems.at[k], recv_sem=recv_sems.at[k],
                                        device_id=device, device_id_type=MESH)


def _full_shape(nm):
    R, C = SHARD_SHAPE[nm]
    return (R, 4 * C) if SHARD_AXIS[nm] == 1 else (4 * R, C)


def _place_shard(nm, shard, chip_arr, dtype):
    R, C = SHARD_SHAPE[nm]
    tr = min(R, 256)
    if SHARD_AXIS[nm] == 1:
        o_map = lambda i, ch: (i, ch[0])
    else:
        o_map = lambda i, ch: (ch[0] * (R // tr) + i, 0)

    def body(ch_ref, s_ref, o_ref):
        o_ref[...] = s_ref[...].astype(dtype)

    return pl.pallas_call(
        body,
        grid_spec=pltpu.PrefetchScalarGridSpec(
            num_scalar_prefetch=1, grid=(R // tr,),
            in_specs=[pl.BlockSpec((tr, C), lambda i, ch: (i, 0))], out_specs=pl.BlockSpec((tr, C), o_map)),
        out_shape=SDS(_full_shape(nm), dtype), name="place_" + nm,
        compiler_params=_cparams(("parallel",), 32))(chip_arr, shard)


def _gather_comm(placed):
    names = list(placed)
    n = len(names)

    def copies(cout, send, recv):
        x, y, c, chip, chips = _position()
        for a, nm in enumerate(names):
            for j, (cx, cy) in enumerate(chips):
                def ici(s, a=a, nm=nm, j=j, cx=cx, cy=cy):
                    w = _shard_window(nm, cout[a], s, c)
                    return _remote(w, w, send, recv, 6 * a + j, (cx, cy, c))

                def d2d(h, a=a, nm=nm, j=j, cx=cx, cy=cy):
                    w = _shard_window(nm, cout[a], 2 * cx + cy, h)
                    return _remote(w, w, send, recv, 6 * a + 3 + j, (x, y, 1 - c))

                yield ici, d2d, chip, 2 * cx + cy, c

    def start(cin, cout, send, recv):
        for ici, d2d, chip, s, c in copies(cout, send, recv):
            ici(chip).start()

    def mid(cin, cout, send, recv):
        for ici, d2d, chip, s, c in copies(cout, send, recv):
            ici(s).wait_recv()
            d2d(c).start()

    def finish(cin, cout, send, recv):
        for ici, d2d, chip, s, c in copies(cout, send, recv):
            d2d(1 - c).wait_recv()
        for ici, d2d, chip, s, c in copies(cout, send, recv):
            ici(chip).wait_send()
            d2d(c).wait_send()

    return _Comm([placed[nm] for nm in names], [SDS(placed[nm].shape, placed[nm].dtype) for nm in names], 6 * n,
                 start, finish, mid, aliases={a: a for a in range(n)})


def _half_rows(nm):
    return SHARD_SHAPE[nm][0] // 2


RS_TILE = 128


def _exchange_comm(ins, out_shapes, copies, n_sems, aliases=None):
    def start(cin, cout, send, recv):
        for cp in copies(cin, cout, send, recv):
            cp.start()

    def finish(cin, cout, send, recv):
        for cp in copies(cin, cout, send, recv):
            cp.wait()

    return _Comm(ins, out_shapes, n_sems, start, finish, aliases=aliases)


def _pair_exchange_comm(grads, names):
    def copies(cin, cout, send, recv):
        x, y, c, chip, chips = _position()
        return [_remote(_shard_window(nm, cin[a], s, 1 - c), cout[a].at[s], send, recv, 4 * a + s, (x, y, 1 - c))
                for a, nm in enumerate(names) for s in range(4)]

    return _exchange_comm([grads[nm] for nm in names],
                          [SDS((4, _half_rows(nm), SHARD_SHAPE[nm][1]), f32) for nm in names], copies, 4 * len(names))


def _pair_sum(nm, g, got, chip_core):
    R, C = SHARD_SHAPE[nm]
    hr = R // 2
    nt = hr // RS_TILE
    if SHARD_AXIS[nm] == 1:
        g_map = lambda i, s, sc: (sc[1] * nt + i, s)
    else:
        g_map = lambda i, s, sc: (s * (R // RS_TILE) + sc[1] * nt + i, 0)

    def body(sc_ref, g_ref, got_ref, o16_ref, own_ref):
        v = g_ref[...] + got_ref[0]
        o16_ref[0] = v.astype(bf16)

        @pl.when(pl.program_id(1) == sc_ref[0])
        def _():
            own_ref[...] = v

    blk3 = pl.BlockSpec((1, RS_TILE, C), lambda i, s, sc: (s, i, 0))
    return pl.pallas_call(
        body,
        grid_spec=pltpu.PrefetchScalarGridSpec(
            num_scalar_prefetch=1, grid=(nt, 4),
            in_specs=[pl.BlockSpec((RS_TILE, C), g_map), blk3],
            out_specs=[blk3, pl.BlockSpec((RS_TILE, C), lambda i, s, sc: (i, 0))]),
        out_shape=[SDS((4, hr, C), bf16), SDS((hr, C), f32)], name="rs_pair_sum_" + nm,
        compiler_params=_cparams(("parallel", "arbitrary"), 32))(chip_core, g, got)


def _chip_exchange_comm(cp, names):
    def copies(cin, cout, send, recv):
        x, y, c, chip, chips = _position()
        return [_remote(cin[a].at[2 * cx + cy], cout[a].at[j], send, recv, 3 * a + j, (cx, cy, c))
                for a, nm in enumerate(names) for j, (cx, cy) in enumerate(chips)]

    return _exchange_comm([cp[nm] for nm in names],
                          [SDS((3, _half_rows(nm), SHARD_SHAPE[nm][1]), bf16) for nm in names], copies, 3 * len(names))


def _chip_sum(nm, own, rc, chip_core):
    R, C = SHARD_SHAPE[nm]
    nt = (R // 2) // RS_TILE

    def body(sc_ref, own_ref, rc_ref, o_ref):
        o_ref[...] = own_ref[...] + rc_ref[0].astype(f32) + rc_ref[1].astype(f32) + rc_ref[2].astype(f32)

    return pl.pallas_call(
        body,
        grid_spec=pltpu.PrefetchScalarGridSpec(
            num_scalar_prefetch=1, grid=(nt,),
            in_specs=[pl.BlockSpec((RS_TILE, C), lambda i, sc: (i, 0)),
                      pl.BlockSpec((3, RS_TILE, C), lambda i, sc: (0, i, 0))],
            out_specs=pl.BlockSpec((RS_TILE, C), lambda i, sc: (sc[1] * nt + i, 0))),
        out_shape=SDS((R, C), f32), name="rs_chip_sum_" + nm,
        compiler_params=_cparams(("parallel",), 32))(chip_core, own, rc)


def _pair_share_comm(tot, names):
    def copies(cin, cout, send, recv):
        x, y, c, chip, chips = _position()
        cps = []
        for a, nm in enumerate(names):
            hr = _half_rows(nm)
            mine = cout[a].at[pl.ds(c * hr, hr), :]
            cps.append(_remote(mine, mine, send, recv, a, (x, y, 1 - c)))
        return cps

    return _exchange_comm([tot[nm] for nm in names], [SDS(SHARD_SHAPE[nm], f32) for nm in names], copies, len(names),
                          aliases={a: a for a in range(len(names))})


SMALL_ROWS = 40


def _allreduce_small(block):
    def body(x_ref, out_ref, buf, send_sems, recv_sems, local_sem):
        x, y, c, chip, chips = _position()
        me, sibling = (x, y, c), (x, y, 1 - c)

        def slot(px, py, pc):
            return buf.at[4 * px + 2 * py + pc]

        def copy(k, block_of, to, src=None):
            return _remote(slot(*block_of) if src is None else src, slot(*block_of), send_sems, recv_sems, k, to)

        mine = pltpu.make_async_copy(x_ref, slot(*me), local_sem)
        mine.start()
        first = [copy(0, me, sibling, src=x_ref)] + [copy(1 + j, me, (*ch, c), src=x_ref) for j, ch in enumerate(chips)]
        for cp in first:
            cp.start()
        passed = [copy(4 + j, (*ch, c), sibling) for j, ch in enumerate(chips)]
        for j, ch in enumerate(chips):
            copy(1 + j, (*ch, c), me).wait_recv()
            passed[j].start()
        copy(0, sibling, me).wait_recv()
        for j, ch in enumerate(chips):
            copy(4 + j, (*ch, 1 - c), me).wait_recv()
        for cp in first + passed:
            cp.wait_send()
        mine.wait()
        acc = buf[0]
        for d in range(1, 8):
            acc = acc + buf[d]
        out_ref[...] = acc

    vm = pl.BlockSpec(memory_space=pltpu.VMEM)
    return pl.pallas_call(
        body, in_specs=[vm], out_specs=vm, out_shape=SDS((SMALL_ROWS, D), f32),
        scratch_shapes=[pltpu.VMEM((8, SMALL_ROWS, D), f32), pltpu.SemaphoreType.DMA((7,)), pltpu.SemaphoreType.DMA((7,)),
                        pltpu.SemaphoreType.DMA],
        name="allreduce_small")(block)


def _adamw(w, g, m, v, name):
    rows, cols = w.shape
    tr = 256 if rows % 256 == 0 else rows

    def body(w_ref, g_ref, m_ref, v_ref, d_ref, nm_ref, nv_ref):
        gv = g_ref[...]
        m2 = ADAM_B1 * m_ref[...] + (1.0 - ADAM_B1) * gv
        v2 = ADAM_B2 * v_ref[...] + (1.0 - ADAM_B2) * jnp.square(gv)
        m_hat = m2 / (1.0 - ADAM_B1 ** ADAM_STEP)
        v_hat = v2 / (1.0 - ADAM_B2 ** ADAM_STEP)
        d_ref[...] = -ADAM_LR * (m_hat / (jnp.sqrt(v_hat) + ADAM_EPS) + ADAM_WD * w_ref[...])
        nm_ref[...] = m2
        nv_ref[...] = v2

    spec = pl.BlockSpec((tr, cols), lambda i: (i, 0))
    return pl.pallas_call(body, grid=(rows // tr,), in_specs=[spec] * 4, out_specs=[spec] * 3,
                          out_shape=[SDS((rows, cols), f32)] * 3, name=name,
                          compiler_params=_cparams(("parallel",), 40))(w, g, m, v)


WEIGHTS = ["norm_mix_g", "w_in", "q_norm_g", "k_norm_g", "attn_sinks", "rel_bias", "w_attn_o", "w_dw", "b_dw",
           "conv_ln_g", "conv_ln_b", "w_conv_out", "w_out", "norm_mlp_g", "w_ff1", "w_ff2"]
ROW_VECS = ["norm_mix_g", "b_dw", "conv_ln_g", "conv_ln_b", "norm_mlp_g"]
MISC_ROW = 5
W_DW_ROW = 8


def _pack_small(vals, loss=None):
    misc = [vals["q_norm_g"].reshape(1, HD), vals["k_norm_g"].reshape(1, HD), vals["attn_sinks"].reshape(1, NQ),
            jnp.zeros((1, 1), f32) if loss is None else loss.reshape(1, 1), jnp.zeros((1, 111), f32),
            vals["rel_bias"].reshape(1, NBUCKET * NQ), jnp.zeros((1, 256), f32)]
    rows = [vals[nm].reshape(1, D) for nm in ROW_VECS] + [jnp.concatenate(misc, axis=1), jnp.zeros((2, D), f32)]
    return jnp.concatenate(rows, axis=0)


def _unpack_small(block):
    out = {nm: block[i:i + 1] for i, nm in enumerate(ROW_VECS)}
    misc = block[MISC_ROW]
    out["q_norm_g"] = misc[0:64].reshape(1, HD)
    out["k_norm_g"] = misc[64:128].reshape(1, HD)
    out["attn_sinks"] = misc[128:144].reshape(1, NQ)
    out["rel_bias"] = misc[256:768].reshape(NBUCKET, NQ)
    return out, misc[144]


def kernel(x, norm_mix_g, w_in, q_norm_g, k_norm_g, attn_sinks, rel_bias, w_attn_o, w_dw, b_dw, conv_ln_g, conv_ln_b, w_conv_out, w_out, norm_mlp_g, w_ff1, w_ff2, loss_target, m_norm_mix_g, m_w_in, m_q_norm_g, m_k_norm_g, m_attn_sinks, m_rel_bias, m_w_attn_o, m_w_dw, m_b_dw, m_conv_ln_g, m_conv_ln_b, m_w_conv_out, m_w_out, m_norm_mlp_g, m_w_ff1, m_w_ff2, v_norm_mix_g, v_w_in, v_q_norm_g, v_k_norm_g, v_attn_sinks, v_rel_bias, v_w_attn_o, v_w_dw, v_b_dw, v_conv_ln_g, v_conv_ln_b, v_w_conv_out, v_w_out, v_norm_mlp_g, v_w_ff1, v_w_ff2):
    args = dict(locals())
    wts = {nm: args[nm] for nm in WEIGHTS}
    mom = {nm: args["m_" + nm] for nm in WEIGHTS}
    var = {nm: args["v_" + nm] for nm in WEIGHTS}
    chip = 2 * lax.axis_index("x") + lax.axis_index("y")

    chip_arr = jnp.reshape(chip, (1,)).astype(jnp.int32)
    chip_core = jnp.stack([chip, lax.axis_index("c")]).astype(jnp.int32)
    placed = {nm: _place_shard(nm, wts[nm][0], chip_arr, bf16) for nm in BIG}
    placed["w_dw"] = _place_shard("w_dw", jnp.pad(w_dw[0], ((0, 1), (0, 0))), chip_arr, f32)

    loss_part, grad_x, g, shards = _forward_backward(x[0], loss_target[0], wts, placed, chip_core)

    small = jnp.concatenate([_pack_small(g, loss_part), g["w_dw"]], axis=0)
    small = _allreduce_small(small)
    grads, loss = _unpack_small(small)
    grads["w_dw"] = lax.dynamic_slice(small[W_DW_ROW:W_DW_ROW + CW], (0, chip * (D // 4)), (CW, D // 4))
    grads.update(shards)

    delta, new_m, new_v = {}, {}, {}
    sd, sm, sv = _adamw(_pack_small(wts), small[0:8], _pack_small(mom), _pack_small(var), "adamw_small")
    for res, blk in ((delta, sd), (new_m, sm), (new_v, sv)):
        res.update(_unpack_small(blk)[0])
    for nm in BIG + ["w_dw"]:
        shp = wts[nm].shape
        two_d = lambda a: a.reshape(shp[-2], shp[-1])
        delta[nm], new_m[nm], new_v[nm] = _adamw(two_d(wts[nm]), grads[nm], two_d(mom[nm]), two_d(var[nm]), "adamw_" + nm)

    def shaped(vals):
        return [vals[nm].reshape(wts[nm].shape) for nm in WEIGHTS]

    return (loss, grad_x[None], *shaped(grads), *shaped(delta), *shaped(new_m), *shaped(new_v))
```

```python
import functools

import numpy as np
import jax
import jax.numpy as jnp
from jax import lax
from jax.experimental import pallas as pl
from jax.experimental.pallas import tpu as pltpu

f32 = jnp.float32
bf16 = jnp.bfloat16
SDS = jax.ShapeDtypeStruct
MESH = pl.DeviceIdType.MESH

D = 1024
HD = 64
NQ = 16
NKV = 4
BLK = 128
CW = 31
HALO = 32
DFF = 4096
NBUCKET = 32
EPS = 1e-6
NEG = -1e30
INW = 5632
MIX_CHUNK = 256
C_Q, C_A, C_G, C_GA, C_GC = 0, 1, 2, 3, 4
C_KV = 10

ADAM_LR = 0.001
ADAM_B1 = 0.9
ADAM_B2 = 0.999
ADAM_EPS = 1e-08
ADAM_WD = 0.01
ADAM_STEP = 10

VMEM_BYTES_V7X = 64 << 20


def _cparams(sem, vmem_mb):
    assert (vmem_mb << 20) < VMEM_BYTES_V7X
    return pltpu.CompilerParams(dimension_semantics=sem, vmem_limit_bytes=vmem_mb << 20)


ANY = pl.BlockSpec(memory_space=pl.ANY)


class _Comm:
    def __init__(self, ins, out_shapes, n_sems, start, finish, mid=None, aliases=None):
        self.ins, self.out_shapes, self.n_sems = list(ins), list(out_shapes), n_sems
        self.start, self.finish, self.mid, self.aliases = start, finish, mid, dict(aliases or {})


def _call(body, args, *, grid, in_specs, out_specs, out_shape, name, sem, vmem_mb, scratch_shapes=(), comm=None,
          mid_step=None):
    n_in, n_out, n_scr = len(in_specs), len(out_specs), len(scratch_shapes)
    if comm is None:
        outs = pl.pallas_call(body, grid=grid, in_specs=list(in_specs), out_specs=list(out_specs),
                              out_shape=list(out_shape), scratch_shapes=list(scratch_shapes), name=name,
                              compiler_params=_cparams(sem, vmem_mb))(*args)
        return list(outs), []
    ci, co = len(comm.ins), len(comm.out_shapes)
    last = grid[0] - 1

    def wrapped(*refs):
        ins, cin = refs[:n_in], refs[n_in:n_in + ci]
        outs = refs[n_in + ci:n_in + ci + n_out]
        cout = refs[n_in + ci + n_out:n_in + ci + n_out + co]
        scr = refs[n_in + ci + n_out + co:]
        send, recv = scr[n_scr], scr[n_scr + 1]
        step = pl.program_id(0)

        @pl.when(step == 0)
        def _():
            comm.start(cin, cout, send, recv)

        body(*ins, *outs, *scr[:n_scr])
        if comm.mid is not None:
            @pl.when(step == mid_step)
            def _():
                comm.mid(cin, cout, send, recv)

        @pl.when(step == last)
        def _():
            comm.finish(cin, cout, send, recv)

    res = pl.pallas_call(
        wrapped, grid=grid, in_specs=list(in_specs) + [ANY] * ci, out_specs=list(out_specs) + [ANY] * co,
        out_shape=list(out_shape) + comm.out_shapes,
        input_output_aliases={n_in + k: n_out + v for k, v in comm.aliases.items()},
        scratch_shapes=list(scratch_shapes) + [pltpu.SemaphoreType.DMA((comm.n_sems,))] * 2,
        name=name, compiler_params=_cparams(("arbitrary",), vmem_mb))(*args, *comm.ins)
    return list(res[:n_out]), list(res[n_out:])


def _run_comm(comm, name):
    ci, co = len(comm.ins), len(comm.out_shapes)

    def body(*refs):
        cin, cout, (send, recv) = refs[:ci], refs[ci:ci + co], refs[ci + co:]
        comm.start(cin, cout, send, recv)
        if comm.mid is not None:
            comm.mid(cin, cout, send, recv)
        comm.finish(cin, cout, send, recv)

    return pl.pallas_call(
        body, in_specs=[ANY] * ci, out_specs=[ANY] * co, out_shape=comm.out_shapes, input_output_aliases=comm.aliases,
        scratch_shapes=[pltpu.SemaphoreType.DMA((comm.n_sems,))] * 2, name=name)(*comm.ins)


def _dot(a, b):
    return jnp.dot(a, b, preferred_element_type=f32)


def _dot_nt(a, b):
    return lax.dot_general(a, b, (((1,), (1,)), ((), ())), preferred_element_type=f32)


def _dot_tn(a, b):
    return lax.dot_general(a, b, (((0,), (0,)), ((), ())), preferred_element_type=f32)


def _sigmoid(x):
    return 1.0 / (1.0 + jnp.exp(-x))


def _low_head_lanes():
    return lax.broadcasted_iota(jnp.int32, (1, 2 * HD), 1) < HD


def _head_blockdiag():
    r = lax.broadcasted_iota(jnp.int32, (2 * HD, 2 * HD), 0) // HD
    c = lax.broadcasted_iota(jnp.int32, (2 * HD, 2 * HD), 1) // HD
    return jnp.where(r == c, 1.0, 0.0).astype(bf16)


def _head_sums(z, bd):
    hi = z.astype(bf16)
    lo = (z - hi.astype(f32)).astype(bf16)
    return _dot(hi, bd) + _dot(lo, bd)


def _rms_inproj(x, g, w):
    T, N = x.shape[0], w.shape[1]
    tm, tn = 512, 512

    def body(x_ref, g_ref, w_ref, p_ref, u_ref):
        xv = x_ref[...]
        r = lax.rsqrt(jnp.mean(xv * xv, axis=-1, keepdims=True) + EPS)
        u = (xv * r * g_ref[...]).astype(bf16)
        u_ref[...] = u
        for c in range(N // tn):
            p_ref[:, c * tn:(c + 1) * tn] = _dot(u, w_ref[:, c * tn:(c + 1) * tn])

    return pl.pallas_call(
        body, grid=(T // tm,),
        in_specs=[pl.BlockSpec((tm, D), lambda i: (i, 0)),
                  pl.BlockSpec((1, D), lambda i: (0, 0)),
                  pl.BlockSpec((D, N), lambda i: (0, 0), pipeline_mode=pl.Buffered(1))],
        out_specs=[pl.BlockSpec((tm, N), lambda i: (i, 0)),
                   pl.BlockSpec((tm, D), lambda i: (i, 0))],
        out_shape=[SDS((T, N), f32), SDS((T, D), bf16)],
        name="rms_inproj", compiler_params=_cparams(("parallel",), 48))(x, g, w)


def _split_pair(pair, out_ref, p, lo):
    rolled = pltpu.roll(pair, HD, 1)
    zero = jnp.zeros_like(pair)
    c = 512 * p
    out_ref[:, c:c + 128] = jnp.where(lo, pair, zero).astype(bf16)
    out_ref[:, c + 128:c + 256] = jnp.where(lo, zero, rolled).astype(bf16)
    out_ref[:, c + 256:c + 384] = jnp.where(lo, rolled, zero).astype(bf16)
    out_ref[:, c + 384:c + 512] = jnp.where(lo, zero, pair).astype(bf16)


def _qk_prep(proj, gq2, gk2):
    T = proj.shape[0]
    tm = 512

    def body(q_ref, kv_ref, gq_ref, gk_ref, qn_ref, kk_ref, vv_ref):
        bd = _head_blockdiag()
        lo = _low_head_lanes()
        for p in range(NQ // 2):
            z = q_ref[:, 128 * p:128 * p + 128]
            r = lax.rsqrt(_head_sums(z * z, bd) * (1.0 / HD) + EPS)
            qn_ref[:, 128 * p:128 * p + 128] = (z * r * gq_ref[...] * (HD ** -0.5)).astype(bf16)
        for p in range(NKV // 2):
            z = kv_ref[:, 128 * p:128 * p + 128]
            r = lax.rsqrt(_head_sums(z * z, bd) * (1.0 / HD) + EPS)
            _split_pair(z * r * gk_ref[...], kk_ref, p, lo)
            _split_pair(kv_ref[:, 256 + 128 * p:256 + 128 * p + 128], vv_ref, p, lo)

    return pl.pallas_call(
        body, grid=(T // tm,),
        in_specs=[pl.BlockSpec((tm, D), lambda i: (i, C_Q)),
                  pl.BlockSpec((tm, 512), lambda i: (i, C_KV)),
                  pl.BlockSpec((1, 128), lambda i: (0, 0)),
                  pl.BlockSpec((1, 128), lambda i: (0, 0))],
        out_specs=[pl.BlockSpec((tm, D), lambda i: (i, 0))] * 3,
        out_shape=[SDS((T, D), bf16)] * 3,
        name="qk_prep", compiler_params=_cparams(("parallel",), 32))(proj, proj, gq2, gk2)


def _bucket_tile():
    qi = np.arange(BLK)[:, None]
    kj = np.arange(BLK)[None, :]
    n = np.where(kj > qi, qi + BLK - kj, qi - kj)
    max_exact = NBUCKET // 2
    nf = np.maximum(n, 1).astype(np.float32)
    large = max_exact + (np.log(nf / max_exact) / np.float32(np.log(128 / max_exact))
                         * (NBUCKET - max_exact)).astype(np.int32)
    large = np.minimum(large, NBUCKET - 1)
    return np.where(n < max_exact, n, large).astype(np.int32)


def _from_prev_block():
    return lax.broadcasted_iota(jnp.int32, (BLK, BLK), 1) > lax.broadcasted_iota(jnp.int32, (BLK, BLK), 0)


def _bias_tiles(rel_bias):
    def body(rb_ref, bk_ref, out_ref):
        bk = bk_ref[...]
        for h in range(NQ):
            acc = jnp.zeros((BLK, BLK), f32)
            for b in range(NBUCKET):
                acc = jnp.where(bk == b, rb_ref[b, h], acc)
            out_ref[h] = acc

    return pl.pallas_call(
        body,
        in_specs=[pl.BlockSpec(memory_space=pltpu.SMEM), pl.BlockSpec(memory_space=pltpu.VMEM)],
        out_specs=pl.BlockSpec(memory_space=pltpu.VMEM),
        out_shape=SDS((NQ, BLK, BLK), f32),
        name="bias_tiles")(rel_bias, jnp.asarray(_bucket_tile()))


def _rows2(ref, c):
    return jnp.concatenate([ref[:, c:c + 128], ref[:, c + 128:c + 256]], axis=0)


def _attn_fwd(qn, kk, vv, bias, sinks, comm=None):
    T = qn.shape[0]
    nb = T // BLK

    def body(s_ref, q_ref, kc_ref, kp_ref, vc_ref, vp_ref, b_ref, o_ref, lse_ref):
        prev = _from_prev_block()
        no_key = jnp.logical_and(prev, pl.program_id(0) == 0)
        for h in range(NKV):
            c = 256 * h
            qs = _rows2(q_ref, c)
            sc = _dot_nt(qs, _rows2(kc_ref, c))
            sp = _dot_nt(qs, _rows2(kp_ref, c))
            vstack = jnp.concatenate([vp_ref[:, c:c + 128], vc_ref[:, c:c + 128],
                                      vp_ref[:, c + 128:c + 256], vc_ref[:, c + 128:c + 256]], axis=0)
            for pr in range(2):
                ps = []
                for e in range(2):
                    hq = 4 * h + 2 * pr + e
                    rows, cols = slice(128 * pr, 128 * pr + 128), slice(128 * e, 128 * e + 128)
                    s = jnp.where(no_key, NEG, jnp.where(prev, sp[rows, cols], sc[rows, cols]) + b_ref[hq])
                    sink = s_ref[0, hq]
                    m = jnp.maximum(jnp.max(s, axis=-1, keepdims=True), sink)
                    ex = jnp.exp(s - m)
                    l = jnp.sum(ex, axis=-1, keepdims=True) + jnp.exp(sink - m)
                    p = ex * (1.0 / l)
                    ps += [jnp.where(prev, p, 0.0).astype(bf16), jnp.where(prev, 0.0, p).astype(bf16)]
                    lse_ref[:, hq:hq + 1] = m + jnp.log(l)
                o_ref[:, c + 128 * pr:c + 128 * pr + 128] = _dot(jnp.concatenate(ps, axis=1), vstack).astype(bf16)

    blk = lambda f: pl.BlockSpec((BLK, D), f)
    cur = lambda n: (n, 0)
    prev = lambda n: (jnp.maximum(n - 1, 0), 0)
    return _call(
        body, (sinks, qn, kk, kk, vv, vv, bias), grid=(nb,),
        in_specs=[pl.BlockSpec(memory_space=pltpu.SMEM), blk(cur), blk(cur), blk(prev), blk(cur), blk(prev),
                  pl.BlockSpec((NQ, BLK, BLK), lambda n: (0, 0, 0))],
        out_specs=[blk(cur), pl.BlockSpec((BLK, NQ), cur)],
        out_shape=[SDS((T, D), bf16), SDS((T, NQ), f32)],
        name="attn_fwd", sem=("parallel",), vmem_mb=32, comm=comm, mid_step=(3 * nb) // 4)


SUB = 8


def _shifted_copies(sh_ref, rows):
    for b in range(1, SUB):
        sh_ref[b, 0:rows, :] = sh_ref[0, pl.ds(b, rows), :]


def _shifted_rows(sh_ref, r0, offset, rows, ln):
    a, b = divmod(offset, SUB)
    return sh_ref[b, pl.ds(pl.multiple_of(r0 + SUB * a, SUB), rows), ln]


def _conv_taps(sh_ref, w_ref, r0, rows, offset_of_tap, init):
    halves = []
    for hf in range(2):
        ln = slice(512 * hf, 512 * hf + 512)
        acc = init(ln)
        for j in range(CW):
            acc = acc + _shifted_rows(sh_ref, r0, offset_of_tap(j), rows, ln) * w_ref[j:j + 1, ln]
        halves.append(acc)
    return halves


def _glu_conv_fwd(proj, w_dw, b_dw, ln_g, ln_b):
    T = proj.shape[0]
    tt, ch = 256, 32

    def body(a_ref, g_ref, ah_ref, gh_ref, w_ref, b_ref, lg_ref, lb_ref, h1_ref, h3_ref, sh):
        i = pl.program_id(0)
        halo = ah_ref[...] * _sigmoid(gh_ref[...])
        sh[0, 0:HALO, :] = jnp.where(i > 0, halo, 0.0)
        sh[0, HALO:HALO + tt, :] = a_ref[...] * _sigmoid(g_ref[...])
        _shifted_copies(sh, tt + HALO - SUB)

        def chunk(c, carry):
            r0 = pl.multiple_of(c * ch, ch)
            lo, hi = _conv_taps(sh, w_ref, r0, ch, lambda j: HALO - (CW - 1) + j,
                                lambda ln: jnp.broadcast_to(b_ref[:, ln], (ch, 512)))
            h1_ref[pl.ds(r0, ch), 0:512] = lo
            h1_ref[pl.ds(r0, ch), 512:1024] = hi
            return carry

        lax.fori_loop(0, tt // ch, chunk, 0)
        h1 = h1_ref[...]
        mu = jnp.mean(h1, axis=-1, keepdims=True)
        xc = h1 - mu
        var = jnp.mean(xc * xc, axis=-1, keepdims=True)
        h2 = xc * lax.rsqrt(var + EPS) * lg_ref[...] + lb_ref[...]
        h3_ref[...] = (h2 * _sigmoid(h2)).astype(bf16)

    hpt = tt // HALO
    tile = lambda cb: pl.BlockSpec((tt, D), lambda i: (i, cb))
    halo = lambda cb: pl.BlockSpec((HALO, D), lambda i: (jnp.maximum(i * hpt - 1, 0), cb))
    vec = pl.BlockSpec((1, D), lambda i: (0, 0))
    return pl.pallas_call(
        body, grid=(T // tt,),
        in_specs=[tile(C_A), tile(C_G), halo(C_A), halo(C_G), pl.BlockSpec((HALO, D), lambda i: (0, 0)), vec, vec, vec],
        out_specs=[pl.BlockSpec((tt, D), lambda i: (i, 0))] * 2,
        out_shape=[SDS((T, D), f32), SDS((T, D), bf16)],
        scratch_shapes=[pltpu.VMEM((SUB, HALO + tt, D), f32)],
        name="glu_conv_fwd", compiler_params=_cparams(("parallel",), 32))(proj, proj, proj, proj, w_dw, b_dw, ln_g, ln_b)


def _mix_out(o, h3, proj, x, w_attn_o, w_conv_out, w_out, g_mlp):
    T = x.shape[0]
    tm = 512

    def body(o_ref, h3_ref, ga_ref, gc_ref, x_ref, wa_ref, wc_ref, wo_ref, g_ref,
             attn_ref, conv_ref, mg_ref, x1_ref, n2_ref):
        x1 = x_ref[...]
        for j in range(D // MIX_CHUNK):
            cols = slice(j * MIX_CHUNK, (j + 1) * MIX_CHUNK)
            attn = _dot(o_ref[...], wa_ref[:, cols])
            conv = _dot(h3_ref[...], wc_ref[:, cols])
            attn_ref[:, cols] = attn.astype(bf16)
            conv_ref[:, cols] = conv.astype(bf16)
            mg = (_sigmoid(ga_ref[:, cols]) * attn + _sigmoid(gc_ref[:, cols]) * conv).astype(bf16)
            mg_ref[:, cols] = mg
            x1 = x1 + _dot(mg, wo_ref[cols, :])
        x1_ref[...] = x1
        r = lax.rsqrt(jnp.mean(x1 * x1, axis=-1, keepdims=True) + EPS)
        n2_ref[...] = (x1 * r * g_ref[...]).astype(bf16)

    tile = lambda cb=0: pl.BlockSpec((tm, D), lambda i: (i, cb))
    wfull = pl.BlockSpec((D, D), lambda i: (0, 0), pipeline_mode=pl.Buffered(1))
    return pl.pallas_call(
        body, grid=(T // tm,),
        in_specs=[tile(), tile(), tile(C_GA), tile(C_GC), tile(), wfull, wfull, wfull,
                  pl.BlockSpec((1, D), lambda i: (0, 0))],
        out_specs=[tile()] * 5,
        out_shape=[SDS((T, D), bf16), SDS((T, D), bf16), SDS((T, D), bf16), SDS((T, D), f32), SDS((T, D), bf16)],
        name="mix_out", compiler_params=_cparams(("parallel",), 48))(o, h3, proj, proj, x, w_attn_o, w_conv_out, w_out, g_mlp)


def _mlp_fwd(n2, w1, w2, x1, tgt):
    T = n2.shape[0]
    tm, tf = 512, 1024

    def body(n2_ref, w1_ref, w2_ref, x1_ref, t_ref, hm_ref, dy_ref, dyb_ref, loss_ref):
        @pl.when(pl.program_id(0) == 0)
        def _():
            loss_ref[...] = jnp.zeros_like(loss_ref)

        n2v = n2_ref[...]
        for c in range(DFF // tf):
            r = jnp.maximum(_dot(n2v, w1_ref[:, c * tf:(c + 1) * tf]), 0.0)
            hm_ref[:, c * tf:(c + 1) * tf] = (r * r).astype(bf16)
        e = x1_ref[...] + _dot(hm_ref[...], w2_ref[...]) - t_ref[...]
        dy = e * (1.0 / D)
        dy_ref[...] = dy
        dyb_ref[...] = dy.astype(bf16)
        loss_ref[...] += 0.5 * jnp.sum(jnp.sum(e * e, axis=-1, keepdims=True) * (1.0 / D))

    row = pl.BlockSpec((tm, D), lambda i: (i, 0))
    once = pl.Buffered(1)
    return pl.pallas_call(
        body, grid=(T // tm,),
        in_specs=[row, pl.BlockSpec((D, DFF), lambda i: (0, 0), pipeline_mode=once),
                  pl.BlockSpec((DFF, D), lambda i: (0, 0), pipeline_mode=once), row, row],
        out_specs=[pl.BlockSpec((tm, DFF), lambda i: (i, 0)), row, row, pl.BlockSpec((8, 128), lambda i: (0, 0))],
        out_shape=[SDS((T, DFF), bf16), SDS((T, D), f32), SDS((T, D), bf16), SDS((8, 128), f32)],
        name="mlp_fwd", compiler_params=_cparams(("arbitrary",), 56))(n2, w1, w2, x1, tgt)


def _rms_bwd(xv, g, dn, dres):
    r = lax.rsqrt(jnp.mean(xv * xv, axis=-1, keepdims=True) + EPS)
    gd = dn * g
    dx = dres + r * gd - xv * (r * r * r) * jnp.mean(xv * gd, axis=-1, keepdims=True)
    dg = jnp.sum(dn * xv * r, axis=0, keepdims=True)
    return dx, dg


def _mlp_bwd(dy, dyb, hmid, w1, w2, x1, g_mlp):
    T = dy.shape[0]
    tm, tf = 512, 1024

    def body(dy_ref, dyb_ref, hm_ref, w1_ref, w2_ref, x1_ref, g_ref, df_ref, dx_ref, dxb_ref, dg_ref):
        @pl.when(pl.program_id(0) == 0)
        def _():
            dg_ref[...] = jnp.zeros_like(dg_ref)

        dyb = dyb_ref[...]
        for c in range(DFF // tf):
            cols = slice(c * tf, (c + 1) * tf)
            d_hm = _dot_nt(dyb, w2_ref[cols, :])
            df_ref[:, cols] = (d_hm * (2.0 * jnp.sqrt(hm_ref[:, cols].astype(f32)))).astype(bf16)
        dn = _dot_nt(df_ref[...], w1_ref[...])
        dx, dg = _rms_bwd(x1_ref[...], g_ref[...], dn, dy_ref[...])
        dx_ref[...] = dx
        dxb_ref[...] = dx.astype(bf16)
        dg_ref[...] += dg

    row = pl.BlockSpec((tm, D), lambda i: (i, 0))
    wide = pl.BlockSpec((tm, DFF), lambda i: (i, 0))
    vec = pl.BlockSpec((1, D), lambda i: (0, 0))
    once = pl.Buffered(1)
    return pl.pallas_call(
        body, grid=(T // tm,),
        in_specs=[row, row, wide, pl.BlockSpec((D, DFF), lambda i: (0, 0), pipeline_mode=once),
                  pl.BlockSpec((DFF, D), lambda i: (0, 0), pipeline_mode=once), row, vec],
        out_specs=[wide, row, row, vec],
        out_shape=[SDS((T, DFF), bf16), SDS((T, D), f32), SDS((T, D), bf16), SDS((1, D), f32)],
        name="mlp_bwd", compiler_params=_cparams(("arbitrary",), 56))(dy, dyb, hmid, w1, w2, x1, g_mlp)


def _wgrad(a, b, name, tn=1024):
    T, M = a.shape
    N = b.shape[1]
    tmm, tk = min(M, 1024), min(T, 2048)

    def body(a_ref, b_ref, o_ref):
        @pl.when(pl.program_id(2) == 0)
        def _():
            o_ref[...] = jnp.zeros_like(o_ref)

        o_ref[...] += _dot_tn(a_ref[...], b_ref[...])

    return pl.pallas_call(
        body, grid=(M // tmm, N // tn, T // tk),
        in_specs=[pl.BlockSpec((tk, tmm), lambda m, n, t: (t, m)), pl.BlockSpec((tk, tn), lambda m, n, t: (t, n))],
        out_specs=pl.BlockSpec((tmm, tn), lambda m, n, t: (m, n)),
        out_shape=SDS((M, N), f32),
        name=name, compiler_params=_cparams(("parallel", "parallel", "arbitrary"), 40))(a, b)


def _mix_bwd(dx1b, proj, attn, conv, w_attn_o, w_conv_out, w_out, comm=None):
    T = dx1b.shape[0]
    tm = 512

    def body(dx_ref, ga_ref, gc_ref, attn_ref, conv_ref, wa_ref, wc_ref, wo_ref,
             dat_ref, dcv_ref, do_ref, dh3_ref, dga_ref, dgc_ref):
        d_o, d_h3 = None, None
        for j in range(D // MIX_CHUNK):
            cols = slice(j * MIX_CHUNK, (j + 1) * MIX_CHUNK)
            dm = _dot_nt(dx_ref[...], wo_ref[cols, :])
            sa = _sigmoid(ga_ref[:, cols])
            sc = _sigmoid(gc_ref[:, cols])
            dat = (dm * sa).astype(bf16)
            dcv = (dm * sc).astype(bf16)
            dat_ref[:, cols] = dat
            dcv_ref[:, cols] = dcv
            dga_ref[:, cols] = (dm * attn_ref[:, cols].astype(f32) * sa * (1.0 - sa)).astype(bf16)
            dgc_ref[:, cols] = (dm * conv_ref[:, cols].astype(f32) * sc * (1.0 - sc)).astype(bf16)
            part_o = _dot_nt(dat, wa_ref[:, cols])
            part_h = _dot_nt(dcv, wc_ref[:, cols])
            d_o = part_o if d_o is None else d_o + part_o
            d_h3 = part_h if d_h3 is None else d_h3 + part_h
        do_ref[...] = d_o.astype(bf16)
        dh3_ref[...] = d_h3

    tile = lambda cb=0: pl.BlockSpec((tm, D), lambda i: (i, cb))
    wfull = pl.BlockSpec((D, D), lambda i: (0, 0), pipeline_mode=pl.Buffered(1))
    return _call(
        body, (dx1b, proj, proj, attn, conv, w_attn_o, w_conv_out, w_out), grid=(T // tm,),
        in_specs=[tile(), tile(C_GA), tile(C_GC), tile(), tile(), wfull, wfull, wfull],
        out_specs=[tile()] * 6,
        out_shape=[SDS((T, D), bf16), SDS((T, D), bf16), SDS((T, D), bf16), SDS((T, D), f32),
                   SDS((T, D), bf16), SDS((T, D), bf16)],
        name="mix_bwd", sem=("parallel",), vmem_mb=48, comm=comm)


def _conv_ln_bwd(dh3, h1, ln_g, ln_b, comm=None):
    T = dh3.shape[0]
    tt = 256

    def body(d_ref, h1_ref, lg_ref, lb_ref, dh1_ref, acc_ref):
        @pl.when(pl.program_id(0) == 0)
        def _():
            acc_ref[...] = jnp.zeros_like(acc_ref)

        h1 = h1_ref[...]
        mu = jnp.mean(h1, axis=-1, keepdims=True)
        xc = h1 - mu
        rstd = lax.rsqrt(jnp.mean(xc * xc, axis=-1, keepdims=True) + EPS)
        xh = xc * rstd
        h2 = xh * lg_ref[...] + lb_ref[...]
        sg = _sigmoid(h2)
        dh2 = d_ref[...] * (sg * (1.0 + h2 * (1.0 - sg)))
        dxh = dh2 * lg_ref[...]
        dh1 = rstd * (dxh - jnp.mean(dxh, axis=-1, keepdims=True) - xh * jnp.mean(dxh * xh, axis=-1, keepdims=True))
        dh1_ref[...] = dh1
        acc_ref[0:1, :] += jnp.sum(dh2 * xh, axis=0, keepdims=True)
        acc_ref[1:2, :] += jnp.sum(dh2, axis=0, keepdims=True)
        acc_ref[2:3, :] += jnp.sum(dh1, axis=0, keepdims=True)

    tile = pl.BlockSpec((tt, D), lambda i: (i, 0))
    vec = pl.BlockSpec((1, D), lambda i: (0, 0))
    return _call(
        body, (dh3, h1, ln_g, ln_b), grid=(T // tt,),
        in_specs=[tile, tile, vec, vec],
        out_specs=[tile, pl.BlockSpec((8, D), lambda i: (0, 0))],
        out_shape=[SDS((T, D), f32), SDS((8, D), f32)],
        name="conv_ln_bwd", sem=("arbitrary",), vmem_mb=32, comm=comm)


def _conv_bwd(dh1, proj, w_dw, comm=None):
    T = dh1.shape[0]
    tt, ch = 256, 32
    nt = T // tt

    def body(d_ref, dn_ref, a_ref, g_ref, ah_ref, gh_ref, w_ref, da_ref, dg_ref, gw_ref, dsh, hsh, dh0, gacc):
        i = pl.program_id(0)

        @pl.when(i == 0)
        def _():
            gacc[...] = jnp.zeros_like(gacc)

        dsh[0, 0:tt, :] = d_ref[...]
        dsh[0, tt:tt + HALO, :] = jnp.where(i < nt - 1, dn_ref[...], 0.0)
        hsh[0, 0:HALO, :] = jnp.where(i > 0, ah_ref[...] * _sigmoid(gh_ref[...]), 0.0)
        hsh[0, HALO:HALO + tt, :] = a_ref[...] * _sigmoid(g_ref[...])
        _shifted_copies(dsh, tt + HALO - SUB)
        _shifted_copies(hsh, tt + HALO - SUB)

        def chunk(c, carry):
            r0 = pl.multiple_of(c * ch, ch)
            lo, hi = _conv_taps(dsh, w_ref, r0, ch, lambda j: (CW - 1) - j, lambda ln: jnp.zeros((ch, 512), f32))
            dh0[pl.ds(r0, ch), 0:512] = lo
            dh0[pl.ds(r0, ch), 512:1024] = hi
            for hf in range(2):
                ln = slice(512 * hf, 512 * hf + 512)
                dv = dsh[0, pl.ds(r0, ch), ln]
                for j in range(CW):
                    pr = dv * _shifted_rows(hsh, r0, HALO - (CW - 1) + j, ch, ln)
                    gacc[8 * j:8 * j + 8, ln] += pr[0:8] + pr[8:16] + pr[16:24] + pr[24:32]
            return carry

        lax.fori_loop(0, tt // ch, chunk, 0)
        a = a_ref[...]
        sg = _sigmoid(g_ref[...])
        d0 = dh0[...]
        da_ref[...] = (d0 * sg).astype(bf16)
        dg_ref[...] = (d0 * a * sg * (1.0 - sg)).astype(bf16)

        @pl.when(i == nt - 1)
        def _():
            gw_ref[...] = jnp.zeros_like(gw_ref)
            for j in range(CW):
                gw_ref[j:j + 1, :] = jnp.sum(gacc[8 * j:8 * j + 8, :], axis=0, keepdims=True)

    hpt = tt // HALO
    tile = lambda cb=0: pl.BlockSpec((tt, D), lambda i: (i, cb))
    halo_prev = lambda cb: pl.BlockSpec((HALO, D), lambda i: (jnp.maximum(i * hpt - 1, 0), cb))
    halo_next = pl.BlockSpec((HALO, D), lambda i: (jnp.minimum((i + 1) * hpt, T // HALO - 1), 0))
    wspec = pl.BlockSpec((HALO, D), lambda i: (0, 0))
    return _call(
        body, (dh1, dh1, proj, proj, proj, proj, w_dw), grid=(nt,),
        in_specs=[tile(), halo_next, tile(C_A), tile(C_G), halo_prev(C_A), halo_prev(C_G), wspec],
        out_specs=[tile(), tile(), wspec],
        out_shape=[SDS((T, D), bf16), SDS((T, D), bf16), SDS((HALO, D), f32)],
        scratch_shapes=[pltpu.VMEM((SUB, tt + HALO, D), f32), pltpu.VMEM((SUB, HALO + tt, D), f32),
                        pltpu.VMEM((tt, D), f32), pltpu.VMEM((8 * HALO, D), f32)],
        name="conv_bwd", sem=("arbitrary",), vmem_mb=48, comm=comm)


def _attn_bwd(qn, kk, vv, bias, sinks, o, do, lse, comm=None):
    T = qn.shape[0]
    nb = T // BLK

    def body(s_ref, q_ref, kc_ref, kp_ref, vc_ref, vp_ref, b_ref, o_ref, do_ref, lse_ref,
             dq_ref, dkc_ref, dkp_ref, dvc_ref, dvp_ref, dsk_ref, dsa_ref):
        n = pl.program_id(0)

        @pl.when(n == 0)
        def _():
            dsk_ref[...] = jnp.zeros_like(dsk_ref)
            dsa_ref[...] = jnp.zeros_like(dsa_ref)

        @pl.when(n == nb)
        def _():
            dkp_ref[...] = jnp.zeros_like(dkp_ref)
            dvp_ref[...] = jnp.zeros_like(dvp_ref)

        @pl.when(n < nb)
        def _():
            from_prev = _from_prev_block()
            no_key = jnp.logical_and(from_prev, n == 0)
            lo = _low_head_lanes()
            dups = {"kc": [], "kp": [], "vc": [], "vp": []}
            for h in range(NKV):
                c = 256 * h
                qs = _rows2(q_ref, c)
                dos = _rows2(do_ref, c)
                sc = _dot_nt(qs, _rows2(kc_ref, c))
                sp = _dot_nt(qs, _rows2(kp_ref, c))
                dpc = _dot_nt(dos, _rows2(vc_ref, c))
                dpp = _dot_nt(dos, _rows2(vp_ref, c))
                kstack = jnp.concatenate([kp_ref[:, c:c + 128], kc_ref[:, c:c + 128],
                                          kp_ref[:, c + 128:c + 256], kc_ref[:, c + 128:c + 256]], axis=0)
                p_c, p_p, ds_c, ds_p = [], [], [], []
                for pr in range(2):
                    cc = c + 128 * pr
                    prod = do_ref[:, cc:cc + 128].astype(f32) * o_ref[:, cc:cc + 128].astype(f32)
                    d_lo = jnp.sum(jnp.where(lo, prod, 0.0), axis=-1, keepdims=True)
                    d_hi = jnp.sum(prod, axis=-1, keepdims=True) - d_lo
                    row_pc, row_pp, row_dc, row_dp = [], [], [], []
                    for e in range(2):
                        hq = 4 * h + 2 * pr + e
                        rows, cols = slice(128 * pr, 128 * pr + 128), slice(128 * e, 128 * e + 128)
                        delta = d_lo if e == 0 else d_hi
                        lse = lse_ref[:, hq:hq + 1]
                        s = jnp.where(from_prev, sp[rows, cols], sc[rows, cols]) + b_ref[hq]
                        p = jnp.where(no_key, 0.0, jnp.exp(s - lse))
                        ds = p * (jnp.where(from_prev, dpp[rows, cols], dpc[rows, cols]) - delta)
                        dsa_ref[hq] += ds
                        dsk_ref[hq] += jnp.broadcast_to(-jnp.sum(jnp.exp(s_ref[0, hq] - lse) * delta), (8, 128))
                        row_pc.append(jnp.where(from_prev, 0.0, p).astype(bf16))
                        row_pp.append(jnp.where(from_prev, p, 0.0).astype(bf16))
                        row_dc.append(jnp.where(from_prev, 0.0, ds).astype(bf16))
                        row_dp.append(jnp.where(from_prev, ds, 0.0).astype(bf16))
                    dq_ref[:, cc:cc + 128] = _dot(jnp.concatenate([row_dp[0], row_dc[0], row_dp[1], row_dc[1]], axis=1), kstack)
                    p_c.append(jnp.concatenate(row_pc, axis=1))
                    p_p.append(jnp.concatenate(row_pp, axis=1))
                    ds_c.append(jnp.concatenate(row_dc, axis=1))
                    ds_p.append(jnp.concatenate(row_dp, axis=1))

                def to_keys(m2, rhs):
                    x2 = _dot_tn(jnp.concatenate(m2, axis=0), rhs)
                    x = jnp.where(lo, x2[0:128], x2[128:256])
                    return x + pltpu.roll(x, HD, 1)

                dups["kc"].append(to_keys(ds_c, qs))
                dups["kp"].append(to_keys(ds_p, qs))
                dups["vc"].append(to_keys(p_c, dos))
                dups["vp"].append(to_keys(p_p, dos))
            for key, ref in (("kc", dkc_ref), ("kp", dkp_ref), ("vc", dvc_ref), ("vp", dvp_ref)):
                d = dups[key]
                ref[:, 0:128] = jnp.where(lo, d[0], d[1])
                ref[:, 128:256] = jnp.where(lo, d[2], d[3])

    clamp = lambda n: jnp.minimum(n, nb - 1)
    blk = lambda f: pl.BlockSpec((BLK, D), f)
    cur = lambda n: (clamp(n), 0)
    prev = lambda n: (jnp.maximum(clamp(n) - 1, 0), 0)
    back = lambda n: (jnp.maximum(n - 1, 0), 0)
    kvb = lambda f: pl.BlockSpec((BLK, NKV * HD), f)
    return _call(
        body, (sinks, qn, kk, kk, vv, vv, bias, o, do, lse), grid=(nb + 1,),
        in_specs=[pl.BlockSpec(memory_space=pltpu.SMEM), blk(cur), blk(cur), blk(prev), blk(cur), blk(prev),
                  pl.BlockSpec((NQ, BLK, BLK), lambda n: (0, 0, 0)), blk(cur), blk(cur),
                  pl.BlockSpec((BLK, NQ), cur)],
        out_specs=[blk(cur), kvb(cur), kvb(back), kvb(cur), kvb(back),
                   pl.BlockSpec((NQ, 8, 128), lambda n: (0, 0, 0)),
                   pl.BlockSpec((NQ, BLK, BLK), lambda n: (0, 0, 0))],
        out_shape=[SDS((T, D), f32)] + [SDS((T, NKV * HD), f32)] * 4 + [SDS((NQ, 8, 128), f32), SDS((NQ, BLK, BLK), f32)],
        name="attn_bwd", sem=("arbitrary",), vmem_mb=40, comm=comm)


def _bias_bwd(dsa):
    def body(bk_ref, ds_ref, out_ref):
        bk = bk_ref[...]
        lane = lax.broadcasted_iota(jnp.int32, (1, 128), 1)
        for h in range(NQ):
            ds = ds_ref[h]
            row = jnp.zeros((1, 128), f32)
            for b in range(NBUCKET):
                row = jnp.where(lane == b, jnp.sum(jnp.where(bk == b, ds, 0.0)), row)
            out_ref[h:h + 1, :] = row

    return pl.pallas_call(body, out_shape=SDS((NQ, 128), f32), name="bias_bwd")(jnp.asarray(_bucket_tile()), dsa)


def _qkv_bwd(proj, gq2, gk2, dqn, dkc, dkp, dvc, dvp):
    T = proj.shape[0]
    tm = 512

    def body(q_ref, kv_ref, gq_ref, gk_ref, dq_ref, dkc_ref, dkp_ref, dvc_ref, dvp_ref,
             oq_ref, okv_ref, ggq_ref, ggk_ref):
        @pl.when(pl.program_id(0) == 0)
        def _():
            ggq_ref[...] = jnp.zeros_like(ggq_ref)
            ggk_ref[...] = jnp.zeros_like(ggk_ref)

        bd = _head_blockdiag()

        def norm_bwd(z, dy, g, scale):
            r = lax.rsqrt(_head_sums(z * z, bd) * (1.0 / HD) + EPS)
            gd = dy * g * scale
            dz = r * gd - z * (r * r * r) * _head_sums(z * gd, bd) * (1.0 / HD)
            return dz, jnp.sum(dy * scale * z * r, axis=0, keepdims=True)

        gq = jnp.zeros((1, 128), f32)
        for p in range(NQ // 2):
            ln = slice(128 * p, 128 * p + 128)
            dz, dg = norm_bwd(q_ref[:, ln], dq_ref[:, ln], gq_ref[...], HD ** -0.5)
            oq_ref[:, ln] = dz.astype(bf16)
            gq = gq + dg
        ggq_ref[...] += gq + pltpu.roll(gq, HD, 1)
        gk = jnp.zeros((1, 128), f32)
        for p in range(NKV // 2):
            ln = slice(128 * p, 128 * p + 128)
            dz, dg = norm_bwd(kv_ref[:, ln], dkc_ref[:, ln] + dkp_ref[:, ln], gk_ref[...], 1.0)
            okv_ref[:, ln] = dz.astype(bf16)
            gk = gk + dg
        ggk_ref[...] += gk + pltpu.roll(gk, HD, 1)
        okv_ref[:, 256:512] = (dvc_ref[...] + dvp_ref[...]).astype(bf16)

    vec = pl.BlockSpec((1, 128), lambda i: (0, 0))
    kvb = pl.BlockSpec((tm, NKV * HD), lambda i: (i, 0))
    return pl.pallas_call(
        body, grid=(T // tm,),
        in_specs=[pl.BlockSpec((tm, D), lambda i: (i, C_Q)), pl.BlockSpec((tm, 512), lambda i: (i, C_KV)), vec, vec,
                  pl.BlockSpec((tm, D), lambda i: (i, 0)), kvb, kvb, kvb, kvb],
        out_specs=[pl.BlockSpec((tm, D), lambda i: (i, 0)), pl.BlockSpec((tm, 512), lambda i: (i, 0)), vec, vec],
        out_shape=[SDS((T, D), bf16), SDS((T, 512), bf16), SDS((1, 128), f32), SDS((1, 128), f32)],
        name="qkv_bwd", compiler_params=_cparams(("arbitrary",), 32))(proj, proj, gq2, gk2, dqn, dkc, dkp, dvc, dvp)


def _inproj_bwd(pieces, w_in, x, dx1, g_mix, comm=None):
    T = x.shape[0]
    tm = 512
    widths = [p.shape[1] for p in pieces]
    offs = [sum(widths[:i]) for i in range(len(widths))]
    assert sum(widths) == INW

    def body(*refs):
        p_refs, (w_ref, x_ref, dx1_ref, g_ref, dx_ref, dg_ref) = refs[:len(pieces)], refs[len(pieces):]

        @pl.when(pl.program_id(0) == 0)
        def _():
            dg_ref[...] = jnp.zeros_like(dg_ref)

        du = None
        for p_ref, off, wd in zip(p_refs, offs, widths):
            part = _dot_nt(p_ref[...], w_ref[:, off:off + wd])
            du = part if du is None else du + part
        dx, dg = _rms_bwd(x_ref[...], g_ref[...], du, dx1_ref[...])
        dx_ref[...] = dx
        dg_ref[...] += dg

    row = pl.BlockSpec((tm, D), lambda i: (i, 0))
    vec = pl.BlockSpec((1, D), lambda i: (0, 0))
    return _call(
        body, (*pieces, w_in, x, dx1, g_mix), grid=(T // tm,),
        in_specs=[pl.BlockSpec((tm, wd), lambda i: (i, 0)) for wd in widths]
        + [pl.BlockSpec((D, INW), lambda i: (0, 0), pipeline_mode=pl.Buffered(1)), row, row, vec],
        out_specs=[row, vec],
        out_shape=[SDS((T, D), f32), SDS((1, D), f32)],
        name="inproj_bwd", sem=("arbitrary",), vmem_mb=48, comm=comm)


def _to_internal_cols(w):
    return jnp.concatenate([w[..., 0:1024], w[..., 1536:INW], w[..., 1024:1536]], axis=-1)


def _forward_backward(x, tgt, w, placed, chip_core):
    def sums(names, grads, got):
        res = [_pair_sum(nm, grads[nm], got_nm, chip_core) for nm, got_nm in zip(names, got)]
        return {nm: r[0] for nm, r in zip(names, res)}, {nm: r[1] for nm, r in zip(names, res)}

    first = ["w_in", "w_dw"]
    w_in, w_dw = _run_comm(_gather_comm({nm: placed[nm] for nm in first}), "gather_first")
    w_in = _to_internal_cols(w_in)
    gq2 = jnp.tile(w["q_norm_g"], (1, 2))
    gk2 = jnp.tile(w["k_norm_g"], (1, 2))
    proj, u = _rms_inproj(x, w["norm_mix_g"], w_in)
    qn, kk, vv = _qk_prep(proj, gq2, gk2)
    bias = _bias_tiles(w["rel_bias"])
    rest = [nm for nm in BIG if nm != "w_in"]
    (o, lse), full = _attn_fwd(qn, kk, vv, bias, w["attn_sinks"], comm=_gather_comm({nm: placed[nm] for nm in rest}))
    full = dict(zip(rest, full))
    h1, h3 = _glu_conv_fwd(proj, w_dw, w["b_dw"], w["conv_ln_g"], w["conv_ln_b"])
    attn, conv, merged, x1, n2 = _mix_out(o, h3, proj, x, full["w_attn_o"], full["w_conv_out"], full["w_out"],
                                          w["norm_mlp_g"])
    hmid, dy, dyb, loss = _mlp_fwd(n2, full["w_ff1"], full["w_ff2"], x1, tgt)

    g, cp, own = {}, {}, {}
    df1, dx1, dx1b, g["norm_mlp_g"] = _mlp_bwd(dy, dyb, hmid, full["w_ff1"], full["w_ff2"], x1, w["norm_mlp_g"])
    ff = ["w_ff1", "w_ff2"]
    gff = {"w_ff2": _wgrad(hmid, dyb, "wgrad_ff2"), "w_ff1": _wgrad(n2, df1, "wgrad_ff1")}
    (dat, dcv, do, dh3, dga, dgc), got = _mix_bwd(dx1b, proj, attn, conv, full["w_attn_o"], full["w_conv_out"],
                                                  full["w_out"], comm=_pair_exchange_comm(gff, ff))
    cp_ff, own_ff = sums(ff, gff, got)
    sq = ["w_out", "w_attn_o", "w_conv_out"]
    gsq = {"w_out": _wgrad(merged, dx1b, "wgrad_out"), "w_attn_o": _wgrad(o, dat, "wgrad_attn_o"),
           "w_conv_out": _wgrad(h3, dcv, "wgrad_conv_out")}
    (dh1, lnacc), got = _conv_ln_bwd(dh3, h1, w["conv_ln_g"], w["conv_ln_b"], comm=_pair_exchange_comm(gsq, sq))
    cp_sq, own_sq = sums(sq, gsq, got)
    cp, own = {**cp_ff, **cp_sq}, {**own_ff, **own_sq}
    g["conv_ln_g"], g["conv_ln_b"], g["b_dw"] = lnacc[0:1], lnacc[1:2], lnacc[2:3]
    five = ff + sq
    (da, dg, g["w_dw"]), rc = _conv_bwd(dh1, proj, w_dw, comm=_chip_exchange_comm(cp, five))
    tot = {nm: _chip_sum(nm, own[nm], rc_nm, chip_core) for nm, rc_nm in zip(five, rc)}
    (dqn, dkc, dkp, dvc, dvp, dsk, dsa), shards = _attn_bwd(qn, kk, vv, bias, w["attn_sinks"], o, do, lse,
                                                            comm=_pair_share_comm(tot, five))
    shards = dict(zip(five, shards))
    g["attn_sinks"] = dsk[:, 0, 0].reshape(1, NQ)
    g["rel_bias"] = _bias_bwd(dsa)[:, 0:NBUCKET].T
    dq, dkv, ggq, ggk = _qkv_bwd(proj, gq2, gk2, dqn, dkc, dkp, dvc, dvp)
    g["q_norm_g"], g["k_norm_g"] = ggq[:, 0:HD], ggk[:, 0:HD]
    pieces = [dq, da, dg, dga, dgc, dkv]
    names = ["q", "a", "g", "ga", "gc", "kv"]
    gw = {nm: _wgrad(u, p, "wgrad_in_" + nm, tn=p.shape[1] if p.shape[1] < 1024 else 1024) for nm, p in zip(names, pieces)}
    gin = {"w_in": jnp.concatenate([gw["q"], gw["kv"], gw["a"], gw["g"], gw["ga"], gw["gc"]], axis=1)}
    got = _run_comm(_pair_exchange_comm(gin, ["w_in"]), "rs_pair_exchange_in")
    cp_in, own_in = sums(["w_in"], gin, got)
    (grad_x, g["norm_mix_g"]), rc = _inproj_bwd(pieces, w_in, x, dx1, w["norm_mix_g"],
                                                comm=_chip_exchange_comm(cp_in, ["w_in"]))
    tot = {"w_in": _chip_sum("w_in", own_in["w_in"], rc[0], chip_core)}
    shards["w_in"] = _run_comm(_pair_share_comm(tot, ["w_in"]), "rs_pair_share_in")[0]
    return loss[0, 0], grad_x, g, shards


BIG = ["w_in", "w_attn_o", "w_conv_out", "w_out", "w_ff1", "w_ff2"]
SHARD_AXIS = {"w_in": 1, "w_attn_o": 0, "w_conv_out": 0, "w_out": 0, "w_ff1": 1, "w_ff2": 0, "w_dw": 1}
SHARD_SHAPE = {"w_in": (D, INW // 4), "w_attn_o": (D // 4, D), "w_conv_out": (D // 4, D), "w_out": (D // 4, D),
               "w_ff1": (D, DFF // 4), "w_ff2": (DFF // 4, D), "w_dw": (HALO, D // 4)}


def _position():
    x, y, c = lax.axis_index("x"), lax.axis_index("y"), lax.axis_index("c")
    other_chips = [(1 - x, y), (x, 1 - y), (1 - x, 1 - y)]
    return x, y, c, 2 * x + y, other_chips


def _shard_window(name, full_ref, s, half=None):
    R, C = SHARD_SHAPE[name]
    r0, nr = (0, R) if half is None else (half * (R // 2), R // 2)
    if SHARD_AXIS[name] == 1:
        return full_ref.at[pl.ds(r0, nr), pl.ds(s * C, C)]
    return full_ref.at[pl.ds(s * R + r0, nr), :]


def _remote(src, dst, send_sems, recv_sems, k, device):
    return pltpu.make_async_remote_copy(src_ref=src, dst_ref=dst, send_sem=send_sems.at[k], recv_sem=recv_sems.at[k],
                                        device_id=device, device_id_type=MESH)


def _full_shape(nm):
    R, C = SHARD_SHAPE[nm]
    return (R, 4 * C) if SHARD_AXIS[nm] == 1 else (4 * R, C)


def _place_shard(nm, shard, chip_arr, dtype):
    R, C = SHARD_SHAPE[nm]
    tr = min(R, 256)
    if SHARD_AXIS[nm] == 1:
        o_map = lambda i, ch: (i, ch[0])
    else:
        o_map = lambda i, ch: (ch[0] * (R // tr) + i, 0)

    def body(ch_ref, s_ref, o_ref):
        o_ref[...] = s_ref[...].astype(dtype)

    return pl.pallas_call(
        body,
        grid_spec=pltpu.PrefetchScalarGridSpec(
            num_scalar_prefetch=1, grid=(R // tr,),
            in_specs=[pl.BlockSpec((tr, C), lambda i, ch: (i, 0))], out_specs=pl.BlockSpec((tr, C), o_map)),
        out_shape=SDS(_full_shape(nm), dtype), name="place_" + nm,
        compiler_params=_cparams(("parallel",), 32))(chip_arr, shard)


def _gather_comm(placed):
    names = list(placed)
    n = len(names)

    def copies(cout, send, recv):
        x, y, c, chip, chips = _position()
        for a, nm in enumerate(names):
            for j, (cx, cy) in enumerate(chips):
                def ici(s, a=a, nm=nm, j=j, cx=cx, cy=cy):
                    w = _shard_window(nm, cout[a], s, c)
                    return _remote(w, w, send, recv, 6 * a + j, (cx, cy, c))

                def d2d(h, a=a, nm=nm, j=j, cx=cx, cy=cy):
                    w = _shard_window(nm, cout[a], 2 * cx + cy, h)
                    return _remote(w, w, send, recv, 6 * a + 3 + j, (x, y, 1 - c))

                yield ici, d2d, chip, 2 * cx + cy, c

    def start(cin, cout, send, recv):
        for ici, d2d, chip, s, c in copies(cout, send, recv):
            ici(chip).start()

    def mid(cin, cout, send, recv):
        for ici, d2d, chip, s, c in copies(cout, send, recv):
            ici(s).wait_recv()
            d2d(c).start()

    def finish(cin, cout, send, recv):
        for ici, d2d, chip, s, c in copies(cout, send, recv):
            d2d(1 - c).wait_recv()
        for ici, d2d, chip, s, c in copies(cout, send, recv):
            ici(chip).wait_send()
            d2d(c).wait_send()

    return _Comm([placed[nm] for nm in names], [SDS(placed[nm].shape, placed[nm].dtype) for nm in names], 6 * n,
                 start, finish, mid, aliases={a: a for a in range(n)})


def _half_rows(nm):
    return SHARD_SHAPE[nm][0] // 2


RS_TILE = 128


def _exchange_comm(ins, out_shapes, copies, n_sems, aliases=None):
    def start(cin, cout, send, recv):
        for cp in copies(cin, cout, send, recv):
            cp.start()

    def finish(cin, cout, send, recv):
        for cp in copies(cin, cout, send, recv):
            cp.wait()

    return _Comm(ins, out_shapes, n_sems, start, finish, aliases=aliases)


def _pair_exchange_comm(grads, names):
    def copies(cin, cout, send, recv):
        x, y, c, chip, chips = _position()
        return [_remote(_shard_window(nm, cin[a], s, 1 - c), cout[a].at[s], send, recv, 4 * a + s, (x, y, 1 - c))
                for a, nm in enumerate(names) for s in range(4)]

    return _exchange_comm([grads[nm] for nm in names],
                          [SDS((4, _half_rows(nm), SHARD_SHAPE[nm][1]), f32) for nm in names], copies, 4 * len(names))


def _pair_sum(nm, g, got, chip_core):
    R, C = SHARD_SHAPE[nm]
    hr = R // 2
    nt = hr // RS_TILE
    if SHARD_AXIS[nm] == 1:
        g_map = lambda i, s, sc: (sc[1] * nt + i, s)
    else:
        g_map = lambda i, s, sc: (s * (R // RS_TILE) + sc[1] * nt + i, 0)

    def body(sc_ref, g_ref, got_ref, o16_ref, own_ref):
        v = g_ref[...] + got_ref[0]
        o16_ref[0] = v.astype(bf16)

        @pl.when(pl.program_id(1) == sc_ref[0])
        def _():
            own_ref[...] = v

    blk3 = pl.BlockSpec((1, RS_TILE, C), lambda i, s, sc: (s, i, 0))
    return pl.pallas_call(
        body,
        grid_spec=pltpu.PrefetchScalarGridSpec(
            num_scalar_prefetch=1, grid=(nt, 4),
            in_specs=[pl.BlockSpec((RS_TILE, C), g_map), blk3],
            out_specs=[blk3, pl.BlockSpec((RS_TILE, C), lambda i, s, sc: (i, 0))]),
        out_shape=[SDS((4, hr, C), bf16), SDS((hr, C), f32)], name="rs_pair_sum_" + nm,
        compiler_params=_cparams(("parallel", "arbitrary"), 32))(chip_core, g, got)


def _chip_exchange_comm(cp, names):
    def copies(cin, cout, send, recv):
        x, y, c, chip, chips = _position()
        return [_remote(cin[a].at[2 * cx + cy], cout[a].at[j], send, recv, 3 * a + j, (cx, cy, c))
                for a, nm in enumerate(names) for j, (cx, cy) in enumerate(chips)]

    return _exchange_comm([cp[nm] for nm in names],
                          [SDS((3, _half_rows(nm), SHARD_SHAPE[nm][1]), bf16) for nm in names], copies, 3 * len(names))


def _chip_sum(nm, own, rc, chip_core):
    R, C = SHARD_SHAPE[nm]
    nt = (R // 2) // RS_TILE

    def body(sc_ref, own_ref, rc_ref, o_ref):
        o_ref[...] = own_ref[...] + rc_ref[0].astype(f32) + rc_ref[1].astype(f32) + rc_ref[2].astype(f32)

    return pl.pallas_call(
        body,
        grid_spec=pltpu.PrefetchScalarGridSpec(
            num_scalar_prefetch=1, grid=(nt,),
            in_specs=[pl.BlockSpec((RS_TILE, C), lambda i, sc: (i, 0)),
                      pl.BlockSpec((3, RS_TILE, C), lambda i, sc: (0, i, 0))],
            out_specs=pl.BlockSpec((RS_TILE, C), lambda i, sc: (sc[1] * nt + i, 0))),
        out_shape=SDS((R, C), f32), name="rs_chip_sum_" + nm,
        compiler_params=_cparams(("parallel",), 32))(chip_core, own, rc)


def _pair_share_comm(tot, names):
    def copies(cin, cout, send, recv):
        x, y, c, chip, chips = _position()
        cps = []
        for a, nm in enumerate(names):
            hr = _half_rows(nm)
            mine = cout[a].at[pl.ds(c * hr, hr), :]
            cps.append(_remote(mine, mine, send, recv, a, (x, y, 1 - c)))
        return cps

    return _exchange_comm([tot[nm] for nm in names], [SDS(SHARD_SHAPE[nm], f32) for nm in names], copies, len(names),
                          aliases={a: a for a in range(len(names))})


SMALL_ROWS = 40


def _allreduce_small(block):
    def body(x_ref, out_ref, buf, send_sems, recv_sems, local_sem):
        x, y, c, chip, chips = _position()
        me, sibling = (x, y, c), (x, y, 1 - c)

        def slot(px, py, pc):
            return buf.at[4 * px + 2 * py + pc]

        def copy(k, block_of, to, src=None):
            return _remote(slot(*block_of) if src is None else src, slot(*block_of), send_sems, recv_sems, k, to)

        mine = pltpu.make_async_copy(x_ref, slot(*me), local_sem)
        mine.start()
        first = [copy(0, me, sibling, src=x_ref)] + [copy(1 + j, me, (*ch, c), src=x_ref) for j, ch in enumerate(chips)]
        for cp in first:
            cp.start()
        passed = [copy(4 + j, (*ch, c), sibling) for j, ch in enumerate(chips)]
        for j, ch in enumerate(chips):
            copy(1 + j, (*ch, c), me).wait_recv()
            passed[j].start()
        copy(0, sibling, me).wait_recv()
        for j, ch in enumerate(chips):
            copy(4 + j, (*ch, 1 - c), me).wait_recv()
        for cp in first + passed:
            cp.wait_send()
        mine.wait()
        acc = buf[0]
        for d in range(1, 8):
            acc = acc + buf[d]
        out_ref[...] = acc

    vm = pl.BlockSpec(memory_space=pltpu.VMEM)
    return pl.pallas_call(
        body, in_specs=[vm], out_specs=vm, out_shape=SDS((SMALL_ROWS, D), f32),
        scratch_shapes=[pltpu.VMEM((8, SMALL_ROWS, D), f32), pltpu.SemaphoreType.DMA((7,)), pltpu.SemaphoreType.DMA((7,)),
                        pltpu.SemaphoreType.DMA],
        name="allreduce_small")(block)


def _adamw(w, g, m, v, name):
    rows, cols = w.shape
    tr = 256 if rows % 256 == 0 else rows

    def body(w_ref, g_ref, m_ref, v_ref, d_ref, nm_ref, nv_ref):
        gv = g_ref[...]
        m2 = ADAM_B1 * m_ref[...] + (1.0 - ADAM_B1) * gv
        v2 = ADAM_B2 * v_ref[...] + (1.0 - ADAM_B2) * jnp.square(gv)
        m_hat = m2 / (1.0 - ADAM_B1 ** ADAM_STEP)
        v_hat = v2 / (1.0 - ADAM_B2 ** ADAM_STEP)
        d_ref[...] = -ADAM_LR * (m_hat / (jnp.sqrt(v_hat) + ADAM_EPS) + ADAM_WD * w_ref[...])
        nm_ref[...] = m2
        nv_ref[...] = v2

    spec = pl.BlockSpec((tr, cols), lambda i: (i, 0))
    return pl.pallas_call(body, grid=(rows // tr,), in_specs=[spec] * 4, out_specs=[spec] * 3,
                          out_shape=[SDS((rows, cols), f32)] * 3, name=name,
                          compiler_params=_cparams(("parallel",), 40))(w, g, m, v)


WEIGHTS = ["norm_mix_g", "w_in", "q_norm_g", "k_norm_g", "attn_sinks", "rel_bias", "w_attn_o", "w_dw", "b_dw",
           "conv_ln_g", "conv_ln_b", "w_conv_out", "w_out", "norm_mlp_g", "w_ff1", "w_ff2"]
ROW_VECS = ["norm_mix_g", "b_dw", "conv_ln_g", "conv_ln_b", "norm_mlp_g"]
MISC_ROW = 5
W_DW_ROW = 8


def _pack_small(vals, loss=None):
    misc = [vals["q_norm_g"].reshape(1, HD), vals["k_norm_g"].reshape(1, HD), vals["attn_sinks"].reshape(1, NQ),
            jnp.zeros((1, 1), f32) if loss is None else loss.reshape(1, 1), jnp.zeros((1, 111), f32),
            vals["rel_bias"].reshape(1, NBUCKET * NQ), jnp.zeros((1, 256), f32)]
    rows = [vals[nm].reshape(1, D) for nm in ROW_VECS] + [jnp.concatenate(misc, axis=1), jnp.zeros((2, D), f32)]
    return jnp.concatenate(rows, axis=0)


def _unpack_small(block):
    out = {nm: block[i:i + 1] for i, nm in enumerate(ROW_VECS)}
    misc = block[MISC_ROW]
    out["q_norm_g"] = misc[0:64].reshape(1, HD)
    out["k_norm_g"] = misc[64:128].reshape(1, HD)
    out["attn_sinks"] = misc[128:144].reshape(1, NQ)
    out["rel_bias"] = misc[256:768].reshape(NBUCKET, NQ)
    return out, misc[144]


def kernel(x, norm_mix_g, w_in, q_norm_g, k_norm_g, attn_sinks, rel_bias, w_attn_o, w_dw, b_dw, conv_ln_g, conv_ln_b, w_conv_out, w_out, norm_mlp_g, w_ff1, w_ff2, loss_target, m_norm_mix_g, m_w_in, m_q_norm_g, m_k_norm_g, m_attn_sinks, m_rel_bias, m_w_attn_o, m_w_dw, m_b_dw, m_conv_ln_g, m_conv_ln_b, m_w_conv_out, m_w_out, m_norm_mlp_g, m_w_ff1, m_w_ff2, v_norm_mix_g, v_w_in, v_q_norm_g, v_k_norm_g, v_attn_sinks, v_rel_bias, v_w_attn_o, v_w_dw, v_b_dw, v_conv_ln_g, v_conv_ln_b, v_w_conv_out, v_w_out, v_norm_mlp_g, v_w_ff1, v_w_ff2):
    args = dict(locals())
    wts = {nm: args[nm] for nm in WEIGHTS}
    mom = {nm: args["m_" + nm] for nm in WEIGHTS}
    var = {nm: args["v_" + nm] for nm in WEIGHTS}
    chip = 2 * lax.axis_index("x") + lax.axis_index("y")

    chip_arr = jnp.reshape(chip, (1,)).astype(jnp.int32)
    chip_core = jnp.stack([chip, lax.axis_index("c")]).astype(jnp.int32)
    placed = {nm: _place_shard(nm, wts[nm][0], chip_arr, bf16) for nm in BIG}
    placed["w_dw"] = _place_shard("w_dw", jnp.pad(w_dw[0], ((0, 1), (0, 0))), chip_arr, f32)

    loss_part, grad_x, g, shards = _forward_backward(x[0], loss_target[0], wts, placed, chip_core)

    small = jnp.concatenate([_pack_small(g, loss_part), g["w_dw"]], axis=0)
    small = _allreduce_small(small)
    grads, loss = _unpack_small(small)
    grads["w_dw"] = lax.dynamic_slice(small[W_DW_ROW:W_DW_ROW + CW], (0, chip * (D // 4)), (CW, D // 4))
    grads.update(shards)

    delta, new_m, new_v = {}, {}, {}
    sd, sm, sv = _adamw(_pack_small(wts), small[0:8], _pack_small(mom), _pack_small(var), "adamw_small")
    for res, blk in ((delta, sd), (new_m, sm), (new_v, sv)):
        res.update(_unpack_small(blk)[0])
    for nm in BIG + ["w_dw"]:
        shp = wts[nm].shape
        two_d = lambda a: a.reshape(shp[-2], shp[-1])
        delta[nm], new_m[nm], new_v[nm] = _adamw(two_d(wts[nm]), grads[nm], two_d(mom[nm]), two_d(var[nm]), "adamw_" + nm)

    def shaped(vals):
        return [vals[nm].reshape(wts[nm].shape) for nm in WEIGHTS]

    return (loss, grad_x[None], *shaped(grads), *shaped(delta), *shaped(new_m), *shaped(new_v))
```

```python
import functools

import numpy as np
import jax
import jax.numpy as jnp
from jax import lax
from jax.experimental import pallas as pl
from jax.experimental.pallas import tpu as pltpu

f32 = jnp.float32
bf16 = jnp.bfloat16
SDS = jax.ShapeDtypeStruct
MESH = pl.DeviceIdType.MESH

D = 1024
HD = 64
NQ = 16
NKV = 4
BLK = 128
CW = 31
HALO = 32
DFF = 4096
NBUCKET = 32
EPS = 1e-6
NEG = -1e30
INW = 5632
MIX_CHUNK = 256
C_Q, C_A, C_G, C_GA, C_GC = 0, 1, 2, 3, 4
C_KV = 10

ADAM_LR = 0.001
ADAM_B1 = 0.9
ADAM_B2 = 0.999
ADAM_EPS = 1e-08
ADAM_WD = 0.01
ADAM_STEP = 10

VMEM_BYTES_V7X = 64 << 20


def _cparams(sem, vmem_mb):
    assert (vmem_mb << 20) < VMEM_BYTES_V7X
    return pltpu.CompilerParams(dimension_semantics=sem, vmem_limit_bytes=vmem_mb << 20)


ANY = pl.BlockSpec(memory_space=pl.ANY)


class _Comm:
    def __init__(self, ins, out_shapes, n_sems, start, finish, mid=None, aliases=None):
        self.ins, self.out_shapes, self.n_sems = list(ins), list(out_shapes), n_sems
        self.start, self.finish, self.mid, self.aliases = start, finish, mid, dict(aliases or {})


def _call(body, args, *, grid, in_specs, out_specs, out_shape, name, sem, vmem_mb, scratch_shapes=(), comm=None,
          mid_step=None):
    n_in, n_out, n_scr = len(in_specs), len(out_specs), len(scratch_shapes)
    if comm is None:
        outs = pl.pallas_call(body, grid=grid, in_specs=list(in_specs), out_specs=list(out_specs),
                              out_shape=list(out_shape), scratch_shapes=list(scratch_shapes), name=name,
                              compiler_params=_cparams(sem, vmem_mb))(*args)
        return list(outs), []
    ci, co = len(comm.ins), len(comm.out_shapes)
    last = grid[0] - 1

    def wrapped(*refs):
        ins, cin = refs[:n_in], refs[n_in:n_in + ci]
        outs = refs[n_in + ci:n_in + ci + n_out]
        cout = refs[n_in + ci + n_out:n_in + ci + n_out + co]
        scr = refs[n_in + ci + n_out + co:]
        send, recv = scr[n_scr], scr[n_scr + 1]
        step = pl.program_id(0)

        @pl.when(step == 0)
        def _():
            comm.start(cin, cout, send, recv)

        body(*ins, *outs, *scr[:n_scr])
        if comm.mid is not None:
            @pl.when(step == mid_step)
            def _():
                comm.mid(cin, cout, send, recv)

        @pl.when(step == last)
        def _():
            comm.finish(cin, cout, send, recv)

    res = pl.pallas_call(
        wrapped, grid=grid, in_specs=list(in_specs) + [ANY] * ci, out_specs=list(out_specs) + [ANY] * co,
        out_shape=list(out_shape) + comm.out_shapes,
        input_output_aliases={n_in + k: n_out + v for k, v in comm.aliases.items()},
        scratch_shapes=list(scratch_shapes) + [pltpu.SemaphoreType.DMA((comm.n_sems,))] * 2,
        name=name, compiler_params=_cparams(("arbitrary",), vmem_mb))(*args, *comm.ins)
    return list(res[:n_out]), list(res[n_out:])


def _run_comm(comm, name):
    ci, co = len(comm.ins), len(comm.out_shapes)

    def body(*refs):
        cin, cout, (send, recv) = refs[:ci], refs[ci:ci + co], refs[ci + co:]
        comm.start(cin, cout, send, recv)
        if comm.mid is not None:
            comm.mid(cin, cout, send, recv)
        comm.finish(cin, cout, send, recv)

    return pl.pallas_call(
        body, in_specs=[ANY] * ci, out_specs=[ANY] * co, out_shape=comm.out_shapes, input_output_aliases=comm.aliases,
        scratch_shapes=[pltpu.SemaphoreType.DMA((comm.n_sems,))] * 2, name=name)(*comm.ins)


def _dot(a, b):
    return jnp.dot(a, b, preferred_element_type=f32)


def _dot_nt(a, b):
    return lax.dot_general(a, b, (((1,), (1,)), ((), ())), preferred_element_type=f32)


def _dot_tn(a, b):
    return lax.dot_general(a, b, (((0,), (0,)), ((), ())), preferred_element_type=f32)


def _sigmoid(x):
    return 1.0 / (1.0 + jnp.exp(-x))


def _low_head_lanes():
    return lax.broadcasted_iota(jnp.int32, (1, 2 * HD), 1) < HD


def _head_blockdiag():
    r = lax.broadcasted_iota(jnp.int32, (2 * HD, 2 * HD), 0) // HD
    c = lax.broadcasted_iota(jnp.int32, (2 * HD, 2 * HD), 1) // HD
    return jnp.where(r == c, 1.0, 0.0).astype(bf16)


def _head_sums(z, bd):
    hi = z.astype(bf16)
    lo = (z - hi.astype(f32)).astype(bf16)
    return _dot(hi, bd) + _dot(lo, bd)


def _rms_inproj(x, g, w):
    T, N = x.shape[0], w.shape[1]
    tm, tn = 512, 512

    def body(x_ref, g_ref, w_ref, p_ref, u_ref):
        xv = x_ref[...]
        r = lax.rsqrt(jnp.mean(xv * xv, axis=-1, keepdims=True) + EPS)
        u = (xv * r * g_ref[...]).astype(bf16)
        u_ref[...] = u
        for c in range(N // tn):
            p_ref[:, c * tn:(c + 1) * tn] = _dot(u, w_ref[:, c * tn:(c + 1) * tn])

    return pl.pallas_call(
        body, grid=(T // tm,),
        in_specs=[pl.BlockSpec((tm, D), lambda i: (i, 0)),
                  pl.BlockSpec((1, D), lambda i: (0, 0)),
                  pl.BlockSpec((D, N), lambda i: (0, 0), pipeline_mode=pl.Buffered(1))],
        out_specs=[pl.BlockSpec((tm, N), lambda i: (i, 0)),
                   pl.BlockSpec((tm, D), lambda i: (i, 0))],
        out_shape=[SDS((T, N), f32), SDS((T, D), bf16)],
        name="rms_inproj", compiler_params=_cparams(("parallel",), 48))(x, g, w)


def _split_pair(pair, out_ref, p, lo):
    rolled = pltpu.roll(pair, HD, 1)
    zero = jnp.zeros_like(pair)
    c = 512 * p
    out_ref[:, c:c + 128] = jnp.where(lo, pair, zero).astype(bf16)
    out_ref[:, c + 128:c + 256] = jnp.where(lo, zero, rolled).astype(bf16)
    out_ref[:, c + 256:c + 384] = jnp.where(lo, rolled, zero).astype(bf16)
    out_ref[:, c + 384:c + 512] = jnp.where(lo, zero, pair).astype(bf16)


def _qk_prep(proj, gq2, gk2):
    T = proj.shape[0]
    tm = 512

    def body(q_ref, kv_ref, gq_ref, gk_ref, qn_ref, kk_ref, vv_ref):
        bd = _head_blockdiag()
        lo = _low_head_lanes()
        for p in range(NQ // 2):
            z = q_ref[:, 128 * p:128 * p + 128]
            r = lax.rsqrt(_head_sums(z * z, bd) * (1.0 / HD) + EPS)
            qn_ref[:, 128 * p:128 * p + 128] = (z * r * gq_ref[...] * (HD ** -0.5)).astype(bf16)
        for p in range(NKV // 2):
            z = kv_ref[:, 128 * p:128 * p + 128]
            r = lax.rsqrt(_head_sums(z * z, bd) * (1.0 / HD) + EPS)
            _split_pair(z * r * gk_ref[...], kk_ref, p, lo)
            _split_pair(kv_ref[:, 256 + 128 * p:256 + 128 * p + 128], vv_ref, p, lo)

    return pl.pallas_call(
        body, grid=(T // tm,),
        in_specs=[pl.BlockSpec((tm, D), lambda i: (i, C_Q)),
                  pl.BlockSpec((tm, 512), lambda i: (i, C_KV)),
                  pl.BlockSpec((1, 128), lambda i: (0, 0)),
                  pl.BlockSpec((1, 128), lambda i: (0, 0))],
        out_specs=[pl.BlockSpec((tm, D), lambda i: (i, 0))] * 3,
        out_shape=[SDS((T, D), bf16)] * 3,
        name="qk_prep", compiler_params=_cparams(("parallel",), 32))(proj, proj, gq2, gk2)


def _bucket_tile():
    qi = np.arange(BLK)[:, None]
    kj = np.arange(BLK)[None, :]
    n = np.where(kj > qi, qi + BLK - kj, qi - kj)
    max_exact = NBUCKET // 2
    nf = np.maximum(n, 1).astype(np.float32)
    large = max_exact + (np.log(nf / max_exact) / np.float32(np.log(128 / max_exact))
                         * (NBUCKET - max_exact)).astype(np.int32)
    large = np.minimum(large, NBUCKET - 1)
    return np.where(n < max_exact, n, large).astype(np.int32)


def _from_prev_block():
    return lax.broadcasted_iota(jnp.int32, (BLK, BLK), 1) > lax.broadcasted_iota(jnp.int32, (BLK, BLK), 0)


def _bias_tiles(rel_bias):
    def body(rb_ref, bk_ref, out_ref):
        bk = bk_ref[...]
        for h in range(NQ):
            acc = jnp.zeros((BLK, BLK), f32)
            for b in range(NBUCKET):
                acc = jnp.where(bk == b, rb_ref[b, h], acc)
            out_ref[h] = acc

    return pl.pallas_call(
        body,
        in_specs=[pl.BlockSpec(memory_space=pltpu.SMEM), pl.BlockSpec(memory_space=pltpu.VMEM)],
        out_specs=pl.BlockSpec(memory_space=pltpu.VMEM),
        out_shape=SDS((NQ, BLK, BLK), f32),
        name="bias_tiles")(rel_bias, jnp.asarray(_bucket_tile()))


def _rows2(ref, c):
    return jnp.concatenate([ref[:, c:c + 128], ref[:, c + 128:c + 256]], axis=0)


def _attn_fwd(qn, kk, vv, bias, sinks, comm=None):
    T = qn.shape[0]
    nb = T // BLK

    def body(s_ref, q_ref, kc_ref, kp_ref, vc_ref, vp_ref, b_ref, o_ref, lse_ref):
        prev = _from_prev_block()
        no_key = jnp.logical_and(prev, pl.program_id(0) == 0)
        scores = []
        for h in range(NKV):
            qs = _rows2(q_ref, 256 * h)
            scores.append((_dot_nt(qs, _rows2(kc_ref, 256 * h)), _dot_nt(qs, _rows2(kp_ref, 256 * h))))
        for h in range(NKV):
            c = 256 * h
            sc, sp = scores[h]
            vstack = jnp.concatenate([vp_ref[:, c:c + 128], vc_ref[:, c:c + 128],
                                      vp_ref[:, c + 128:c + 256], vc_ref[:, c + 128:c + 256]], axis=0)
            for pr in range(2):
                ps = []
                for e in range(2):
                    hq = 4 * h + 2 * pr + e
                    rows, cols = slice(128 * pr, 128 * pr + 128), slice(128 * e, 128 * e + 128)
                    s = jnp.where(no_key, NEG, jnp.where(prev, sp[rows, cols], sc[rows, cols]) + b_ref[hq])
                    sink = s_ref[0, hq]
                    m = jnp.maximum(jnp.max(s, axis=-1, keepdims=True), sink)
                    ex = jnp.exp(s - m)
                    l = jnp.sum(ex, axis=-1, keepdims=True) + jnp.exp(sink - m)
                    p = ex * (1.0 / l)
                    ps += [jnp.where(prev, p, 0.0).astype(bf16), jnp.where(prev, 0.0, p).astype(bf16)]
                    lse_ref[:, hq:hq + 1] = m + jnp.log(l)
                o_ref[:, c + 128 * pr:c + 128 * pr + 128] = _dot(jnp.concatenate(ps, axis=1), vstack).astype(bf16)

    blk = lambda f: pl.BlockSpec((BLK, D), f)
    cur = lambda n: (n, 0)
    prev = lambda n: (jnp.maximum(n - 1, 0), 0)
    return _call(
        body, (sinks, qn, kk, kk, vv, vv, bias), grid=(nb,),
        in_specs=[pl.BlockSpec(memory_space=pltpu.SMEM), blk(cur), blk(cur), blk(prev), blk(cur), blk(prev),
                  pl.BlockSpec((NQ, BLK, BLK), lambda n: (0, 0, 0))],
        out_specs=[blk(cur), pl.BlockSpec((BLK, NQ), cur)],
        out_shape=[SDS((T, D), bf16), SDS((T, NQ), f32)],
        name="attn_fwd", sem=("parallel",), vmem_mb=32, comm=comm, mid_step=(3 * nb) // 4)


SUB = 8


def _shifted_copies(sh_ref, rows):
    for b in range(1, SUB):
        sh_ref[b, 0:rows, :] = sh_ref[0, pl.ds(b, rows), :]


def _taps_by_shift(sh_ref, r0, rows, offset_of_tap, ln):
    groups = {}
    for j in range(CW):
        a, b = divmod(offset_of_tap(j), SUB)
        groups.setdefault(b, []).append((a, j))
    for b, taps in groups.items():
        a0 = min(a for a, _ in taps)
        a1 = max(a for a, _ in taps)
        win = sh_ref[b, pl.ds(pl.multiple_of(r0 + SUB * a0, SUB), rows + SUB * (a1 - a0)), ln]
        for a, j in taps:
            yield j, win[SUB * (a - a0):SUB * (a - a0) + rows]


CONV_LANES = 128


def _lane_groups():
    return [slice(q * CONV_LANES, (q + 1) * CONV_LANES) for q in range(D // CONV_LANES)]


def _conv_taps(sh_ref, w_ref, r0, rows, offset_of_tap, init):
    out = []
    for ln in _lane_groups():
        acc = init(ln)
        for j, x in _taps_by_shift(sh_ref, r0, rows, offset_of_tap, ln):
            acc = acc + x * w_ref[j:j + 1, ln]
        out.append((ln, acc))
    return out


def _glu_conv_fwd(proj, w_dw, b_dw, ln_g, ln_b):
    T = proj.shape[0]
    tt, ch = 256, 32

    def body(a_ref, g_ref, ah_ref, gh_ref, w_ref, b_ref, lg_ref, lb_ref, h1_ref, h3_ref, sh):
        i = pl.program_id(0)
        halo = ah_ref[...] * _sigmoid(gh_ref[...])
        sh[0, 0:HALO, :] = jnp.where(i > 0, halo, 0.0)
        sh[0, HALO:HALO + tt, :] = a_ref[...] * _sigmoid(g_ref[...])
        _shifted_copies(sh, tt + HALO - SUB)

        def chunk(c, carry):
            r0 = pl.multiple_of(c * ch, ch)
            for ln, acc in _conv_taps(sh, w_ref, r0, ch, lambda j: HALO - (CW - 1) + j,
                                      lambda ln: jnp.broadcast_to(b_ref[:, ln], (ch, CONV_LANES))):
                h1_ref[pl.ds(r0, ch), ln] = acc
            return carry

        lax.fori_loop(0, tt // ch, chunk, 0)
        h1 = h1_ref[...]
        mu = jnp.mean(h1, axis=-1, keepdims=True)
        xc = h1 - mu
        var = jnp.mean(xc * xc, axis=-1, keepdims=True)
        h2 = xc * lax.rsqrt(var + EPS) * lg_ref[...] + lb_ref[...]
        h3_ref[...] = (h2 * _sigmoid(h2)).astype(bf16)

    hpt = tt // HALO
    tile = lambda cb: pl.BlockSpec((tt, D), lambda i: (i, cb))
    halo = lambda cb: pl.BlockSpec((HALO, D), lambda i: (jnp.maximum(i * hpt - 1, 0), cb))
    vec = pl.BlockSpec((1, D), lambda i: (0, 0))
    return pl.pallas_call(
        body, grid=(T // tt,),
        in_specs=[tile(C_A), tile(C_G), halo(C_A), halo(C_G), pl.BlockSpec((HALO, D), lambda i: (0, 0)), vec, vec, vec],
        out_specs=[pl.BlockSpec((tt, D), lambda i: (i, 0))] * 2,
        out_shape=[SDS((T, D), f32), SDS((T, D), bf16)],
        scratch_shapes=[pltpu.VMEM((SUB, HALO + tt, D), f32)],
        name="glu_conv_fwd", compiler_params=_cparams(("parallel",), 32))(proj, proj, proj, proj, w_dw, b_dw, ln_g, ln_b)


def _mix_out(o, h3, proj, x, w_attn_o, w_conv_out, w_out, g_mlp):
    T = x.shape[0]
    tm = 512

    def body(o_ref, h3_ref, ga_ref, gc_ref, x_ref, wa_ref, wc_ref, wo_ref, g_ref,
             attn_ref, conv_ref, mg_ref, x1_ref, n2_ref):
        x1 = x_ref[...]
        for j in range(D // MIX_CHUNK):
            cols = slice(j * MIX_CHUNK, (j + 1) * MIX_CHUNK)
            attn = _dot(o_ref[...], wa_ref[:, cols])
            conv = _dot(h3_ref[...], wc_ref[:, cols])
            attn_ref[:, cols] = attn.astype(bf16)
            conv_ref[:, cols] = conv.astype(bf16)
            mg = (_sigmoid(ga_ref[:, cols]) * attn + _sigmoid(gc_ref[:, cols]) * conv).astype(bf16)
            mg_ref[:, cols] = mg
            x1 = x1 + _dot(mg, wo_ref[cols, :])
        x1_ref[...] = x1
        r = lax.rsqrt(jnp.mean(x1 * x1, axis=-1, keepdims=True) + EPS)
        n2_ref[...] = (x1 * r * g_ref[...]).astype(bf16)

    tile = lambda cb=0: pl.BlockSpec((tm, D), lambda i: (i, cb))
    wfull = pl.BlockSpec((D, D), lambda i: (0, 0), pipeline_mode=pl.Buffered(1))
    return pl.pallas_call(
        body, grid=(T // tm,),
        in_specs=[tile(), tile(), tile(C_GA), tile(C_GC), tile(), wfull, wfull, wfull,
                  pl.BlockSpec((1, D), lambda i: (0, 0))],
        out_specs=[tile()] * 5,
        out_shape=[SDS((T, D), bf16), SDS((T, D), bf16), SDS((T, D), bf16), SDS((T, D), f32), SDS((T, D), bf16)],
        name="mix_out", compiler_params=_cparams(("parallel",), 48))(o, h3, proj, proj, x, w_attn_o, w_conv_out, w_out, g_mlp)


def _mlp_fwd(n2, w1, w2, x1, tgt):
    T = n2.shape[0]
    tm, tf = 512, 1024

    def body(n2_ref, w1_ref, w2_ref, x1_ref, t_ref, hm_ref, dy_ref, dyb_ref, loss_ref):
        @pl.when(pl.program_id(0) == 0)
        def _():
            loss_ref[...] = jnp.zeros_like(loss_ref)

        n2v = n2_ref[...]
        for c in range(DFF // tf):
            r = jnp.maximum(_dot(n2v, w1_ref[:, c * tf:(c + 1) * tf]), 0.0)
            hm_ref[:, c * tf:(c + 1) * tf] = (r * r).astype(bf16)
        e = x1_ref[...] + _dot(hm_ref[...], w2_ref[...]) - t_ref[...]
        dy = e * (1.0 / D)
        dy_ref[...] = dy
        dyb_ref[...] = dy.astype(bf16)
        loss_ref[...] += 0.5 * jnp.sum(jnp.sum(e * e, axis=-1, keepdims=True) * (1.0 / D))

    row = pl.BlockSpec((tm, D), lambda i: (i, 0))
    once = pl.Buffered(1)
    return pl.pallas_call(
        body, grid=(T // tm,),
        in_specs=[row, pl.BlockSpec((D, DFF), lambda i: (0, 0), pipeline_mode=once),
                  pl.BlockSpec((DFF, D), lambda i: (0, 0), pipeline_mode=once), row, row],
        out_specs=[pl.BlockSpec((tm, DFF), lambda i: (i, 0)), row, row, pl.BlockSpec((8, 128), lambda i: (0, 0))],
        out_shape=[SDS((T, DFF), bf16), SDS((T, D), f32), SDS((T, D), bf16), SDS((8, 128), f32)],
        name="mlp_fwd", compiler_params=_cparams(("arbitrary",), 56))(n2, w1, w2, x1, tgt)


def _rms_bwd(xv, g, dn, dres):
    r = lax.rsqrt(jnp.mean(xv * xv, axis=-1, keepdims=True) + EPS)
    gd = dn * g
    dx = dres + r * gd - xv * (r * r * r) * jnp.mean(xv * gd, axis=-1, keepdims=True)
    dg = jnp.sum(dn * xv * r, axis=0, keepdims=True)
    return dx, dg


def _mlp_bwd(dy, dyb, hmid, w1, w2, x1, g_mlp):
    T = dy.shape[0]
    tm, tf = 512, 1024

    def body(dy_ref, dyb_ref, hm_ref, w1_ref, w2_ref, x1_ref, g_ref, df_ref, dx_ref, dxb_ref, dg_ref):
        @pl.when(pl.program_id(0) == 0)
        def _():
            dg_ref[...] = jnp.zeros_like(dg_ref)

        dyb = dyb_ref[...]
        for c in range(DFF // tf):
            cols = slice(c * tf, (c + 1) * tf)
            d_hm = _dot_nt(dyb, w2_ref[cols, :])
            df_ref[:, cols] = (d_hm * (2.0 * jnp.sqrt(hm_ref[:, cols].astype(f32)))).astype(bf16)
        dn = _dot_nt(df_ref[...], w1_ref[...])
        dx, dg = _rms_bwd(x1_ref[...], g_ref[...], dn, dy_ref[...])
        dx_ref[...] = dx
        dxb_ref[...] = dx.astype(bf16)
        dg_ref[...] += dg

    row = pl.BlockSpec((tm, D), lambda i: (i, 0))
    wide = pl.BlockSpec((tm, DFF), lambda i: (i, 0))
    vec = pl.BlockSpec((1, D), lambda i: (0, 0))
    once = pl.Buffered(1)
    return pl.pallas_call(
        body, grid=(T // tm,),
        in_specs=[row, row, wide, pl.BlockSpec((D, DFF), lambda i: (0, 0), pipeline_mode=once),
                  pl.BlockSpec((DFF, D), lambda i: (0, 0), pipeline_mode=once), row, vec],
        out_specs=[wide, row, row, vec],
        out_shape=[SDS((T, DFF), bf16), SDS((T, D), f32), SDS((T, D), bf16), SDS((1, D), f32)],
        name="mlp_bwd", compiler_params=_cparams(("arbitrary",), 56))(dy, dyb, hmid, w1, w2, x1, g_mlp)


def _wgrad(a, b, name, tn=1024):
    T, M = a.shape
    N = b.shape[1]
    tmm, tk = min(M, 1024), min(T, 2048)

    def body(a_ref, b_ref, o_ref):
        @pl.when(pl.program_id(2) == 0)
        def _():
            o_ref[...] = jnp.zeros_like(o_ref)

        o_ref[...] += _dot_tn(a_ref[...], b_ref[...])

    return pl.pallas_call(
        body, grid=(M // tmm, N // tn, T // tk),
        in_specs=[pl.BlockSpec((tk, tmm), lambda m, n, t: (t, m)), pl.BlockSpec((tk, tn), lambda m, n, t: (t, n))],
        out_specs=pl.BlockSpec((tmm, tn), lambda m, n, t: (m, n)),
        out_shape=SDS((M, N), f32),
        name=name, compiler_params=_cparams(("parallel", "parallel", "arbitrary"), 40))(a, b)


def _mix_bwd(dx1b, proj, attn, conv, w_attn_o, w_conv_out, w_out, comm=None):
    T = dx1b.shape[0]
    tm = 512

    def body(dx_ref, ga_ref, gc_ref, attn_ref, conv_ref, wa_ref, wc_ref, wo_ref,
             dat_ref, dcv_ref, do_ref, dh3_ref, dga_ref, dgc_ref):
        d_o, d_h3 = None, None
        for j in range(D // MIX_CHUNK):
            cols = slice(j * MIX_CHUNK, (j + 1) * MIX_CHUNK)
            dm = _dot_nt(dx_ref[...], wo_ref[cols, :])
            sa = _sigmoid(ga_ref[:, cols])
            sc = _sigmoid(gc_ref[:, cols])
            dat = (dm * sa).astype(bf16)
            dcv = (dm * sc).astype(bf16)
            dat_ref[:, cols] = dat
            dcv_ref[:, cols] = dcv
            dga_ref[:, cols] = (dm * attn_ref[:, cols].astype(f32) * sa * (1.0 - sa)).astype(bf16)
            dgc_ref[:, cols] = (dm * conv_ref[:, cols].astype(f32) * sc * (1.0 - sc)).astype(bf16)
            part_o = _dot_nt(dat, wa_ref[:, cols])
            part_h = _dot_nt(dcv, wc_ref[:, cols])
            d_o = part_o if d_o is None else d_o + part_o
            d_h3 = part_h if d_h3 is None else d_h3 + part_h
        do_ref[...] = d_o.astype(bf16)
        dh3_ref[...] = d_h3

    tile = lambda cb=0: pl.BlockSpec((tm, D), lambda i: (i, cb))
    wfull = pl.BlockSpec((D, D), lambda i: (0, 0), pipeline_mode=pl.Buffered(1))
    return _call(
        body, (dx1b, proj, proj, attn, conv, w_attn_o, w_conv_out, w_out), grid=(T // tm,),
        in_specs=[tile(), tile(C_GA), tile(C_GC), tile(), tile(), wfull, wfull, wfull],
        out_specs=[tile()] * 6,
        out_shape=[SDS((T, D), bf16), SDS((T, D), bf16), SDS((T, D), bf16), SDS((T, D), f32),
                   SDS((T, D), bf16), SDS((T, D), bf16)],
        name="mix_bwd", sem=("parallel",), vmem_mb=48, comm=comm)


def _conv_ln_bwd(dh3, h1, ln_g, ln_b, comm=None):
    T = dh3.shape[0]
    tt = 256

    def body(d_ref, h1_ref, lg_ref, lb_ref, dh1_ref, acc_ref):
        @pl.when(pl.program_id(0) == 0)
        def _():
            acc_ref[...] = jnp.zeros_like(acc_ref)

        h1 = h1_ref[...]
        mu = jnp.mean(h1, axis=-1, keepdims=True)
        xc = h1 - mu
        rstd = lax.rsqrt(jnp.mean(xc * xc, axis=-1, keepdims=True) + EPS)
        xh = xc * rstd
        h2 = xh * lg_ref[...] + lb_ref[...]
        sg = _sigmoid(h2)
        dh2 = d_ref[...] * (sg * (1.0 + h2 * (1.0 - sg)))
        dxh = dh2 * lg_ref[...]
        dh1 = rstd * (dxh - jnp.mean(dxh, axis=-1, keepdims=True) - xh * jnp.mean(dxh * xh, axis=-1, keepdims=True))
        dh1_ref[...] = dh1
        acc_ref[0:1, :] += jnp.sum(dh2 * xh, axis=0, keepdims=True)
        acc_ref[1:2, :] += jnp.sum(dh2, axis=0, keepdims=True)
        acc_ref[2:3, :] += jnp.sum(dh1, axis=0, keepdims=True)

    tile = pl.BlockSpec((tt, D), lambda i: (i, 0))
    vec = pl.BlockSpec((1, D), lambda i: (0, 0))
    return _call(
        body, (dh3, h1, ln_g, ln_b), grid=(T // tt,),
        in_specs=[tile, tile, vec, vec],
        out_specs=[tile, pl.BlockSpec((8, D), lambda i: (0, 0))],
        out_shape=[SDS((T, D), f32), SDS((8, D), f32)],
        name="conv_ln_bwd", sem=("arbitrary",), vmem_mb=32, comm=comm)


def _conv_bwd(dh1, proj, w_dw, comm=None):
    T = dh1.shape[0]
    tt, ch = 256, 32
    nt = T // tt

    def body(d_ref, dn_ref, a_ref, g_ref, ah_ref, gh_ref, w_ref, da_ref, dg_ref, gw_ref, dsh, hsh, dh0, gacc):
        i = pl.program_id(0)

        @pl.when(i == 0)
        def _():
            gacc[...] = jnp.zeros_like(gacc)

        dsh[0, 0:tt, :] = d_ref[...]
        dsh[0, tt:tt + HALO, :] = jnp.where(i < nt - 1, dn_ref[...], 0.0)
        hsh[0, 0:HALO, :] = jnp.where(i > 0, ah_ref[...] * _sigmoid(gh_ref[...]), 0.0)
        hsh[0, HALO:HALO + tt, :] = a_ref[...] * _sigmoid(g_ref[...])
        _shifted_copies(dsh, tt + HALO - SUB)
        _shifted_copies(hsh, tt + HALO - SUB)

        def chunk(c, carry):
            r0 = pl.multiple_of(c * ch, ch)
            for ln, acc in _conv_taps(dsh, w_ref, r0, ch, lambda j: (CW - 1) - j,
                                      lambda ln: jnp.zeros((ch, CONV_LANES), f32)):
                dh0[pl.ds(r0, ch), ln] = acc
            for ln in _lane_groups():
                dv = dsh[0, pl.ds(r0, ch), ln]
                for j, hv in _taps_by_shift(hsh, r0, ch, lambda j: HALO - (CW - 1) + j, ln):
                    pr = dv * hv
                    gacc[8 * j:8 * j + 8, ln] += pr[0:8] + pr[8:16] + pr[16:24] + pr[24:32]
            return carry

        lax.fori_loop(0, tt // ch, chunk, 0)
        a = a_ref[...]
        sg = _sigmoid(g_ref[...])
        d0 = dh0[...]
        da_ref[...] = (d0 * sg).astype(bf16)
        dg_ref[...] = (d0 * a * sg * (1.0 - sg)).astype(bf16)

        @pl.when(i == nt - 1)
        def _():
            gw_ref[...] = jnp.zeros_like(gw_ref)
            for j in range(CW):
                gw_ref[j:j + 1, :] = jnp.sum(gacc[8 * j:8 * j + 8, :], axis=0, keepdims=True)

    hpt = tt // HALO
    tile = lambda cb=0: pl.BlockSpec((tt, D), lambda i: (i, cb))
    halo_prev = lambda cb: pl.BlockSpec((HALO, D), lambda i: (jnp.maximum(i * hpt - 1, 0), cb))
    halo_next = pl.BlockSpec((HALO, D), lambda i: (jnp.minimum((i + 1) * hpt, T // HALO - 1), 0))
    wspec = pl.BlockSpec((HALO, D), lambda i: (0, 0))
    return _call(
        body, (dh1, dh1, proj, proj, proj, proj, w_dw), grid=(nt,),
        in_specs=[tile(), halo_next, tile(C_A), tile(C_G), halo_prev(C_A), halo_prev(C_G), wspec],
        out_specs=[tile(), tile(), wspec],
        out_shape=[SDS((T, D), bf16), SDS((T, D), bf16), SDS((HALO, D), f32)],
        scratch_shapes=[pltpu.VMEM((SUB, tt + HALO, D), f32), pltpu.VMEM((SUB, HALO + tt, D), f32),
                        pltpu.VMEM((tt, D), f32), pltpu.VMEM((8 * HALO, D), f32)],
        name="conv_bwd", sem=("arbitrary",), vmem_mb=48, comm=comm)


def _attn_bwd(qn, kk, vv, bias, sinks, o, do, lse, comm=None):
    T = qn.shape[0]
    nb = T // BLK

    def body(s_ref, q_ref, kc_ref, kp_ref, vc_ref, vp_ref, b_ref, o_ref, do_ref, lse_ref,
             dq_ref, dkc_ref, dkp_ref, dvc_ref, dvp_ref, dsk_ref, dsa_ref):
        n = pl.program_id(0)

        @pl.when(n == 0)
        def _():
            dsk_ref[...] = jnp.zeros_like(dsk_ref)
            dsa_ref[...] = jnp.zeros_like(dsa_ref)

        @pl.when(n == nb)
        def _():
            dkp_ref[...] = jnp.zeros_like(dkp_ref)
            dvp_ref[...] = jnp.zeros_like(dvp_ref)

        @pl.when(n < nb)
        def _():
            from_prev = _from_prev_block()
            no_key = jnp.logical_and(from_prev, n == 0)
            lo = _low_head_lanes()
            dups = {"kc": [], "kp": [], "vc": [], "vp": []}
            products = []
            for h in range(NKV):
                qs = _rows2(q_ref, 256 * h)
                dos = _rows2(do_ref, 256 * h)
                products.append((qs, dos, _dot_nt(qs, _rows2(kc_ref, 256 * h)), _dot_nt(qs, _rows2(kp_ref, 256 * h)),
                                 _dot_nt(dos, _rows2(vc_ref, 256 * h)), _dot_nt(dos, _rows2(vp_ref, 256 * h))))
            for h in range(NKV):
                c = 256 * h
                qs, dos, sc, sp, dpc, dpp = products[h]
                kstack = jnp.concatenate([kp_ref[:, c:c + 128], kc_ref[:, c:c + 128],
                                          kp_ref[:, c + 128:c + 256], kc_ref[:, c + 128:c + 256]], axis=0)
                p_c, p_p, ds_c, ds_p = [], [], [], []
                for pr in range(2):
                    cc = c + 128 * pr
                    prod = do_ref[:, cc:cc + 128].astype(f32) * o_ref[:, cc:cc + 128].astype(f32)
                    d_lo = jnp.sum(jnp.where(lo, prod, 0.0), axis=-1, keepdims=True)
                    d_hi = jnp.sum(prod, axis=-1, keepdims=True) - d_lo
                    row_pc, row_pp, row_dc, row_dp = [], [], [], []
                    for e in range(2):
                        hq = 4 * h + 2 * pr + e
                        rows, cols = slice(128 * pr, 128 * pr + 128), slice(128 * e, 128 * e + 128)
                        delta = d_lo if e == 0 else d_hi
                        lse = lse_ref[:, hq:hq + 1]
                        s = jnp.where(from_prev, sp[rows, cols], sc[rows, cols]) + b_ref[hq]
                        p = jnp.where(no_key, 0.0, jnp.exp(s - lse))
                        ds = p * (jnp.where(from_prev, dpp[rows, cols], dpc[rows, cols]) - delta)
                        dsa_ref[hq] += ds
                        dsk_ref[hq] += jnp.broadcast_to(-jnp.sum(jnp.exp(s_ref[0, hq] - lse) * delta), (8, 128))
                        row_pc.append(jnp.where(from_prev, 0.0, p).astype(bf16))
                        row_pp.append(jnp.where(from_prev, p, 0.0).astype(bf16))
                        row_dc.append(jnp.where(from_prev, 0.0, ds).astype(bf16))
                        row_dp.append(jnp.where(from_prev, ds, 0.0).astype(bf16))
                    dq_ref[:, cc:cc + 128] = _dot(jnp.concatenate([row_dp[0], row_dc[0], row_dp[1], row_dc[1]], axis=1), kstack)
                    p_c.append(jnp.concatenate(row_pc, axis=1))
                    p_p.append(jnp.concatenate(row_pp, axis=1))
                    ds_c.append(jnp.concatenate(row_dc, axis=1))
                    ds_p.append(jnp.concatenate(row_dp, axis=1))

                def to_keys(m2, rhs):
                    x2 = _dot_tn(jnp.concatenate(m2, axis=0), rhs)
                    x = jnp.where(lo, x2[0:128], x2[128:256])
                    return x + pltpu.roll(x, HD, 1)

                dups["kc"].append(to_keys(ds_c, qs))
                dups["kp"].append(to_keys(ds_p, qs))
                dups["vc"].append(to_keys(p_c, dos))
                dups["vp"].append(to_keys(p_p, dos))
            for key, ref in (("kc", dkc_ref), ("kp", dkp_ref), ("vc", dvc_ref), ("vp", dvp_ref)):
                d = dups[key]
                ref[:, 0:128] = jnp.where(lo, d[0], d[1])
                ref[:, 128:256] = jnp.where(lo, d[2], d[3])

    clamp = lambda n: jnp.minimum(n, nb - 1)
    blk = lambda f: pl.BlockSpec((BLK, D), f)
    cur = lambda n: (clamp(n), 0)
    prev = lambda n: (jnp.maximum(clamp(n) - 1, 0), 0)
    back = lambda n: (jnp.maximum(n - 1, 0), 0)
    kvb = lambda f: pl.BlockSpec((BLK, NKV * HD), f)
    return _call(
        body, (sinks, qn, kk, kk, vv, vv, bias, o, do, lse), grid=(nb + 1,),
        in_specs=[pl.BlockSpec(memory_space=pltpu.SMEM), blk(cur), blk(cur), blk(prev), blk(cur), blk(prev),
                  pl.BlockSpec((NQ, BLK, BLK), lambda n: (0, 0, 0)), blk(cur), blk(cur),
                  pl.BlockSpec((BLK, NQ), cur)],
        out_specs=[blk(cur), kvb(cur), kvb(back), kvb(cur), kvb(back),
                   pl.BlockSpec((NQ, 8, 128), lambda n: (0, 0, 0)),
                   pl.BlockSpec((NQ, BLK, BLK), lambda n: (0, 0, 0))],
        out_shape=[SDS((T, D), f32)] + [SDS((T, NKV * HD), f32)] * 4 + [SDS((NQ, 8, 128), f32), SDS((NQ, BLK, BLK), f32)],
        name="attn_bwd", sem=("arbitrary",), vmem_mb=40, comm=comm)


def _bias_bwd(dsa):
    def body(bk_ref, ds_ref, out_ref):
        bk = bk_ref[...]
        lane = lax.broadcasted_iota(jnp.int32, (1, 128), 1)
        for h in range(NQ):
            ds = ds_ref[h]
            row = jnp.zeros((1, 128), f32)
            for b in range(NBUCKET):
                row = jnp.where(lane == b, jnp.sum(jnp.where(bk == b, ds, 0.0)), row)
            out_ref[h:h + 1, :] = row

    return pl.pallas_call(body, out_shape=SDS((NQ, 128), f32), name="bias_bwd")(jnp.asarray(_bucket_tile()), dsa)


def _qkv_bwd(proj, gq2, gk2, dqn, dkc, dkp, dvc, dvp):
    T = proj.shape[0]
    tm = 512

    def body(q_ref, kv_ref, gq_ref, gk_ref, dq_ref, dkc_ref, dkp_ref, dvc_ref, dvp_ref,
             oq_ref, okv_ref, ggq_ref, ggk_ref):
        @pl.when(pl.program_id(0) == 0)
        def _():
            ggq_ref[...] = jnp.zeros_like(ggq_ref)
            ggk_ref[...] = jnp.zeros_like(ggk_ref)

        bd = _head_blockdiag()

        def norm_bwd(z, dy, g, scale):
            r = lax.rsqrt(_head_sums(z * z, bd) * (1.0 / HD) + EPS)
            gd = dy * g * scale
            dz = r * gd - z * (r * r * r) * _head_sums(z * gd, bd) * (1.0 / HD)
            return dz, jnp.sum(dy * scale * z * r, axis=0, keepdims=True)

        gq = jnp.zeros((1, 128), f32)
        for p in range(NQ // 2):
            ln = slice(128 * p, 128 * p + 128)
            dz, dg = norm_bwd(q_ref[:, ln], dq_ref[:, ln], gq_ref[...], HD ** -0.5)
            oq_ref[:, ln] = dz.astype(bf16)
            gq = gq + dg
        ggq_ref[...] += gq + pltpu.roll(gq, HD, 1)
        gk = jnp.zeros((1, 128), f32)
        for p in range(NKV // 2):
            ln = slice(128 * p, 128 * p + 128)
            dz, dg = norm_bwd(kv_ref[:, ln], dkc_ref[:, ln] + dkp_ref[:, ln], gk_ref[...], 1.0)
            okv_ref[:, ln] = dz.astype(bf16)
            gk = gk + dg
        ggk_ref[...] += gk + pltpu.roll(gk, HD, 1)
        okv_ref[:, 256:512] = (dvc_ref[...] + dvp_ref[...]).astype(bf16)

    vec = pl.BlockSpec((1, 128), lambda i: (0, 0))
    kvb = pl.BlockSpec((tm, NKV * HD), lambda i: (i, 0))
    return pl.pallas_call(
        body, grid=(T // tm,),
        in_specs=[pl.BlockSpec((tm, D), lambda i: (i, C_Q)), pl.BlockSpec((tm, 512), lambda i: (i, C_KV)), vec, vec,
                  pl.BlockSpec((tm, D), lambda i: (i, 0)), kvb, kvb, kvb, kvb],
        out_specs=[pl.BlockSpec((tm, D), lambda i: (i, 0)), pl.BlockSpec((tm, 512), lambda i: (i, 0)), vec, vec],
        out_shape=[SDS((T, D), bf16), SDS((T, 512), bf16), SDS((1, 128), f32), SDS((1, 128), f32)],
        name="qkv_bwd", compiler_params=_cparams(("arbitrary",), 32))(proj, proj, gq2, gk2, dqn, dkc, dkp, dvc, dvp)


def _inproj_bwd(pieces, w_in, x, dx1, g_mix, comm=None):
    T = x.shape[0]
    tm = 512
    widths = [p.shape[1] for p in pieces]
    offs = [sum(widths[:i]) for i in range(len(widths))]
    assert sum(widths) == INW

    def body(*refs):
        p_refs, (w_ref, x_ref, dx1_ref, g_ref, dx_ref, dg_ref) = refs[:len(pieces)], refs[len(pieces):]

        @pl.when(pl.program_id(0) == 0)
        def _():
            dg_ref[...] = jnp.zeros_like(dg_ref)

        du = None
        for p_ref, off, wd in zip(p_refs, offs, widths):
            part = _dot_nt(p_ref[...], w_ref[:, off:off + wd])
            du = part if du is None else du + part
        dx, dg = _rms_bwd(x_ref[...], g_ref[...], du, dx1_ref[...])
        dx_ref[...] = dx
        dg_ref[...] += dg

    row = pl.BlockSpec((tm, D), lambda i: (i, 0))
    vec = pl.BlockSpec((1, D), lambda i: (0, 0))
    return _call(
        body, (*pieces, w_in, x, dx1, g_mix), grid=(T // tm,),
        in_specs=[pl.BlockSpec((tm, wd), lambda i: (i, 0)) for wd in widths]
        + [pl.BlockSpec((D, INW), lambda i: (0, 0), pipeline_mode=pl.Buffered(1)), row, row, vec],
        out_specs=[row, vec],
        out_shape=[SDS((T, D), f32), SDS((1, D), f32)],
        name="inproj_bwd", sem=("arbitrary",), vmem_mb=48, comm=comm)


def _to_internal_cols(w):
    return jnp.concatenate([w[..., 0:1024], w[..., 1536:INW], w[..., 1024:1536]], axis=-1)


def _forward_backward(x, tgt, w, placed, chip_core):
    def sums(names, grads, got):
        res = [_pair_sum(nm, grads[nm], got_nm, chip_core) for nm, got_nm in zip(names, got)]
        return {nm: r[0] for nm, r in zip(names, res)}, {nm: r[1] for nm, r in zip(names, res)}

    first = ["w_in", "w_dw"]
    w_in, w_dw = _run_comm(_gather_comm({nm: placed[nm] for nm in first}), "gather_first")
    w_in = _to_internal_cols(w_in)
    gq2 = jnp.tile(w["q_norm_g"], (1, 2))
    gk2 = jnp.tile(w["k_norm_g"], (1, 2))
    proj, u = _rms_inproj(x, w["norm_mix_g"], w_in)
    qn, kk, vv = _qk_prep(proj, gq2, gk2)
    bias = _bias_tiles(w["rel_bias"])
    rest = [nm for nm in BIG if nm != "w_in"]
    (o, lse), full = _attn_fwd(qn, kk, vv, bias, w["attn_sinks"], comm=_gather_comm({nm: placed[nm] for nm in rest}))
    full = dict(zip(rest, full))
    h1, h3 = _glu_conv_fwd(proj, w_dw, w["b_dw"], w["conv_ln_g"], w["conv_ln_b"])
    attn, conv, merged, x1, n2 = _mix_out(o, h3, proj, x, full["w_attn_o"], full["w_conv_out"], full["w_out"],
                                          w["norm_mlp_g"])
    hmid, dy, dyb, loss = _mlp_fwd(n2, full["w_ff1"], full["w_ff2"], x1, tgt)

    g, cp, own = {}, {}, {}
    df1, dx1, dx1b, g["norm_mlp_g"] = _mlp_bwd(dy, dyb, hmid, full["w_ff1"], full["w_ff2"], x1, w["norm_mlp_g"])
    ff = ["w_ff1", "w_ff2"]
    gff = {"w_ff2": _wgrad(hmid, dyb, "wgrad_ff2"), "w_ff1": _wgrad(n2, df1, "wgrad_ff1")}
    (dat, dcv, do, dh3, dga, dgc), got = _mix_bwd(dx1b, proj, attn, conv, full["w_attn_o"], full["w_conv_out"],
                                                  full["w_out"], comm=_pair_exchange_comm(gff, ff))
    cp_ff, own_ff = sums(ff, gff, got)
    sq = ["w_out", "w_attn_o", "w_conv_out"]
    gsq = {"w_out": _wgrad(merged, dx1b, "wgrad_out"), "w_attn_o": _wgrad(o, dat, "wgrad_attn_o"),
           "w_conv_out": _wgrad(h3, dcv, "wgrad_conv_out")}
    (dh1, lnacc), got = _conv_ln_bwd(dh3, h1, w["conv_ln_g"], w["conv_ln_b"], comm=_pair_exchange_comm(gsq, sq))
    cp_sq, own_sq = sums(sq, gsq, got)
    cp, own = {**cp_ff, **cp_sq}, {**own_ff, **own_sq}
    g["conv_ln_g"], g["conv_ln_b"], g["b_dw"] = lnacc[0:1], lnacc[1:2], lnacc[2:3]
    five = ff + sq
    (da, dg, g["w_dw"]), rc = _conv_bwd(dh1, proj, w_dw, comm=_chip_exchange_comm(cp, five))
    tot = {nm: _chip_sum(nm, own[nm], rc_nm, chip_core) for nm, rc_nm in zip(five, rc)}
    (dqn, dkc, dkp, dvc, dvp, dsk, dsa), shards = _attn_bwd(qn, kk, vv, bias, w["attn_sinks"], o, do, lse,
                                                            comm=_pair_share_comm(tot, five))
    shards = dict(zip(five, shards))
    g["attn_sinks"] = dsk[:, 0, 0].reshape(1, NQ)
    g["rel_bias"] = _bias_bwd(dsa)[:, 0:NBUCKET].T
    dq, dkv, ggq, ggk = _qkv_bwd(proj, gq2, gk2, dqn, dkc, dkp, dvc, dvp)
    g["q_norm_g"], g["k_norm_g"] = ggq[:, 0:HD], ggk[:, 0:HD]
    pieces = [dq, da, dg, dga, dgc, dkv]
    names = ["q", "a", "g", "ga", "gc", "kv"]
    gw = {nm: _wgrad(u, p, "wgrad_in_" + nm, tn=p.shape[1] if p.shape[1] < 1024 else 1024) for nm, p in zip(names, pieces)}
    gin = {"w_in": jnp.concatenate([gw["q"], gw["kv"], gw["a"], gw["g"], gw["ga"], gw["gc"]], axis=1)}
    got = _run_comm(_pair_exchange_comm(gin, ["w_in"]), "rs_pair_exchange_in")
    cp_in, own_in = sums(["w_in"], gin, got)
    (grad_x, g["norm_mix_g"]), rc = _inproj_bwd(pieces, w_in, x, dx1, w["norm_mix_g"],
                                                comm=_chip_exchange_comm(cp_in, ["w_in"]))
    tot = {"w_in": _chip_sum("w_in", own_in["w_in"], rc[0], chip_core)}
    shards["w_in"] = _run_comm(_pair_share_comm(tot, ["w_in"]), "rs_pair_share_in")[0]
    return loss[0, 0], grad_x, g, shards


BIG = ["w_in", "w_attn_o", "w_conv_out", "w_out", "w_ff1", "w_ff2"]
SHARD_AXIS = {"w_in": 1, "w_attn_o": 0, "w_conv_out": 0, "w_out": 0, "w_ff1": 1, "w_ff2": 0, "w_dw": 1}
SHARD_SHAPE = {"w_in": (D, INW // 4), "w_attn_o": (D // 4, D), "w_conv_out": (D // 4, D), "w_out": (D // 4, D),
               "w_ff1": (D, DFF // 4), "w_ff2": (DFF // 4, D), "w_dw": (HALO, D // 4)}


def _position():
    x, y, c = lax.axis_index("x"), lax.axis_index("y"), lax.axis_index("c")
    other_chips = [(1 - x, y), (x, 1 - y), (1 - x, 1 - y)]
    return x, y, c, 2 * x + y, other_chips


def _shard_window(name, full_ref, s, half=None):
    R, C = SHARD_SHAPE[name]
    r0, nr = (0, R) if half is None else (half * (R // 2), R // 2)
    if SHARD_AXIS[name] == 1:
        return full_ref.at[pl.ds(r0, nr), pl.ds(s * C, C)]
    return full_ref.at[pl.ds(s * R + r0, nr), :]


def _remote(src, dst, send_sems, recv_sems, k, device):
    return pltpu.make_async_remote_copy(src_ref=src, dst_ref=dst, send_sem=send_sems.at[k], recv_sem=recv_sems.at[k],
                                        device_id=device, device_id_type=MESH)


def _full_shape(nm):
    R, C = SHARD_SHAPE[nm]
    return (R, 4 * C) if SHARD_AXIS[nm] == 1 else (4 * R, C)


def _place_shard(nm, shard, chip_arr, dtype):
    R, C = SHARD_SHAPE[nm]
    tr = min(R, 256)
    if SHARD_AXIS[nm] == 1:
        o_map = lambda i, ch: (i, ch[0])
    else:
        o_map = lambda i, ch: (ch[0] * (R // tr) + i, 0)

    def body(ch_ref, s_ref, o_ref):
        o_ref[...] = s_ref[...].astype(dtype)

    return pl.pallas_call(
        body,
        grid_spec=pltpu.PrefetchScalarGridSpec(
            num_scalar_prefetch=1, grid=(R // tr,),
            in_specs=[pl.BlockSpec((tr, C), lambda i, ch: (i, 0))], out_specs=pl.BlockSpec((tr, C), o_map)),
        out_shape=SDS(_full_shape(nm), dtype), name="place_" + nm,
        compiler_params=_cparams(("parallel",), 32))(chip_arr, shard)


def _gather_comm(placed):
    names = list(placed)
    n = len(names)

    def copies(cout, send, recv):
        x, y, c, chip, chips = _position()
        for a, nm in enumerate(names):
            for j, (cx, cy) in enumerate(chips):
                def ici(s, a=a, nm=nm, j=j, cx=cx, cy=cy):
                    w = _shard_window(nm, cout[a], s, c)
                    return _remote(w, w, send, recv, 6 * a + j, (cx, cy, c))

                def d2d(h, a=a, nm=nm, j=j, cx=cx, cy=cy):
                    w = _shard_window(nm, cout[a], 2 * cx + cy, h)
                    return _remote(w, w, send, recv, 6 * a + 3 + j, (x, y, 1 - c))

                yield ici, d2d, chip, 2 * cx + cy, c

    def start(cin, cout, send, recv):
        for ici, d2d, chip, s, c in copies(cout, send, recv):
            ici(chip).start()

    def mid(cin, cout, send, recv):
        for ici, d2d, chip, s, c in copies(cout, send, recv):
            ici(s).wait_recv()
            d2d(c).start()

    def finish(cin, cout, send, recv):
        for ici, d2d, chip, s, c in copies(cout, send, recv):
            d2d(1 - c).wait_recv()
        for ici, d2d, chip, s, c in copies(cout, send, recv):
            ici(chip).wait_send()
            d2d(c).wait_send()

    return _Comm([placed[nm] for nm in names], [SDS(placed[nm].shape, placed[nm].dtype) for nm in names], 6 * n,
                 start, finish, mid, aliases={a: a for a in range(n)})


def _half_rows(nm):
    return SHARD_SHAPE[nm][0] // 2


RS_TILE = 128


def _exchange_comm(ins, out_shapes, copies, n_sems, aliases=None):
    def start(cin, cout, send, recv):
        for cp in copies(cin, cout, send, recv):
            cp.start()

    def finish(cin, cout, send, recv):
        for cp in copies(cin, cout, send, recv):
            cp.wait()

    return _Comm(ins, out_shapes, n_sems, start, finish, aliases=aliases)


def _pair_exchange_comm(grads, names):
    def copies(cin, cout, send, recv):
        x, y, c, chip, chips = _position()
        return [_remote(_shard_window(nm, cin[a], s, 1 - c), cout[a].at[s], send, recv, 4 * a + s, (x, y, 1 - c))
                for a, nm in enumerate(names) for s in range(4)]

    return _exchange_comm([grads[nm] for nm in names],
                          [SDS((4, _half_rows(nm), SHARD_SHAPE[nm][1]), f32) for nm in names], copies, 4 * len(names))


def _pair_sum(nm, g, got, chip_core):
    R, C = SHARD_SHAPE[nm]
    hr = R // 2
    nt = hr // RS_TILE
    if SHARD_AXIS[nm] == 1:
        g_map = lambda i, s, sc: (sc[1] * nt + i, s)
    else:
        g_map = lambda i, s, sc: (s * (R // RS_TILE) + sc[1] * nt + i, 0)

    def body(sc_ref, g_ref, got_ref, o16_ref, own_ref):
        v = g_ref[...] + got_ref[0]
        o16_ref[0] = v.astype(bf16)

        @pl.when(pl.program_id(1) == sc_ref[0])
        def _():
            own_ref[...] = v

    blk3 = pl.BlockSpec((1, RS_TILE, C), lambda i, s, sc: (s, i, 0))
    return pl.pallas_call(
        body,
        grid_spec=pltpu.PrefetchScalarGridSpec(
            num_scalar_prefetch=1, grid=(nt, 4),
            in_specs=[pl.BlockSpec((RS_TILE, C), g_map), blk3],
            out_specs=[blk3, pl.BlockSpec((RS_TILE, C), lambda i, s, sc: (i, 0))]),
        out_shape=[SDS((4, hr, C), bf16), SDS((hr, C), f32)], name="rs_pair_sum_" + nm,
        compiler_params=_cparams(("parallel", "arbitrary"), 32))(chip_core, g, got)


def _chip_exchange_comm(cp, names):
    def copies(cin, cout, send, recv):
        x, y, c, chip, chips = _position()
        return [_remote(cin[a].at[2 * cx + cy], cout[a].at[j], send, recv, 3 * a + j, (cx, cy, c))
                for a, nm in enumerate(names) for j, (cx, cy) in enumerate(chips)]

    return _exchange_comm([cp[nm] for nm in names],
                          [SDS((3, _half_rows(nm), SHARD_SHAPE[nm][1]), bf16) for nm in names], copies, 3 * len(names))


def _chip_sum(nm, own, rc, chip_core):
    R, C = SHARD_SHAPE[nm]
    nt = (R // 2) // RS_TILE

    def body(sc_ref, own_ref, rc_ref, o_ref):
        o_ref[...] = own_ref[...] + rc_ref[0].astype(f32) + rc_ref[1].astype(f32) + rc_ref[2].astype(f32)

    return pl.pallas_call(
        body,
        grid_spec=pltpu.PrefetchScalarGridSpec(
            num_scalar_prefetch=1, grid=(nt,),
            in_specs=[pl.BlockSpec((RS_TILE, C), lambda i, sc: (i, 0)),
                      pl.BlockSpec((3, RS_TILE, C), lambda i, sc: (0, i, 0))],
            out_specs=pl.BlockSpec((RS_TILE, C), lambda i, sc: (sc[1] * nt + i, 0))),
        out_shape=SDS((R, C), f32), name="rs_chip_sum_" + nm,
        compiler_params=_cparams(("parallel",), 32))(chip_core, own, rc)


def _pair_share_comm(tot, names):
    def copies(cin, cout, send, recv):
        x, y, c, chip, chips = _position()
        cps = []
        for a, nm in enumerate(names):
            hr = _half_rows(nm)
            mine = cout[a].at[pl.ds(c * hr, hr), :]
            cps.append(_remote(mine, mine, send, recv, a, (x, y, 1 - c)))
        return cps

    return _exchange_comm([tot[nm] for nm in names], [SDS(SHARD_SHAPE[nm], f32) for nm in names], copies, len(names),
                          aliases={a: a for a in range(len(names))})


SMALL_ROWS = 40


def _allreduce_small(block):
    def body(x_ref, out_ref, buf, send_sems, recv_sems, local_sem):
        x, y, c, chip, chips = _position()
        me, sibling = (x, y, c), (x, y, 1 - c)

        def slot(px, py, pc):
            return buf.at[4 * px + 2 * py + pc]

        def copy(k, block_of, to, src=None):
            return _remote(slot(*block_of) if src is None else src, slot(*block_of), send_sems, recv_sems, k, to)

        mine = pltpu.make_async_copy(x_ref, slot(*me), local_sem)
        mine.start()
        first = [copy(0, me, sibling, src=x_ref)] + [copy(1 + j, me, (*ch, c), src=x_ref) for j, ch in enumerate(chips)]
        for cp in first:
            cp.start()
        passed = [copy(4 + j, (*ch, c), sibling) for j, ch in enumerate(chips)]
        for j, ch in enumerate(chips):
            copy(1 + j, (*ch, c), me).wait_recv()
            passed[j].start()
        copy(0, sibling, me).wait_recv()
        for j, ch in enumerate(chips):
            copy(4 + j, (*ch, 1 - c), me).wait_recv()
        for cp in first + passed:
            cp.wait_send()
        mine.wait()
        acc = buf[0]
        for d in range(1, 8):
            acc = acc + buf[d]
        out_ref[...] = acc

    vm = pl.BlockSpec(memory_space=pltpu.VMEM)
    return pl.pallas_call(
        body, in_specs=[vm], out_specs=vm, out_shape=SDS((SMALL_ROWS, D), f32),
        scratch_shapes=[pltpu.VMEM((8, SMALL_ROWS, D), f32), pltpu.SemaphoreType.DMA((7,)), pltpu.SemaphoreType.DMA((7,)),
                        pltpu.SemaphoreType.DMA],
        name="allreduce_small")(block)


def _adamw(w, g, m, v, name):
    rows, cols = w.shape
    tr = 256 if rows % 256 == 0 else rows

    def body(w_ref, g_ref, m_ref, v_ref, d_ref, nm_ref, nv_ref):
        gv = g_ref[...]
        m2 = ADAM_B1 * m_ref[...] + (1.0 - ADAM_B1) * gv
        v2 = ADAM_B2 * v_ref[...] + (1.0 - ADAM_B2) * jnp.square(gv)
        m_hat = m2 / (1.0 - ADAM_B1 ** ADAM_STEP)
        v_hat = v2 / (1.0 - ADAM_B2 ** ADAM_STEP)
        d_ref[...] = -ADAM_LR * (m_hat / (jnp.sqrt(v_hat) + ADAM_EPS) + ADAM_WD * w_ref[...])
        nm_ref[...] = m2
        nv_ref[...] = v2

    spec = pl.BlockSpec((tr, cols), lambda i: (i, 0))
    return pl.pallas_call(body, grid=(rows // tr,), in_specs=[spec] * 4, out_specs=[spec] * 3,
                          out_shape=[SDS((rows, cols), f32)] * 3, name=name,
                          compiler_params=_cparams(("parallel",), 40))(w, g, m, v)


WEIGHTS = ["norm_mix_g", "w_in", "q_norm_g", "k_norm_g", "attn_sinks", "rel_bias", "w_attn_o", "w_dw", "b_dw",
           "conv_ln_g", "conv_ln_b", "w_conv_out", "w_out", "norm_mlp_g", "w_ff1", "w_ff2"]
ROW_VECS = ["norm_mix_g", "b_dw", "conv_ln_g", "conv_ln_b", "norm_mlp_g"]
MISC_ROW = 5
W_DW_ROW = 8


def _pack_small(vals, loss=None):
    misc = [vals["q_norm_g"].reshape(1, HD), vals["k_norm_g"].reshape(1, HD), vals["attn_sinks"].reshape(1, NQ),
            jnp.zeros((1, 1), f32) if loss is None else loss.reshape(1, 1), jnp.zeros((1, 111), f32),
            vals["rel_bias"].reshape(1, NBUCKET * NQ), jnp.zeros((1, 256), f32)]
    rows = [vals[nm].reshape(1, D) for nm in ROW_VECS] + [jnp.concatenate(misc, axis=1), jnp.zeros((2, D), f32)]
    return jnp.concatenate(rows, axis=0)


def _unpack_small(block):
    out = {nm: block[i:i + 1] for i, nm in enumerate(ROW_VECS)}
    misc = block[MISC_ROW]
    out["q_norm_g"] = misc[0:64].reshape(1, HD)
    out["k_norm_g"] = misc[64:128].reshape(1, HD)
    out["attn_sinks"] = misc[128:144].reshape(1, NQ)
    out["rel_bias"] = misc[256:768].reshape(NBUCKET, NQ)
    return out, misc[144]


def kernel(x, norm_mix_g, w_in, q_norm_g, k_norm_g, attn_sinks, rel_bias, w_attn_o, w_dw, b_dw, conv_ln_g, conv_ln_b, w_conv_out, w_out, norm_mlp_g, w_ff1, w_ff2, loss_target, m_norm_mix_g, m_w_in, m_q_norm_g, m_k_norm_g, m_attn_sinks, m_rel_bias, m_w_attn_o, m_w_dw, m_b_dw, m_conv_ln_g, m_conv_ln_b, m_w_conv_out, m_w_out, m_norm_mlp_g, m_w_ff1, m_w_ff2, v_norm_mix_g, v_w_in, v_q_norm_g, v_k_norm_g, v_attn_sinks, v_rel_bias, v_w_attn_o, v_w_dw, v_b_dw, v_conv_ln_g, v_conv_ln_b, v_w_conv_out, v_w_out, v_norm_mlp_g, v_w_ff1, v_w_ff2):
    args = dict(locals())
    wts = {nm: args[nm] for nm in WEIGHTS}
    mom = {nm: args["m_" + nm] for nm in WEIGHTS}
    var = {nm: args["v_" + nm] for nm in WEIGHTS}
    chip = 2 * lax.axis_index("x") + lax.axis_index("y")

    chip_arr = jnp.reshape(chip, (1,)).astype(jnp.int32)
    chip_core = jnp.stack([chip, lax.axis_index("c")]).astype(jnp.int32)
    placed = {nm: _place_shard(nm, wts[nm][0], chip_arr, bf16) for nm in BIG}
    placed["w_dw"] = _place_shard("w_dw", jnp.pad(w_dw[0], ((0, 1), (0, 0))), chip_arr, f32)

    loss_part, grad_x, g, shards = _forward_backward(x[0], loss_target[0], wts, placed, chip_core)

    small = jnp.concatenate([_pack_small(g, loss_part), g["w_dw"]], axis=0)
    small = _allreduce_small(small)
    grads, loss = _unpack_small(small)
    grads["w_dw"] = lax.dynamic_slice(small[W_DW_ROW:W_DW_ROW + CW], (0, chip * (D // 4)), (CW, D // 4))
    grads.update(shards)

    delta, new_m, new_v = {}, {}, {}
    sd, sm, sv = _adamw(_pack_small(wts), small[0:8], _pack_small(mom), _pack_small(var), "adamw_small")
    for res, blk in ((delta, sd), (new_m, sm), (new_v, sv)):
        res.update(_unpack_small(blk)[0])
    for nm in BIG + ["w_dw"]:
        shp = wts[nm].shape
        two_d = lambda a: a.reshape(shp[-2], shp[-1])
        delta[nm], new_m[nm], new_v[nm] = _adamw(two_d(wts[nm]), grads[nm], two_d(mom[nm]), two_d(var[nm]), "adamw_" + nm)

    def shaped(vals):
        return [vals[nm].reshape(wts[nm].shape) for nm in WEIGHTS]

    return (loss, grad_x[None], *shaped(grads), *shaped(delta), *shaped(new_m), *shaped(new_v))
```

```python
import functools

import numpy as np
import jax
import jax.numpy as jnp
from jax import lax
from jax.experimental import pallas as pl
from jax.experimental.pallas import tpu as pltpu

f32 = jnp.float32
bf16 = jnp.bfloat16
SDS = jax.ShapeDtypeStruct
MESH = pl.DeviceIdType.MESH

D = 1024
HD = 64
NQ = 16
NKV = 4
BLK = 128
CW = 31
HALO = 32
DFF = 4096
NBUCKET = 32
EPS = 1e-6
NEG = -1e30
INW = 5632
MIX_CHUNK = 256
C_Q, C_A, C_G, C_GA, C_GC = 0, 1, 2, 3, 4
C_KV = 10

ADAM_LR = 0.001
ADAM_B1 = 0.9
ADAM_B2 = 0.999
ADAM_EPS = 1e-08
ADAM_WD = 0.01
ADAM_STEP = 10

VMEM_BYTES_V7X = 64 << 20


def _cparams(sem, vmem_mb):
    assert (vmem_mb << 20) < VMEM_BYTES_V7X
    return pltpu.CompilerParams(dimension_semantics=sem, vmem_limit_bytes=vmem_mb << 20)


ANY = pl.BlockSpec(memory_space=pl.ANY)


class _Comm:
    def __init__(self, ins, out_shapes, n_sems, start, finish, mid=None, aliases=None):
        self.ins, self.out_shapes, self.n_sems = list(ins), list(out_shapes), n_sems
        self.start, self.finish, self.mid, self.aliases = start, finish, mid, dict(aliases or {})


def _call(body, args, *, grid, in_specs, out_specs, out_shape, name, sem, vmem_mb, scratch_shapes=(), comm=None,
          mid_step=None):
    n_in, n_out, n_scr = len(in_specs), len(out_specs), len(scratch_shapes)
    if comm is None:
        outs = pl.pallas_call(body, grid=grid, in_specs=list(in_specs), out_specs=list(out_specs),
                              out_shape=list(out_shape), scratch_shapes=list(scratch_shapes), name=name,
                              compiler_params=_cparams(sem, vmem_mb))(*args)
        return list(outs), []
    ci, co = len(comm.ins), len(comm.out_shapes)
    last = grid[0] - 1

    def wrapped(*refs):
        ins, cin = refs[:n_in], refs[n_in:n_in + ci]
        outs = refs[n_in + ci:n_in + ci + n_out]
        cout = refs[n_in + ci + n_out:n_in + ci + n_out + co]
        scr = refs[n_in + ci + n_out + co:]
        send, recv = scr[n_scr], scr[n_scr + 1]
        step = pl.program_id(0)

        @pl.when(step == 0)
        def _():
            comm.start(cin, cout, send, recv)

        body(*ins, *outs, *scr[:n_scr])
        if comm.mid is not None:
            @pl.when(step == mid_step)
            def _():
                comm.mid(cin, cout, send, recv)

        @pl.when(step == last)
        def _():
            comm.finish(cin, cout, send, recv)

    res = pl.pallas_call(
        wrapped, grid=grid, in_specs=list(in_specs) + [ANY] * ci, out_specs=list(out_specs) + [ANY] * co,
        out_shape=list(out_shape) + comm.out_shapes,
        input_output_aliases={n_in + k: n_out + v for k, v in comm.aliases.items()},
        scratch_shapes=list(scratch_shapes) + [pltpu.SemaphoreType.DMA((comm.n_sems,))] * 2,
        name=name, compiler_params=_cparams(("arbitrary",), vmem_mb))(*args, *comm.ins)
    return list(res[:n_out]), list(res[n_out:])


def _run_comm(comm, name):
    ci, co = len(comm.ins), len(comm.out_shapes)

    def body(*refs):
        cin, cout, (send, recv) = refs[:ci], refs[ci:ci + co], refs[ci + co:]
        comm.start(cin, cout, send, recv)
        if comm.mid is not None:
            comm.mid(cin, cout, send, recv)
        comm.finish(cin, cout, send, recv)

    return pl.pallas_call(
        body, in_specs=[ANY] * ci, out_specs=[ANY] * co, out_shape=comm.out_shapes, input_output_aliases=comm.aliases,
        scratch_shapes=[pltpu.SemaphoreType.DMA((comm.n_sems,))] * 2, name=name)(*comm.ins)


def _dot(a, b):
    return jnp.dot(a, b, preferred_element_type=f32)


def _dot_nt(a, b):
    return lax.dot_general(a, b, (((1,), (1,)), ((), ())), preferred_element_type=f32)


def _dot_tn(a, b):
    return lax.dot_general(a, b, (((0,), (0,)), ((), ())), preferred_element_type=f32)


def _sigmoid(x):
    return 1.0 / (1.0 + jnp.exp(-x))


def _low_head_lanes():
    return lax.broadcasted_iota(jnp.int32, (1, 2 * HD), 1) < HD


def _head_blockdiag():
    r = lax.broadcasted_iota(jnp.int32, (2 * HD, 2 * HD), 0) // HD
    c = lax.broadcasted_iota(jnp.int32, (2 * HD, 2 * HD), 1) // HD
    return jnp.where(r == c, 1.0, 0.0).astype(bf16)


def _head_sums(z, bd):
    hi = z.astype(bf16)
    lo = (z - hi.astype(f32)).astype(bf16)
    return _dot(hi, bd) + _dot(lo, bd)


def _rms_inproj(x, g, w, perm, comm=None):
    T, N = x.shape[0], w.shape[1]
    tn = 512
    conv_cols = (C_A * D, (C_G + 1) * D)

    def body(x_ref, g_ref, w_ref, perm_ref, p_ref, u_ref, tail_ref):
        xv = x_ref[...]
        r = lax.rsqrt(jnp.mean(xv * xv, axis=-1, keepdims=True) + EPS)
        u = (xv * r * g_ref[...]).astype(bf16)
        u_ref[...] = u
        u_blocks = _dot(perm_ref[...], u).astype(bf16)
        for c in range(N // tn):
            cols = slice(c * tn, (c + 1) * tn)
            lhs = u_blocks if conv_cols[0] <= c * tn < conv_cols[1] else u
            p_ref[:, cols] = _dot(lhs, w_ref[:, cols])
        tail_ref[...] = _dot(u[TT - NBLK:TT], w_ref[:, conv_cols[0]:conv_cols[1]])

    once = pl.Buffered(1)
    (proj, u, tail), got = _call(
        body, (x, g, w, perm), grid=(T // TT,),
        in_specs=[pl.BlockSpec((TT, D), lambda i: (i, 0)),
                  pl.BlockSpec((1, D), lambda i: (0, 0)),
                  pl.BlockSpec((D, N), lambda i: (0, 0), pipeline_mode=once),
                  pl.BlockSpec((TT, TT), lambda i: (0, 0), pipeline_mode=once)],
        out_specs=[pl.BlockSpec((TT, N), lambda i: (i, 0)),
                   pl.BlockSpec((TT, D), lambda i: (i, 0)),
                   pl.BlockSpec((NBLK, 2 * D), lambda i: (i, 0))],
        out_shape=[SDS((T, N), f32), SDS((T, D), bf16), SDS((T // TT * NBLK, 2 * D), f32)],
        name="rms_inproj", sem=("parallel",), vmem_mb=48, comm=comm, mid_step=(3 * (T // TT)) // 4)
    return proj, u, tail, got


def _split_pair(pair, out_ref, p, lo):
    rolled = pltpu.roll(pair, HD, 1)
    zero = jnp.zeros_like(pair)
    c = 512 * p
    out_ref[:, c:c + 128] = jnp.where(lo, pair, zero).astype(bf16)
    out_ref[:, c + 128:c + 256] = jnp.where(lo, zero, rolled).astype(bf16)
    out_ref[:, c + 256:c + 384] = jnp.where(lo, rolled, zero).astype(bf16)
    out_ref[:, c + 384:c + 512] = jnp.where(lo, zero, pair).astype(bf16)


def _qk_prep(proj, gq2, gk2):
    T = proj.shape[0]
    tm = 512

    def body(q_ref, kv_ref, gq_ref, gk_ref, qn_ref, kk_ref, vv_ref):
        bd = _head_blockdiag()
        lo = _low_head_lanes()
        for p in range(NQ // 2):
            z = q_ref[:, 128 * p:128 * p + 128]
            r = lax.rsqrt(_head_sums(z * z, bd) * (1.0 / HD) + EPS)
            qn_ref[:, 128 * p:128 * p + 128] = (z * r * gq_ref[...] * (HD ** -0.5)).astype(bf16)
        for p in range(NKV // 2):
            z = kv_ref[:, 128 * p:128 * p + 128]
            r = lax.rsqrt(_head_sums(z * z, bd) * (1.0 / HD) + EPS)
            _split_pair(z * r * gk_ref[...], kk_ref, p, lo)
            _split_pair(kv_ref[:, 256 + 128 * p:256 + 128 * p + 128], vv_ref, p, lo)

    return pl.pallas_call(
        body, grid=(T // tm,),
        in_specs=[pl.BlockSpec((tm, D), lambda i: (i, C_Q)),
                  pl.BlockSpec((tm, 512), lambda i: (i, C_KV)),
                  pl.BlockSpec((1, 128), lambda i: (0, 0)),
                  pl.BlockSpec((1, 128), lambda i: (0, 0))],
        out_specs=[pl.BlockSpec((tm, D), lambda i: (i, 0))] * 3,
        out_shape=[SDS((T, D), bf16)] * 3,
        name="qk_prep", compiler_params=_cparams(("parallel",), 32))(proj, proj, gq2, gk2)


def _bucket_tile():
    qi = np.arange(BLK)[:, None]
    kj = np.arange(BLK)[None, :]
    n = np.where(kj > qi, qi + BLK - kj, qi - kj)
    max_exact = NBUCKET // 2
    nf = np.maximum(n, 1).astype(np.float32)
    large = max_exact + (np.log(nf / max_exact) / np.float32(np.log(128 / max_exact))
                         * (NBUCKET - max_exact)).astype(np.int32)
    large = np.minimum(large, NBUCKET - 1)
    return np.where(n < max_exact, n, large).astype(np.int32)


def _from_prev_block():
    return lax.broadcasted_iota(jnp.int32, (BLK, BLK), 1) > lax.broadcasted_iota(jnp.int32, (BLK, BLK), 0)


def _bias_tiles(rel_bias):
    def body(rb_ref, bk_ref, out_ref):
        bk = bk_ref[...]
        for h in range(NQ):
            acc = jnp.zeros((BLK, BLK), f32)
            for b in range(NBUCKET):
                acc = jnp.where(bk == b, rb_ref[b, h], acc)
            out_ref[h] = acc

    return pl.pallas_call(
        body,
        in_specs=[pl.BlockSpec(memory_space=pltpu.SMEM), pl.BlockSpec(memory_space=pltpu.VMEM)],
        out_specs=pl.BlockSpec(memory_space=pltpu.VMEM),
        out_shape=SDS((NQ, BLK, BLK), f32),
        name="bias_tiles")(rel_bias, jnp.asarray(_bucket_tile()))


def _rows2(ref, c):
    return jnp.concatenate([ref[:, c:c + 128], ref[:, c + 128:c + 256]], axis=0)


def _attn_fwd(qn, kk, vv, bias, sinks, comm=None):
    T = qn.shape[0]
    nb = T // BLK

    def body(s_ref, q_ref, kc_ref, kp_ref, vc_ref, vp_ref, b_ref, o_ref, lse_ref):
        prev = _from_prev_block()
        no_key = jnp.logical_and(prev, pl.program_id(0) == 0)
        scores = []
        for h in range(NKV):
            qs = _rows2(q_ref, 256 * h)
            scores.append((_dot_nt(qs, _rows2(kc_ref, 256 * h)), _dot_nt(qs, _rows2(kp_ref, 256 * h))))
        for h in range(NKV):
            c = 256 * h
            sc, sp = scores[h]
            vstack = jnp.concatenate([vp_ref[:, c:c + 128], vc_ref[:, c:c + 128],
                                      vp_ref[:, c + 128:c + 256], vc_ref[:, c + 128:c + 256]], axis=0)
            for pr in range(2):
                ps = []
                for e in range(2):
                    hq = 4 * h + 2 * pr + e
                    rows, cols = slice(128 * pr, 128 * pr + 128), slice(128 * e, 128 * e + 128)
                    s = jnp.where(no_key, NEG, jnp.where(prev, sp[rows, cols], sc[rows, cols]) + b_ref[hq])
                    sink = s_ref[0, hq]
                    m = jnp.maximum(jnp.max(s, axis=-1, keepdims=True), sink)
                    ex = jnp.exp(s - m)
                    l = jnp.sum(ex, axis=-1, keepdims=True) + jnp.exp(sink - m)
                    p = ex * (1.0 / l)
                    ps += [jnp.where(prev, p, 0.0).astype(bf16), jnp.where(prev, 0.0, p).astype(bf16)]
                    lse_ref[:, hq:hq + 1] = m + jnp.log(l)
                o_ref[:, c + 128 * pr:c + 128 * pr + 128] = _dot(jnp.concatenate(ps, axis=1), vstack).astype(bf16)

    blk = lambda f: pl.BlockSpec((BLK, D), f)
    cur = lambda n: (n, 0)
    prev = lambda n: (jnp.maximum(n - 1, 0), 0)
    return _call(
        body, (sinks, qn, kk, kk, vv, vv, bias), grid=(nb,),
        in_specs=[pl.BlockSpec(memory_space=pltpu.SMEM), blk(cur), blk(cur), blk(prev), blk(cur), blk(prev),
                  pl.BlockSpec((NQ, BLK, BLK), lambda n: (0, 0, 0))],
        out_specs=[blk(cur), pl.BlockSpec((BLK, NQ), cur)],
        out_shape=[SDS((T, D), bf16), SDS((T, NQ), f32)],
        name="attn_fwd", sem=("parallel",), vmem_mb=32, comm=comm, mid_step=(3 * nb) // 4)


TT = 512
NBLK = 32
RPB = TT // NBLK
CONV_LANES = 128
KBLK = 4
GBLK = 8
TAPG = 8


def _block_perm():
    p = np.arange(TT)
    m = np.zeros((TT, TT), np.float32)
    m[p, NBLK * (p % RPB) + p // RPB] = 1.0
    return jnp.asarray(m, bf16), jnp.asarray(m.T, bf16)


def _lane_groups():
    return [slice(q * CONV_LANES, (q + 1) * CONV_LANES) for q in range(D // CONV_LANES)]


def _fill_time_blocks(z, tile, edge, causal):
    row = lax.broadcasted_iota(jnp.int32, (RPB, 1), 0)
    for k in range(NBLK):
        blk = tile[RPB * k:RPB * (k + 1)]
        if causal:
            z[NBLK + k] = blk
            z[k] = jnp.where(row == 0, edge[k:k + 1], pltpu.roll(blk, 1, 0))
        else:
            z[k] = blk
            z[NBLK + k] = jnp.where(row == RPB - 1, edge[k:k + 1], pltpu.roll(blk, RPB - 1, 0))


def _block_conv(z, w_ref, tap_offset, init, store):
    def step(s, carry):
        k0 = s * KBLK
        for ln in _lane_groups():
            accs = [init(ln) for _ in range(KBLK)]
            for g0 in range(0, CW, TAPG):
                taps = range(g0, min(g0 + TAPG, CW))
                lo = min(tap_offset(j) for j in taps)
                hi = max(tap_offset(j) for j in taps)
                win = [z[k0 + lo + d, :, ln] for d in range(KBLK + hi - lo)]
                for j in taps:
                    wv = w_ref[j:j + 1, ln]
                    for q in range(KBLK):
                        accs[q] = accs[q] + win[q + tap_offset(j) - lo] * wv
            for q in range(KBLK):
                store(k0 + q, ln, accs[q])
        return carry

    lax.fori_loop(0, NBLK // KBLK, step, 0)


def _block_rows(k):
    return pl.ds(pl.multiple_of(k * RPB, RPB), RPB)


def _glu_conv_fwd(proj, tail, w_dw, b_dw, ln_g, ln_b, perm_t, comm=None):
    T = proj.shape[0]

    def body(a_ref, g_ref, ta_ref, tg_ref, w_ref, b_ref, lg_ref, lb_ref, pt_ref, h1_ref, h3_ref, z):
        edge = jnp.where(pl.program_id(0) > 0, ta_ref[...] * _sigmoid(tg_ref[...]), 0.0)
        _fill_time_blocks(z, a_ref[...] * _sigmoid(g_ref[...]), edge, causal=True)

        def store(k, ln, value):
            h1_ref[_block_rows(k), ln] = value

        _block_conv(z, w_ref, lambda j: NBLK - (CW - 1) + j,
                    lambda ln: jnp.broadcast_to(b_ref[:, ln], (RPB, CONV_LANES)), store)
        h1 = h1_ref[...]
        mu = jnp.mean(h1, axis=-1, keepdims=True)
        xc = h1 - mu
        var = jnp.mean(xc * xc, axis=-1, keepdims=True)
        h2 = xc * lax.rsqrt(var + EPS) * lg_ref[...] + lb_ref[...]
        h3_ref[...] = _dot(pt_ref[...], (h2 * _sigmoid(h2)).astype(bf16)).astype(bf16)

    tile = lambda cb: pl.BlockSpec((TT, D), lambda i: (i, cb))
    edge = lambda cb: pl.BlockSpec((NBLK, D), lambda i: (jnp.maximum(i - 1, 0), cb))
    vec = pl.BlockSpec((1, D), lambda i: (0, 0))
    (h1, h3), got = _call(
        body, (proj, proj, tail, tail, w_dw, b_dw, ln_g, ln_b, perm_t), grid=(T // TT,),
        in_specs=[tile(C_A), tile(C_G), edge(0), edge(1), pl.BlockSpec((HALO, D), lambda i: (0, 0)), vec, vec, vec,
                  pl.BlockSpec((TT, TT), lambda i: (0, 0), pipeline_mode=pl.Buffered(1))],
        out_specs=[pl.BlockSpec((TT, D), lambda i: (i, 0))] * 2,
        out_shape=[SDS((T, D), f32), SDS((T, D), bf16)],
        scratch_shapes=[pltpu.VMEM((2 * NBLK, RPB, D), f32)],
        name="glu_conv_fwd", sem=("parallel",), vmem_mb=40, comm=comm, mid_step=(3 * (T // TT)) // 4)
    return h1, h3, got


def _mix_out(o, h3, proj, x, w_attn_o, w_conv_out, w_out, g_mlp):
    T = x.shape[0]
    tm = 512

    def body(o_ref, h3_ref, ga_ref, gc_ref, x_ref, wa_ref, wc_ref, wo_ref, g_ref,
             attn_ref, conv_ref, mg_ref, x1_ref, n2_ref):
        x1 = x_ref[...]
        for j in range(D // MIX_CHUNK):
            cols = slice(j * MIX_CHUNK, (j + 1) * MIX_CHUNK)
            attn = _dot(o_ref[...], wa_ref[:, cols])
            conv = _dot(h3_ref[...], wc_ref[:, cols])
            attn_ref[:, cols] = attn.astype(bf16)
            conv_ref[:, cols] = conv.astype(bf16)
            mg = (_sigmoid(ga_ref[:, cols]) * attn + _sigmoid(gc_ref[:, cols]) * conv).astype(bf16)
            mg_ref[:, cols] = mg
            x1 = x1 + _dot(mg, wo_ref[cols, :])
        x1_ref[...] = x1
        r = lax.rsqrt(jnp.mean(x1 * x1, axis=-1, keepdims=True) + EPS)
        n2_ref[...] = (x1 * r * g_ref[...]).astype(bf16)

    tile = lambda cb=0: pl.BlockSpec((tm, D), lambda i: (i, cb))
    wfull = pl.BlockSpec((D, D), lambda i: (0, 0), pipeline_mode=pl.Buffered(1))
    return pl.pallas_call(
        body, grid=(T // tm,),
        in_specs=[tile(), tile(), tile(C_GA), tile(C_GC), tile(), wfull, wfull, wfull,
                  pl.BlockSpec((1, D), lambda i: (0, 0))],
        out_specs=[tile()] * 5,
        out_shape=[SDS((T, D), bf16), SDS((T, D), bf16), SDS((T, D), bf16), SDS((T, D), f32), SDS((T, D), bf16)],
        name="mix_out", compiler_params=_cparams(("parallel",), 48))(o, h3, proj, proj, x, w_attn_o, w_conv_out, w_out, g_mlp)


def _mlp_fwd(n2, w1, w2, x1, tgt):
    T = n2.shape[0]
    tm, tf = 512, 1024

    def body(n2_ref, w1_ref, w2_ref, x1_ref, t_ref, hm_ref, dy_ref, dyb_ref, loss_ref):
        @pl.when(pl.program_id(0) == 0)
        def _():
            loss_ref[...] = jnp.zeros_like(loss_ref)

        n2v = n2_ref[...]
        for c in range(DFF // tf):
            r = jnp.maximum(_dot(n2v, w1_ref[:, c * tf:(c + 1) * tf]), 0.0)
            hm_ref[:, c * tf:(c + 1) * tf] = (r * r).astype(bf16)
        e = x1_ref[...] + _dot(hm_ref[...], w2_ref[...]) - t_ref[...]
        dy = e * (1.0 / D)
        dy_ref[...] = dy
        dyb_ref[...] = dy.astype(bf16)
        loss_ref[...] += 0.5 * jnp.sum(jnp.sum(e * e, axis=-1, keepdims=True) * (1.0 / D))

    row = pl.BlockSpec((tm, D), lambda i: (i, 0))
    once = pl.Buffered(1)
    return pl.pallas_call(
        body, grid=(T // tm,),
        in_specs=[row, pl.BlockSpec((D, DFF), lambda i: (0, 0), pipeline_mode=once),
                  pl.BlockSpec((DFF, D), lambda i: (0, 0), pipeline_mode=once), row, row],
        out_specs=[pl.BlockSpec((tm, DFF), lambda i: (i, 0)), row, row, pl.BlockSpec((8, 128), lambda i: (0, 0))],
        out_shape=[SDS((T, DFF), bf16), SDS((T, D), f32), SDS((T, D), bf16), SDS((8, 128), f32)],
        name="mlp_fwd", compiler_params=_cparams(("arbitrary",), 56))(n2, w1, w2, x1, tgt)


def _rms_bwd(xv, g, dn, dres):
    r = lax.rsqrt(jnp.mean(xv * xv, axis=-1, keepdims=True) + EPS)
    gd = dn * g
    dx = dres + r * gd - xv * (r * r * r) * jnp.mean(xv * gd, axis=-1, keepdims=True)
    dg = jnp.sum(dn * xv * r, axis=0, keepdims=True)
    return dx, dg


def _mlp_bwd(dy, dyb, hmid, w1, w2, x1, g_mlp):
    T = dy.shape[0]
    tm, tf = 512, 1024

    def body(dy_ref, dyb_ref, hm_ref, w1_ref, w2_ref, x1_ref, g_ref, df_ref, dx_ref, dxb_ref, dg_ref):
        @pl.when(pl.program_id(0) == 0)
        def _():
            dg_ref[...] = jnp.zeros_like(dg_ref)

        dyb = dyb_ref[...]
        for c in range(DFF // tf):
            cols = slice(c * tf, (c + 1) * tf)
            d_hm = _dot_nt(dyb, w2_ref[cols, :])
            df_ref[:, cols] = (d_hm * (2.0 * jnp.sqrt(hm_ref[:, cols].astype(f32)))).astype(bf16)
        dn = _dot_nt(df_ref[...], w1_ref[...])
        dx, dg = _rms_bwd(x1_ref[...], g_ref[...], dn, dy_ref[...])
        dx_ref[...] = dx
        dxb_ref[...] = dx.astype(bf16)
        dg_ref[...] += dg

    row = pl.BlockSpec((tm, D), lambda i: (i, 0))
    wide = pl.BlockSpec((tm, DFF), lambda i: (i, 0))
    vec = pl.BlockSpec((1, D), lambda i: (0, 0))
    once = pl.Buffered(1)
    return pl.pallas_call(
        body, grid=(T // tm,),
        in_specs=[row, row, wide, pl.BlockSpec((D, DFF), lambda i: (0, 0), pipeline_mode=once),
                  pl.BlockSpec((DFF, D), lambda i: (0, 0), pipeline_mode=once), row, vec],
        out_specs=[wide, row, row, vec],
        out_shape=[SDS((T, DFF), bf16), SDS((T, D), f32), SDS((T, D), bf16), SDS((1, D), f32)],
        name="mlp_bwd", compiler_params=_cparams(("arbitrary",), 56))(dy, dyb, hmid, w1, w2, x1, g_mlp)


def _wgrad(a, b, name, tn=1024):
    T, M = a.shape
    N = b.shape[1]
    tmm, tk = min(M, 1024), min(T, 2048)

    def body(a_ref, b_ref, o_ref):
        @pl.when(pl.program_id(2) == 0)
        def _():
            o_ref[...] = jnp.zeros_like(o_ref)

        o_ref[...] += _dot_tn(a_ref[...], b_ref[...])

    return pl.pallas_call(
        body, grid=(M // tmm, N // tn, T // tk),
        in_specs=[pl.BlockSpec((tk, tmm), lambda m, n, t: (t, m)), pl.BlockSpec((tk, tn), lambda m, n, t: (t, n))],
        out_specs=pl.BlockSpec((tmm, tn), lambda m, n, t: (m, n)),
        out_shape=SDS((M, N), f32),
        name=name, compiler_params=_cparams(("parallel", "parallel", "arbitrary"), 40))(a, b)


def _mix_bwd(dx1b, proj, attn, conv, w_attn_o, w_conv_out, w_out, perm, comm=None):
    T = dx1b.shape[0]
    tm = TT

    def body(dx_ref, ga_ref, gc_ref, attn_ref, conv_ref, wa_ref, wc_ref, wo_ref, perm_ref,
             dat_ref, dcv_ref, do_ref, dh3_ref, dga_ref, dgc_ref):
        d_o, d_h3 = None, None
        for j in range(D // MIX_CHUNK):
            cols = slice(j * MIX_CHUNK, (j + 1) * MIX_CHUNK)
            dm = _dot_nt(dx_ref[...], wo_ref[cols, :])
            sa = _sigmoid(ga_ref[:, cols])
            sc = _sigmoid(gc_ref[:, cols])
            dat = (dm * sa).astype(bf16)
            dcv = (dm * sc).astype(bf16)
            dat_ref[:, cols] = dat
            dcv_ref[:, cols] = dcv
            dga_ref[:, cols] = (dm * attn_ref[:, cols].astype(f32) * sa * (1.0 - sa)).astype(bf16)
            dgc_ref[:, cols] = (dm * conv_ref[:, cols].astype(f32) * sc * (1.0 - sc)).astype(bf16)
            part_o = _dot_nt(dat, wa_ref[:, cols])
            part_h = _dot_nt(_dot(perm_ref[...], dcv).astype(bf16), wc_ref[:, cols])
            d_o = part_o if d_o is None else d_o + part_o
            d_h3 = part_h if d_h3 is None else d_h3 + part_h
        do_ref[...] = d_o.astype(bf16)
        dh3_ref[...] = d_h3

    tile = lambda cb=0: pl.BlockSpec((tm, D), lambda i: (i, cb))
    wfull = pl.BlockSpec((D, D), lambda i: (0, 0), pipeline_mode=pl.Buffered(1))
    return _call(
        body, (dx1b, proj, proj, attn, conv, w_attn_o, w_conv_out, w_out, perm), grid=(T // tm,),
        in_specs=[tile(), tile(C_GA), tile(C_GC), tile(), tile(), wfull, wfull, wfull,
                  pl.BlockSpec((TT, TT), lambda i: (0, 0), pipeline_mode=pl.Buffered(1))],
        out_specs=[tile()] * 6,
        out_shape=[SDS((T, D), bf16), SDS((T, D), bf16), SDS((T, D), bf16), SDS((T, D), f32),
                   SDS((T, D), bf16), SDS((T, D), bf16)],
        name="mix_bwd", sem=("parallel",), vmem_mb=48, comm=comm)


def _conv_ln_bwd(dh3, h1, ln_g, ln_b, comm=None):
    T = dh3.shape[0]
    tt = TT // 2
    bpt = tt // RPB

    def body(d_ref, h1_ref, lg_ref, lb_ref, dh1_ref, acc_ref, head_ref):
        @pl.when(pl.program_id(0) == 0)
        def _():
            acc_ref[...] = jnp.zeros_like(acc_ref)

        h1 = h1_ref[...]
        mu = jnp.mean(h1, axis=-1, keepdims=True)
        xc = h1 - mu
        rstd = lax.rsqrt(jnp.mean(xc * xc, axis=-1, keepdims=True) + EPS)
        xh = xc * rstd
        h2 = xh * lg_ref[...] + lb_ref[...]
        sg = _sigmoid(h2)
        dh2 = d_ref[...] * (sg * (1.0 + h2 * (1.0 - sg)))
        dxh = dh2 * lg_ref[...]
        dh1 = rstd * (dxh - jnp.mean(dxh, axis=-1, keepdims=True) - xh * jnp.mean(dxh * xh, axis=-1, keepdims=True))
        dh1_ref[...] = dh1
        for k in range(bpt):
            head_ref[k:k + 1, :] = dh1[RPB * k:RPB * k + 1]
        acc_ref[0:1, :] += jnp.sum(dh2 * xh, axis=0, keepdims=True)
        acc_ref[1:2, :] += jnp.sum(dh2, axis=0, keepdims=True)
        acc_ref[2:3, :] += jnp.sum(dh1, axis=0, keepdims=True)

    tile = pl.BlockSpec((tt, D), lambda i: (i, 0))
    vec = pl.BlockSpec((1, D), lambda i: (0, 0))
    return _call(
        body, (dh3, h1, ln_g, ln_b), grid=(T // tt,),
        in_specs=[tile, tile, vec, vec],
        out_specs=[tile, pl.BlockSpec((8, D), lambda i: (0, 0)), pl.BlockSpec((bpt, D), lambda i: (i, 0))],
        out_shape=[SDS((T, D), f32), SDS((8, D), f32), SDS((T // RPB, D), f32)],
        name="conv_ln_bwd", sem=("arbitrary",), vmem_mb=32, comm=comm)


def _conv_bwd(dh1, head, proj, tail, w_dw, perm_t, comm=None):
    T = dh1.shape[0]
    nt = T // TT

    def body(d_ref, hd_ref, a_ref, g_ref, ta_ref, tg_ref, w_ref, pt_ref, da_ref, dg_ref, gw_ref, zd, zh, dh0, gacc):
        i = pl.program_id(0)

        @pl.when(i == 0)
        def _():
            gacc[...] = jnp.zeros_like(gacc)

        a = a_ref[...]
        sg = _sigmoid(g_ref[...])
        _fill_time_blocks(zd, d_ref[...], jnp.where(i < nt - 1, hd_ref[...], 0.0), causal=False)
        _fill_time_blocks(zh, a * sg, jnp.where(i > 0, ta_ref[...] * _sigmoid(tg_ref[...]), 0.0), causal=True)

        def store(k, ln, value):
            dh0[_block_rows(k), ln] = value

        _block_conv(zd, w_ref, lambda j: (CW - 1) - j, lambda ln: jnp.zeros((RPB, CONV_LANES), f32), store)

        for ln in _lane_groups():
            for g0 in range(0, CW, TAPG):
                taps = list(range(g0, min(g0 + TAPG, CW)))

                def add_blocks(s, accs, ln=ln, taps=taps):
                    k0 = s * GBLK
                    first = k0 + NBLK - (CW - 1) + taps[0]
                    win = [zh[first + t, :, ln] for t in range(GBLK + len(taps) - 1)]
                    accs = list(accs)
                    for q in range(GBLK):
                        d = zd[k0 + q, :, ln]
                        for n, j in enumerate(taps):
                            accs[n] = accs[n] + d * win[q + j - taps[0]]
                    return tuple(accs)

                accs = lax.fori_loop(0, NBLK // GBLK, add_blocks,
                                     tuple(jnp.zeros((RPB, CONV_LANES), f32) for _ in taps))
                for j, acc in zip(taps, accs):
                    gacc[j, :, ln] += acc

        d0 = dh0[...]
        da_ref[...] = _dot(pt_ref[...], (d0 * sg).astype(bf16)).astype(bf16)
        dg_ref[...] = _dot(pt_ref[...], (d0 * a * sg * (1.0 - sg)).astype(bf16)).astype(bf16)

        @pl.when(i == nt - 1)
        def _():
            gw_ref[...] = jnp.zeros_like(gw_ref)
            for j in range(CW):
                gw_ref[j:j + 1, :] = jnp.sum(gacc[j], axis=0, keepdims=True)

    tile = lambda cb=0: pl.BlockSpec((TT, D), lambda i: (i, cb))
    prev_edge = lambda cb: pl.BlockSpec((NBLK, D), lambda i: (jnp.maximum(i - 1, 0), cb))
    next_edge = pl.BlockSpec((NBLK, D), lambda i: (jnp.minimum(i + 1, nt - 1), 0))
    wspec = pl.BlockSpec((HALO, D), lambda i: (0, 0))
    return _call(
        body, (dh1, head, proj, proj, tail, tail, w_dw, perm_t), grid=(nt,),
        in_specs=[tile(), next_edge, tile(C_A), tile(C_G), prev_edge(0), prev_edge(1), wspec,
                  pl.BlockSpec((TT, TT), lambda i: (0, 0), pipeline_mode=pl.Buffered(1))],
        out_specs=[tile(), tile(), wspec],
        out_shape=[SDS((T, D), bf16), SDS((T, D), bf16), SDS((HALO, D), f32)],
        scratch_shapes=[pltpu.VMEM((2 * NBLK, RPB, D), f32), pltpu.VMEM((2 * NBLK, RPB, D), f32),
                        pltpu.VMEM((TT, D), f32), pltpu.VMEM((HALO, RPB, D), f32)],
        name="conv_bwd", sem=("arbitrary",), vmem_mb=48, comm=comm)


def _attn_bwd(qn, kk, vv, bias, sinks, o, do, lse, comm=None):
    T = qn.shape[0]
    nb = T // BLK

    def body(s_ref, q_ref, kc_ref, kp_ref, vc_ref, vp_ref, b_ref, o_ref, do_ref, lse_ref,
             dq_ref, dkc_ref, dkp_ref, dvc_ref, dvp_ref, dsk_ref, dsa_ref):
        n = pl.program_id(0)

        @pl.when(n == 0)
        def _():
            dsk_ref[...] = jnp.zeros_like(dsk_ref)
            dsa_ref[...] = jnp.zeros_like(dsa_ref)

        @pl.when(n == nb)
        def _():
            dkp_ref[...] = jnp.zeros_like(dkp_ref)
            dvp_ref[...] = jnp.zeros_like(dvp_ref)

        @pl.when(n < nb)
        def _():
            from_prev = _from_prev_block()
            no_key = jnp.logical_and(from_prev, n == 0)
            lo = _low_head_lanes()
            dups = {"kc": [], "kp": [], "vc": [], "vp": []}
            products = []
            for h in range(NKV):
                qs = _rows2(q_ref, 256 * h)
                dos = _rows2(do_ref, 256 * h)
                products.append((qs, dos, _dot_nt(qs, _rows2(kc_ref, 256 * h)), _dot_nt(qs, _rows2(kp_ref, 256 * h)),
                                 _dot_nt(dos, _rows2(vc_ref, 256 * h)), _dot_nt(dos, _rows2(vp_ref, 256 * h))))
            for h in range(NKV):
                c = 256 * h
                qs, dos, sc, sp, dpc, dpp = products[h]
                kstack = jnp.concatenate([kp_ref[:, c:c + 128], kc_ref[:, c:c + 128],
                                          kp_ref[:, c + 128:c + 256], kc_ref[:, c + 128:c + 256]], axis=0)
                p_c, p_p, ds_c, ds_p = [], [], [], []
                for pr in range(2):
                    cc = c + 128 * pr
                    prod = do_ref[:, cc:cc + 128].astype(f32) * o_ref[:, cc:cc + 128].astype(f32)
                    d_lo = jnp.sum(jnp.where(lo, prod, 0.0), axis=-1, keepdims=True)
                    d_hi = jnp.sum(prod, axis=-1, keepdims=True) - d_lo
                    row_pc, row_pp, row_dc, row_dp = [], [], [], []
                    for e in range(2):
                        hq = 4 * h + 2 * pr + e
                        rows, cols = slice(128 * pr, 128 * pr + 128), slice(128 * e, 128 * e + 128)
                        delta = d_lo if e == 0 else d_hi
                        lse = lse_ref[:, hq:hq + 1]
                        s = jnp.where(from_prev, sp[rows, cols], sc[rows, cols]) + b_ref[hq]
                        p = jnp.where(no_key, 0.0, jnp.exp(s - lse))
                        ds = p * (jnp.where(from_prev, dpp[rows, cols], dpc[rows, cols]) - delta)
                        dsa_ref[hq] += ds
                        dsk_ref[hq] += jnp.broadcast_to(-jnp.sum(jnp.exp(s_ref[0, hq] - lse) * delta), (8, 128))
                        row_pc.append(jnp.where(from_prev, 0.0, p).astype(bf16))
                        row_pp.append(jnp.where(from_prev, p, 0.0).astype(bf16))
                        row_dc.append(jnp.where(from_prev, 0.0, ds).astype(bf16))
                        row_dp.append(jnp.where(from_prev, ds, 0.0).astype(bf16))
                    dq_ref[:, cc:cc + 128] = _dot(jnp.concatenate([row_dp[0], row_dc[0], row_dp[1], row_dc[1]], axis=1), kstack)
                    p_c.append(jnp.concatenate(row_pc, axis=1))
                    p_p.append(jnp.concatenate(row_pp, axis=1))
                    ds_c.append(jnp.concatenate(row_dc, axis=1))
                    ds_p.append(jnp.concatenate(row_dp, axis=1))

                def to_keys(m2, rhs):
                    x2 = _dot_tn(jnp.concatenate(m2, axis=0), rhs)
                    x = jnp.where(lo, x2[0:128], x2[128:256])
                    return x + pltpu.roll(x, HD, 1)

                dups["kc"].append(to_keys(ds_c, qs))
                dups["kp"].append(to_keys(ds_p, qs))
                dups["vc"].append(to_keys(p_c, dos))
                dups["vp"].append(to_keys(p_p, dos))
            for key, ref in (("kc", dkc_ref), ("kp", dkp_ref), ("vc", dvc_ref), ("vp", dvp_ref)):
                d = dups[key]
                ref[:, 0:128] = jnp.where(lo, d[0], d[1])
                ref[:, 128:256] = jnp.where(lo, d[2], d[3])

    clamp = lambda n: jnp.minimum(n, nb - 1)
    blk = lambda f: pl.BlockSpec((BLK, D), f)
    cur = lambda n: (clamp(n), 0)
    prev = lambda n: (jnp.maximum(clamp(n) - 1, 0), 0)
    back = lambda n: (jnp.maximum(n - 1, 0), 0)
    kvb = lambda f: pl.BlockSpec((BLK, NKV * HD), f)
    return _call(
        body, (sinks, qn, kk, kk, vv, vv, bias, o, do, lse), grid=(nb + 1,),
        in_specs=[pl.BlockSpec(memory_space=pltpu.SMEM), blk(cur), blk(cur), blk(prev), blk(cur), blk(prev),
                  pl.BlockSpec((NQ, BLK, BLK), lambda n: (0, 0, 0)), blk(cur), blk(cur),
                  pl.BlockSpec((BLK, NQ), cur)],
        out_specs=[blk(cur), kvb(cur), kvb(back), kvb(cur), kvb(back),
                   pl.BlockSpec((NQ, 8, 128), lambda n: (0, 0, 0)),
                   pl.BlockSpec((NQ, BLK, BLK), lambda n: (0, 0, 0))],
        out_shape=[SDS((T, D), f32)] + [SDS((T, NKV * HD), f32)] * 4 + [SDS((NQ, 8, 128), f32), SDS((NQ, BLK, BLK), f32)],
        name="attn_bwd", sem=("arbitrary",), vmem_mb=40, comm=comm)


def _bias_bwd(dsa):
    def body(bk_ref, ds_ref, out_ref):
        bk = bk_ref[...]
        lane = lax.broadcasted_iota(jnp.int32, (1, 128), 1)
        for h in range(NQ):
            ds = ds_ref[h]
            row = jnp.zeros((1, 128), f32)
            for b in range(NBUCKET):
                row = jnp.where(lane == b, jnp.sum(jnp.where(bk == b, ds, 0.0)), row)
            out_ref[h:h + 1, :] = row

    return pl.pallas_call(body, out_shape=SDS((NQ, 128), f32), name="bias_bwd")(jnp.asarray(_bucket_tile()), dsa)


def _qkv_bwd(proj, gq2, gk2, dqn, dkc, dkp, dvc, dvp):
    T = proj.shape[0]
    tm = 512

    def body(q_ref, kv_ref, gq_ref, gk_ref, dq_ref, dkc_ref, dkp_ref, dvc_ref, dvp_ref,
             oq_ref, okv_ref, ggq_ref, ggk_ref):
        @pl.when(pl.program_id(0) == 0)
        def _():
            ggq_ref[...] = jnp.zeros_like(ggq_ref)
            ggk_ref[...] = jnp.zeros_like(ggk_ref)

        bd = _head_blockdiag()

        def norm_bwd(z, dy, g, scale):
            r = lax.rsqrt(_head_sums(z * z, bd) * (1.0 / HD) + EPS)
            gd = dy * g * scale
            dz = r * gd - z * (r * r * r) * _head_sums(z * gd, bd) * (1.0 / HD)
            return dz, jnp.sum(dy * scale * z * r, axis=0, keepdims=True)

        gq = jnp.zeros((1, 128), f32)
        for p in range(NQ // 2):
            ln = slice(128 * p, 128 * p + 128)
            dz, dg = norm_bwd(q_ref[:, ln], dq_ref[:, ln], gq_ref[...], HD ** -0.5)
            oq_ref[:, ln] = dz.astype(bf16)
            gq = gq + dg
        ggq_ref[...] += gq + pltpu.roll(gq, HD, 1)
        gk = jnp.zeros((1, 128), f32)
        for p in range(NKV // 2):
            ln = slice(128 * p, 128 * p + 128)
            dz, dg = norm_bwd(kv_ref[:, ln], dkc_ref[:, ln] + dkp_ref[:, ln], gk_ref[...], 1.0)
            okv_ref[:, ln] = dz.astype(bf16)
            gk = gk + dg
        ggk_ref[...] += gk + pltpu.roll(gk, HD, 1)
        okv_ref[:, 256:512] = (dvc_ref[...] + dvp_ref[...]).astype(bf16)

    vec = pl.BlockSpec((1, 128), lambda i: (0, 0))
    kvb = pl.BlockSpec((tm, NKV * HD), lambda i: (i, 0))
    return pl.pallas_call(
        body, grid=(T // tm,),
        in_specs=[pl.BlockSpec((tm, D), lambda i: (i, C_Q)), pl.BlockSpec((tm, 512), lambda i: (i, C_KV)), vec, vec,
                  pl.BlockSpec((tm, D), lambda i: (i, 0)), kvb, kvb, kvb, kvb],
        out_specs=[pl.BlockSpec((tm, D), lambda i: (i, 0)), pl.BlockSpec((tm, 512), lambda i: (i, 0)), vec, vec],
        out_shape=[SDS((T, D), bf16), SDS((T, 512), bf16), SDS((1, 128), f32), SDS((1, 128), f32)],
        name="qkv_bwd", compiler_params=_cparams(("arbitrary",), 32))(proj, proj, gq2, gk2, dqn, dkc, dkp, dvc, dvp)


def _inproj_bwd(pieces, w_in, x, dx1, g_mix, comm=None):
    T = x.shape[0]
    tm = 512
    widths = [p.shape[1] for p in pieces]
    offs = [sum(widths[:i]) for i in range(len(widths))]
    assert sum(widths) == INW

    def body(*refs):
        p_refs, (w_ref, x_ref, dx1_ref, g_ref, dx_ref, dg_ref) = refs[:len(pieces)], refs[len(pieces):]

        @pl.when(pl.program_id(0) == 0)
        def _():
            dg_ref[...] = jnp.zeros_like(dg_ref)

        du = None
        for p_ref, off, wd in zip(p_refs, offs, widths):
            part = _dot_nt(p_ref[...], w_ref[:, off:off + wd])
            du = part if du is None else du + part
        dx, dg = _rms_bwd(x_ref[...], g_ref[...], du, dx1_ref[...])
        dx_ref[...] = dx
        dg_ref[...] += dg

    row = pl.BlockSpec((tm, D), lambda i: (i, 0))
    vec = pl.BlockSpec((1, D), lambda i: (0, 0))
    return _call(
        body, (*pieces, w_in, x, dx1, g_mix), grid=(T // tm,),
        in_specs=[pl.BlockSpec((tm, wd), lambda i: (i, 0)) for wd in widths]
        + [pl.BlockSpec((D, INW), lambda i: (0, 0), pipeline_mode=pl.Buffered(1)), row, row, vec],
        out_specs=[row, vec],
        out_shape=[SDS((T, D), f32), SDS((1, D), f32)],
        name="inproj_bwd", sem=("arbitrary",), vmem_mb=48, comm=comm)


def _to_internal_cols(w):
    return jnp.concatenate([w[..., 0:1024], w[..., 1536:INW], w[..., 1024:1536]], axis=-1)


def _forward_backward(x, tgt, w, placed, chip_core):
    def sums(names, grads, got):
        res = [_pair_sum(nm, grads[nm], got_nm, chip_core) for nm, got_nm in zip(names, got)]
        return {nm: r[0] for nm, r in zip(names, res)}, {nm: r[1] for nm, r in zip(names, res)}

    first = ["w_in", "w_dw"]
    w_in, w_dw = _run_comm(_gather_comm({nm: placed[nm] for nm in first}), "gather_first")
    w_in = _to_internal_cols(w_in)
    gq2 = jnp.tile(w["q_norm_g"], (1, 2))
    gk2 = jnp.tile(w["k_norm_g"], (1, 2))
    def gathered_in(names):
        return names, _gather_comm({nm: placed[nm] for nm in names})

    full = {}
    perm, perm_t = _block_perm()
    names, comm = gathered_in(["w_out", "w_attn_o", "w_conv_out"])
    proj, u, tail, got = _rms_inproj(x, w["norm_mix_g"], w_in, perm, comm=comm)
    full.update(zip(names, got))
    qn, kk, vv = _qk_prep(proj, gq2, gk2)
    bias = _bias_tiles(w["rel_bias"])
    names, comm = gathered_in(["w_ff1"])
    (o, lse), got = _attn_fwd(qn, kk, vv, bias, w["attn_sinks"], comm=comm)
    full.update(zip(names, got))
    names, comm = gathered_in(["w_ff2"])
    h1, h3, got = _glu_conv_fwd(proj, tail, w_dw, w["b_dw"], w["conv_ln_g"], w["conv_ln_b"], perm_t, comm=comm)
    full.update(zip(names, got))
    attn, conv, merged, x1, n2 = _mix_out(o, h3, proj, x, full["w_attn_o"], full["w_conv_out"], full["w_out"],
                                          w["norm_mlp_g"])
    hmid, dy, dyb, loss = _mlp_fwd(n2, full["w_ff1"], full["w_ff2"], x1, tgt)

    g, cp, own = {}, {}, {}
    df1, dx1, dx1b, g["norm_mlp_g"] = _mlp_bwd(dy, dyb, hmid, full["w_ff1"], full["w_ff2"], x1, w["norm_mlp_g"])
    ff = ["w_ff1", "w_ff2"]
    gff = {"w_ff2": _wgrad(hmid, dyb, "wgrad_ff2"), "w_ff1": _wgrad(n2, df1, "wgrad_ff1")}
    (dat, dcv, do, dh3, dga, dgc), got = _mix_bwd(dx1b, proj, attn, conv, full["w_attn_o"], full["w_conv_out"],
                                                  full["w_out"], perm, comm=_pair_exchange_comm(gff, ff))
    cp_ff, own_ff = sums(ff, gff, got)
    sq = ["w_out", "w_attn_o", "w_conv_out"]
    gsq = {"w_out": _wgrad(merged, dx1b, "wgrad_out"), "w_attn_o": _wgrad(o, dat, "wgrad_attn_o"),
           "w_conv_out": _wgrad(h3, dcv, "wgrad_conv_out")}
    (dh1, lnacc, head), got = _conv_ln_bwd(dh3, h1, w["conv_ln_g"], w["conv_ln_b"], comm=_pair_exchange_comm(gsq, sq))
    cp_sq, own_sq = sums(sq, gsq, got)
    cp, own = {**cp_ff, **cp_sq}, {**own_ff, **own_sq}
    g["conv_ln_g"], g["conv_ln_b"], g["b_dw"] = lnacc[0:1], lnacc[1:2], lnacc[2:3]
    five = ff + sq
    (da, dg, g["w_dw"]), rc = _conv_bwd(dh1, head, proj, tail, w_dw, perm_t, comm=_chip_exchange_comm(cp, five))
    tot = {nm: _chip_sum(nm, own[nm], rc_nm, chip_core) for nm, rc_nm in zip(five, rc)}
    (dqn, dkc, dkp, dvc, dvp, dsk, dsa), shards = _attn_bwd(qn, kk, vv, bias, w["attn_sinks"], o, do, lse,
                                                            comm=_pair_share_comm(tot, five))
    shards = dict(zip(five, shards))
    g["attn_sinks"] = dsk[:, 0, 0].reshape(1, NQ)
    g["rel_bias"] = _bias_bwd(dsa)[:, 0:NBUCKET].T
    dq, dkv, ggq, ggk = _qkv_bwd(proj, gq2, gk2, dqn, dkc, dkp, dvc, dvp)
    g["q_norm_g"], g["k_norm_g"] = ggq[:, 0:HD], ggk[:, 0:HD]
    pieces = [dq, da, dg, dga, dgc, dkv]
    names = ["q", "a", "g", "ga", "gc", "kv"]
    gw = {nm: _wgrad(u, p, "wgrad_in_" + nm, tn=p.shape[1] if p.shape[1] < 1024 else 1024) for nm, p in zip(names, pieces)}
    gin = {"w_in": jnp.concatenate([gw["q"], gw["kv"], gw["a"], gw["g"], gw["ga"], gw["gc"]], axis=1)}
    got = _run_comm(_pair_exchange_comm(gin, ["w_in"]), "rs_pair_exchange_in")
    cp_in, own_in = sums(["w_in"], gin, got)
    (grad_x, g["norm_mix_g"]), rc = _inproj_bwd(pieces, w_in, x, dx1, w["norm_mix_g"],
                                                comm=_chip_exchange_comm(cp_in, ["w_in"]))
    tot = {"w_in": _chip_sum("w_in", own_in["w_in"], rc[0], chip_core)}
    shards["w_in"] = _run_comm(_pair_share_comm(tot, ["w_in"]), "rs_pair_share_in")[0]
    return loss[0, 0], grad_x, g, shards


BIG = ["w_in", "w_attn_o", "w_conv_out", "w_out", "w_ff1", "w_ff2"]
SHARD_AXIS = {"w_in": 1, "w_attn_o": 0, "w_conv_out": 0, "w_out": 0, "w_ff1": 1, "w_ff2": 0, "w_dw": 1}
SHARD_SHAPE = {"w_in": (D, INW // 4), "w_attn_o": (D // 4, D), "w_conv_out": (D // 4, D), "w_out": (D // 4, D),
               "w_ff1": (D, DFF // 4), "w_ff2": (DFF // 4, D), "w_dw": (HALO, D // 4)}


def _position():
    x, y, c = lax.axis_index("x"), lax.axis_index("y"), lax.axis_index("c")
    other_chips = [(1 - x, y), (x, 1 - y), (1 - x, 1 - y)]
    return x, y, c, 2 * x + y, other_chips


def _shard_window(name, full_ref, s, half=None):
    R, C = SHARD_SHAPE[name]
    r0, nr = (0, R) if half is None else (half * (R // 2), R // 2)
    if SHARD_AXIS[name] == 1:
        return full_ref.at[pl.ds(r0, nr), pl.ds(s * C, C)]
    return full_ref.at[pl.ds(s * R + r0, nr), :]


def _remote(src, dst, send_sems, recv_sems, k, device):
    return pltpu.make_async_remote_copy(src_ref=src, dst_ref=dst, send_sem=send_sems.at[k], recv_sem=recv_sems.at[k],
                                        device_id=device, device_id_type=MESH)


def _full_shape(nm):
    R, C = SHARD_SHAPE[nm]
    return (R, 4 * C) if SHARD_AXIS[nm] == 1 else (4 * R, C)


def _place_shard(nm, shard, chip_arr, dtype):
    R, C = SHARD_SHAPE[nm]
    tr = min(R, 256)
    if SHARD_AXIS[nm] == 1:
        o_map = lambda i, ch: (i, ch[0])
    else:
        o_map = lambda i, ch: (ch[0] * (R // tr) + i, 0)

    def body(ch_ref, s_ref, o_ref):
        o_ref[...] = s_ref[...].astype(dtype)

    return pl.pallas_call(
        body,
        grid_spec=pltpu.PrefetchScalarGridSpec(
            num_scalar_prefetch=1, grid=(R // tr,),
            in_specs=[pl.BlockSpec((tr, C), lambda i, ch: (i, 0))], out_specs=pl.BlockSpec((tr, C), o_map)),
        out_shape=SDS(_full_shape(nm), dtype), name="place_" + nm,
        compiler_params=_cparams(("parallel",), 32))(chip_arr, shard)


def _gather_comm(placed):
    names = list(placed)
    n = len(names)

    def copies(cout, send, recv):
        x, y, c, chip, chips = _position()
        for a, nm in enumerate(names):
            for j, (cx, cy) in enumerate(chips):
                def ici(s, a=a, nm=nm, j=j, cx=cx, cy=cy):
                    w = _shard_window(nm, cout[a], s, c)
                    return _remote(w, w, send, recv, 6 * a + j, (cx, cy, c))

                def d2d(h, a=a, nm=nm, j=j, cx=cx, cy=cy):
                    w = _shard_window(nm, cout[a], 2 * cx + cy, h)
                    return _remote(w, w, send, recv, 6 * a + 3 + j, (x, y, 1 - c))

                yield ici, d2d, chip, 2 * cx + cy, c

    def start(cin, cout, send, recv):
        for ici, d2d, chip, s, c in copies(cout, send, recv):
            ici(chip).start()

    def mid(cin, cout, send, recv):
        for ici, d2d, chip, s, c in copies(cout, send, recv):
            ici(s).wait_recv()
            d2d(c).start()

    def finish(cin, cout, send, recv):
        for ici, d2d, chip, s, c in copies(cout, send, recv):
            d2d(1 - c).wait_recv()
        for ici, d2d, chip, s, c in copies(cout, send, recv):
            ici(chip).wait_send()
            d2d(c).wait_send()

    return _Comm([placed[nm] for nm in names], [SDS(placed[nm].shape, placed[nm].dtype) for nm in names], 6 * n,
                 start, finish, mid, aliases={a: a for a in range(n)})


def _half_rows(nm):
    return SHARD_SHAPE[nm][0] // 2


RS_TILE = 128


def _exchange_comm(ins, out_shapes, copies, n_sems, aliases=None):
    def start(cin, cout, send, recv):
        for cp in copies(cin, cout, send, recv):
            cp.start()

    def finish(cin, cout, send, recv):
        for cp in copies(cin, cout, send, recv):
            cp.wait()

    return _Comm(ins, out_shapes, n_sems, start, finish, aliases=aliases)


def _pair_exchange_comm(grads, names):
    def copies(cin, cout, send, recv):
        x, y, c, chip, chips = _position()
        return [_remote(_shard_window(nm, cin[a], s, 1 - c), cout[a].at[s], send, recv, 4 * a + s, (x, y, 1 - c))
                for a, nm in enumerate(names) for s in range(4)]

    return _exchange_comm([grads[nm] for nm in names],
                          [SDS((4, _half_rows(nm), SHARD_SHAPE[nm][1]), f32) for nm in names], copies, 4 * len(names))


def _pair_sum(nm, g, got, chip_core):
    R, C = SHARD_SHAPE[nm]
    hr = R // 2
    nt = hr // RS_TILE
    if SHARD_AXIS[nm] == 1:
        g_map = lambda i, s, sc: (sc[1] * nt + i, s)
    else:
        g_map = lambda i, s, sc: (s * (R // RS_TILE) + sc[1] * nt + i, 0)

    def body(sc_ref, g_ref, got_ref, o16_ref, own_ref):
        v = g_ref[...] + got_ref[0]
        o16_ref[0] = v.astype(bf16)

        @pl.when(pl.program_id(1) == sc_ref[0])
        def _():
            own_ref[...] = v

    blk3 = pl.BlockSpec((1, RS_TILE, C), lambda i, s, sc: (s, i, 0))
    return pl.pallas_call(
        body,
        grid_spec=pltpu.PrefetchScalarGridSpec(
            num_scalar_prefetch=1, grid=(nt, 4),
            in_specs=[pl.BlockSpec((RS_TILE, C), g_map), blk3],
            out_specs=[blk3, pl.BlockSpec((RS_TILE, C), lambda i, s, sc: (i, 0))]),
        out_shape=[SDS((4, hr, C), bf16), SDS((hr, C), f32)], name="rs_pair_sum_" + nm,
        compiler_params=_cparams(("parallel", "arbitrary"), 32))(chip_core, g, got)


def _chip_exchange_comm(cp, names):
    def copies(cin, cout, send, recv):
        x, y, c, chip, chips = _position()
        return [_remote(cin[a].at[2 * cx + cy], cout[a].at[j], send, recv, 3 * a + j, (cx, cy, c))
                for a, nm in enumerate(names) for j, (cx, cy) in enumerate(chips)]

    return _exchange_comm([cp[nm] for nm in names],
                          [SDS((3, _half_rows(nm), SHARD_SHAPE[nm][1]), bf16) for nm in names], copies, 3 * len(names))


def _chip_sum(nm, own, rc, chip_core):
    R, C = SHARD_SHAPE[nm]
    nt = (R // 2) // RS_TILE

    def body(sc_ref, own_ref, rc_ref, o_ref):
        o_ref[...] = own_ref[...] + rc_ref[0].astype(f32) + rc_ref[1].astype(f32) + rc_ref[2].astype(f32)

    return pl.pallas_call(
        body,
        grid_spec=pltpu.PrefetchScalarGridSpec(
            num_scalar_prefetch=1, grid=(nt,),
            in_specs=[pl.BlockSpec((RS_TILE, C), lambda i, sc: (i, 0)),
                      pl.BlockSpec((3, RS_TILE, C), lambda i, sc: (0, i, 0))],
            out_specs=pl.BlockSpec((RS_TILE, C), lambda i, sc: (sc[1] * nt + i, 0))),
        out_shape=SDS((R, C), f32), name="rs_chip_sum_" + nm,
        compiler_params=_cparams(("parallel",), 32))(chip_core, own, rc)


def _pair_share_comm(tot, names):
    def copies(cin, cout, send, recv):
        x, y, c, chip, chips = _position()
        cps = []
        for a, nm in enumerate(names):
            hr = _half_rows(nm)
            mine = cout[a].at[pl.ds(c * hr, hr), :]
            cps.append(_remote(mine, mine, send, recv, a, (x, y, 1 - c)))
        return cps

    return _exchange_comm([tot[nm] for nm in names], [SDS(SHARD_SHAPE[nm], f32) for nm in names], copies, len(names),
                          aliases={a: a for a in range(len(names))})


SMALL_ROWS = 40


def _allreduce_small(block):
    def body(x_ref, out_ref, buf, send_sems, recv_sems, local_sem):
        x, y, c, chip, chips = _position()
        me, sibling = (x, y, c), (x, y, 1 - c)

        def slot(px, py, pc):
            return buf.at[4 * px + 2 * py + pc]

        def copy(k, block_of, to, src=None):
            return _remote(slot(*block_of) if src is None else src, slot(*block_of), send_sems, recv_sems, k, to)

        mine = pltpu.make_async_copy(x_ref, slot(*me), local_sem)
        mine.start()
        first = [copy(0, me, sibling, src=x_ref)] + [copy(1 + j, me, (*ch, c), src=x_ref) for j, ch in enumerate(chips)]
        for cp in first:
            cp.start()
        passed = [copy(4 + j, (*ch, c), sibling) for j, ch in enumerate(chips)]
        for j, ch in enumerate(chips):
            copy(1 + j, (*ch, c), me).wait_recv()
            passed[j].start()
        copy(0, sibling, me).wait_recv()
        for j, ch in enumerate(chips):
            copy(4 + j, (*ch, 1 - c), me).wait_recv()
        for cp in first + passed:
            cp.wait_send()
        mine.wait()
        acc = buf[0]
        for d in range(1, 8):
            acc = acc + buf[d]
        out_ref[...] = acc

    vm = pl.BlockSpec(memory_space=pltpu.VMEM)
    return pl.pallas_call(
        body, in_specs=[vm], out_specs=vm, out_shape=SDS((SMALL_ROWS, D), f32),
        scratch_shapes=[pltpu.VMEM((8, SMALL_ROWS, D), f32), pltpu.SemaphoreType.DMA((7,)), pltpu.SemaphoreType.DMA((7,)),
                        pltpu.SemaphoreType.DMA],
        name="allreduce_small")(block)


def _adamw(w, g, m, v, name):
    rows, cols = w.shape
    tr = 256 if rows % 256 == 0 else rows

    def body(w_ref, g_ref, m_ref, v_ref, d_ref, nm_ref, nv_ref):
        gv = g_ref[...]
        m2 = ADAM_B1 * m_ref[...] + (1.0 - ADAM_B1) * gv
        v2 = ADAM_B2 * v_ref[...] + (1.0 - ADAM_B2) * jnp.square(gv)
        m_hat = m2 / (1.0 - ADAM_B1 ** ADAM_STEP)
        v_hat = v2 / (1.0 - ADAM_B2 ** ADAM_STEP)
        d_ref[...] = -ADAM_LR * (m_hat / (jnp.sqrt(v_hat) + ADAM_EPS) + ADAM_WD * w_ref[...])
        nm_ref[...] = m2
        nv_ref[...] = v2

    spec = pl.BlockSpec((tr, cols), lambda i: (i, 0))
    return pl.pallas_call(body, grid=(rows // tr,), in_specs=[spec] * 4, out_specs=[spec] * 3,
                          out_shape=[SDS((rows, cols), f32)] * 3, name=name,
                          compiler_params=_cparams(("parallel",), 40))(w, g, m, v)


WEIGHTS = ["norm_mix_g", "w_in", "q_norm_g", "k_norm_g", "attn_sinks", "rel_bias", "w_attn_o", "w_dw", "b_dw",
           "conv_ln_g", "conv_ln_b", "w_conv_out", "w_out", "norm_mlp_g", "w_ff1", "w_ff2"]
ROW_VECS = ["norm_mix_g", "b_dw", "conv_ln_g", "conv_ln_b", "norm_mlp_g"]
MISC_ROW = 5
W_DW_ROW = 8


def _pack_small(vals, loss=None):
    misc = [vals["q_norm_g"].reshape(1, HD), vals["k_norm_g"].reshape(1, HD), vals["attn_sinks"].reshape(1, NQ),
            jnp.zeros((1, 1), f32) if loss is None else loss.reshape(1, 1), jnp.zeros((1, 111), f32),
            vals["rel_bias"].reshape(1, NBUCKET * NQ), jnp.zeros((1, 256), f32)]
    rows = [vals[nm].reshape(1, D) for nm in ROW_VECS] + [jnp.concatenate(misc, axis=1), jnp.zeros((2, D), f32)]
    return jnp.concatenate(rows, axis=0)


def _unpack_small(block):
    out = {nm: block[i:i + 1] for i, nm in enumerate(ROW_VECS)}
    misc = block[MISC_ROW]
    out["q_norm_g"] = misc[0:64].reshape(1, HD)
    out["k_norm_g"] = misc[64:128].reshape(1, HD)
    out["attn_sinks"] = misc[128:144].reshape(1, NQ)
    out["rel_bias"] = misc[256:768].reshape(NBUCKET, NQ)
    return out, misc[144]


def kernel(x, norm_mix_g, w_in, q_norm_g, k_norm_g, attn_sinks, rel_bias, w_attn_o, w_dw, b_dw, conv_ln_g, conv_ln_b, w_conv_out, w_out, norm_mlp_g, w_ff1, w_ff2, loss_target, m_norm_mix_g, m_w_in, m_q_norm_g, m_k_norm_g, m_attn_sinks, m_rel_bias, m_w_attn_o, m_w_dw, m_b_dw, m_conv_ln_g, m_conv_ln_b, m_w_conv_out, m_w_out, m_norm_mlp_g, m_w_ff1, m_w_ff2, v_norm_mix_g, v_w_in, v_q_norm_g, v_k_norm_g, v_attn_sinks, v_rel_bias, v_w_attn_o, v_w_dw, v_b_dw, v_conv_ln_g, v_conv_ln_b, v_w_conv_out, v_w_out, v_norm_mlp_g, v_w_ff1, v_w_ff2):
    args = dict(locals())
    wts = {nm: args[nm] for nm in WEIGHTS}
    mom = {nm: args["m_" + nm] for nm in WEIGHTS}
    var = {nm: args["v_" + nm] for nm in WEIGHTS}
    chip = 2 * lax.axis_index("x") + lax.axis_index("y")

    chip_arr = jnp.reshape(chip, (1,)).astype(jnp.int32)
    chip_core = jnp.stack([chip, lax.axis_index("c")]).astype(jnp.int32)
    placed = {nm: _place_shard(nm, wts[nm][0], chip_arr, bf16) for nm in BIG}
    placed["w_dw"] = _place_shard("w_dw", jnp.pad(w_dw[0], ((0, 1), (0, 0))), chip_arr, f32)

    loss_part, grad_x, g, shards = _forward_backward(x[0], loss_target[0], wts, placed, chip_core)

    small = jnp.concatenate([_pack_small(g, loss_part), g["w_dw"]], axis=0)
    small = _allreduce_small(small)
    grads, loss = _unpack_small(small)
    grads["w_dw"] = lax.dynamic_slice(small[W_DW_ROW:W_DW_ROW + CW], (0, chip * (D // 4)), (CW, D // 4))
    grads.update(shards)

    delta, new_m, new_v = {}, {}, {}
    sd, sm, sv = _adamw(_pack_small(wts), small[0:8], _pack_small(mom), _pack_small(var), "adamw_small")
    for res, blk in ((delta, sd), (new_m, sm), (new_v, sv)):
        res.update(_unpack_small(blk)[0])
    for nm in BIG + ["w_dw"]:
        shp = wts[nm].shape
        two_d = lambda a: a.reshape(shp[-2], shp[-1])
        delta[nm], new_m[nm], new_v[nm] = _adamw(two_d(wts[nm]), grads[nm], two_d(mom[nm]), two_d(var[nm]), "adamw_" + nm)

    def shaped(vals):
        return [vals[nm].reshape(wts[nm].shape) for nm in WEIGHTS]

    return (loss, grad_x[None], *shaped(grads), *shaped(delta), *shaped(new_m), *shaped(new_v))
```

```python
import functools

import numpy as np
import jax
import jax.numpy as jnp
from jax import lax
from jax.experimental import pallas as pl
from jax.experimental.pallas import tpu as pltpu

f32 = jnp.float32
bf16 = jnp.bfloat16
SDS = jax.ShapeDtypeStruct
MESH = pl.DeviceIdType.MESH

D = 1024
HD = 64
NQ = 16
NKV = 4
BLK = 128
CW = 31
HALO = 32
DFF = 4096
NBUCKET = 32
EPS = 1e-6
NEG = -1e30
INW = 5632
MIX_CHUNK = 256
C_Q, C_A, C_G, C_GA, C_GC = 0, 1, 2, 3, 4
C_KV = 10

ADAM_LR = 0.001
ADAM_B1 = 0.9
ADAM_B2 = 0.999
ADAM_EPS = 1e-08
ADAM_WD = 0.01
ADAM_STEP = 10

VMEM_BYTES_V7X = 64 << 20


def _cparams(sem, vmem_mb):
    assert (vmem_mb << 20) < VMEM_BYTES_V7X
    return pltpu.CompilerParams(dimension_semantics=sem, vmem_limit_bytes=vmem_mb << 20)


ANY = pl.BlockSpec(memory_space=pl.ANY)


HBM_PIN_BYTES = 1 << 20


def _hbm(a):
    if a.ndim >= 2 and a.size * a.dtype.itemsize >= HBM_PIN_BYTES:
        return pltpu.with_memory_space_constraint(a, pltpu.HBM)
    return a


class _Comm:
    def __init__(self, ins, out_shapes, n_sems, start, finish, mid=None, aliases=None):
        self.ins, self.out_shapes, self.n_sems = list(ins), list(out_shapes), n_sems
        self.start, self.finish, self.mid, self.aliases = start, finish, mid, dict(aliases or {})


def _call(body, args, *, grid, in_specs, out_specs, out_shape, name, sem, vmem_mb, scratch_shapes=(), comm=None,
          mid_step=None):
    n_in, n_out, n_scr = len(in_specs), len(out_specs), len(scratch_shapes)
    args = [_hbm(a) for a in args]
    if comm is None:
        outs = pl.pallas_call(body, grid=grid, in_specs=list(in_specs), out_specs=list(out_specs),
                              out_shape=list(out_shape), scratch_shapes=list(scratch_shapes), name=name,
                              compiler_params=_cparams(sem, vmem_mb))(*args)
        return list(outs), []
    ci, co = len(comm.ins), len(comm.out_shapes)
    last = grid[0] - 1

    def wrapped(*refs):
        ins, cin = refs[:n_in], refs[n_in:n_in + ci]
        outs = refs[n_in + ci:n_in + ci + n_out]
        cout = refs[n_in + ci + n_out:n_in + ci + n_out + co]
        scr = refs[n_in + ci + n_out + co:]
        send, recv = scr[n_scr], scr[n_scr + 1]
        step = pl.program_id(0)

        @pl.when(step == 0)
        def _():
            comm.start(cin, cout, send, recv)

        body(*ins, *outs, *scr[:n_scr])
        if comm.mid is not None:
            @pl.when(step == mid_step)
            def _():
                comm.mid(cin, cout, send, recv)

        @pl.when(step == last)
        def _():
            comm.finish(cin, cout, send, recv)

    res = pl.pallas_call(
        wrapped, grid=grid, in_specs=list(in_specs) + [ANY] * ci, out_specs=list(out_specs) + [ANY] * co,
        out_shape=list(out_shape) + comm.out_shapes,
        input_output_aliases={n_in + k: n_out + v for k, v in comm.aliases.items()},
        scratch_shapes=list(scratch_shapes) + [pltpu.SemaphoreType.DMA((comm.n_sems,))] * 2,
        name=name, compiler_params=_cparams(("arbitrary",), vmem_mb))(*args, *[_hbm(a) for a in comm.ins])
    return list(res[:n_out]), list(res[n_out:])


def _run_comm(comm, name):
    ci, co = len(comm.ins), len(comm.out_shapes)

    def body(*refs):
        cin, cout, (send, recv) = refs[:ci], refs[ci:ci + co], refs[ci + co:]
        comm.start(cin, cout, send, recv)
        if comm.mid is not None:
            comm.mid(cin, cout, send, recv)
        comm.finish(cin, cout, send, recv)

    return pl.pallas_call(
        body, in_specs=[ANY] * ci, out_specs=[ANY] * co, out_shape=comm.out_shapes, input_output_aliases=comm.aliases,
        scratch_shapes=[pltpu.SemaphoreType.DMA((comm.n_sems,))] * 2, name=name)(*comm.ins)


def _dot(a, b):
    return jnp.dot(a, b, preferred_element_type=f32)


def _dot_nt(a, b):
    return lax.dot_general(a, b, (((1,), (1,)), ((), ())), preferred_element_type=f32)


def _dot_tn(a, b):
    return lax.dot_general(a, b, (((0,), (0,)), ((), ())), preferred_element_type=f32)


def _sigmoid(x):
    return 1.0 / (1.0 + jnp.exp(-x))


def _low_head_lanes():
    return lax.broadcasted_iota(jnp.int32, (1, 2 * HD), 1) < HD


def _head_blockdiag():
    r = lax.broadcasted_iota(jnp.int32, (2 * HD, 2 * HD), 0) // HD
    c = lax.broadcasted_iota(jnp.int32, (2 * HD, 2 * HD), 1) // HD
    return jnp.where(r == c, 1.0, 0.0).astype(bf16)


def _head_sums(z, bd):
    hi = z.astype(bf16)
    lo = (z - hi.astype(f32)).astype(bf16)
    return _dot(hi, bd) + _dot(lo, bd)


def _weight_cols(start, width):
    kv_width = 2 * NKV * HD
    if start < D:
        orig = start
    elif start < INW - kv_width:
        orig = start + kv_width
    else:
        orig = start - (INW - kv_width) + D
    assert (start < D) == (start + width <= D) and (start < INW - kv_width) == (start + width <= INW - kv_width)
    return slice(orig, orig + width)


def _rms_inproj(x, g, w, perm, comm=None):
    T, N = x.shape[0], w.shape[1]
    tn = 512
    conv_cols = (C_A * D, (C_G + 1) * D)

    def body(x_ref, g_ref, w_ref, perm_ref, p_ref, u_ref, tail_ref):
        xv = x_ref[...]
        r = lax.rsqrt(jnp.mean(xv * xv, axis=-1, keepdims=True) + EPS)
        u = (xv * r * g_ref[...]).astype(bf16)
        u_ref[...] = u
        u_blocks = _dot(perm_ref[...], u).astype(bf16)
        for c in range(N // tn):
            w_cols = _weight_cols(c * tn, tn)
            lhs = u_blocks if conv_cols[0] <= c * tn < conv_cols[1] else u
            p_ref[:, c * tn:(c + 1) * tn] = _dot(lhs, w_ref[:, w_cols])
        tail_ref[...] = _dot(u[TT - NBLK:TT], w_ref[:, _weight_cols(conv_cols[0], conv_cols[1] - conv_cols[0])])

    once = pl.Buffered(1)
    (proj, u, tail), got = _call(
        body, (x, g, w, perm), grid=(T // TT,),
        in_specs=[pl.BlockSpec((TT, D), lambda i: (i, 0)),
                  pl.BlockSpec((1, D), lambda i: (0, 0)),
                  pl.BlockSpec((D, N), lambda i: (0, 0), pipeline_mode=once),
                  pl.BlockSpec((TT, TT), lambda i: (0, 0), pipeline_mode=once)],
        out_specs=[pl.BlockSpec((TT, N), lambda i: (i, 0)),
                   pl.BlockSpec((TT, D), lambda i: (i, 0)),
                   pl.BlockSpec((NBLK, 2 * D), lambda i: (i, 0))],
        out_shape=[SDS((T, N), f32), SDS((T, D), bf16), SDS((T // TT * NBLK, 2 * D), f32)],
        name="rms_inproj", sem=("parallel",), vmem_mb=48, comm=comm, mid_step=(3 * (T // TT)) // 4)
    return proj, u, tail, got


def _split_pair(pair, out_ref, p, lo):
    rolled = pltpu.roll(pair, HD, 1)
    zero = jnp.zeros_like(pair)
    c = 512 * p
    out_ref[:, c:c + 128] = jnp.where(lo, pair, zero).astype(bf16)
    out_ref[:, c + 128:c + 256] = jnp.where(lo, zero, rolled).astype(bf16)
    out_ref[:, c + 256:c + 384] = jnp.where(lo, rolled, zero).astype(bf16)
    out_ref[:, c + 384:c + 512] = jnp.where(lo, zero, pair).astype(bf16)


def _qk_prep(proj, gq2, gk2):
    T = proj.shape[0]
    tm = 512

    def body(q_ref, kv_ref, gq_ref, gk_ref, qn_ref, kk_ref, vv_ref):
        bd = _head_blockdiag()
        lo = _low_head_lanes()
        for p in range(NQ // 2):
            z = q_ref[:, 128 * p:128 * p + 128]
            r = lax.rsqrt(_head_sums(z * z, bd) * (1.0 / HD) + EPS)
            qn_ref[:, 128 * p:128 * p + 128] = (z * r * gq_ref[...] * (HD ** -0.5)).astype(bf16)
        for p in range(NKV // 2):
            z = kv_ref[:, 128 * p:128 * p + 128]
            r = lax.rsqrt(_head_sums(z * z, bd) * (1.0 / HD) + EPS)
            _split_pair(z * r * gk_ref[...], kk_ref, p, lo)
            _split_pair(kv_ref[:, 256 + 128 * p:256 + 128 * p + 128], vv_ref, p, lo)

    return pl.pallas_call(
        body, grid=(T // tm,),
        in_specs=[pl.BlockSpec((tm, D), lambda i: (i, C_Q)),
                  pl.BlockSpec((tm, 512), lambda i: (i, C_KV)),
                  pl.BlockSpec((1, 128), lambda i: (0, 0)),
                  pl.BlockSpec((1, 128), lambda i: (0, 0))],
        out_specs=[pl.BlockSpec((tm, D), lambda i: (i, 0))] * 3,
        out_shape=[SDS((T, D), bf16)] * 3,
        name="qk_prep", compiler_params=_cparams(("parallel",), 32))(proj, proj, gq2, gk2)


def _bucket_tile():
    qi = np.arange(BLK)[:, None]
    kj = np.arange(BLK)[None, :]
    n = np.where(kj > qi, qi + BLK - kj, qi - kj)
    max_exact = NBUCKET // 2
    nf = np.maximum(n, 1).astype(np.float32)
    large = max_exact + (np.log(nf / max_exact) / np.float32(np.log(128 / max_exact))
                         * (NBUCKET - max_exact)).astype(np.int32)
    large = np.minimum(large, NBUCKET - 1)
    return np.where(n < max_exact, n, large).astype(np.int32)


def _from_prev_block():
    return lax.broadcasted_iota(jnp.int32, (BLK, BLK), 1) > lax.broadcasted_iota(jnp.int32, (BLK, BLK), 0)


def _bias_tiles(rel_bias):
    def body(rb_ref, bk_ref, out_ref):
        bk = bk_ref[...]
        for h in range(NQ):
            acc = jnp.zeros((BLK, BLK), f32)
            for b in range(NBUCKET):
                acc = jnp.where(bk == b, rb_ref[b, h], acc)
            out_ref[h] = acc

    return pl.pallas_call(
        body,
        in_specs=[pl.BlockSpec(memory_space=pltpu.SMEM), pl.BlockSpec(memory_space=pltpu.VMEM)],
        out_specs=pl.BlockSpec(memory_space=pltpu.VMEM),
        out_shape=SDS((NQ, BLK, BLK), f32),
        name="bias_tiles")(rel_bias, jnp.asarray(_bucket_tile()))


def _rows2(ref, c):
    return jnp.concatenate([ref[:, c:c + 128], ref[:, c + 128:c + 256]], axis=0)


def _attn_fwd(qn, kk, vv, bias, sinks, comm=None):
    T = qn.shape[0]
    nb = T // BLK

    def body(s_ref, q_ref, kc_ref, kp_ref, vc_ref, vp_ref, b_ref, o_ref, lse_ref):
        prev = _from_prev_block()
        no_key = jnp.logical_and(prev, pl.program_id(0) == 0)
        scores = []
        for h in range(NKV):
            qs = _rows2(q_ref, 256 * h)
            scores.append((_dot_nt(qs, _rows2(kc_ref, 256 * h)), _dot_nt(qs, _rows2(kp_ref, 256 * h))))
        for h in range(NKV):
            c = 256 * h
            sc, sp = scores[h]
            vstack = jnp.concatenate([vp_ref[:, c:c + 128], vc_ref[:, c:c + 128],
                                      vp_ref[:, c + 128:c + 256], vc_ref[:, c + 128:c + 256]], axis=0)
            for pr in range(2):
                ps = []
                for e in range(2):
                    hq = 4 * h + 2 * pr + e
                    rows, cols = slice(128 * pr, 128 * pr + 128), slice(128 * e, 128 * e + 128)
                    s = jnp.where(no_key, NEG, jnp.where(prev, sp[rows, cols], sc[rows, cols]) + b_ref[hq])
                    sink = s_ref[0, hq]
                    m = jnp.maximum(jnp.max(s, axis=-1, keepdims=True), sink)
                    ex = jnp.exp(s - m)
                    l = jnp.sum(ex, axis=-1, keepdims=True) + jnp.exp(sink - m)
                    p = ex * (1.0 / l)
                    ps += [jnp.where(prev, p, 0.0).astype(bf16), jnp.where(prev, 0.0, p).astype(bf16)]
                    lse_ref[:, hq:hq + 1] = m + jnp.log(l)
                o_ref[:, c + 128 * pr:c + 128 * pr + 128] = _dot(jnp.concatenate(ps, axis=1), vstack).astype(bf16)

    blk = lambda f: pl.BlockSpec((BLK, D), f)
    cur = lambda n: (n, 0)
    prev = lambda n: (jnp.maximum(n - 1, 0), 0)
    return _call(
        body, (sinks, qn, kk, kk, vv, vv, bias), grid=(nb,),
        in_specs=[pl.BlockSpec(memory_space=pltpu.SMEM), blk(cur), blk(cur), blk(prev), blk(cur), blk(prev),
                  pl.BlockSpec((NQ, BLK, BLK), lambda n: (0, 0, 0))],
        out_specs=[blk(cur), pl.BlockSpec((BLK, NQ), cur)],
        out_shape=[SDS((T, D), bf16), SDS((T, NQ), f32)],
        name="attn_fwd", sem=("parallel",), vmem_mb=32, comm=comm, mid_step=(3 * nb) // 4)


TT = 512
NBLK = 32
RPB = TT // NBLK
CONV_LANES = 128
KBLK = 4
GBLK = 8
TAPG = 8


def _block_perm():
    p = np.arange(TT)
    m = np.zeros((TT, TT), np.float32)
    m[p, NBLK * (p % RPB) + p // RPB] = 1.0
    return jnp.asarray(m, bf16), jnp.asarray(m.T, bf16)


def _lane_groups():
    return [slice(q * CONV_LANES, (q + 1) * CONV_LANES) for q in range(D // CONV_LANES)]


def _fill_time_blocks(z, tile, edge, causal):
    row = lax.broadcasted_iota(jnp.int32, (RPB, 1), 0)
    for k in range(NBLK):
        blk = tile[RPB * k:RPB * (k + 1)]
        if causal:
            z[NBLK + k] = blk
            z[k] = jnp.where(row == 0, edge[k:k + 1], pltpu.roll(blk, 1, 0))
        else:
            z[k] = blk
            z[NBLK + k] = jnp.where(row == RPB - 1, edge[k:k + 1], pltpu.roll(blk, RPB - 1, 0))


def _block_conv(z, w_ref, tap_offset, init, store):
    def step(s, carry):
        k0 = s * KBLK
        for ln in _lane_groups():
            accs = [init(ln) for _ in range(KBLK)]
            for g0 in range(0, CW, TAPG):
                taps = range(g0, min(g0 + TAPG, CW))
                lo = min(tap_offset(j) for j in taps)
                hi = max(tap_offset(j) for j in taps)
                win = [z[k0 + lo + d, :, ln] for d in range(KBLK + hi - lo)]
                for j in taps:
                    wv = w_ref[j:j + 1, ln]
                    for q in range(KBLK):
                        accs[q] = accs[q] + win[q + tap_offset(j) - lo] * wv
            for q in range(KBLK):
                store(k0 + q, ln, accs[q])
        return carry

    lax.fori_loop(0, NBLK // KBLK, step, 0)


def _block_rows(k):
    return pl.ds(pl.multiple_of(k * RPB, RPB), RPB)


def _glu_conv_fwd(proj, tail, w_dw, b_dw, ln_g, ln_b, perm_t, comm=None):
    T = proj.shape[0]

    def body(a_ref, g_ref, ta_ref, tg_ref, w_ref, b_ref, lg_ref, lb_ref, pt_ref, h1_ref, h3_ref, z):
        edge = jnp.where(pl.program_id(0) > 0, ta_ref[...] * _sigmoid(tg_ref[...]), 0.0)
        _fill_time_blocks(z, a_ref[...] * _sigmoid(g_ref[...]), edge, causal=True)

        def store(k, ln, value):
            h1_ref[_block_rows(k), ln] = value

        _block_conv(z, w_ref, lambda j: NBLK - (CW - 1) + j,
                    lambda ln: jnp.broadcast_to(b_ref[:, ln], (RPB, CONV_LANES)), store)
        h1 = h1_ref[...]
        mu = jnp.mean(h1, axis=-1, keepdims=True)
        xc = h1 - mu
        var = jnp.mean(xc * xc, axis=-1, keepdims=True)
        h2 = xc * lax.rsqrt(var + EPS) * lg_ref[...] + lb_ref[...]
        h3_ref[...] = _dot(pt_ref[...], (h2 * _sigmoid(h2)).astype(bf16)).astype(bf16)

    tile = lambda cb: pl.BlockSpec((TT, D), lambda i: (i, cb))
    edge = lambda cb: pl.BlockSpec((NBLK, D), lambda i: (jnp.maximum(i - 1, 0), cb))
    vec = pl.BlockSpec((1, D), lambda i: (0, 0))
    (h1, h3), got = _call(
        body, (proj, proj, tail, tail, w_dw, b_dw, ln_g, ln_b, perm_t), grid=(T // TT,),
        in_specs=[tile(C_A), tile(C_G), edge(0), edge(1), pl.BlockSpec((HALO, D), lambda i: (0, 0)), vec, vec, vec,
                  pl.BlockSpec((TT, TT), lambda i: (0, 0), pipeline_mode=pl.Buffered(1))],
        out_specs=[pl.BlockSpec((TT, D), lambda i: (i, 0))] * 2,
        out_shape=[SDS((T, D), f32), SDS((T, D), bf16)],
        scratch_shapes=[pltpu.VMEM((2 * NBLK, RPB, D), f32)],
        name="glu_conv_fwd", sem=("parallel",), vmem_mb=40, comm=comm, mid_step=(3 * (T // TT)) // 4)
    return h1, h3, got


def _mix_out(o, h3, proj, x, w_attn_o, w_conv_out, w_out, g_mlp):
    T = x.shape[0]
    tm = 512

    def body(o_ref, h3_ref, ga_ref, gc_ref, x_ref, wa_ref, wc_ref, wo_ref, g_ref,
             attn_ref, conv_ref, mg_ref, x1_ref, n2_ref):
        x1 = x_ref[...]
        for j in range(D // MIX_CHUNK):
            cols = slice(j * MIX_CHUNK, (j + 1) * MIX_CHUNK)
            attn = _dot(o_ref[...], wa_ref[:, cols])
            conv = _dot(h3_ref[...], wc_ref[:, cols])
            attn_ref[:, cols] = attn.astype(bf16)
            conv_ref[:, cols] = conv.astype(bf16)
            mg = (_sigmoid(ga_ref[:, cols]) * attn + _sigmoid(gc_ref[:, cols]) * conv).astype(bf16)
            mg_ref[:, cols] = mg
            x1 = x1 + _dot(mg, wo_ref[cols, :])
        x1_ref[...] = x1
        r = lax.rsqrt(jnp.mean(x1 * x1, axis=-1, keepdims=True) + EPS)
        n2_ref[...] = (x1 * r * g_ref[...]).astype(bf16)

    tile = lambda cb=0: pl.BlockSpec((tm, D), lambda i: (i, cb))
    wfull = pl.BlockSpec((D, D), lambda i: (0, 0), pipeline_mode=pl.Buffered(1))
    return pl.pallas_call(
        body, grid=(T // tm,),
        in_specs=[tile(), tile(), tile(C_GA), tile(C_GC), tile(), wfull, wfull, wfull,
                  pl.BlockSpec((1, D), lambda i: (0, 0))],
        out_specs=[tile()] * 5,
        out_shape=[SDS((T, D), bf16), SDS((T, D), bf16), SDS((T, D), bf16), SDS((T, D), f32), SDS((T, D), bf16)],
        name="mix_out", compiler_params=_cparams(("parallel",), 48))(o, h3, proj, proj, x, w_attn_o, w_conv_out, w_out, g_mlp)


def _mlp_fwd(n2, w1, w2, x1, tgt):
    T = n2.shape[0]
    tm, tf = 512, 1024

    def body(n2_ref, w1_ref, w2_ref, x1_ref, t_ref, hm_ref, dy_ref, dyb_ref, loss_ref):
        @pl.when(pl.program_id(0) == 0)
        def _():
            loss_ref[...] = jnp.zeros_like(loss_ref)

        n2v = n2_ref[...]
        for c in range(DFF // tf):
            r = jnp.maximum(_dot(n2v, w1_ref[:, c * tf:(c + 1) * tf]), 0.0)
            hm_ref[:, c * tf:(c + 1) * tf] = (r * r).astype(bf16)
        e = x1_ref[...] + _dot(hm_ref[...], w2_ref[...]) - t_ref[...]
        dy = e * (1.0 / D)
        dy_ref[...] = dy
        dyb_ref[...] = dy.astype(bf16)
        loss_ref[...] += 0.5 * jnp.sum(jnp.sum(e * e, axis=-1, keepdims=True) * (1.0 / D))

    row = pl.BlockSpec((tm, D), lambda i: (i, 0))
    once = pl.Buffered(1)
    return pl.pallas_call(
        body, grid=(T // tm,),
        in_specs=[row, pl.BlockSpec((D, DFF), lambda i: (0, 0), pipeline_mode=once),
                  pl.BlockSpec((DFF, D), lambda i: (0, 0), pipeline_mode=once), row, row],
        out_specs=[pl.BlockSpec((tm, DFF), lambda i: (i, 0)), row, row, pl.BlockSpec((8, 128), lambda i: (0, 0))],
        out_shape=[SDS((T, DFF), bf16), SDS((T, D), f32), SDS((T, D), bf16), SDS((8, 128), f32)],
        name="mlp_fwd", compiler_params=_cparams(("arbitrary",), 56))(n2, w1, w2, x1, tgt)


def _rms_bwd(xv, g, dn, dres):
    r = lax.rsqrt(jnp.mean(xv * xv, axis=-1, keepdims=True) + EPS)
    gd = dn * g
    dx = dres + r * gd - xv * (r * r * r) * jnp.mean(xv * gd, axis=-1, keepdims=True)
    dg = jnp.sum(dn * xv * r, axis=0, keepdims=True)
    return dx, dg


def _mlp_bwd(dy, dyb, hmid, w1, w2, x1, g_mlp):
    T = dy.shape[0]
    tm, tf = 512, 1024

    def body(dy_ref, dyb_ref, hm_ref, w1_ref, w2_ref, x1_ref, g_ref, df_ref, dx_ref, dxb_ref, dg_ref):
        @pl.when(pl.program_id(0) == 0)
        def _():
            dg_ref[...] = jnp.zeros_like(dg_ref)

        dyb = dyb_ref[...]
        for c in range(DFF // tf):
            cols = slice(c * tf, (c + 1) * tf)
            d_hm = _dot_nt(dyb, w2_ref[cols, :])
            df_ref[:, cols] = (d_hm * (2.0 * jnp.sqrt(hm_ref[:, cols].astype(f32)))).astype(bf16)
        dn = _dot_nt(df_ref[...], w1_ref[...])
        dx, dg = _rms_bwd(x1_ref[...], g_ref[...], dn, dy_ref[...])
        dx_ref[...] = dx
        dxb_ref[...] = dx.astype(bf16)
        dg_ref[...] += dg

    row = pl.BlockSpec((tm, D), lambda i: (i, 0))
    wide = pl.BlockSpec((tm, DFF), lambda i: (i, 0))
    vec = pl.BlockSpec((1, D), lambda i: (0, 0))
    once = pl.Buffered(1)
    return pl.pallas_call(
        body, grid=(T // tm,),
        in_specs=[row, row, wide, pl.BlockSpec((D, DFF), lambda i: (0, 0), pipeline_mode=once),
                  pl.BlockSpec((DFF, D), lambda i: (0, 0), pipeline_mode=once), row, vec],
        out_specs=[wide, row, row, vec],
        out_shape=[SDS((T, DFF), bf16), SDS((T, D), f32), SDS((T, D), bf16), SDS((1, D), f32)],
        name="mlp_bwd", compiler_params=_cparams(("arbitrary",), 56))(dy, dyb, hmid, w1, w2, x1, g_mlp)


def _wgrad(a, b, name, tn=1024):
    T, M = a.shape
    N = b.shape[1]
    tmm, tk = min(M, 1024), min(T, 2048)

    def body(a_ref, b_ref, o_ref):
        @pl.when(pl.program_id(2) == 0)
        def _():
            o_ref[...] = jnp.zeros_like(o_ref)

        o_ref[...] += _dot_tn(a_ref[...], b_ref[...])

    return pl.pallas_call(
        body, grid=(M // tmm, N // tn, T // tk),
        in_specs=[pl.BlockSpec((tk, tmm), lambda m, n, t: (t, m)), pl.BlockSpec((tk, tn), lambda m, n, t: (t, n))],
        out_specs=pl.BlockSpec((tmm, tn), lambda m, n, t: (m, n)),
        out_shape=SDS((M, N), f32),
        name=name, compiler_params=_cparams(("parallel", "parallel", "arbitrary"), 40))(a, b)


def _mix_bwd(dx1b, proj, attn, conv, w_attn_o, w_conv_out, w_out, perm, comm=None):
    T = dx1b.shape[0]
    tm = TT

    def body(dx_ref, ga_ref, gc_ref, attn_ref, conv_ref, wa_ref, wc_ref, wo_ref, perm_ref,
             dat_ref, dcv_ref, do_ref, dh3_ref, dga_ref, dgc_ref):
        d_o, d_h3 = None, None
        for j in range(D // MIX_CHUNK):
            cols = slice(j * MIX_CHUNK, (j + 1) * MIX_CHUNK)
            dm = _dot_nt(dx_ref[...], wo_ref[cols, :])
            sa = _sigmoid(ga_ref[:, cols])
            sc = _sigmoid(gc_ref[:, cols])
            dat = (dm * sa).astype(bf16)
            dcv = (dm * sc).astype(bf16)
            dat_ref[:, cols] = dat
            dcv_ref[:, cols] = dcv
            dga_ref[:, cols] = (dm * attn_ref[:, cols].astype(f32) * sa * (1.0 - sa)).astype(bf16)
            dgc_ref[:, cols] = (dm * conv_ref[:, cols].astype(f32) * sc * (1.0 - sc)).astype(bf16)
            part_o = _dot_nt(dat, wa_ref[:, cols])
            part_h = _dot_nt(_dot(perm_ref[...], dcv).astype(bf16), wc_ref[:, cols])
            d_o = part_o if d_o is None else d_o + part_o
            d_h3 = part_h if d_h3 is None else d_h3 + part_h
        do_ref[...] = d_o.astype(bf16)
        dh3_ref[...] = d_h3

    tile = lambda cb=0: pl.BlockSpec((tm, D), lambda i: (i, cb))
    wfull = pl.BlockSpec((D, D), lambda i: (0, 0), pipeline_mode=pl.Buffered(1))
    return _call(
        body, (dx1b, proj, proj, attn, conv, w_attn_o, w_conv_out, w_out, perm), grid=(T // tm,),
        in_specs=[tile(), tile(C_GA), tile(C_GC), tile(), tile(), wfull, wfull, wfull,
                  pl.BlockSpec((TT, TT), lambda i: (0, 0), pipeline_mode=pl.Buffered(1))],
        out_specs=[tile()] * 6,
        out_shape=[SDS((T, D), bf16), SDS((T, D), bf16), SDS((T, D), bf16), SDS((T, D), f32),
                   SDS((T, D), bf16), SDS((T, D), bf16)],
        name="mix_bwd", sem=("parallel",), vmem_mb=48, comm=comm)


def _conv_ln_bwd(dh3, h1, ln_g, ln_b, comm=None):
    T = dh3.shape[0]
    tt = TT // 2
    bpt = tt // RPB

    def body(d_ref, h1_ref, lg_ref, lb_ref, dh1_ref, acc_ref, head_ref):
        @pl.when(pl.program_id(0) == 0)
        def _():
            acc_ref[...] = jnp.zeros_like(acc_ref)

        h1 = h1_ref[...]
        mu = jnp.mean(h1, axis=-1, keepdims=True)
        xc = h1 - mu
        rstd = lax.rsqrt(jnp.mean(xc * xc, axis=-1, keepdims=True) + EPS)
        xh = xc * rstd
        h2 = xh * lg_ref[...] + lb_ref[...]
        sg = _sigmoid(h2)
        dh2 = d_ref[...] * (sg * (1.0 + h2 * (1.0 - sg)))
        dxh = dh2 * lg_ref[...]
        dh1 = rstd * (dxh - jnp.mean(dxh, axis=-1, keepdims=True) - xh * jnp.mean(dxh * xh, axis=-1, keepdims=True))
        dh1_ref[...] = dh1
        for k in range(bpt):
            head_ref[k:k + 1, :] = dh1[RPB * k:RPB * k + 1]
        acc_ref[0:1, :] += jnp.sum(dh2 * xh, axis=0, keepdims=True)
        acc_ref[1:2, :] += jnp.sum(dh2, axis=0, keepdims=True)
        acc_ref[2:3, :] += jnp.sum(dh1, axis=0, keepdims=True)

    tile = pl.BlockSpec((tt, D), lambda i: (i, 0))
    vec = pl.BlockSpec((1, D), lambda i: (0, 0))
    return _call(
        body, (dh3, h1, ln_g, ln_b), grid=(T // tt,),
        in_specs=[tile, tile, vec, vec],
        out_specs=[tile, pl.BlockSpec((8, D), lambda i: (0, 0)), pl.BlockSpec((bpt, D), lambda i: (i, 0))],
        out_shape=[SDS((T, D), f32), SDS((8, D), f32), SDS((T // RPB, D), f32)],
        name="conv_ln_bwd", sem=("arbitrary",), vmem_mb=32, comm=comm)


def _conv_bwd(dh1, head, proj, tail, w_dw, perm_t, comm=None):
    T = dh1.shape[0]
    nt = T // TT

    def body(d_ref, hd_ref, a_ref, g_ref, ta_ref, tg_ref, w_ref, pt_ref, da_ref, dg_ref, gw_ref, zd, zh, dh0, gacc):
        i = pl.program_id(0)

        @pl.when(i == 0)
        def _():
            gacc[...] = jnp.zeros_like(gacc)

        a = a_ref[...]
        sg = _sigmoid(g_ref[...])
        _fill_time_blocks(zd, d_ref[...], jnp.where(i < nt - 1, hd_ref[...], 0.0), causal=False)
        _fill_time_blocks(zh, a * sg, jnp.where(i > 0, ta_ref[...] * _sigmoid(tg_ref[...]), 0.0), causal=True)

        def store(k, ln, value):
            dh0[_block_rows(k), ln] = value

        _block_conv(zd, w_ref, lambda j: (CW - 1) - j, lambda ln: jnp.zeros((RPB, CONV_LANES), f32), store)

        for ln in _lane_groups():
            for g0 in range(0, CW, TAPG):
                taps = list(range(g0, min(g0 + TAPG, CW)))

                def add_blocks(s, accs, ln=ln, taps=taps):
                    k0 = s * GBLK
                    first = k0 + NBLK - (CW - 1) + taps[0]
                    win = [zh[first + t, :, ln] for t in range(GBLK + len(taps) - 1)]
                    accs = list(accs)
                    for q in range(GBLK):
                        d = zd[k0 + q, :, ln]
                        for n, j in enumerate(taps):
                            accs[n] = accs[n] + d * win[q + j - taps[0]]
                    return tuple(accs)

                accs = lax.fori_loop(0, NBLK // GBLK, add_blocks,
                                     tuple(jnp.zeros((RPB, CONV_LANES), f32) for _ in taps))
                for j, acc in zip(taps, accs):
                    gacc[j, :, ln] += acc

        d0 = dh0[...]
        da_ref[...] = _dot(pt_ref[...], (d0 * sg).astype(bf16)).astype(bf16)
        dg_ref[...] = _dot(pt_ref[...], (d0 * a * sg * (1.0 - sg)).astype(bf16)).astype(bf16)

        @pl.when(i == nt - 1)
        def _():
            gw_ref[...] = jnp.zeros_like(gw_ref)
            for j in range(CW):
                gw_ref[j:j + 1, :] = jnp.sum(gacc[j], axis=0, keepdims=True)

    tile = lambda cb=0: pl.BlockSpec((TT, D), lambda i: (i, cb))
    prev_edge = lambda cb: pl.BlockSpec((NBLK, D), lambda i: (jnp.maximum(i - 1, 0), cb))
    next_edge = pl.BlockSpec((NBLK, D), lambda i: (jnp.minimum(i + 1, nt - 1), 0))
    wspec = pl.BlockSpec((HALO, D), lambda i: (0, 0))
    return _call(
        body, (dh1, head, proj, proj, tail, tail, w_dw, perm_t), grid=(nt,),
        in_specs=[tile(), next_edge, tile(C_A), tile(C_G), prev_edge(0), prev_edge(1), wspec,
                  pl.BlockSpec((TT, TT), lambda i: (0, 0), pipeline_mode=pl.Buffered(1))],
        out_specs=[tile(), tile(), wspec],
        out_shape=[SDS((T, D), bf16), SDS((T, D), bf16), SDS((HALO, D), f32)],
        scratch_shapes=[pltpu.VMEM((2 * NBLK, RPB, D), f32), pltpu.VMEM((2 * NBLK, RPB, D), f32),
                        pltpu.VMEM((TT, D), f32), pltpu.VMEM((HALO, RPB, D), f32)],
        name="conv_bwd", sem=("arbitrary",), vmem_mb=48, comm=comm)


def _attn_bwd(qn, kk, vv, bias, sinks, o, do, lse, comm=None):
    T = qn.shape[0]
    nb = T // BLK

    def body(s_ref, q_ref, kc_ref, kp_ref, vc_ref, vp_ref, b_ref, o_ref, do_ref, lse_ref,
             dq_ref, dkc_ref, dkp_ref, dvc_ref, dvp_ref, dsk_ref, dsa_ref):
        n = pl.program_id(0)

        @pl.when(n == 0)
        def _():
            dsk_ref[...] = jnp.zeros_like(dsk_ref)
            dsa_ref[...] = jnp.zeros_like(dsa_ref)

        @pl.when(n == nb)
        def _():
            dkp_ref[...] = jnp.zeros_like(dkp_ref)
            dvp_ref[...] = jnp.zeros_like(dvp_ref)

        @pl.when(n < nb)
        def _():
            from_prev = _from_prev_block()
            no_key = jnp.logical_and(from_prev, n == 0)
            lo = _low_head_lanes()
            dups = {"kc": [], "kp": [], "vc": [], "vp": []}
            products = []
            for h in range(NKV):
                qs = _rows2(q_ref, 256 * h)
                dos = _rows2(do_ref, 256 * h)
                products.append((qs, dos, _dot_nt(qs, _rows2(kc_ref, 256 * h)), _dot_nt(qs, _rows2(kp_ref, 256 * h)),
                                 _dot_nt(dos, _rows2(vc_ref, 256 * h)), _dot_nt(dos, _rows2(vp_ref, 256 * h))))
            for h in range(NKV):
                c = 256 * h
                qs, dos, sc, sp, dpc, dpp = products[h]
                kstack = jnp.concatenate([kp_ref[:, c:c + 128], kc_ref[:, c:c + 128],
                                          kp_ref[:, c + 128:c + 256], kc_ref[:, c + 128:c + 256]], axis=0)
                p_c, p_p, ds_c, ds_p = [], [], [], []
                for pr in range(2):
                    cc = c + 128 * pr
                    prod = do_ref[:, cc:cc + 128].astype(f32) * o_ref[:, cc:cc + 128].astype(f32)
                    d_lo = jnp.sum(jnp.where(lo, prod, 0.0), axis=-1, keepdims=True)
                    d_hi = jnp.sum(prod, axis=-1, keepdims=True) - d_lo
                    row_pc, row_pp, row_dc, row_dp = [], [], [], []
                    for e in range(2):
                        hq = 4 * h + 2 * pr + e
                        rows, cols = slice(128 * pr, 128 * pr + 128), slice(128 * e, 128 * e + 128)
                        delta = d_lo if e == 0 else d_hi
                        lse = lse_ref[:, hq:hq + 1]
                        s = jnp.where(from_prev, sp[rows, cols], sc[rows, cols]) + b_ref[hq]
                        p = jnp.where(no_key, 0.0, jnp.exp(s - lse))
                        ds = p * (jnp.where(from_prev, dpp[rows, cols], dpc[rows, cols]) - delta)
                        dsa_ref[hq] += ds
                        dsk_ref[hq] += jnp.broadcast_to(-jnp.sum(jnp.exp(s_ref[0, hq] - lse) * delta), (8, 128))
                        row_pc.append(jnp.where(from_prev, 0.0, p).astype(bf16))
                        row_pp.append(jnp.where(from_prev, p, 0.0).astype(bf16))
                        row_dc.append(jnp.where(from_prev, 0.0, ds).astype(bf16))
                        row_dp.append(jnp.where(from_prev, ds, 0.0).astype(bf16))
                    dq_ref[:, cc:cc + 128] = _dot(jnp.concatenate([row_dp[0], row_dc[0], row_dp[1], row_dc[1]], axis=1), kstack)
                    p_c.append(jnp.concatenate(row_pc, axis=1))
                    p_p.append(jnp.concatenate(row_pp, axis=1))
                    ds_c.append(jnp.concatenate(row_dc, axis=1))
                    ds_p.append(jnp.concatenate(row_dp, axis=1))

                def to_keys(m2, rhs):
                    x2 = _dot_tn(jnp.concatenate(m2, axis=0), rhs)
                    x = jnp.where(lo, x2[0:128], x2[128:256])
                    return x + pltpu.roll(x, HD, 1)

                dups["kc"].append(to_keys(ds_c, qs))
                dups["kp"].append(to_keys(ds_p, qs))
                dups["vc"].append(to_keys(p_c, dos))
                dups["vp"].append(to_keys(p_p, dos))
            for key, ref in (("kc", dkc_ref), ("kp", dkp_ref), ("vc", dvc_ref), ("vp", dvp_ref)):
                d = dups[key]
                ref[:, 0:128] = jnp.where(lo, d[0], d[1])
                ref[:, 128:256] = jnp.where(lo, d[2], d[3])

    clamp = lambda n: jnp.minimum(n, nb - 1)
    blk = lambda f: pl.BlockSpec((BLK, D), f)
    cur = lambda n: (clamp(n), 0)
    prev = lambda n: (jnp.maximum(clamp(n) - 1, 0), 0)
    back = lambda n: (jnp.maximum(n - 1, 0), 0)
    kvb = lambda f: pl.BlockSpec((BLK, NKV * HD), f)
    return _call(
        body, (sinks, qn, kk, kk, vv, vv, bias, o, do, lse), grid=(nb + 1,),
        in_specs=[pl.BlockSpec(memory_space=pltpu.SMEM), blk(cur), blk(cur), blk(prev), blk(cur), blk(prev),
                  pl.BlockSpec((NQ, BLK, BLK), lambda n: (0, 0, 0)), blk(cur), blk(cur),
                  pl.BlockSpec((BLK, NQ), cur)],
        out_specs=[blk(cur), kvb(cur), kvb(back), kvb(cur), kvb(back),
                   pl.BlockSpec((NQ, 8, 128), lambda n: (0, 0, 0)),
                   pl.BlockSpec((NQ, BLK, BLK), lambda n: (0, 0, 0))],
        out_shape=[SDS((T, D), f32)] + [SDS((T, NKV * HD), f32)] * 4 + [SDS((NQ, 8, 128), f32), SDS((NQ, BLK, BLK), f32)],
        name="attn_bwd", sem=("arbitrary",), vmem_mb=40, comm=comm)


def _bias_bwd(dsa):
    def body(bk_ref, ds_ref, out_ref):
        bk = bk_ref[...]
        lane = lax.broadcasted_iota(jnp.int32, (1, 128), 1)
        for h in range(NQ):
            ds = ds_ref[h]
            row = jnp.zeros((1, 128), f32)
            for b in range(NBUCKET):
                row = jnp.where(lane == b, jnp.sum(jnp.where(bk == b, ds, 0.0)), row)
            out_ref[h:h + 1, :] = row

    return pl.pallas_call(body, out_shape=SDS((NQ, 128), f32), name="bias_bwd")(jnp.asarray(_bucket_tile()), dsa)


def _qkv_bwd(proj, gq2, gk2, dqn, dkc, dkp, dvc, dvp):
    T = proj.shape[0]
    tm = 512

    def body(q_ref, kv_ref, gq_ref, gk_ref, dq_ref, dkc_ref, dkp_ref, dvc_ref, dvp_ref,
             oq_ref, okv_ref, ggq_ref, ggk_ref):
        @pl.when(pl.program_id(0) == 0)
        def _():
            ggq_ref[...] = jnp.zeros_like(ggq_ref)
            ggk_ref[...] = jnp.zeros_like(ggk_ref)

        bd = _head_blockdiag()

        def norm_bwd(z, dy, g, scale):
            r = lax.rsqrt(_head_sums(z * z, bd) * (1.0 / HD) + EPS)
            gd = dy * g * scale
            dz = r * gd - z * (r * r * r) * _head_sums(z * gd, bd) * (1.0 / HD)
            return dz, jnp.sum(dy * scale * z * r, axis=0, keepdims=True)

        gq = jnp.zeros((1, 128), f32)
        for p in range(NQ // 2):
            ln = slice(128 * p, 128 * p + 128)
            dz, dg = norm_bwd(q_ref[:, ln], dq_ref[:, ln], gq_ref[...], HD ** -0.5)
            oq_ref[:, ln] = dz.astype(bf16)
            gq = gq + dg
        ggq_ref[...] += gq + pltpu.roll(gq, HD, 1)
        gk = jnp.zeros((1, 128), f32)
        for p in range(NKV // 2):
            ln = slice(128 * p, 128 * p + 128)
            dz, dg = norm_bwd(kv_ref[:, ln], dkc_ref[:, ln] + dkp_ref[:, ln], gk_ref[...], 1.0)
            okv_ref[:, ln] = dz.astype(bf16)
            gk = gk + dg
        ggk_ref[...] += gk + pltpu.roll(gk, HD, 1)
        okv_ref[:, 256:512] = (dvc_ref[...] + dvp_ref[...]).astype(bf16)

    vec = pl.BlockSpec((1, 128), lambda i: (0, 0))
    kvb = pl.BlockSpec((tm, NKV * HD), lambda i: (i, 0))
    return pl.pallas_call(
        body, grid=(T // tm,),
        in_specs=[pl.BlockSpec((tm, D), lambda i: (i, C_Q)), pl.BlockSpec((tm, 512), lambda i: (i, C_KV)), vec, vec,
                  pl.BlockSpec((tm, D), lambda i: (i, 0)), kvb, kvb, kvb, kvb],
        out_specs=[pl.BlockSpec((tm, D), lambda i: (i, 0)), pl.BlockSpec((tm, 512), lambda i: (i, 0)), vec, vec],
        out_shape=[SDS((T, D), bf16), SDS((T, 512), bf16), SDS((1, 128), f32), SDS((1, 128), f32)],
        name="qkv_bwd", compiler_params=_cparams(("arbitrary",), 32))(proj, proj, gq2, gk2, dqn, dkc, dkp, dvc, dvp)


def _inproj_bwd(pieces, w_in, x, dx1, g_mix, comm=None):
    T = x.shape[0]
    tm = 512
    widths = [p.shape[1] for p in pieces]
    offs = [sum(widths[:i]) for i in range(len(widths))]
    assert sum(widths) == INW

    def body(*refs):
        p_refs, (w_ref, x_ref, dx1_ref, g_ref, dx_ref, dg_ref) = refs[:len(pieces)], refs[len(pieces):]

        @pl.when(pl.program_id(0) == 0)
        def _():
            dg_ref[...] = jnp.zeros_like(dg_ref)

        du = None
        for p_ref, off, wd in zip(p_refs, offs, widths):
            part = _dot_nt(p_ref[...], w_ref[:, _weight_cols(off, wd)])
            du = part if du is None else du + part
        dx, dg = _rms_bwd(x_ref[...], g_ref[...], du, dx1_ref[...])
        dx_ref[...] = dx
        dg_ref[...] += dg

    row = pl.BlockSpec((tm, D), lambda i: (i, 0))
    vec = pl.BlockSpec((1, D), lambda i: (0, 0))
    return _call(
        body, (*pieces, w_in, x, dx1, g_mix), grid=(T // tm,),
        in_specs=[pl.BlockSpec((tm, wd), lambda i: (i, 0)) for wd in widths]
        + [pl.BlockSpec((D, INW), lambda i: (0, 0), pipeline_mode=pl.Buffered(1)), row, row, vec],
        out_specs=[row, vec],
        out_shape=[SDS((T, D), f32), SDS((1, D), f32)],
        name="inproj_bwd", sem=("arbitrary",), vmem_mb=48, comm=comm)


def _forward_backward(x, tgt, w, placed, chip_core):
    def sums(names, grads, got):
        res = [_pair_sum(nm, grads[nm], got_nm, chip_core) for nm, got_nm in zip(names, got)]
        return {nm: r[0] for nm, r in zip(names, res)}, {nm: r[1] for nm, r in zip(names, res)}

    first = ["w_in", "w_dw"]
    w_in, w_dw = _run_comm(_gather_comm({nm: placed[nm] for nm in first}), "gather_first")
    gq2 = jnp.tile(w["q_norm_g"], (1, 2))
    gk2 = jnp.tile(w["k_norm_g"], (1, 2))
    def gathered_in(names):
        return names, _gather_comm({nm: placed[nm] for nm in names})

    full = {}
    perm, perm_t = _block_perm()
    names, comm = gathered_in(["w_out", "w_attn_o", "w_conv_out"])
    proj, u, tail, got = _rms_inproj(x, w["norm_mix_g"], w_in, perm, comm=comm)
    full.update(zip(names, got))
    qn, kk, vv = _qk_prep(proj, gq2, gk2)
    bias = _bias_tiles(w["rel_bias"])
    names, comm = gathered_in(["w_ff1"])
    (o, lse), got = _attn_fwd(qn, kk, vv, bias, w["attn_sinks"], comm=comm)
    full.update(zip(names, got))
    names, comm = gathered_in(["w_ff2"])
    h1, h3, got = _glu_conv_fwd(proj, tail, w_dw, w["b_dw"], w["conv_ln_g"], w["conv_ln_b"], perm_t, comm=comm)
    full.update(zip(names, got))
    attn, conv, merged, x1, n2 = _mix_out(o, h3, proj, x, full["w_attn_o"], full["w_conv_out"], full["w_out"],
                                          w["norm_mlp_g"])
    hmid, dy, dyb, loss = _mlp_fwd(n2, full["w_ff1"], full["w_ff2"], x1, tgt)

    g, cp, own = {}, {}, {}
    df1, dx1, dx1b, g["norm_mlp_g"] = _mlp_bwd(dy, dyb, hmid, full["w_ff1"], full["w_ff2"], x1, w["norm_mlp_g"])
    ff = ["w_ff1", "w_ff2"]
    gff = {"w_ff2": _wgrad(hmid, dyb, "wgrad_ff2"), "w_ff1": _wgrad(n2, df1, "wgrad_ff1")}
    (dat, dcv, do, dh3, dga, dgc), got = _mix_bwd(dx1b, proj, attn, conv, full["w_attn_o"], full["w_conv_out"],
                                                  full["w_out"], perm, comm=_pair_exchange_comm(gff, ff))
    cp_ff, own_ff = sums(ff, gff, got)
    sq = ["w_out", "w_attn_o", "w_conv_out"]
    gsq = {"w_out": _wgrad(merged, dx1b, "wgrad_out"), "w_attn_o": _wgrad(o, dat, "wgrad_attn_o"),
           "w_conv_out": _wgrad(h3, dcv, "wgrad_conv_out")}
    (dh1, lnacc, head), got = _conv_ln_bwd(dh3, h1, w["conv_ln_g"], w["conv_ln_b"], comm=_pair_exchange_comm(gsq, sq))
    cp_sq, own_sq = sums(sq, gsq, got)
    cp, own = {**cp_ff, **cp_sq}, {**own_ff, **own_sq}
    g["conv_ln_g"], g["conv_ln_b"], g["b_dw"] = lnacc[0:1], lnacc[1:2], lnacc[2:3]
    five = ff + sq
    (da, dg, g["w_dw"]), rc = _conv_bwd(dh1, head, proj, tail, w_dw, perm_t, comm=_chip_exchange_comm(cp, five))
    tot = {nm: _chip_sum(nm, own[nm], rc_nm, chip_core) for nm, rc_nm in zip(five, rc)}
    (dqn, dkc, dkp, dvc, dvp, dsk, dsa), shards = _attn_bwd(qn, kk, vv, bias, w["attn_sinks"], o, do, lse,
                                                            comm=_pair_share_comm(tot, five))
    shards = dict(zip(five, shards))
    g["attn_sinks"] = dsk[:, 0, 0].reshape(1, NQ)
    g["rel_bias"] = _bias_bwd(dsa)[:, 0:NBUCKET].T
    dq, dkv, ggq, ggk = _qkv_bwd(proj, gq2, gk2, dqn, dkc, dkp, dvc, dvp)
    g["q_norm_g"], g["k_norm_g"] = ggq[:, 0:HD], ggk[:, 0:HD]
    pieces = [dq, da, dg, dga, dgc, dkv]
    names = ["q", "a", "g", "ga", "gc", "kv"]
    gw = {nm: _wgrad(u, p, "wgrad_in_" + nm, tn=p.shape[1] if p.shape[1] < 1024 else 1024) for nm, p in zip(names, pieces)}
    gin = {"w_in": jnp.concatenate([gw["q"], gw["kv"], gw["a"], gw["g"], gw["ga"], gw["gc"]], axis=1)}
    got = _run_comm(_pair_exchange_comm(gin, ["w_in"]), "rs_pair_exchange_in")
    cp_in, own_in = sums(["w_in"], gin, got)
    (grad_x, g["norm_mix_g"]), rc = _inproj_bwd(pieces, w_in, x, dx1, w["norm_mix_g"],
                                                comm=_chip_exchange_comm(cp_in, ["w_in"]))
    tot = {"w_in": _chip_sum("w_in", own_in["w_in"], rc[0], chip_core)}
    shards["w_in"] = _run_comm(_pair_share_comm(tot, ["w_in"]), "rs_pair_share_in")[0]
    return loss[0, 0], grad_x, g, shards


BIG = ["w_in", "w_attn_o", "w_conv_out", "w_out", "w_ff1", "w_ff2"]
SHARD_AXIS = {"w_in": 1, "w_attn_o": 0, "w_conv_out": 0, "w_out": 0, "w_ff1": 1, "w_ff2": 0, "w_dw": 1}
SHARD_SHAPE = {"w_in": (D, INW // 4), "w_attn_o": (D // 4, D), "w_conv_out": (D // 4, D), "w_out": (D // 4, D),
               "w_ff1": (D, DFF // 4), "w_ff2": (DFF // 4, D), "w_dw": (HALO, D // 4)}


def _position():
    x, y, c = lax.axis_index("x"), lax.axis_index("y"), lax.axis_index("c")
    other_chips = [(1 - x, y), (x, 1 - y), (1 - x, 1 - y)]
    return x, y, c, 2 * x + y, other_chips


def _shard_window(name, full_ref, s, half=None):
    R, C = SHARD_SHAPE[name]
    r0, nr = (0, R) if half is None else (half * (R // 2), R // 2)
    if SHARD_AXIS[name] == 1:
        return full_ref.at[pl.ds(r0, nr), pl.ds(s * C, C)]
    return full_ref.at[pl.ds(s * R + r0, nr), :]


def _remote(src, dst, send_sems, recv_sems, k, device):
    return pltpu.make_async_remote_copy(src_ref=src, dst_ref=dst, send_sem=send_sems.at[k], recv_sem=recv_sems.at[k],
                                        device_id=device, device_id_type=MESH)


def _full_shape(nm):
    R, C = SHARD_SHAPE[nm]
    return (R, 4 * C) if SHARD_AXIS[nm] == 1 else (4 * R, C)


def _place_shard(nm, shard, chip_arr, dtype):
    R, C = SHARD_SHAPE[nm]
    tr = min(R, 256)
    if SHARD_AXIS[nm] == 1:
        o_map = lambda i, ch: (i, ch[0])
    else:
        o_map = lambda i, ch: (ch[0] * (R // tr) + i, 0)

    def body(ch_ref, s_ref, o_ref):
        o_ref[...] = s_ref[...].astype(dtype)

    return pl.pallas_call(
        body,
        grid_spec=pltpu.PrefetchScalarGridSpec(
            num_scalar_prefetch=1, grid=(R // tr,),
            in_specs=[pl.BlockSpec((tr, C), lambda i, ch: (i, 0))], out_specs=pl.BlockSpec((tr, C), o_map)),
        out_shape=SDS(_full_shape(nm), dtype), name="place_" + nm,
        compiler_params=_cparams(("parallel",), 32))(chip_arr, shard)


def _gather_comm(placed):
    names = list(placed)
    n = len(names)

    def copies(cout, send, recv):
        x, y, c, chip, chips = _position()
        for a, nm in enumerate(names):
            for j, (cx, cy) in enumerate(chips):
                def ici(s, a=a, nm=nm, j=j, cx=cx, cy=cy):
                    w = _shard_window(nm, cout[a], s, c)
                    return _remote(w, w, send, recv, 6 * a + j, (cx, cy, c))

                def d2d(h, a=a, nm=nm, j=j, cx=cx, cy=cy):
                    w = _shard_window(nm, cout[a], 2 * cx + cy, h)
                    return _remote(w, w, send, recv, 6 * a + 3 + j, (x, y, 1 - c))

                yield ici, d2d, chip, 2 * cx + cy, c

    def start(cin, cout, send, recv):
        for ici, d2d, chip, s, c in copies(cout, send, recv):
            ici(chip).start()

    def mid(cin, cout, send, recv):
        for ici, d2d, chip, s, c in copies(cout, send, recv):
            ici(s).wait_recv()
            d2d(c).start()

    def finish(cin, cout, send, recv):
        for ici, d2d, chip, s, c in copies(cout, send, recv):
            d2d(1 - c).wait_recv()
        for ici, d2d, chip, s, c in copies(cout, send, recv):
            ici(chip).wait_send()
            d2d(c).wait_send()

    return _Comm([placed[nm] for nm in names], [SDS(placed[nm].shape, placed[nm].dtype) for nm in names], 6 * n,
                 start, finish, mid, aliases={a: a for a in range(n)})


def _half_rows(nm):
    return SHARD_SHAPE[nm][0] // 2


RS_TILE = 128


def _exchange_comm(ins, out_shapes, copies, n_sems, aliases=None):
    def start(cin, cout, send, recv):
        for cp in copies(cin, cout, send, recv):
            cp.start()

    def finish(cin, cout, send, recv):
        for cp in copies(cin, cout, send, recv):
            cp.wait()

    return _Comm(ins, out_shapes, n_sems, start, finish, aliases=aliases)


def _pair_exchange_comm(grads, names):
    def copies(cin, cout, send, recv):
        x, y, c, chip, chips = _position()
        return [_remote(_shard_window(nm, cin[a], s, 1 - c), cout[a].at[s], send, recv, 4 * a + s, (x, y, 1 - c))
                for a, nm in enumerate(names) for s in range(4)]

    return _exchange_comm([grads[nm] for nm in names],
                          [SDS((4, _half_rows(nm), SHARD_SHAPE[nm][1]), f32) for nm in names], copies, 4 * len(names))


def _pair_sum(nm, g, got, chip_core):
    R, C = SHARD_SHAPE[nm]
    hr = R // 2
    nt = hr // RS_TILE
    if SHARD_AXIS[nm] == 1:
        g_map = lambda i, s, sc: (sc[1] * nt + i, s)
    else:
        g_map = lambda i, s, sc: (s * (R // RS_TILE) + sc[1] * nt + i, 0)

    def body(sc_ref, g_ref, got_ref, o16_ref, own_ref):
        v = g_ref[...] + got_ref[0]
        o16_ref[0] = v.astype(bf16)

        @pl.when(pl.program_id(1) == sc_ref[0])
        def _():
            own_ref[...] = v

    blk3 = pl.BlockSpec((1, RS_TILE, C), lambda i, s, sc: (s, i, 0))
    return pl.pallas_call(
        body,
        grid_spec=pltpu.PrefetchScalarGridSpec(
            num_scalar_prefetch=1, grid=(nt, 4),
            in_specs=[pl.BlockSpec((RS_TILE, C), g_map), blk3],
            out_specs=[blk3, pl.BlockSpec((RS_TILE, C), lambda i, s, sc: (i, 0))]),
        out_shape=[SDS((4, hr, C), bf16), SDS((hr, C), f32)], name="rs_pair_sum_" + nm,
        compiler_params=_cparams(("parallel", "arbitrary"), 32))(chip_core, g, got)


def _chip_exchange_comm(cp, names):
    def copies(cin, cout, send, recv):
        x, y, c, chip, chips = _position()
        return [_remote(cin[a].at[2 * cx + cy], cout[a].at[j], send, recv, 3 * a + j, (cx, cy, c))
                for a, nm in enumerate(names) for j, (cx, cy) in enumerate(chips)]

    return _exchange_comm([cp[nm] for nm in names],
                          [SDS((3, _half_rows(nm), SHARD_SHAPE[nm][1]), bf16) for nm in names], copies, 3 * len(names))


def _chip_sum(nm, own, rc, chip_core):
    R, C = SHARD_SHAPE[nm]
    nt = (R // 2) // RS_TILE

    def body(sc_ref, own_ref, rc_ref, o_ref):
        o_ref[...] = own_ref[...] + rc_ref[0].astype(f32) + rc_ref[1].astype(f32) + rc_ref[2].astype(f32)

    return pl.pallas_call(
        body,
        grid_spec=pltpu.PrefetchScalarGridSpec(
            num_scalar_prefetch=1, grid=(nt,),
            in_specs=[pl.BlockSpec((RS_TILE, C), lambda i, sc: (i, 0)),
                      pl.BlockSpec((3, RS_TILE, C), lambda i, sc: (0, i, 0))],
            out_specs=pl.BlockSpec((RS_TILE, C), lambda i, sc: (sc[1] * nt + i, 0))),
        out_shape=SDS((R, C), f32), name="rs_chip_sum_" + nm,
        compiler_params=_cparams(("parallel",), 32))(chip_core, own, rc)


def _pair_share_comm(tot, names):
    def copies(cin, cout, send, recv):
        x, y, c, chip, chips = _position()
        cps = []
        for a, nm in enumerate(names):
            hr = _half_rows(nm)
            mine = cout[a].at[pl.ds(c * hr, hr), :]
            cps.append(_remote(mine, mine, send, recv, a, (x, y, 1 - c)))
        return cps

    return _exchange_comm([tot[nm] for nm in names], [SDS(SHARD_SHAPE[nm], f32) for nm in names], copies, len(names),
                          aliases={a: a for a in range(len(names))})


SMALL_ROWS = 40


def _allreduce_small(block):
    def body(x_ref, out_ref, buf, send_sems, recv_sems, local_sem):
        x, y, c, chip, chips = _position()
        me, sibling = (x, y, c), (x, y, 1 - c)

        def slot(px, py, pc):
            return buf.at[4 * px + 2 * py + pc]

        def copy(k, block_of, to, src=None):
            return _remote(slot(*block_of) if src is None else src, slot(*block_of), send_sems, recv_sems, k, to)

        mine = pltpu.make_async_copy(x_ref, slot(*me), local_sem)
        mine.start()
        first = [copy(0, me, sibling, src=x_ref)] + [copy(1 + j, me, (*ch, c), src=x_ref) for j, ch in enumerate(chips)]
        for cp in first:
            cp.start()
        passed = [copy(4 + j, (*ch, c), sibling) for j, ch in enumerate(chips)]
        for j, ch in enumerate(chips):
            copy(1 + j, (*ch, c), me).wait_recv()
            passed[j].start()
        copy(0, sibling, me).wait_recv()
        for j, ch in enumerate(chips):
            copy(4 + j, (*ch, 1 - c), me).wait_recv()
        for cp in first + passed:
            cp.wait_send()
        mine.wait()
        acc = buf[0]
        for d in range(1, 8):
            acc = acc + buf[d]
        out_ref[...] = acc

    vm = pl.BlockSpec(memory_space=pltpu.VMEM)
    return pl.pallas_call(
        body, in_specs=[vm], out_specs=vm, out_shape=SDS((SMALL_ROWS, D), f32),
        scratch_shapes=[pltpu.VMEM((8, SMALL_ROWS, D), f32), pltpu.SemaphoreType.DMA((7,)), pltpu.SemaphoreType.DMA((7,)),
                        pltpu.SemaphoreType.DMA],
        name="allreduce_small")(block)


def _adamw(w, g, m, v, name):
    rows, cols = w.shape
    tr = 256 if rows % 256 == 0 else rows

    def body(w_ref, g_ref, m_ref, v_ref, d_ref, nm_ref, nv_ref):
        gv = g_ref[...]
        m2 = ADAM_B1 * m_ref[...] + (1.0 - ADAM_B1) * gv
        v2 = ADAM_B2 * v_ref[...] + (1.0 - ADAM_B2) * jnp.square(gv)
        m_hat = m2 / (1.0 - ADAM_B1 ** ADAM_STEP)
        v_hat = v2 / (1.0 - ADAM_B2 ** ADAM_STEP)
        d_ref[...] = -ADAM_LR * (m_hat / (jnp.sqrt(v_hat) + ADAM_EPS) + ADAM_WD * w_ref[...])
        nm_ref[...] = m2
        nv_ref[...] = v2

    spec = pl.BlockSpec((tr, cols), lambda i: (i, 0))
    return pl.pallas_call(body, grid=(rows // tr,), in_specs=[spec] * 4, out_specs=[spec] * 3,
                          out_shape=[SDS((rows, cols), f32)] * 3, name=name,
                          compiler_params=_cparams(("parallel",), 40))(w, g, m, v)


WEIGHTS = ["norm_mix_g", "w_in", "q_norm_g", "k_norm_g", "attn_sinks", "rel_bias", "w_attn_o", "w_dw", "b_dw",
           "conv_ln_g", "conv_ln_b", "w_conv_out", "w_out", "norm_mlp_g", "w_ff1", "w_ff2"]
ROW_VECS = ["norm_mix_g", "b_dw", "conv_ln_g", "conv_ln_b", "norm_mlp_g"]
MISC_ROW = 5
W_DW_ROW = 8


def _pack_small(vals, loss=None):
    misc = [vals["q_norm_g"].reshape(1, HD), vals["k_norm_g"].reshape(1, HD), vals["attn_sinks"].reshape(1, NQ),
            jnp.zeros((1, 1), f32) if loss is None else loss.reshape(1, 1), jnp.zeros((1, 111), f32),
            vals["rel_bias"].reshape(1, NBUCKET * NQ), jnp.zeros((1, 256), f32)]
    rows = [vals[nm].reshape(1, D) for nm in ROW_VECS] + [jnp.concatenate(misc, axis=1), jnp.zeros((2, D), f32)]
    return jnp.concatenate(rows, axis=0)


def _unpack_small(block):
    out = {nm: block[i:i + 1] for i, nm in enumerate(ROW_VECS)}
    misc = block[MISC_ROW]
    out["q_norm_g"] = misc[0:64].reshape(1, HD)
    out["k_norm_g"] = misc[64:128].reshape(1, HD)
    out["attn_sinks"] = misc[128:144].reshape(1, NQ)
    out["rel_bias"] = misc[256:768].reshape(NBUCKET, NQ)
    return out, misc[144]


def kernel(x, norm_mix_g, w_in, q_norm_g, k_norm_g, attn_sinks, rel_bias, w_attn_o, w_dw, b_dw, conv_ln_g, conv_ln_b, w_conv_out, w_out, norm_mlp_g, w_ff1, w_ff2, loss_target, m_norm_mix_g, m_w_in, m_q_norm_g, m_k_norm_g, m_attn_sinks, m_rel_bias, m_w_attn_o, m_w_dw, m_b_dw, m_conv_ln_g, m_conv_ln_b, m_w_conv_out, m_w_out, m_norm_mlp_g, m_w_ff1, m_w_ff2, v_norm_mix_g, v_w_in, v_q_norm_g, v_k_norm_g, v_attn_sinks, v_rel_bias, v_w_attn_o, v_w_dw, v_b_dw, v_conv_ln_g, v_conv_ln_b, v_w_conv_out, v_w_out, v_norm_mlp_g, v_w_ff1, v_w_ff2):
    args = dict(locals())
    wts = {nm: args[nm] for nm in WEIGHTS}
    mom = {nm: args["m_" + nm] for nm in WEIGHTS}
    var = {nm: args["v_" + nm] for nm in WEIGHTS}
    chip = 2 * lax.axis_index("x") + lax.axis_index("y")

    chip_arr = jnp.reshape(chip, (1,)).astype(jnp.int32)
    chip_core = jnp.stack([chip, lax.axis_index("c")]).astype(jnp.int32)
    placed = {nm: _place_shard(nm, wts[nm][0], chip_arr, bf16) for nm in BIG}
    placed["w_dw"] = _place_shard("w_dw", jnp.pad(w_dw[0], ((0, 1), (0, 0))), chip_arr, f32)

    loss_part, grad_x, g, shards = _forward_backward(x[0], loss_target[0], wts, placed, chip_core)

    small = jnp.concatenate([_pack_small(g, loss_part), g["w_dw"]], axis=0)
    small = _allreduce_small(small)
    grads, loss = _unpack_small(small)
    grads["w_dw"] = lax.dynamic_slice(small[W_DW_ROW:W_DW_ROW + CW], (0, chip * (D // 4)), (CW, D // 4))
    grads.update(shards)

    delta, new_m, new_v = {}, {}, {}
    sd, sm, sv = _adamw(_pack_small(wts), small[0:8], _pack_small(mom), _pack_small(var), "adamw_small")
    for res, blk in ((delta, sd), (new_m, sm), (new_v, sv)):
        res.update(_unpack_small(blk)[0])
    for nm in BIG + ["w_dw"]:
        shp = wts[nm].shape
        two_d = lambda a: a.reshape(shp[-2], shp[-1])
        delta[nm], new_m[nm], new_v[nm] = _adamw(two_d(wts[nm]), grads[nm], two_d(mom[nm]), two_d(var[nm]), "adamw_" + nm)

    def shaped(vals):
        return [vals[nm].reshape(wts[nm].shape) for nm in WEIGHTS]

    return (loss, grad_x[None], *shaped(grads), *shaped(delta), *shaped(new_m), *shaped(new_v))
```

```python
import functools

import numpy as np
import jax
import jax.numpy as jnp
from jax import lax
from jax.experimental import pallas as pl
from jax.experimental.pallas import tpu as pltpu

f32 = jnp.float32
bf16 = jnp.bfloat16
SDS = jax.ShapeDtypeStruct
MESH = pl.DeviceIdType.MESH

D = 1024
HD = 64
NQ = 16
NKV = 4
BLK = 128
CW = 31
HALO = 32
DFF = 4096
NBUCKET = 32
EPS = 1e-6
NEG = -1e30
INW = 5632
MIX_CHUNK = 256
C_Q, C_A, C_G, C_GA, C_GC = 0, 1, 2, 3, 4
C_KV = 10

ADAM_LR = 0.001
ADAM_B1 = 0.9
ADAM_B2 = 0.999
ADAM_EPS = 1e-08
ADAM_WD = 0.01
ADAM_STEP = 10

VMEM_BYTES_V7X = 64 << 20


def _cparams(sem, vmem_mb):
    assert (vmem_mb << 20) < VMEM_BYTES_V7X
    return pltpu.CompilerParams(dimension_semantics=sem, vmem_limit_bytes=vmem_mb << 20)


ANY = pl.BlockSpec(memory_space=pl.ANY)


HBM_PIN_BYTES = 1 << 20


def _hbm(a):
    if a.ndim >= 2 and a.size * a.dtype.itemsize >= HBM_PIN_BYTES:
        return pltpu.with_memory_space_constraint(a, pltpu.HBM)
    return a


class _Comm:
    def __init__(self, ins, out_shapes, n_sems, start, finish, mid=None, aliases=None):
        self.ins, self.out_shapes, self.n_sems = list(ins), list(out_shapes), n_sems
        self.start, self.finish, self.mid, self.aliases = start, finish, mid, dict(aliases or {})


def _call(body, args, *, grid, in_specs, out_specs, out_shape, name, sem, vmem_mb, scratch_shapes=(), comm=None,
          mid_step=None):
    n_in, n_out, n_scr = len(in_specs), len(out_specs), len(scratch_shapes)
    args = [_hbm(a) for a in args]
    if comm is None:
        outs = pl.pallas_call(body, grid=grid, in_specs=list(in_specs), out_specs=list(out_specs),
                              out_shape=list(out_shape), scratch_shapes=list(scratch_shapes), name=name,
                              compiler_params=_cparams(sem, vmem_mb))(*args)
        return list(outs), []
    ci, co = len(comm.ins), len(comm.out_shapes)
    last = grid[0] - 1

    def wrapped(*refs):
        ins, cin = refs[:n_in], refs[n_in:n_in + ci]
        outs = refs[n_in + ci:n_in + ci + n_out]
        cout = refs[n_in + ci + n_out:n_in + ci + n_out + co]
        scr = refs[n_in + ci + n_out + co:]
        send, recv = scr[n_scr], scr[n_scr + 1]
        step = pl.program_id(0)

        @pl.when(step == 0)
        def _():
            comm.start(cin, cout, send, recv)

        body(*ins, *outs, *scr[:n_scr])
        if comm.mid is not None:
            @pl.when(step == mid_step)
            def _():
                comm.mid(cin, cout, send, recv)

        @pl.when(step == last)
        def _():
            comm.finish(cin, cout, send, recv)

    res = pl.pallas_call(
        wrapped, grid=grid, in_specs=list(in_specs) + [ANY] * ci, out_specs=list(out_specs) + [ANY] * co,
        out_shape=list(out_shape) + comm.out_shapes,
        input_output_aliases={n_in + k: n_out + v for k, v in comm.aliases.items()},
        scratch_shapes=list(scratch_shapes) + [pltpu.SemaphoreType.DMA((comm.n_sems,))] * 2,
        name=name, compiler_params=_cparams(("arbitrary",), vmem_mb))(*args, *[_hbm(a) for a in comm.ins])
    return list(res[:n_out]), list(res[n_out:])


def _run_comm(comm, name):
    ci, co = len(comm.ins), len(comm.out_shapes)

    def body(*refs):
        cin, cout, (send, recv) = refs[:ci], refs[ci:ci + co], refs[ci + co:]
        comm.start(cin, cout, send, recv)
        if comm.mid is not None:
            comm.mid(cin, cout, send, recv)
        comm.finish(cin, cout, send, recv)

    return pl.pallas_call(
        body, in_specs=[ANY] * ci, out_specs=[ANY] * co, out_shape=comm.out_shapes, input_output_aliases=comm.aliases,
        scratch_shapes=[pltpu.SemaphoreType.DMA((comm.n_sems,))] * 2, name=name)(*comm.ins)


def _dot(a, b):
    return jnp.dot(a, b, preferred_element_type=f32)


def _dot_nt(a, b):
    return lax.dot_general(a, b, (((1,), (1,)), ((), ())), preferred_element_type=f32)


def _dot_tn(a, b):
    return lax.dot_general(a, b, (((0,), (0,)), ((), ())), preferred_element_type=f32)


def _sigmoid(x):
    return 1.0 / (1.0 + jnp.exp(-x))


def _low_head_lanes():
    return lax.broadcasted_iota(jnp.int32, (1, 2 * HD), 1) < HD


def _head_blockdiag():
    r = lax.broadcasted_iota(jnp.int32, (2 * HD, 2 * HD), 0) // HD
    c = lax.broadcasted_iota(jnp.int32, (2 * HD, 2 * HD), 1) // HD
    return jnp.where(r == c, 1.0, 0.0).astype(bf16)


def _head_sums(z, bd):
    hi = z.astype(bf16)
    lo = (z - hi.astype(f32)).astype(bf16)
    return _dot(hi, bd) + _dot(lo, bd)


def _weight_cols(start, width):
    kv_width = 2 * NKV * HD
    if start < D:
        orig = start
    elif start < INW - kv_width:
        orig = start + kv_width
    else:
        orig = start - (INW - kv_width) + D
    assert (start < D) == (start + width <= D) and (start < INW - kv_width) == (start + width <= INW - kv_width)
    return slice(orig, orig + width)


def _rms_inproj(x, g, w, perm, gq2, gk2, comm=None):
    T, N = x.shape[0], w.shape[1]
    tn = 512
    conv_cols = (C_A * D, (C_G + 1) * D)
    attn_chunks = [C_Q * D // tn, C_Q * D // tn + 1, C_KV]

    def body(x_ref, g_ref, w_ref, perm_ref, gq_ref, gk_ref, p_ref, u_ref, tail_ref, qn_ref, kk_ref, vv_ref):
        xv = x_ref[...]
        r = lax.rsqrt(jnp.mean(xv * xv, axis=-1, keepdims=True) + EPS)
        u = (xv * r * g_ref[...]).astype(bf16)
        u_ref[...] = u
        u_blocks = _dot(perm_ref[...], u).astype(bf16)

        def project(c):
            lhs = u_blocks if conv_cols[0] <= c * tn < conv_cols[1] else u
            p_ref[:, c * tn:(c + 1) * tn] = _dot(lhs, w_ref[:, _weight_cols(c * tn, tn)])

        for c in attn_chunks:
            project(c)
        bd = _head_blockdiag()
        lo = _low_head_lanes()
        for p in range(NQ // 2):
            z = p_ref[:, C_Q * D + 128 * p:C_Q * D + 128 * p + 128]
            rq = lax.rsqrt(_head_sums(z * z, bd) * (1.0 / HD) + EPS)
            qn_ref[:, 128 * p:128 * p + 128] = (z * rq * gq_ref[...] * (HD ** -0.5)).astype(bf16)
        kv0 = C_KV * tn
        for p in range(NKV // 2):
            z = p_ref[:, kv0 + 128 * p:kv0 + 128 * p + 128]
            rk = lax.rsqrt(_head_sums(z * z, bd) * (1.0 / HD) + EPS)
            _split_pair(z * rk * gk_ref[...], kk_ref, p, lo)
            _split_pair(p_ref[:, kv0 + 256 + 128 * p:kv0 + 256 + 128 * p + 128], vv_ref, p, lo)
        for c in range(N // tn):
            if c not in attn_chunks:
                project(c)
        tail_ref[...] = _dot(u[TT - NBLK:TT], w_ref[:, _weight_cols(conv_cols[0], conv_cols[1] - conv_cols[0])])

    once = pl.Buffered(1)
    row = pl.BlockSpec((TT, D), lambda i: (i, 0))
    vec = pl.BlockSpec((1, 128), lambda i: (0, 0))
    return _call(
        body, (x, g, w, perm, gq2, gk2), grid=(T // TT,),
        in_specs=[row, pl.BlockSpec((1, D), lambda i: (0, 0)),
                  pl.BlockSpec((D, N), lambda i: (0, 0), pipeline_mode=once),
                  pl.BlockSpec((TT, TT), lambda i: (0, 0), pipeline_mode=once), vec, vec],
        out_specs=[pl.BlockSpec((TT, N), lambda i: (i, 0)), row, pl.BlockSpec((NBLK, 2 * D), lambda i: (i, 0)),
                   row, row, row],
        out_shape=[SDS((T, N), f32), SDS((T, D), bf16), SDS((T // TT * NBLK, 2 * D), f32)] + [SDS((T, D), bf16)] * 3,
        name="rms_inproj", sem=("parallel",), vmem_mb=56, comm=comm, mid_step=(3 * (T // TT)) // 4)


def _split_pair(pair, out_ref, p, lo):
    rolled = pltpu.roll(pair, HD, 1)
    zero = jnp.zeros_like(pair)
    c = 512 * p
    out_ref[:, c:c + 128] = jnp.where(lo, pair, zero).astype(bf16)
    out_ref[:, c + 128:c + 256] = jnp.where(lo, zero, rolled).astype(bf16)
    out_ref[:, c + 256:c + 384] = jnp.where(lo, rolled, zero).astype(bf16)
    out_ref[:, c + 384:c + 512] = jnp.where(lo, zero, pair).astype(bf16)


def _bucket_tile():
    qi = np.arange(BLK)[:, None]
    kj = np.arange(BLK)[None, :]
    n = np.where(kj > qi, qi + BLK - kj, qi - kj)
    max_exact = NBUCKET // 2
    nf = np.maximum(n, 1).astype(np.float32)
    large = max_exact + (np.log(nf / max_exact) / np.float32(np.log(128 / max_exact))
                         * (NBUCKET - max_exact)).astype(np.int32)
    large = np.minimum(large, NBUCKET - 1)
    return np.where(n < max_exact, n, large).astype(np.int32)


def _from_prev_block():
    return lax.broadcasted_iota(jnp.int32, (BLK, BLK), 1) > lax.broadcasted_iota(jnp.int32, (BLK, BLK), 0)


def _bias_tiles(rel_bias):
    def body(rb_ref, bk_ref, out_ref):
        bk = bk_ref[...]
        for h in range(NQ):
            acc = jnp.zeros((BLK, BLK), f32)
            for b in range(NBUCKET):
                acc = jnp.where(bk == b, rb_ref[b, h], acc)
            out_ref[h] = acc

    return pl.pallas_call(
        body,
        in_specs=[pl.BlockSpec(memory_space=pltpu.SMEM), pl.BlockSpec(memory_space=pltpu.VMEM)],
        out_specs=pl.BlockSpec(memory_space=pltpu.VMEM),
        out_shape=SDS((NQ, BLK, BLK), f32),
        name="bias_tiles")(rel_bias, jnp.asarray(_bucket_tile()))


def _rows2(ref, c):
    return jnp.concatenate([ref[:, c:c + 128], ref[:, c + 128:c + 256]], axis=0)


def _attn_fwd(qn, kk, vv, bias, sinks, comm=None):
    T = qn.shape[0]
    nb = T // BLK

    def body(s_ref, q_ref, kc_ref, kp_ref, vc_ref, vp_ref, b_ref, o_ref, lse_ref):
        prev = _from_prev_block()
        no_key = jnp.logical_and(prev, pl.program_id(0) == 0)
        scores = []
        for h in range(NKV):
            qs = _rows2(q_ref, 256 * h)
            scores.append((_dot_nt(qs, _rows2(kc_ref, 256 * h)), _dot_nt(qs, _rows2(kp_ref, 256 * h))))
        for h in range(NKV):
            c = 256 * h
            sc, sp = scores[h]
            vstack = jnp.concatenate([vp_ref[:, c:c + 128], vc_ref[:, c:c + 128],
                                      vp_ref[:, c + 128:c + 256], vc_ref[:, c + 128:c + 256]], axis=0)
            for pr in range(2):
                ps = []
                for e in range(2):
                    hq = 4 * h + 2 * pr + e
                    rows, cols = slice(128 * pr, 128 * pr + 128), slice(128 * e, 128 * e + 128)
                    s = jnp.where(no_key, NEG, jnp.where(prev, sp[rows, cols], sc[rows, cols]) + b_ref[hq])
                    sink = s_ref[0, hq]
                    m = jnp.maximum(jnp.max(s, axis=-1, keepdims=True), sink)
                    ex = jnp.exp(s - m)
                    l = jnp.sum(ex, axis=-1, keepdims=True) + jnp.exp(sink - m)
                    p = ex * (1.0 / l)
                    ps += [jnp.where(prev, p, 0.0).astype(bf16), jnp.where(prev, 0.0, p).astype(bf16)]
                    lse_ref[:, hq:hq + 1] = m + jnp.log(l)
                o_ref[:, c + 128 * pr:c + 128 * pr + 128] = _dot(jnp.concatenate(ps, axis=1), vstack).astype(bf16)

    blk = lambda f: pl.BlockSpec((BLK, D), f)
    cur = lambda n: (n, 0)
    prev = lambda n: (jnp.maximum(n - 1, 0), 0)
    return _call(
        body, (sinks, qn, kk, kk, vv, vv, bias), grid=(nb,),
        in_specs=[pl.BlockSpec(memory_space=pltpu.SMEM), blk(cur), blk(cur), blk(prev), blk(cur), blk(prev),
                  pl.BlockSpec((NQ, BLK, BLK), lambda n: (0, 0, 0))],
        out_specs=[blk(cur), pl.BlockSpec((BLK, NQ), cur)],
        out_shape=[SDS((T, D), bf16), SDS((T, NQ), f32)],
        name="attn_fwd", sem=("parallel",), vmem_mb=32, comm=comm, mid_step=(3 * nb) // 4)


TT = 512
NBLK = 32
RPB = TT // NBLK
CONV_LANES = 128
KBLK = 4
GBLK = 8
TAPG = 8


def _block_perm():
    p = np.arange(TT)
    m = np.zeros((TT, TT), np.float32)
    m[p, NBLK * (p % RPB) + p // RPB] = 1.0
    return jnp.asarray(m, bf16), jnp.asarray(m.T, bf16)


def _lane_groups():
    return [slice(q * CONV_LANES, (q + 1) * CONV_LANES) for q in range(D // CONV_LANES)]


def _fill_time_blocks(z, tile, edge, causal):
    row = lax.broadcasted_iota(jnp.int32, (RPB, 1), 0)
    for k in range(NBLK):
        blk = tile[RPB * k:RPB * (k + 1)]
        if causal:
            z[NBLK + k] = blk
            z[k] = jnp.where(row == 0, edge[k:k + 1], pltpu.roll(blk, 1, 0))
        else:
            z[k] = blk
            z[NBLK + k] = jnp.where(row == RPB - 1, edge[k:k + 1], pltpu.roll(blk, RPB - 1, 0))


def _block_conv(z, w_ref, tap_offset, init, store):
    def step(s, carry):
        k0 = s * KBLK
        for ln in _lane_groups():
            accs = [init(ln) for _ in range(KBLK)]
            for g0 in range(0, CW, TAPG):
                taps = range(g0, min(g0 + TAPG, CW))
                lo = min(tap_offset(j) for j in taps)
                hi = max(tap_offset(j) for j in taps)
                win = [z[k0 + lo + d, :, ln] for d in range(KBLK + hi - lo)]
                for j in taps:
                    wv = w_ref[j:j + 1, ln]
                    for q in range(KBLK):
                        accs[q] = accs[q] + win[q + tap_offset(j) - lo] * wv
            for q in range(KBLK):
                store(k0 + q, ln, accs[q])
        return carry

    lax.fori_loop(0, NBLK // KBLK, step, 0)


def _block_rows(k):
    return pl.ds(pl.multiple_of(k * RPB, RPB), RPB)


def _glu_conv_fwd(proj, tail, w_dw, b_dw, ln_g, ln_b, perm_t, comm=None):
    T = proj.shape[0]

    def body(a_ref, g_ref, ta_ref, tg_ref, w_ref, b_ref, lg_ref, lb_ref, pt_ref, h1_ref, h3_ref, z):
        edge = jnp.where(pl.program_id(0) > 0, ta_ref[...] * _sigmoid(tg_ref[...]), 0.0)
        _fill_time_blocks(z, a_ref[...] * _sigmoid(g_ref[...]), edge, causal=True)

        def store(k, ln, value):
            h1_ref[_block_rows(k), ln] = value

        _block_conv(z, w_ref, lambda j: NBLK - (CW - 1) + j,
                    lambda ln: jnp.broadcast_to(b_ref[:, ln], (RPB, CONV_LANES)), store)
        h1 = h1_ref[...]
        mu = jnp.mean(h1, axis=-1, keepdims=True)
        xc = h1 - mu
        var = jnp.mean(xc * xc, axis=-1, keepdims=True)
        h2 = xc * lax.rsqrt(var + EPS) * lg_ref[...] + lb_ref[...]
        h3_ref[...] = _dot(pt_ref[...], (h2 * _sigmoid(h2)).astype(bf16)).astype(bf16)

    tile = lambda cb: pl.BlockSpec((TT, D), lambda i: (i, cb))
    edge = lambda cb: pl.BlockSpec((NBLK, D), lambda i: (jnp.maximum(i - 1, 0), cb))
    vec = pl.BlockSpec((1, D), lambda i: (0, 0))
    (h1, h3), got = _call(
        body, (proj, proj, tail, tail, w_dw, b_dw, ln_g, ln_b, perm_t), grid=(T // TT,),
        in_specs=[tile(C_A), tile(C_G), edge(0), edge(1), pl.BlockSpec((HALO, D), lambda i: (0, 0)), vec, vec, vec,
                  pl.BlockSpec((TT, TT), lambda i: (0, 0), pipeline_mode=pl.Buffered(1))],
        out_specs=[pl.BlockSpec((TT, D), lambda i: (i, 0))] * 2,
        out_shape=[SDS((T, D), f32), SDS((T, D), bf16)],
        scratch_shapes=[pltpu.VMEM((2 * NBLK, RPB, D), f32)],
        name="glu_conv_fwd", sem=("parallel",), vmem_mb=40, comm=comm, mid_step=(3 * (T // TT)) // 4)
    return h1, h3, got


def _mix_out(o, h3, proj, x, w_attn_o, w_conv_out, w_out, g_mlp):
    T = x.shape[0]
    tm = 512

    def body(o_ref, h3_ref, ga_ref, gc_ref, x_ref, wa_ref, wc_ref, wo_ref, g_ref,
             attn_ref, conv_ref, mg_ref, x1_ref, n2_ref):
        x1 = x_ref[...]
        for j in range(D // MIX_CHUNK):
            cols = slice(j * MIX_CHUNK, (j + 1) * MIX_CHUNK)
            attn = _dot(o_ref[...], wa_ref[:, cols])
            conv = _dot(h3_ref[...], wc_ref[:, cols])
            attn_ref[:, cols] = attn.astype(bf16)
            conv_ref[:, cols] = conv.astype(bf16)
            mg = (_sigmoid(ga_ref[:, cols]) * attn + _sigmoid(gc_ref[:, cols]) * conv).astype(bf16)
            mg_ref[:, cols] = mg
            x1 = x1 + _dot(mg, wo_ref[cols, :])
        x1_ref[...] = x1
        r = lax.rsqrt(jnp.mean(x1 * x1, axis=-1, keepdims=True) + EPS)
        n2_ref[...] = (x1 * r * g_ref[...]).astype(bf16)

    tile = lambda cb=0: pl.BlockSpec((tm, D), lambda i: (i, cb))
    wfull = pl.BlockSpec((D, D), lambda i: (0, 0), pipeline_mode=pl.Buffered(1))
    return pl.pallas_call(
        body, grid=(T // tm,),
        in_specs=[tile(), tile(), tile(C_GA), tile(C_GC), tile(), wfull, wfull, wfull,
                  pl.BlockSpec((1, D), lambda i: (0, 0))],
        out_specs=[tile()] * 5,
        out_shape=[SDS((T, D), bf16), SDS((T, D), bf16), SDS((T, D), bf16), SDS((T, D), f32), SDS((T, D), bf16)],
        name="mix_out", compiler_params=_cparams(("parallel",), 48))(o, h3, proj, proj, x, w_attn_o, w_conv_out, w_out, g_mlp)


def _mlp_fwd(n2, w1, w2, x1, tgt):
    T = n2.shape[0]
    tm, tf = 512, 1024

    def body(n2_ref, w1_ref, w2_ref, x1_ref, t_ref, hm_ref, dy_ref, dyb_ref, loss_ref):
        @pl.when(pl.program_id(0) == 0)
        def _():
            loss_ref[...] = jnp.zeros_like(loss_ref)

        n2v = n2_ref[...]
        for c in range(DFF // tf):
            r = jnp.maximum(_dot(n2v, w1_ref[:, c * tf:(c + 1) * tf]), 0.0)
            hm_ref[:, c * tf:(c + 1) * tf] = (r * r).astype(bf16)
        e = x1_ref[...] + _dot(hm_ref[...], w2_ref[...]) - t_ref[...]
        dy = e * (1.0 / D)
        dy_ref[...] = dy
        dyb_ref[...] = dy.astype(bf16)
        loss_ref[...] += 0.5 * jnp.sum(jnp.sum(e * e, axis=-1, keepdims=True) * (1.0 / D))

    row = pl.BlockSpec((tm, D), lambda i: (i, 0))
    once = pl.Buffered(1)
    return pl.pallas_call(
        body, grid=(T // tm,),
        in_specs=[row, pl.BlockSpec((D, DFF), lambda i: (0, 0), pipeline_mode=once),
                  pl.BlockSpec((DFF, D), lambda i: (0, 0), pipeline_mode=once), row, row],
        out_specs=[pl.BlockSpec((tm, DFF), lambda i: (i, 0)), row, row, pl.BlockSpec((8, 128), lambda i: (0, 0))],
        out_shape=[SDS((T, DFF), bf16), SDS((T, D), f32), SDS((T, D), bf16), SDS((8, 128), f32)],
        name="mlp_fwd", compiler_params=_cparams(("arbitrary",), 56))(n2, w1, w2, x1, tgt)


def _rms_bwd(xv, g, dn, dres):
    r = lax.rsqrt(jnp.mean(xv * xv, axis=-1, keepdims=True) + EPS)
    gd = dn * g
    dx = dres + r * gd - xv * (r * r * r) * jnp.mean(xv * gd, axis=-1, keepdims=True)
    dg = jnp.sum(dn * xv * r, axis=0, keepdims=True)
    return dx, dg


def _mlp_bwd(dy, dyb, hmid, w1, w2, x1, g_mlp):
    T = dy.shape[0]
    tm, tf = 512, 1024

    def body(dy_ref, dyb_ref, hm_ref, w1_ref, w2_ref, x1_ref, g_ref, df_ref, dx_ref, dxb_ref, dg_ref):
        @pl.when(pl.program_id(0) == 0)
        def _():
            dg_ref[...] = jnp.zeros_like(dg_ref)

        dyb = dyb_ref[...]
        for c in range(DFF // tf):
            cols = slice(c * tf, (c + 1) * tf)
            d_hm = _dot_nt(dyb, w2_ref[cols, :])
            df_ref[:, cols] = (d_hm * (2.0 * jnp.sqrt(hm_ref[:, cols].astype(f32)))).astype(bf16)
        dn = _dot_nt(df_ref[...], w1_ref[...])
        dx, dg = _rms_bwd(x1_ref[...], g_ref[...], dn, dy_ref[...])
        dx_ref[...] = dx
        dxb_ref[...] = dx.astype(bf16)
        dg_ref[...] += dg

    row = pl.BlockSpec((tm, D), lambda i: (i, 0))
    wide = pl.BlockSpec((tm, DFF), lambda i: (i, 0))
    vec = pl.BlockSpec((1, D), lambda i: (0, 0))
    once = pl.Buffered(1)
    return pl.pallas_call(
        body, grid=(T // tm,),
        in_specs=[row, row, wide, pl.BlockSpec((D, DFF), lambda i: (0, 0), pipeline_mode=once),
                  pl.BlockSpec((DFF, D), lambda i: (0, 0), pipeline_mode=once), row, vec],
        out_specs=[wide, row, row, vec],
        out_shape=[SDS((T, DFF), bf16), SDS((T, D), f32), SDS((T, D), bf16), SDS((1, D), f32)],
        name="mlp_bwd", compiler_params=_cparams(("arbitrary",), 56))(dy, dyb, hmid, w1, w2, x1, g_mlp)


def _wgrad(a, b, name, tn=1024):
    T, M = a.shape
    N = b.shape[1]
    tmm, tk = min(M, 1024), min(T, 2048)

    def body(a_ref, b_ref, o_ref):
        @pl.when(pl.program_id(2) == 0)
        def _():
            o_ref[...] = jnp.zeros_like(o_ref)

        o_ref[...] += _dot_tn(a_ref[...], b_ref[...])

    return pl.pallas_call(
        body, grid=(M // tmm, N // tn, T // tk),
        in_specs=[pl.BlockSpec((tk, tmm), lambda m, n, t: (t, m)), pl.BlockSpec((tk, tn), lambda m, n, t: (t, n))],
        out_specs=pl.BlockSpec((tmm, tn), lambda m, n, t: (m, n)),
        out_shape=SDS((M, N), f32),
        name=name, compiler_params=_cparams(("parallel", "parallel", "arbitrary"), 40))(a, b)


def _mix_bwd(dx1b, proj, attn, conv, w_attn_o, w_conv_out, w_out, perm, comm=None):
    T = dx1b.shape[0]
    tm = TT

    def body(dx_ref, ga_ref, gc_ref, attn_ref, conv_ref, wa_ref, wc_ref, wo_ref, perm_ref,
             dat_ref, dcv_ref, do_ref, dh3_ref, dga_ref, dgc_ref):
        d_o, d_h3 = None, None
        for j in range(D // MIX_CHUNK):
            cols = slice(j * MIX_CHUNK, (j + 1) * MIX_CHUNK)
            dm = _dot_nt(dx_ref[...], wo_ref[cols, :])
            sa = _sigmoid(ga_ref[:, cols])
            sc = _sigmoid(gc_ref[:, cols])
            dat = (dm * sa).astype(bf16)
            dcv = (dm * sc).astype(bf16)
            dat_ref[:, cols] = dat
            dcv_ref[:, cols] = dcv
            dga_ref[:, cols] = (dm * attn_ref[:, cols].astype(f32) * sa * (1.0 - sa)).astype(bf16)
            dgc_ref[:, cols] = (dm * conv_ref[:, cols].astype(f32) * sc * (1.0 - sc)).astype(bf16)
            part_o = _dot_nt(dat, wa_ref[:, cols])
            part_h = _dot_nt(_dot(perm_ref[...], dcv).astype(bf16), wc_ref[:, cols])
            d_o = part_o if d_o is None else d_o + part_o
            d_h3 = part_h if d_h3 is None else d_h3 + part_h
        do_ref[...] = d_o.astype(bf16)
        dh3_ref[...] = d_h3

    tile = lambda cb=0: pl.BlockSpec((tm, D), lambda i: (i, cb))
    wfull = pl.BlockSpec((D, D), lambda i: (0, 0), pipeline_mode=pl.Buffered(1))
    return _call(
        body, (dx1b, proj, proj, attn, conv, w_attn_o, w_conv_out, w_out, perm), grid=(T // tm,),
        in_specs=[tile(), tile(C_GA), tile(C_GC), tile(), tile(), wfull, wfull, wfull,
                  pl.BlockSpec((TT, TT), lambda i: (0, 0), pipeline_mode=pl.Buffered(1))],
        out_specs=[tile()] * 6,
        out_shape=[SDS((T, D), bf16), SDS((T, D), bf16), SDS((T, D), bf16), SDS((T, D), f32),
                   SDS((T, D), bf16), SDS((T, D), bf16)],
        name="mix_bwd", sem=("parallel",), vmem_mb=48, comm=comm)


def _conv_ln_bwd(dh3, h1, ln_g, ln_b, comm=None):
    T = dh3.shape[0]
    tt = TT // 2
    bpt = tt // RPB

    def body(d_ref, h1_ref, lg_ref, lb_ref, dh1_ref, acc_ref, head_ref):
        @pl.when(pl.program_id(0) == 0)
        def _():
            acc_ref[...] = jnp.zeros_like(acc_ref)

        h1 = h1_ref[...]
        mu = jnp.mean(h1, axis=-1, keepdims=True)
        xc = h1 - mu
        rstd = lax.rsqrt(jnp.mean(xc * xc, axis=-1, keepdims=True) + EPS)
        xh = xc * rstd
        h2 = xh * lg_ref[...] + lb_ref[...]
        sg = _sigmoid(h2)
        dh2 = d_ref[...] * (sg * (1.0 + h2 * (1.0 - sg)))
        dxh = dh2 * lg_ref[...]
        dh1 = rstd * (dxh - jnp.mean(dxh, axis=-1, keepdims=True) - xh * jnp.mean(dxh * xh, axis=-1, keepdims=True))
        dh1_ref[...] = dh1
        for k in range(bpt):
            head_ref[k:k + 1, :] = dh1[RPB * k:RPB * k + 1]
        acc_ref[0:1, :] += jnp.sum(dh2 * xh, axis=0, keepdims=True)
        acc_ref[1:2, :] += jnp.sum(dh2, axis=0, keepdims=True)
        acc_ref[2:3, :] += jnp.sum(dh1, axis=0, keepdims=True)

    tile = pl.BlockSpec((tt, D), lambda i: (i, 0))
    vec = pl.BlockSpec((1, D), lambda i: (0, 0))
    return _call(
        body, (dh3, h1, ln_g, ln_b), grid=(T // tt,),
        in_specs=[tile, tile, vec, vec],
        out_specs=[tile, pl.BlockSpec((8, D), lambda i: (0, 0)), pl.BlockSpec((bpt, D), lambda i: (i, 0))],
        out_shape=[SDS((T, D), f32), SDS((8, D), f32), SDS((T // RPB, D), f32)],
        name="conv_ln_bwd", sem=("arbitrary",), vmem_mb=32, comm=comm)


def _conv_bwd(dh1, head, proj, tail, w_dw, perm_t, comm=None):
    T = dh1.shape[0]
    nt = T // TT

    def body(d_ref, hd_ref, a_ref, g_ref, ta_ref, tg_ref, w_ref, pt_ref, da_ref, dg_ref, gw_ref, zd, zh, dh0, gacc):
        i = pl.program_id(0)

        @pl.when(i == 0)
        def _():
            gacc[...] = jnp.zeros_like(gacc)

        a = a_ref[...]
        sg = _sigmoid(g_ref[...])
        _fill_time_blocks(zd, d_ref[...], jnp.where(i < nt - 1, hd_ref[...], 0.0), causal=False)
        _fill_time_blocks(zh, a * sg, jnp.where(i > 0, ta_ref[...] * _sigmoid(tg_ref[...]), 0.0), causal=True)

        def store(k, ln, value):
            dh0[_block_rows(k), ln] = value

        _block_conv(zd, w_ref, lambda j: (CW - 1) - j, lambda ln: jnp.zeros((RPB, CONV_LANES), f32), store)

        for ln in _lane_groups():
            for g0 in range(0, CW, TAPG):
                taps = list(range(g0, min(g0 + TAPG, CW)))

                def add_blocks(s, accs, ln=ln, taps=taps):
                    k0 = s * GBLK
                    first = k0 + NBLK - (CW - 1) + taps[0]
                    win = [zh[first + t, :, ln] for t in range(GBLK + len(taps) - 1)]
                    accs = list(accs)
                    for q in range(GBLK):
                        d = zd[k0 + q, :, ln]
                        for n, j in enumerate(taps):
                            accs[n] = accs[n] + d * win[q + j - taps[0]]
                    return tuple(accs)

                accs = lax.fori_loop(0, NBLK // GBLK, add_blocks,
                                     tuple(jnp.zeros((RPB, CONV_LANES), f32) for _ in taps))
                for j, acc in zip(taps, accs):
                    gacc[j, :, ln] += acc

        d0 = dh0[...]
        da_ref[...] = _dot(pt_ref[...], (d0 * sg).astype(bf16)).astype(bf16)
        dg_ref[...] = _dot(pt_ref[...], (d0 * a * sg * (1.0 - sg)).astype(bf16)).astype(bf16)

        @pl.when(i == nt - 1)
        def _():
            gw_ref[...] = jnp.zeros_like(gw_ref)
            for j in range(CW):
                gw_ref[j:j + 1, :] = jnp.sum(gacc[j], axis=0, keepdims=True)

    tile = lambda cb=0: pl.BlockSpec((TT, D), lambda i: (i, cb))
    prev_edge = lambda cb: pl.BlockSpec((NBLK, D), lambda i: (jnp.maximum(i - 1, 0), cb))
    next_edge = pl.BlockSpec((NBLK, D), lambda i: (jnp.minimum(i + 1, nt - 1), 0))
    wspec = pl.BlockSpec((HALO, D), lambda i: (0, 0))
    return _call(
        body, (dh1, head, proj, proj, tail, tail, w_dw, perm_t), grid=(nt,),
        in_specs=[tile(), next_edge, tile(C_A), tile(C_G), prev_edge(0), prev_edge(1), wspec,
                  pl.BlockSpec((TT, TT), lambda i: (0, 0), pipeline_mode=pl.Buffered(1))],
        out_specs=[tile(), tile(), wspec],
        out_shape=[SDS((T, D), bf16), SDS((T, D), bf16), SDS((HALO, D), f32)],
        scratch_shapes=[pltpu.VMEM((2 * NBLK, RPB, D), f32), pltpu.VMEM((2 * NBLK, RPB, D), f32),
                        pltpu.VMEM((TT, D), f32), pltpu.VMEM((HALO, RPB, D), f32)],
        name="conv_bwd", sem=("arbitrary",), vmem_mb=48, comm=comm)


def _attn_bwd(qn, kk, vv, bias, sinks, o, do, lse, comm=None):
    T = qn.shape[0]
    nb = T // BLK

    def body(s_ref, q_ref, kc_ref, kp_ref, vc_ref, vp_ref, b_ref, o_ref, do_ref, lse_ref,
             dq_ref, dkc_ref, dkp_ref, dvc_ref, dvp_ref, dsk_ref, dsa_ref):
        n = pl.program_id(0)

        @pl.when(n == 0)
        def _():
            dsk_ref[...] = jnp.zeros_like(dsk_ref)
            dsa_ref[...] = jnp.zeros_like(dsa_ref)

        @pl.when(n == nb)
        def _():
            dkp_ref[...] = jnp.zeros_like(dkp_ref)
            dvp_ref[...] = jnp.zeros_like(dvp_ref)

        @pl.when(n < nb)
        def _():
            from_prev = _from_prev_block()
            no_key = jnp.logical_and(from_prev, n == 0)
            lo = _low_head_lanes()
            dups = {"kc": [], "kp": [], "vc": [], "vp": []}
            products = []
            for h in range(NKV):
                qs = _rows2(q_ref, 256 * h)
                dos = _rows2(do_ref, 256 * h)
                products.append((qs, dos, _dot_nt(qs, _rows2(kc_ref, 256 * h)), _dot_nt(qs, _rows2(kp_ref, 256 * h)),
                                 _dot_nt(dos, _rows2(vc_ref, 256 * h)), _dot_nt(dos, _rows2(vp_ref, 256 * h))))
            for h in range(NKV):
                c = 256 * h
                qs, dos, sc, sp, dpc, dpp = products[h]
                kstack = jnp.concatenate([kp_ref[:, c:c + 128], kc_ref[:, c:c + 128],
                                          kp_ref[:, c + 128:c + 256], kc_ref[:, c + 128:c + 256]], axis=0)
                p_c, p_p, ds_c, ds_p = [], [], [], []
                for pr in range(2):
                    cc = c + 128 * pr
                    prod = do_ref[:, cc:cc + 128].astype(f32) * o_ref[:, cc:cc + 128].astype(f32)
                    d_lo = jnp.sum(jnp.where(lo, prod, 0.0), axis=-1, keepdims=True)
                    d_hi = jnp.sum(prod, axis=-1, keepdims=True) - d_lo
                    row_pc, row_pp, row_dc, row_dp = [], [], [], []
                    for e in range(2):
                        hq = 4 * h + 2 * pr + e
                        rows, cols = slice(128 * pr, 128 * pr + 128), slice(128 * e, 128 * e + 128)
                        delta = d_lo if e == 0 else d_hi
                        lse = lse_ref[:, hq:hq + 1]
                        s = jnp.where(from_prev, sp[rows, cols], sc[rows, cols]) + b_ref[hq]
                        p = jnp.where(no_key, 0.0, jnp.exp(s - lse))
                        ds = p * (jnp.where(from_prev, dpp[rows, cols], dpc[rows, cols]) - delta)
                        dsa_ref[hq] += ds
                        dsk_ref[hq] += jnp.broadcast_to(-jnp.sum(jnp.exp(s_ref[0, hq] - lse) * delta), (8, 128))
                        row_pc.append(jnp.where(from_prev, 0.0, p).astype(bf16))
                        row_pp.append(jnp.where(from_prev, p, 0.0).astype(bf16))
                        row_dc.append(jnp.where(from_prev, 0.0, ds).astype(bf16))
                        row_dp.append(jnp.where(from_prev, ds, 0.0).astype(bf16))
                    dq_ref[:, cc:cc + 128] = _dot(jnp.concatenate([row_dp[0], row_dc[0], row_dp[1], row_dc[1]], axis=1), kstack)
                    p_c.append(jnp.concatenate(row_pc, axis=1))
                    p_p.append(jnp.concatenate(row_pp, axis=1))
                    ds_c.append(jnp.concatenate(row_dc, axis=1))
                    ds_p.append(jnp.concatenate(row_dp, axis=1))

                def to_keys(m2, rhs):
                    x2 = _dot_tn(jnp.concatenate(m2, axis=0), rhs)
                    x = jnp.where(lo, x2[0:128], x2[128:256])
                    return x + pltpu.roll(x, HD, 1)

                dups["kc"].append(to_keys(ds_c, qs))
                dups["kp"].append(to_keys(ds_p, qs))
                dups["vc"].append(to_keys(p_c, dos))
                dups["vp"].append(to_keys(p_p, dos))
            for key, ref in (("kc", dkc_ref), ("kp", dkp_ref), ("vc", dvc_ref), ("vp", dvp_ref)):
                d = dups[key]
                ref[:, 0:128] = jnp.where(lo, d[0], d[1])
                ref[:, 128:256] = jnp.where(lo, d[2], d[3])

    clamp = lambda n: jnp.minimum(n, nb - 1)
    blk = lambda f: pl.BlockSpec((BLK, D), f)
    cur = lambda n: (clamp(n), 0)
    prev = lambda n: (jnp.maximum(clamp(n) - 1, 0), 0)
    back = lambda n: (jnp.maximum(n - 1, 0), 0)
    kvb = lambda f: pl.BlockSpec((BLK, NKV * HD), f)
    return _call(
        body, (sinks, qn, kk, kk, vv, vv, bias, o, do, lse), grid=(nb + 1,),
        in_specs=[pl.BlockSpec(memory_space=pltpu.SMEM), blk(cur), blk(cur), blk(prev), blk(cur), blk(prev),
                  pl.BlockSpec((NQ, BLK, BLK), lambda n: (0, 0, 0)), blk(cur), blk(cur),
                  pl.BlockSpec((BLK, NQ), cur)],
        out_specs=[blk(cur), kvb(cur), kvb(back), kvb(cur), kvb(back),
                   pl.BlockSpec((NQ, 8, 128), lambda n: (0, 0, 0)),
                   pl.BlockSpec((NQ, BLK, BLK), lambda n: (0, 0, 0))],
        out_shape=[SDS((T, D), f32)] + [SDS((T, NKV * HD), f32)] * 4 + [SDS((NQ, 8, 128), f32), SDS((NQ, BLK, BLK), f32)],
        name="attn_bwd", sem=("arbitrary",), vmem_mb=40, comm=comm)


def _bias_bwd(dsa):
    def body(bk_ref, ds_ref, out_ref):
        bk = bk_ref[...]
        lane = lax.broadcasted_iota(jnp.int32, (1, 128), 1)
        for h in range(NQ):
            ds = ds_ref[h]
            row = jnp.zeros((1, 128), f32)
            for b in range(NBUCKET):
                row = jnp.where(lane == b, jnp.sum(jnp.where(bk == b, ds, 0.0)), row)
            out_ref[h:h + 1, :] = row

    return pl.pallas_call(body, out_shape=SDS((NQ, 128), f32), name="bias_bwd")(jnp.asarray(_bucket_tile()), dsa)


def _qkv_bwd(proj, gq2, gk2, dqn, dkc, dkp, dvc, dvp):
    T = proj.shape[0]
    tm = 512

    def body(q_ref, kv_ref, gq_ref, gk_ref, dq_ref, dkc_ref, dkp_ref, dvc_ref, dvp_ref,
             oq_ref, okv_ref, ggq_ref, ggk_ref):
        @pl.when(pl.program_id(0) == 0)
        def _():
            ggq_ref[...] = jnp.zeros_like(ggq_ref)
            ggk_ref[...] = jnp.zeros_like(ggk_ref)

        bd = _head_blockdiag()

        def norm_bwd(z, dy, g, scale):
            r = lax.rsqrt(_head_sums(z * z, bd) * (1.0 / HD) + EPS)
            gd = dy * g * scale
            dz = r * gd - z * (r * r * r) * _head_sums(z * gd, bd) * (1.0 / HD)
            return dz, jnp.sum(dy * scale * z * r, axis=0, keepdims=True)

        gq = jnp.zeros((1, 128), f32)
        for p in range(NQ // 2):
            ln = slice(128 * p, 128 * p + 128)
            dz, dg = norm_bwd(q_ref[:, ln], dq_ref[:, ln], gq_ref[...], HD ** -0.5)
            oq_ref[:, ln] = dz.astype(bf16)
            gq = gq + dg
        ggq_ref[...] += gq + pltpu.roll(gq, HD, 1)
        gk = jnp.zeros((1, 128), f32)
        for p in range(NKV // 2):
            ln = slice(128 * p, 128 * p + 128)
            dz, dg = norm_bwd(kv_ref[:, ln], dkc_ref[:, ln] + dkp_ref[:, ln], gk_ref[...], 1.0)
            okv_ref[:, ln] = dz.astype(bf16)
            gk = gk + dg
        ggk_ref[...] += gk + pltpu.roll(gk, HD, 1)
        okv_ref[:, 256:512] = (dvc_ref[...] + dvp_ref[...]).astype(bf16)

    vec = pl.BlockSpec((1, 128), lambda i: (0, 0))
    kvb = pl.BlockSpec((tm, NKV * HD), lambda i: (i, 0))
    return pl.pallas_call(
        body, grid=(T // tm,),
        in_specs=[pl.BlockSpec((tm, D), lambda i: (i, C_Q)), pl.BlockSpec((tm, 512), lambda i: (i, C_KV)), vec, vec,
                  pl.BlockSpec((tm, D), lambda i: (i, 0)), kvb, kvb, kvb, kvb],
        out_specs=[pl.BlockSpec((tm, D), lambda i: (i, 0)), pl.BlockSpec((tm, 512), lambda i: (i, 0)), vec, vec],
        out_shape=[SDS((T, D), bf16), SDS((T, 512), bf16), SDS((1, 128), f32), SDS((1, 128), f32)],
        name="qkv_bwd", compiler_params=_cparams(("arbitrary",), 32))(proj, proj, gq2, gk2, dqn, dkc, dkp, dvc, dvp)


def _inproj_bwd(pieces, w_in, x, dx1, g_mix, comm=None):
    T = x.shape[0]
    tm = 512
    widths = [p.shape[1] for p in pieces]
    offs = [sum(widths[:i]) for i in range(len(widths))]
    assert sum(widths) == INW

    def body(*refs):
        p_refs, (w_ref, x_ref, dx1_ref, g_ref, dx_ref, dg_ref) = refs[:len(pieces)], refs[len(pieces):]

        @pl.when(pl.program_id(0) == 0)
        def _():
            dg_ref[...] = jnp.zeros_like(dg_ref)

        du = None
        for p_ref, off, wd in zip(p_refs, offs, widths):
            part = _dot_nt(p_ref[...], w_ref[:, _weight_cols(off, wd)])
            du = part if du is None else du + part
        dx, dg = _rms_bwd(x_ref[...], g_ref[...], du, dx1_ref[...])
        dx_ref[...] = dx
        dg_ref[...] += dg

    row = pl.BlockSpec((tm, D), lambda i: (i, 0))
    vec = pl.BlockSpec((1, D), lambda i: (0, 0))
    return _call(
        body, (*pieces, w_in, x, dx1, g_mix), grid=(T // tm,),
        in_specs=[pl.BlockSpec((tm, wd), lambda i: (i, 0)) for wd in widths]
        + [pl.BlockSpec((D, INW), lambda i: (0, 0), pipeline_mode=pl.Buffered(1)), row, row, vec],
        out_specs=[row, vec],
        out_shape=[SDS((T, D), f32), SDS((1, D), f32)],
        name="inproj_bwd", sem=("arbitrary",), vmem_mb=48, comm=comm)


def _forward_backward(x, tgt, w, placed, chip_core):
    def sums(names, grads, got):
        res = [_pair_sum(nm, grads[nm], got_nm, chip_core) for nm, got_nm in zip(names, got)]
        return {nm: r[0] for nm, r in zip(names, res)}, {nm: r[1] for nm, r in zip(names, res)}

    first = ["w_in", "w_dw"]
    w_in, w_dw = _run_comm(_gather_comm({nm: placed[nm] for nm in first}), "gather_first")
    gq2 = jnp.tile(w["q_norm_g"], (1, 2))
    gk2 = jnp.tile(w["k_norm_g"], (1, 2))
    def gathered_in(names):
        return names, _gather_comm({nm: placed[nm] for nm in names})

    full = {}
    perm, perm_t = _block_perm()
    names, comm = gathered_in(["w_out", "w_attn_o", "w_conv_out"])
    (proj, u, tail, qn, kk, vv), got = _rms_inproj(x, w["norm_mix_g"], w_in, perm, gq2, gk2, comm=comm)
    full.update(zip(names, got))
    bias = _bias_tiles(w["rel_bias"])
    names, comm = gathered_in(["w_ff1"])
    (o, lse), got = _attn_fwd(qn, kk, vv, bias, w["attn_sinks"], comm=comm)
    full.update(zip(names, got))
    names, comm = gathered_in(["w_ff2"])
    h1, h3, got = _glu_conv_fwd(proj, tail, w_dw, w["b_dw"], w["conv_ln_g"], w["conv_ln_b"], perm_t, comm=comm)
    full.update(zip(names, got))
    attn, conv, merged, x1, n2 = _mix_out(o, h3, proj, x, full["w_attn_o"], full["w_conv_out"], full["w_out"],
                                          w["norm_mlp_g"])
    hmid, dy, dyb, loss = _mlp_fwd(n2, full["w_ff1"], full["w_ff2"], x1, tgt)

    g, cp, own = {}, {}, {}
    df1, dx1, dx1b, g["norm_mlp_g"] = _mlp_bwd(dy, dyb, hmid, full["w_ff1"], full["w_ff2"], x1, w["norm_mlp_g"])
    ff = ["w_ff1", "w_ff2"]
    gff = {"w_ff2": _wgrad(hmid, dyb, "wgrad_ff2"), "w_ff1": _wgrad(n2, df1, "wgrad_ff1")}
    (dat, dcv, do, dh3, dga, dgc), got = _mix_bwd(dx1b, proj, attn, conv, full["w_attn_o"], full["w_conv_out"],
                                                  full["w_out"], perm, comm=_pair_exchange_comm(gff, ff))
    cp_ff, own_ff = sums(ff, gff, got)
    sq = ["w_out", "w_attn_o", "w_conv_out"]
    gsq = {"w_out": _wgrad(merged, dx1b, "wgrad_out"), "w_attn_o": _wgrad(o, dat, "wgrad_attn_o"),
           "w_conv_out": _wgrad(h3, dcv, "wgrad_conv_out")}
    (dh1, lnacc, head), got = _conv_ln_bwd(dh3, h1, w["conv_ln_g"], w["conv_ln_b"], comm=_pair_exchange_comm(gsq, sq))
    cp_sq, own_sq = sums(sq, gsq, got)
    cp, own = {**cp_ff, **cp_sq}, {**own_ff, **own_sq}
    g["conv_ln_g"], g["conv_ln_b"], g["b_dw"] = lnacc[0:1], lnacc[1:2], lnacc[2:3]
    five = ff + sq
    (da, dg, g["w_dw"]), rc = _conv_bwd(dh1, head, proj, tail, w_dw, perm_t, comm=_chip_exchange_comm(cp, five))
    tot = {nm: _chip_sum(nm, own[nm], rc_nm, chip_core) for nm, rc_nm in zip(five, rc)}
    (dqn, dkc, dkp, dvc, dvp, dsk, dsa), shards = _attn_bwd(qn, kk, vv, bias, w["attn_sinks"], o, do, lse,
                                                            comm=_pair_share_comm(tot, five))
    shards = dict(zip(five, shards))
    g["attn_sinks"] = dsk[:, 0, 0].reshape(1, NQ)
    g["rel_bias"] = _bias_bwd(dsa)[:, 0:NBUCKET].T
    dq, dkv, ggq, ggk = _qkv_bwd(proj, gq2, gk2, dqn, dkc, dkp, dvc, dvp)
    g["q_norm_g"], g["k_norm_g"] = ggq[:, 0:HD], ggk[:, 0:HD]
    pieces = [dq, da, dg, dga, dgc, dkv]
    names = ["q", "a", "g", "ga", "gc", "kv"]
    gw = {nm: _wgrad(u, p, "wgrad_in_" + nm, tn=p.shape[1] if p.shape[1] < 1024 else 1024) for nm, p in zip(names, pieces)}
    gin = {"w_in": jnp.concatenate([gw["q"], gw["kv"], gw["a"], gw["g"], gw["ga"], gw["gc"]], axis=1)}
    got = _run_comm(_pair_exchange_comm(gin, ["w_in"]), "rs_pair_exchange_in")
    cp_in, own_in = sums(["w_in"], gin, got)
    (grad_x, g["norm_mix_g"]), rc = _inproj_bwd(pieces, w_in, x, dx1, w["norm_mix_g"],
                                                comm=_chip_exchange_comm(cp_in, ["w_in"]))
    tot = {"w_in": _chip_sum("w_in", own_in["w_in"], rc[0], chip_core)}
    shards["w_in"] = _run_comm(_pair_share_comm(tot, ["w_in"]), "rs_pair_share_in")[0]
    return loss[0, 0], grad_x, g, shards


BIG = ["w_in", "w_attn_o", "w_conv_out", "w_out", "w_ff1", "w_ff2"]
SHARD_AXIS = {"w_in": 1, "w_attn_o": 0, "w_conv_out": 0, "w_out": 0, "w_ff1": 1, "w_ff2": 0, "w_dw": 1}
SHARD_SHAPE = {"w_in": (D, INW // 4), "w_attn_o": (D // 4, D), "w_conv_out": (D // 4, D), "w_out": (D // 4, D),
               "w_ff1": (D, DFF // 4), "w_ff2": (DFF // 4, D), "w_dw": (HALO, D // 4)}


def _position():
    x, y, c = lax.axis_index("x"), lax.axis_index("y"), lax.axis_index("c")
    other_chips = [(1 - x, y), (x, 1 - y), (1 - x, 1 - y)]
    return x, y, c, 2 * x + y, other_chips


def _shard_window(name, full_ref, s, half=None):
    R, C = SHARD_SHAPE[name]
    r0, nr = (0, R) if half is None else (half * (R // 2), R // 2)
    if SHARD_AXIS[name] == 1:
        return full_ref.at[pl.ds(r0, nr), pl.ds(s * C, C)]
    return full_ref.at[pl.ds(s * R + r0, nr), :]


def _remote(src, dst, send_sems, recv_sems, k, device):
    return pltpu.make_async_remote_copy(src_ref=src, dst_ref=dst, send_sem=send_sems.at[k], recv_sem=recv_sems.at[k],
                                        device_id=device, device_id_type=MESH)


def _full_shape(nm):
    R, C = SHARD_SHAPE[nm]
    return (R, 4 * C) if SHARD_AXIS[nm] == 1 else (4 * R, C)


def _place_shard(nm, shard, chip_arr, dtype):
    R, C = SHARD_SHAPE[nm]
    tr = min(R, 256)
    if SHARD_AXIS[nm] == 1:
        o_map = lambda i, ch: (i, ch[0])
    else:
        o_map = lambda i, ch: (ch[0] * (R // tr) + i, 0)

    def body(ch_ref, s_ref, o_ref):
        o_ref[...] = s_ref[...].astype(dtype)

    return pl.pallas_call(
        body,
        grid_spec=pltpu.PrefetchScalarGridSpec(
            num_scalar_prefetch=1, grid=(R // tr,),
            in_specs=[pl.BlockSpec((tr, C), lambda i, ch: (i, 0))], out_specs=pl.BlockSpec((tr, C), o_map)),
        out_shape=SDS(_full_shape(nm), dtype), name="place_" + nm,
        compiler_params=_cparams(("parallel",), 32))(chip_arr, shard)


def _gather_comm(placed):
    names = list(placed)
    n = len(names)

    def copies(cout, send, recv):
        x, y, c, chip, chips = _position()
        for a, nm in enumerate(names):
            for j, (cx, cy) in enumerate(chips):
                def ici(s, a=a, nm=nm, j=j, cx=cx, cy=cy):
                    w = _shard_window(nm, cout[a], s, c)
                    return _remote(w, w, send, recv, 6 * a + j, (cx, cy, c))

                def d2d(h, a=a, nm=nm, j=j, cx=cx, cy=cy):
                    w = _shard_window(nm, cout[a], 2 * cx + cy, h)
                    return _remote(w, w, send, recv, 6 * a + 3 + j, (x, y, 1 - c))

                yield ici, d2d, chip, 2 * cx + cy, c

    def start(cin, cout, send, recv):
        for ici, d2d, chip, s, c in copies(cout, send, recv):
            ici(chip).start()

    def mid(cin, cout, send, recv):
        for ici, d2d, chip, s, c in copies(cout, send, recv):
            ici(s).wait_recv()
            d2d(c).start()

    def finish(cin, cout, send, recv):
        for ici, d2d, chip, s, c in copies(cout, send, recv):
            d2d(1 - c).wait_recv()
        for ici, d2d, chip, s, c in copies(cout, send, recv):
            ici(chip).wait_send()
            d2d(c).wait_send()

    return _Comm([placed[nm] for nm in names], [SDS(placed[nm].shape, placed[nm].dtype) for nm in names], 6 * n,
                 start, finish, mid, aliases={a: a for a in range(n)})


def _half_rows(nm):
    return SHARD_SHAPE[nm][0] // 2


RS_TILE = 128


def _exchange_comm(ins, out_shapes, copies, n_sems, aliases=None):
    def start(cin, cout, send, recv):
        for cp in copies(cin, cout, send, recv):
            cp.start()

    def finish(cin, cout, send, recv):
        for cp in copies(cin, cout, send, recv):
            cp.wait()

    return _Comm(ins, out_shapes, n_sems, start, finish, aliases=aliases)


def _pair_exchange_comm(grads, names):
    def copies(cin, cout, send, recv):
        x, y, c, chip, chips = _position()
        return [_remote(_shard_window(nm, cin[a], s, 1 - c), cout[a].at[s], send, recv, 4 * a + s, (x, y, 1 - c))
                for a, nm in enumerate(names) for s in range(4)]

    return _exchange_comm([grads[nm] for nm in names],
                          [SDS((4, _half_rows(nm), SHARD_SHAPE[nm][1]), f32) for nm in names], copies, 4 * len(names))


def _pair_sum(nm, g, got, chip_core):
    R, C = SHARD_SHAPE[nm]
    hr = R // 2
    nt = hr // RS_TILE
    if SHARD_AXIS[nm] == 1:
        g_map = lambda i, s, sc: (sc[1] * nt + i, s)
    else:
        g_map = lambda i, s, sc: (s * (R // RS_TILE) + sc[1] * nt + i, 0)

    def body(sc_ref, g_ref, got_ref, o16_ref, own_ref):
        v = g_ref[...] + got_ref[0]
        o16_ref[0] = v.astype(bf16)

        @pl.when(pl.program_id(1) == sc_ref[0])
        def _():
            own_ref[...] = v

    blk3 = pl.BlockSpec((1, RS_TILE, C), lambda i, s, sc: (s, i, 0))
    return pl.pallas_call(
        body,
        grid_spec=pltpu.PrefetchScalarGridSpec(
            num_scalar_prefetch=1, grid=(nt, 4),
            in_specs=[pl.BlockSpec((RS_TILE, C), g_map), blk3],
            out_specs=[blk3, pl.BlockSpec((RS_TILE, C), lambda i, s, sc: (i, 0))]),
        out_shape=[SDS((4, hr, C), bf16), SDS((hr, C), f32)], name="rs_pair_sum_" + nm,
        compiler_params=_cparams(("parallel", "arbitrary"), 32))(chip_core, g, got)


def _chip_exchange_comm(cp, names):
    def copies(cin, cout, send, recv):
        x, y, c, chip, chips = _position()
        return [_remote(cin[a].at[2 * cx + cy], cout[a].at[j], send, recv, 3 * a + j, (cx, cy, c))
                for a, nm in enumerate(names) for j, (cx, cy) in enumerate(chips)]

    return _exchange_comm([cp[nm] for nm in names],
                          [SDS((3, _half_rows(nm), SHARD_SHAPE[nm][1]), bf16) for nm in names], copies, 3 * len(names))


def _chip_sum(nm, own, rc, chip_core):
    R, C = SHARD_SHAPE[nm]
    nt = (R // 2) // RS_TILE

    def body(sc_ref, own_ref, rc_ref, o_ref):
        o_ref[...] = own_ref[...] + rc_ref[0].astype(f32) + rc_ref[1].astype(f32) + rc_ref[2].astype(f32)

    return pl.pallas_call(
        body,
        grid_spec=pltpu.PrefetchScalarGridSpec(
            num_scalar_prefetch=1, grid=(nt,),
            in_specs=[pl.BlockSpec((RS_TILE, C), lambda i, sc: (i, 0)),
                      pl.BlockSpec((3, RS_TILE, C), lambda i, sc: (0, i, 0))],
            out_specs=pl.BlockSpec((RS_TILE, C), lambda i, sc: (sc[1] * nt + i, 0))),
        out_shape=SDS((R, C), f32), name="rs_chip_sum_" + nm,
        compiler_params=_cparams(("parallel",), 32))(chip_core, own, rc)


def _pair_share_comm(tot, names):
    def copies(cin, cout, send, recv):
        x, y, c, chip, chips = _position()
        cps = []
        for a, nm in enumerate(names):
            hr = _half_rows(nm)
            mine = cout[a].at[pl.ds(c * hr, hr), :]
            cps.append(_remote(mine, mine, send, recv, a, (x, y, 1 - c)))
        return cps

    return _exchange_comm([tot[nm] for nm in names], [SDS(SHARD_SHAPE[nm], f32) for nm in names], copies, len(names),
                          aliases={a: a for a in range(len(names))})


SMALL_ROWS = 40


def _allreduce_small(block):
    def body(x_ref, out_ref, buf, send_sems, recv_sems, local_sem):
        x, y, c, chip, chips = _position()
        me, sibling = (x, y, c), (x, y, 1 - c)

        def slot(px, py, pc):
            return buf.at[4 * px + 2 * py + pc]

        def copy(k, block_of, to, src=None):
            return _remote(slot(*block_of) if src is None else src, slot(*block_of), send_sems, recv_sems, k, to)

        mine = pltpu.make_async_copy(x_ref, slot(*me), local_sem)
        mine.start()
        first = [copy(0, me, sibling, src=x_ref)] + [copy(1 + j, me, (*ch, c), src=x_ref) for j, ch in enumerate(chips)]
        for cp in first:
            cp.start()
        passed = [copy(4 + j, (*ch, c), sibling) for j, ch in enumerate(chips)]
        for j, ch in enumerate(chips):
            copy(1 + j, (*ch, c), me).wait_recv()
            passed[j].start()
        copy(0, sibling, me).wait_recv()
        for j, ch in enumerate(chips):
            copy(4 + j, (*ch, 1 - c), me).wait_recv()
        for cp in first + passed:
            cp.wait_send()
        mine.wait()
        acc = buf[0]
        for d in range(1, 8):
            acc = acc + buf[d]
        out_ref[...] = acc

    vm = pl.BlockSpec(memory_space=pltpu.VMEM)
    return pl.pallas_call(
        body, in_specs=[vm], out_specs=vm, out_shape=SDS((SMALL_ROWS, D), f32),
        scratch_shapes=[pltpu.VMEM((8, SMALL_ROWS, D), f32), pltpu.SemaphoreType.DMA((7,)), pltpu.SemaphoreType.DMA((7,)),
                        pltpu.SemaphoreType.DMA],
        name="allreduce_small")(block)


def _adamw(w, g, m, v, name):
    rows, cols = w.shape
    tr = 256 if rows % 256 == 0 else rows

    def body(w_ref, g_ref, m_ref, v_ref, d_ref, nm_ref, nv_ref):
        gv = g_ref[...]
        m2 = ADAM_B1 * m_ref[...] + (1.0 - ADAM_B1) * gv
        v2 = ADAM_B2 * v_ref[...] + (1.0 - ADAM_B2) * jnp.square(gv)
        m_hat = m2 / (1.0 - ADAM_B1 ** ADAM_STEP)
        v_hat = v2 / (1.0 - ADAM_B2 ** ADAM_STEP)
        d_ref[...] = -ADAM_LR * (m_hat / (jnp.sqrt(v_hat) + ADAM_EPS) + ADAM_WD * w_ref[...])
        nm_ref[...] = m2
        nv_ref[...] = v2

    spec = pl.BlockSpec((tr, cols), lambda i: (i, 0))
    return pl.pallas_call(body, grid=(rows // tr,), in_specs=[spec] * 4, out_specs=[spec] * 3,
                          out_shape=[SDS((rows, cols), f32)] * 3, name=name,
                          compiler_params=_cparams(("parallel",), 40))(w, g, m, v)


WEIGHTS = ["norm_mix_g", "w_in", "q_norm_g", "k_norm_g", "attn_sinks", "rel_bias", "w_attn_o", "w_dw", "b_dw",
           "conv_ln_g", "conv_ln_b", "w_conv_out", "w_out", "norm_mlp_g", "w_ff1", "w_ff2"]
ROW_VECS = ["norm_mix_g", "b_dw", "conv_ln_g", "conv_ln_b", "norm_mlp_g"]
MISC_ROW = 5
W_DW_ROW = 8


def _pack_small(vals, loss=None):
    misc = [vals["q_norm_g"].reshape(1, HD), vals["k_norm_g"].reshape(1, HD), vals["attn_sinks"].reshape(1, NQ),
            jnp.zeros((1, 1), f32) if loss is None else loss.reshape(1, 1), jnp.zeros((1, 111), f32),
            vals["rel_bias"].reshape(1, NBUCKET * NQ), jnp.zeros((1, 256), f32)]
    rows = [vals[nm].reshape(1, D) for nm in ROW_VECS] + [jnp.concatenate(misc, axis=1), jnp.zeros((2, D), f32)]
    return jnp.concatenate(rows, axis=0)


def _unpack_small(block):
    out = {nm: block[i:i + 1] for i, nm in enumerate(ROW_VECS)}
    misc = block[MISC_ROW]
    out["q_norm_g"] = misc[0:64].reshape(1, HD)
    out["k_norm_g"] = misc[64:128].reshape(1, HD)
    out["attn_sinks"] = misc[128:144].reshape(1, NQ)
    out["rel_bias"] = misc[256:768].reshape(NBUCKET, NQ)
    return out, misc[144]


def kernel(x, norm_mix_g, w_in, q_norm_g, k_norm_g, attn_sinks, rel_bias, w_attn_o, w_dw, b_dw, conv_ln_g, conv_ln_b, w_conv_out, w_out, norm_mlp_g, w_ff1, w_ff2, loss_target, m_norm_mix_g, m_w_in, m_q_norm_g, m_k_norm_g, m_attn_sinks, m_rel_bias, m_w_attn_o, m_w_dw, m_b_dw, m_conv_ln_g, m_conv_ln_b, m_w_conv_out, m_w_out, m_norm_mlp_g, m_w_ff1, m_w_ff2, v_norm_mix_g, v_w_in, v_q_norm_g, v_k_norm_g, v_attn_sinks, v_rel_bias, v_w_attn_o, v_w_dw, v_b_dw, v_conv_ln_g, v_conv_ln_b, v_w_conv_out, v_w_out, v_norm_mlp_g, v_w_ff1, v_w_ff2):
    args = dict(locals())
    wts = {nm: args[nm] for nm in WEIGHTS}
    mom = {nm: args["m_" + nm] for nm in WEIGHTS}
    var = {nm: args["v_" + nm] for nm in WEIGHTS}
    chip = 2 * lax.axis_index("x") + lax.axis_index("y")

    chip_arr = jnp.reshape(chip, (1,)).astype(jnp.int32)
    chip_core = jnp.stack([chip, lax.axis_index("c")]).astype(jnp.int32)
    placed = {nm: _place_shard(nm, wts[nm][0], chip_arr, bf16) for nm in BIG}
    placed["w_dw"] = _place_shard("w_dw", jnp.pad(w_dw[0], ((0, 1), (0, 0))), chip_arr, f32)

    loss_part, grad_x, g, shards = _forward_backward(x[0], loss_target[0], wts, placed, chip_core)

    small = jnp.concatenate([_pack_small(g, loss_part), g["w_dw"]], axis=0)
    small = _allreduce_small(small)
    grads, loss = _unpack_small(small)
    grads["w_dw"] = lax.dynamic_slice(small[W_DW_ROW:W_DW_ROW + CW], (0, chip * (D // 4)), (CW, D // 4))
    grads.update(shards)

    delta, new_m, new_v = {}, {}, {}
    sd, sm, sv = _adamw(_pack_small(wts), small[0:8], _pack_small(mom), _pack_small(var), "adamw_small")
    for res, blk in ((delta, sd), (new_m, sm), (new_v, sv)):
        res.update(_unpack_small(blk)[0])
    for nm in BIG + ["w_dw"]:
        shp = wts[nm].shape
        two_d = lambda a: a.reshape(shp[-2], shp[-1])
        delta[nm], new_m[nm], new_v[nm] = _adamw(two_d(wts[nm]), grads[nm], two_d(mom[nm]), two_d(var[nm]), "adamw_" + nm)

    def shaped(vals):
        return [vals[nm].reshape(wts[nm].shape) for nm in WEIGHTS]

    return (loss, grad_x[None], *shaped(grads), *shaped(delta), *shaped(new_m), *shaped(new_v))
```

```python
import functools

import numpy as np
import jax
import jax.numpy as jnp
from jax import lax
from jax.experimental import pallas as pl
from jax.experimental.pallas import tpu as pltpu

f32 = jnp.float32
bf16 = jnp.bfloat16
SDS = jax.ShapeDtypeStruct
MESH = pl.DeviceIdType.MESH

D = 1024
HD = 64
NQ = 16
NKV = 4
BLK = 128
CW = 31
HALO = 32
DFF = 4096
NBUCKET = 32
EPS = 1e-6
NEG = -1e30
INW = 5632
MIX_CHUNK = 256
C_Q, C_A, C_G, C_GA, C_GC = 0, 1, 2, 3, 4
C_KV = 10

ADAM_LR = 0.001
ADAM_B1 = 0.9
ADAM_B2 = 0.999
ADAM_EPS = 1e-08
ADAM_WD = 0.01
ADAM_STEP = 10

VMEM_BYTES_V7X = 64 << 20


def _cparams(sem, vmem_mb):
    assert (vmem_mb << 20) < VMEM_BYTES_V7X
    return pltpu.CompilerParams(dimension_semantics=sem, vmem_limit_bytes=vmem_mb << 20)


ANY = pl.BlockSpec(memory_space=pl.ANY)


HBM_PIN_BYTES = 1 << 20


def _hbm(a):
    if a.ndim >= 2 and a.size * a.dtype.itemsize >= HBM_PIN_BYTES:
        return pltpu.with_memory_space_constraint(a, pltpu.HBM)
    return a


class _Comm:
    def __init__(self, ins, out_shapes, n_sems, start, finish, mid=None, aliases=None):
        self.ins, self.out_shapes, self.n_sems = list(ins), list(out_shapes), n_sems
        self.start, self.finish, self.mid, self.aliases = start, finish, mid, dict(aliases or {})


class _SemOffset:
    def __init__(self, sems, base):
        self._sems, self._base = sems, base
        self.at = self

    def __getitem__(self, k):
        return self._sems.at[self._base + k]


def _merge_comms(a, b):
    assert a.mid is None and b.mid is None
    n_in, n_out = len(a.ins), len(a.out_shapes)

    def both(fa, fb):
        def run(cin, cout, send, recv):
            fa(cin[:n_in], cout[:n_out], send, recv)
            fb(cin[n_in:], cout[n_out:], _SemOffset(send, a.n_sems), _SemOffset(recv, a.n_sems))
        return run

    aliases = {**a.aliases, **{n_in + k: n_out + v for k, v in b.aliases.items()}}
    return _Comm(a.ins + b.ins, a.out_shapes + b.out_shapes, a.n_sems + b.n_sems, both(a.start, b.start),
                 both(a.finish, b.finish), aliases=aliases)


def _call(body, args, *, grid, in_specs, out_specs, out_shape, name, sem, vmem_mb, scratch_shapes=(), comm=None,
          mid_step=None):
    n_in, n_out, n_scr = len(in_specs), len(out_specs), len(scratch_shapes)
    args = [_hbm(a) for a in args]
    if comm is None:
        outs = pl.pallas_call(body, grid=grid, in_specs=list(in_specs), out_specs=list(out_specs),
                              out_shape=list(out_shape), scratch_shapes=list(scratch_shapes), name=name,
                              compiler_params=_cparams(sem, vmem_mb))(*args)
        return list(outs), []
    ci, co = len(comm.ins), len(comm.out_shapes)
    last = grid[0] - 1

    def wrapped(*refs):
        ins, cin = refs[:n_in], refs[n_in:n_in + ci]
        outs = refs[n_in + ci:n_in + ci + n_out]
        cout = refs[n_in + ci + n_out:n_in + ci + n_out + co]
        scr = refs[n_in + ci + n_out + co:]
        send, recv = scr[n_scr], scr[n_scr + 1]
        step = pl.program_id(0)

        @pl.when(step == 0)
        def _():
            comm.start(cin, cout, send, recv)

        body(*ins, *outs, *scr[:n_scr])
        if comm.mid is not None:
            @pl.when(step == mid_step)
            def _():
                comm.mid(cin, cout, send, recv)

        @pl.when(step == last)
        def _():
            comm.finish(cin, cout, send, recv)

    res = pl.pallas_call(
        wrapped, grid=grid, in_specs=list(in_specs) + [ANY] * ci, out_specs=list(out_specs) + [ANY] * co,
        out_shape=list(out_shape) + comm.out_shapes,
        input_output_aliases={n_in + k: n_out + v for k, v in comm.aliases.items()},
        scratch_shapes=list(scratch_shapes) + [pltpu.SemaphoreType.DMA((comm.n_sems,))] * 2,
        name=name, compiler_params=_cparams(("arbitrary",), vmem_mb))(*args, *[_hbm(a) for a in comm.ins])
    return list(res[:n_out]), list(res[n_out:])


def _run_comm(comm, name):
    ci, co = len(comm.ins), len(comm.out_shapes)

    def body(*refs):
        cin, cout, (send, recv) = refs[:ci], refs[ci:ci + co], refs[ci + co:]
        comm.start(cin, cout, send, recv)
        if comm.mid is not None:
            comm.mid(cin, cout, send, recv)
        comm.finish(cin, cout, send, recv)

    return pl.pallas_call(
        body, in_specs=[ANY] * ci, out_specs=[ANY] * co, out_shape=comm.out_shapes, input_output_aliases=comm.aliases,
        scratch_shapes=[pltpu.SemaphoreType.DMA((comm.n_sems,))] * 2, name=name)(*comm.ins)


def _dot(a, b):
    return jnp.dot(a, b, preferred_element_type=f32)


def _dot_nt(a, b):
    return lax.dot_general(a, b, (((1,), (1,)), ((), ())), preferred_element_type=f32)


def _dot_tn(a, b):
    return lax.dot_general(a, b, (((0,), (0,)), ((), ())), preferred_element_type=f32)


def _sigmoid(x):
    return 1.0 / (1.0 + jnp.exp(-x))


def _low_head_lanes():
    return lax.broadcasted_iota(jnp.int32, (1, 2 * HD), 1) < HD


def _head_blockdiag():
    r = lax.broadcasted_iota(jnp.int32, (2 * HD, 2 * HD), 0) // HD
    c = lax.broadcasted_iota(jnp.int32, (2 * HD, 2 * HD), 1) // HD
    return jnp.where(r == c, 1.0, 0.0).astype(bf16)


def _head_sums(z, bd):
    hi = z.astype(bf16)
    lo = (z - hi.astype(f32)).astype(bf16)
    return _dot(hi, bd) + _dot(lo, bd)


def _weight_cols(start, width):
    kv_width = 2 * NKV * HD
    if start < D:
        orig = start
    elif start < INW - kv_width:
        orig = start + kv_width
    else:
        orig = start - (INW - kv_width) + D
    assert (start < D) == (start + width <= D) and (start < INW - kv_width) == (start + width <= INW - kv_width)
    return slice(orig, orig + width)


def _rms_inproj(x, g, w, perm, gq2, gk2, comm=None):
    T, N = x.shape[0], w.shape[1]
    tn = 512
    conv_cols = (C_A * D, (C_G + 1) * D)
    attn_chunks = [C_Q * D // tn, C_Q * D // tn + 1, C_KV]

    def body(x_ref, g_ref, w_ref, perm_ref, gq_ref, gk_ref, p_ref, u_ref, tail_ref, qn_ref, kk_ref, vv_ref):
        xv = x_ref[...]
        r = lax.rsqrt(jnp.mean(xv * xv, axis=-1, keepdims=True) + EPS)
        u = (xv * r * g_ref[...]).astype(bf16)
        u_ref[...] = u
        u_blocks = _dot(perm_ref[...], u).astype(bf16)

        def project(c):
            lhs = u_blocks if conv_cols[0] <= c * tn < conv_cols[1] else u
            p_ref[:, c * tn:(c + 1) * tn] = _dot(lhs, w_ref[:, _weight_cols(c * tn, tn)])

        for c in attn_chunks:
            project(c)
        bd = _head_blockdiag()
        lo = _low_head_lanes()
        for p in range(NQ // 2):
            z = p_ref[:, C_Q * D + 128 * p:C_Q * D + 128 * p + 128]
            rq = lax.rsqrt(_head_sums(z * z, bd) * (1.0 / HD) + EPS)
            qn_ref[:, 128 * p:128 * p + 128] = (z * rq * gq_ref[...] * (HD ** -0.5)).astype(bf16)
        kv0 = C_KV * tn
        for p in range(NKV // 2):
            z = p_ref[:, kv0 + 128 * p:kv0 + 128 * p + 128]
            rk = lax.rsqrt(_head_sums(z * z, bd) * (1.0 / HD) + EPS)
            _split_pair(z * rk * gk_ref[...], kk_ref, p, lo)
            _split_pair(p_ref[:, kv0 + 256 + 128 * p:kv0 + 256 + 128 * p + 128], vv_ref, p, lo)
        for c in range(N // tn):
            if c not in attn_chunks:
                project(c)
        tail_ref[...] = _dot(u[TT - NBLK:TT], w_ref[:, _weight_cols(conv_cols[0], conv_cols[1] - conv_cols[0])])

    once = pl.Buffered(1)
    row = pl.BlockSpec((TT, D), lambda i: (i, 0))
    vec = pl.BlockSpec((1, 128), lambda i: (0, 0))
    return _call(
        body, (x, g, w, perm, gq2, gk2), grid=(T // TT,),
        in_specs=[row, pl.BlockSpec((1, D), lambda i: (0, 0)),
                  pl.BlockSpec((D, N), lambda i: (0, 0), pipeline_mode=once),
                  pl.BlockSpec((TT, TT), lambda i: (0, 0), pipeline_mode=once), vec, vec],
        out_specs=[pl.BlockSpec((TT, N), lambda i: (i, 0)), row, pl.BlockSpec((NBLK, 2 * D), lambda i: (i, 0)),
                   row, row, row],
        out_shape=[SDS((T, N), f32), SDS((T, D), bf16), SDS((T // TT * NBLK, 2 * D), f32)] + [SDS((T, D), bf16)] * 3,
        name="rms_inproj", sem=("parallel",), vmem_mb=56, comm=comm, mid_step=(3 * (T // TT)) // 4)


def _split_pair(pair, out_ref, p, lo):
    rolled = pltpu.roll(pair, HD, 1)
    zero = jnp.zeros_like(pair)
    c = 512 * p
    out_ref[:, c:c + 128] = jnp.where(lo, pair, zero).astype(bf16)
    out_ref[:, c + 128:c + 256] = jnp.where(lo, zero, rolled).astype(bf16)
    out_ref[:, c + 256:c + 384] = jnp.where(lo, rolled, zero).astype(bf16)
    out_ref[:, c + 384:c + 512] = jnp.where(lo, zero, pair).astype(bf16)


def _bucket_tile():
    qi = np.arange(BLK)[:, None]
    kj = np.arange(BLK)[None, :]
    n = np.where(kj > qi, qi + BLK - kj, qi - kj)
    max_exact = NBUCKET // 2
    nf = np.maximum(n, 1).astype(np.float32)
    large = max_exact + (np.log(nf / max_exact) / np.float32(np.log(128 / max_exact))
                         * (NBUCKET - max_exact)).astype(np.int32)
    large = np.minimum(large, NBUCKET - 1)
    return np.where(n < max_exact, n, large).astype(np.int32)


def _from_prev_block():
    return lax.broadcasted_iota(jnp.int32, (BLK, BLK), 1) > lax.broadcasted_iota(jnp.int32, (BLK, BLK), 0)


def _bias_tiles(rel_bias):
    def body(rb_ref, bk_ref, out_ref):
        bk = bk_ref[...]
        for h in range(NQ):
            acc = jnp.zeros((BLK, BLK), f32)
            for b in range(NBUCKET):
                acc = jnp.where(bk == b, rb_ref[b, h], acc)
            out_ref[h] = acc

    return pl.pallas_call(
        body,
        in_specs=[pl.BlockSpec(memory_space=pltpu.SMEM), pl.BlockSpec(memory_space=pltpu.VMEM)],
        out_specs=pl.BlockSpec(memory_space=pltpu.VMEM),
        out_shape=SDS((NQ, BLK, BLK), f32),
        name="bias_tiles")(rel_bias, jnp.asarray(_bucket_tile()))


def _rows2(ref, c):
    return jnp.concatenate([ref[:, c:c + 128], ref[:, c + 128:c + 256]], axis=0)


def _attn_fwd(qn, kk, vv, bias, sinks, comm=None):
    T = qn.shape[0]
    nb = T // BLK

    def body(s_ref, q_ref, kc_ref, kp_ref, vc_ref, vp_ref, b_ref, o_ref, lse_ref):
        prev = _from_prev_block()
        no_key = jnp.logical_and(prev, pl.program_id(0) == 0)
        scores = []
        for h in range(NKV):
            qs = _rows2(q_ref, 256 * h)
            scores.append((_dot_nt(qs, _rows2(kc_ref, 256 * h)), _dot_nt(qs, _rows2(kp_ref, 256 * h))))
        for h in range(NKV):
            c = 256 * h
            sc, sp = scores[h]
            vstack = jnp.concatenate([vp_ref[:, c:c + 128], vc_ref[:, c:c + 128],
                                      vp_ref[:, c + 128:c + 256], vc_ref[:, c + 128:c + 256]], axis=0)
            for pr in range(2):
                ps = []
                for e in range(2):
                    hq = 4 * h + 2 * pr + e
                    rows, cols = slice(128 * pr, 128 * pr + 128), slice(128 * e, 128 * e + 128)
                    s = jnp.where(no_key, NEG, jnp.where(prev, sp[rows, cols], sc[rows, cols]) + b_ref[hq])
                    sink = s_ref[0, hq]
                    m = jnp.maximum(jnp.max(s, axis=-1, keepdims=True), sink)
                    ex = jnp.exp(s - m)
                    l = jnp.sum(ex, axis=-1, keepdims=True) + jnp.exp(sink - m)
                    p = ex * (1.0 / l)
                    ps += [jnp.where(prev, p, 0.0).astype(bf16), jnp.where(prev, 0.0, p).astype(bf16)]
                    lse_ref[:, hq:hq + 1] = m + jnp.log(l)
                o_ref[:, c + 128 * pr:c + 128 * pr + 128] = _dot(jnp.concatenate(ps, axis=1), vstack).astype(bf16)

    blk = lambda f: pl.BlockSpec((BLK, D), f)
    cur = lambda n: (n, 0)
    prev = lambda n: (jnp.maximum(n - 1, 0), 0)
    return _call(
        body, (sinks, qn, kk, kk, vv, vv, bias), grid=(nb,),
        in_specs=[pl.BlockSpec(memory_space=pltpu.SMEM), blk(cur), blk(cur), blk(prev), blk(cur), blk(prev),
                  pl.BlockSpec((NQ, BLK, BLK), lambda n: (0, 0, 0))],
        out_specs=[blk(cur), pl.BlockSpec((BLK, NQ), cur)],
        out_shape=[SDS((T, D), bf16), SDS((T, NQ), f32)],
        name="attn_fwd", sem=("parallel",), vmem_mb=32, comm=comm, mid_step=(3 * nb) // 4)


TT = 512
NBLK = 32
RPB = TT // NBLK
CONV_LANES = 128
KBLK = 4
GBLK = 8
TAPG = 8


def _block_perm():
    p = np.arange(TT)
    m = np.zeros((TT, TT), np.float32)
    m[p, NBLK * (p % RPB) + p // RPB] = 1.0
    return jnp.asarray(m, bf16), jnp.asarray(m.T, bf16)


def _lane_groups():
    return [slice(q * CONV_LANES, (q + 1) * CONV_LANES) for q in range(D // CONV_LANES)]


def _fill_time_blocks(z, tile, edge, causal):
    row = lax.broadcasted_iota(jnp.int32, (RPB, 1), 0)
    for k in range(NBLK):
        blk = tile[RPB * k:RPB * (k + 1)]
        if causal:
            z[NBLK + k] = blk
            z[k] = jnp.where(row == 0, edge[k:k + 1], pltpu.roll(blk, 1, 0))
        else:
            z[k] = blk
            z[NBLK + k] = jnp.where(row == RPB - 1, edge[k:k + 1], pltpu.roll(blk, RPB - 1, 0))


def _block_conv(z, w_ref, tap_offset, init, store):
    def step(s, carry):
        k0 = s * KBLK
        for ln in _lane_groups():
            accs = [init(ln) for _ in range(KBLK)]
            for g0 in range(0, CW, TAPG):
                taps = range(g0, min(g0 + TAPG, CW))
                lo = min(tap_offset(j) for j in taps)
                hi = max(tap_offset(j) for j in taps)
                win = [z[k0 + lo + d, :, ln] for d in range(KBLK + hi - lo)]
                for j in taps:
                    wv = w_ref[j:j + 1, ln]
                    for q in range(KBLK):
                        accs[q] = accs[q] + win[q + tap_offset(j) - lo] * wv
            for q in range(KBLK):
                store(k0 + q, ln, accs[q])
        return carry

    lax.fori_loop(0, NBLK // KBLK, step, 0)


def _block_rows(k):
    return pl.ds(pl.multiple_of(k * RPB, RPB), RPB)


def _glu_conv_fwd(proj, tail, w_dw, b_dw, ln_g, ln_b, perm_t, comm=None):
    T = proj.shape[0]

    def body(a_ref, g_ref, ta_ref, tg_ref, w_ref, b_ref, lg_ref, lb_ref, pt_ref, h1_ref, h3_ref, z):
        edge = jnp.where(pl.program_id(0) > 0, ta_ref[...] * _sigmoid(tg_ref[...]), 0.0)
        _fill_time_blocks(z, a_ref[...] * _sigmoid(g_ref[...]), edge, causal=True)

        def store(k, ln, value):
            h1_ref[_block_rows(k), ln] = value

        _block_conv(z, w_ref, lambda j: NBLK - (CW - 1) + j,
                    lambda ln: jnp.broadcast_to(b_ref[:, ln], (RPB, CONV_LANES)), store)
        h1 = h1_ref[...]
        mu = jnp.mean(h1, axis=-1, keepdims=True)
        xc = h1 - mu
        var = jnp.mean(xc * xc, axis=-1, keepdims=True)
        h2 = xc * lax.rsqrt(var + EPS) * lg_ref[...] + lb_ref[...]
        h3_ref[...] = _dot(pt_ref[...], (h2 * _sigmoid(h2)).astype(bf16)).astype(bf16)

    tile = lambda cb: pl.BlockSpec((TT, D), lambda i: (i, cb))
    edge = lambda cb: pl.BlockSpec((NBLK, D), lambda i: (jnp.maximum(i - 1, 0), cb))
    vec = pl.BlockSpec((1, D), lambda i: (0, 0))
    (h1, h3), got = _call(
        body, (proj, proj, tail, tail, w_dw, b_dw, ln_g, ln_b, perm_t), grid=(T // TT,),
        in_specs=[tile(C_A), tile(C_G), edge(0), edge(1), pl.BlockSpec((HALO, D), lambda i: (0, 0)), vec, vec, vec,
                  pl.BlockSpec((TT, TT), lambda i: (0, 0), pipeline_mode=pl.Buffered(1))],
        out_specs=[pl.BlockSpec((TT, D), lambda i: (i, 0))] * 2,
        out_shape=[SDS((T, D), f32), SDS((T, D), bf16)],
        scratch_shapes=[pltpu.VMEM((2 * NBLK, RPB, D), f32)],
        name="glu_conv_fwd", sem=("parallel",), vmem_mb=40, comm=comm, mid_step=(3 * (T // TT)) // 4)
    return h1, h3, got


def _mix_out(o, h3, proj, x, w_attn_o, w_conv_out, w_out, g_mlp):
    T = x.shape[0]
    tm = 512

    def body(o_ref, h3_ref, ga_ref, gc_ref, x_ref, wa_ref, wc_ref, wo_ref, g_ref,
             attn_ref, conv_ref, mg_ref, x1_ref, n2_ref):
        x1 = x_ref[...]
        for j in range(D // MIX_CHUNK):
            cols = slice(j * MIX_CHUNK, (j + 1) * MIX_CHUNK)
            attn = _dot(o_ref[...], wa_ref[:, cols])
            conv = _dot(h3_ref[...], wc_ref[:, cols])
            attn_ref[:, cols] = attn.astype(bf16)
            conv_ref[:, cols] = conv.astype(bf16)
            mg = (_sigmoid(ga_ref[:, cols]) * attn + _sigmoid(gc_ref[:, cols]) * conv).astype(bf16)
            mg_ref[:, cols] = mg
            x1 = x1 + _dot(mg, wo_ref[cols, :])
        x1_ref[...] = x1
        r = lax.rsqrt(jnp.mean(x1 * x1, axis=-1, keepdims=True) + EPS)
        n2_ref[...] = (x1 * r * g_ref[...]).astype(bf16)

    tile = lambda cb=0: pl.BlockSpec((tm, D), lambda i: (i, cb))
    wfull = pl.BlockSpec((D, D), lambda i: (0, 0), pipeline_mode=pl.Buffered(1))
    return pl.pallas_call(
        body, grid=(T // tm,),
        in_specs=[tile(), tile(), tile(C_GA), tile(C_GC), tile(), wfull, wfull, wfull,
                  pl.BlockSpec((1, D), lambda i: (0, 0))],
        out_specs=[tile()] * 5,
        out_shape=[SDS((T, D), bf16), SDS((T, D), bf16), SDS((T, D), bf16), SDS((T, D), f32), SDS((T, D), bf16)],
        name="mix_out", compiler_params=_cparams(("parallel",), 48))(o, h3, proj, proj, x, w_attn_o, w_conv_out, w_out, g_mlp)


def _mlp_fwd(n2, w1, w2, x1, tgt):
    T = n2.shape[0]
    tm, tf = 512, 1024

    def body(n2_ref, w1_ref, w2_ref, x1_ref, t_ref, hm_ref, dy_ref, dyb_ref, loss_ref):
        @pl.when(pl.program_id(0) == 0)
        def _():
            loss_ref[...] = jnp.zeros_like(loss_ref)

        n2v = n2_ref[...]
        for c in range(DFF // tf):
            r = jnp.maximum(_dot(n2v, w1_ref[:, c * tf:(c + 1) * tf]), 0.0)
            hm_ref[:, c * tf:(c + 1) * tf] = (r * r).astype(bf16)
        e = x1_ref[...] + _dot(hm_ref[...], w2_ref[...]) - t_ref[...]
        dy = e * (1.0 / D)
        dy_ref[...] = dy
        dyb_ref[...] = dy.astype(bf16)
        loss_ref[...] += 0.5 * jnp.sum(jnp.sum(e * e, axis=-1, keepdims=True) * (1.0 / D))

    row = pl.BlockSpec((tm, D), lambda i: (i, 0))
    once = pl.Buffered(1)
    return pl.pallas_call(
        body, grid=(T // tm,),
        in_specs=[row, pl.BlockSpec((D, DFF), lambda i: (0, 0), pipeline_mode=once),
                  pl.BlockSpec((DFF, D), lambda i: (0, 0), pipeline_mode=once), row, row],
        out_specs=[pl.BlockSpec((tm, DFF), lambda i: (i, 0)), row, row, pl.BlockSpec((8, 128), lambda i: (0, 0))],
        out_shape=[SDS((T, DFF), bf16), SDS((T, D), f32), SDS((T, D), bf16), SDS((8, 128), f32)],
        name="mlp_fwd", compiler_params=_cparams(("arbitrary",), 56))(n2, w1, w2, x1, tgt)


def _rms_bwd(xv, g, dn, dres):
    r = lax.rsqrt(jnp.mean(xv * xv, axis=-1, keepdims=True) + EPS)
    gd = dn * g
    dx = dres + r * gd - xv * (r * r * r) * jnp.mean(xv * gd, axis=-1, keepdims=True)
    dg = jnp.sum(dn * xv * r, axis=0, keepdims=True)
    return dx, dg


def _mlp_bwd(dy, dyb, hmid, w1, w2, x1, g_mlp):
    T = dy.shape[0]
    tm, tf = 512, 1024

    def body(dy_ref, dyb_ref, hm_ref, w1_ref, w2_ref, x1_ref, g_ref, df_ref, dx_ref, dxb_ref, dg_ref):
        @pl.when(pl.program_id(0) == 0)
        def _():
            dg_ref[...] = jnp.zeros_like(dg_ref)

        dyb = dyb_ref[...]
        for c in range(DFF // tf):
            cols = slice(c * tf, (c + 1) * tf)
            d_hm = _dot_nt(dyb, w2_ref[cols, :])
            df_ref[:, cols] = (d_hm * (2.0 * jnp.sqrt(hm_ref[:, cols].astype(f32)))).astype(bf16)
        dn = _dot_nt(df_ref[...], w1_ref[...])
        dx, dg = _rms_bwd(x1_ref[...], g_ref[...], dn, dy_ref[...])
        dx_ref[...] = dx
        dxb_ref[...] = dx.astype(bf16)
        dg_ref[...] += dg

    row = pl.BlockSpec((tm, D), lambda i: (i, 0))
    wide = pl.BlockSpec((tm, DFF), lambda i: (i, 0))
    vec = pl.BlockSpec((1, D), lambda i: (0, 0))
    once = pl.Buffered(1)
    return pl.pallas_call(
        body, grid=(T // tm,),
        in_specs=[row, row, wide, pl.BlockSpec((D, DFF), lambda i: (0, 0), pipeline_mode=once),
                  pl.BlockSpec((DFF, D), lambda i: (0, 0), pipeline_mode=once), row, vec],
        out_specs=[wide, row, row, vec],
        out_shape=[SDS((T, DFF), bf16), SDS((T, D), f32), SDS((T, D), bf16), SDS((1, D), f32)],
        name="mlp_bwd", compiler_params=_cparams(("arbitrary",), 56))(dy, dyb, hmid, w1, w2, x1, g_mlp)


def _wgrad(a, b, name, tn=1024):
    T, M = a.shape
    N = b.shape[1]
    tmm, tk = min(M, 1024), min(T, 2048)

    def body(a_ref, b_ref, o_ref):
        @pl.when(pl.program_id(2) == 0)
        def _():
            o_ref[...] = jnp.zeros_like(o_ref)

        o_ref[...] += _dot_tn(a_ref[...], b_ref[...])

    return pl.pallas_call(
        body, grid=(M // tmm, N // tn, T // tk),
        in_specs=[pl.BlockSpec((tk, tmm), lambda m, n, t: (t, m)), pl.BlockSpec((tk, tn), lambda m, n, t: (t, n))],
        out_specs=pl.BlockSpec((tmm, tn), lambda m, n, t: (m, n)),
        out_shape=SDS((M, N), f32),
        name=name, compiler_params=_cparams(("parallel", "parallel", "arbitrary"), 40))(a, b)


def _mix_bwd(dx1b, proj, attn, conv, h1, w_attn_o, w_conv_out, w_out, ln_g, ln_b, perm, comm=None):
    T = dx1b.shape[0]
    tm = TT

    def body(dx_ref, ga_ref, gc_ref, attn_ref, conv_ref, h1_ref, wa_ref, wc_ref, wo_ref, lg_ref, lb_ref, perm_ref,
             dat_ref, dcv_ref, do_ref, dh1_ref, dga_ref, dgc_ref, acc_ref, head_ref):
        @pl.when(pl.program_id(0) == 0)
        def _():
            acc_ref[...] = jnp.zeros_like(acc_ref)

        d_o, d_h3 = None, None
        for j in range(D // MIX_CHUNK):
            cols = slice(j * MIX_CHUNK, (j + 1) * MIX_CHUNK)
            dm = _dot_nt(dx_ref[...], wo_ref[cols, :])
            sa = _sigmoid(ga_ref[:, cols])
            sc = _sigmoid(gc_ref[:, cols])
            dat = (dm * sa).astype(bf16)
            dcv = (dm * sc).astype(bf16)
            dat_ref[:, cols] = dat
            dcv_ref[:, cols] = dcv
            dga_ref[:, cols] = (dm * attn_ref[:, cols].astype(f32) * sa * (1.0 - sa)).astype(bf16)
            dgc_ref[:, cols] = (dm * conv_ref[:, cols].astype(f32) * sc * (1.0 - sc)).astype(bf16)
            part_o = _dot_nt(dat, wa_ref[:, cols])
            part_h = _dot_nt(_dot(perm_ref[...], dcv).astype(bf16), wc_ref[:, cols])
            d_o = part_o if d_o is None else d_o + part_o
            d_h3 = part_h if d_h3 is None else d_h3 + part_h
        do_ref[...] = d_o.astype(bf16)
        h1 = h1_ref[...]
        mu = jnp.mean(h1, axis=-1, keepdims=True)
        xc = h1 - mu
        rstd = lax.rsqrt(jnp.mean(xc * xc, axis=-1, keepdims=True) + EPS)
        xh = xc * rstd
        h2 = xh * lg_ref[...] + lb_ref[...]
        sg = _sigmoid(h2)
        dh2 = d_h3 * (sg * (1.0 + h2 * (1.0 - sg)))
        dxh = dh2 * lg_ref[...]
        dh1 = rstd * (dxh - jnp.mean(dxh, axis=-1, keepdims=True) - xh * jnp.mean(dxh * xh, axis=-1, keepdims=True))
        dh1_ref[...] = dh1
        for k in range(NBLK):
            head_ref[k:k + 1, :] = dh1[RPB * k:RPB * k + 1]
        acc_ref[0:1, :] += jnp.sum(dh2 * xh, axis=0, keepdims=True)
        acc_ref[1:2, :] += jnp.sum(dh2, axis=0, keepdims=True)
        acc_ref[2:3, :] += jnp.sum(dh1, axis=0, keepdims=True)

    tile = lambda cb=0: pl.BlockSpec((tm, D), lambda i: (i, cb))
    wfull = pl.BlockSpec((D, D), lambda i: (0, 0), pipeline_mode=pl.Buffered(1))
    vec = pl.BlockSpec((1, D), lambda i: (0, 0))
    return _call(
        body, (dx1b, proj, proj, attn, conv, h1, w_attn_o, w_conv_out, w_out, ln_g, ln_b, perm), grid=(T // tm,),
        in_specs=[tile(), tile(C_GA), tile(C_GC), tile(), tile(), tile(), wfull, wfull, wfull, vec, vec,
                  pl.BlockSpec((TT, TT), lambda i: (0, 0), pipeline_mode=pl.Buffered(1))],
        out_specs=[tile()] * 6 + [pl.BlockSpec((8, D), lambda i: (0, 0)), pl.BlockSpec((NBLK, D), lambda i: (i, 0))],
        out_shape=[SDS((T, D), bf16), SDS((T, D), bf16), SDS((T, D), bf16), SDS((T, D), f32),
                   SDS((T, D), bf16), SDS((T, D), bf16), SDS((8, D), f32), SDS((T // RPB, D), f32)],
        name="mix_bwd", sem=("arbitrary",), vmem_mb=56, comm=comm)


def _conv_bwd(dh1, head, proj, tail, w_dw, perm_t, comm=None):
    T = dh1.shape[0]
    nt = T // TT

    def body(d_ref, hd_ref, a_ref, g_ref, ta_ref, tg_ref, w_ref, pt_ref, da_ref, dg_ref, gw_ref, zd, zh, dh0, gacc):
        i = pl.program_id(0)

        @pl.when(i == 0)
        def _():
            gacc[...] = jnp.zeros_like(gacc)

        a = a_ref[...]
        sg = _sigmoid(g_ref[...])
        _fill_time_blocks(zd, d_ref[...], jnp.where(i < nt - 1, hd_ref[...], 0.0), causal=False)
        _fill_time_blocks(zh, a * sg, jnp.where(i > 0, ta_ref[...] * _sigmoid(tg_ref[...]), 0.0), causal=True)

        def store(k, ln, value):
            dh0[_block_rows(k), ln] = value

        _block_conv(zd, w_ref, lambda j: (CW - 1) - j, lambda ln: jnp.zeros((RPB, CONV_LANES), f32), store)

        for ln in _lane_groups():
            for g0 in range(0, CW, TAPG):
                taps = list(range(g0, min(g0 + TAPG, CW)))

                def add_blocks(s, accs, ln=ln, taps=taps):
                    k0 = s * GBLK
                    first = k0 + NBLK - (CW - 1) + taps[0]
                    win = [zh[first + t, :, ln] for t in range(GBLK + len(taps) - 1)]
                    accs = list(accs)
                    for q in range(GBLK):
                        d = zd[k0 + q, :, ln]
                        for n, j in enumerate(taps):
                            accs[n] = accs[n] + d * win[q + j - taps[0]]
                    return tuple(accs)

                accs = lax.fori_loop(0, NBLK // GBLK, add_blocks,
                                     tuple(jnp.zeros((RPB, CONV_LANES), f32) for _ in taps))
                for j, acc in zip(taps, accs):
                    gacc[j, :, ln] += acc

        d0 = dh0[...]
        da_ref[...] = _dot(pt_ref[...], (d0 * sg).astype(bf16)).astype(bf16)
        dg_ref[...] = _dot(pt_ref[...], (d0 * a * sg * (1.0 - sg)).astype(bf16)).astype(bf16)

        @pl.when(i == nt - 1)
        def _():
            gw_ref[...] = jnp.zeros_like(gw_ref)
            for j in range(CW):
                gw_ref[j:j + 1, :] = jnp.sum(gacc[j], axis=0, keepdims=True)

    tile = lambda cb=0: pl.BlockSpec((TT, D), lambda i: (i, cb))
    prev_edge = lambda cb: pl.BlockSpec((NBLK, D), lambda i: (jnp.maximum(i - 1, 0), cb))
    next_edge = pl.BlockSpec((NBLK, D), lambda i: (jnp.minimum(i + 1, nt - 1), 0))
    wspec = pl.BlockSpec((HALO, D), lambda i: (0, 0))
    return _call(
        body, (dh1, head, proj, proj, tail, tail, w_dw, perm_t), grid=(nt,),
        in_specs=[tile(), next_edge, tile(C_A), tile(C_G), prev_edge(0), prev_edge(1), wspec,
                  pl.BlockSpec((TT, TT), lambda i: (0, 0), pipeline_mode=pl.Buffered(1))],
        out_specs=[tile(), tile(), wspec],
        out_shape=[SDS((T, D), bf16), SDS((T, D), bf16), SDS((HALO, D), f32)],
        scratch_shapes=[pltpu.VMEM((2 * NBLK, RPB, D), f32), pltpu.VMEM((2 * NBLK, RPB, D), f32),
                        pltpu.VMEM((TT, D), f32), pltpu.VMEM((HALO, RPB, D), f32)],
        name="conv_bwd", sem=("arbitrary",), vmem_mb=48, comm=comm)


def _attn_bwd(qn, kk, vv, bias, sinks, o, do, lse, comm=None):
    T = qn.shape[0]
    nb = T // BLK

    def body(s_ref, q_ref, kc_ref, kp_ref, vc_ref, vp_ref, b_ref, o_ref, do_ref, lse_ref,
             dq_ref, dkc_ref, dkp_ref, dvc_ref, dvp_ref, dsk_ref, dsa_ref):
        n = pl.program_id(0)

        @pl.when(n == 0)
        def _():
            dsk_ref[...] = jnp.zeros_like(dsk_ref)
            dsa_ref[...] = jnp.zeros_like(dsa_ref)

        @pl.when(n == nb)
        def _():
            dkp_ref[...] = jnp.zeros_like(dkp_ref)
            dvp_ref[...] = jnp.zeros_like(dvp_ref)

        @pl.when(n < nb)
        def _():
            from_prev = _from_prev_block()
            no_key = jnp.logical_and(from_prev, n == 0)
            lo = _low_head_lanes()
            dups = {"kc": [], "kp": [], "vc": [], "vp": []}
            products = []
            for h in range(NKV):
                qs = _rows2(q_ref, 256 * h)
                dos = _rows2(do_ref, 256 * h)
                products.append((qs, dos, _dot_nt(qs, _rows2(kc_ref, 256 * h)), _dot_nt(qs, _rows2(kp_ref, 256 * h)),
                                 _dot_nt(dos, _rows2(vc_ref, 256 * h)), _dot_nt(dos, _rows2(vp_ref, 256 * h))))
            for h in range(NKV):
                c = 256 * h
                qs, dos, sc, sp, dpc, dpp = products[h]
                kstack = jnp.concatenate([kp_ref[:, c:c + 128], kc_ref[:, c:c + 128],
                                          kp_ref[:, c + 128:c + 256], kc_ref[:, c + 128:c + 256]], axis=0)
                p_c, p_p, ds_c, ds_p = [], [], [], []
                for pr in range(2):
                    cc = c + 128 * pr
                    prod = do_ref[:, cc:cc + 128].astype(f32) * o_ref[:, cc:cc + 128].astype(f32)
                    d_lo = jnp.sum(jnp.where(lo, prod, 0.0), axis=-1, keepdims=True)
                    d_hi = jnp.sum(prod, axis=-1, keepdims=True) - d_lo
                    row_pc, row_pp, row_dc, row_dp = [], [], [], []
                    for e in range(2):
                        hq = 4 * h + 2 * pr + e
                        rows, cols = slice(128 * pr, 128 * pr + 128), slice(128 * e, 128 * e + 128)
                        delta = d_lo if e == 0 else d_hi
                        lse = lse_ref[:, hq:hq + 1]
                        s = jnp.where(from_prev, sp[rows, cols], sc[rows, cols]) + b_ref[hq]
                        p = jnp.where(no_key, 0.0, jnp.exp(s - lse))
                        ds = p * (jnp.where(from_prev, dpp[rows, cols], dpc[rows, cols]) - delta)
                        dsa_ref[hq] += ds
                        dsk_ref[hq] += jnp.broadcast_to(-jnp.sum(jnp.exp(s_ref[0, hq] - lse) * delta), (8, 128))
                        row_pc.append(jnp.where(from_prev, 0.0, p).astype(bf16))
                        row_pp.append(jnp.where(from_prev, p, 0.0).astype(bf16))
                        row_dc.append(jnp.where(from_prev, 0.0, ds).astype(bf16))
                        row_dp.append(jnp.where(from_prev, ds, 0.0).astype(bf16))
                    dq_ref[:, cc:cc + 128] = _dot(jnp.concatenate([row_dp[0], row_dc[0], row_dp[1], row_dc[1]], axis=1), kstack)
                    p_c.append(jnp.concatenate(row_pc, axis=1))
                    p_p.append(jnp.concatenate(row_pp, axis=1))
                    ds_c.append(jnp.concatenate(row_dc, axis=1))
                    ds_p.append(jnp.concatenate(row_dp, axis=1))

                def to_keys(m2, rhs):
                    x2 = _dot_tn(jnp.concatenate(m2, axis=0), rhs)
                    x = jnp.where(lo, x2[0:128], x2[128:256])
                    return x + pltpu.roll(x, HD, 1)

                dups["kc"].append(to_keys(ds_c, qs))
                dups["kp"].append(to_keys(ds_p, qs))
                dups["vc"].append(to_keys(p_c, dos))
                dups["vp"].append(to_keys(p_p, dos))
            for key, ref in (("kc", dkc_ref), ("kp", dkp_ref), ("vc", dvc_ref), ("vp", dvp_ref)):
                d = dups[key]
                ref[:, 0:128] = jnp.where(lo, d[0], d[1])
                ref[:, 128:256] = jnp.where(lo, d[2], d[3])

    clamp = lambda n: jnp.minimum(n, nb - 1)
    blk = lambda f: pl.BlockSpec((BLK, D), f)
    cur = lambda n: (clamp(n), 0)
    prev = lambda n: (jnp.maximum(clamp(n) - 1, 0), 0)
    back = lambda n: (jnp.maximum(n - 1, 0), 0)
    kvb = lambda f: pl.BlockSpec((BLK, NKV * HD), f)
    return _call(
        body, (sinks, qn, kk, kk, vv, vv, bias, o, do, lse), grid=(nb + 1,),
        in_specs=[pl.BlockSpec(memory_space=pltpu.SMEM), blk(cur), blk(cur), blk(prev), blk(cur), blk(prev),
                  pl.BlockSpec((NQ, BLK, BLK), lambda n: (0, 0, 0)), blk(cur), blk(cur),
                  pl.BlockSpec((BLK, NQ), cur)],
        out_specs=[blk(cur), kvb(cur), kvb(back), kvb(cur), kvb(back),
                   pl.BlockSpec((NQ, 8, 128), lambda n: (0, 0, 0)),
                   pl.BlockSpec((NQ, BLK, BLK), lambda n: (0, 0, 0))],
        out_shape=[SDS((T, D), f32)] + [SDS((T, NKV * HD), f32)] * 4 + [SDS((NQ, 8, 128), f32), SDS((NQ, BLK, BLK), f32)],
        name="attn_bwd", sem=("arbitrary",), vmem_mb=40, comm=comm)


def _bias_bwd(dsa):
    def body(bk_ref, ds_ref, out_ref):
        bk = bk_ref[...]
        lane = lax.broadcasted_iota(jnp.int32, (1, 128), 1)
        for h in range(NQ):
            ds = ds_ref[h]
            row = jnp.zeros((1, 128), f32)
            for b in range(NBUCKET):
                row = jnp.where(lane == b, jnp.sum(jnp.where(bk == b, ds, 0.0)), row)
            out_ref[h:h + 1, :] = row

    return pl.pallas_call(body, out_shape=SDS((NQ, 128), f32), name="bias_bwd")(jnp.asarray(_bucket_tile()), dsa)


def _qkv_bwd(proj, gq2, gk2, dqn, dkc, dkp, dvc, dvp, comm=None):
    T = proj.shape[0]
    tm = 512

    def body(q_ref, kv_ref, gq_ref, gk_ref, dq_ref, dkc_ref, dkp_ref, dvc_ref, dvp_ref,
             oq_ref, okv_ref, ggq_ref, ggk_ref):
        @pl.when(pl.program_id(0) == 0)
        def _():
            ggq_ref[...] = jnp.zeros_like(ggq_ref)
            ggk_ref[...] = jnp.zeros_like(ggk_ref)

        bd = _head_blockdiag()

        def norm_bwd(z, dy, g, scale):
            r = lax.rsqrt(_head_sums(z * z, bd) * (1.0 / HD) + EPS)
            gd = dy * g * scale
            dz = r * gd - z * (r * r * r) * _head_sums(z * gd, bd) * (1.0 / HD)
            return dz, jnp.sum(dy * scale * z * r, axis=0, keepdims=True)

        gq = jnp.zeros((1, 128), f32)
        for p in range(NQ // 2):
            ln = slice(128 * p, 128 * p + 128)
            dz, dg = norm_bwd(q_ref[:, ln], dq_ref[:, ln], gq_ref[...], HD ** -0.5)
            oq_ref[:, ln] = dz.astype(bf16)
            gq = gq + dg
        ggq_ref[...] += gq + pltpu.roll(gq, HD, 1)
        gk = jnp.zeros((1, 128), f32)
        for p in range(NKV // 2):
            ln = slice(128 * p, 128 * p + 128)
            dz, dg = norm_bwd(kv_ref[:, ln], dkc_ref[:, ln] + dkp_ref[:, ln], gk_ref[...], 1.0)
            okv_ref[:, ln] = dz.astype(bf16)
            gk = gk + dg
        ggk_ref[...] += gk + pltpu.roll(gk, HD, 1)
        okv_ref[:, 256:512] = (dvc_ref[...] + dvp_ref[...]).astype(bf16)

    vec = pl.BlockSpec((1, 128), lambda i: (0, 0))
    kvb = pl.BlockSpec((tm, NKV * HD), lambda i: (i, 0))
    return _call(
        body, (proj, proj, gq2, gk2, dqn, dkc, dkp, dvc, dvp), grid=(T // tm,),
        in_specs=[pl.BlockSpec((tm, D), lambda i: (i, C_Q)), pl.BlockSpec((tm, 512), lambda i: (i, C_KV)), vec, vec,
                  pl.BlockSpec((tm, D), lambda i: (i, 0)), kvb, kvb, kvb, kvb],
        out_specs=[pl.BlockSpec((tm, D), lambda i: (i, 0)), pl.BlockSpec((tm, 512), lambda i: (i, 0)), vec, vec],
        out_shape=[SDS((T, D), bf16), SDS((T, 512), bf16), SDS((1, 128), f32), SDS((1, 128), f32)],
        name="qkv_bwd", sem=("arbitrary",), vmem_mb=32, comm=comm)


def _inproj_bwd(pieces, w_in, x, dx1, g_mix, comm=None):
    T = x.shape[0]
    tm = 512
    widths = [p.shape[1] for p in pieces]
    offs = [sum(widths[:i]) for i in range(len(widths))]
    assert sum(widths) == INW

    def body(*refs):
        p_refs, (w_ref, x_ref, dx1_ref, g_ref, dx_ref, dg_ref) = refs[:len(pieces)], refs[len(pieces):]

        @pl.when(pl.program_id(0) == 0)
        def _():
            dg_ref[...] = jnp.zeros_like(dg_ref)

        du = None
        for p_ref, off, wd in zip(p_refs, offs, widths):
            part = _dot_nt(p_ref[...], w_ref[:, _weight_cols(off, wd)])
            du = part if du is None else du + part
        dx, dg = _rms_bwd(x_ref[...], g_ref[...], du, dx1_ref[...])
        dx_ref[...] = dx
        dg_ref[...] += dg

    row = pl.BlockSpec((tm, D), lambda i: (i, 0))
    vec = pl.BlockSpec((1, D), lambda i: (0, 0))
    return _call(
        body, (*pieces, w_in, x, dx1, g_mix), grid=(T // tm,),
        in_specs=[pl.BlockSpec((tm, wd), lambda i: (i, 0)) for wd in widths]
        + [pl.BlockSpec((D, INW), lambda i: (0, 0), pipeline_mode=pl.Buffered(1)), row, row, vec],
        out_specs=[row, vec],
        out_shape=[SDS((T, D), f32), SDS((1, D), f32)],
        name="inproj_bwd", sem=("arbitrary",), vmem_mb=48, comm=comm)


def _forward_backward(x, tgt, w, placed, chip_core):
    def sums(names, grads, got):
        res = [_pair_sum(nm, grads[nm], got_nm, chip_core) for nm, got_nm in zip(names, got)]
        return {nm: r[0] for nm, r in zip(names, res)}, {nm: r[1] for nm, r in zip(names, res)}

    first = ["w_in", "w_dw"]
    w_in, w_dw = _run_comm(_gather_comm({nm: placed[nm] for nm in first}), "gather_first")
    gq2 = jnp.tile(w["q_norm_g"], (1, 2))
    gk2 = jnp.tile(w["k_norm_g"], (1, 2))
    def gathered_in(names):
        return names, _gather_comm({nm: placed[nm] for nm in names})

    full = {}
    perm, perm_t = _block_perm()
    names, comm = gathered_in(["w_out", "w_attn_o", "w_conv_out"])
    (proj, u, tail, qn, kk, vv), got = _rms_inproj(x, w["norm_mix_g"], w_in, perm, gq2, gk2, comm=comm)
    full.update(zip(names, got))
    bias = _bias_tiles(w["rel_bias"])
    names, comm = gathered_in(["w_ff1"])
    (o, lse), got = _attn_fwd(qn, kk, vv, bias, w["attn_sinks"], comm=comm)
    full.update(zip(names, got))
    names, comm = gathered_in(["w_ff2"])
    h1, h3, got = _glu_conv_fwd(proj, tail, w_dw, w["b_dw"], w["conv_ln_g"], w["conv_ln_b"], perm_t, comm=comm)
    full.update(zip(names, got))
    attn, conv, merged, x1, n2 = _mix_out(o, h3, proj, x, full["w_attn_o"], full["w_conv_out"], full["w_out"],
                                          w["norm_mlp_g"])
    hmid, dy, dyb, loss = _mlp_fwd(n2, full["w_ff1"], full["w_ff2"], x1, tgt)

    g = {}
    df1, dx1, dx1b, g["norm_mlp_g"] = _mlp_bwd(dy, dyb, hmid, full["w_ff1"], full["w_ff2"], x1, w["norm_mlp_g"])
    ff = ["w_ff1", "w_ff2"]
    gff = {"w_ff2": _wgrad(hmid, dyb, "wgrad_ff2"), "w_ff1": _wgrad(n2, df1, "wgrad_ff1")}
    (dat, dcv, do, dh1, dga, dgc, lnacc, head), got = _mix_bwd(
        dx1b, proj, attn, conv, h1, full["w_attn_o"], full["w_conv_out"], full["w_out"], w["conv_ln_g"],
        w["conv_ln_b"], perm, comm=_pair_exchange_comm(gff, ff))
    g["conv_ln_g"], g["conv_ln_b"], g["b_dw"] = lnacc[0:1], lnacc[1:2], lnacc[2:3]
    cp_ff, own_ff = sums(ff, gff, got)
    sq = ["w_out", "w_attn_o", "w_conv_out"]
    gsq = {"w_out": _wgrad(merged, dx1b, "wgrad_out"), "w_attn_o": _wgrad(o, dat, "wgrad_attn_o"),
           "w_conv_out": _wgrad(h3, dcv, "wgrad_conv_out")}
    (da, dg, g["w_dw"]), got = _conv_bwd(dh1, head, proj, tail, w_dw, perm_t, comm=_merge_comms(
        _pair_exchange_comm(gsq, sq), _chip_exchange_comm(cp_ff, ff)))
    cp_sq, own_sq = sums(sq, gsq, got[:len(sq)])
    tot_ff = {nm: _chip_sum(nm, own_ff[nm], rc_nm, chip_core) for nm, rc_nm in zip(ff, got[len(sq):])}
    (dqn, dkc, dkp, dvc, dvp, dsk, dsa), got = _attn_bwd(qn, kk, vv, bias, w["attn_sinks"], o, do, lse, comm=_merge_comms(
        _chip_exchange_comm(cp_sq, sq), _pair_share_comm(tot_ff, ff)))
    tot_sq = {nm: _chip_sum(nm, own_sq[nm], rc_nm, chip_core) for nm, rc_nm in zip(sq, got[:len(sq)])}
    shards = dict(zip(ff, got[len(sq):]))
    g["attn_sinks"] = dsk[:, 0, 0].reshape(1, NQ)
    g["rel_bias"] = _bias_bwd(dsa)[:, 0:NBUCKET].T
    (dq, dkv, ggq, ggk), got = _qkv_bwd(proj, gq2, gk2, dqn, dkc, dkp, dvc, dvp, comm=_pair_share_comm(tot_sq, sq))
    shards.update(zip(sq, got))
    g["q_norm_g"], g["k_norm_g"] = ggq[:, 0:HD], ggk[:, 0:HD]
    pieces = [dq, da, dg, dga, dgc, dkv]
    names = ["q", "a", "g", "ga", "gc", "kv"]
    gw = {nm: _wgrad(u, p, "wgrad_in_" + nm, tn=p.shape[1] if p.shape[1] < 1024 else 1024) for nm, p in zip(names, pieces)}
    gin = {"w_in": jnp.concatenate([gw["q"], gw["kv"], gw["a"], gw["g"], gw["ga"], gw["gc"]], axis=1)}
    got = _run_comm(_pair_exchange_comm(gin, ["w_in"]), "rs_pair_exchange_in")
    cp_in, own_in = sums(["w_in"], gin, got)
    (grad_x, g["norm_mix_g"]), rc = _inproj_bwd(pieces, w_in, x, dx1, w["norm_mix_g"],
                                                comm=_chip_exchange_comm(cp_in, ["w_in"]))
    tot = {"w_in": _chip_sum("w_in", own_in["w_in"], rc[0], chip_core)}
    shards["w_in"] = _run_comm(_pair_share_comm(tot, ["w_in"]), "rs_pair_share_in")[0]
    return loss[0, 0], grad_x, g, shards


BIG = ["w_in", "w_attn_o", "w_conv_out", "w_out", "w_ff1", "w_ff2"]
SHARD_AXIS = {"w_in": 1, "w_attn_o": 0, "w_conv_out": 0, "w_out": 0, "w_ff1": 1, "w_ff2": 0, "w_dw": 1}
SHARD_SHAPE = {"w_in": (D, INW // 4), "w_attn_o": (D // 4, D), "w_conv_out": (D // 4, D), "w_out": (D // 4, D),
               "w_ff1": (D, DFF // 4), "w_ff2": (DFF // 4, D), "w_dw": (HALO, D // 4)}


def _position():
    x, y, c = lax.axis_index("x"), lax.axis_index("y"), lax.axis_index("c")
    other_chips = [(1 - x, y), (x, 1 - y), (1 - x, 1 - y)]
    return x, y, c, 2 * x + y, other_chips


def _shard_window(name, full_ref, s, half=None):
    R, C = SHARD_SHAPE[name]
    r0, nr = (0, R) if half is None else (half * (R // 2), R // 2)
    if SHARD_AXIS[name] == 1:
        return full_ref.at[pl.ds(r0, nr), pl.ds(s * C, C)]
    return full_ref.at[pl.ds(s * R + r0, nr), :]


def _remote(src, dst, send_sems, recv_sems, k, device):
    return pltpu.make_async_remote_copy(src_ref=src, dst_ref=dst, send_sem=send_sems.at[k], recv_sem=recv_sems.at[k],
                                        device_id=device, device_id_type=MESH)


def _full_shape(nm):
    R, C = SHARD_SHAPE[nm]
    return (R, 4 * C) if SHARD_AXIS[nm] == 1 else (4 * R, C)


def _place_shard(nm, shard, chip_arr, dtype):
    R, C = SHARD_SHAPE[nm]
    tr = min(R, 256)
    if SHARD_AXIS[nm] == 1:
        o_map = lambda i, ch: (i, ch[0])
    else:
        o_map = lambda i, ch: (ch[0] * (R // tr) + i, 0)

    def body(ch_ref, s_ref, o_ref):
        o_ref[...] = s_ref[...].astype(dtype)

    return pl.pallas_call(
        body,
        grid_spec=pltpu.PrefetchScalarGridSpec(
            num_scalar_prefetch=1, grid=(R // tr,),
            in_specs=[pl.BlockSpec((tr, C), lambda i, ch: (i, 0))], out_specs=pl.BlockSpec((tr, C), o_map)),
        out_shape=SDS(_full_shape(nm), dtype), name="place_" + nm,
        compiler_params=_cparams(("parallel",), 32))(chip_arr, shard)


def _gather_comm(placed):
    names = list(placed)
    n = len(names)

    def copies(cout, send, recv):
        x, y, c, chip, chips = _position()
        for a, nm in enumerate(names):
            for j, (cx, cy) in enumerate(chips):
                def ici(s, a=a, nm=nm, j=j, cx=cx, cy=cy):
                    w = _shard_window(nm, cout[a], s, c)
                    return _remote(w, w, send, recv, 6 * a + j, (cx, cy, c))

                def d2d(h, a=a, nm=nm, j=j, cx=cx, cy=cy):
                    w = _shard_window(nm, cout[a], 2 * cx + cy, h)
                    return _remote(w, w, send, recv, 6 * a + 3 + j, (x, y, 1 - c))

                yield ici, d2d, chip, 2 * cx + cy, c

    def start(cin, cout, send, recv):
        for ici, d2d, chip, s, c in copies(cout, send, recv):
            ici(chip).start()

    def mid(cin, cout, send, recv):
        for ici, d2d, chip, s, c in copies(cout, send, recv):
            ici(s).wait_recv()
            d2d(c).start()

    def finish(cin, cout, send, recv):
        for ici, d2d, chip, s, c in copies(cout, send, recv):
            d2d(1 - c).wait_recv()
        for ici, d2d, chip, s, c in copies(cout, send, recv):
            ici(chip).wait_send()
            d2d(c).wait_send()

    return _Comm([placed[nm] for nm in names], [SDS(placed[nm].shape, placed[nm].dtype) for nm in names], 6 * n,
                 start, finish, mid, aliases={a: a for a in range(n)})


def _half_rows(nm):
    return SHARD_SHAPE[nm][0] // 2


RS_TILE = 128


def _exchange_comm(ins, out_shapes, copies, n_sems, aliases=None):
    def start(cin, cout, send, recv):
        for cp in copies(cin, cout, send, recv):
            cp.start()

    def finish(cin, cout, send, recv):
        for cp in copies(cin, cout, send, recv):
            cp.wait()

    return _Comm(ins, out_shapes, n_sems, start, finish, aliases=aliases)


def _pair_exchange_comm(grads, names):
    def copies(cin, cout, send, recv):
        x, y, c, chip, chips = _position()
        return [_remote(_shard_window(nm, cin[a], s, 1 - c), cout[a].at[s], send, recv, 4 * a + s, (x, y, 1 - c))
                for a, nm in enumerate(names) for s in range(4)]

    return _exchange_comm([grads[nm] for nm in names],
                          [SDS((4, _half_rows(nm), SHARD_SHAPE[nm][1]), f32) for nm in names], copies, 4 * len(names))


def _pair_sum(nm, g, got, chip_core):
    R, C = SHARD_SHAPE[nm]
    hr = R // 2
    nt = hr // RS_TILE
    if SHARD_AXIS[nm] == 1:
        g_map = lambda i, s, sc: (sc[1] * nt + i, s)
    else:
        g_map = lambda i, s, sc: (s * (R // RS_TILE) + sc[1] * nt + i, 0)

    def body(sc_ref, g_ref, got_ref, o16_ref, own_ref):
        v = g_ref[...] + got_ref[0]
        o16_ref[0] = v.astype(bf16)

        @pl.when(pl.program_id(1) == sc_ref[0])
        def _():
            own_ref[...] = v

    blk3 = pl.BlockSpec((1, RS_TILE, C), lambda i, s, sc: (s, i, 0))
    return pl.pallas_call(
        body,
        grid_spec=pltpu.PrefetchScalarGridSpec(
            num_scalar_prefetch=1, grid=(nt, 4),
            in_specs=[pl.BlockSpec((RS_TILE, C), g_map), blk3],
            out_specs=[blk3, pl.BlockSpec((RS_TILE, C), lambda i, s, sc: (i, 0))]),
        out_shape=[SDS((4, hr, C), bf16), SDS((hr, C), f32)], name="rs_pair_sum_" + nm,
        compiler_params=_cparams(("parallel", "arbitrary"), 32))(chip_core, g, got)


def _chip_exchange_comm(cp, names):
    def copies(cin, cout, send, recv):
        x, y, c, chip, chips = _position()
        return [_remote(cin[a].at[2 * cx + cy], cout[a].at[j], send, recv, 3 * a + j, (cx, cy, c))
                for a, nm in enumerate(names) for j, (cx, cy) in enumerate(chips)]

    return _exchange_comm([cp[nm] for nm in names],
                          [SDS((3, _half_rows(nm), SHARD_SHAPE[nm][1]), bf16) for nm in names], copies, 3 * len(names))


def _chip_sum(nm, own, rc, chip_core):
    R, C = SHARD_SHAPE[nm]
    nt = (R // 2) // RS_TILE

    def body(sc_ref, own_ref, rc_ref, o_ref):
        o_ref[...] = own_ref[...] + rc_ref[0].astype(f32) + rc_ref[1].astype(f32) + rc_ref[2].astype(f32)

    return pl.pallas_call(
        body,
        grid_spec=pltpu.PrefetchScalarGridSpec(
            num_scalar_prefetch=1, grid=(nt,),
            in_specs=[pl.BlockSpec((RS_TILE, C), lambda i, sc: (i, 0)),
                      pl.BlockSpec((3, RS_TILE, C), lambda i, sc: (0, i, 0))],
            out_specs=pl.BlockSpec((RS_TILE, C), lambda i, sc: (sc[1] * nt + i, 0))),
        out_shape=SDS((R, C), f32), name="rs_chip_sum_" + nm,
        compiler_params=_cparams(("parallel",), 32))(chip_core, own, rc)


def _pair_share_comm(tot, names):
    def copies(cin, cout, send, recv):
        x, y, c, chip, chips = _position()
        cps = []
        for a, nm in enumerate(names):
            hr = _half_rows(nm)
            mine = cout[a].at[pl.ds(c * hr, hr), :]
            cps.append(_remote(mine, mine, send, recv, a, (x, y, 1 - c)))
        return cps

    return _exchange_comm([tot[nm] for nm in names], [SDS(SHARD_SHAPE[nm], f32) for nm in names], copies, len(names),
                          aliases={a: a for a in range(len(names))})


SMALL_ROWS = 40


def _allreduce_small(block):
    def body(x_ref, out_ref, buf, send_sems, recv_sems, local_sem):
        x, y, c, chip, chips = _position()
        me, sibling = (x, y, c), (x, y, 1 - c)

        def slot(px, py, pc):
            return buf.at[4 * px + 2 * py + pc]

        def copy(k, block_of, to, src=None):
            return _remote(slot(*block_of) if src is None else src, slot(*block_of), send_sems, recv_sems, k, to)

        mine = pltpu.make_async_copy(x_ref, slot(*me), local_sem)
        mine.start()
        first = [copy(0, me, sibling, src=x_ref)] + [copy(1 + j, me, (*ch, c), src=x_ref) for j, ch in enumerate(chips)]
        for cp in first:
            cp.start()
        passed = [copy(4 + j, (*ch, c), sibling) for j, ch in enumerate(chips)]
        for j, ch in enumerate(chips):
            copy(1 + j, (*ch, c), me).wait_recv()
            passed[j].start()
        copy(0, sibling, me).wait_recv()
        for j, ch in enumerate(chips):
            copy(4 + j, (*ch, 1 - c), me).wait_recv()
        for cp in first + passed:
            cp.wait_send()
        mine.wait()
        acc = buf[0]
        for d in range(1, 8):
            acc = acc + buf[d]
        out_ref[...] = acc

    vm = pl.BlockSpec(memory_space=pltpu.VMEM)
    return pl.pallas_call(
        body, in_specs=[vm], out_specs=vm, out_shape=SDS((SMALL_ROWS, D), f32),
        scratch_shapes=[pltpu.VMEM((8, SMALL_ROWS, D), f32), pltpu.SemaphoreType.DMA((7,)), pltpu.SemaphoreType.DMA((7,)),
                        pltpu.SemaphoreType.DMA],
        name="allreduce_small")(block)


def _adamw(w, g, m, v, name):
    rows, cols = w.shape
    tr = 256 if rows % 256 == 0 else rows

    def body(w_ref, g_ref, m_ref, v_ref, d_ref, nm_ref, nv_ref):
        gv = g_ref[...]
        m2 = ADAM_B1 * m_ref[...] + (1.0 - ADAM_B1) * gv
        v2 = ADAM_B2 * v_ref[...] + (1.0 - ADAM_B2) * jnp.square(gv)
        m_hat = m2 / (1.0 - ADAM_B1 ** ADAM_STEP)
        v_hat = v2 / (1.0 - ADAM_B2 ** ADAM_STEP)
        d_ref[...] = -ADAM_LR * (m_hat / (jnp.sqrt(v_hat) + ADAM_EPS) + ADAM_WD * w_ref[...])
        nm_ref[...] = m2
        nv_ref[...] = v2

    spec = pl.BlockSpec((tr, cols), lambda i: (i, 0))
    return pl.pallas_call(body, grid=(rows // tr,), in_specs=[spec] * 4, out_specs=[spec] * 3,
                          out_shape=[SDS((rows, cols), f32)] * 3, name=name,
                          compiler_params=_cparams(("parallel",), 40))(w, g, m, v)


WEIGHTS = ["norm_mix_g", "w_in", "q_norm_g", "k_norm_g", "attn_sinks", "rel_bias", "w_attn_o", "w_dw", "b_dw",
           "conv_ln_g", "conv_ln_b", "w_conv_out", "w_out", "norm_mlp_g", "w_ff1", "w_ff2"]
ROW_VECS = ["norm_mix_g", "b_dw", "conv_ln_g", "conv_ln_b", "norm_mlp_g"]
MISC_ROW = 5
W_DW_ROW = 8


def _pack_small(vals, loss=None):
    misc = [vals["q_norm_g"].reshape(1, HD), vals["k_norm_g"].reshape(1, HD), vals["attn_sinks"].reshape(1, NQ),
            jnp.zeros((1, 1), f32) if loss is None else loss.reshape(1, 1), jnp.zeros((1, 111), f32),
            vals["rel_bias"].reshape(1, NBUCKET * NQ), jnp.zeros((1, 256), f32)]
    rows = [vals[nm].reshape(1, D) for nm in ROW_VECS] + [jnp.concatenate(misc, axis=1), jnp.zeros((2, D), f32)]
    return jnp.concatenate(rows, axis=0)


def _unpack_small(block):
    out = {nm: block[i:i + 1] for i, nm in enumerate(ROW_VECS)}
    misc = block[MISC_ROW]
    out["q_norm_g"] = misc[0:64].reshape(1, HD)
    out["k_norm_g"] = misc[64:128].reshape(1, HD)
    out["attn_sinks"] = misc[128:144].reshape(1, NQ)
    out["rel_bias"] = misc[256:768].reshape(NBUCKET, NQ)
    return out, misc[144]


def kernel(x, norm_mix_g, w_in, q_norm_g, k_norm_g, attn_sinks, rel_bias, w_attn_o, w_dw, b_dw, conv_ln_g, conv_ln_b, w_conv_out, w_out, norm_mlp_g, w_ff1, w_ff2, loss_target, m_norm_mix_g, m_w_in, m_q_norm_g, m_k_norm_g, m_attn_sinks, m_rel_bias, m_w_attn_o, m_w_dw, m_b_dw, m_conv_ln_g, m_conv_ln_b, m_w_conv_out, m_w_out, m_norm_mlp_g, m_w_ff1, m_w_ff2, v_norm_mix_g, v_w_in, v_q_norm_g, v_k_norm_g, v_attn_sinks, v_rel_bias, v_w_attn_o, v_w_dw, v_b_dw, v_conv_ln_g, v_conv_ln_b, v_w_conv_out, v_w_out, v_norm_mlp_g, v_w_ff1, v_w_ff2):
    args = dict(locals())
    wts = {nm: args[nm] for nm in WEIGHTS}
    mom = {nm: args["m_" + nm] for nm in WEIGHTS}
    var = {nm: args["v_" + nm] for nm in WEIGHTS}
    chip = 2 * lax.axis_index("x") + lax.axis_index("y")

    chip_arr = jnp.reshape(chip, (1,)).astype(jnp.int32)
    chip_core = jnp.stack([chip, lax.axis_index("c")]).astype(jnp.int32)
    placed = {nm: _place_shard(nm, wts[nm][0], chip_arr, bf16) for nm in BIG}
    placed["w_dw"] = _place_shard("w_dw", jnp.pad(w_dw[0], ((0, 1), (0, 0))), chip_arr, f32)

    loss_part, grad_x, g, shards = _forward_backward(x[0], loss_target[0], wts, placed, chip_core)

    small = jnp.concatenate([_pack_small(g, loss_part), g["w_dw"]], axis=0)
    small = _allreduce_small(small)
    grads, loss = _unpack_small(small)
    grads["w_dw"] = lax.dynamic_slice(small[W_DW_ROW:W_DW_ROW + CW], (0, chip * (D // 4)), (CW, D // 4))
    grads.update(shards)

    delta, new_m, new_v = {}, {}, {}
    sd, sm, sv = _adamw(_pack_small(wts), small[0:8], _pack_small(mom), _pack_small(var), "adamw_small")
    for res, blk in ((delta, sd), (new_m, sm), (new_v, sv)):
        res.update(_unpack_small(blk)[0])
    for nm in BIG + ["w_dw"]:
        shp = wts[nm].shape
        two_d = lambda a: a.reshape(shp[-2], shp[-1])
        delta[nm], new_m[nm], new_v[nm] = _adamw(two_d(wts[nm]), grads[nm], two_d(mom[nm]), two_d(var[nm]), "adamw_" + nm)

    def shaped(vals):
        return [vals[nm].reshape(wts[nm].shape) for nm in WEIGHTS]

    return (loss, grad_x[None], *shaped(grads), *shaped(delta), *shaped(new_m), *shaped(new_v))
```

```python
import functools

import numpy as np
import jax
import jax.numpy as jnp
from jax import lax
from jax.experimental import pallas as pl
from jax.experimental.pallas import tpu as pltpu

f32 = jnp.float32
bf16 = jnp.bfloat16
SDS = jax.ShapeDtypeStruct
MESH = pl.DeviceIdType.MESH

D = 1024
HD = 64
NQ = 16
NKV = 4
BLK = 128
CW = 31
HALO = 32
DFF = 4096
NBUCKET = 32
EPS = 1e-6
NEG = -1e30
INW = 5632
MIX_CHUNK = 256
C_Q, C_A, C_G, C_GA, C_GC = 0, 1, 2, 3, 4
C_KV = 10

ADAM_LR = 0.001
ADAM_B1 = 0.9
ADAM_B2 = 0.999
ADAM_EPS = 1e-08
ADAM_WD = 0.01
ADAM_STEP = 10

VMEM_BYTES_V7X = 64 << 20


def _cparams(sem, vmem_mb):
    assert (vmem_mb << 20) < VMEM_BYTES_V7X
    return pltpu.CompilerParams(dimension_semantics=sem, vmem_limit_bytes=vmem_mb << 20)


ANY = pl.BlockSpec(memory_space=pl.ANY)


HBM_PIN_BYTES = 1 << 20


def _hbm(a):
    if a.ndim >= 2 and a.size * a.dtype.itemsize >= HBM_PIN_BYTES:
        return pltpu.with_memory_space_constraint(a, pltpu.HBM)
    return a


class _Comm:
    def __init__(self, ins, out_shapes, n_sems, start, finish, mid=None, aliases=None):
        self.ins, self.out_shapes, self.n_sems = list(ins), list(out_shapes), n_sems
        self.start, self.finish, self.mid, self.aliases = start, finish, mid, dict(aliases or {})


class _SemOffset:
    def __init__(self, sems, base):
        self._sems, self._base = sems, base
        self.at = self

    def __getitem__(self, k):
        return self._sems.at[self._base + k]


def _merge_comms(a, b):
    assert a.mid is None and b.mid is None
    n_in, n_out = len(a.ins), len(a.out_shapes)

    def both(fa, fb):
        def run(cin, cout, send, recv):
            fa(cin[:n_in], cout[:n_out], send, recv)
            fb(cin[n_in:], cout[n_out:], _SemOffset(send, a.n_sems), _SemOffset(recv, a.n_sems))
        return run

    aliases = {**a.aliases, **{n_in + k: n_out + v for k, v in b.aliases.items()}}
    return _Comm(a.ins + b.ins, a.out_shapes + b.out_shapes, a.n_sems + b.n_sems, both(a.start, b.start),
                 both(a.finish, b.finish), aliases=aliases)


def _call(body, args, *, grid, in_specs, out_specs, out_shape, name, sem, vmem_mb, scratch_shapes=(), comm=None,
          mid_step=None):
    n_in, n_out, n_scr = len(in_specs), len(out_specs), len(scratch_shapes)
    args = [_hbm(a) for a in args]
    if comm is None:
        outs = pl.pallas_call(body, grid=grid, in_specs=list(in_specs), out_specs=list(out_specs),
                              out_shape=list(out_shape), scratch_shapes=list(scratch_shapes), name=name,
                              compiler_params=_cparams(sem, vmem_mb))(*args)
        return list(outs), []
    ci, co = len(comm.ins), len(comm.out_shapes)
    last = grid[0] - 1

    def wrapped(*refs):
        ins, cin = refs[:n_in], refs[n_in:n_in + ci]
        outs = refs[n_in + ci:n_in + ci + n_out]
        cout = refs[n_in + ci + n_out:n_in + ci + n_out + co]
        scr = refs[n_in + ci + n_out + co:]
        send, recv = scr[n_scr], scr[n_scr + 1]
        step = pl.program_id(0)

        @pl.when(step == 0)
        def _():
            comm.start(cin, cout, send, recv)

        body(*ins, *outs, *scr[:n_scr])
        if comm.mid is not None:
            @pl.when(step == mid_step)
            def _():
                comm.mid(cin, cout, send, recv)

        @pl.when(step == last)
        def _():
            comm.finish(cin, cout, send, recv)

    res = pl.pallas_call(
        wrapped, grid=grid, in_specs=list(in_specs) + [ANY] * ci, out_specs=list(out_specs) + [ANY] * co,
        out_shape=list(out_shape) + comm.out_shapes,
        input_output_aliases={n_in + k: n_out + v for k, v in comm.aliases.items()},
        scratch_shapes=list(scratch_shapes) + [pltpu.SemaphoreType.DMA((comm.n_sems,))] * 2,
        name=name, compiler_params=_cparams(("arbitrary",), vmem_mb))(*args, *[_hbm(a) for a in comm.ins])
    return list(res[:n_out]), list(res[n_out:])


def _run_comm(comm, name):
    ci, co = len(comm.ins), len(comm.out_shapes)

    def body(*refs):
        cin, cout, (send, recv) = refs[:ci], refs[ci:ci + co], refs[ci + co:]
        comm.start(cin, cout, send, recv)
        if comm.mid is not None:
            comm.mid(cin, cout, send, recv)
        comm.finish(cin, cout, send, recv)

    return pl.pallas_call(
        body, in_specs=[ANY] * ci, out_specs=[ANY] * co, out_shape=comm.out_shapes, input_output_aliases=comm.aliases,
        scratch_shapes=[pltpu.SemaphoreType.DMA((comm.n_sems,))] * 2, name=name)(*comm.ins)


def _dot(a, b):
    return jnp.dot(a, b, preferred_element_type=f32)


def _dot_nt(a, b):
    return lax.dot_general(a, b, (((1,), (1,)), ((), ())), preferred_element_type=f32)


def _dot_tn(a, b):
    return lax.dot_general(a, b, (((0,), (0,)), ((), ())), preferred_element_type=f32)


def _sigmoid(x):
    return 1.0 / (1.0 + jnp.exp(-x))


def _low_head_lanes():
    return lax.broadcasted_iota(jnp.int32, (1, 2 * HD), 1) < HD


def _head_blockdiag():
    r = lax.broadcasted_iota(jnp.int32, (2 * HD, 2 * HD), 0) // HD
    c = lax.broadcasted_iota(jnp.int32, (2 * HD, 2 * HD), 1) // HD
    return jnp.where(r == c, 1.0, 0.0).astype(bf16)


def _head_sums(z, bd):
    hi = z.astype(bf16)
    lo = (z - hi.astype(f32)).astype(bf16)
    return _dot(hi, bd) + _dot(lo, bd)


def _weight_cols(start, width):
    kv_width = 2 * NKV * HD
    if start < D:
        orig = start
    elif start < INW - kv_width:
        orig = start + kv_width
    else:
        orig = start - (INW - kv_width) + D
    assert (start < D) == (start + width <= D) and (start < INW - kv_width) == (start + width <= INW - kv_width)
    return slice(orig, orig + width)


def _rms_inproj(x, g, w, perm, gq2, gk2, comm=None):
    T, N = x.shape[0], w.shape[1]
    tn = 512
    conv_cols = (C_A * D, (C_G + 1) * D)
    attn_chunks = [C_Q * D // tn, C_Q * D // tn + 1, C_KV]

    def body(x_ref, g_ref, w_ref, perm_ref, gq_ref, gk_ref, p_ref, u_ref, tail_ref, qn_ref, kk_ref, vv_ref):
        xv = x_ref[...]
        r = lax.rsqrt(jnp.mean(xv * xv, axis=-1, keepdims=True) + EPS)
        u = (xv * r * g_ref[...]).astype(bf16)
        u_ref[...] = u
        u_blocks = _dot(perm_ref[...], u).astype(bf16)

        def project(c):
            lhs = u_blocks if conv_cols[0] <= c * tn < conv_cols[1] else u
            p_ref[:, c * tn:(c + 1) * tn] = _dot(lhs, w_ref[:, _weight_cols(c * tn, tn)])

        for c in attn_chunks:
            project(c)
        bd = _head_blockdiag()
        lo = _low_head_lanes()
        for p in range(NQ // 2):
            z = p_ref[:, C_Q * D + 128 * p:C_Q * D + 128 * p + 128]
            rq = lax.rsqrt(_head_sums(z * z, bd) * (1.0 / HD) + EPS)
            qn_ref[:, 128 * p:128 * p + 128] = (z * rq * gq_ref[...] * (HD ** -0.5)).astype(bf16)
        kv0 = C_KV * tn
        for p in range(NKV // 2):
            z = p_ref[:, kv0 + 128 * p:kv0 + 128 * p + 128]
            rk = lax.rsqrt(_head_sums(z * z, bd) * (1.0 / HD) + EPS)
            _split_pair(z * rk * gk_ref[...], kk_ref, p, lo)
            _split_pair(p_ref[:, kv0 + 256 + 128 * p:kv0 + 256 + 128 * p + 128], vv_ref, p, lo)
        for c in range(N // tn):
            if c not in attn_chunks:
                project(c)
        tail_ref[...] = _dot(u[TT - NBLK:TT], w_ref[:, _weight_cols(conv_cols[0], conv_cols[1] - conv_cols[0])])

    once = pl.Buffered(1)
    row = pl.BlockSpec((TT, D), lambda i: (i, 0))
    vec = pl.BlockSpec((1, 128), lambda i: (0, 0))
    return _call(
        body, (x, g, w, perm, gq2, gk2), grid=(T // TT,),
        in_specs=[row, pl.BlockSpec((1, D), lambda i: (0, 0)),
                  pl.BlockSpec((D, N), lambda i: (0, 0), pipeline_mode=once),
                  pl.BlockSpec((TT, TT), lambda i: (0, 0), pipeline_mode=once), vec, vec],
        out_specs=[pl.BlockSpec((TT, N), lambda i: (i, 0)), row, pl.BlockSpec((NBLK, 2 * D), lambda i: (i, 0)),
                   row, row, row],
        out_shape=[SDS((T, N), f32), SDS((T, D), bf16), SDS((T // TT * NBLK, 2 * D), f32)] + [SDS((T, D), bf16)] * 3,
        name="rms_inproj", sem=("parallel",), vmem_mb=56, comm=comm, mid_step=(3 * (T // TT)) // 4)


def _split_pair(pair, out_ref, p, lo):
    rolled = pltpu.roll(pair, HD, 1)
    zero = jnp.zeros_like(pair)
    c = 512 * p
    out_ref[:, c:c + 128] = jnp.where(lo, pair, zero).astype(bf16)
    out_ref[:, c + 128:c + 256] = jnp.where(lo, zero, rolled).astype(bf16)
    out_ref[:, c + 256:c + 384] = jnp.where(lo, rolled, zero).astype(bf16)
    out_ref[:, c + 384:c + 512] = jnp.where(lo, zero, pair).astype(bf16)


def _bucket_tile():
    qi = np.arange(BLK)[:, None]
    kj = np.arange(BLK)[None, :]
    n = np.where(kj > qi, qi + BLK - kj, qi - kj)
    max_exact = NBUCKET // 2
    nf = np.maximum(n, 1).astype(np.float32)
    large = max_exact + (np.log(nf / max_exact) / np.float32(np.log(128 / max_exact))
                         * (NBUCKET - max_exact)).astype(np.int32)
    large = np.minimum(large, NBUCKET - 1)
    return np.where(n < max_exact, n, large).astype(np.int32)


def _from_prev_block():
    return lax.broadcasted_iota(jnp.int32, (BLK, BLK), 1) > lax.broadcasted_iota(jnp.int32, (BLK, BLK), 0)


def _bias_tiles(rel_bias):
    def body(rb_ref, bk_ref, out_ref):
        bk = bk_ref[...]
        for h in range(NQ):
            acc = jnp.zeros((BLK, BLK), f32)
            for b in range(NBUCKET):
                acc = jnp.where(bk == b, rb_ref[b, h], acc)
            out_ref[h] = acc

    return pl.pallas_call(
        body,
        in_specs=[pl.BlockSpec(memory_space=pltpu.SMEM), pl.BlockSpec(memory_space=pltpu.VMEM)],
        out_specs=pl.BlockSpec(memory_space=pltpu.VMEM),
        out_shape=SDS((NQ, BLK, BLK), f32),
        name="bias_tiles")(rel_bias, jnp.asarray(_bucket_tile()))


def _rows2(ref, c):
    return jnp.concatenate([ref[:, c:c + 128], ref[:, c + 128:c + 256]], axis=0)


def _attn_fwd(qn, kk, vv, bias, sinks, comm=None):
    T = qn.shape[0]
    nb = T // BLK

    def body(s_ref, q_ref, kc_ref, kp_ref, vc_ref, vp_ref, b_ref, o_ref, lse_ref):
        prev = _from_prev_block()
        no_key = jnp.logical_and(prev, pl.program_id(0) == 0)
        scores = []
        for h in range(NKV):
            qs = _rows2(q_ref, 256 * h)
            scores.append((_dot_nt(qs, _rows2(kc_ref, 256 * h)), _dot_nt(qs, _rows2(kp_ref, 256 * h))))
        for h in range(NKV):
            c = 256 * h
            sc, sp = scores[h]
            vstack = jnp.concatenate([vp_ref[:, c:c + 128], vc_ref[:, c:c + 128],
                                      vp_ref[:, c + 128:c + 256], vc_ref[:, c + 128:c + 256]], axis=0)
            for pr in range(2):
                ps = []
                for e in range(2):
                    hq = 4 * h + 2 * pr + e
                    rows, cols = slice(128 * pr, 128 * pr + 128), slice(128 * e, 128 * e + 128)
                    s = jnp.where(no_key, NEG, jnp.where(prev, sp[rows, cols], sc[rows, cols]) + b_ref[hq])
                    sink = s_ref[0, hq]
                    m = jnp.maximum(jnp.max(s, axis=-1, keepdims=True), sink)
                    ex = jnp.exp(s - m)
                    l = jnp.sum(ex, axis=-1, keepdims=True) + jnp.exp(sink - m)
                    p = ex * (1.0 / l)
                    ps += [jnp.where(prev, p, 0.0).astype(bf16), jnp.where(prev, 0.0, p).astype(bf16)]
                    lse_ref[:, hq:hq + 1] = m + jnp.log(l)
                o_ref[:, c + 128 * pr:c + 128 * pr + 128] = _dot(jnp.concatenate(ps, axis=1), vstack).astype(bf16)

    blk = lambda f: pl.BlockSpec((BLK, D), f)
    cur = lambda n: (n, 0)
    prev = lambda n: (jnp.maximum(n - 1, 0), 0)
    return _call(
        body, (sinks, qn, kk, kk, vv, vv, bias), grid=(nb,),
        in_specs=[pl.BlockSpec(memory_space=pltpu.SMEM), blk(cur), blk(cur), blk(prev), blk(cur), blk(prev),
                  pl.BlockSpec((NQ, BLK, BLK), lambda n: (0, 0, 0))],
        out_specs=[blk(cur), pl.BlockSpec((BLK, NQ), cur)],
        out_shape=[SDS((T, D), bf16), SDS((T, NQ), f32)],
        name="attn_fwd", sem=("parallel",), vmem_mb=32, comm=comm, mid_step=(3 * nb) // 4)


TT = 512
NBLK = 32
RPB = TT // NBLK
CONV_LANES = 128
KBLK = 4
GBLK = 8
TAPG = 8


def _block_perm():
    p = np.arange(TT)
    m = np.zeros((TT, TT), np.float32)
    m[p, NBLK * (p % RPB) + p // RPB] = 1.0
    return jnp.asarray(m, bf16), jnp.asarray(m.T, bf16)


def _lane_groups():
    return [slice(q * CONV_LANES, (q + 1) * CONV_LANES) for q in range(D // CONV_LANES)]


def _fill_time_blocks(z, tile, edge, causal):
    row = lax.broadcasted_iota(jnp.int32, (RPB, 1), 0)
    for k in range(NBLK):
        blk = tile[RPB * k:RPB * (k + 1)]
        if causal:
            z[NBLK + k] = blk
            z[k] = jnp.where(row == 0, edge[k:k + 1], pltpu.roll(blk, 1, 0))
        else:
            z[k] = blk
            z[NBLK + k] = jnp.where(row == RPB - 1, edge[k:k + 1], pltpu.roll(blk, RPB - 1, 0))


def _block_conv(z, w_ref, tap_offset, init, store):
    def step(s, carry):
        k0 = s * KBLK
        for ln in _lane_groups():
            accs = [init(ln) for _ in range(KBLK)]
            for g0 in range(0, CW, TAPG):
                taps = range(g0, min(g0 + TAPG, CW))
                lo = min(tap_offset(j) for j in taps)
                hi = max(tap_offset(j) for j in taps)
                win = [z[k0 + lo + d, :, ln] for d in range(KBLK + hi - lo)]
                for j in taps:
                    wv = w_ref[j:j + 1, ln]
                    for q in range(KBLK):
                        accs[q] = accs[q] + win[q + tap_offset(j) - lo] * wv
            for q in range(KBLK):
                store(k0 + q, ln, accs[q])
        return carry

    lax.fori_loop(0, NBLK // KBLK, step, 0)


def _block_rows(k):
    return pl.ds(pl.multiple_of(k * RPB, RPB), RPB)


def _glu_conv_fwd(proj, tail, w_dw, b_dw, ln_g, ln_b, perm_t, comm=None):
    T = proj.shape[0]

    def body(a_ref, g_ref, ta_ref, tg_ref, w_ref, b_ref, lg_ref, lb_ref, pt_ref, h1_ref, h3_ref, z):
        edge = jnp.where(pl.program_id(0) > 0, ta_ref[...] * _sigmoid(tg_ref[...]), 0.0)
        _fill_time_blocks(z, a_ref[...] * _sigmoid(g_ref[...]), edge, causal=True)

        def store(k, ln, value):
            h1_ref[_block_rows(k), ln] = value

        _block_conv(z, w_ref, lambda j: NBLK - (CW - 1) + j,
                    lambda ln: jnp.broadcast_to(b_ref[:, ln], (RPB, CONV_LANES)), store)
        h1 = h1_ref[...]
        mu = jnp.mean(h1, axis=-1, keepdims=True)
        xc = h1 - mu
        var = jnp.mean(xc * xc, axis=-1, keepdims=True)
        h2 = xc * lax.rsqrt(var + EPS) * lg_ref[...] + lb_ref[...]
        h3_ref[...] = _dot(pt_ref[...], (h2 * _sigmoid(h2)).astype(bf16)).astype(bf16)

    tile = lambda cb: pl.BlockSpec((TT, D), lambda i: (i, cb))
    edge = lambda cb: pl.BlockSpec((NBLK, D), lambda i: (jnp.maximum(i - 1, 0), cb))
    vec = pl.BlockSpec((1, D), lambda i: (0, 0))
    (h1, h3), got = _call(
        body, (proj, proj, tail, tail, w_dw, b_dw, ln_g, ln_b, perm_t), grid=(T // TT,),
        in_specs=[tile(C_A), tile(C_G), edge(0), edge(1), pl.BlockSpec((HALO, D), lambda i: (0, 0)), vec, vec, vec,
                  pl.BlockSpec((TT, TT), lambda i: (0, 0), pipeline_mode=pl.Buffered(1))],
        out_specs=[pl.BlockSpec((TT, D), lambda i: (i, 0))] * 2,
        out_shape=[SDS((T, D), f32), SDS((T, D), bf16)],
        scratch_shapes=[pltpu.VMEM((2 * NBLK, RPB, D), f32)],
        name="glu_conv_fwd", sem=("parallel",), vmem_mb=40, comm=comm, mid_step=(3 * (T // TT)) // 4)
    return h1, h3, got


def _mix_out(o, h3, proj, x, w_attn_o, w_conv_out, w_out, g_mlp):
    T = x.shape[0]
    tm = 512

    def body(o_ref, h3_ref, ga_ref, gc_ref, x_ref, wa_ref, wc_ref, wo_ref, g_ref,
             attn_ref, conv_ref, mg_ref, x1_ref, n2_ref):
        x1 = x_ref[...]
        for j in range(D // MIX_CHUNK):
            cols = slice(j * MIX_CHUNK, (j + 1) * MIX_CHUNK)
            attn = _dot(o_ref[...], wa_ref[:, cols])
            conv = _dot(h3_ref[...], wc_ref[:, cols])
            attn_ref[:, cols] = attn.astype(bf16)
            conv_ref[:, cols] = conv.astype(bf16)
            mg = (_sigmoid(ga_ref[:, cols]) * attn + _sigmoid(gc_ref[:, cols]) * conv).astype(bf16)
            mg_ref[:, cols] = mg
            x1 = x1 + _dot(mg, wo_ref[cols, :])
        x1_ref[...] = x1
        r = lax.rsqrt(jnp.mean(x1 * x1, axis=-1, keepdims=True) + EPS)
        n2_ref[...] = (x1 * r * g_ref[...]).astype(bf16)

    tile = lambda cb=0: pl.BlockSpec((tm, D), lambda i: (i, cb))
    wfull = pl.BlockSpec((D, D), lambda i: (0, 0), pipeline_mode=pl.Buffered(1))
    return pl.pallas_call(
        body, grid=(T // tm,),
        in_specs=[tile(), tile(), tile(C_GA), tile(C_GC), tile(), wfull, wfull, wfull,
                  pl.BlockSpec((1, D), lambda i: (0, 0))],
        out_specs=[tile()] * 5,
        out_shape=[SDS((T, D), bf16), SDS((T, D), bf16), SDS((T, D), bf16), SDS((T, D), f32), SDS((T, D), bf16)],
        name="mix_out", compiler_params=_cparams(("parallel",), 48))(o, h3, proj, proj, x, w_attn_o, w_conv_out, w_out, g_mlp)


def _mlp_fwd(n2, w1, w2, x1, tgt):
    T = n2.shape[0]
    tm, tf = 512, 1024

    def body(n2_ref, w1_ref, w2_ref, x1_ref, t_ref, hm_ref, slope_ref, dy_ref, dyb_ref, loss_ref):
        @pl.when(pl.program_id(0) == 0)
        def _():
            loss_ref[...] = jnp.zeros_like(loss_ref)

        n2v = n2_ref[...]
        for c in range(DFF // tf):
            r = jnp.maximum(_dot(n2v, w1_ref[:, c * tf:(c + 1) * tf]), 0.0)
            hm_ref[:, c * tf:(c + 1) * tf] = (r * r).astype(bf16)
            slope_ref[:, c * tf:(c + 1) * tf] = (2.0 * r).astype(bf16)
        e = x1_ref[...] + _dot(hm_ref[...], w2_ref[...]) - t_ref[...]
        dy = e * (1.0 / D)
        dy_ref[...] = dy
        dyb_ref[...] = dy.astype(bf16)
        loss_ref[...] += 0.5 * jnp.sum(jnp.sum(e * e, axis=-1, keepdims=True) * (1.0 / D))

    row = pl.BlockSpec((tm, D), lambda i: (i, 0))
    once = pl.Buffered(1)
    return pl.pallas_call(
        body, grid=(T // tm,),
        in_specs=[row, pl.BlockSpec((D, DFF), lambda i: (0, 0), pipeline_mode=once),
                  pl.BlockSpec((DFF, D), lambda i: (0, 0), pipeline_mode=once), row, row],
        out_specs=[pl.BlockSpec((tm, DFF), lambda i: (i, 0)), pl.BlockSpec((tm, DFF), lambda i: (i, 0)), row, row,
                   pl.BlockSpec((8, 128), lambda i: (0, 0))],
        out_shape=[SDS((T, DFF), bf16), SDS((T, DFF), bf16), SDS((T, D), f32), SDS((T, D), bf16), SDS((8, 128), f32)],
        name="mlp_fwd", compiler_params=_cparams(("arbitrary",), 60))(n2, w1, w2, x1, tgt)


def _rms_bwd(xv, g, dn, dres):
    r = lax.rsqrt(jnp.mean(xv * xv, axis=-1, keepdims=True) + EPS)
    gd = dn * g
    dx = dres + r * gd - xv * (r * r * r) * jnp.mean(xv * gd, axis=-1, keepdims=True)
    dg = jnp.sum(dn * xv * r, axis=0, keepdims=True)
    return dx, dg


def _mlp_bwd(dy, dyb, slope, w1, w2, x1, g_mlp):
    T = dy.shape[0]
    tm, tf = 512, 1024

    def body(dy_ref, dyb_ref, slope_ref, w1_ref, w2_ref, x1_ref, g_ref, df_ref, dx_ref, dxb_ref, dg_ref):
        @pl.when(pl.program_id(0) == 0)
        def _():
            dg_ref[...] = jnp.zeros_like(dg_ref)

        dyb = dyb_ref[...]
        for c in range(DFF // tf):
            cols = slice(c * tf, (c + 1) * tf)
            d_hm = _dot_nt(dyb, w2_ref[cols, :])
            df_ref[:, cols] = (d_hm * slope_ref[:, cols].astype(f32)).astype(bf16)
        dn = _dot_nt(df_ref[...], w1_ref[...])
        dx, dg = _rms_bwd(x1_ref[...], g_ref[...], dn, dy_ref[...])
        dx_ref[...] = dx
        dxb_ref[...] = dx.astype(bf16)
        dg_ref[...] += dg

    row = pl.BlockSpec((tm, D), lambda i: (i, 0))
    wide = pl.BlockSpec((tm, DFF), lambda i: (i, 0))
    vec = pl.BlockSpec((1, D), lambda i: (0, 0))
    once = pl.Buffered(1)
    return pl.pallas_call(
        body, grid=(T // tm,),
        in_specs=[row, row, wide, pl.BlockSpec((D, DFF), lambda i: (0, 0), pipeline_mode=once),
                  pl.BlockSpec((DFF, D), lambda i: (0, 0), pipeline_mode=once), row, vec],
        out_specs=[wide, row, row, vec],
        out_shape=[SDS((T, DFF), bf16), SDS((T, D), f32), SDS((T, D), bf16), SDS((1, D), f32)],
        name="mlp_bwd", compiler_params=_cparams(("arbitrary",), 56))(dy, dyb, slope, w1, w2, x1, g_mlp)


def _wgrad(a, b, name, tn=1024):
    T, M = a.shape
    N = b.shape[1]
    tmm, tk = min(M, 1024), min(T, 2048)

    def body(a_ref, b_ref, o_ref):
        @pl.when(pl.program_id(2) == 0)
        def _():
            o_ref[...] = jnp.zeros_like(o_ref)

        o_ref[...] += _dot_tn(a_ref[...], b_ref[...])

    return pl.pallas_call(
        body, grid=(M // tmm, N // tn, T // tk),
        in_specs=[pl.BlockSpec((tk, tmm), lambda m, n, t: (t, m)), pl.BlockSpec((tk, tn), lambda m, n, t: (t, n))],
        out_specs=pl.BlockSpec((tmm, tn), lambda m, n, t: (m, n)),
        out_shape=SDS((M, N), f32),
        name=name, compiler_params=_cparams(("parallel", "parallel", "arbitrary"), 40))(a, b)


def _mix_bwd(dx1b, proj, attn, conv, h1, w_attn_o, w_conv_out, w_out, ln_g, ln_b, perm, comm=None):
    T = dx1b.shape[0]
    tm = TT

    def body(dx_ref, ga_ref, gc_ref, attn_ref, conv_ref, h1_ref, wa_ref, wc_ref, wo_ref, lg_ref, lb_ref, perm_ref,
             dat_ref, dcv_ref, do_ref, dh1_ref, dga_ref, dgc_ref, acc_ref, head_ref):
        @pl.when(pl.program_id(0) == 0)
        def _():
            acc_ref[...] = jnp.zeros_like(acc_ref)

        d_o, d_h3 = None, None
        for j in range(D // MIX_CHUNK):
            cols = slice(j * MIX_CHUNK, (j + 1) * MIX_CHUNK)
            dm = _dot_nt(dx_ref[...], wo_ref[cols, :])
            sa = _sigmoid(ga_ref[:, cols])
            sc = _sigmoid(gc_ref[:, cols])
            dat = (dm * sa).astype(bf16)
            dcv = (dm * sc).astype(bf16)
            dat_ref[:, cols] = dat
            dcv_ref[:, cols] = dcv
            dga_ref[:, cols] = (dm * attn_ref[:, cols].astype(f32) * sa * (1.0 - sa)).astype(bf16)
            dgc_ref[:, cols] = (dm * conv_ref[:, cols].astype(f32) * sc * (1.0 - sc)).astype(bf16)
            part_o = _dot_nt(dat, wa_ref[:, cols])
            part_h = _dot_nt(_dot(perm_ref[...], dcv).astype(bf16), wc_ref[:, cols])
            d_o = part_o if d_o is None else d_o + part_o
            d_h3 = part_h if d_h3 is None else d_h3 + part_h
        do_ref[...] = d_o.astype(bf16)
        h1 = h1_ref[...]
        mu = jnp.mean(h1, axis=-1, keepdims=True)
        xc = h1 - mu
        rstd = lax.rsqrt(jnp.mean(xc * xc, axis=-1, keepdims=True) + EPS)
        xh = xc * rstd
        h2 = xh * lg_ref[...] + lb_ref[...]
        sg = _sigmoid(h2)
        dh2 = d_h3 * (sg * (1.0 + h2 * (1.0 - sg)))
        dxh = dh2 * lg_ref[...]
        dh1 = rstd * (dxh - jnp.mean(dxh, axis=-1, keepdims=True) - xh * jnp.mean(dxh * xh, axis=-1, keepdims=True))
        dh1_ref[...] = dh1
        for k in range(NBLK):
            head_ref[k:k + 1, :] = dh1[RPB * k:RPB * k + 1]
        acc_ref[0:1, :] += jnp.sum(dh2 * xh, axis=0, keepdims=True)
        acc_ref[1:2, :] += jnp.sum(dh2, axis=0, keepdims=True)
        acc_ref[2:3, :] += jnp.sum(dh1, axis=0, keepdims=True)

    tile = lambda cb=0: pl.BlockSpec((tm, D), lambda i: (i, cb))
    wfull = pl.BlockSpec((D, D), lambda i: (0, 0), pipeline_mode=pl.Buffered(1))
    vec = pl.BlockSpec((1, D), lambda i: (0, 0))
    return _call(
        body, (dx1b, proj, proj, attn, conv, h1, w_attn_o, w_conv_out, w_out, ln_g, ln_b, perm), grid=(T // tm,),
        in_specs=[tile(), tile(C_GA), tile(C_GC), tile(), tile(), tile(), wfull, wfull, wfull, vec, vec,
                  pl.BlockSpec((TT, TT), lambda i: (0, 0), pipeline_mode=pl.Buffered(1))],
        out_specs=[tile()] * 6 + [pl.BlockSpec((8, D), lambda i: (0, 0)), pl.BlockSpec((NBLK, D), lambda i: (i, 0))],
        out_shape=[SDS((T, D), bf16), SDS((T, D), bf16), SDS((T, D), bf16), SDS((T, D), f32),
                   SDS((T, D), bf16), SDS((T, D), bf16), SDS((8, D), f32), SDS((T // RPB, D), f32)],
        name="mix_bwd", sem=("arbitrary",), vmem_mb=56, comm=comm)


def _conv_bwd(dh1, head, proj, tail, w_dw, perm_t, comm=None):
    T = dh1.shape[0]
    nt = T // TT

    def body(d_ref, hd_ref, a_ref, g_ref, ta_ref, tg_ref, w_ref, pt_ref, da_ref, dg_ref, gw_ref, zd, zh, dh0, gacc):
        i = pl.program_id(0)

        @pl.when(i == 0)
        def _():
            gacc[...] = jnp.zeros_like(gacc)

        a = a_ref[...]
        sg = _sigmoid(g_ref[...])
        _fill_time_blocks(zd, d_ref[...], jnp.where(i < nt - 1, hd_ref[...], 0.0), causal=False)
        _fill_time_blocks(zh, a * sg, jnp.where(i > 0, ta_ref[...] * _sigmoid(tg_ref[...]), 0.0), causal=True)

        def store(k, ln, value):
            dh0[_block_rows(k), ln] = value

        _block_conv(zd, w_ref, lambda j: (CW - 1) - j, lambda ln: jnp.zeros((RPB, CONV_LANES), f32), store)

        for ln in _lane_groups():
            for g0 in range(0, CW, TAPG):
                taps = list(range(g0, min(g0 + TAPG, CW)))

                def add_blocks(s, accs, ln=ln, taps=taps):
                    k0 = s * GBLK
                    first = k0 + NBLK - (CW - 1) + taps[0]
                    win = [zh[first + t, :, ln] for t in range(GBLK + len(taps) - 1)]
                    accs = list(accs)
                    for q in range(GBLK):
                        d = zd[k0 + q, :, ln]
                        for n, j in enumerate(taps):
                            accs[n] = accs[n] + d * win[q + j - taps[0]]
                    return tuple(accs)

                accs = lax.fori_loop(0, NBLK // GBLK, add_blocks,
                                     tuple(jnp.zeros((RPB, CONV_LANES), f32) for _ in taps))
                for j, acc in zip(taps, accs):
                    gacc[j, :, ln] += acc

        d0 = dh0[...]
        da_ref[...] = _dot(pt_ref[...], (d0 * sg).astype(bf16)).astype(bf16)
        dg_ref[...] = _dot(pt_ref[...], (d0 * a * sg * (1.0 - sg)).astype(bf16)).astype(bf16)

        @pl.when(i == nt - 1)
        def _():
            gw_ref[...] = jnp.zeros_like(gw_ref)
            for j in range(CW):
                gw_ref[j:j + 1, :] = jnp.sum(gacc[j], axis=0, keepdims=True)

    tile = lambda cb=0: pl.BlockSpec((TT, D), lambda i: (i, cb))
    prev_edge = lambda cb: pl.BlockSpec((NBLK, D), lambda i: (jnp.maximum(i - 1, 0), cb))
    next_edge = pl.BlockSpec((NBLK, D), lambda i: (jnp.minimum(i + 1, nt - 1), 0))
    wspec = pl.BlockSpec((HALO, D), lambda i: (0, 0))
    return _call(
        body, (dh1, head, proj, proj, tail, tail, w_dw, perm_t), grid=(nt,),
        in_specs=[tile(), next_edge, tile(C_A), tile(C_G), prev_edge(0), prev_edge(1), wspec,
                  pl.BlockSpec((TT, TT), lambda i: (0, 0), pipeline_mode=pl.Buffered(1))],
        out_specs=[tile(), tile(), wspec],
        out_shape=[SDS((T, D), bf16), SDS((T, D), bf16), SDS((HALO, D), f32)],
        scratch_shapes=[pltpu.VMEM((2 * NBLK, RPB, D), f32), pltpu.VMEM((2 * NBLK, RPB, D), f32),
                        pltpu.VMEM((TT, D), f32), pltpu.VMEM((HALO, RPB, D), f32)],
        name="conv_bwd", sem=("arbitrary",), vmem_mb=48, comm=comm)


def _attn_bwd(qn, kk, vv, bias, sinks, o, do, lse, comm=None):
    T = qn.shape[0]
    nb = T // BLK

    def body(s_ref, q_ref, kc_ref, kp_ref, vc_ref, vp_ref, b_ref, o_ref, do_ref, lse_ref,
             dq_ref, dkc_ref, dkp_ref, dvc_ref, dvp_ref, dsk_ref, dsa_ref):
        n = pl.program_id(0)

        @pl.when(n == 0)
        def _():
            dsk_ref[...] = jnp.zeros_like(dsk_ref)
            dsa_ref[...] = jnp.zeros_like(dsa_ref)

        @pl.when(n == nb)
        def _():
            dkp_ref[...] = jnp.zeros_like(dkp_ref)
            dvp_ref[...] = jnp.zeros_like(dvp_ref)

        @pl.when(n < nb)
        def _():
            from_prev = _from_prev_block()
            no_key = jnp.logical_and(from_prev, n == 0)
            lo = _low_head_lanes()
            dups = {"kc": [], "kp": [], "vc": [], "vp": []}
            products = []
            for h in range(NKV):
                qs = _rows2(q_ref, 256 * h)
                dos = _rows2(do_ref, 256 * h)
                products.append((qs, dos, _dot_nt(qs, _rows2(kc_ref, 256 * h)), _dot_nt(qs, _rows2(kp_ref, 256 * h)),
                                 _dot_nt(dos, _rows2(vc_ref, 256 * h)), _dot_nt(dos, _rows2(vp_ref, 256 * h))))
            for h in range(NKV):
                c = 256 * h
                qs, dos, sc, sp, dpc, dpp = products[h]
                kstack = jnp.concatenate([kp_ref[:, c:c + 128], kc_ref[:, c:c + 128],
                                          kp_ref[:, c + 128:c + 256], kc_ref[:, c + 128:c + 256]], axis=0)
                p_c, p_p, ds_c, ds_p = [], [], [], []
                for pr in range(2):
                    cc = c + 128 * pr
                    prod = do_ref[:, cc:cc + 128].astype(f32) * o_ref[:, cc:cc + 128].astype(f32)
                    d_lo = jnp.sum(jnp.where(lo, prod, 0.0), axis=-1, keepdims=True)
                    d_hi = jnp.sum(prod, axis=-1, keepdims=True) - d_lo
                    row_pc, row_pp, row_dc, row_dp = [], [], [], []
                    for e in range(2):
                        hq = 4 * h + 2 * pr + e
                        rows, cols = slice(128 * pr, 128 * pr + 128), slice(128 * e, 128 * e + 128)
                        delta = d_lo if e == 0 else d_hi
                        lse = lse_ref[:, hq:hq + 1]
                        s = jnp.where(from_prev, sp[rows, cols], sc[rows, cols]) + b_ref[hq]
                        p = jnp.where(no_key, 0.0, jnp.exp(s - lse))
                        ds = p * (jnp.where(from_prev, dpp[rows, cols], dpc[rows, cols]) - delta)
                        dsa_ref[hq] += ds
                        dsk_ref[hq] += jnp.broadcast_to(-jnp.sum(jnp.exp(s_ref[0, hq] - lse) * delta), (8, 128))
                        row_pc.append(jnp.where(from_prev, 0.0, p).astype(bf16))
                        row_pp.append(jnp.where(from_prev, p, 0.0).astype(bf16))
                        row_dc.append(jnp.where(from_prev, 0.0, ds).astype(bf16))
                        row_dp.append(jnp.where(from_prev, ds, 0.0).astype(bf16))
                    dq_ref[:, cc:cc + 128] = _dot(jnp.concatenate([row_dp[0], row_dc[0], row_dp[1], row_dc[1]], axis=1),
                                                  kstack).astype(bf16)
                    p_c.append(jnp.concatenate(row_pc, axis=1))
                    p_p.append(jnp.concatenate(row_pp, axis=1))
                    ds_c.append(jnp.concatenate(row_dc, axis=1))
                    ds_p.append(jnp.concatenate(row_dp, axis=1))

                def to_keys(m2, rhs):
                    x2 = _dot_tn(jnp.concatenate(m2, axis=0), rhs)
                    x = jnp.where(lo, x2[0:128], x2[128:256])
                    return x + pltpu.roll(x, HD, 1)

                dups["kc"].append(to_keys(ds_c, qs))
                dups["kp"].append(to_keys(ds_p, qs))
                dups["vc"].append(to_keys(p_c, dos))
                dups["vp"].append(to_keys(p_p, dos))
            for key, ref in (("kc", dkc_ref), ("kp", dkp_ref), ("vc", dvc_ref), ("vp", dvp_ref)):
                d = dups[key]
                ref[:, 0:128] = jnp.where(lo, d[0], d[1]).astype(bf16)
                ref[:, 128:256] = jnp.where(lo, d[2], d[3]).astype(bf16)

    clamp = lambda n: jnp.minimum(n, nb - 1)
    blk = lambda f: pl.BlockSpec((BLK, D), f)
    cur = lambda n: (clamp(n), 0)
    prev = lambda n: (jnp.maximum(clamp(n) - 1, 0), 0)
    back = lambda n: (jnp.maximum(n - 1, 0), 0)
    kvb = lambda f: pl.BlockSpec((BLK, NKV * HD), f)
    return _call(
        body, (sinks, qn, kk, kk, vv, vv, bias, o, do, lse), grid=(nb + 1,),
        in_specs=[pl.BlockSpec(memory_space=pltpu.SMEM), blk(cur), blk(cur), blk(prev), blk(cur), blk(prev),
                  pl.BlockSpec((NQ, BLK, BLK), lambda n: (0, 0, 0)), blk(cur), blk(cur),
                  pl.BlockSpec((BLK, NQ), cur)],
        out_specs=[blk(cur), kvb(cur), kvb(back), kvb(cur), kvb(back),
                   pl.BlockSpec((NQ, 8, 128), lambda n: (0, 0, 0)),
                   pl.BlockSpec((NQ, BLK, BLK), lambda n: (0, 0, 0))],
        out_shape=[SDS((T, D), bf16)] + [SDS((T, NKV * HD), bf16)] * 4 + [SDS((NQ, 8, 128), f32), SDS((NQ, BLK, BLK), f32)],
        name="attn_bwd", sem=("arbitrary",), vmem_mb=40, comm=comm)


def _bias_bwd(dsa):
    def body(bk_ref, ds_ref, out_ref):
        bk = bk_ref[...]
        lane = lax.broadcasted_iota(jnp.int32, (1, 128), 1)
        for h in range(NQ):
            ds = ds_ref[h]
            row = jnp.zeros((1, 128), f32)
            for b in range(NBUCKET):
                row = jnp.where(lane == b, jnp.sum(jnp.where(bk == b, ds, 0.0)), row)
            out_ref[h:h + 1, :] = row

    return pl.pallas_call(body, out_shape=SDS((NQ, 128), f32), name="bias_bwd")(jnp.asarray(_bucket_tile()), dsa)


def _qkv_bwd(proj, gq2, gk2, dqn, dkc, dkp, dvc, dvp, comm=None):
    T = proj.shape[0]
    tm = 512

    def body(q_ref, kv_ref, gq_ref, gk_ref, dq_ref, dkc_ref, dkp_ref, dvc_ref, dvp_ref,
             oq_ref, okv_ref, ggq_ref, ggk_ref):
        @pl.when(pl.program_id(0) == 0)
        def _():
            ggq_ref[...] = jnp.zeros_like(ggq_ref)
            ggk_ref[...] = jnp.zeros_like(ggk_ref)

        bd = _head_blockdiag()

        def norm_bwd(z, dy, g, scale):
            r = lax.rsqrt(_head_sums(z * z, bd) * (1.0 / HD) + EPS)
            gd = dy * g * scale
            dz = r * gd - z * (r * r * r) * _head_sums(z * gd, bd) * (1.0 / HD)
            return dz, jnp.sum(dy * scale * z * r, axis=0, keepdims=True)

        gq = jnp.zeros((1, 128), f32)
        for p in range(NQ // 2):
            ln = slice(128 * p, 128 * p + 128)
            dz, dg = norm_bwd(q_ref[:, ln], dq_ref[:, ln].astype(f32), gq_ref[...], HD ** -0.5)
            oq_ref[:, ln] = dz.astype(bf16)
            gq = gq + dg
        ggq_ref[...] += gq + pltpu.roll(gq, HD, 1)
        gk = jnp.zeros((1, 128), f32)
        for p in range(NKV // 2):
            ln = slice(128 * p, 128 * p + 128)
            dz, dg = norm_bwd(kv_ref[:, ln], dkc_ref[:, ln].astype(f32) + dkp_ref[:, ln].astype(f32), gk_ref[...], 1.0)
            okv_ref[:, ln] = dz.astype(bf16)
            gk = gk + dg
        ggk_ref[...] += gk + pltpu.roll(gk, HD, 1)
        okv_ref[:, 256:512] = (dvc_ref[...].astype(f32) + dvp_ref[...].astype(f32)).astype(bf16)

    vec = pl.BlockSpec((1, 128), lambda i: (0, 0))
    kvb = pl.BlockSpec((tm, NKV * HD), lambda i: (i, 0))
    return _call(
        body, (proj, proj, gq2, gk2, dqn, dkc, dkp, dvc, dvp), grid=(T // tm,),
        in_specs=[pl.BlockSpec((tm, D), lambda i: (i, C_Q)), pl.BlockSpec((tm, 512), lambda i: (i, C_KV)), vec, vec,
                  pl.BlockSpec((tm, D), lambda i: (i, 0)), kvb, kvb, kvb, kvb],
        out_specs=[pl.BlockSpec((tm, D), lambda i: (i, 0)), pl.BlockSpec((tm, 512), lambda i: (i, 0)), vec, vec],
        out_shape=[SDS((T, D), bf16), SDS((T, 512), bf16), SDS((1, 128), f32), SDS((1, 128), f32)],
        name="qkv_bwd", sem=("arbitrary",), vmem_mb=32, comm=comm)


def _inproj_bwd(pieces, w_in, x, dx1, g_mix, comm=None):
    T = x.shape[0]
    tm = 512
    widths = [p.shape[1] for p in pieces]
    offs = [sum(widths[:i]) for i in range(len(widths))]
    assert sum(widths) == INW

    def body(*refs):
        p_refs, (w_ref, x_ref, dx1_ref, g_ref, dx_ref, dg_ref) = refs[:len(pieces)], refs[len(pieces):]

        @pl.when(pl.program_id(0) == 0)
        def _():
            dg_ref[...] = jnp.zeros_like(dg_ref)

        du = None
        for p_ref, off, wd in zip(p_refs, offs, widths):
            part = _dot_nt(p_ref[...], w_ref[:, _weight_cols(off, wd)])
            du = part if du is None else du + part
        dx, dg = _rms_bwd(x_ref[...], g_ref[...], du, dx1_ref[...])
        dx_ref[...] = dx
        dg_ref[...] += dg

    row = pl.BlockSpec((tm, D), lambda i: (i, 0))
    vec = pl.BlockSpec((1, D), lambda i: (0, 0))
    return _call(
        body, (*pieces, w_in, x, dx1, g_mix), grid=(T // tm,),
        in_specs=[pl.BlockSpec((tm, wd), lambda i: (i, 0)) for wd in widths]
        + [pl.BlockSpec((D, INW), lambda i: (0, 0), pipeline_mode=pl.Buffered(1)), row, row, vec],
        out_specs=[row, vec],
        out_shape=[SDS((T, D), f32), SDS((1, D), f32)],
        name="inproj_bwd", sem=("arbitrary",), vmem_mb=48, comm=comm)


def _forward_backward(x, tgt, w, placed, chip_core):
    def sums(names, grads, got):
        res = [_pair_sum(nm, grads[nm], got_nm, chip_core) for nm, got_nm in zip(names, got)]
        return {nm: r[0] for nm, r in zip(names, res)}, {nm: r[1] for nm, r in zip(names, res)}

    first = ["w_in", "w_dw"]
    w_in, w_dw = _run_comm(_gather_comm({nm: placed[nm] for nm in first}), "gather_first")
    gq2 = jnp.tile(w["q_norm_g"], (1, 2))
    gk2 = jnp.tile(w["k_norm_g"], (1, 2))
    def gathered_in(names):
        return names, _gather_comm({nm: placed[nm] for nm in names})

    full = {}
    perm, perm_t = _block_perm()
    names, comm = gathered_in(["w_out", "w_attn_o", "w_conv_out"])
    (proj, u, tail, qn, kk, vv), got = _rms_inproj(x, w["norm_mix_g"], w_in, perm, gq2, gk2, comm=comm)
    full.update(zip(names, got))
    bias = _bias_tiles(w["rel_bias"])
    names, comm = gathered_in(["w_ff1"])
    (o, lse), got = _attn_fwd(qn, kk, vv, bias, w["attn_sinks"], comm=comm)
    full.update(zip(names, got))
    names, comm = gathered_in(["w_ff2"])
    h1, h3, got = _glu_conv_fwd(proj, tail, w_dw, w["b_dw"], w["conv_ln_g"], w["conv_ln_b"], perm_t, comm=comm)
    full.update(zip(names, got))
    attn, conv, merged, x1, n2 = _mix_out(o, h3, proj, x, full["w_attn_o"], full["w_conv_out"], full["w_out"],
                                          w["norm_mlp_g"])
    hmid, slope, dy, dyb, loss = _mlp_fwd(n2, full["w_ff1"], full["w_ff2"], x1, tgt)

    g = {}
    df1, dx1, dx1b, g["norm_mlp_g"] = _mlp_bwd(dy, dyb, slope, full["w_ff1"], full["w_ff2"], x1, w["norm_mlp_g"])
    ff = ["w_ff1", "w_ff2"]
    gff = {"w_ff2": _wgrad(hmid, dyb, "wgrad_ff2"), "w_ff1": _wgrad(n2, df1, "wgrad_ff1")}
    (dat, dcv, do, dh1, dga, dgc, lnacc, head), got = _mix_bwd(
        dx1b, proj, attn, conv, h1, full["w_attn_o"], full["w_conv_out"], full["w_out"], w["conv_ln_g"],
        w["conv_ln_b"], perm, comm=_pair_exchange_comm(gff, ff))
    g["conv_ln_g"], g["conv_ln_b"], g["b_dw"] = lnacc[0:1], lnacc[1:2], lnacc[2:3]
    cp_ff, own_ff = sums(ff, gff, got)
    sq = ["w_out", "w_attn_o", "w_conv_out"]
    gsq = {"w_out": _wgrad(merged, dx1b, "wgrad_out"), "w_attn_o": _wgrad(o, dat, "wgrad_attn_o"),
           "w_conv_out": _wgrad(h3, dcv, "wgrad_conv_out")}
    (da, dg, g["w_dw"]), got = _conv_bwd(dh1, head, proj, tail, w_dw, perm_t, comm=_merge_comms(
        _pair_exchange_comm(gsq, sq), _chip_exchange_comm(cp_ff, ff)))
    cp_sq, own_sq = sums(sq, gsq, got[:len(sq)])
    tot_ff = {nm: _chip_sum(nm, own_ff[nm], rc_nm, chip_core) for nm, rc_nm in zip(ff, got[len(sq):])}
    (dqn, dkc, dkp, dvc, dvp, dsk, dsa), got = _attn_bwd(qn, kk, vv, bias, w["attn_sinks"], o, do, lse, comm=_merge_comms(
        _chip_exchange_comm(cp_sq, sq), _pair_share_comm(tot_ff, ff)))
    tot_sq = {nm: _chip_sum(nm, own_sq[nm], rc_nm, chip_core) for nm, rc_nm in zip(sq, got[:len(sq)])}
    shards = dict(zip(ff, got[len(sq):]))
    g["attn_sinks"] = dsk[:, 0, 0].reshape(1, NQ)
    g["rel_bias"] = _bias_bwd(dsa)[:, 0:NBUCKET].T
    (dq, dkv, ggq, ggk), got = _qkv_bwd(proj, gq2, gk2, dqn, dkc, dkp, dvc, dvp, comm=_pair_share_comm(tot_sq, sq))
    shards.update(zip(sq, got))
    g["q_norm_g"], g["k_norm_g"] = ggq[:, 0:HD], ggk[:, 0:HD]
    pieces = [dq, da, dg, dga, dgc, dkv]
    names = ["q", "a", "g", "ga", "gc", "kv"]
    gw = {nm: _wgrad(u, p, "wgrad_in_" + nm, tn=p.shape[1] if p.shape[1] < 1024 else 1024) for nm, p in zip(names, pieces)}
    gin = {"w_in": jnp.concatenate([gw["q"], gw["kv"], gw["a"], gw["g"], gw["ga"], gw["gc"]], axis=1)}
    got = _run_comm(_pair_exchange_comm(gin, ["w_in"]), "rs_pair_exchange_in")
    cp_in, own_in = sums(["w_in"], gin, got)
    (grad_x, g["norm_mix_g"]), rc = _inproj_bwd(pieces, w_in, x, dx1, w["norm_mix_g"],
                                                comm=_chip_exchange_comm(cp_in, ["w_in"]))
    tot = {"w_in": _chip_sum("w_in", own_in["w_in"], rc[0], chip_core)}
    shards["w_in"] = _run_comm(_pair_share_comm(tot, ["w_in"]), "rs_pair_share_in")[0]
    return loss[0, 0], grad_x, g, shards


BIG = ["w_in", "w_attn_o", "w_conv_out", "w_out", "w_ff1", "w_ff2"]
SHARD_AXIS = {"w_in": 1, "w_attn_o": 0, "w_conv_out": 0, "w_out": 0, "w_ff1": 1, "w_ff2": 0, "w_dw": 1}
SHARD_SHAPE = {"w_in": (D, INW // 4), "w_attn_o": (D // 4, D), "w_conv_out": (D // 4, D), "w_out": (D // 4, D),
               "w_ff1": (D, DFF // 4), "w_ff2": (DFF // 4, D), "w_dw": (HALO, D // 4)}


def _position():
    x, y, c = lax.axis_index("x"), lax.axis_index("y"), lax.axis_index("c")
    other_chips = [(1 - x, y), (x, 1 - y), (1 - x, 1 - y)]
    return x, y, c, 2 * x + y, other_chips


def _shard_window(name, full_ref, s, half=None):
    R, C = SHARD_SHAPE[name]
    r0, nr = (0, R) if half is None else (half * (R // 2), R // 2)
    if SHARD_AXIS[name] == 1:
        return full_ref.at[pl.ds(r0, nr), pl.ds(s * C, C)]
    return full_ref.at[pl.ds(s * R + r0, nr), :]


def _remote(src, dst, send_sems, recv_sems, k, device):
    return pltpu.make_async_remote_copy(src_ref=src, dst_ref=dst, send_sem=send_sems.at[k], recv_sem=recv_sems.at[k],
                                        device_id=device, device_id_type=MESH)


def _full_shape(nm):
    R, C = SHARD_SHAPE[nm]
    return (R, 4 * C) if SHARD_AXIS[nm] == 1 else (4 * R, C)


def _place_shard(nm, shard, chip_arr, dtype):
    R, C = SHARD_SHAPE[nm]
    tr = min(R, 256)
    if SHARD_AXIS[nm] == 1:
        o_map = lambda i, ch: (i, ch[0])
    else:
        o_map = lambda i, ch: (ch[0] * (R // tr) + i, 0)

    def body(ch_ref, s_ref, o_ref):
        o_ref[...] = s_ref[...].astype(dtype)

    return pl.pallas_call(
        body,
        grid_spec=pltpu.PrefetchScalarGridSpec(
            num_scalar_prefetch=1, grid=(R // tr,),
            in_specs=[pl.BlockSpec((tr, C), lambda i, ch: (i, 0))], out_specs=pl.BlockSpec((tr, C), o_map)),
        out_shape=SDS(_full_shape(nm), dtype), name="place_" + nm,
        compiler_params=_cparams(("parallel",), 32))(chip_arr, shard)


def _gather_comm(placed):
    names = list(placed)
    n = len(names)

    def copies(cout, send, recv):
        x, y, c, chip, chips = _position()
        for a, nm in enumerate(names):
            for j, (cx, cy) in enumerate(chips):
                def ici(s, a=a, nm=nm, j=j, cx=cx, cy=cy):
                    w = _shard_window(nm, cout[a], s, c)
                    return _remote(w, w, send, recv, 6 * a + j, (cx, cy, c))

                def d2d(h, a=a, nm=nm, j=j, cx=cx, cy=cy):
                    w = _shard_window(nm, cout[a], 2 * cx + cy, h)
                    return _remote(w, w, send, recv, 6 * a + 3 + j, (x, y, 1 - c))

                yield ici, d2d, chip, 2 * cx + cy, c

    def start(cin, cout, send, recv):
        for ici, d2d, chip, s, c in copies(cout, send, recv):
            ici(chip).start()

    def mid(cin, cout, send, recv):
        for ici, d2d, chip, s, c in copies(cout, send, recv):
            ici(s).wait_recv()
            d2d(c).start()

    def finish(cin, cout, send, recv):
        for ici, d2d, chip, s, c in copies(cout, send, recv):
            d2d(1 - c).wait_recv()
        for ici, d2d, chip, s, c in copies(cout, send, recv):
            ici(chip).wait_send()
            d2d(c).wait_send()

    return _Comm([placed[nm] for nm in names], [SDS(placed[nm].shape, placed[nm].dtype) for nm in names], 6 * n,
                 start, finish, mid, aliases={a: a for a in range(n)})


def _half_rows(nm):
    return SHARD_SHAPE[nm][0] // 2


RS_TILE = 128


def _exchange_comm(ins, out_shapes, copies, n_sems, aliases=None):
    def start(cin, cout, send, recv):
        for cp in copies(cin, cout, send, recv):
            cp.start()

    def finish(cin, cout, send, recv):
        for cp in copies(cin, cout, send, recv):
            cp.wait()

    return _Comm(ins, out_shapes, n_sems, start, finish, aliases=aliases)


def _pair_exchange_comm(grads, names):
    def copies(cin, cout, send, recv):
        x, y, c, chip, chips = _position()
        return [_remote(_shard_window(nm, cin[a], s, 1 - c), cout[a].at[s], send, recv, 4 * a + s, (x, y, 1 - c))
                for a, nm in enumerate(names) for s in range(4)]

    return _exchange_comm([grads[nm] for nm in names],
                          [SDS((4, _half_rows(nm), SHARD_SHAPE[nm][1]), f32) for nm in names], copies, 4 * len(names))


def _pair_sum(nm, g, got, chip_core):
    R, C = SHARD_SHAPE[nm]
    hr = R // 2
    nt = hr // RS_TILE
    if SHARD_AXIS[nm] == 1:
        g_map = lambda i, s, sc: (sc[1] * nt + i, s)
    else:
        g_map = lambda i, s, sc: (s * (R // RS_TILE) + sc[1] * nt + i, 0)

    def body(sc_ref, g_ref, got_ref, o16_ref, own_ref):
        v = g_ref[...] + got_ref[0]
        o16_ref[0] = v.astype(bf16)

        @pl.when(pl.program_id(1) == sc_ref[0])
        def _():
            own_ref[...] = v

    blk3 = pl.BlockSpec((1, RS_TILE, C), lambda i, s, sc: (s, i, 0))
    return pl.pallas_call(
        body,
        grid_spec=pltpu.PrefetchScalarGridSpec(
            num_scalar_prefetch=1, grid=(nt, 4),
            in_specs=[pl.BlockSpec((RS_TILE, C), g_map), blk3],
            out_specs=[blk3, pl.BlockSpec((RS_TILE, C), lambda i, s, sc: (i, 0))]),
        out_shape=[SDS((4, hr, C), bf16), SDS((hr, C), f32)], name="rs_pair_sum_" + nm,
        compiler_params=_cparams(("parallel", "arbitrary"), 32))(chip_core, g, got)


def _chip_exchange_comm(cp, names):
    def copies(cin, cout, send, recv):
        x, y, c, chip, chips = _position()
        return [_remote(cin[a].at[2 * cx + cy], cout[a].at[j], send, recv, 3 * a + j, (cx, cy, c))
                for a, nm in enumerate(names) for j, (cx, cy) in enumerate(chips)]

    return _exchange_comm([cp[nm] for nm in names],
                          [SDS((3, _half_rows(nm), SHARD_SHAPE[nm][1]), bf16) for nm in names], copies, 3 * len(names))


def _chip_sum(nm, own, rc, chip_core):
    R, C = SHARD_SHAPE[nm]
    nt = (R // 2) // RS_TILE

    def body(sc_ref, own_ref, rc_ref, o_ref):
        o_ref[...] = own_ref[...] + rc_ref[0].astype(f32) + rc_ref[1].astype(f32) + rc_ref[2].astype(f32)

    return pl.pallas_call(
        body,
        grid_spec=pltpu.PrefetchScalarGridSpec(
            num_scalar_prefetch=1, grid=(nt,),
            in_specs=[pl.BlockSpec((RS_TILE, C), lambda i, sc: (i, 0)),
                      pl.BlockSpec((3, RS_TILE, C), lambda i, sc: (0, i, 0))],
            out_specs=pl.BlockSpec((RS_TILE, C), lambda i, sc: (sc[1] * nt + i, 0))),
        out_shape=SDS((R, C), f32), name="rs_chip_sum_" + nm,
        compiler_params=_cparams(("parallel",), 32))(chip_core, own, rc)


def _pair_share_comm(tot, names):
    def copies(cin, cout, send, recv):
        x, y, c, chip, chips = _position()
        cps = []
        for a, nm in enumerate(names):
            hr = _half_rows(nm)
            mine = cout[a].at[pl.ds(c * hr, hr), :]
            cps.append(_remote(mine, mine, send, recv, a, (x, y, 1 - c)))
        return cps

    return _exchange_comm([tot[nm] for nm in names], [SDS(SHARD_SHAPE[nm], f32) for nm in names], copies, len(names),
                          aliases={a: a for a in range(len(names))})


SMALL_ROWS = 40


def _allreduce_small(block):
    def body(x_ref, out_ref, buf, send_sems, recv_sems, local_sem):
        x, y, c, chip, chips = _position()
        me, sibling = (x, y, c), (x, y, 1 - c)

        def slot(px, py, pc):
            return buf.at[4 * px + 2 * py + pc]

        def copy(k, block_of, to, src=None):
            return _remote(slot(*block_of) if src is None else src, slot(*block_of), send_sems, recv_sems, k, to)

        mine = pltpu.make_async_copy(x_ref, slot(*me), local_sem)
        mine.start()
        first = [copy(0, me, sibling, src=x_ref)] + [copy(1 + j, me, (*ch, c), src=x_ref) for j, ch in enumerate(chips)]
        for cp in first:
            cp.start()
        passed = [copy(4 + j, (*ch, c), sibling) for j, ch in enumerate(chips)]
        for j, ch in enumerate(chips):
            copy(1 + j, (*ch, c), me).wait_recv()
            passed[j].start()
        copy(0, sibling, me).wait_recv()
        for j, ch in enumerate(chips):
            copy(4 + j, (*ch, 1 - c), me).wait_recv()
        for cp in first + passed:
            cp.wait_send()
        mine.wait()
        acc = buf[0]
        for d in range(1, 8):
            acc = acc + buf[d]
        out_ref[...] = acc

    vm = pl.BlockSpec(memory_space=pltpu.VMEM)
    return pl.pallas_call(
        body, in_specs=[vm], out_specs=vm, out_shape=SDS((SMALL_ROWS, D), f32),
        scratch_shapes=[pltpu.VMEM((8, SMALL_ROWS, D), f32), pltpu.SemaphoreType.DMA((7,)), pltpu.SemaphoreType.DMA((7,)),
                        pltpu.SemaphoreType.DMA],
        name="allreduce_small")(block)


def _adamw(w, g, m, v, name):
    rows, cols = w.shape
    tr = 256 if rows % 256 == 0 else rows

    def body(w_ref, g_ref, m_ref, v_ref, d_ref, nm_ref, nv_ref):
        gv = g_ref[...]
        m2 = ADAM_B1 * m_ref[...] + (1.0 - ADAM_B1) * gv
        v2 = ADAM_B2 * v_ref[...] + (1.0 - ADAM_B2) * jnp.square(gv)
        m_hat = m2 / (1.0 - ADAM_B1 ** ADAM_STEP)
        v_hat = v2 / (1.0 - ADAM_B2 ** ADAM_STEP)
        d_ref[...] = -ADAM_LR * (m_hat / (jnp.sqrt(v_hat) + ADAM_EPS) + ADAM_WD * w_ref[...])
        nm_ref[...] = m2
        nv_ref[...] = v2

    spec = pl.BlockSpec((tr, cols), lambda i: (i, 0))
    return pl.pallas_call(body, grid=(rows // tr,), in_specs=[spec] * 4, out_specs=[spec] * 3,
                          out_shape=[SDS((rows, cols), f32)] * 3, name=name,
                          compiler_params=_cparams(("parallel",), 40))(w, g, m, v)


WEIGHTS = ["norm_mix_g", "w_in", "q_norm_g", "k_norm_g", "attn_sinks", "rel_bias", "w_attn_o", "w_dw", "b_dw",
           "conv_ln_g", "conv_ln_b", "w_conv_out", "w_out", "norm_mlp_g", "w_ff1", "w_ff2"]
ROW_VECS = ["norm_mix_g", "b_dw", "conv_ln_g", "conv_ln_b", "norm_mlp_g"]
MISC_ROW = 5
W_DW_ROW = 8


def _pack_small(vals, loss=None):
    misc = [vals["q_norm_g"].reshape(1, HD), vals["k_norm_g"].reshape(1, HD), vals["attn_sinks"].reshape(1, NQ),
            jnp.zeros((1, 1), f32) if loss is None else loss.reshape(1, 1), jnp.zeros((1, 111), f32),
            vals["rel_bias"].reshape(1, NBUCKET * NQ), jnp.zeros((1, 256), f32)]
    rows = [vals[nm].reshape(1, D) for nm in ROW_VECS] + [jnp.concatenate(misc, axis=1), jnp.zeros((2, D), f32)]
    return jnp.concatenate(rows, axis=0)


def _unpack_small(block):
    out = {nm: block[i:i + 1] for i, nm in enumerate(ROW_VECS)}
    misc = block[MISC_ROW]
    out["q_norm_g"] = misc[0:64].reshape(1, HD)
    out["k_norm_g"] = misc[64:128].reshape(1, HD)
    out["attn_sinks"] = misc[128:144].reshape(1, NQ)
    out["rel_bias"] = misc[256:768].reshape(NBUCKET, NQ)
    return out, misc[144]


def kernel(x, norm_mix_g, w_in, q_norm_g, k_norm_g, attn_sinks, rel_bias, w_attn_o, w_dw, b_dw, conv_ln_g, conv_ln_b, w_conv_out, w_out, norm_mlp_g, w_ff1, w_ff2, loss_target, m_norm_mix_g, m_w_in, m_q_norm_g, m_k_norm_g, m_attn_sinks, m_rel_bias, m_w_attn_o, m_w_dw, m_b_dw, m_conv_ln_g, m_conv_ln_b, m_w_conv_out, m_w_out, m_norm_mlp_g, m_w_ff1, m_w_ff2, v_norm_mix_g, v_w_in, v_q_norm_g, v_k_norm_g, v_attn_sinks, v_rel_bias, v_w_attn_o, v_w_dw, v_b_dw, v_conv_ln_g, v_conv_ln_b, v_w_conv_out, v_w_out, v_norm_mlp_g, v_w_ff1, v_w_ff2):
    args = dict(locals())
    wts = {nm: args[nm] for nm in WEIGHTS}
    mom = {nm: args["m_" + nm] for nm in WEIGHTS}
    var = {nm: args["v_" + nm] for nm in WEIGHTS}
    chip = 2 * lax.axis_index("x") + lax.axis_index("y")

    chip_arr = jnp.reshape(chip, (1,)).astype(jnp.int32)
    chip_core = jnp.stack([chip, lax.axis_index("c")]).astype(jnp.int32)
    placed = {nm: _place_shard(nm, wts[nm][0], chip_arr, bf16) for nm in BIG}
    placed["w_dw"] = _place_shard("w_dw", jnp.pad(w_dw[0], ((0, 1), (0, 0))), chip_arr, f32)

    loss_part, grad_x, g, shards = _forward_backward(x[0], loss_target[0], wts, placed, chip_core)

    small = jnp.concatenate([_pack_small(g, loss_part), g["w_dw"]], axis=0)
    small = _allreduce_small(small)
    grads, loss = _unpack_small(small)
    grads["w_dw"] = lax.dynamic_slice(small[W_DW_ROW:W_DW_ROW + CW], (0, chip * (D // 4)), (CW, D // 4))
    grads.update(shards)

    delta, new_m, new_v = {}, {}, {}
    sd, sm, sv = _adamw(_pack_small(wts), small[0:8], _pack_small(mom), _pack_small(var), "adamw_small")
    for res, blk in ((delta, sd), (new_m, sm), (new_v, sv)):
        res.update(_unpack_small(blk)[0])
    for nm in BIG + ["w_dw"]:
        shp = wts[nm].shape
        two_d = lambda a: a.reshape(shp[-2], shp[-1])
        delta[nm], new_m[nm], new_v[nm] = _adamw(two_d(wts[nm]), grads[nm], two_d(mom[nm]), two_d(var[nm]), "adamw_" + nm)

    def shaped(vals):
        return [vals[nm].reshape(wts[nm].shape) for nm in WEIGHTS]

    return (loss, grad_x[None], *shaped(grads), *shaped(delta), *shaped(new_m), *shaped(new_v))
```

```python
import numpy as np
import jax
import jax.numpy as jnp
from jax import lax
from jax.experimental import pallas as pl
from jax.experimental.pallas import tpu as pltpu

f32 = jnp.float32
bf16 = jnp.bfloat16
SDS = jax.ShapeDtypeStruct
MESH = pl.DeviceIdType.MESH

D = 1024
HD = 64
NQ = 16
NKV = 4
BLK = 128
CW = 31
HALO = 32
DFF = 4096
NBUCKET = 32
EPS = 1e-6
NEG = -1e30
INW = 5632
MIX_CHUNK = 256
C_Q, C_A, C_G, C_GA, C_GC = 0, 1, 2, 3, 4
C_KV = 10

ADAM_LR = 0.001
ADAM_B1 = 0.9
ADAM_B2 = 0.999
ADAM_EPS = 1e-08
ADAM_WD = 0.01
ADAM_STEP = 10

VMEM_BYTES_V7X = 64 << 20


def _cparams(sem, vmem_mb):
    assert (vmem_mb << 20) < VMEM_BYTES_V7X
    return pltpu.CompilerParams(dimension_semantics=sem, vmem_limit_bytes=vmem_mb << 20)


ANY = pl.BlockSpec(memory_space=pl.ANY)


class _Comm:
    def __init__(self, ins, out_shapes, n_sems, start, finish, mid=None, aliases=None):
        self.ins, self.out_shapes, self.n_sems = list(ins), list(out_shapes), n_sems
        self.start, self.finish, self.mid, self.aliases = start, finish, mid, dict(aliases or {})


class _SemOffset:
    def __init__(self, sems, base):
        self._sems, self._base = sems, base
        self.at = self

    def __getitem__(self, k):
        return self._sems.at[self._base + k]


def _merge_comms(a, b):
    assert a.mid is None and b.mid is None
    n_in, n_out = len(a.ins), len(a.out_shapes)

    def both(fa, fb):
        def run(cin, cout, send, recv):
            fa(cin[:n_in], cout[:n_out], send, recv)
            fb(cin[n_in:], cout[n_out:], _SemOffset(send, a.n_sems), _SemOffset(recv, a.n_sems))
        return run

    aliases = {**a.aliases, **{n_in + k: n_out + v for k, v in b.aliases.items()}}
    return _Comm(a.ins + b.ins, a.out_shapes + b.out_shapes, a.n_sems + b.n_sems, both(a.start, b.start),
                 both(a.finish, b.finish), aliases=aliases)


def _call(body, args, *, grid, in_specs, out_specs, out_shape, name, sem, vmem_mb, scratch_shapes=(), comm=None,
          mid_step=None):
    n_in, n_out, n_scr = len(in_specs), len(out_specs), len(scratch_shapes)
    if comm is None:
        outs = pl.pallas_call(body, grid=grid, in_specs=list(in_specs), out_specs=list(out_specs),
                              out_shape=list(out_shape), scratch_shapes=list(scratch_shapes), name=name,
                              compiler_params=_cparams(sem, vmem_mb))(*args)
        return list(outs), []
    ci, co = len(comm.ins), len(comm.out_shapes)
    last = grid[0] - 1

    def wrapped(*refs):
        ins, cin = refs[:n_in], refs[n_in:n_in + ci]
        outs = refs[n_in + ci:n_in + ci + n_out]
        cout = refs[n_in + ci + n_out:n_in + ci + n_out + co]
        scr = refs[n_in + ci + n_out + co:]
        send, recv = scr[n_scr], scr[n_scr + 1]
        step = pl.program_id(0)

        @pl.when(step == 0)
        def _():
            comm.start(cin, cout, send, recv)

        body(*ins, *outs, *scr[:n_scr])
        if comm.mid is not None:
            @pl.when(step == mid_step)
            def _():
                comm.mid(cin, cout, send, recv)

        @pl.when(step == last)
        def _():
            comm.finish(cin, cout, send, recv)

    res = pl.pallas_call(
        wrapped, grid=grid, in_specs=list(in_specs) + [ANY] * ci, out_specs=list(out_specs) + [ANY] * co,
        out_shape=list(out_shape) + comm.out_shapes,
        input_output_aliases={n_in + k: n_out + v for k, v in comm.aliases.items()},
        scratch_shapes=list(scratch_shapes) + [pltpu.SemaphoreType.DMA((comm.n_sems,))] * 2,
        name=name, compiler_params=_cparams(("arbitrary",), vmem_mb))(*args, *comm.ins)
    return list(res[:n_out]), list(res[n_out:])


def _run_comm(comm, name):
    ci, co = len(comm.ins), len(comm.out_shapes)

    def body(*refs):
        cin, cout, (send, recv) = refs[:ci], refs[ci:ci + co], refs[ci + co:]
        comm.start(cin, cout, send, recv)
        if comm.mid is not None:
            comm.mid(cin, cout, send, recv)
        comm.finish(cin, cout, send, recv)

    return pl.pallas_call(
        body, in_specs=[ANY] * ci, out_specs=[ANY] * co, out_shape=comm.out_shapes, input_output_aliases=comm.aliases,
        scratch_shapes=[pltpu.SemaphoreType.DMA((comm.n_sems,))] * 2, name=name)(*comm.ins)


def _dot(a, b):
    return jnp.dot(a, b, preferred_element_type=f32)


def _dot_nt(a, b):
    return lax.dot_general(a, b, (((1,), (1,)), ((), ())), preferred_element_type=f32)


def _dot_tn(a, b):
    return lax.dot_general(a, b, (((0,), (0,)), ((), ())), preferred_element_type=f32)


def _sigmoid(x):
    return 0.5 * jnp.tanh(0.5 * x) + 0.5


def _low_head_lanes():
    return lax.broadcasted_iota(jnp.int32, (1, 2 * HD), 1) < HD


def _head_blockdiag():
    r = lax.broadcasted_iota(jnp.int32, (2 * HD, 2 * HD), 0) // HD
    c = lax.broadcasted_iota(jnp.int32, (2 * HD, 2 * HD), 1) // HD
    return jnp.where(r == c, 1.0, 0.0).astype(bf16)


def _head_sums(z, bd):
    hi = z.astype(bf16)
    lo = (z - hi.astype(f32)).astype(bf16)
    return _dot(hi, bd) + _dot(lo, bd)


def _weight_cols(start, width):
    kv_width = 2 * NKV * HD
    if start < D:
        orig = start
    elif start < INW - kv_width:
        orig = start + kv_width
    else:
        orig = start - (INW - kv_width) + D
    assert (start < D) == (start + width <= D) and (start < INW - kv_width) == (start + width <= INW - kv_width)
    return slice(orig, orig + width)


def _rms_inproj(x, g, w, perm, gq2, gk2, comm=None):
    T, N = x.shape[0], w.shape[1]
    tn = 512
    conv_cols = (C_A * D, (C_G + 1) * D)
    attn_chunks = [C_Q * D // tn, C_Q * D // tn + 1, C_KV]

    def body(x_ref, g_ref, w_ref, perm_ref, gq_ref, gk_ref, p_ref, u_ref, tail_ref, qn_ref, kk_ref, vv_ref):
        xv = x_ref[...]
        r = lax.rsqrt(jnp.mean(xv * xv, axis=-1, keepdims=True) + EPS)
        u = (xv * r * g_ref[...]).astype(bf16)
        u_ref[...] = u
        u_blocks = _dot(perm_ref[...], u).astype(bf16)

        def project(c):
            lhs = u_blocks if conv_cols[0] <= c * tn < conv_cols[1] else u
            p_ref[:, c * tn:(c + 1) * tn] = _dot(lhs, w_ref[:, _weight_cols(c * tn, tn)])

        for c in attn_chunks:
            project(c)
        bd = _head_blockdiag()
        lo = _low_head_lanes()
        for p in range(NQ // 2):
            z = p_ref[:, C_Q * D + 128 * p:C_Q * D + 128 * p + 128]
            rq = lax.rsqrt(_head_sums(z * z, bd) * (1.0 / HD) + EPS)
            qn_ref[:, 128 * p:128 * p + 128] = (z * rq * gq_ref[...] * (HD ** -0.5)).astype(bf16)
        kv0 = C_KV * tn
        for p in range(NKV // 2):
            z = p_ref[:, kv0 + 128 * p:kv0 + 128 * p + 128]
            rk = lax.rsqrt(_head_sums(z * z, bd) * (1.0 / HD) + EPS)
            _split_pair(z * rk * gk_ref[...], kk_ref, p, lo)
            _split_pair(p_ref[:, kv0 + 256 + 128 * p:kv0 + 256 + 128 * p + 128], vv_ref, p, lo)
        for c in range(N // tn):
            if c not in attn_chunks:
                project(c)
        tail_ref[...] = _dot(u[TT - NBLK:TT], w_ref[:, _weight_cols(conv_cols[0], conv_cols[1] - conv_cols[0])])

    once = pl.Buffered(1)
    row = pl.BlockSpec((TT, D), lambda i: (i, 0))
    vec = pl.BlockSpec((1, 128), lambda i: (0, 0))
    return _call(
        body, (x, g, w, perm, gq2, gk2), grid=(T // TT,),
        in_specs=[row, pl.BlockSpec((1, D), lambda i: (0, 0)),
                  pl.BlockSpec((D, N), lambda i: (0, 0), pipeline_mode=once),
                  pl.BlockSpec((TT, TT), lambda i: (0, 0), pipeline_mode=once), vec, vec],
        out_specs=[pl.BlockSpec((TT, N), lambda i: (i, 0)), row, pl.BlockSpec((NBLK, 2 * D), lambda i: (i, 0)),
                   row, row, row],
        out_shape=[SDS((T, N), f32), SDS((T, D), bf16), SDS((T // TT * NBLK, 2 * D), f32)] + [SDS((T, D), bf16)] * 3,
        name="rms_inproj", sem=("parallel",), vmem_mb=56, comm=comm, mid_step=(3 * (T // TT)) // 4)


def _split_pair(pair, out_ref, p, lo):
    rolled = pltpu.roll(pair, HD, 1)
    zero = jnp.zeros_like(pair)
    c = 512 * p
    out_ref[:, c:c + 128] = jnp.where(lo, pair, zero).astype(bf16)
    out_ref[:, c + 128:c + 256] = jnp.where(lo, zero, rolled).astype(bf16)
    out_ref[:, c + 256:c + 384] = jnp.where(lo, rolled, zero).astype(bf16)
    out_ref[:, c + 384:c + 512] = jnp.where(lo, zero, pair).astype(bf16)


def _bucket_tile():
    qi = np.arange(BLK)[:, None]
    kj = np.arange(BLK)[None, :]
    n = np.where(kj > qi, qi + BLK - kj, qi - kj)
    max_exact = NBUCKET // 2
    nf = np.maximum(n, 1).astype(np.float32)
    large = max_exact + (np.log(nf / max_exact) / np.float32(np.log(128 / max_exact))
                         * (NBUCKET - max_exact)).astype(np.int32)
    large = np.minimum(large, NBUCKET - 1)
    return np.where(n < max_exact, n, large).astype(np.int32)


def _from_prev_block():
    return lax.broadcasted_iota(jnp.int32, (BLK, BLK), 1) > lax.broadcasted_iota(jnp.int32, (BLK, BLK), 0)


def _bias_tiles(rel_bias):
    def body(rb_ref, bk_ref, out_ref):
        bk = bk_ref[...]
        for h in range(NQ):
            acc = jnp.zeros((BLK, BLK), f32)
            for b in range(NBUCKET):
                acc = jnp.where(bk == b, rb_ref[b, h], acc)
            out_ref[h] = acc

    return pl.pallas_call(
        body,
        in_specs=[pl.BlockSpec(memory_space=pltpu.SMEM), pl.BlockSpec(memory_space=pltpu.VMEM)],
        out_specs=pl.BlockSpec(memory_space=pltpu.VMEM),
        out_shape=SDS((NQ, BLK, BLK), f32),
        name="bias_tiles")(rel_bias, jnp.asarray(_bucket_tile()))


def _rows2(ref, c):
    return jnp.concatenate([ref[:, c:c + 128], ref[:, c + 128:c + 256]], axis=0)


def _attn_fwd(qn, kk, vv, bias, sinks, comm=None):
    T = qn.shape[0]
    nb = T // BLK

    def body(s_ref, q_ref, kc_ref, kp_ref, vc_ref, vp_ref, b_ref, o_ref, lse_ref):
        prev = _from_prev_block()
        no_key = jnp.logical_and(prev, pl.program_id(0) == 0)
        scores = []
        for h in range(NKV):
            qs = _rows2(q_ref, 256 * h)
            scores.append((_dot_nt(qs, _rows2(kc_ref, 256 * h)), _dot_nt(qs, _rows2(kp_ref, 256 * h))))
        for h in range(NKV):
            c = 256 * h
            sc, sp = scores[h]
            vstack = jnp.concatenate([vp_ref[:, c:c + 128], vc_ref[:, c:c + 128],
                                      vp_ref[:, c + 128:c + 256], vc_ref[:, c + 128:c + 256]], axis=0)
            for pr in range(2):
                ps = []
                for e in range(2):
                    hq = 4 * h + 2 * pr + e
                    rows, cols = slice(128 * pr, 128 * pr + 128), slice(128 * e, 128 * e + 128)
                    s = jnp.where(no_key, NEG, jnp.where(prev, sp[rows, cols], sc[rows, cols]) + b_ref[hq])
                    sink = s_ref[0, hq]
                    m = jnp.maximum(jnp.max(s, axis=-1, keepdims=True), sink)
                    ex = jnp.exp(s - m)
                    l = jnp.sum(ex, axis=-1, keepdims=True) + jnp.exp(sink - m)
                    p = ex * (1.0 / l)
                    ps += [jnp.where(prev, p, 0.0).astype(bf16), jnp.where(prev, 0.0, p).astype(bf16)]
                    lse_ref[:, hq:hq + 1] = m + jnp.log(l)
                o_ref[:, c + 128 * pr:c + 128 * pr + 128] = _dot(jnp.concatenate(ps, axis=1), vstack).astype(bf16)

    blk = lambda f: pl.BlockSpec((BLK, D), f)
    cur = lambda n: (n, 0)
    prev = lambda n: (jnp.maximum(n - 1, 0), 0)
    return _call(
        body, (sinks, qn, kk, kk, vv, vv, bias), grid=(nb,),
        in_specs=[pl.BlockSpec(memory_space=pltpu.SMEM), blk(cur), blk(cur), blk(prev), blk(cur), blk(prev),
                  pl.BlockSpec((NQ, BLK, BLK), lambda n: (0, 0, 0))],
        out_specs=[blk(cur), pl.BlockSpec((BLK, NQ), cur)],
        out_shape=[SDS((T, D), bf16), SDS((T, NQ), f32)],
        name="attn_fwd", sem=("parallel",), vmem_mb=32, comm=comm, mid_step=(3 * nb) // 4)


TT = 512
NBLK = 32
RPB = TT // NBLK
CONV_LANES = 128
KBLK = 4
GBLK = 8
TAPG = 8


def _block_perm():
    p = np.arange(TT)
    m = np.zeros((TT, TT), np.float32)
    m[p, NBLK * (p % RPB) + p // RPB] = 1.0
    return jnp.asarray(m, bf16), jnp.asarray(m.T, bf16)


def _lane_groups():
    return [slice(q * CONV_LANES, (q + 1) * CONV_LANES) for q in range(D // CONV_LANES)]


def _fill_time_blocks(z, tile, edge, causal):
    row = lax.broadcasted_iota(jnp.int32, (RPB, 1), 0)
    for k in range(NBLK):
        blk = tile[RPB * k:RPB * (k + 1)]
        if causal:
            z[NBLK + k] = blk
            z[k] = jnp.where(row == 0, edge[k:k + 1], pltpu.roll(blk, 1, 0))
        else:
            z[k] = blk
            z[NBLK + k] = jnp.where(row == RPB - 1, edge[k:k + 1], pltpu.roll(blk, RPB - 1, 0))


def _block_conv(z, w_ref, tap_offset, init, store):
    def step(s, carry):
        k0 = s * KBLK
        for ln in _lane_groups():
            accs = [init(ln) for _ in range(KBLK)]
            for g0 in range(0, CW, TAPG):
                taps = range(g0, min(g0 + TAPG, CW))
                lo = min(tap_offset(j) for j in taps)
                hi = max(tap_offset(j) for j in taps)
                win = [z[k0 + lo + d, :, ln] for d in range(KBLK + hi - lo)]
                for j in taps:
                    wv = w_ref[j:j + 1, ln]
                    for q in range(KBLK):
                        accs[q] = accs[q] + win[q + tap_offset(j) - lo] * wv
            for q in range(KBLK):
                store(k0 + q, ln, accs[q])
        return carry

    lax.fori_loop(0, NBLK // KBLK, step, 0)


def _block_rows(k):
    return pl.ds(pl.multiple_of(k * RPB, RPB), RPB)


def _glu_conv_fwd(proj, tail, w_dw, b_dw, ln_g, ln_b, perm_t, comm=None):
    T = proj.shape[0]

    def body(a_ref, g_ref, ta_ref, tg_ref, w_ref, b_ref, lg_ref, lb_ref, pt_ref, h1_ref, h3_ref, z):
        edge = jnp.where(pl.program_id(0) > 0, ta_ref[...] * _sigmoid(tg_ref[...]), 0.0)
        _fill_time_blocks(z, a_ref[...] * _sigmoid(g_ref[...]), edge, causal=True)

        def store(k, ln, value):
            h1_ref[_block_rows(k), ln] = value

        _block_conv(z, w_ref, lambda j: NBLK - (CW - 1) + j,
                    lambda ln: jnp.broadcast_to(b_ref[:, ln], (RPB, CONV_LANES)), store)
        h1 = h1_ref[...]
        mu = jnp.mean(h1, axis=-1, keepdims=True)
        xc = h1 - mu
        var = jnp.mean(xc * xc, axis=-1, keepdims=True)
        h2 = xc * lax.rsqrt(var + EPS) * lg_ref[...] + lb_ref[...]
        h3_ref[...] = _dot(pt_ref[...], (h2 * _sigmoid(h2)).astype(bf16)).astype(bf16)

    tile = lambda cb: pl.BlockSpec((TT, D), lambda i: (i, cb))
    edge = lambda cb: pl.BlockSpec((NBLK, D), lambda i: (jnp.maximum(i - 1, 0), cb))
    vec = pl.BlockSpec((1, D), lambda i: (0, 0))
    (h1, h3), got = _call(
        body, (proj, proj, tail, tail, w_dw, b_dw, ln_g, ln_b, perm_t), grid=(T // TT,),
        in_specs=[tile(C_A), tile(C_G), edge(0), edge(1), pl.BlockSpec((HALO, D), lambda i: (0, 0)), vec, vec, vec,
                  pl.BlockSpec((TT, TT), lambda i: (0, 0), pipeline_mode=pl.Buffered(1))],
        out_specs=[pl.BlockSpec((TT, D), lambda i: (i, 0))] * 2,
        out_shape=[SDS((T, D), f32), SDS((T, D), bf16)],
        scratch_shapes=[pltpu.VMEM((2 * NBLK, RPB, D), f32)],
        name="glu_conv_fwd", sem=("parallel",), vmem_mb=40, comm=comm, mid_step=(3 * (T // TT)) // 4)
    return h1, h3, got


def _mix_out(o, h3, proj, x, w_attn_o, w_conv_out, w_out, g_mlp):
    T = x.shape[0]
    tm = 512

    def body(o_ref, h3_ref, ga_ref, gc_ref, x_ref, wa_ref, wc_ref, wo_ref, g_ref,
             attn_ref, conv_ref, mg_ref, x1_ref, n2_ref):
        x1 = x_ref[...]
        for j in range(D // MIX_CHUNK):
            cols = slice(j * MIX_CHUNK, (j + 1) * MIX_CHUNK)
            attn = _dot(o_ref[...], wa_ref[:, cols])
            conv = _dot(h3_ref[...], wc_ref[:, cols])
            attn_ref[:, cols] = attn.astype(bf16)
            conv_ref[:, cols] = conv.astype(bf16)
            mg = (_sigmoid(ga_ref[:, cols]) * attn + _sigmoid(gc_ref[:, cols]) * conv).astype(bf16)
            mg_ref[:, cols] = mg
            x1 = x1 + _dot(mg, wo_ref[cols, :])
        x1_ref[...] = x1
        r = lax.rsqrt(jnp.mean(x1 * x1, axis=-1, keepdims=True) + EPS)
        n2_ref[...] = (x1 * r * g_ref[...]).astype(bf16)

    tile = lambda cb=0: pl.BlockSpec((tm, D), lambda i: (i, cb))
    wfull = pl.BlockSpec((D, D), lambda i: (0, 0), pipeline_mode=pl.Buffered(1))
    return pl.pallas_call(
        body, grid=(T // tm,),
        in_specs=[tile(), tile(), tile(C_GA), tile(C_GC), tile(), wfull, wfull, wfull,
                  pl.BlockSpec((1, D), lambda i: (0, 0))],
        out_specs=[tile()] * 5,
        out_shape=[SDS((T, D), bf16), SDS((T, D), bf16), SDS((T, D), bf16), SDS((T, D), f32), SDS((T, D), bf16)],
        name="mix_out", compiler_params=_cparams(("parallel",), 48))(o, h3, proj, proj, x, w_attn_o, w_conv_out, w_out, g_mlp)


def _mlp_fwd(n2, w1, w2, x1, tgt):
    T = n2.shape[0]
    tm, tf = 512, 1024

    def body(n2_ref, w1_ref, w2_ref, x1_ref, t_ref, hm_ref, slope_ref, dy_ref, dyb_ref, loss_ref):
        @pl.when(pl.program_id(0) == 0)
        def _():
            loss_ref[...] = jnp.zeros_like(loss_ref)

        n2v = n2_ref[...]
        for c in range(DFF // tf):
            r = jnp.maximum(_dot(n2v, w1_ref[:, c * tf:(c + 1) * tf]), 0.0)
            hm_ref[:, c * tf:(c + 1) * tf] = (r * r).astype(bf16)
            slope_ref[:, c * tf:(c + 1) * tf] = (2.0 * r).astype(bf16)
        e = x1_ref[...] + _dot(hm_ref[...], w2_ref[...]) - t_ref[...]
        dy = e * (1.0 / D)
        dy_ref[...] = dy
        dyb_ref[...] = dy.astype(bf16)
        loss_ref[...] += 0.5 * jnp.sum(jnp.sum(e * e, axis=-1, keepdims=True) * (1.0 / D))

    row = pl.BlockSpec((tm, D), lambda i: (i, 0))
    once = pl.Buffered(1)
    return pl.pallas_call(
        body, grid=(T // tm,),
        in_specs=[row, pl.BlockSpec((D, DFF), lambda i: (0, 0), pipeline_mode=once),
                  pl.BlockSpec((DFF, D), lambda i: (0, 0), pipeline_mode=once), row, row],
        out_specs=[pl.BlockSpec((tm, DFF), lambda i: (i, 0)), pl.BlockSpec((tm, DFF), lambda i: (i, 0)), row, row,
                   pl.BlockSpec((8, 128), lambda i: (0, 0))],
        out_shape=[SDS((T, DFF), bf16), SDS((T, DFF), bf16), SDS((T, D), f32), SDS((T, D), bf16), SDS((8, 128), f32)],
        name="mlp_fwd", compiler_params=_cparams(("arbitrary",), 60))(n2, w1, w2, x1, tgt)


def _rms_bwd(xv, g, dn, dres):
    r = lax.rsqrt(jnp.mean(xv * xv, axis=-1, keepdims=True) + EPS)
    gd = dn * g
    dx = dres + r * gd - xv * (r * r * r) * jnp.mean(xv * gd, axis=-1, keepdims=True)
    dg = jnp.sum(dn * xv * r, axis=0, keepdims=True)
    return dx, dg


def _mlp_bwd(dy, dyb, slope, w1, w2, x1, g_mlp):
    T = dy.shape[0]
    tm, tf = 512, 1024

    def body(dy_ref, dyb_ref, slope_ref, w1_ref, w2_ref, x1_ref, g_ref, df_ref, dx_ref, dxb_ref, dg_ref):
        @pl.when(pl.program_id(0) == 0)
        def _():
            dg_ref[...] = jnp.zeros_like(dg_ref)

        dyb = dyb_ref[...]
        for c in range(DFF // tf):
            cols = slice(c * tf, (c + 1) * tf)
            d_hm = _dot_nt(dyb, w2_ref[cols, :])
            df_ref[:, cols] = (d_hm * slope_ref[:, cols].astype(f32)).astype(bf16)
        dn = _dot_nt(df_ref[...], w1_ref[...])
        dx, dg = _rms_bwd(x1_ref[...], g_ref[...], dn, dy_ref[...])
        dx_ref[...] = dx
        dxb_ref[...] = dx.astype(bf16)
        dg_ref[...] += dg

    row = pl.BlockSpec((tm, D), lambda i: (i, 0))
    wide = pl.BlockSpec((tm, DFF), lambda i: (i, 0))
    vec = pl.BlockSpec((1, D), lambda i: (0, 0))
    once = pl.Buffered(1)
    return pl.pallas_call(
        body, grid=(T // tm,),
        in_specs=[row, row, wide, pl.BlockSpec((D, DFF), lambda i: (0, 0), pipeline_mode=once),
                  pl.BlockSpec((DFF, D), lambda i: (0, 0), pipeline_mode=once), row, vec],
        out_specs=[wide, row, row, vec],
        out_shape=[SDS((T, DFF), bf16), SDS((T, D), f32), SDS((T, D), bf16), SDS((1, D), f32)],
        name="mlp_bwd", compiler_params=_cparams(("arbitrary",), 56))(dy, dyb, slope, w1, w2, x1, g_mlp)


def _wgrad(a, b, name, tn=1024):
    T, M = a.shape
    N = b.shape[1]
    tmm, tk = min(M, 1024), min(T, 2048)

    def body(a_ref, b_ref, o_ref):
        @pl.when(pl.program_id(2) == 0)
        def _():
            o_ref[...] = jnp.zeros_like(o_ref)

        o_ref[...] += _dot_tn(a_ref[...], b_ref[...])

    return pl.pallas_call(
        body, grid=(M // tmm, N // tn, T // tk),
        in_specs=[pl.BlockSpec((tk, tmm), lambda m, n, t: (t, m)), pl.BlockSpec((tk, tn), lambda m, n, t: (t, n))],
        out_specs=pl.BlockSpec((tmm, tn), lambda m, n, t: (m, n)),
        out_shape=SDS((M, N), f32),
        name=name, compiler_params=_cparams(("parallel", "parallel", "arbitrary"), 40))(a, b)


def _mix_bwd(dx1b, proj, attn, conv, h1, w_attn_o, w_conv_out, w_out, ln_g, ln_b, perm, comm=None):
    T = dx1b.shape[0]
    tm = TT

    def body(dx_ref, ga_ref, gc_ref, attn_ref, conv_ref, h1_ref, wa_ref, wc_ref, wo_ref, lg_ref, lb_ref, perm_ref,
             dat_ref, dcv_ref, do_ref, dh1_ref, dga_ref, dgc_ref, acc_ref, head_ref):
        @pl.when(pl.program_id(0) == 0)
        def _():
            acc_ref[...] = jnp.zeros_like(acc_ref)

        d_o, d_h3 = None, None
        for j in range(D // MIX_CHUNK):
            cols = slice(j * MIX_CHUNK, (j + 1) * MIX_CHUNK)
            dm = _dot_nt(dx_ref[...], wo_ref[cols, :])
            sa = _sigmoid(ga_ref[:, cols])
            sc = _sigmoid(gc_ref[:, cols])
            dat = (dm * sa).astype(bf16)
            dcv = (dm * sc).astype(bf16)
            dat_ref[:, cols] = dat
            dcv_ref[:, cols] = dcv
            dga_ref[:, cols] = (dm * attn_ref[:, cols].astype(f32) * sa * (1.0 - sa)).astype(bf16)
            dgc_ref[:, cols] = (dm * conv_ref[:, cols].astype(f32) * sc * (1.0 - sc)).astype(bf16)
            part_o = _dot_nt(dat, wa_ref[:, cols])
            part_h = _dot_nt(_dot(perm_ref[...], dcv).astype(bf16), wc_ref[:, cols])
            d_o = part_o if d_o is None else d_o + part_o
            d_h3 = part_h if d_h3 is None else d_h3 + part_h
        do_ref[...] = d_o.astype(bf16)
        h1 = h1_ref[...]
        mu = jnp.mean(h1, axis=-1, keepdims=True)
        xc = h1 - mu
        rstd = lax.rsqrt(jnp.mean(xc * xc, axis=-1, keepdims=True) + EPS)
        xh = xc * rstd
        h2 = xh * lg_ref[...] + lb_ref[...]
        sg = _sigmoid(h2)
        dh2 = d_h3 * (sg * (1.0 + h2 * (1.0 - sg)))
        dxh = dh2 * lg_ref[...]
        dh1 = rstd * (dxh - jnp.mean(dxh, axis=-1, keepdims=True) - xh * jnp.mean(dxh * xh, axis=-1, keepdims=True))
        dh1_ref[...] = dh1
        for k in range(NBLK):
            head_ref[k:k + 1, :] = dh1[RPB * k:RPB * k + 1]
        acc_ref[0:1, :] += jnp.sum(dh2 * xh, axis=0, keepdims=True)
        acc_ref[1:2, :] += jnp.sum(dh2, axis=0, keepdims=True)
        acc_ref[2:3, :] += jnp.sum(dh1, axis=0, keepdims=True)

    tile = lambda cb=0: pl.BlockSpec((tm, D), lambda i: (i, cb))
    wfull = pl.BlockSpec((D, D), lambda i: (0, 0), pipeline_mode=pl.Buffered(1))
    vec = pl.BlockSpec((1, D), lambda i: (0, 0))
    return _call(
        body, (dx1b, proj, proj, attn, conv, h1, w_attn_o, w_conv_out, w_out, ln_g, ln_b, perm), grid=(T // tm,),
        in_specs=[tile(), tile(C_GA), tile(C_GC), tile(), tile(), tile(), wfull, wfull, wfull, vec, vec,
                  pl.BlockSpec((TT, TT), lambda i: (0, 0), pipeline_mode=pl.Buffered(1))],
        out_specs=[tile()] * 6 + [pl.BlockSpec((8, D), lambda i: (0, 0)), pl.BlockSpec((NBLK, D), lambda i: (i, 0))],
        out_shape=[SDS((T, D), bf16), SDS((T, D), bf16), SDS((T, D), bf16), SDS((T, D), f32),
                   SDS((T, D), bf16), SDS((T, D), bf16), SDS((8, D), f32), SDS((T // RPB, D), f32)],
        name="mix_bwd", sem=("arbitrary",), vmem_mb=56, comm=comm)


def _conv_bwd(dh1, head, proj, tail, w_dw, perm_t, comm=None):
    T = dh1.shape[0]
    nt = T // TT

    def body(d_ref, hd_ref, a_ref, g_ref, ta_ref, tg_ref, w_ref, pt_ref, da_ref, dg_ref, gw_ref, zd, zh, dh0, gacc):
        i = pl.program_id(0)

        @pl.when(i == 0)
        def _():
            gacc[...] = jnp.zeros_like(gacc)

        a = a_ref[...]
        sg = _sigmoid(g_ref[...])
        _fill_time_blocks(zd, d_ref[...], jnp.where(i < nt - 1, hd_ref[...], 0.0), causal=False)
        _fill_time_blocks(zh, a * sg, jnp.where(i > 0, ta_ref[...] * _sigmoid(tg_ref[...]), 0.0), causal=True)

        def store(k, ln, value):
            dh0[_block_rows(k), ln] = value

        _block_conv(zd, w_ref, lambda j: (CW - 1) - j, lambda ln: jnp.zeros((RPB, CONV_LANES), f32), store)

        for ln in _lane_groups():
            for g0 in range(0, CW, TAPG):
                taps = list(range(g0, min(g0 + TAPG, CW)))

                def add_blocks(s, accs, ln=ln, taps=taps):
                    k0 = s * GBLK
                    first = k0 + NBLK - (CW - 1) + taps[0]
                    win = [zh[first + t, :, ln] for t in range(GBLK + len(taps) - 1)]
                    accs = list(accs)
                    for q in range(GBLK):
                        d = zd[k0 + q, :, ln]
                        for n, j in enumerate(taps):
                            accs[n] = accs[n] + d * win[q + j - taps[0]]
                    return tuple(accs)

                accs = lax.fori_loop(0, NBLK // GBLK, add_blocks,
                                     tuple(jnp.zeros((RPB, CONV_LANES), f32) for _ in taps))
                for j, acc in zip(taps, accs):
                    gacc[j, :, ln] += acc

        d0 = dh0[...]
        da_ref[...] = _dot(pt_ref[...], (d0 * sg).astype(bf16)).astype(bf16)
        dg_ref[...] = _dot(pt_ref[...], (d0 * a * sg * (1.0 - sg)).astype(bf16)).astype(bf16)

        @pl.when(i == nt - 1)
        def _():
            gw_ref[...] = jnp.zeros_like(gw_ref)
            for j in range(CW):
                gw_ref[j:j + 1, :] = jnp.sum(gacc[j], axis=0, keepdims=True)

    tile = lambda cb=0: pl.BlockSpec((TT, D), lambda i: (i, cb))
    prev_edge = lambda cb: pl.BlockSpec((NBLK, D), lambda i: (jnp.maximum(i - 1, 0), cb))
    next_edge = pl.BlockSpec((NBLK, D), lambda i: (jnp.minimum(i + 1, nt - 1), 0))
    wspec = pl.BlockSpec((HALO, D), lambda i: (0, 0))
    return _call(
        body, (dh1, head, proj, proj, tail, tail, w_dw, perm_t), grid=(nt,),
        in_specs=[tile(), next_edge, tile(C_A), tile(C_G), prev_edge(0), prev_edge(1), wspec,
                  pl.BlockSpec((TT, TT), lambda i: (0, 0), pipeline_mode=pl.Buffered(1))],
        out_specs=[tile(), tile(), wspec],
        out_shape=[SDS((T, D), bf16), SDS((T, D), bf16), SDS((HALO, D), f32)],
        scratch_shapes=[pltpu.VMEM((2 * NBLK, RPB, D), f32), pltpu.VMEM((2 * NBLK, RPB, D), f32),
                        pltpu.VMEM((TT, D), f32), pltpu.VMEM((HALO, RPB, D), f32)],
        name="conv_bwd", sem=("arbitrary",), vmem_mb=48, comm=comm)


def _attn_bwd(qn, kk, vv, bias, sinks, o, do, lse, comm=None):
    T = qn.shape[0]
    nb = T // BLK

    def body(s_ref, q_ref, kc_ref, kp_ref, vc_ref, vp_ref, b_ref, o_ref, do_ref, lse_ref,
             dq_ref, dkc_ref, dkp_ref, dvc_ref, dvp_ref, dsk_ref, dsa_ref):
        n = pl.program_id(0)

        @pl.when(n == 0)
        def _():
            dsk_ref[...] = jnp.zeros_like(dsk_ref)
            dsa_ref[...] = jnp.zeros_like(dsa_ref)

        @pl.when(n == nb)
        def _():
            dkp_ref[...] = jnp.zeros_like(dkp_ref)
            dvp_ref[...] = jnp.zeros_like(dvp_ref)

        @pl.when(n < nb)
        def _():
            from_prev = _from_prev_block()
            no_key = jnp.logical_and(from_prev, n == 0)
            lo = _low_head_lanes()
            dups = {"kc": [], "kp": [], "vc": [], "vp": []}
            products = []
            for h in range(NKV):
                qs = _rows2(q_ref, 256 * h)
                dos = _rows2(do_ref, 256 * h)
                products.append((qs, dos, _dot_nt(qs, _rows2(kc_ref, 256 * h)), _dot_nt(qs, _rows2(kp_ref, 256 * h)),
                                 _dot_nt(dos, _rows2(vc_ref, 256 * h)), _dot_nt(dos, _rows2(vp_ref, 256 * h))))
            for h in range(NKV):
                c = 256 * h
                qs, dos, sc, sp, dpc, dpp = products[h]
                kstack = jnp.concatenate([kp_ref[:, c:c + 128], kc_ref[:, c:c + 128],
                                          kp_ref[:, c + 128:c + 256], kc_ref[:, c + 128:c + 256]], axis=0)
                p_c, p_p, ds_c, ds_p = [], [], [], []
                for pr in range(2):
                    cc = c + 128 * pr
                    prod = do_ref[:, cc:cc + 128].astype(f32) * o_ref[:, cc:cc + 128].astype(f32)
                    d_lo = jnp.sum(jnp.where(lo, prod, 0.0), axis=-1, keepdims=True)
                    d_hi = jnp.sum(prod, axis=-1, keepdims=True) - d_lo
                    row_pc, row_pp, row_dc, row_dp = [], [], [], []
                    for e in range(2):
                        hq = 4 * h + 2 * pr + e
                        rows, cols = slice(128 * pr, 128 * pr + 128), slice(128 * e, 128 * e + 128)
                        delta = d_lo if e == 0 else d_hi
                        lse = lse_ref[:, hq:hq + 1]
                        s = jnp.where(from_prev, sp[rows, cols], sc[rows, cols]) + b_ref[hq]
                        p = jnp.where(no_key, 0.0, jnp.exp(s - lse))
                        ds = p * (jnp.where(from_prev, dpp[rows, cols], dpc[rows, cols]) - delta)
                        dsa_ref[hq] += ds
                        dsk_ref[hq] += jnp.broadcast_to(-jnp.sum(jnp.exp(s_ref[0, hq] - lse) * delta), (8, 128))
                        row_pc.append(jnp.where(from_prev, 0.0, p).astype(bf16))
                        row_pp.append(jnp.where(from_prev, p, 0.0).astype(bf16))
                        row_dc.append(jnp.where(from_prev, 0.0, ds).astype(bf16))
                        row_dp.append(jnp.where(from_prev, ds, 0.0).astype(bf16))
                    dq_ref[:, cc:cc + 128] = _dot(jnp.concatenate([row_dp[0], row_dc[0], row_dp[1], row_dc[1]], axis=1),
                                                  kstack).astype(bf16)
                    p_c.append(jnp.concatenate(row_pc, axis=1))
                    p_p.append(jnp.concatenate(row_pp, axis=1))
                    ds_c.append(jnp.concatenate(row_dc, axis=1))
                    ds_p.append(jnp.concatenate(row_dp, axis=1))

                def to_keys(m2, rhs):
                    x2 = _dot_tn(jnp.concatenate(m2, axis=0), rhs)
                    x = jnp.where(lo, x2[0:128], x2[128:256])
                    return x + pltpu.roll(x, HD, 1)

                dups["kc"].append(to_keys(ds_c, qs))
                dups["kp"].append(to_keys(ds_p, qs))
                dups["vc"].append(to_keys(p_c, dos))
                dups["vp"].append(to_keys(p_p, dos))
            for key, ref in (("kc", dkc_ref), ("kp", dkp_ref), ("vc", dvc_ref), ("vp", dvp_ref)):
                d = dups[key]
                ref[:, 0:128] = jnp.where(lo, d[0], d[1]).astype(bf16)
                ref[:, 128:256] = jnp.where(lo, d[2], d[3]).astype(bf16)

    clamp = lambda n: jnp.minimum(n, nb - 1)
    blk = lambda f: pl.BlockSpec((BLK, D), f)
    cur = lambda n: (clamp(n), 0)
    prev = lambda n: (jnp.maximum(clamp(n) - 1, 0), 0)
    back = lambda n: (jnp.maximum(n - 1, 0), 0)
    kvb = lambda f: pl.BlockSpec((BLK, NKV * HD), f)
    return _call(
        body, (sinks, qn, kk, kk, vv, vv, bias, o, do, lse), grid=(nb + 1,),
        in_specs=[pl.BlockSpec(memory_space=pltpu.SMEM), blk(cur), blk(cur), blk(prev), blk(cur), blk(prev),
                  pl.BlockSpec((NQ, BLK, BLK), lambda n: (0, 0, 0)), blk(cur), blk(cur),
                  pl.BlockSpec((BLK, NQ), cur)],
        out_specs=[blk(cur), kvb(cur), kvb(back), kvb(cur), kvb(back),
                   pl.BlockSpec((NQ, 8, 128), lambda n: (0, 0, 0)),
                   pl.BlockSpec((NQ, BLK, BLK), lambda n: (0, 0, 0))],
        out_shape=[SDS((T, D), bf16)] + [SDS((T, NKV * HD), bf16)] * 4 + [SDS((NQ, 8, 128), f32), SDS((NQ, BLK, BLK), f32)],
        name="attn_bwd", sem=("arbitrary",), vmem_mb=40, comm=comm)


def _bias_bwd(dsa):
    def body(bk_ref, ds_ref, out_ref):
        bk = bk_ref[...]
        lane = lax.broadcasted_iota(jnp.int32, (1, 128), 1)
        for h in range(NQ):
            ds = ds_ref[h]
            row = jnp.zeros((1, 128), f32)
            for b in range(NBUCKET):
                row = jnp.where(lane == b, jnp.sum(jnp.where(bk == b, ds, 0.0)), row)
            out_ref[h:h + 1, :] = row

    return pl.pallas_call(body, out_shape=SDS((NQ, 128), f32), name="bias_bwd")(jnp.asarray(_bucket_tile()), dsa)


def _qkv_bwd(proj, gq2, gk2, dqn, dkc, dkp, dvc, dvp, comm=None):
    T = proj.shape[0]
    tm = 512

    def body(q_ref, kv_ref, gq_ref, gk_ref, dq_ref, dkc_ref, dkp_ref, dvc_ref, dvp_ref,
             oq_ref, okv_ref, ggq_ref, ggk_ref):
        @pl.when(pl.program_id(0) == 0)
        def _():
            ggq_ref[...] = jnp.zeros_like(ggq_ref)
            ggk_ref[...] = jnp.zeros_like(ggk_ref)

        bd = _head_blockdiag()

        def norm_bwd(z, dy, g, scale):
            r = lax.rsqrt(_head_sums(z * z, bd) * (1.0 / HD) + EPS)
            gd = dy * g * scale
            dz = r * gd - z * (r * r * r) * _head_sums(z * gd, bd) * (1.0 / HD)
            return dz, jnp.sum(dy * scale * z * r, axis=0, keepdims=True)

        gq = jnp.zeros((1, 128), f32)
        for p in range(NQ // 2):
            ln = slice(128 * p, 128 * p + 128)
            dz, dg = norm_bwd(q_ref[:, ln], dq_ref[:, ln].astype(f32), gq_ref[...], HD ** -0.5)
            oq_ref[:, ln] = dz.astype(bf16)
            gq = gq + dg
        ggq_ref[...] += gq + pltpu.roll(gq, HD, 1)
        gk = jnp.zeros((1, 128), f32)
        for p in range(NKV // 2):
            ln = slice(128 * p, 128 * p + 128)
            dz, dg = norm_bwd(kv_ref[:, ln], dkc_ref[:, ln].astype(f32) + dkp_ref[:, ln].astype(f32), gk_ref[...], 1.0)
            okv_ref[:, ln] = dz.astype(bf16)
            gk = gk + dg
        ggk_ref[...] += gk + pltpu.roll(gk, HD, 1)
        okv_ref[:, 256:512] = (dvc_ref[...].astype(f32) + dvp_ref[...].astype(f32)).astype(bf16)

    vec = pl.BlockSpec((1, 128), lambda i: (0, 0))
    kvb = pl.BlockSpec((tm, NKV * HD), lambda i: (i, 0))
    return _call(
        body, (proj, proj, gq2, gk2, dqn, dkc, dkp, dvc, dvp), grid=(T // tm,),
        in_specs=[pl.BlockSpec((tm, D), lambda i: (i, C_Q)), pl.BlockSpec((tm, 512), lambda i: (i, C_KV)), vec, vec,
                  pl.BlockSpec((tm, D), lambda i: (i, 0)), kvb, kvb, kvb, kvb],
        out_specs=[pl.BlockSpec((tm, D), lambda i: (i, 0)), pl.BlockSpec((tm, 512), lambda i: (i, 0)), vec, vec],
        out_shape=[SDS((T, D), bf16), SDS((T, 512), bf16), SDS((1, 128), f32), SDS((1, 128), f32)],
        name="qkv_bwd", sem=("arbitrary",), vmem_mb=32, comm=comm)


def _inproj_bwd(pieces, w_in, x, dx1, g_mix, comm=None):
    T = x.shape[0]
    tm = 512
    widths = [p.shape[1] for p in pieces]
    offs = [sum(widths[:i]) for i in range(len(widths))]
    assert sum(widths) == INW

    def body(*refs):
        p_refs, (w_ref, x_ref, dx1_ref, g_ref, dx_ref, dg_ref) = refs[:len(pieces)], refs[len(pieces):]

        @pl.when(pl.program_id(0) == 0)
        def _():
            dg_ref[...] = jnp.zeros_like(dg_ref)

        du = None
        for p_ref, off, wd in zip(p_refs, offs, widths):
            part = _dot_nt(p_ref[...], w_ref[:, _weight_cols(off, wd)])
            du = part if du is None else du + part
        dx, dg = _rms_bwd(x_ref[...], g_ref[...], du, dx1_ref[...])
        dx_ref[...] = dx
        dg_ref[...] += dg

    row = pl.BlockSpec((tm, D), lambda i: (i, 0))
    vec = pl.BlockSpec((1, D), lambda i: (0, 0))
    return _call(
        body, (*pieces, w_in, x, dx1, g_mix), grid=(T // tm,),
        in_specs=[pl.BlockSpec((tm, wd), lambda i: (i, 0)) for wd in widths]
        + [pl.BlockSpec((D, INW), lambda i: (0, 0), pipeline_mode=pl.Buffered(1)), row, row, vec],
        out_specs=[row, vec],
        out_shape=[SDS((T, D), f32), SDS((1, D), f32)],
        name="inproj_bwd", sem=("arbitrary",), vmem_mb=48, comm=comm)


def _forward_backward(x, tgt, w, placed, chip_core):
    def sums(names, grads, got):
        res = [_pair_sum(nm, grads[nm], got_nm, chip_core) for nm, got_nm in zip(names, got)]
        return {nm: r[0] for nm, r in zip(names, res)}, {nm: r[1] for nm, r in zip(names, res)}

    first = ["w_in", "w_dw"]
    w_in, w_dw = _run_comm(_gather_comm({nm: placed[nm] for nm in first}), "gather_first")
    gq2 = jnp.tile(w["q_norm_g"], (1, 2))
    gk2 = jnp.tile(w["k_norm_g"], (1, 2))
    def gathered_in(names):
        return names, _gather_comm({nm: placed[nm] for nm in names})

    full = {}
    perm, perm_t = _block_perm()
    names, comm = gathered_in(["w_out", "w_attn_o", "w_conv_out"])
    (proj, u, tail, qn, kk, vv), got = _rms_inproj(x, w["norm_mix_g"], w_in, perm, gq2, gk2, comm=comm)
    full.update(zip(names, got))
    bias = _bias_tiles(w["rel_bias"])
    names, comm = gathered_in(["w_ff1"])
    (o, lse), got = _attn_fwd(qn, kk, vv, bias, w["attn_sinks"], comm=comm)
    full.update(zip(names, got))
    names, comm = gathered_in(["w_ff2"])
    h1, h3, got = _glu_conv_fwd(proj, tail, w_dw, w["b_dw"], w["conv_ln_g"], w["conv_ln_b"], perm_t, comm=comm)
    full.update(zip(names, got))
    attn, conv, merged, x1, n2 = _mix_out(o, h3, proj, x, full["w_attn_o"], full["w_conv_out"], full["w_out"],
                                          w["norm_mlp_g"])
    hmid, slope, dy, dyb, loss = _mlp_fwd(n2, full["w_ff1"], full["w_ff2"], x1, tgt)

    g = {}
    df1, dx1, dx1b, g["norm_mlp_g"] = _mlp_bwd(dy, dyb, slope, full["w_ff1"], full["w_ff2"], x1, w["norm_mlp_g"])
    ff = ["w_ff1", "w_ff2"]
    gff = {"w_ff2": _wgrad(hmid, dyb, "wgrad_ff2"), "w_ff1": _wgrad(n2, df1, "wgrad_ff1")}
    (dat, dcv, do, dh1, dga, dgc, lnacc, head), got = _mix_bwd(
        dx1b, proj, attn, conv, h1, full["w_attn_o"], full["w_conv_out"], full["w_out"], w["conv_ln_g"],
        w["conv_ln_b"], perm, comm=_pair_exchange_comm(gff, ff))
    g["conv_ln_g"], g["conv_ln_b"], g["b_dw"] = lnacc[0:1], lnacc[1:2], lnacc[2:3]
    cp_ff, own_ff = sums(ff, gff, got)
    sq = ["w_out", "w_attn_o", "w_conv_out"]
    gsq = {"w_out": _wgrad(merged, dx1b, "wgrad_out"), "w_attn_o": _wgrad(o, dat, "wgrad_attn_o"),
           "w_conv_out": _wgrad(h3, dcv, "wgrad_conv_out")}
    (da, dg, g["w_dw"]), got = _conv_bwd(dh1, head, proj, tail, w_dw, perm_t, comm=_merge_comms(
        _pair_exchange_comm(gsq, sq), _chip_exchange_comm(cp_ff, ff)))
    cp_sq, own_sq = sums(sq, gsq, got[:len(sq)])
    tot_ff = {nm: _chip_sum(nm, own_ff[nm], rc_nm, chip_core) for nm, rc_nm in zip(ff, got[len(sq):])}
    (dqn, dkc, dkp, dvc, dvp, dsk, dsa), got = _attn_bwd(qn, kk, vv, bias, w["attn_sinks"], o, do, lse, comm=_merge_comms(
        _chip_exchange_comm(cp_sq, sq), _pair_share_comm(tot_ff, ff)))
    tot_sq = {nm: _chip_sum(nm, own_sq[nm], rc_nm, chip_core) for nm, rc_nm in zip(sq, got[:len(sq)])}
    shards = dict(zip(ff, got[len(sq):]))
    g["attn_sinks"] = dsk[:, 0, 0].reshape(1, NQ)
    g["rel_bias"] = _bias_bwd(dsa)[:, 0:NBUCKET].T
    (dq, dkv, ggq, ggk), got = _qkv_bwd(proj, gq2, gk2, dqn, dkc, dkp, dvc, dvp, comm=_pair_share_comm(tot_sq, sq))
    shards.update(zip(sq, got))
    g["q_norm_g"], g["k_norm_g"] = ggq[:, 0:HD], ggk[:, 0:HD]
    pieces = [dq, da, dg, dga, dgc, dkv]
    names = ["q", "a", "g", "ga", "gc", "kv"]
    gw = {nm: _wgrad(u, p, "wgrad_in_" + nm, tn=p.shape[1] if p.shape[1] < 1024 else 1024) for nm, p in zip(names, pieces)}
    gin = {"w_in": jnp.concatenate([gw["q"], gw["kv"], gw["a"], gw["g"], gw["ga"], gw["gc"]], axis=1)}
    got = _run_comm(_pair_exchange_comm(gin, ["w_in"]), "rs_pair_exchange_in")
    cp_in, own_in = sums(["w_in"], gin, got)
    (grad_x, g["norm_mix_g"]), rc = _inproj_bwd(pieces, w_in, x, dx1, w["norm_mix_g"],
                                                comm=_chip_exchange_comm(cp_in, ["w_in"]))
    tot = {"w_in": _chip_sum("w_in", own_in["w_in"], rc[0], chip_core)}
    shards["w_in"] = _run_comm(_pair_share_comm(tot, ["w_in"]), "rs_pair_share_in")[0]
    return loss[0, 0], grad_x, g, shards


BIG = ["w_in", "w_attn_o", "w_conv_out", "w_out", "w_ff1", "w_ff2"]
SHARD_AXIS = {"w_in": 1, "w_attn_o": 0, "w_conv_out": 0, "w_out": 0, "w_ff1": 1, "w_ff2": 0, "w_dw": 1}
SHARD_SHAPE = {"w_in": (D, INW // 4), "w_attn_o": (D // 4, D), "w_conv_out": (D // 4, D), "w_out": (D // 4, D),
               "w_ff1": (D, DFF // 4), "w_ff2": (DFF // 4, D), "w_dw": (HALO, D // 4)}


def _position():
    x, y, c = lax.axis_index("x"), lax.axis_index("y"), lax.axis_index("c")
    other_chips = [(1 - x, y), (x, 1 - y), (1 - x, 1 - y)]
    return x, y, c, 2 * x + y, other_chips


def _shard_window(name, full_ref, s, half=None):
    R, C = SHARD_SHAPE[name]
    r0, nr = (0, R) if half is None else (half * (R // 2), R // 2)
    if SHARD_AXIS[name] == 1:
        return full_ref.at[pl.ds(r0, nr), pl.ds(s * C, C)]
    return full_ref.at[pl.ds(s * R + r0, nr), :]


def _remote(src, dst, send_sems, recv_sems, k, device):
    return pltpu.make_async_remote_copy(src_ref=src, dst_ref=dst, send_sem=send_sems.at[k], recv_sem=recv_sems.at[k],
                                        device_id=device, device_id_type=MESH)


def _full_shape(nm):
    R, C = SHARD_SHAPE[nm]
    return (R, 4 * C) if SHARD_AXIS[nm] == 1 else (4 * R, C)


def _place_shard(nm, shard, chip_arr, dtype):
    R, C = SHARD_SHAPE[nm]
    tr = min(R, 256)
    if SHARD_AXIS[nm] == 1:
        o_map = lambda i, ch: (i, ch[0])
    else:
        o_map = lambda i, ch: (ch[0] * (R // tr) + i, 0)

    def body(ch_ref, s_ref, o_ref):
        o_ref[...] = s_ref[...].astype(dtype)

    return pl.pallas_call(
        body,
        grid_spec=pltpu.PrefetchScalarGridSpec(
            num_scalar_prefetch=1, grid=(R // tr,),
            in_specs=[pl.BlockSpec((tr, C), lambda i, ch: (i, 0))], out_specs=pl.BlockSpec((tr, C), o_map)),
        out_shape=SDS(_full_shape(nm), dtype), name="place_" + nm,
        compiler_params=_cparams(("parallel",), 32))(chip_arr, shard)


def _gather_comm(placed):
    names = list(placed)
    n = len(names)

    def copies(cout, send, recv):
        x, y, c, chip, chips = _position()
        for a, nm in enumerate(names):
            for j, (cx, cy) in enumerate(chips):
                def ici(s, a=a, nm=nm, j=j, cx=cx, cy=cy):
                    w = _shard_window(nm, cout[a], s, c)
                    return _remote(w, w, send, recv, 6 * a + j, (cx, cy, c))

                def d2d(h, a=a, nm=nm, j=j, cx=cx, cy=cy):
                    w = _shard_window(nm, cout[a], 2 * cx + cy, h)
                    return _remote(w, w, send, recv, 6 * a + 3 + j, (x, y, 1 - c))

                yield ici, d2d, chip, 2 * cx + cy, c

    def start(cin, cout, send, recv):
        for ici, d2d, chip, s, c in copies(cout, send, recv):
            ici(chip).start()

    def mid(cin, cout, send, recv):
        for ici, d2d, chip, s, c in copies(cout, send, recv):
            ici(s).wait_recv()
            d2d(c).start()

    def finish(cin, cout, send, recv):
        for ici, d2d, chip, s, c in copies(cout, send, recv):
            d2d(1 - c).wait_recv()
        for ici, d2d, chip, s, c in copies(cout, send, recv):
            ici(chip).wait_send()
            d2d(c).wait_send()

    return _Comm([placed[nm] for nm in names], [SDS(placed[nm].shape, placed[nm].dtype) for nm in names], 6 * n,
                 start, finish, mid, aliases={a: a for a in range(n)})


def _half_rows(nm):
    return SHARD_SHAPE[nm][0] // 2


RS_TILE = 128


def _exchange_comm(ins, out_shapes, copies, n_sems, aliases=None):
    def start(cin, cout, send, recv):
        for cp in copies(cin, cout, send, recv):
            cp.start()

    def finish(cin, cout, send, recv):
        for cp in copies(cin, cout, send, recv):
            cp.wait()

    return _Comm(ins, out_shapes, n_sems, start, finish, aliases=aliases)


def _pair_exchange_comm(grads, names):
    def copies(cin, cout, send, recv):
        x, y, c, chip, chips = _position()
        return [_remote(_shard_window(nm, cin[a], s, 1 - c), cout[a].at[s], send, recv, 4 * a + s, (x, y, 1 - c))
                for a, nm in enumerate(names) for s in range(4)]

    return _exchange_comm([grads[nm] for nm in names],
                          [SDS((4, _half_rows(nm), SHARD_SHAPE[nm][1]), f32) for nm in names], copies, 4 * len(names))


def _pair_sum(nm, g, got, chip_core):
    R, C = SHARD_SHAPE[nm]
    hr = R // 2
    nt = hr // RS_TILE
    if SHARD_AXIS[nm] == 1:
        g_map = lambda i, s, sc: (sc[1] * nt + i, s)
    else:
        g_map = lambda i, s, sc: (s * (R // RS_TILE) + sc[1] * nt + i, 0)

    def body(sc_ref, g_ref, got_ref, o16_ref, own_ref):
        v = g_ref[...] + got_ref[0]
        o16_ref[0] = v.astype(bf16)

        @pl.when(pl.program_id(1) == sc_ref[0])
        def _():
            own_ref[...] = v

    blk3 = pl.BlockSpec((1, RS_TILE, C), lambda i, s, sc: (s, i, 0))
    return pl.pallas_call(
        body,
        grid_spec=pltpu.PrefetchScalarGridSpec(
            num_scalar_prefetch=1, grid=(nt, 4),
            in_specs=[pl.BlockSpec((RS_TILE, C), g_map), blk3],
            out_specs=[blk3, pl.BlockSpec((RS_TILE, C), lambda i, s, sc: (i, 0))]),
        out_shape=[SDS((4, hr, C), bf16), SDS((hr, C), f32)], name="rs_pair_sum_" + nm,
        compiler_params=_cparams(("parallel", "arbitrary"), 32))(chip_core, g, got)


def _chip_exchange_comm(cp, names):
    def copies(cin, cout, send, recv):
        x, y, c, chip, chips = _position()
        return [_remote(cin[a].at[2 * cx + cy], cout[a].at[j], send, recv, 3 * a + j, (cx, cy, c))
                for a, nm in enumerate(names) for j, (cx, cy) in enumerate(chips)]

    return _exchange_comm([cp[nm] for nm in names],
                          [SDS((3, _half_rows(nm), SHARD_SHAPE[nm][1]), bf16) for nm in names], copies, 3 * len(names))


def _chip_sum(nm, own, rc, chip_core):
    R, C = SHARD_SHAPE[nm]
    nt = (R // 2) // RS_TILE

    def body(sc_ref, own_ref, rc_ref, o_ref):
        o_ref[...] = own_ref[...] + rc_ref[0].astype(f32) + rc_ref[1].astype(f32) + rc_ref[2].astype(f32)

    return pl.pallas_call(
        body,
        grid_spec=pltpu.PrefetchScalarGridSpec(
            num_scalar_prefetch=1, grid=(nt,),
            in_specs=[pl.BlockSpec((RS_TILE, C), lambda i, sc: (i, 0)),
                      pl.BlockSpec((3, RS_TILE, C), lambda i, sc: (0, i, 0))],
            out_specs=pl.BlockSpec((RS_TILE, C), lambda i, sc: (sc[1] * nt + i, 0))),
        out_shape=SDS((R, C), f32), name="rs_chip_sum_" + nm,
        compiler_params=_cparams(("parallel",), 32))(chip_core, own, rc)


def _pair_share_comm(tot, names):
    def copies(cin, cout, send, recv):
        x, y, c, chip, chips = _position()
        cps = []
        for a, nm in enumerate(names):
            hr = _half_rows(nm)
            mine = cout[a].at[pl.ds(c * hr, hr), :]
            cps.append(_remote(mine, mine, send, recv, a, (x, y, 1 - c)))
        return cps

    return _exchange_comm([tot[nm] for nm in names], [SDS(SHARD_SHAPE[nm], f32) for nm in names], copies, len(names),
                          aliases={a: a for a in range(len(names))})


SMALL_ROWS = 40


def _allreduce_small(block):
    def body(x_ref, out_ref, buf, send_sems, recv_sems, local_sem):
        x, y, c, chip, chips = _position()
        me, sibling = (x, y, c), (x, y, 1 - c)

        def slot(px, py, pc):
            return buf.at[4 * px + 2 * py + pc]

        def copy(k, block_of, to, src=None):
            return _remote(slot(*block_of) if src is None else src, slot(*block_of), send_sems, recv_sems, k, to)

        mine = pltpu.make_async_copy(x_ref, slot(*me), local_sem)
        mine.start()
        first = [copy(0, me, sibling, src=x_ref)] + [copy(1 + j, me, (*ch, c), src=x_ref) for j, ch in enumerate(chips)]
        for cp in first:
            cp.start()
        passed = [copy(4 + j, (*ch, c), sibling) for j, ch in enumerate(chips)]
        for j, ch in enumerate(chips):
            copy(1 + j, (*ch, c), me).wait_recv()
            passed[j].start()
        copy(0, sibling, me).wait_recv()
        for j, ch in enumerate(chips):
            copy(4 + j, (*ch, 1 - c), me).wait_recv()
        for cp in first + passed:
            cp.wait_send()
        mine.wait()
        acc = buf[0]
        for d in range(1, 8):
            acc = acc + buf[d]
        out_ref[...] = acc

    vm = pl.BlockSpec(memory_space=pltpu.VMEM)
    return pl.pallas_call(
        body, in_specs=[vm], out_specs=vm, out_shape=SDS((SMALL_ROWS, D), f32),
        scratch_shapes=[pltpu.VMEM((8, SMALL_ROWS, D), f32), pltpu.SemaphoreType.DMA((7,)), pltpu.SemaphoreType.DMA((7,)),
                        pltpu.SemaphoreType.DMA],
        name="allreduce_small")(block)


def _adamw(w, g, m, v, name):
    rows, cols = w.shape
    tr = 256 if rows % 256 == 0 else rows

    def body(w_ref, g_ref, m_ref, v_ref, d_ref, nm_ref, nv_ref):
        gv = g_ref[...]
        m2 = ADAM_B1 * m_ref[...] + (1.0 - ADAM_B1) * gv
        v2 = ADAM_B2 * v_ref[...] + (1.0 - ADAM_B2) * jnp.square(gv)
        m_hat = m2 / (1.0 - ADAM_B1 ** ADAM_STEP)
        v_hat = v2 / (1.0 - ADAM_B2 ** ADAM_STEP)
        d_ref[...] = -ADAM_LR * (m_hat / (jnp.sqrt(v_hat) + ADAM_EPS) + ADAM_WD * w_ref[...])
        nm_ref[...] = m2
        nv_ref[...] = v2

    spec = pl.BlockSpec((tr, cols), lambda i: (i, 0))
    return pl.pallas_call(body, grid=(rows // tr,), in_specs=[spec] * 4, out_specs=[spec] * 3,
                          out_shape=[SDS((rows, cols), f32)] * 3, name=name,
                          compiler_params=_cparams(("parallel",), 40))(w, g, m, v)


WEIGHTS = ["norm_mix_g", "w_in", "q_norm_g", "k_norm_g", "attn_sinks", "rel_bias", "w_attn_o", "w_dw", "b_dw",
           "conv_ln_g", "conv_ln_b", "w_conv_out", "w_out", "norm_mlp_g", "w_ff1", "w_ff2"]
ROW_VECS = ["norm_mix_g", "b_dw", "conv_ln_g", "conv_ln_b", "norm_mlp_g"]
MISC_ROW = 5
W_DW_ROW = 8


def _pack_small(vals, loss=None):
    misc = [vals["q_norm_g"].reshape(1, HD), vals["k_norm_g"].reshape(1, HD), vals["attn_sinks"].reshape(1, NQ),
            jnp.zeros((1, 1), f32) if loss is None else loss.reshape(1, 1), jnp.zeros((1, 111), f32),
            vals["rel_bias"].reshape(1, NBUCKET * NQ), jnp.zeros((1, 256), f32)]
    rows = [vals[nm].reshape(1, D) for nm in ROW_VECS] + [jnp.concatenate(misc, axis=1), jnp.zeros((2, D), f32)]
    return jnp.concatenate(rows, axis=0)


def _unpack_small(block):
    out = {nm: block[i:i + 1] for i, nm in enumerate(ROW_VECS)}
    misc = block[MISC_ROW]
    out["q_norm_g"] = misc[0:64].reshape(1, HD)
    out["k_norm_g"] = misc[64:128].reshape(1, HD)
    out["attn_sinks"] = misc[128:144].reshape(1, NQ)
    out["rel_bias"] = misc[256:768].reshape(NBUCKET, NQ)
    return out, misc[144]


def kernel(x, norm_mix_g, w_in, q_norm_g, k_norm_g, attn_sinks, rel_bias, w_attn_o, w_dw, b_dw, conv_ln_g, conv_ln_b, w_conv_out, w_out, norm_mlp_g, w_ff1, w_ff2, loss_target, m_norm_mix_g, m_w_in, m_q_norm_g, m_k_norm_g, m_attn_sinks, m_rel_bias, m_w_attn_o, m_w_dw, m_b_dw, m_conv_ln_g, m_conv_ln_b, m_w_conv_out, m_w_out, m_norm_mlp_g, m_w_ff1, m_w_ff2, v_norm_mix_g, v_w_in, v_q_norm_g, v_k_norm_g, v_attn_sinks, v_rel_bias, v_w_attn_o, v_w_dw, v_b_dw, v_conv_ln_g, v_conv_ln_b, v_w_conv_out, v_w_out, v_norm_mlp_g, v_w_ff1, v_w_ff2):
    args = dict(locals())
    wts = {nm: args[nm] for nm in WEIGHTS}
    mom = {nm: args["m_" + nm] for nm in WEIGHTS}
    var = {nm: args["v_" + nm] for nm in WEIGHTS}
    chip = 2 * lax.axis_index("x") + lax.axis_index("y")

    chip_arr = jnp.reshape(chip, (1,)).astype(jnp.int32)
    chip_core = jnp.stack([chip, lax.axis_index("c")]).astype(jnp.int32)
    placed = {nm: _place_shard(nm, wts[nm][0], chip_arr, bf16) for nm in BIG}
    placed["w_dw"] = _place_shard("w_dw", jnp.pad(w_dw[0], ((0, 1), (0, 0))), chip_arr, f32)

    loss_part, grad_x, g, shards = _forward_backward(x[0], loss_target[0], wts, placed, chip_core)

    small = jnp.concatenate([_pack_small(g, loss_part), g["w_dw"]], axis=0)
    small = _allreduce_small(small)
    grads, loss = _unpack_small(small)
    grads["w_dw"] = lax.dynamic_slice(small[W_DW_ROW:W_DW_ROW + CW], (0, chip * (D // 4)), (CW, D // 4))
    grads.update(shards)

    delta, new_m, new_v = {}, {}, {}
    sd, sm, sv = _adamw(_pack_small(wts), small[0:8], _pack_small(mom), _pack_small(var), "adamw_small")
    for res, blk in ((delta, sd), (new_m, sm), (new_v, sv)):
        res.update(_unpack_small(blk)[0])
    for nm in BIG + ["w_dw"]:
        shp = wts[nm].shape
        two_d = lambda a: a.reshape(shp[-2], shp[-1])
        delta[nm], new_m[nm], new_v[nm] = _adamw(two_d(wts[nm]), grads[nm], two_d(mom[nm]), two_d(var[nm]), "adamw_" + nm)

    def shaped(vals):
        return [vals[nm].reshape(wts[nm].shape) for nm in WEIGHTS]

    return (loss, grad_x[None], *shaped(grads), *shaped(delta), *shaped(new_m), *shaped(new_v))
```

```python
import numpy as np
import jax
import jax.numpy as jnp
from jax import lax
from jax.experimental import pallas as pl
from jax.experimental.pallas import tpu as pltpu

f32 = jnp.float32
bf16 = jnp.bfloat16
SDS = jax.ShapeDtypeStruct
MESH = pl.DeviceIdType.MESH

D = 1024
HD = 64
NQ = 16
NKV = 4
BLK = 128
CW = 31
HALO = 32
DFF = 4096
NBUCKET = 32
EPS = 1e-6
NEG = -1e30
INW = 5632
MIX_CHUNK = 256
C_Q, C_A, C_G, C_GA, C_GC = 0, 1, 2, 3, 4
C_KV = 10

ADAM_LR = 0.001
ADAM_B1 = 0.9
ADAM_B2 = 0.999
ADAM_EPS = 1e-08
ADAM_WD = 0.01
ADAM_STEP = 10

VMEM_BYTES_V7X = 64 << 20


def _cparams(sem, vmem_mb):
    assert (vmem_mb << 20) < VMEM_BYTES_V7X
    return pltpu.CompilerParams(dimension_semantics=sem, vmem_limit_bytes=vmem_mb << 20)


ANY = pl.BlockSpec(memory_space=pl.ANY)


class _Comm:
    def __init__(self, ins, out_shapes, n_sems, start, finish, mid=None, aliases=None):
        self.ins, self.out_shapes, self.n_sems = list(ins), list(out_shapes), n_sems
        self.start, self.finish, self.mid, self.aliases = start, finish, mid, dict(aliases or {})


class _SemOffset:
    def __init__(self, sems, base):
        self._sems, self._base = sems, base
        self.at = self

    def __getitem__(self, k):
        return self._sems.at[self._base + k]


def _merge_comms(a, b):
    assert a.mid is None and b.mid is None
    n_in, n_out = len(a.ins), len(a.out_shapes)

    def both(fa, fb):
        def run(cin, cout, send, recv):
            fa(cin[:n_in], cout[:n_out], send, recv)
            fb(cin[n_in:], cout[n_out:], _SemOffset(send, a.n_sems), _SemOffset(recv, a.n_sems))
        return run

    aliases = {**a.aliases, **{n_in + k: n_out + v for k, v in b.aliases.items()}}
    return _Comm(a.ins + b.ins, a.out_shapes + b.out_shapes, a.n_sems + b.n_sems, both(a.start, b.start),
                 both(a.finish, b.finish), aliases=aliases)


def _call(body, args, *, grid, in_specs, out_specs, out_shape, name, sem, vmem_mb, scratch_shapes=(), comm=None,
          mid_step=None):
    n_in, n_out, n_scr = len(in_specs), len(out_specs), len(scratch_shapes)
    if comm is None:
        outs = pl.pallas_call(body, grid=grid, in_specs=list(in_specs), out_specs=list(out_specs),
                              out_shape=list(out_shape), scratch_shapes=list(scratch_shapes), name=name,
                              compiler_params=_cparams(sem, vmem_mb))(*args)
        return list(outs), []
    ci, co = len(comm.ins), len(comm.out_shapes)
    last = grid[0] - 1

    def wrapped(*refs):
        ins, cin = refs[:n_in], refs[n_in:n_in + ci]
        outs = refs[n_in + ci:n_in + ci + n_out]
        cout = refs[n_in + ci + n_out:n_in + ci + n_out + co]
        scr = refs[n_in + ci + n_out + co:]
        send, recv = scr[n_scr], scr[n_scr + 1]
        step = pl.program_id(0)

        @pl.when(step == 0)
        def _():
            comm.start(cin, cout, send, recv)

        body(*ins, *outs, *scr[:n_scr])
        if comm.mid is not None:
            @pl.when(step == mid_step)
            def _():
                comm.mid(cin, cout, send, recv)

        @pl.when(step == last)
        def _():
            comm.finish(cin, cout, send, recv)

    res = pl.pallas_call(
        wrapped, grid=grid, in_specs=list(in_specs) + [ANY] * ci, out_specs=list(out_specs) + [ANY] * co,
        out_shape=list(out_shape) + comm.out_shapes,
        input_output_aliases={n_in + k: n_out + v for k, v in comm.aliases.items()},
        scratch_shapes=list(scratch_shapes) + [pltpu.SemaphoreType.DMA((comm.n_sems,))] * 2,
        name=name, compiler_params=_cparams(("arbitrary",), vmem_mb))(*args, *comm.ins)
    return list(res[:n_out]), list(res[n_out:])


def _run_comm(comm, name):
    ci, co = len(comm.ins), len(comm.out_shapes)

    def body(*refs):
        cin, cout, (send, recv) = refs[:ci], refs[ci:ci + co], refs[ci + co:]
        comm.start(cin, cout, send, recv)
        if comm.mid is not None:
            comm.mid(cin, cout, send, recv)
        comm.finish(cin, cout, send, recv)

    return pl.pallas_call(
        body, in_specs=[ANY] * ci, out_specs=[ANY] * co, out_shape=comm.out_shapes, input_output_aliases=comm.aliases,
        scratch_shapes=[pltpu.SemaphoreType.DMA((comm.n_sems,))] * 2, name=name)(*comm.ins)


def _dot(a, b):
    return jnp.dot(a, b, preferred_element_type=f32)


def _dot_nt(a, b):
    return lax.dot_general(a, b, (((1,), (1,)), ((), ())), preferred_element_type=f32)


def _dot_tn(a, b):
    return lax.dot_general(a, b, (((0,), (0,)), ((), ())), preferred_element_type=f32)


def _sigmoid(x):
    return 0.5 * jnp.tanh(0.5 * x) + 0.5


def _low_head_lanes():
    return lax.broadcasted_iota(jnp.int32, (1, 2 * HD), 1) < HD


def _head_blockdiag():
    r = lax.broadcasted_iota(jnp.int32, (2 * HD, 2 * HD), 0) // HD
    c = lax.broadcasted_iota(jnp.int32, (2 * HD, 2 * HD), 1) // HD
    return jnp.where(r == c, 1.0, 0.0).astype(bf16)


def _head_sums(z, bd):
    hi = z.astype(bf16)
    lo = (z - hi.astype(f32)).astype(bf16)
    return _dot(hi, bd) + _dot(lo, bd)


def _weight_cols(start, width):
    kv_width = 2 * NKV * HD
    if start < D:
        orig = start
    elif start < INW - kv_width:
        orig = start + kv_width
    else:
        orig = start - (INW - kv_width) + D
    assert (start < D) == (start + width <= D) and (start < INW - kv_width) == (start + width <= INW - kv_width)
    return slice(orig, orig + width)


def _rms_inproj(x, g, w, perm, gq2, gk2, comm=None):
    T, N = x.shape[0], w.shape[1]
    tn = 512
    conv_cols = (C_A * D, (C_G + 1) * D)
    attn_chunks = [C_Q * D // tn, C_Q * D // tn + 1, C_KV]

    def body(x_ref, g_ref, w_ref, perm_ref, gq_ref, gk_ref, p_ref, u_ref, tail_ref, qn_ref, kk_ref, vv_ref):
        xv = x_ref[...]
        r = lax.rsqrt(jnp.mean(xv * xv, axis=-1, keepdims=True) + EPS)
        u = (xv * r * g_ref[...]).astype(bf16)
        u_ref[...] = u
        u_blocks = _dot(perm_ref[...], u).astype(bf16)

        def project(c):
            lhs = u_blocks if conv_cols[0] <= c * tn < conv_cols[1] else u
            p_ref[:, c * tn:(c + 1) * tn] = _dot(lhs, w_ref[:, _weight_cols(c * tn, tn)])

        for c in attn_chunks:
            project(c)
        bd = _head_blockdiag()
        lo = _low_head_lanes()
        for p in range(NQ // 2):
            z = p_ref[:, C_Q * D + 128 * p:C_Q * D + 128 * p + 128]
            rq = lax.rsqrt(_head_sums(z * z, bd) * (1.0 / HD) + EPS)
            qn_ref[:, 128 * p:128 * p + 128] = (z * rq * gq_ref[...] * (HD ** -0.5)).astype(bf16)
        kv0 = C_KV * tn
        for p in range(NKV // 2):
            z = p_ref[:, kv0 + 128 * p:kv0 + 128 * p + 128]
            rk = lax.rsqrt(_head_sums(z * z, bd) * (1.0 / HD) + EPS)
            _split_pair(z * rk * gk_ref[...], kk_ref, p, lo)
            _split_pair(p_ref[:, kv0 + 256 + 128 * p:kv0 + 256 + 128 * p + 128], vv_ref, p, lo)
        for c in range(N // tn):
            if c not in attn_chunks:
                project(c)
        for k in range(NBLK):
            tail_ref[k:k + 1, :] = p_ref[RPB * k + RPB - 1:RPB * (k + 1), conv_cols[0]:conv_cols[1]]

    once = pl.Buffered(1)
    row = pl.BlockSpec((TT, D), lambda i: (i, 0))
    vec = pl.BlockSpec((1, 128), lambda i: (0, 0))
    return _call(
        body, (x, g, w, perm, gq2, gk2), grid=(T // TT,),
        in_specs=[row, pl.BlockSpec((1, D), lambda i: (0, 0)),
                  pl.BlockSpec((D, N), lambda i: (0, 0), pipeline_mode=once),
                  pl.BlockSpec((TT, TT), lambda i: (0, 0), pipeline_mode=once), vec, vec],
        out_specs=[pl.BlockSpec((TT, N), lambda i: (i, 0)), row, pl.BlockSpec((NBLK, 2 * D), lambda i: (i, 0)),
                   row, row, row],
        out_shape=[SDS((T, N), f32), SDS((T, D), bf16), SDS((T // TT * NBLK, 2 * D), f32)] + [SDS((T, D), bf16)] * 3,
        name="rms_inproj", sem=("parallel",), vmem_mb=56, comm=comm, mid_step=(3 * (T // TT)) // 4)


def _split_pair(pair, out_ref, p, lo):
    rolled = pltpu.roll(pair, HD, 1)
    zero = jnp.zeros_like(pair)
    c = 512 * p
    out_ref[:, c:c + 128] = jnp.where(lo, pair, zero).astype(bf16)
    out_ref[:, c + 128:c + 256] = jnp.where(lo, zero, rolled).astype(bf16)
    out_ref[:, c + 256:c + 384] = jnp.where(lo, rolled, zero).astype(bf16)
    out_ref[:, c + 384:c + 512] = jnp.where(lo, zero, pair).astype(bf16)


def _bucket_tile():
    qi = np.arange(BLK)[:, None]
    kj = np.arange(BLK)[None, :]
    n = np.where(kj > qi, qi + BLK - kj, qi - kj)
    max_exact = NBUCKET // 2
    nf = np.maximum(n, 1).astype(np.float32)
    large = max_exact + (np.log(nf / max_exact) / np.float32(np.log(128 / max_exact))
                         * (NBUCKET - max_exact)).astype(np.int32)
    large = np.minimum(large, NBUCKET - 1)
    return np.where(n < max_exact, n, large).astype(np.int32)


def _from_prev_block():
    return lax.broadcasted_iota(jnp.int32, (BLK, BLK), 1) > lax.broadcasted_iota(jnp.int32, (BLK, BLK), 0)


def _bias_tiles(rel_bias):
    def body(rb_ref, bk_ref, out_ref):
        bk = bk_ref[...]
        for h in range(NQ):
            acc = jnp.zeros((BLK, BLK), f32)
            for b in range(NBUCKET):
                acc = jnp.where(bk == b, rb_ref[b, h], acc)
            out_ref[h] = acc

    return pl.pallas_call(
        body,
        in_specs=[pl.BlockSpec(memory_space=pltpu.SMEM), pl.BlockSpec(memory_space=pltpu.VMEM)],
        out_specs=pl.BlockSpec(memory_space=pltpu.VMEM),
        out_shape=SDS((NQ, BLK, BLK), f32),
        name="bias_tiles")(rel_bias, jnp.asarray(_bucket_tile()))


def _rows2(ref, c):
    return jnp.concatenate([ref[:, c:c + 128], ref[:, c + 128:c + 256]], axis=0)


def _attn_fwd(qn, kk, vv, bias, sinks, comm=None):
    T = qn.shape[0]
    nb = T // BLK

    def body(s_ref, q_ref, kc_ref, kp_ref, vc_ref, vp_ref, b_ref, o_ref, lse_ref):
        prev = _from_prev_block()
        no_key = jnp.logical_and(prev, pl.program_id(0) == 0)
        scores = []
        for h in range(NKV):
            qs = _rows2(q_ref, 256 * h)
            scores.append((_dot_nt(qs, _rows2(kc_ref, 256 * h)), _dot_nt(qs, _rows2(kp_ref, 256 * h))))
        for h in range(NKV):
            c = 256 * h
            sc, sp = scores[h]
            vstack = jnp.concatenate([vp_ref[:, c:c + 128], vc_ref[:, c:c + 128],
                                      vp_ref[:, c + 128:c + 256], vc_ref[:, c + 128:c + 256]], axis=0)
            for pr in range(2):
                ps = []
                for e in range(2):
                    hq = 4 * h + 2 * pr + e
                    rows, cols = slice(128 * pr, 128 * pr + 128), slice(128 * e, 128 * e + 128)
                    s = jnp.where(no_key, NEG, jnp.where(prev, sp[rows, cols], sc[rows, cols]) + b_ref[hq])
                    sink = s_ref[0, hq]
                    m = jnp.maximum(jnp.max(s, axis=-1, keepdims=True), sink)
                    ex = jnp.exp(s - m)
                    l = jnp.sum(ex, axis=-1, keepdims=True) + jnp.exp(sink - m)
                    p = ex * (1.0 / l)
                    ps += [jnp.where(prev, p, 0.0).astype(bf16), jnp.where(prev, 0.0, p).astype(bf16)]
                    lse_ref[:, hq:hq + 1] = m + jnp.log(l)
                o_ref[:, c + 128 * pr:c + 128 * pr + 128] = _dot(jnp.concatenate(ps, axis=1), vstack).astype(bf16)

    blk = lambda f: pl.BlockSpec((BLK, D), f)
    cur = lambda n: (n, 0)
    prev = lambda n: (jnp.maximum(n - 1, 0), 0)
    return _call(
        body, (sinks, qn, kk, kk, vv, vv, bias), grid=(nb,),
        in_specs=[pl.BlockSpec(memory_space=pltpu.SMEM), blk(cur), blk(cur), blk(prev), blk(cur), blk(prev),
                  pl.BlockSpec((NQ, BLK, BLK), lambda n: (0, 0, 0))],
        out_specs=[blk(cur), pl.BlockSpec((BLK, NQ), cur)],
        out_shape=[SDS((T, D), bf16), SDS((T, NQ), f32)],
        name="attn_fwd", sem=("parallel",), vmem_mb=32, comm=comm, mid_step=(3 * nb) // 4)


TT = 512
NBLK = 32
RPB = TT // NBLK
CONV_LANES = 128
KBLK = 4
GBLK = 8
TAPG = 8


def _block_perm():
    p = np.arange(TT)
    m = np.zeros((TT, TT), np.float32)
    m[p, NBLK * (p % RPB) + p // RPB] = 1.0
    return jnp.asarray(m, bf16), jnp.asarray(m.T, bf16)


def _lane_groups():
    return [slice(q * CONV_LANES, (q + 1) * CONV_LANES) for q in range(D // CONV_LANES)]


def _fill_time_blocks(z, tile, edge, causal):
    row = lax.broadcasted_iota(jnp.int32, (RPB, 1), 0)
    for k in range(NBLK):
        blk = tile[RPB * k:RPB * (k + 1)]
        if causal:
            z[NBLK + k] = blk
            z[k] = jnp.where(row == 0, edge[k:k + 1], pltpu.roll(blk, 1, 0))
        else:
            z[k] = blk
            z[NBLK + k] = jnp.where(row == RPB - 1, edge[k:k + 1], pltpu.roll(blk, RPB - 1, 0))


def _block_conv(z, w_ref, tap_offset, init, store):
    def step(s, carry):
        k0 = s * KBLK
        for ln in _lane_groups():
            accs = [init(ln) for _ in range(KBLK)]
            for g0 in range(0, CW, TAPG):
                taps = range(g0, min(g0 + TAPG, CW))
                lo = min(tap_offset(j) for j in taps)
                hi = max(tap_offset(j) for j in taps)
                win = [z[k0 + lo + d, :, ln] for d in range(KBLK + hi - lo)]
                for j in taps:
                    wv = w_ref[j:j + 1, ln]
                    for q in range(KBLK):
                        accs[q] = accs[q] + win[q + tap_offset(j) - lo] * wv
            for q in range(KBLK):
                store(k0 + q, ln, accs[q])
        return carry

    lax.fori_loop(0, NBLK // KBLK, step, 0)


def _block_rows(k):
    return pl.ds(pl.multiple_of(k * RPB, RPB), RPB)


def _glu_conv_fwd(proj, tail, w_dw, b_dw, ln_g, ln_b, perm_t, comm=None):
    T = proj.shape[0]

    def body(a_ref, g_ref, ta_ref, tg_ref, w_ref, b_ref, lg_ref, lb_ref, pt_ref, h1_ref, h3_ref, z):
        edge = jnp.where(pl.program_id(0) > 0, ta_ref[...] * _sigmoid(tg_ref[...]), 0.0)
        _fill_time_blocks(z, a_ref[...] * _sigmoid(g_ref[...]), edge, causal=True)

        def store(k, ln, value):
            h1_ref[_block_rows(k), ln] = value

        _block_conv(z, w_ref, lambda j: NBLK - (CW - 1) + j,
                    lambda ln: jnp.broadcast_to(b_ref[:, ln], (RPB, CONV_LANES)), store)
        h1 = h1_ref[...]
        mu = jnp.mean(h1, axis=-1, keepdims=True)
        xc = h1 - mu
        var = jnp.mean(xc * xc, axis=-1, keepdims=True)
        h2 = xc * lax.rsqrt(var + EPS) * lg_ref[...] + lb_ref[...]
        h3_ref[...] = _dot(pt_ref[...], (h2 * _sigmoid(h2)).astype(bf16)).astype(bf16)

    tile = lambda cb: pl.BlockSpec((TT, D), lambda i: (i, cb))
    edge = lambda cb: pl.BlockSpec((NBLK, D), lambda i: (jnp.maximum(i - 1, 0), cb))
    vec = pl.BlockSpec((1, D), lambda i: (0, 0))
    (h1, h3), got = _call(
        body, (proj, proj, tail, tail, w_dw, b_dw, ln_g, ln_b, perm_t), grid=(T // TT,),
        in_specs=[tile(C_A), tile(C_G), edge(0), edge(1), pl.BlockSpec((HALO, D), lambda i: (0, 0)), vec, vec, vec,
                  pl.BlockSpec((TT, TT), lambda i: (0, 0), pipeline_mode=pl.Buffered(1))],
        out_specs=[pl.BlockSpec((TT, D), lambda i: (i, 0))] * 2,
        out_shape=[SDS((T, D), f32), SDS((T, D), bf16)],
        scratch_shapes=[pltpu.VMEM((2 * NBLK, RPB, D), f32)],
        name="glu_conv_fwd", sem=("parallel",), vmem_mb=40, comm=comm, mid_step=(3 * (T // TT)) // 4)
    return h1, h3, got


def _mix_out(o, h3, proj, x, w_attn_o, w_conv_out, w_out, g_mlp):
    T = x.shape[0]
    tm = 512

    def body(o_ref, h3_ref, ga_ref, gc_ref, x_ref, wa_ref, wc_ref, wo_ref, g_ref,
             attn_ref, conv_ref, mg_ref, x1_ref, n2_ref):
        x1 = x_ref[...]
        for j in range(D // MIX_CHUNK):
            cols = slice(j * MIX_CHUNK, (j + 1) * MIX_CHUNK)
            attn = _dot(o_ref[...], wa_ref[:, cols])
            conv = _dot(h3_ref[...], wc_ref[:, cols])
            attn_ref[:, cols] = attn.astype(bf16)
            conv_ref[:, cols] = conv.astype(bf16)
            mg = (_sigmoid(ga_ref[:, cols]) * attn + _sigmoid(gc_ref[:, cols]) * conv).astype(bf16)
            mg_ref[:, cols] = mg
            x1 = x1 + _dot(mg, wo_ref[cols, :])
        x1_ref[...] = x1
        r = lax.rsqrt(jnp.mean(x1 * x1, axis=-1, keepdims=True) + EPS)
        n2_ref[...] = (x1 * r * g_ref[...]).astype(bf16)

    tile = lambda cb=0: pl.BlockSpec((tm, D), lambda i: (i, cb))
    wfull = pl.BlockSpec((D, D), lambda i: (0, 0), pipeline_mode=pl.Buffered(1))
    return pl.pallas_call(
        body, grid=(T // tm,),
        in_specs=[tile(), tile(), tile(C_GA), tile(C_GC), tile(), wfull, wfull, wfull,
                  pl.BlockSpec((1, D), lambda i: (0, 0))],
        out_specs=[tile()] * 5,
        out_shape=[SDS((T, D), bf16), SDS((T, D), bf16), SDS((T, D), bf16), SDS((T, D), f32), SDS((T, D), bf16)],
        name="mix_out", compiler_params=_cparams(("parallel",), 48))(o, h3, proj, proj, x, w_attn_o, w_conv_out, w_out, g_mlp)


def _mlp_fwd(n2, w1, w2, x1, tgt):
    T = n2.shape[0]
    tm, tf = 512, 1024

    def body(n2_ref, w1_ref, w2_ref, x1_ref, t_ref, hm_ref, slope_ref, dy_ref, dyb_ref, loss_ref):
        @pl.when(pl.program_id(0) == 0)
        def _():
            loss_ref[...] = jnp.zeros_like(loss_ref)

        n2v = n2_ref[...]
        for c in range(DFF // tf):
            r = jnp.maximum(_dot(n2v, w1_ref[:, c * tf:(c + 1) * tf]), 0.0)
            hm_ref[:, c * tf:(c + 1) * tf] = (r * r).astype(bf16)
            slope_ref[:, c * tf:(c + 1) * tf] = (2.0 * r).astype(bf16)
        e = x1_ref[...] + _dot(hm_ref[...], w2_ref[...]) - t_ref[...]
        dy = e * (1.0 / D)
        dy_ref[...] = dy
        dyb_ref[...] = dy.astype(bf16)
        loss_ref[...] += 0.5 * jnp.sum(jnp.sum(e * e, axis=-1, keepdims=True) * (1.0 / D))

    row = pl.BlockSpec((tm, D), lambda i: (i, 0))
    once = pl.Buffered(1)
    return pl.pallas_call(
        body, grid=(T // tm,),
        in_specs=[row, pl.BlockSpec((D, DFF), lambda i: (0, 0), pipeline_mode=once),
                  pl.BlockSpec((DFF, D), lambda i: (0, 0), pipeline_mode=once), row, row],
        out_specs=[pl.BlockSpec((tm, DFF), lambda i: (i, 0)), pl.BlockSpec((tm, DFF), lambda i: (i, 0)), row, row,
                   pl.BlockSpec((8, 128), lambda i: (0, 0))],
        out_shape=[SDS((T, DFF), bf16), SDS((T, DFF), bf16), SDS((T, D), f32), SDS((T, D), bf16), SDS((8, 128), f32)],
        name="mlp_fwd", compiler_params=_cparams(("arbitrary",), 60))(n2, w1, w2, x1, tgt)


def _rms_bwd(xv, g, dn, dres):
    r = lax.rsqrt(jnp.mean(xv * xv, axis=-1, keepdims=True) + EPS)
    gd = dn * g
    dx = dres + r * gd - xv * (r * r * r) * jnp.mean(xv * gd, axis=-1, keepdims=True)
    dg = jnp.sum(dn * xv * r, axis=0, keepdims=True)
    return dx, dg


def _mlp_bwd(dy, dyb, slope, w1, w2, x1, g_mlp):
    T = dy.shape[0]
    tm, tf = 512, 1024

    def body(dy_ref, dyb_ref, slope_ref, w1_ref, w2_ref, x1_ref, g_ref, df_ref, dx_ref, dxb_ref, dg_ref):
        @pl.when(pl.program_id(0) == 0)
        def _():
            dg_ref[...] = jnp.zeros_like(dg_ref)

        dyb = dyb_ref[...]
        for c in range(DFF // tf):
            cols = slice(c * tf, (c + 1) * tf)
            d_hm = _dot_nt(dyb, w2_ref[cols, :])
            df_ref[:, cols] = (d_hm * slope_ref[:, cols].astype(f32)).astype(bf16)
        dn = _dot_nt(df_ref[...], w1_ref[...])
        dx, dg = _rms_bwd(x1_ref[...], g_ref[...], dn, dy_ref[...])
        dx_ref[...] = dx
        dxb_ref[...] = dx.astype(bf16)
        dg_ref[...] += dg

    row = pl.BlockSpec((tm, D), lambda i: (i, 0))
    wide = pl.BlockSpec((tm, DFF), lambda i: (i, 0))
    vec = pl.BlockSpec((1, D), lambda i: (0, 0))
    once = pl.Buffered(1)
    return pl.pallas_call(
        body, grid=(T // tm,),
        in_specs=[row, row, wide, pl.BlockSpec((D, DFF), lambda i: (0, 0), pipeline_mode=once),
                  pl.BlockSpec((DFF, D), lambda i: (0, 0), pipeline_mode=once), row, vec],
        out_specs=[wide, row, row, vec],
        out_shape=[SDS((T, DFF), bf16), SDS((T, D), f32), SDS((T, D), bf16), SDS((1, D), f32)],
        name="mlp_bwd", compiler_params=_cparams(("arbitrary",), 56))(dy, dyb, slope, w1, w2, x1, g_mlp)


def _wgrad(a, b, name, tn=1024):
    T, M = a.shape
    N = b.shape[1]
    tmm, tk = min(M, 1024), min(T, 2048)

    def body(a_ref, b_ref, o_ref):
        @pl.when(pl.program_id(2) == 0)
        def _():
            o_ref[...] = jnp.zeros_like(o_ref)

        o_ref[...] += _dot_tn(a_ref[...], b_ref[...])

    return pl.pallas_call(
        body, grid=(M // tmm, N // tn, T // tk),
        in_specs=[pl.BlockSpec((tk, tmm), lambda m, n, t: (t, m)), pl.BlockSpec((tk, tn), lambda m, n, t: (t, n))],
        out_specs=pl.BlockSpec((tmm, tn), lambda m, n, t: (m, n)),
        out_shape=SDS((M, N), f32),
        name=name, compiler_params=_cparams(("parallel", "parallel", "arbitrary"), 40))(a, b)


def _mix_bwd(dx1b, proj, attn, conv, h1, w_attn_o, w_conv_out, w_out, ln_g, ln_b, perm, comm=None):
    T = dx1b.shape[0]
    tm = TT

    def body(dx_ref, ga_ref, gc_ref, attn_ref, conv_ref, h1_ref, wa_ref, wc_ref, wo_ref, lg_ref, lb_ref, perm_ref,
             dat_ref, dcv_ref, do_ref, dh1_ref, dga_ref, dgc_ref, acc_ref, head_ref):
        @pl.when(pl.program_id(0) == 0)
        def _():
            acc_ref[...] = jnp.zeros_like(acc_ref)

        d_o, d_h3 = None, None
        for j in range(D // MIX_CHUNK):
            cols = slice(j * MIX_CHUNK, (j + 1) * MIX_CHUNK)
            dm = _dot_nt(dx_ref[...], wo_ref[cols, :])
            sa = _sigmoid(ga_ref[:, cols])
            sc = _sigmoid(gc_ref[:, cols])
            dat = (dm * sa).astype(bf16)
            dcv = (dm * sc).astype(bf16)
            dat_ref[:, cols] = dat
            dcv_ref[:, cols] = dcv
            dga_ref[:, cols] = (dm * attn_ref[:, cols].astype(f32) * sa * (1.0 - sa)).astype(bf16)
            dgc_ref[:, cols] = (dm * conv_ref[:, cols].astype(f32) * sc * (1.0 - sc)).astype(bf16)
            part_o = _dot_nt(dat, wa_ref[:, cols])
            part_h = _dot_nt(_dot(perm_ref[...], dcv).astype(bf16), wc_ref[:, cols])
            d_o = part_o if d_o is None else d_o + part_o
            d_h3 = part_h if d_h3 is None else d_h3 + part_h
        do_ref[...] = d_o.astype(bf16)
        h1 = h1_ref[...]
        mu = jnp.mean(h1, axis=-1, keepdims=True)
        xc = h1 - mu
        rstd = lax.rsqrt(jnp.mean(xc * xc, axis=-1, keepdims=True) + EPS)
        xh = xc * rstd
        h2 = xh * lg_ref[...] + lb_ref[...]
        sg = _sigmoid(h2)
        dh2 = d_h3 * (sg * (1.0 + h2 * (1.0 - sg)))
        dxh = dh2 * lg_ref[...]
        dh1 = rstd * (dxh - jnp.mean(dxh, axis=-1, keepdims=True) - xh * jnp.mean(dxh * xh, axis=-1, keepdims=True))
        dh1_ref[...] = dh1
        for k in range(NBLK):
            head_ref[k:k + 1, :] = dh1[RPB * k:RPB * k + 1]
        acc_ref[0:1, :] += jnp.sum(dh2 * xh, axis=0, keepdims=True)
        acc_ref[1:2, :] += jnp.sum(dh2, axis=0, keepdims=True)
        acc_ref[2:3, :] += jnp.sum(dh1, axis=0, keepdims=True)

    tile = lambda cb=0: pl.BlockSpec((tm, D), lambda i: (i, cb))
    wfull = pl.BlockSpec((D, D), lambda i: (0, 0), pipeline_mode=pl.Buffered(1))
    vec = pl.BlockSpec((1, D), lambda i: (0, 0))
    return _call(
        body, (dx1b, proj, proj, attn, conv, h1, w_attn_o, w_conv_out, w_out, ln_g, ln_b, perm), grid=(T // tm,),
        in_specs=[tile(), tile(C_GA), tile(C_GC), tile(), tile(), tile(), wfull, wfull, wfull, vec, vec,
                  pl.BlockSpec((TT, TT), lambda i: (0, 0), pipeline_mode=pl.Buffered(1))],
        out_specs=[tile()] * 6 + [pl.BlockSpec((8, D), lambda i: (0, 0)), pl.BlockSpec((NBLK, D), lambda i: (i, 0))],
        out_shape=[SDS((T, D), bf16), SDS((T, D), bf16), SDS((T, D), bf16), SDS((T, D), f32),
                   SDS((T, D), bf16), SDS((T, D), bf16), SDS((8, D), f32), SDS((T // RPB, D), f32)],
        name="mix_bwd", sem=("arbitrary",), vmem_mb=56, comm=comm)


def _conv_bwd(dh1, head, proj, tail, w_dw, perm_t, comm=None):
    T = dh1.shape[0]
    nt = T // TT

    def body(d_ref, hd_ref, a_ref, g_ref, ta_ref, tg_ref, w_ref, pt_ref, da_ref, dg_ref, gw_ref, zd, zh, dh0, gacc):
        i = pl.program_id(0)

        @pl.when(i == 0)
        def _():
            gacc[...] = jnp.zeros_like(gacc)

        a = a_ref[...]
        sg = _sigmoid(g_ref[...])
        _fill_time_blocks(zd, d_ref[...], jnp.where(i < nt - 1, hd_ref[...], 0.0), causal=False)
        _fill_time_blocks(zh, a * sg, jnp.where(i > 0, ta_ref[...] * _sigmoid(tg_ref[...]), 0.0), causal=True)

        def store(k, ln, value):
            dh0[_block_rows(k), ln] = value

        _block_conv(zd, w_ref, lambda j: (CW - 1) - j, lambda ln: jnp.zeros((RPB, CONV_LANES), f32), store)

        for ln in _lane_groups():
            for g0 in range(0, CW, TAPG):
                taps = list(range(g0, min(g0 + TAPG, CW)))

                def add_blocks(s, accs, ln=ln, taps=taps):
                    k0 = s * GBLK
                    first = k0 + NBLK - (CW - 1) + taps[0]
                    win = [zh[first + t, :, ln] for t in range(GBLK + len(taps) - 1)]
                    accs = list(accs)
                    for q in range(GBLK):
                        d = zd[k0 + q, :, ln]
                        for n, j in enumerate(taps):
                            accs[n] = accs[n] + d * win[q + j - taps[0]]
                    return tuple(accs)

                accs = lax.fori_loop(0, NBLK // GBLK, add_blocks,
                                     tuple(jnp.zeros((RPB, CONV_LANES), f32) for _ in taps))
                for j, acc in zip(taps, accs):
                    gacc[j, :, ln] += acc

        d0 = dh0[...]
        da_ref[...] = _dot(pt_ref[...], (d0 * sg).astype(bf16)).astype(bf16)
        dg_ref[...] = _dot(pt_ref[...], (d0 * a * sg * (1.0 - sg)).astype(bf16)).astype(bf16)

        @pl.when(i == nt - 1)
        def _():
            gw_ref[...] = jnp.zeros_like(gw_ref)
            for j in range(CW):
                gw_ref[j:j + 1, :] = jnp.sum(gacc[j], axis=0, keepdims=True)

    tile = lambda cb=0: pl.BlockSpec((TT, D), lambda i: (i, cb))
    prev_edge = lambda cb: pl.BlockSpec((NBLK, D), lambda i: (jnp.maximum(i - 1, 0), cb))
    next_edge = pl.BlockSpec((NBLK, D), lambda i: (jnp.minimum(i + 1, nt - 1), 0))
    wspec = pl.BlockSpec((HALO, D), lambda i: (0, 0))
    return _call(
        body, (dh1, head, proj, proj, tail, tail, w_dw, perm_t), grid=(nt,),
        in_specs=[tile(), next_edge, tile(C_A), tile(C_G), prev_edge(0), prev_edge(1), wspec,
                  pl.BlockSpec((TT, TT), lambda i: (0, 0), pipeline_mode=pl.Buffered(1))],
        out_specs=[tile(), tile(), wspec],
        out_shape=[SDS((T, D), bf16), SDS((T, D), bf16), SDS((HALO, D), f32)],
        scratch_shapes=[pltpu.VMEM((2 * NBLK, RPB, D), f32), pltpu.VMEM((2 * NBLK, RPB, D), f32),
                        pltpu.VMEM((TT, D), f32), pltpu.VMEM((HALO, RPB, D), f32)],
        name="conv_bwd", sem=("arbitrary",), vmem_mb=48, comm=comm)


def _attn_bwd(qn, kk, vv, bias, sinks, o, do, lse, comm=None):
    T = qn.shape[0]
    nb = T // BLK

    def body(s_ref, q_ref, kc_ref, kp_ref, vc_ref, vp_ref, b_ref, o_ref, do_ref, lse_ref,
             dq_ref, dkc_ref, dkp_ref, dvc_ref, dvp_ref, dsk_ref, dsa_ref):
        n = pl.program_id(0)

        @pl.when(n == 0)
        def _():
            dsk_ref[...] = jnp.zeros_like(dsk_ref)
            dsa_ref[...] = jnp.zeros_like(dsa_ref)

        @pl.when(n == nb)
        def _():
            dkp_ref[...] = jnp.zeros_like(dkp_ref)
            dvp_ref[...] = jnp.zeros_like(dvp_ref)

        @pl.when(n < nb)
        def _():
            from_prev = _from_prev_block()
            no_key = jnp.logical_and(from_prev, n == 0)
            lo = _low_head_lanes()
            dups = {"kc": [], "kp": [], "vc": [], "vp": []}
            products = []
            for h in range(NKV):
                qs = _rows2(q_ref, 256 * h)
                dos = _rows2(do_ref, 256 * h)
                products.append((qs, dos, _dot_nt(qs, _rows2(kc_ref, 256 * h)), _dot_nt(qs, _rows2(kp_ref, 256 * h)),
                                 _dot_nt(dos, _rows2(vc_ref, 256 * h)), _dot_nt(dos, _rows2(vp_ref, 256 * h))))
            for h in range(NKV):
                c = 256 * h
                qs, dos, sc, sp, dpc, dpp = products[h]
                kstack = jnp.concatenate([kp_ref[:, c:c + 128], kc_ref[:, c:c + 128],
                                          kp_ref[:, c + 128:c + 256], kc_ref[:, c + 128:c + 256]], axis=0)
                p_c, p_p, ds_c, ds_p = [], [], [], []
                for pr in range(2):
                    cc = c + 128 * pr
                    prod = do_ref[:, cc:cc + 128].astype(f32) * o_ref[:, cc:cc + 128].astype(f32)
                    d_lo = jnp.sum(jnp.where(lo, prod, 0.0), axis=-1, keepdims=True)
                    d_hi = jnp.sum(prod, axis=-1, keepdims=True) - d_lo
                    row_pc, row_pp, row_dc, row_dp = [], [], [], []
                    for e in range(2):
                        hq = 4 * h + 2 * pr + e
                        rows, cols = slice(128 * pr, 128 * pr + 128), slice(128 * e, 128 * e + 128)
                        delta = d_lo if e == 0 else d_hi
                        lse = lse_ref[:, hq:hq + 1]
                        s = jnp.where(from_prev, sp[rows, cols], sc[rows, cols]) + b_ref[hq]
                        p = jnp.where(no_key, 0.0, jnp.exp(s - lse))
                        ds = p * (jnp.where(from_prev, dpp[rows, cols], dpc[rows, cols]) - delta)
                        dsa_ref[hq] += ds
                        dsk_ref[hq] += jnp.broadcast_to(-jnp.sum(jnp.exp(s_ref[0, hq] - lse) * delta), (8, 128))
                        row_pc.append(jnp.where(from_prev, 0.0, p).astype(bf16))
                        row_pp.append(jnp.where(from_prev, p, 0.0).astype(bf16))
                        row_dc.append(jnp.where(from_prev, 0.0, ds).astype(bf16))
                        row_dp.append(jnp.where(from_prev, ds, 0.0).astype(bf16))
                    dq_ref[:, cc:cc + 128] = _dot(jnp.concatenate([row_dp[0], row_dc[0], row_dp[1], row_dc[1]], axis=1),
                                                  kstack).astype(bf16)
                    p_c.append(jnp.concatenate(row_pc, axis=1))
                    p_p.append(jnp.concatenate(row_pp, axis=1))
                    ds_c.append(jnp.concatenate(row_dc, axis=1))
                    ds_p.append(jnp.concatenate(row_dp, axis=1))

                def to_keys(m2, rhs):
                    x2 = _dot_tn(jnp.concatenate(m2, axis=0), rhs)
                    x = jnp.where(lo, x2[0:128], x2[128:256])
                    return x + pltpu.roll(x, HD, 1)

                dups["kc"].append(to_keys(ds_c, qs))
                dups["kp"].append(to_keys(ds_p, qs))
                dups["vc"].append(to_keys(p_c, dos))
                dups["vp"].append(to_keys(p_p, dos))
            for key, ref in (("kc", dkc_ref), ("kp", dkp_ref), ("vc", dvc_ref), ("vp", dvp_ref)):
                d = dups[key]
                ref[:, 0:128] = jnp.where(lo, d[0], d[1]).astype(bf16)
                ref[:, 128:256] = jnp.where(lo, d[2], d[3]).astype(bf16)

    clamp = lambda n: jnp.minimum(n, nb - 1)
    blk = lambda f: pl.BlockSpec((BLK, D), f)
    cur = lambda n: (clamp(n), 0)
    prev = lambda n: (jnp.maximum(clamp(n) - 1, 0), 0)
    back = lambda n: (jnp.maximum(n - 1, 0), 0)
    kvb = lambda f: pl.BlockSpec((BLK, NKV * HD), f)
    return _call(
        body, (sinks, qn, kk, kk, vv, vv, bias, o, do, lse), grid=(nb + 1,),
        in_specs=[pl.BlockSpec(memory_space=pltpu.SMEM), blk(cur), blk(cur), blk(prev), blk(cur), blk(prev),
                  pl.BlockSpec((NQ, BLK, BLK), lambda n: (0, 0, 0)), blk(cur), blk(cur),
                  pl.BlockSpec((BLK, NQ), cur)],
        out_specs=[blk(cur), kvb(cur), kvb(back), kvb(cur), kvb(back),
                   pl.BlockSpec((NQ, 8, 128), lambda n: (0, 0, 0)),
                   pl.BlockSpec((NQ, BLK, BLK), lambda n: (0, 0, 0))],
        out_shape=[SDS((T, D), bf16)] + [SDS((T, NKV * HD), bf16)] * 4 + [SDS((NQ, 8, 128), f32), SDS((NQ, BLK, BLK), f32)],
        name="attn_bwd", sem=("arbitrary",), vmem_mb=40, comm=comm)


def _bias_bwd(dsa):
    def body(bk_ref, ds_ref, out_ref):
        bk = bk_ref[...]
        lane = lax.broadcasted_iota(jnp.int32, (1, 128), 1)
        for h in range(NQ):
            ds = ds_ref[h]
            row = jnp.zeros((1, 128), f32)
            for b in range(NBUCKET):
                row = jnp.where(lane == b, jnp.sum(jnp.where(bk == b, ds, 0.0)), row)
            out_ref[h:h + 1, :] = row

    return pl.pallas_call(body, out_shape=SDS((NQ, 128), f32), name="bias_bwd")(jnp.asarray(_bucket_tile()), dsa)


def _qkv_bwd(proj, gq2, gk2, dqn, dkc, dkp, dvc, dvp, comm=None):
    T = proj.shape[0]
    tm = 512

    def body(q_ref, kv_ref, gq_ref, gk_ref, dq_ref, dkc_ref, dkp_ref, dvc_ref, dvp_ref,
             oq_ref, okv_ref, ggq_ref, ggk_ref):
        @pl.when(pl.program_id(0) == 0)
        def _():
            ggq_ref[...] = jnp.zeros_like(ggq_ref)
            ggk_ref[...] = jnp.zeros_like(ggk_ref)

        bd = _head_blockdiag()

        def norm_bwd(z, dy, g, scale):
            r = lax.rsqrt(_head_sums(z * z, bd) * (1.0 / HD) + EPS)
            gd = dy * g * scale
            dz = r * gd - z * (r * r * r) * _head_sums(z * gd, bd) * (1.0 / HD)
            return dz, jnp.sum(dy * scale * z * r, axis=0, keepdims=True)

        gq = jnp.zeros((1, 128), f32)
        for p in range(NQ // 2):
            ln = slice(128 * p, 128 * p + 128)
            dz, dg = norm_bwd(q_ref[:, ln], dq_ref[:, ln].astype(f32), gq_ref[...], HD ** -0.5)
            oq_ref[:, ln] = dz.astype(bf16)
            gq = gq + dg
        ggq_ref[...] += gq + pltpu.roll(gq, HD, 1)
        gk = jnp.zeros((1, 128), f32)
        for p in range(NKV // 2):
            ln = slice(128 * p, 128 * p + 128)
            dz, dg = norm_bwd(kv_ref[:, ln], dkc_ref[:, ln].astype(f32) + dkp_ref[:, ln].astype(f32), gk_ref[...], 1.0)
            okv_ref[:, ln] = dz.astype(bf16)
            gk = gk + dg
        ggk_ref[...] += gk + pltpu.roll(gk, HD, 1)
        okv_ref[:, 256:512] = (dvc_ref[...].astype(f32) + dvp_ref[...].astype(f32)).astype(bf16)

    vec = pl.BlockSpec((1, 128), lambda i: (0, 0))
    kvb = pl.BlockSpec((tm, NKV * HD), lambda i: (i, 0))
    return _call(
        body, (proj, proj, gq2, gk2, dqn, dkc, dkp, dvc, dvp), grid=(T // tm,),
        in_specs=[pl.BlockSpec((tm, D), lambda i: (i, C_Q)), pl.BlockSpec((tm, 512), lambda i: (i, C_KV)), vec, vec,
                  pl.BlockSpec((tm, D), lambda i: (i, 0)), kvb, kvb, kvb, kvb],
        out_specs=[pl.BlockSpec((tm, D), lambda i: (i, 0)), pl.BlockSpec((tm, 512), lambda i: (i, 0)), vec, vec],
        out_shape=[SDS((T, D), bf16), SDS((T, 512), bf16), SDS((1, 128), f32), SDS((1, 128), f32)],
        name="qkv_bwd", sem=("arbitrary",), vmem_mb=32, comm=comm)


def _inproj_bwd(pieces, w_in, x, dx1, g_mix, comm=None):
    T = x.shape[0]
    tm = 512
    widths = [p.shape[1] for p in pieces]
    offs = [sum(widths[:i]) for i in range(len(widths))]
    assert sum(widths) == INW

    def body(*refs):
        p_refs, (w_ref, x_ref, dx1_ref, g_ref, dx_ref, dg_ref) = refs[:len(pieces)], refs[len(pieces):]

        @pl.when(pl.program_id(0) == 0)
        def _():
            dg_ref[...] = jnp.zeros_like(dg_ref)

        du = None
        for p_ref, off, wd in zip(p_refs, offs, widths):
            part = _dot_nt(p_ref[...], w_ref[:, _weight_cols(off, wd)])
            du = part if du is None else du + part
        dx, dg = _rms_bwd(x_ref[...], g_ref[...], du, dx1_ref[...])
        dx_ref[...] = dx
        dg_ref[...] += dg

    row = pl.BlockSpec((tm, D), lambda i: (i, 0))
    vec = pl.BlockSpec((1, D), lambda i: (0, 0))
    return _call(
        body, (*pieces, w_in, x, dx1, g_mix), grid=(T // tm,),
        in_specs=[pl.BlockSpec((tm, wd), lambda i: (i, 0)) for wd in widths]
        + [pl.BlockSpec((D, INW), lambda i: (0, 0), pipeline_mode=pl.Buffered(1)), row, row, vec],
        out_specs=[row, vec],
        out_shape=[SDS((T, D), f32), SDS((1, D), f32)],
        name="inproj_bwd", sem=("arbitrary",), vmem_mb=48, comm=comm)


def _forward_backward(x, tgt, w, placed, chip_core):
    def sums(names, grads, got):
        res = [_pair_sum(nm, grads[nm], got_nm, chip_core) for nm, got_nm in zip(names, got)]
        return {nm: r[0] for nm, r in zip(names, res)}, {nm: r[1] for nm, r in zip(names, res)}

    first = ["w_in", "w_dw"]
    w_in, w_dw = _run_comm(_gather_comm({nm: placed[nm] for nm in first}), "gather_first")
    gq2 = jnp.tile(w["q_norm_g"], (1, 2))
    gk2 = jnp.tile(w["k_norm_g"], (1, 2))
    def gathered_in(names):
        return names, _gather_comm({nm: placed[nm] for nm in names})

    full = {}
    perm, perm_t = _block_perm()
    names, comm = gathered_in(["w_out", "w_attn_o", "w_conv_out"])
    (proj, u, tail, qn, kk, vv), got = _rms_inproj(x, w["norm_mix_g"], w_in, perm, gq2, gk2, comm=comm)
    full.update(zip(names, got))
    bias = _bias_tiles(w["rel_bias"])
    names, comm = gathered_in(["w_ff1"])
    (o, lse), got = _attn_fwd(qn, kk, vv, bias, w["attn_sinks"], comm=comm)
    full.update(zip(names, got))
    names, comm = gathered_in(["w_ff2"])
    h1, h3, got = _glu_conv_fwd(proj, tail, w_dw, w["b_dw"], w["conv_ln_g"], w["conv_ln_b"], perm_t, comm=comm)
    full.update(zip(names, got))
    attn, conv, merged, x1, n2 = _mix_out(o, h3, proj, x, full["w_attn_o"], full["w_conv_out"], full["w_out"],
                                          w["norm_mlp_g"])
    hmid, slope, dy, dyb, loss = _mlp_fwd(n2, full["w_ff1"], full["w_ff2"], x1, tgt)

    g = {}
    df1, dx1, dx1b, g["norm_mlp_g"] = _mlp_bwd(dy, dyb, slope, full["w_ff1"], full["w_ff2"], x1, w["norm_mlp_g"])
    ff = ["w_ff1", "w_ff2"]
    gff = {"w_ff2": _wgrad(hmid, dyb, "wgrad_ff2"), "w_ff1": _wgrad(n2, df1, "wgrad_ff1")}
    (dat, dcv, do, dh1, dga, dgc, lnacc, head), got = _mix_bwd(
        dx1b, proj, attn, conv, h1, full["w_attn_o"], full["w_conv_out"], full["w_out"], w["conv_ln_g"],
        w["conv_ln_b"], perm, comm=_pair_exchange_comm(gff, ff))
    g["conv_ln_g"], g["conv_ln_b"], g["b_dw"] = lnacc[0:1], lnacc[1:2], lnacc[2:3]
    cp_ff, own_ff = sums(ff, gff, got)
    sq = ["w_out", "w_attn_o", "w_conv_out"]
    gsq = {"w_out": _wgrad(merged, dx1b, "wgrad_out"), "w_attn_o": _wgrad(o, dat, "wgrad_attn_o"),
           "w_conv_out": _wgrad(h3, dcv, "wgrad_conv_out")}
    (da, dg, g["w_dw"]), got = _conv_bwd(dh1, head, proj, tail, w_dw, perm_t, comm=_merge_comms(
        _pair_exchange_comm(gsq, sq), _chip_exchange_comm(cp_ff, ff)))
    cp_sq, own_sq = sums(sq, gsq, got[:len(sq)])
    tot_ff = {nm: _chip_sum(nm, own_ff[nm], rc_nm, chip_core) for nm, rc_nm in zip(ff, got[len(sq):])}
    (dqn, dkc, dkp, dvc, dvp, dsk, dsa), got = _attn_bwd(qn, kk, vv, bias, w["attn_sinks"], o, do, lse, comm=_merge_comms(
        _chip_exchange_comm(cp_sq, sq), _pair_share_comm(tot_ff, ff)))
    tot_sq = {nm: _chip_sum(nm, own_sq[nm], rc_nm, chip_core) for nm, rc_nm in zip(sq, got[:len(sq)])}
    shards = dict(zip(ff, got[len(sq):]))
    g["attn_sinks"] = dsk[:, 0, 0].reshape(1, NQ)
    g["rel_bias"] = _bias_bwd(dsa)[:, 0:NBUCKET].T
    (dq, dkv, ggq, ggk), got = _qkv_bwd(proj, gq2, gk2, dqn, dkc, dkp, dvc, dvp, comm=_pair_share_comm(tot_sq, sq))
    shards.update(zip(sq, got))
    g["q_norm_g"], g["k_norm_g"] = ggq[:, 0:HD], ggk[:, 0:HD]
    pieces = [dq, da, dg, dga, dgc, dkv]
    names = ["q", "a", "g", "ga", "gc", "kv"]
    gw = {nm: _wgrad(u, p, "wgrad_in_" + nm, tn=p.shape[1] if p.shape[1] < 1024 else 1024) for nm, p in zip(names, pieces)}
    gin = {"w_in": jnp.concatenate([gw["q"], gw["kv"], gw["a"], gw["g"], gw["ga"], gw["gc"]], axis=1)}
    got = _run_comm(_pair_exchange_comm(gin, ["w_in"]), "rs_pair_exchange_in")
    cp_in, own_in = sums(["w_in"], gin, got)
    (grad_x, g["norm_mix_g"]), rc = _inproj_bwd(pieces, w_in, x, dx1, w["norm_mix_g"],
                                                comm=_chip_exchange_comm(cp_in, ["w_in"]))
    tot = {"w_in": _chip_sum("w_in", own_in["w_in"], rc[0], chip_core)}
    shards["w_in"] = _run_comm(_pair_share_comm(tot, ["w_in"]), "rs_pair_share_in")[0]
    return loss[0, 0], grad_x, g, shards


BIG = ["w_in", "w_attn_o", "w_conv_out", "w_out", "w_ff1", "w_ff2"]
SHARD_AXIS = {"w_in": 1, "w_attn_o": 0, "w_conv_out": 0, "w_out": 0, "w_ff1": 1, "w_ff2": 0, "w_dw": 1}
SHARD_SHAPE = {"w_in": (D, INW // 4), "w_attn_o": (D // 4, D), "w_conv_out": (D // 4, D), "w_out": (D // 4, D),
               "w_ff1": (D, DFF // 4), "w_ff2": (DFF // 4, D), "w_dw": (HALO, D // 4)}


def _position():
    x, y, c = lax.axis_index("x"), lax.axis_index("y"), lax.axis_index("c")
    other_chips = [(1 - x, y), (x, 1 - y), (1 - x, 1 - y)]
    return x, y, c, 2 * x + y, other_chips


def _shard_window(name, full_ref, s, half=None):
    R, C = SHARD_SHAPE[name]
    r0, nr = (0, R) if half is None else (half * (R // 2), R // 2)
    if SHARD_AXIS[name] == 1:
        return full_ref.at[pl.ds(r0, nr), pl.ds(s * C, C)]
    return full_ref.at[pl.ds(s * R + r0, nr), :]


def _remote(src, dst, send_sems, recv_sems, k, device):
    return pltpu.make_async_remote_copy(src_ref=src, dst_ref=dst, send_sem=send_sems.at[k], recv_sem=recv_sems.at[k],
                                        device_id=device, device_id_type=MESH)


def _full_shape(nm):
    R, C = SHARD_SHAPE[nm]
    return (R, 4 * C) if SHARD_AXIS[nm] == 1 else (4 * R, C)


def _place_shard(nm, shard, chip_arr, dtype):
    R, C = SHARD_SHAPE[nm]
    tr = min(R, 256)
    if SHARD_AXIS[nm] == 1:
        o_map = lambda i, ch: (i, ch[0])
    else:
        o_map = lambda i, ch: (ch[0] * (R // tr) + i, 0)

    def body(ch_ref, s_ref, o_ref):
        o_ref[...] = s_ref[...].astype(dtype)

    return pl.pallas_call(
        body,
        grid_spec=pltpu.PrefetchScalarGridSpec(
            num_scalar_prefetch=1, grid=(R // tr,),
            in_specs=[pl.BlockSpec((tr, C), lambda i, ch: (i, 0))], out_specs=pl.BlockSpec((tr, C), o_map)),
        out_shape=SDS(_full_shape(nm), dtype), name="place_" + nm,
        compiler_params=_cparams(("parallel",), 32))(chip_arr, shard)


def _gather_comm(placed):
    names = list(placed)
    n = len(names)

    def copies(cout, send, recv):
        x, y, c, chip, chips = _position()
        for a, nm in enumerate(names):
            for j, (cx, cy) in enumerate(chips):
                def ici(s, a=a, nm=nm, j=j, cx=cx, cy=cy):
                    w = _shard_window(nm, cout[a], s, c)
                    return _remote(w, w, send, recv, 6 * a + j, (cx, cy, c))

                def d2d(h, a=a, nm=nm, j=j, cx=cx, cy=cy):
                    w = _shard_window(nm, cout[a], 2 * cx + cy, h)
                    return _remote(w, w, send, recv, 6 * a + 3 + j, (x, y, 1 - c))

                yield ici, d2d, chip, 2 * cx + cy, c

    def start(cin, cout, send, recv):
        for ici, d2d, chip, s, c in copies(cout, send, recv):
            ici(chip).start()

    def mid(cin, cout, send, recv):
        for ici, d2d, chip, s, c in copies(cout, send, recv):
            ici(s).wait_recv()
            d2d(c).start()

    def finish(cin, cout, send, recv):
        for ici, d2d, chip, s, c in copies(cout, send, recv):
            d2d(1 - c).wait_recv()
        for ici, d2d, chip, s, c in copies(cout, send, recv):
            ici(chip).wait_send()
            d2d(c).wait_send()

    return _Comm([placed[nm] for nm in names], [SDS(placed[nm].shape, placed[nm].dtype) for nm in names], 6 * n,
                 start, finish, mid, aliases={a: a for a in range(n)})


def _half_rows(nm):
    return SHARD_SHAPE[nm][0] // 2


RS_TILE = 128


def _exchange_comm(ins, out_shapes, copies, n_sems, aliases=None):
    def start(cin, cout, send, recv):
        for cp in copies(cin, cout, send, recv):
            cp.start()

    def finish(cin, cout, send, recv):
        for cp in copies(cin, cout, send, recv):
            cp.wait()

    return _Comm(ins, out_shapes, n_sems, start, finish, aliases=aliases)


def _pair_exchange_comm(grads, names):
    def copies(cin, cout, send, recv):
        x, y, c, chip, chips = _position()
        return [_remote(_shard_window(nm, cin[a], s, 1 - c), cout[a].at[s], send, recv, 4 * a + s, (x, y, 1 - c))
                for a, nm in enumerate(names) for s in range(4)]

    return _exchange_comm([grads[nm] for nm in names],
                          [SDS((4, _half_rows(nm), SHARD_SHAPE[nm][1]), f32) for nm in names], copies, 4 * len(names))


def _pair_sum(nm, g, got, chip_core):
    R, C = SHARD_SHAPE[nm]
    hr = R // 2
    nt = hr // RS_TILE
    if SHARD_AXIS[nm] == 1:
        g_map = lambda i, s, sc: (sc[1] * nt + i, s)
    else:
        g_map = lambda i, s, sc: (s * (R // RS_TILE) + sc[1] * nt + i, 0)

    def body(sc_ref, g_ref, got_ref, o16_ref, own_ref):
        v = g_ref[...] + got_ref[0]
        o16_ref[0] = v.astype(bf16)

        @pl.when(pl.program_id(1) == sc_ref[0])
        def _():
            own_ref[...] = v

    blk3 = pl.BlockSpec((1, RS_TILE, C), lambda i, s, sc: (s, i, 0))
    return pl.pallas_call(
        body,
        grid_spec=pltpu.PrefetchScalarGridSpec(
            num_scalar_prefetch=1, grid=(nt, 4),
            in_specs=[pl.BlockSpec((RS_TILE, C), g_map), blk3],
            out_specs=[blk3, pl.BlockSpec((RS_TILE, C), lambda i, s, sc: (i, 0))]),
        out_shape=[SDS((4, hr, C), bf16), SDS((hr, C), f32)], name="rs_pair_sum_" + nm,
        compiler_params=_cparams(("parallel", "arbitrary"), 32))(chip_core, g, got)


def _chip_exchange_comm(cp, names):
    def copies(cin, cout, send, recv):
        x, y, c, chip, chips = _position()
        return [_remote(cin[a].at[2 * cx + cy], cout[a].at[j], send, recv, 3 * a + j, (cx, cy, c))
                for a, nm in enumerate(names) for j, (cx, cy) in enumerate(chips)]

    return _exchange_comm([cp[nm] for nm in names],
                          [SDS((3, _half_rows(nm), SHARD_SHAPE[nm][1]), bf16) for nm in names], copies, 3 * len(names))


def _chip_sum(nm, own, rc, chip_core):
    R, C = SHARD_SHAPE[nm]
    nt = (R // 2) // RS_TILE

    def body(sc_ref, own_ref, rc_ref, o_ref):
        o_ref[...] = own_ref[...] + rc_ref[0].astype(f32) + rc_ref[1].astype(f32) + rc_ref[2].astype(f32)

    return pl.pallas_call(
        body,
        grid_spec=pltpu.PrefetchScalarGridSpec(
            num_scalar_prefetch=1, grid=(nt,),
            in_specs=[pl.BlockSpec((RS_TILE, C), lambda i, sc: (i, 0)),
                      pl.BlockSpec((3, RS_TILE, C), lambda i, sc: (0, i, 0))],
            out_specs=pl.BlockSpec((RS_TILE, C), lambda i, sc: (sc[1] * nt + i, 0))),
        out_shape=SDS((R, C), f32), name="rs_chip_sum_" + nm,
        compiler_params=_cparams(("parallel",), 32))(chip_core, own, rc)


def _pair_share_comm(tot, names):
    def copies(cin, cout, send, recv):
        x, y, c, chip, chips = _position()
        cps = []
        for a, nm in enumerate(names):
            hr = _half_rows(nm)
            mine = cout[a].at[pl.ds(c * hr, hr), :]
            cps.append(_remote(mine, mine, send, recv, a, (x, y, 1 - c)))
        return cps

    return _exchange_comm([tot[nm] for nm in names], [SDS(SHARD_SHAPE[nm], f32) for nm in names], copies, len(names),
                          aliases={a: a for a in range(len(names))})


SMALL_ROWS = 40


def _allreduce_small(block):
    def body(x_ref, out_ref, buf, send_sems, recv_sems, local_sem):
        x, y, c, chip, chips = _position()
        me, sibling = (x, y, c), (x, y, 1 - c)

        def slot(px, py, pc):
            return buf.at[4 * px + 2 * py + pc]

        def copy(k, block_of, to, src=None):
            return _remote(slot(*block_of) if src is None else src, slot(*block_of), send_sems, recv_sems, k, to)

        mine = pltpu.make_async_copy(x_ref, slot(*me), local_sem)
        mine.start()
        first = [copy(0, me, sibling, src=x_ref)] + [copy(1 + j, me, (*ch, c), src=x_ref) for j, ch in enumerate(chips)]
        for cp in first:
            cp.start()
        passed = [copy(4 + j, (*ch, c), sibling) for j, ch in enumerate(chips)]
        for j, ch in enumerate(chips):
            copy(1 + j, (*ch, c), me).wait_recv()
            passed[j].start()
        copy(0, sibling, me).wait_recv()
        for j, ch in enumerate(chips):
            copy(4 + j, (*ch, 1 - c), me).wait_recv()
        for cp in first + passed:
            cp.wait_send()
        mine.wait()
        acc = buf[0]
        for d in range(1, 8):
            acc = acc + buf[d]
        out_ref[...] = acc

    vm = pl.BlockSpec(memory_space=pltpu.VMEM)
    return pl.pallas_call(
        body, in_specs=[vm], out_specs=vm, out_shape=SDS((SMALL_ROWS, D), f32),
        scratch_shapes=[pltpu.VMEM((8, SMALL_ROWS, D), f32), pltpu.SemaphoreType.DMA((7,)), pltpu.SemaphoreType.DMA((7,)),
                        pltpu.SemaphoreType.DMA],
        name="allreduce_small")(block)


def _adamw(w, g, m, v, name):
    rows, cols = w.shape
    tr = 256 if rows % 256 == 0 else rows

    def body(w_ref, g_ref, m_ref, v_ref, d_ref, nm_ref, nv_ref):
        gv = g_ref[...]
        m2 = ADAM_B1 * m_ref[...] + (1.0 - ADAM_B1) * gv
        v2 = ADAM_B2 * v_ref[...] + (1.0 - ADAM_B2) * jnp.square(gv)
        m_hat = m2 / (1.0 - ADAM_B1 ** ADAM_STEP)
        v_hat = v2 / (1.0 - ADAM_B2 ** ADAM_STEP)
        d_ref[...] = -ADAM_LR * (m_hat / (jnp.sqrt(v_hat) + ADAM_EPS) + ADAM_WD * w_ref[...])
        nm_ref[...] = m2
        nv_ref[...] = v2

    spec = pl.BlockSpec((tr, cols), lambda i: (i, 0))
    return pl.pallas_call(body, grid=(rows // tr,), in_specs=[spec] * 4, out_specs=[spec] * 3,
                          out_shape=[SDS((rows, cols), f32)] * 3, name=name,
                          compiler_params=_cparams(("parallel",), 40))(w, g, m, v)


WEIGHTS = ["norm_mix_g", "w_in", "q_norm_g", "k_norm_g", "attn_sinks", "rel_bias", "w_attn_o", "w_dw", "b_dw",
           "conv_ln_g", "conv_ln_b", "w_conv_out", "w_out", "norm_mlp_g", "w_ff1", "w_ff2"]
ROW_VECS = ["norm_mix_g", "b_dw", "conv_ln_g", "conv_ln_b", "norm_mlp_g"]
MISC_ROW = 5
W_DW_ROW = 8


def _pack_small(vals, loss=None):
    misc = [vals["q_norm_g"].reshape(1, HD), vals["k_norm_g"].reshape(1, HD), vals["attn_sinks"].reshape(1, NQ),
            jnp.zeros((1, 1), f32) if loss is None else loss.reshape(1, 1), jnp.zeros((1, 111), f32),
            vals["rel_bias"].reshape(1, NBUCKET * NQ), jnp.zeros((1, 256), f32)]
    rows = [vals[nm].reshape(1, D) for nm in ROW_VECS] + [jnp.concatenate(misc, axis=1), jnp.zeros((2, D), f32)]
    return jnp.concatenate(rows, axis=0)


def _unpack_small(block):
    out = {nm: block[i:i + 1] for i, nm in enumerate(ROW_VECS)}
    misc = block[MISC_ROW]
    out["q_norm_g"] = misc[0:64].reshape(1, HD)
    out["k_norm_g"] = misc[64:128].reshape(1, HD)
    out["attn_sinks"] = misc[128:144].reshape(1, NQ)
    out["rel_bias"] = misc[256:768].reshape(NBUCKET, NQ)
    return out, misc[144]


def kernel(x, norm_mix_g, w_in, q_norm_g, k_norm_g, attn_sinks, rel_bias, w_attn_o, w_dw, b_dw, conv_ln_g, conv_ln_b, w_conv_out, w_out, norm_mlp_g, w_ff1, w_ff2, loss_target, m_norm_mix_g, m_w_in, m_q_norm_g, m_k_norm_g, m_attn_sinks, m_rel_bias, m_w_attn_o, m_w_dw, m_b_dw, m_conv_ln_g, m_conv_ln_b, m_w_conv_out, m_w_out, m_norm_mlp_g, m_w_ff1, m_w_ff2, v_norm_mix_g, v_w_in, v_q_norm_g, v_k_norm_g, v_attn_sinks, v_rel_bias, v_w_attn_o, v_w_dw, v_b_dw, v_conv_ln_g, v_conv_ln_b, v_w_conv_out, v_w_out, v_norm_mlp_g, v_w_ff1, v_w_ff2):
    args = dict(locals())
    wts = {nm: args[nm] for nm in WEIGHTS}
    mom = {nm: args["m_" + nm] for nm in WEIGHTS}
    var = {nm: args["v_" + nm] for nm in WEIGHTS}
    chip = 2 * lax.axis_index("x") + lax.axis_index("y")

    chip_arr = jnp.reshape(chip, (1,)).astype(jnp.int32)
    chip_core = jnp.stack([chip, lax.axis_index("c")]).astype(jnp.int32)
    placed = {nm: _place_shard(nm, wts[nm][0], chip_arr, bf16) for nm in BIG}
    placed["w_dw"] = _place_shard("w_dw", jnp.pad(w_dw[0], ((0, 1), (0, 0))), chip_arr, f32)

    loss_part, grad_x, g, shards = _forward_backward(x[0], loss_target[0], wts, placed, chip_core)

    small = jnp.concatenate([_pack_small(g, loss_part), g["w_dw"]], axis=0)
    small = _allreduce_small(small)
    grads, loss = _unpack_small(small)
    grads["w_dw"] = lax.dynamic_slice(small[W_DW_ROW:W_DW_ROW + CW], (0, chip * (D // 4)), (CW, D // 4))
    grads.update(shards)

    delta, new_m, new_v = {}, {}, {}
    sd, sm, sv = _adamw(_pack_small(wts), small[0:8], _pack_small(mom), _pack_small(var), "adamw_small")
    for res, blk in ((delta, sd), (new_m, sm), (new_v, sv)):
        res.update(_unpack_small(blk)[0])
    for nm in BIG + ["w_dw"]:
        shp = wts[nm].shape
        two_d = lambda a: a.reshape(shp[-2], shp[-1])
        delta[nm], new_m[nm], new_v[nm] = _adamw(two_d(wts[nm]), grads[nm], two_d(mom[nm]), two_d(var[nm]), "adamw_" + nm)

    def shaped(vals):
        return [vals[nm].reshape(wts[nm].shape) for nm in WEIGHTS]

    return (loss, grad_x[None], *shaped(grads), *shaped(delta), *shaped(new_m), *shaped(new_v))
```

```python
import numpy as np
import jax
import jax.numpy as jnp
from jax import lax
from jax.experimental import pallas as pl
from jax.experimental.pallas import tpu as pltpu

f32 = jnp.float32
bf16 = jnp.bfloat16
SDS = jax.ShapeDtypeStruct
MESH = pl.DeviceIdType.MESH

D = 1024
HD = 64
NQ = 16
NKV = 4
BLK = 128
CW = 31
HALO = 32
DFF = 4096
NBUCKET = 32
EPS = 1e-6
NEG = -1e30
INW = 5632
MIX_CHUNK = 256
C_Q, C_A, C_G, C_GA, C_GC = 0, 1, 2, 3, 4
C_KV = 10

ADAM_LR = 0.001
ADAM_B1 = 0.9
ADAM_B2 = 0.999
ADAM_EPS = 1e-08
ADAM_WD = 0.01
ADAM_STEP = 10

VMEM_BYTES_V7X = 64 << 20


def _cparams(sem, vmem_mb):
    assert (vmem_mb << 20) < VMEM_BYTES_V7X
    return pltpu.CompilerParams(dimension_semantics=sem, vmem_limit_bytes=vmem_mb << 20)


ANY = pl.BlockSpec(memory_space=pl.ANY)


class _Comm:
    def __init__(self, ins, out_shapes, n_sems, start, finish, mid=None, aliases=None):
        self.ins, self.out_shapes, self.n_sems = list(ins), list(out_shapes), n_sems
        self.start, self.finish, self.mid, self.aliases = start, finish, mid, dict(aliases or {})


class _SemOffset:
    def __init__(self, sems, base):
        self._sems, self._base = sems, base
        self.at = self

    def __getitem__(self, k):
        return self._sems.at[self._base + k]


def _merge_comms(a, b):
    assert a.mid is None and b.mid is None
    n_in, n_out = len(a.ins), len(a.out_shapes)

    def both(fa, fb):
        def run(cin, cout, send, recv):
            fa(cin[:n_in], cout[:n_out], send, recv)
            fb(cin[n_in:], cout[n_out:], _SemOffset(send, a.n_sems), _SemOffset(recv, a.n_sems))
        return run

    aliases = {**a.aliases, **{n_in + k: n_out + v for k, v in b.aliases.items()}}
    return _Comm(a.ins + b.ins, a.out_shapes + b.out_shapes, a.n_sems + b.n_sems, both(a.start, b.start),
                 both(a.finish, b.finish), aliases=aliases)


def _call(body, args, *, grid, in_specs, out_specs, out_shape, name, sem, vmem_mb, scratch_shapes=(), comm=None,
          mid_step=None):
    n_in, n_out, n_scr = len(in_specs), len(out_specs), len(scratch_shapes)
    if comm is None:
        outs = pl.pallas_call(body, grid=grid, in_specs=list(in_specs), out_specs=list(out_specs),
                              out_shape=list(out_shape), scratch_shapes=list(scratch_shapes), name=name,
                              compiler_params=_cparams(sem, vmem_mb))(*args)
        return list(outs), []
    ci, co = len(comm.ins), len(comm.out_shapes)
    last = grid[0] - 1

    def wrapped(*refs):
        ins, cin = refs[:n_in], refs[n_in:n_in + ci]
        outs = refs[n_in + ci:n_in + ci + n_out]
        cout = refs[n_in + ci + n_out:n_in + ci + n_out + co]
        scr = refs[n_in + ci + n_out + co:]
        send, recv = scr[n_scr], scr[n_scr + 1]
        step = pl.program_id(0)

        @pl.when(step == 0)
        def _():
            comm.start(cin, cout, send, recv)

        body(*ins, *outs, *scr[:n_scr])
        if comm.mid is not None:
            @pl.when(step == mid_step)
            def _():
                comm.mid(cin, cout, send, recv)

        @pl.when(step == last)
        def _():
            comm.finish(cin, cout, send, recv)

    res = pl.pallas_call(
        wrapped, grid=grid, in_specs=list(in_specs) + [ANY] * ci, out_specs=list(out_specs) + [ANY] * co,
        out_shape=list(out_shape) + comm.out_shapes,
        input_output_aliases={n_in + k: n_out + v for k, v in comm.aliases.items()},
        scratch_shapes=list(scratch_shapes) + [pltpu.SemaphoreType.DMA((comm.n_sems,))] * 2,
        name=name, compiler_params=_cparams(("arbitrary",), vmem_mb))(*args, *comm.ins)
    return list(res[:n_out]), list(res[n_out:])


def _run_comm(comm, name):
    ci, co = len(comm.ins), len(comm.out_shapes)

    def body(*refs):
        cin, cout, (send, recv) = refs[:ci], refs[ci:ci + co], refs[ci + co:]
        comm.start(cin, cout, send, recv)
        if comm.mid is not None:
            comm.mid(cin, cout, send, recv)
        comm.finish(cin, cout, send, recv)

    return pl.pallas_call(
        body, in_specs=[ANY] * ci, out_specs=[ANY] * co, out_shape=comm.out_shapes, input_output_aliases=comm.aliases,
        scratch_shapes=[pltpu.SemaphoreType.DMA((comm.n_sems,))] * 2, name=name)(*comm.ins)


def _dot(a, b):
    return jnp.dot(a, b, preferred_element_type=f32)


def _dot_nt(a, b):
    return lax.dot_general(a, b, (((1,), (1,)), ((), ())), preferred_element_type=f32)


def _dot_tn(a, b):
    return lax.dot_general(a, b, (((0,), (0,)), ((), ())), preferred_element_type=f32)


def _sigmoid(x):
    return 0.5 * jnp.tanh(0.5 * x) + 0.5


def _low_head_lanes():
    return lax.broadcasted_iota(jnp.int32, (1, 2 * HD), 1) < HD


def _head_blockdiag():
    r = lax.broadcasted_iota(jnp.int32, (2 * HD, 2 * HD), 0) // HD
    c = lax.broadcasted_iota(jnp.int32, (2 * HD, 2 * HD), 1) // HD
    return jnp.where(r == c, 1.0, 0.0).astype(bf16)


def _head_sums(z, bd):
    hi = z.astype(bf16)
    lo = (z - hi.astype(f32)).astype(bf16)
    return _dot(hi, bd) + _dot(lo, bd)


def _weight_cols(start, width):
    kv_width = 2 * NKV * HD
    if start < D:
        orig = start
    elif start < INW - kv_width:
        orig = start + kv_width
    else:
        orig = start - (INW - kv_width) + D
    assert (start < D) == (start + width <= D) and (start < INW - kv_width) == (start + width <= INW - kv_width)
    return slice(orig, orig + width)


def _rms_inproj(x, g, w, perm, gq2, gk2, comm=None):
    T, N = x.shape[0], w.shape[1]
    tn = 512
    conv_cols = (C_A * D, (C_G + 1) * D)
    attn_chunks = [C_Q * D // tn, C_Q * D // tn + 1, C_KV]

    def body(x_ref, g_ref, w_ref, perm_ref, gq_ref, gk_ref, p_ref, u_ref, tail_ref, qn_ref, kk_ref, vv_ref):
        xv = x_ref[...]
        r = lax.rsqrt(jnp.mean(xv * xv, axis=-1, keepdims=True) + EPS)
        u = (xv * r * g_ref[...]).astype(bf16)
        u_ref[...] = u
        u_blocks = _dot(perm_ref[...], u).astype(bf16)

        def project(c):
            lhs = u_blocks if conv_cols[0] <= c * tn < conv_cols[1] else u
            p_ref[:, c * tn:(c + 1) * tn] = _dot(lhs, w_ref[:, _weight_cols(c * tn, tn)])

        for c in attn_chunks:
            project(c)
        bd = _head_blockdiag()
        lo = _low_head_lanes()
        for p in range(NQ // 2):
            z = p_ref[:, C_Q * D + 128 * p:C_Q * D + 128 * p + 128]
            rq = lax.rsqrt(_head_sums(z * z, bd) * (1.0 / HD) + EPS)
            qn_ref[:, 128 * p:128 * p + 128] = (z * rq * gq_ref[...] * (HD ** -0.5)).astype(bf16)
        kv0 = C_KV * tn
        for p in range(NKV // 2):
            z = p_ref[:, kv0 + 128 * p:kv0 + 128 * p + 128]
            rk = lax.rsqrt(_head_sums(z * z, bd) * (1.0 / HD) + EPS)
            _split_pair(z * rk * gk_ref[...], kk_ref, p, lo)
            _split_pair(p_ref[:, kv0 + 256 + 128 * p:kv0 + 256 + 128 * p + 128], vv_ref, p, lo)
        for c in range(N // tn):
            if c not in attn_chunks:
                project(c)
        for k in range(NBLK):
            tail_ref[k:k + 1, :] = p_ref[RPB * k + RPB - 1:RPB * (k + 1), conv_cols[0]:conv_cols[1]]

    once = pl.Buffered(1)
    row = pl.BlockSpec((TT, D), lambda i: (i, 0))
    vec = pl.BlockSpec((1, 128), lambda i: (0, 0))
    return _call(
        body, (x, g, w, perm, gq2, gk2), grid=(T // TT,),
        in_specs=[row, pl.BlockSpec((1, D), lambda i: (0, 0)),
                  pl.BlockSpec((D, N), lambda i: (0, 0), pipeline_mode=once),
                  pl.BlockSpec((TT, TT), lambda i: (0, 0), pipeline_mode=once), vec, vec],
        out_specs=[pl.BlockSpec((TT, N), lambda i: (i, 0)), row, pl.BlockSpec((NBLK, 2 * D), lambda i: (i, 0)),
                   row, row, row],
        out_shape=[SDS((T, N), f32), SDS((T, D), bf16), SDS((T // TT * NBLK, 2 * D), f32)] + [SDS((T, D), bf16)] * 3,
        name="rms_inproj", sem=("parallel",), vmem_mb=56, comm=comm, mid_step=(3 * (T // TT)) // 4)


def _split_pair(pair, out_ref, p, lo):
    rolled = pltpu.roll(pair, HD, 1)
    zero = jnp.zeros_like(pair)
    c = 512 * p
    out_ref[:, c:c + 128] = jnp.where(lo, pair, zero).astype(bf16)
    out_ref[:, c + 128:c + 256] = jnp.where(lo, zero, rolled).astype(bf16)
    out_ref[:, c + 256:c + 384] = jnp.where(lo, rolled, zero).astype(bf16)
    out_ref[:, c + 384:c + 512] = jnp.where(lo, zero, pair).astype(bf16)


def _bucket_tile():
    qi = np.arange(BLK)[:, None]
    kj = np.arange(BLK)[None, :]
    n = np.where(kj > qi, qi + BLK - kj, qi - kj)
    max_exact = NBUCKET // 2
    nf = np.maximum(n, 1).astype(np.float32)
    large = max_exact + (np.log(nf / max_exact) / np.float32(np.log(128 / max_exact))
                         * (NBUCKET - max_exact)).astype(np.int32)
    large = np.minimum(large, NBUCKET - 1)
    return np.where(n < max_exact, n, large).astype(np.int32)


def _from_prev_block():
    return lax.broadcasted_iota(jnp.int32, (BLK, BLK), 1) > lax.broadcasted_iota(jnp.int32, (BLK, BLK), 0)


def _bias_tiles(rel_bias):
    def body(rb_ref, bk_ref, out_ref):
        bk = bk_ref[...]
        for h in range(NQ):
            acc = jnp.zeros((BLK, BLK), f32)
            for b in range(NBUCKET):
                acc = jnp.where(bk == b, rb_ref[b, h], acc)
            out_ref[h] = acc

    return pl.pallas_call(
        body,
        in_specs=[pl.BlockSpec(memory_space=pltpu.SMEM), pl.BlockSpec(memory_space=pltpu.VMEM)],
        out_specs=pl.BlockSpec(memory_space=pltpu.VMEM),
        out_shape=SDS((NQ, BLK, BLK), f32),
        name="bias_tiles")(rel_bias, jnp.asarray(_bucket_tile()))


def _rows2(ref, c):
    return jnp.concatenate([ref[:, c:c + 128], ref[:, c + 128:c + 256]], axis=0)


def _attn_fwd(qn, kk, vv, bias, sinks, comm=None):
    T = qn.shape[0]
    nb = T // BLK

    def body(s_ref, q_ref, kc_ref, kp_ref, vc_ref, vp_ref, b_ref, o_ref, lse_ref):
        prev = _from_prev_block()
        no_key = jnp.logical_and(prev, pl.program_id(0) == 0)
        scores = []
        for h in range(NKV):
            qs = _rows2(q_ref, 256 * h)
            scores.append((_dot_nt(qs, _rows2(kc_ref, 256 * h)), _dot_nt(qs, _rows2(kp_ref, 256 * h))))
        for h in range(NKV):
            c = 256 * h
            sc, sp = scores[h]
            vstack = jnp.concatenate([vp_ref[:, c:c + 128], vc_ref[:, c:c + 128],
                                      vp_ref[:, c + 128:c + 256], vc_ref[:, c + 128:c + 256]], axis=0)
            for pr in range(2):
                ps = []
                for e in range(2):
                    hq = 4 * h + 2 * pr + e
                    rows, cols = slice(128 * pr, 128 * pr + 128), slice(128 * e, 128 * e + 128)
                    s = jnp.where(no_key, NEG, jnp.where(prev, sp[rows, cols], sc[rows, cols]) + b_ref[hq])
                    sink = s_ref[0, hq]
                    m = jnp.maximum(jnp.max(s, axis=-1, keepdims=True), sink)
                    ex = jnp.exp(s - m)
                    l = jnp.sum(ex, axis=-1, keepdims=True) + jnp.exp(sink - m)
                    p = ex * (1.0 / l)
                    ps += [jnp.where(prev, p, 0.0).astype(bf16), jnp.where(prev, 0.0, p).astype(bf16)]
                    lse_ref[:, hq:hq + 1] = m + jnp.log(l)
                o_ref[:, c + 128 * pr:c + 128 * pr + 128] = _dot(jnp.concatenate(ps, axis=1), vstack).astype(bf16)

    blk = lambda f: pl.BlockSpec((BLK, D), f)
    cur = lambda n: (n, 0)
    prev = lambda n: (jnp.maximum(n - 1, 0), 0)
    return _call(
        body, (sinks, qn, kk, kk, vv, vv, bias), grid=(nb,),
        in_specs=[pl.BlockSpec(memory_space=pltpu.SMEM), blk(cur), blk(cur), blk(prev), blk(cur), blk(prev),
                  pl.BlockSpec((NQ, BLK, BLK), lambda n: (0, 0, 0))],
        out_specs=[blk(cur), pl.BlockSpec((BLK, NQ), cur)],
        out_shape=[SDS((T, D), bf16), SDS((T, NQ), f32)],
        name="attn_fwd", sem=("parallel",), vmem_mb=32, comm=comm, mid_step=(3 * nb) // 4)


TT = 512
NBLK = 32
RPB = TT // NBLK
CONV_LANES = 128
KBLK = 4
GBLK = 8
TAPG = 8


def _block_perm():
    p = np.arange(TT)
    m = np.zeros((TT, TT), np.float32)
    m[p, NBLK * (p % RPB) + p // RPB] = 1.0
    return jnp.asarray(m, bf16), jnp.asarray(m.T, bf16)


def _lane_groups():
    return [slice(q * CONV_LANES, (q + 1) * CONV_LANES) for q in range(D // CONV_LANES)]


def _fill_time_blocks(z, tile, edge, causal):
    row = lax.broadcasted_iota(jnp.int32, (RPB, 1), 0)
    for k in range(NBLK):
        blk = tile[RPB * k:RPB * (k + 1)]
        if causal:
            z[NBLK + k] = blk
            z[k] = jnp.where(row == 0, edge[k:k + 1], pltpu.roll(blk, 1, 0))
        else:
            z[k] = blk
            z[NBLK + k] = jnp.where(row == RPB - 1, edge[k:k + 1], pltpu.roll(blk, RPB - 1, 0))


def _block_conv(z, w_ref, tap_offset, init, store):
    def step(s, carry):
        k0 = s * KBLK
        for ln in _lane_groups():
            accs = [init(ln) for _ in range(KBLK)]
            for g0 in range(0, CW, TAPG):
                taps = range(g0, min(g0 + TAPG, CW))
                lo = min(tap_offset(j) for j in taps)
                hi = max(tap_offset(j) for j in taps)
                win = [z[k0 + lo + d, :, ln] for d in range(KBLK + hi - lo)]
                for j in taps:
                    wv = w_ref[j:j + 1, ln]
                    for q in range(KBLK):
                        accs[q] = accs[q] + win[q + tap_offset(j) - lo] * wv
            for q in range(KBLK):
                store(k0 + q, ln, accs[q])
        return carry

    lax.fori_loop(0, NBLK // KBLK, step, 0)


def _block_rows(k):
    return pl.ds(pl.multiple_of(k * RPB, RPB), RPB)


def _glu_conv_fwd(proj, tail, w_dw, b_dw, ln_g, ln_b, perm_t, comm=None):
    T = proj.shape[0]

    def body(a_ref, g_ref, ta_ref, tg_ref, w_ref, b_ref, lg_ref, lb_ref, pt_ref, h1_ref, h3_ref, z):
        edge = jnp.where(pl.program_id(0) > 0, ta_ref[...] * _sigmoid(tg_ref[...]), 0.0)
        _fill_time_blocks(z, a_ref[...] * _sigmoid(g_ref[...]), edge, causal=True)

        def store(k, ln, value):
            h1_ref[_block_rows(k), ln] = value

        _block_conv(z, w_ref, lambda j: NBLK - (CW - 1) + j,
                    lambda ln: jnp.broadcast_to(b_ref[:, ln], (RPB, CONV_LANES)), store)
        h1 = h1_ref[...]
        mu = jnp.mean(h1, axis=-1, keepdims=True)
        xc = h1 - mu
        var = jnp.mean(xc * xc, axis=-1, keepdims=True)
        h2 = xc * lax.rsqrt(var + EPS) * lg_ref[...] + lb_ref[...]
        h3_ref[...] = _dot(pt_ref[...], (h2 * _sigmoid(h2)).astype(bf16)).astype(bf16)

    tile = lambda cb: pl.BlockSpec((TT, D), lambda i: (i, cb))
    edge = lambda cb: pl.BlockSpec((NBLK, D), lambda i: (jnp.maximum(i - 1, 0), cb))
    vec = pl.BlockSpec((1, D), lambda i: (0, 0))
    (h1, h3), got = _call(
        body, (proj, proj, tail, tail, w_dw, b_dw, ln_g, ln_b, perm_t), grid=(T // TT,),
        in_specs=[tile(C_A), tile(C_G), edge(0), edge(1), pl.BlockSpec((HALO, D), lambda i: (0, 0)), vec, vec, vec,
                  pl.BlockSpec((TT, TT), lambda i: (0, 0), pipeline_mode=pl.Buffered(1))],
        out_specs=[pl.BlockSpec((TT, D), lambda i: (i, 0))] * 2,
        out_shape=[SDS((T, D), f32), SDS((T, D), bf16)],
        scratch_shapes=[pltpu.VMEM((2 * NBLK, RPB, D), f32)],
        name="glu_conv_fwd", sem=("parallel",), vmem_mb=40, comm=comm, mid_step=(3 * (T // TT)) // 4)
    return h1, h3, got


def _mix_out(o, h3, proj, x, w_attn_o, w_conv_out, w_out, g_mlp):
    T = x.shape[0]
    tm = 512

    def body(o_ref, h3_ref, ga_ref, gc_ref, x_ref, wa_ref, wc_ref, wo_ref, g_ref,
             attn_ref, conv_ref, mg_ref, x1_ref, n2_ref):
        x1 = x_ref[...]
        for j in range(D // MIX_CHUNK):
            cols = slice(j * MIX_CHUNK, (j + 1) * MIX_CHUNK)
            attn = _dot(o_ref[...], wa_ref[:, cols])
            conv = _dot(h3_ref[...], wc_ref[:, cols])
            attn_ref[:, cols] = attn.astype(bf16)
            conv_ref[:, cols] = conv.astype(bf16)
            mg = (_sigmoid(ga_ref[:, cols]) * attn + _sigmoid(gc_ref[:, cols]) * conv).astype(bf16)
            mg_ref[:, cols] = mg
            x1 = x1 + _dot(mg, wo_ref[cols, :])
        x1_ref[...] = x1
        r = lax.rsqrt(jnp.mean(x1 * x1, axis=-1, keepdims=True) + EPS)
        n2_ref[...] = (x1 * r * g_ref[...]).astype(bf16)

    tile = lambda cb=0: pl.BlockSpec((tm, D), lambda i: (i, cb))
    wfull = pl.BlockSpec((D, D), lambda i: (0, 0), pipeline_mode=pl.Buffered(1))
    return pl.pallas_call(
        body, grid=(T // tm,),
        in_specs=[tile(), tile(), tile(C_GA), tile(C_GC), tile(), wfull, wfull, wfull,
                  pl.BlockSpec((1, D), lambda i: (0, 0))],
        out_specs=[tile()] * 5,
        out_shape=[SDS((T, D), bf16), SDS((T, D), bf16), SDS((T, D), bf16), SDS((T, D), f32), SDS((T, D), bf16)],
        name="mix_out", compiler_params=_cparams(("parallel",), 48))(o, h3, proj, proj, x, w_attn_o, w_conv_out, w_out, g_mlp)


def _mlp_fwd(n2, w1, w2, x1, tgt):
    T = n2.shape[0]
    tm, tf = 512, 1024

    def body(n2_ref, w1_ref, w2_ref, x1_ref, t_ref, hm_ref, slope_ref, dy_ref, dyb_ref, loss_ref):
        @pl.when(pl.program_id(0) == 0)
        def _():
            loss_ref[...] = jnp.zeros_like(loss_ref)

        n2v = n2_ref[...]
        for c in range(DFF // tf):
            r = jnp.maximum(_dot(n2v, w1_ref[:, c * tf:(c + 1) * tf]), 0.0)
            hm_ref[:, c * tf:(c + 1) * tf] = (r * r).astype(bf16)
            slope_ref[:, c * tf:(c + 1) * tf] = (2.0 * r).astype(bf16)
        e = x1_ref[...] + _dot(hm_ref[...], w2_ref[...]) - t_ref[...]
        dy = e * (1.0 / D)
        dy_ref[...] = dy
        dyb_ref[...] = dy.astype(bf16)
        loss_ref[...] += 0.5 * jnp.sum(jnp.sum(e * e, axis=-1, keepdims=True) * (1.0 / D))

    row = pl.BlockSpec((tm, D), lambda i: (i, 0))
    once = pl.Buffered(1)
    return pl.pallas_call(
        body, grid=(T // tm,),
        in_specs=[row, pl.BlockSpec((D, DFF), lambda i: (0, 0), pipeline_mode=once),
                  pl.BlockSpec((DFF, D), lambda i: (0, 0), pipeline_mode=once), row, row],
        out_specs=[pl.BlockSpec((tm, DFF), lambda i: (i, 0)), pl.BlockSpec((tm, DFF), lambda i: (i, 0)), row, row,
                   pl.BlockSpec((8, 128), lambda i: (0, 0))],
        out_shape=[SDS((T, DFF), bf16), SDS((T, DFF), bf16), SDS((T, D), f32), SDS((T, D), bf16), SDS((8, 128), f32)],
        name="mlp_fwd", compiler_params=_cparams(("arbitrary",), 60))(n2, w1, w2, x1, tgt)


def _rms_bwd(xv, g, dn, dres):
    r = lax.rsqrt(jnp.mean(xv * xv, axis=-1, keepdims=True) + EPS)
    gd = dn * g
    dx = dres + r * gd - xv * (r * r * r) * jnp.mean(xv * gd, axis=-1, keepdims=True)
    dg = jnp.sum(dn * xv * r, axis=0, keepdims=True)
    return dx, dg


def _mlp_bwd(dy, dyb, slope, w1, w2, x1, g_mlp):
    T = dy.shape[0]
    tm, tf = 512, 1024

    def body(dy_ref, dyb_ref, slope_ref, w1_ref, w2_ref, x1_ref, g_ref, df_ref, dx_ref, dxb_ref, dg_ref):
        @pl.when(pl.program_id(0) == 0)
        def _():
            dg_ref[...] = jnp.zeros_like(dg_ref)

        dyb = dyb_ref[...]
        for c in range(DFF // tf):
            cols = slice(c * tf, (c + 1) * tf)
            d_hm = _dot_nt(dyb, w2_ref[cols, :])
            df_ref[:, cols] = (d_hm * slope_ref[:, cols].astype(f32)).astype(bf16)
        dn = _dot_nt(df_ref[...], w1_ref[...])
        dx, dg = _rms_bwd(x1_ref[...], g_ref[...], dn, dy_ref[...])
        dx_ref[...] = dx
        dxb_ref[...] = dx.astype(bf16)
        dg_ref[...] += dg

    row = pl.BlockSpec((tm, D), lambda i: (i, 0))
    wide = pl.BlockSpec((tm, DFF), lambda i: (i, 0))
    vec = pl.BlockSpec((1, D), lambda i: (0, 0))
    once = pl.Buffered(1)
    return pl.pallas_call(
        body, grid=(T // tm,),
        in_specs=[row, row, wide, pl.BlockSpec((D, DFF), lambda i: (0, 0), pipeline_mode=once),
                  pl.BlockSpec((DFF, D), lambda i: (0, 0), pipeline_mode=once), row, vec],
        out_specs=[wide, row, row, vec],
        out_shape=[SDS((T, DFF), bf16), SDS((T, D), f32), SDS((T, D), bf16), SDS((1, D), f32)],
        name="mlp_bwd", compiler_params=_cparams(("arbitrary",), 56))(dy, dyb, slope, w1, w2, x1, g_mlp)


def _wgrad(a, b, name, tn=1024):
    T, M = a.shape
    N = b.shape[1]
    tmm, tk = min(M, 1024), min(T, 2048)

    def body(a_ref, b_ref, o_ref):
        @pl.when(pl.program_id(2) == 0)
        def _():
            o_ref[...] = jnp.zeros_like(o_ref)

        o_ref[...] += _dot_tn(a_ref[...], b_ref[...])

    return pl.pallas_call(
        body, grid=(M // tmm, N // tn, T // tk),
        in_specs=[pl.BlockSpec((tk, tmm), lambda m, n, t: (t, m)), pl.BlockSpec((tk, tn), lambda m, n, t: (t, n))],
        out_specs=pl.BlockSpec((tmm, tn), lambda m, n, t: (m, n)),
        out_shape=SDS((M, N), f32),
        name=name, compiler_params=_cparams(("parallel", "parallel", "arbitrary"), 40))(a, b)


def _mix_bwd(dx1b, proj, attn, conv, h1, w_attn_o, w_conv_out, w_out, ln_g, ln_b, perm, comm=None):
    T = dx1b.shape[0]
    tm = TT

    def body(dx_ref, ga_ref, gc_ref, attn_ref, conv_ref, h1_ref, wa_ref, wc_ref, wo_ref, lg_ref, lb_ref, perm_ref,
             dat_ref, dcv_ref, do_ref, dh1_ref, dga_ref, dgc_ref, acc_ref, head_ref):
        @pl.when(pl.program_id(0) == 0)
        def _():
            acc_ref[...] = jnp.zeros_like(acc_ref)

        d_o, d_h3 = None, None
        for j in range(D // MIX_CHUNK):
            cols = slice(j * MIX_CHUNK, (j + 1) * MIX_CHUNK)
            dm = _dot_nt(dx_ref[...], wo_ref[cols, :])
            sa = _sigmoid(ga_ref[:, cols])
            sc = _sigmoid(gc_ref[:, cols])
            dat = (dm * sa).astype(bf16)
            dcv = (dm * sc).astype(bf16)
            dat_ref[:, cols] = dat
            dcv_ref[:, cols] = dcv
            dga_ref[:, cols] = (dm * attn_ref[:, cols].astype(f32) * sa * (1.0 - sa)).astype(bf16)
            dgc_ref[:, cols] = (dm * conv_ref[:, cols].astype(f32) * sc * (1.0 - sc)).astype(bf16)
            part_o = _dot_nt(dat, wa_ref[:, cols])
            part_h = _dot_nt(_dot(perm_ref[...], dcv).astype(bf16), wc_ref[:, cols])
            d_o = part_o if d_o is None else d_o + part_o
            d_h3 = part_h if d_h3 is None else d_h3 + part_h
        do_ref[...] = d_o.astype(bf16)
        h1 = h1_ref[...]
        mu = jnp.mean(h1, axis=-1, keepdims=True)
        xc = h1 - mu
        rstd = lax.rsqrt(jnp.mean(xc * xc, axis=-1, keepdims=True) + EPS)
        xh = xc * rstd
        h2 = xh * lg_ref[...] + lb_ref[...]
        sg = _sigmoid(h2)
        dh2 = d_h3 * (sg * (1.0 + h2 * (1.0 - sg)))
        dxh = dh2 * lg_ref[...]
        dh1 = rstd * (dxh - jnp.mean(dxh, axis=-1, keepdims=True) - xh * jnp.mean(dxh * xh, axis=-1, keepdims=True))
        dh1_ref[...] = dh1
        for k in range(NBLK):
            head_ref[k:k + 1, :] = dh1[RPB * k:RPB * k + 1]
        acc_ref[0:1, :] += jnp.sum(dh2 * xh, axis=0, keepdims=True)
        acc_ref[1:2, :] += jnp.sum(dh2, axis=0, keepdims=True)
        acc_ref[2:3, :] += jnp.sum(dh1, axis=0, keepdims=True)

    tile = lambda cb=0: pl.BlockSpec((tm, D), lambda i: (i, cb))
    wfull = pl.BlockSpec((D, D), lambda i: (0, 0), pipeline_mode=pl.Buffered(1))
    vec = pl.BlockSpec((1, D), lambda i: (0, 0))
    return _call(
        body, (dx1b, proj, proj, attn, conv, h1, w_attn_o, w_conv_out, w_out, ln_g, ln_b, perm), grid=(T // tm,),
        in_specs=[tile(), tile(C_GA), tile(C_GC), tile(), tile(), tile(), wfull, wfull, wfull, vec, vec,
                  pl.BlockSpec((TT, TT), lambda i: (0, 0), pipeline_mode=pl.Buffered(1))],
        out_specs=[tile()] * 6 + [pl.BlockSpec((8, D), lambda i: (0, 0)), pl.BlockSpec((NBLK, D), lambda i: (i, 0))],
        out_shape=[SDS((T, D), bf16), SDS((T, D), bf16), SDS((T, D), bf16), SDS((T, D), f32),
                   SDS((T, D), bf16), SDS((T, D), bf16), SDS((8, D), f32), SDS((T // RPB, D), f32)],
        name="mix_bwd", sem=("arbitrary",), vmem_mb=56, comm=comm)


def _conv_bwd(dh1, head, proj, tail, w_dw, perm_t, comm=None):
    T = dh1.shape[0]
    nt = T // TT

    def body(d_ref, hd_ref, a_ref, g_ref, ta_ref, tg_ref, w_ref, pt_ref, da_ref, dg_ref, gw_ref, zd, zh, dh0, gacc):
        i = pl.program_id(0)

        @pl.when(i == 0)
        def _():
            gacc[...] = jnp.zeros_like(gacc)

        a = a_ref[...]
        sg = _sigmoid(g_ref[...])
        _fill_time_blocks(zd, d_ref[...], jnp.where(i < nt - 1, hd_ref[...], 0.0), causal=False)
        _fill_time_blocks(zh, a * sg, jnp.where(i > 0, ta_ref[...] * _sigmoid(tg_ref[...]), 0.0), causal=True)

        def store(k, ln, value):
            dh0[_block_rows(k), ln] = value

        _block_conv(zd, w_ref, lambda j: (CW - 1) - j, lambda ln: jnp.zeros((RPB, CONV_LANES), f32), store)

        for ln in _lane_groups():
            for g0 in range(0, CW, TAPG):
                taps = list(range(g0, min(g0 + TAPG, CW)))

                def add_blocks(s, accs, ln=ln, taps=taps):
                    k0 = s * GBLK
                    first = k0 + NBLK - (CW - 1) + taps[0]
                    win = [zh[first + t, :, ln] for t in range(GBLK + len(taps) - 1)]
                    accs = list(accs)
                    for q in range(GBLK):
                        d = zd[k0 + q, :, ln]
                        for n, j in enumerate(taps):
                            accs[n] = accs[n] + d * win[q + j - taps[0]]
                    return tuple(accs)

                accs = lax.fori_loop(0, NBLK // GBLK, add_blocks,
                                     tuple(jnp.zeros((RPB, CONV_LANES), f32) for _ in taps))
                for j, acc in zip(taps, accs):
                    gacc[j, :, ln] += acc

        d0 = dh0[...]
        da_ref[...] = _dot(pt_ref[...], (d0 * sg).astype(bf16)).astype(bf16)
        dg_ref[...] = _dot(pt_ref[...], (d0 * a * sg * (1.0 - sg)).astype(bf16)).astype(bf16)

        @pl.when(i == nt - 1)
        def _():
            gw_ref[...] = jnp.zeros_like(gw_ref)
            for j in range(CW):
                gw_ref[j:j + 1, :] = jnp.sum(gacc[j], axis=0, keepdims=True)

    tile = lambda cb=0: pl.BlockSpec((TT, D), lambda i: (i, cb))
    prev_edge = lambda cb: pl.BlockSpec((NBLK, D), lambda i: (jnp.maximum(i - 1, 0), cb))
    next_edge = pl.BlockSpec((NBLK, D), lambda i: (jnp.minimum(i + 1, nt - 1), 0))
    wspec = pl.BlockSpec((HALO, D), lambda i: (0, 0))
    return _call(
        body, (dh1, head, proj, proj, tail, tail, w_dw, perm_t), grid=(nt,),
        in_specs=[tile(), next_edge, tile(C_A), tile(C_G), prev_edge(0), prev_edge(1), wspec,
                  pl.BlockSpec((TT, TT), lambda i: (0, 0), pipeline_mode=pl.Buffered(1))],
        out_specs=[tile(), tile(), wspec],
        out_shape=[SDS((T, D), bf16), SDS((T, D), bf16), SDS((HALO, D), f32)],
        scratch_shapes=[pltpu.VMEM((2 * NBLK, RPB, D), f32), pltpu.VMEM((2 * NBLK, RPB, D), f32),
                        pltpu.VMEM((TT, D), f32), pltpu.VMEM((HALO, RPB, D), f32)],
        name="conv_bwd", sem=("arbitrary",), vmem_mb=48, comm=comm)


def _attn_bwd(qn, kk, vv, bias, sinks, o, do, lse, comm=None):
    T = qn.shape[0]
    nb = T // BLK

    def body(s_ref, q_ref, kc_ref, kp_ref, vc_ref, vp_ref, b_ref, o_ref, do_ref, lse_ref,
             dq_ref, dkc_ref, dkp_ref, dvc_ref, dvp_ref, dsk_ref, dsa_ref):
        n = pl.program_id(0)

        @pl.when(n == 0)
        def _():
            dsk_ref[...] = jnp.zeros_like(dsk_ref)
            dsa_ref[...] = jnp.zeros_like(dsa_ref)

        @pl.when(n == nb)
        def _():
            dkp_ref[...] = jnp.zeros_like(dkp_ref)
            dvp_ref[...] = jnp.zeros_like(dvp_ref)

        @pl.when(n < nb)
        def _():
            from_prev = _from_prev_block()
            no_key = jnp.logical_and(from_prev, n == 0)
            lo = _low_head_lanes()
            dups = {"kc": [], "kp": [], "vc": [], "vp": []}
            products = []
            for h in range(NKV):
                qs = _rows2(q_ref, 256 * h)
                dos = _rows2(do_ref, 256 * h)
                products.append((qs, dos, _dot_nt(qs, _rows2(kc_ref, 256 * h)), _dot_nt(qs, _rows2(kp_ref, 256 * h)),
                                 _dot_nt(dos, _rows2(vc_ref, 256 * h)), _dot_nt(dos, _rows2(vp_ref, 256 * h))))
            for h in range(NKV):
                c = 256 * h
                qs, dos, sc, sp, dpc, dpp = products[h]
                kstack = jnp.concatenate([kp_ref[:, c:c + 128], kc_ref[:, c:c + 128],
                                          kp_ref[:, c + 128:c + 256], kc_ref[:, c + 128:c + 256]], axis=0)
                p_c, p_p, ds_c, ds_p = [], [], [], []
                for pr in range(2):
                    cc = c + 128 * pr
                    prod = do_ref[:, cc:cc + 128].astype(f32) * o_ref[:, cc:cc + 128].astype(f32)
                    d_lo = jnp.sum(jnp.where(lo, prod, 0.0), axis=-1, keepdims=True)
                    d_hi = jnp.sum(prod, axis=-1, keepdims=True) - d_lo
                    row_pc, row_pp, row_dc, row_dp = [], [], [], []
                    for e in range(2):
                        hq = 4 * h + 2 * pr + e
                        rows, cols = slice(128 * pr, 128 * pr + 128), slice(128 * e, 128 * e + 128)
                        delta = d_lo if e == 0 else d_hi
                        lse = lse_ref[:, hq:hq + 1]
                        s = jnp.where(from_prev, sp[rows, cols], sc[rows, cols]) + b_ref[hq]
                        p = jnp.where(no_key, 0.0, jnp.exp(s - lse))
                        ds = p * (jnp.where(from_prev, dpp[rows, cols], dpc[rows, cols]) - delta)
                        dsa_ref[hq] += ds
                        dsk_ref[hq] += jnp.broadcast_to(-jnp.sum(jnp.exp(s_ref[0, hq] - lse) * delta), (8, 128))
                        row_pc.append(jnp.where(from_prev, 0.0, p).astype(bf16))
                        row_pp.append(jnp.where(from_prev, p, 0.0).astype(bf16))
                        row_dc.append(jnp.where(from_prev, 0.0, ds).astype(bf16))
                        row_dp.append(jnp.where(from_prev, ds, 0.0).astype(bf16))
                    dq_ref[:, cc:cc + 128] = _dot(jnp.concatenate([row_dp[0], row_dc[0], row_dp[1], row_dc[1]], axis=1),
                                                  kstack).astype(bf16)
                    p_c.append(jnp.concatenate(row_pc, axis=1))
                    p_p.append(jnp.concatenate(row_pp, axis=1))
                    ds_c.append(jnp.concatenate(row_dc, axis=1))
                    ds_p.append(jnp.concatenate(row_dp, axis=1))

                def to_keys(m2, rhs):
                    x2 = _dot_tn(jnp.concatenate(m2, axis=0), rhs)
                    x = jnp.where(lo, x2[0:128], x2[128:256])
                    return x + pltpu.roll(x, HD, 1)

                dups["kc"].append(to_keys(ds_c, qs))
                dups["kp"].append(to_keys(ds_p, qs))
                dups["vc"].append(to_keys(p_c, dos))
                dups["vp"].append(to_keys(p_p, dos))
            for key, ref in (("kc", dkc_ref), ("kp", dkp_ref), ("vc", dvc_ref), ("vp", dvp_ref)):
                d = dups[key]
                ref[:, 0:128] = jnp.where(lo, d[0], d[1]).astype(bf16)
                ref[:, 128:256] = jnp.where(lo, d[2], d[3]).astype(bf16)

    clamp = lambda n: jnp.minimum(n, nb - 1)
    blk = lambda f: pl.BlockSpec((BLK, D), f)
    cur = lambda n: (clamp(n), 0)
    prev = lambda n: (jnp.maximum(clamp(n) - 1, 0), 0)
    back = lambda n: (jnp.maximum(n - 1, 0), 0)
    kvb = lambda f: pl.BlockSpec((BLK, NKV * HD), f)
    return _call(
        body, (sinks, qn, kk, kk, vv, vv, bias, o, do, lse), grid=(nb + 1,),
        in_specs=[pl.BlockSpec(memory_space=pltpu.SMEM), blk(cur), blk(cur), blk(prev), blk(cur), blk(prev),
                  pl.BlockSpec((NQ, BLK, BLK), lambda n: (0, 0, 0)), blk(cur), blk(cur),
                  pl.BlockSpec((BLK, NQ), cur)],
        out_specs=[blk(cur), kvb(cur), kvb(back), kvb(cur), kvb(back),
                   pl.BlockSpec((NQ, 8, 128), lambda n: (0, 0, 0)),
                   pl.BlockSpec((NQ, BLK, BLK), lambda n: (0, 0, 0))],
        out_shape=[SDS((T, D), bf16)] + [SDS((T, NKV * HD), bf16)] * 4 + [SDS((NQ, 8, 128), f32), SDS((NQ, BLK, BLK), f32)],
        name="attn_bwd", sem=("arbitrary",), vmem_mb=40, comm=comm)


def _bias_bwd(dsa):
    def body(bk_ref, ds_ref, out_ref):
        bk = bk_ref[...]
        lane = lax.broadcasted_iota(jnp.int32, (1, 128), 1)
        for h in range(NQ):
            ds = ds_ref[h]
            row = jnp.zeros((1, 128), f32)
            for b in range(NBUCKET):
                row = jnp.where(lane == b, jnp.sum(jnp.where(bk == b, ds, 0.0)), row)
            out_ref[h:h + 1, :] = row

    return pl.pallas_call(body, out_shape=SDS((NQ, 128), f32), name="bias_bwd")(jnp.asarray(_bucket_tile()), dsa)


def _qkv_bwd(proj, gq2, gk2, dqn, dkc, dkp, dvc, dvp, comm=None):
    T = proj.shape[0]
    tm = 512

    def body(q_ref, kv_ref, gq_ref, gk_ref, dq_ref, dkc_ref, dkp_ref, dvc_ref, dvp_ref,
             oq_ref, okv_ref, ggq_ref, ggk_ref):
        @pl.when(pl.program_id(0) == 0)
        def _():
            ggq_ref[...] = jnp.zeros_like(ggq_ref)
            ggk_ref[...] = jnp.zeros_like(ggk_ref)

        bd = _head_blockdiag()

        def norm_bwd(z, dy, g, scale):
            r = lax.rsqrt(_head_sums(z * z, bd) * (1.0 / HD) + EPS)
            gd = dy * g * scale
            dz = r * gd - z * (r * r * r) * _head_sums(z * gd, bd) * (1.0 / HD)
            return dz, jnp.sum(dy * scale * z * r, axis=0, keepdims=True)

        gq = jnp.zeros((1, 128), f32)
        for p in range(NQ // 2):
            ln = slice(128 * p, 128 * p + 128)
            dz, dg = norm_bwd(q_ref[:, ln], dq_ref[:, ln].astype(f32), gq_ref[...], HD ** -0.5)
            oq_ref[:, ln] = dz.astype(bf16)
            gq = gq + dg
        ggq_ref[...] += gq + pltpu.roll(gq, HD, 1)
        gk = jnp.zeros((1, 128), f32)
        for p in range(NKV // 2):
            ln = slice(128 * p, 128 * p + 128)
            dz, dg = norm_bwd(kv_ref[:, ln], dkc_ref[:, ln].astype(f32) + dkp_ref[:, ln].astype(f32), gk_ref[...], 1.0)
            okv_ref[:, ln] = dz.astype(bf16)
            gk = gk + dg
        ggk_ref[...] += gk + pltpu.roll(gk, HD, 1)
        okv_ref[:, 256:512] = (dvc_ref[...].astype(f32) + dvp_ref[...].astype(f32)).astype(bf16)

    vec = pl.BlockSpec((1, 128), lambda i: (0, 0))
    kvb = pl.BlockSpec((tm, NKV * HD), lambda i: (i, 0))
    return _call(
        body, (proj, proj, gq2, gk2, dqn, dkc, dkp, dvc, dvp), grid=(T // tm,),
        in_specs=[pl.BlockSpec((tm, D), lambda i: (i, C_Q)), pl.BlockSpec((tm, 512), lambda i: (i, C_KV)), vec, vec,
                  pl.BlockSpec((tm, D), lambda i: (i, 0)), kvb, kvb, kvb, kvb],
        out_specs=[pl.BlockSpec((tm, D), lambda i: (i, 0)), pl.BlockSpec((tm, 512), lambda i: (i, 0)), vec, vec],
        out_shape=[SDS((T, D), bf16), SDS((T, 512), bf16), SDS((1, 128), f32), SDS((1, 128), f32)],
        name="qkv_bwd", sem=("arbitrary",), vmem_mb=32, comm=comm)


def _inproj_bwd(pieces, w_in, x, dx1, g_mix, comm=None):
    T = x.shape[0]
    tm = 512
    widths = [p.shape[1] for p in pieces]
    offs = [sum(widths[:i]) for i in range(len(widths))]
    assert sum(widths) == INW

    def body(*refs):
        p_refs, (w_ref, x_ref, dx1_ref, g_ref, dx_ref, dg_ref) = refs[:len(pieces)], refs[len(pieces):]

        @pl.when(pl.program_id(0) == 0)
        def _():
            dg_ref[...] = jnp.zeros_like(dg_ref)

        du = None
        for p_ref, off, wd in zip(p_refs, offs, widths):
            part = _dot_nt(p_ref[...], w_ref[:, _weight_cols(off, wd)])
            du = part if du is None else du + part
        dx, dg = _rms_bwd(x_ref[...], g_ref[...], du, dx1_ref[...])
        dx_ref[...] = dx
        dg_ref[...] += dg

    row = pl.BlockSpec((tm, D), lambda i: (i, 0))
    vec = pl.BlockSpec((1, D), lambda i: (0, 0))
    return _call(
        body, (*pieces, w_in, x, dx1, g_mix), grid=(T // tm,),
        in_specs=[pl.BlockSpec((tm, wd), lambda i: (i, 0)) for wd in widths]
        + [pl.BlockSpec((D, INW), lambda i: (0, 0), pipeline_mode=pl.Buffered(1)), row, row, vec],
        out_specs=[row, vec],
        out_shape=[SDS((T, D), f32), SDS((1, D), f32)],
        name="inproj_bwd", sem=("arbitrary",), vmem_mb=48, comm=comm)


def _forward_backward(x, tgt, w, placed, chip_core):
    def sums(names, grads, got):
        res = [_pair_sum(nm, grads[nm], got_nm, chip_core) for nm, got_nm in zip(names, got)]
        return {nm: r[0] for nm, r in zip(names, res)}, {nm: r[1] for nm, r in zip(names, res)}

    first = ["w_in", "w_dw"]
    w_in, w_dw = _run_comm(_gather_comm({nm: placed[nm] for nm in first}), "gather_first")
    gq2 = jnp.tile(w["q_norm_g"], (1, 2))
    gk2 = jnp.tile(w["k_norm_g"], (1, 2))
    def gathered_in(names):
        return names, _gather_comm({nm: placed[nm] for nm in names})

    full = {}
    perm, perm_t = _block_perm()
    names, comm = gathered_in(["w_out", "w_attn_o", "w_conv_out"])
    (proj, u, tail, qn, kk, vv), got = _rms_inproj(x, w["norm_mix_g"], w_in, perm, gq2, gk2, comm=comm)
    full.update(zip(names, got))
    bias = _bias_tiles(w["rel_bias"])
    names, comm = gathered_in(["w_ff1"])
    (o, lse), got = _attn_fwd(qn, kk, vv, bias, w["attn_sinks"], comm=comm)
    full.update(zip(names, got))
    names, comm = gathered_in(["w_ff2"])
    h1, h3, got = _glu_conv_fwd(proj, tail, w_dw, w["b_dw"], w["conv_ln_g"], w["conv_ln_b"], perm_t, comm=comm)
    full.update(zip(names, got))
    attn, conv, merged, x1, n2 = _mix_out(o, h3, proj, x, full["w_attn_o"], full["w_conv_out"], full["w_out"],
                                          w["norm_mlp_g"])
    hmid, slope, dy, dyb, loss = _mlp_fwd(n2, full["w_ff1"], full["w_ff2"], x1, tgt)

    g = {}
    df1, dx1, dx1b, g["norm_mlp_g"] = _mlp_bwd(dy, dyb, slope, full["w_ff1"], full["w_ff2"], x1, w["norm_mlp_g"])
    ff = ["w_ff1", "w_ff2"]
    gff = {"w_ff2": _wgrad(hmid, dyb, "wgrad_ff2"), "w_ff1": _wgrad(n2, df1, "wgrad_ff1")}
    (dat, dcv, do, dh1, dga, dgc, lnacc, head), got = _mix_bwd(
        dx1b, proj, attn, conv, h1, full["w_attn_o"], full["w_conv_out"], full["w_out"], w["conv_ln_g"],
        w["conv_ln_b"], perm, comm=_pair_exchange_comm(gff, ff))
    g["conv_ln_g"], g["conv_ln_b"], g["b_dw"] = lnacc[0:1], lnacc[1:2], lnacc[2:3]
    cp_ff, own_ff = sums(ff, gff, got)
    sq = ["w_out", "w_attn_o", "w_conv_out"]
    gsq = {"w_out": _wgrad(merged, dx1b, "wgrad_out"), "w_attn_o": _wgrad(o, dat, "wgrad_attn_o"),
           "w_conv_out": _wgrad(h3, dcv, "wgrad_conv_out")}
    (da, dg, g["w_dw"]), got = _conv_bwd(dh1, head, proj, tail, w_dw, perm_t, comm=_merge_comms(
        _pair_exchange_comm(gsq, sq), _chip_exchange_comm(cp_ff, ff)))
    cp_sq, own_sq = sums(sq, gsq, got[:len(sq)])
    tot_ff = {nm: _chip_sum(nm, own_ff[nm], rc_nm, chip_core) for nm, rc_nm in zip(ff, got[len(sq):])}
    (dqn, dkc, dkp, dvc, dvp, dsk, dsa), got = _attn_bwd(qn, kk, vv, bias, w["attn_sinks"], o, do, lse, comm=_merge_comms(
        _chip_exchange_comm(cp_sq, sq), _pair_share_comm(tot_ff, ff)))
    tot_sq = {nm: _chip_sum(nm, own_sq[nm], rc_nm, chip_core) for nm, rc_nm in zip(sq, got[:len(sq)])}
    shards = dict(zip(ff, got[len(sq):]))
    g["attn_sinks"] = dsk[:, 0, 0].reshape(1, NQ)
    g["rel_bias"] = _bias_bwd(dsa)[:, 0:NBUCKET].T
    (dq, dkv, ggq, ggk), got = _qkv_bwd(proj, gq2, gk2, dqn, dkc, dkp, dvc, dvp, comm=_pair_share_comm(tot_sq, sq))
    shards.update(zip(sq, got))
    g["q_norm_g"], g["k_norm_g"] = ggq[:, 0:HD], ggk[:, 0:HD]
    pieces = [dq, da, dg, dga, dgc, dkv]
    names = ["q", "a", "g", "ga", "gc", "kv"]
    gw = {nm: _wgrad(u, p, "wgrad_in_" + nm, tn=p.shape[1] if p.shape[1] < 1024 else 1024) for nm, p in zip(names, pieces)}
    gin = {"w_in": jnp.concatenate([gw["q"], gw["kv"], gw["a"], gw["g"], gw["ga"], gw["gc"]], axis=1)}
    got = _run_comm(_pair_exchange_comm(gin, ["w_in"]), "rs_pair_exchange_in")
    cp_in, own_in = sums(["w_in"], gin, got)
    (grad_x, g["norm_mix_g"]), rc = _inproj_bwd(pieces, w_in, x, dx1, w["norm_mix_g"],
                                                comm=_chip_exchange_comm(cp_in, ["w_in"]))
    tot = {"w_in": _chip_sum("w_in", own_in["w_in"], rc[0], chip_core)}
    shards["w_in"] = _run_comm(_pair_share_comm(tot, ["w_in"]), "rs_pair_share_in")[0]
    return loss[0, 0], grad_x, g, shards


BIG = ["w_in", "w_attn_o", "w_conv_out", "w_out", "w_ff1", "w_ff2"]
SHARD_AXIS = {"w_in": 1, "w_attn_o": 0, "w_conv_out": 0, "w_out": 0, "w_ff1": 1, "w_ff2": 0, "w_dw": 1}
SHARD_SHAPE = {"w_in": (D, INW // 4), "w_attn_o": (D // 4, D), "w_conv_out": (D // 4, D), "w_out": (D // 4, D),
               "w_ff1": (D, DFF // 4), "w_ff2": (DFF // 4, D), "w_dw": (HALO, D // 4)}


def _position():
    x, y, c = lax.axis_index("x"), lax.axis_index("y"), lax.axis_index("c")
    other_chips = [(1 - x, y), (x, 1 - y), (1 - x, 1 - y)]
    return x, y, c, 2 * x + y, other_chips


def _shard_window(name, full_ref, s, half=None):
    R, C = SHARD_SHAPE[name]
    r0, nr = (0, R) if half is None else (half * (R // 2), R // 2)
    if SHARD_AXIS[name] == 1:
        return full_ref.at[pl.ds(r0, nr), pl.ds(s * C, C)]
    return full_ref.at[pl.ds(s * R + r0, nr), :]


def _remote(src, dst, send_sems, recv_sems, k, device):
    return pltpu.make_async_remote_copy(src_ref=src, dst_ref=dst, send_sem=send_sems.at[k], recv_sem=recv_sems.at[k],
                                        device_id=device, device_id_type=MESH)


def _full_shape(nm):
    R, C = SHARD_SHAPE[nm]
    return (R, 4 * C) if SHARD_AXIS[nm] == 1 else (4 * R, C)


def _place_shard(nm, shard, chip_arr, dtype):
    R, C = SHARD_SHAPE[nm]
    tr = min(R, 256)
    if SHARD_AXIS[nm] == 1:
        o_map = lambda i, ch: (i, ch[0])
    else:
        o_map = lambda i, ch: (ch[0] * (R // tr) + i, 0)

    def body(ch_ref, s_ref, o_ref):
        o_ref[...] = s_ref[...].astype(dtype)

    return pl.pallas_call(
        body,
        grid_spec=pltpu.PrefetchScalarGridSpec(
            num_scalar_prefetch=1, grid=(R // tr,),
            in_specs=[pl.BlockSpec((tr, C), lambda i, ch: (i, 0))], out_specs=pl.BlockSpec((tr, C), o_map)),
        out_shape=SDS(_full_shape(nm), dtype), name="place_" + nm,
        compiler_params=_cparams(("parallel",), 32))(chip_arr, shard)


def _gather_comm(placed):
    names = list(placed)
    n = len(names)

    def copies(cout, send, recv):
        x, y, c, chip, chips = _position()
        for a, nm in enumerate(names):
            for j, (cx, cy) in enumerate(chips):
                def ici(s, a=a, nm=nm, j=j, cx=cx, cy=cy):
                    w = _shard_window(nm, cout[a], s, c)
                    return _remote(w, w, send, recv, 6 * a + j, (cx, cy, c))

                def d2d(h, a=a, nm=nm, j=j, cx=cx, cy=cy):
                    w = _shard_window(nm, cout[a], 2 * cx + cy, h)
                    return _remote(w, w, send, recv, 6 * a + 3 + j, (x, y, 1 - c))

                yield ici, d2d, chip, 2 * cx + cy, c

    def start(cin, cout, send, recv):
        for ici, d2d, chip, s, c in copies(cout, send, recv):
            ici(chip).start()

    def mid(cin, cout, send, recv):
        for ici, d2d, chip, s, c in copies(cout, send, recv):
            ici(s).wait_recv()
            d2d(c).start()

    def finish(cin, cout, send, recv):
        for ici, d2d, chip, s, c in copies(cout, send, recv):
            d2d(1 - c).wait_recv()
        for ici, d2d, chip, s, c in copies(cout, send, recv):
            ici(chip).wait_send()
            d2d(c).wait_send()

    return _Comm([placed[nm] for nm in names], [SDS(placed[nm].shape, placed[nm].dtype) for nm in names], 6 * n,
                 start, finish, mid, aliases={a: a for a in range(n)})


def _half_rows(nm):
    return SHARD_SHAPE[nm][0] // 2


RS_TILE = 128
PAIR_SUM_TILE = 512


def _exchange_comm(ins, out_shapes, copies, n_sems, aliases=None):
    def start(cin, cout, send, recv):
        for cp in copies(cin, cout, send, recv):
            cp.start()

    def finish(cin, cout, send, recv):
        for cp in copies(cin, cout, send, recv):
            cp.wait()

    return _Comm(ins, out_shapes, n_sems, start, finish, aliases=aliases)


def _pair_exchange_comm(grads, names):
    def copies(cin, cout, send, recv):
        x, y, c, chip, chips = _position()
        return [_remote(_shard_window(nm, cin[a], s, 1 - c), cout[a].at[s], send, recv, 4 * a + s, (x, y, 1 - c))
                for a, nm in enumerate(names) for s in range(4)]

    return _exchange_comm([grads[nm] for nm in names],
                          [SDS((4, _half_rows(nm), SHARD_SHAPE[nm][1]), f32) for nm in names], copies, 4 * len(names))


def _pair_sum(nm, g, got, chip_core):
    R, C = SHARD_SHAPE[nm]
    hr = R // 2
    tile = min(hr, PAIR_SUM_TILE)
    nt = hr // tile
    if SHARD_AXIS[nm] == 1:
        g_map = lambda i, s, sc: (sc[1] * nt + i, s)
    else:
        g_map = lambda i, s, sc: (s * (R // tile) + sc[1] * nt + i, 0)

    def body(sc_ref, g_ref, got_ref, o16_ref, own_ref):
        v = g_ref[...] + got_ref[0]
        o16_ref[0] = v.astype(bf16)

        @pl.when(pl.program_id(1) == sc_ref[0])
        def _():
            own_ref[...] = v

    blk3 = pl.BlockSpec((1, tile, C), lambda i, s, sc: (s, i, 0))
    return pl.pallas_call(
        body,
        grid_spec=pltpu.PrefetchScalarGridSpec(
            num_scalar_prefetch=1, grid=(nt, 4),
            in_specs=[pl.BlockSpec((tile, C), g_map), blk3],
            out_specs=[blk3, pl.BlockSpec((tile, C), lambda i, s, sc: (i, 0))]),
        out_shape=[SDS((4, hr, C), bf16), SDS((hr, C), f32)], name="rs_pair_sum_" + nm,
        compiler_params=_cparams(("parallel", "arbitrary"), 32))(chip_core, g, got)


def _chip_exchange_comm(cp, names):
    def copies(cin, cout, send, recv):
        x, y, c, chip, chips = _position()
        return [_remote(cin[a].at[2 * cx + cy], cout[a].at[j], send, recv, 3 * a + j, (cx, cy, c))
                for a, nm in enumerate(names) for j, (cx, cy) in enumerate(chips)]

    return _exchange_comm([cp[nm] for nm in names],
                          [SDS((3, _half_rows(nm), SHARD_SHAPE[nm][1]), bf16) for nm in names], copies, 3 * len(names))


def _chip_sum(nm, own, rc, chip_core):
    R, C = SHARD_SHAPE[nm]
    nt = (R // 2) // RS_TILE

    def body(sc_ref, own_ref, rc_ref, o_ref):
        o_ref[...] = own_ref[...] + rc_ref[0].astype(f32) + rc_ref[1].astype(f32) + rc_ref[2].astype(f32)

    return pl.pallas_call(
        body,
        grid_spec=pltpu.PrefetchScalarGridSpec(
            num_scalar_prefetch=1, grid=(nt,),
            in_specs=[pl.BlockSpec((RS_TILE, C), lambda i, sc: (i, 0)),
                      pl.BlockSpec((3, RS_TILE, C), lambda i, sc: (0, i, 0))],
            out_specs=pl.BlockSpec((RS_TILE, C), lambda i, sc: (sc[1] * nt + i, 0))),
        out_shape=SDS((R, C), f32), name="rs_chip_sum_" + nm,
        compiler_params=_cparams(("parallel",), 32))(chip_core, own, rc)


def _pair_share_comm(tot, names):
    def copies(cin, cout, send, recv):
        x, y, c, chip, chips = _position()
        cps = []
        for a, nm in enumerate(names):
            hr = _half_rows(nm)
            mine = cout[a].at[pl.ds(c * hr, hr), :]
            cps.append(_remote(mine, mine, send, recv, a, (x, y, 1 - c)))
        return cps

    return _exchange_comm([tot[nm] for nm in names], [SDS(SHARD_SHAPE[nm], f32) for nm in names], copies, len(names),
                          aliases={a: a for a in range(len(names))})


SMALL_ROWS = 40


def _allreduce_small(block):
    def body(x_ref, out_ref, buf, send_sems, recv_sems, local_sem):
        x, y, c, chip, chips = _position()
        me, sibling = (x, y, c), (x, y, 1 - c)

        def slot(px, py, pc):
            return buf.at[4 * px + 2 * py + pc]

        def copy(k, block_of, to, src=None):
            return _remote(slot(*block_of) if src is None else src, slot(*block_of), send_sems, recv_sems, k, to)

        mine = pltpu.make_async_copy(x_ref, slot(*me), local_sem)
        mine.start()
        first = [copy(0, me, sibling, src=x_ref)] + [copy(1 + j, me, (*ch, c), src=x_ref) for j, ch in enumerate(chips)]
        for cp in first:
            cp.start()
        passed = [copy(4 + j, (*ch, c), sibling) for j, ch in enumerate(chips)]
        for j, ch in enumerate(chips):
            copy(1 + j, (*ch, c), me).wait_recv()
            passed[j].start()
        copy(0, sibling, me).wait_recv()
        for j, ch in enumerate(chips):
            copy(4 + j, (*ch, 1 - c), me).wait_recv()
        for cp in first + passed:
            cp.wait_send()
        mine.wait()
        acc = buf[0]
        for d in range(1, 8):
            acc = acc + buf[d]
        out_ref[...] = acc

    vm = pl.BlockSpec(memory_space=pltpu.VMEM)
    return pl.pallas_call(
        body, in_specs=[vm], out_specs=vm, out_shape=SDS((SMALL_ROWS, D), f32),
        scratch_shapes=[pltpu.VMEM((8, SMALL_ROWS, D), f32), pltpu.SemaphoreType.DMA((7,)), pltpu.SemaphoreType.DMA((7,)),
                        pltpu.SemaphoreType.DMA],
        name="allreduce_small")(block)


def _adamw(w, g, m, v, name):
    rows, cols = w.shape
    tr = 256 if rows % 256 == 0 else rows

    def body(w_ref, g_ref, m_ref, v_ref, d_ref, nm_ref, nv_ref):
        gv = g_ref[...]
        m2 = ADAM_B1 * m_ref[...] + (1.0 - ADAM_B1) * gv
        v2 = ADAM_B2 * v_ref[...] + (1.0 - ADAM_B2) * jnp.square(gv)
        m_hat = m2 / (1.0 - ADAM_B1 ** ADAM_STEP)
        v_hat = v2 / (1.0 - ADAM_B2 ** ADAM_STEP)
        d_ref[...] = -ADAM_LR * (m_hat / (jnp.sqrt(v_hat) + ADAM_EPS) + ADAM_WD * w_ref[...])
        nm_ref[...] = m2
        nv_ref[...] = v2

    spec = pl.BlockSpec((tr, cols), lambda i: (i, 0))
    return pl.pallas_call(body, grid=(rows // tr,), in_specs=[spec] * 4, out_specs=[spec] * 3,
                          out_shape=[SDS((rows, cols), f32)] * 3, name=name,
                          compiler_params=_cparams(("parallel",), 40))(w, g, m, v)


WEIGHTS = ["norm_mix_g", "w_in", "q_norm_g", "k_norm_g", "attn_sinks", "rel_bias", "w_attn_o", "w_dw", "b_dw",
           "conv_ln_g", "conv_ln_b", "w_conv_out", "w_out", "norm_mlp_g", "w_ff1", "w_ff2"]
ROW_VECS = ["norm_mix_g", "b_dw", "conv_ln_g", "conv_ln_b", "norm_mlp_g"]
MISC_ROW = 5
W_DW_ROW = 8


def _pack_small(vals, loss=None):
    misc = [vals["q_norm_g"].reshape(1, HD), vals["k_norm_g"].reshape(1, HD), vals["attn_sinks"].reshape(1, NQ),
            jnp.zeros((1, 1), f32) if loss is None else loss.reshape(1, 1), jnp.zeros((1, 111), f32),
            vals["rel_bias"].reshape(1, NBUCKET * NQ), jnp.zeros((1, 256), f32)]
    rows = [vals[nm].reshape(1, D) for nm in ROW_VECS] + [jnp.concatenate(misc, axis=1), jnp.zeros((2, D), f32)]
    return jnp.concatenate(rows, axis=0)


def _unpack_small(block):
    out = {nm: block[i:i + 1] for i, nm in enumerate(ROW_VECS)}
    misc = block[MISC_ROW]
    out["q_norm_g"] = misc[0:64].reshape(1, HD)
    out["k_norm_g"] = misc[64:128].reshape(1, HD)
    out["attn_sinks"] = misc[128:144].reshape(1, NQ)
    out["rel_bias"] = misc[256:768].reshape(NBUCKET, NQ)
    return out, misc[144]


def kernel(x, norm_mix_g, w_in, q_norm_g, k_norm_g, attn_sinks, rel_bias, w_attn_o, w_dw, b_dw, conv_ln_g, conv_ln_b, w_conv_out, w_out, norm_mlp_g, w_ff1, w_ff2, loss_target, m_norm_mix_g, m_w_in, m_q_norm_g, m_k_norm_g, m_attn_sinks, m_rel_bias, m_w_attn_o, m_w_dw, m_b_dw, m_conv_ln_g, m_conv_ln_b, m_w_conv_out, m_w_out, m_norm_mlp_g, m_w_ff1, m_w_ff2, v_norm_mix_g, v_w_in, v_q_norm_g, v_k_norm_g, v_attn_sinks, v_rel_bias, v_w_attn_o, v_w_dw, v_b_dw, v_conv_ln_g, v_conv_ln_b, v_w_conv_out, v_w_out, v_norm_mlp_g, v_w_ff1, v_w_ff2):
    args = dict(locals())
    wts = {nm: args[nm] for nm in WEIGHTS}
    mom = {nm: args["m_" + nm] for nm in WEIGHTS}
    var = {nm: args["v_" + nm] for nm in WEIGHTS}
    chip = 2 * lax.axis_index("x") + lax.axis_index("y")

    chip_arr = jnp.reshape(chip, (1,)).astype(jnp.int32)
    chip_core = jnp.stack([chip, lax.axis_index("c")]).astype(jnp.int32)
    placed = {nm: _place_shard(nm, wts[nm][0], chip_arr, bf16) for nm in BIG}
    placed["w_dw"] = _place_shard("w_dw", jnp.pad(w_dw[0], ((0, 1), (0, 0))), chip_arr, f32)

    loss_part, grad_x, g, shards = _forward_backward(x[0], loss_target[0], wts, placed, chip_core)

    small = jnp.concatenate([_pack_small(g, loss_part), g["w_dw"]], axis=0)
    small = _allreduce_small(small)
    grads, loss = _unpack_small(small)
    grads["w_dw"] = lax.dynamic_slice(small[W_DW_ROW:W_DW_ROW + CW], (0, chip * (D // 4)), (CW, D // 4))
    grads.update(shards)

    delta, new_m, new_v = {}, {}, {}
    sd, sm, sv = _adamw(_pack_small(wts), small[0:8], _pack_small(mom), _pack_small(var), "adamw_small")
    for res, blk in ((delta, sd), (new_m, sm), (new_v, sv)):
        res.update(_unpack_small(blk)[0])
    for nm in BIG + ["w_dw"]:
        shp = wts[nm].shape
        two_d = lambda a: a.reshape(shp[-2], shp[-1])
        delta[nm], new_m[nm], new_v[nm] = _adamw(two_d(wts[nm]), grads[nm], two_d(mom[nm]), two_d(var[nm]), "adamw_" + nm)

    def shaped(vals):
        return [vals[nm].reshape(wts[nm].shape) for nm in WEIGHTS]

    return (loss, grad_x[None], *shaped(grads), *shaped(delta), *shaped(new_m), *shaped(new_v))
```

```python
import numpy as np
import jax
import jax.numpy as jnp
from jax import lax
from jax.experimental import pallas as pl
from jax.experimental.pallas import tpu as pltpu

f32 = jnp.float32
bf16 = jnp.bfloat16
SDS = jax.ShapeDtypeStruct
MESH = pl.DeviceIdType.MESH

D = 1024
HD = 64
NQ = 16
NKV = 4
BLK = 128
CW = 31
HALO = 32
DFF = 4096
NBUCKET = 32
EPS = 1e-6
NEG = -1e30
INW = 5632
MIX_CHUNK = 256
C_Q, C_A, C_G, C_GA, C_GC = 0, 1, 2, 3, 4
C_KV = 10

ADAM_LR = 0.001
ADAM_B1 = 0.9
ADAM_B2 = 0.999
ADAM_EPS = 1e-08
ADAM_WD = 0.01
ADAM_STEP = 10

VMEM_BYTES_V7X = 64 << 20


def _cparams(sem, vmem_mb):
    assert (vmem_mb << 20) < VMEM_BYTES_V7X
    return pltpu.CompilerParams(dimension_semantics=sem, vmem_limit_bytes=vmem_mb << 20)


ANY = pl.BlockSpec(memory_space=pl.ANY)


class _Comm:
    def __init__(self, ins, out_shapes, n_sems, start, finish, mid=None, aliases=None):
        self.ins, self.out_shapes, self.n_sems = list(ins), list(out_shapes), n_sems
        self.start, self.finish, self.mid, self.aliases = start, finish, mid, dict(aliases or {})


class _SemOffset:
    def __init__(self, sems, base):
        self._sems, self._base = sems, base
        self.at = self

    def __getitem__(self, k):
        return self._sems.at[self._base + k]


def _merge_comms(a, b):
    assert a.mid is None and b.mid is None
    n_in, n_out = len(a.ins), len(a.out_shapes)

    def both(fa, fb):
        def run(cin, cout, send, recv):
            fa(cin[:n_in], cout[:n_out], send, recv)
            fb(cin[n_in:], cout[n_out:], _SemOffset(send, a.n_sems), _SemOffset(recv, a.n_sems))
        return run

    aliases = {**a.aliases, **{n_in + k: n_out + v for k, v in b.aliases.items()}}
    return _Comm(a.ins + b.ins, a.out_shapes + b.out_shapes, a.n_sems + b.n_sems, both(a.start, b.start),
                 both(a.finish, b.finish), aliases=aliases)


def _call(body, args, *, grid, in_specs, out_specs, out_shape, name, sem, vmem_mb, scratch_shapes=(), comm=None,
          mid_step=None):
    n_in, n_out, n_scr = len(in_specs), len(out_specs), len(scratch_shapes)
    if comm is None:
        outs = pl.pallas_call(body, grid=grid, in_specs=list(in_specs), out_specs=list(out_specs),
                              out_shape=list(out_shape), scratch_shapes=list(scratch_shapes), name=name,
                              compiler_params=_cparams(sem, vmem_mb))(*args)
        return list(outs), []
    ci, co = len(comm.ins), len(comm.out_shapes)
    last = grid[0] - 1

    def wrapped(*refs):
        ins, cin = refs[:n_in], refs[n_in:n_in + ci]
        outs = refs[n_in + ci:n_in + ci + n_out]
        cout = refs[n_in + ci + n_out:n_in + ci + n_out + co]
        scr = refs[n_in + ci + n_out + co:]
        send, recv = scr[n_scr], scr[n_scr + 1]
        step = pl.program_id(0)

        @pl.when(step == 0)
        def _():
            comm.start(cin, cout, send, recv)

        body(*ins, *outs, *scr[:n_scr])
        if comm.mid is not None:
            @pl.when(step == mid_step)
            def _():
                comm.mid(cin, cout, send, recv)

        @pl.when(step == last)
        def _():
            comm.finish(cin, cout, send, recv)

    res = pl.pallas_call(
        wrapped, grid=grid, in_specs=list(in_specs) + [ANY] * ci, out_specs=list(out_specs) + [ANY] * co,
        out_shape=list(out_shape) + comm.out_shapes,
        input_output_aliases={n_in + k: n_out + v for k, v in comm.aliases.items()},
        scratch_shapes=list(scratch_shapes) + [pltpu.SemaphoreType.DMA((comm.n_sems,))] * 2,
        name=name, compiler_params=_cparams(("arbitrary",), vmem_mb))(*args, *comm.ins)
    return list(res[:n_out]), list(res[n_out:])


def _run_comm(comm, name):
    ci, co = len(comm.ins), len(comm.out_shapes)

    def body(*refs):
        cin, cout, (send, recv) = refs[:ci], refs[ci:ci + co], refs[ci + co:]
        comm.start(cin, cout, send, recv)
        if comm.mid is not None:
            comm.mid(cin, cout, send, recv)
        comm.finish(cin, cout, send, recv)

    return pl.pallas_call(
        body, in_specs=[ANY] * ci, out_specs=[ANY] * co, out_shape=comm.out_shapes, input_output_aliases=comm.aliases,
        scratch_shapes=[pltpu.SemaphoreType.DMA((comm.n_sems,))] * 2, name=name)(*comm.ins)


def _dot(a, b):
    return jnp.dot(a, b, preferred_element_type=f32)


def _dot_nt(a, b):
    return lax.dot_general(a, b, (((1,), (1,)), ((), ())), preferred_element_type=f32)


def _dot_tn(a, b):
    return lax.dot_general(a, b, (((0,), (0,)), ((), ())), preferred_element_type=f32)


def _sigmoid(x):
    return 0.5 * jnp.tanh(0.5 * x) + 0.5


def _low_head_lanes():
    return lax.broadcasted_iota(jnp.int32, (1, 2 * HD), 1) < HD


def _head_blockdiag():
    r = lax.broadcasted_iota(jnp.int32, (2 * HD, 2 * HD), 0) // HD
    c = lax.broadcasted_iota(jnp.int32, (2 * HD, 2 * HD), 1) // HD
    return jnp.where(r == c, 1.0, 0.0).astype(bf16)


def _head_sums(z, bd):
    hi = z.astype(bf16)
    lo = (z - hi.astype(f32)).astype(bf16)
    return _dot(hi, bd) + _dot(lo, bd)


def _weight_cols(start, width):
    kv_width = 2 * NKV * HD
    if start < D:
        orig = start
    elif start < INW - kv_width:
        orig = start + kv_width
    else:
        orig = start - (INW - kv_width) + D
    assert (start < D) == (start + width <= D) and (start < INW - kv_width) == (start + width <= INW - kv_width)
    return slice(orig, orig + width)


def _rms_inproj(x, g, w, perm, gq2, gk2, comm=None):
    T, N = x.shape[0], w.shape[1]
    tn = 512
    conv_cols = (C_A * D, (C_G + 1) * D)
    attn_chunks = [C_Q * D // tn, C_Q * D // tn + 1, C_KV]

    def body(x_ref, g_ref, w_ref, perm_ref, gq_ref, gk_ref, p_ref, u_ref, tail_ref, qn_ref, kk_ref, vv_ref):
        xv = x_ref[...]
        r = lax.rsqrt(jnp.mean(xv * xv, axis=-1, keepdims=True) + EPS)
        u = (xv * r * g_ref[...]).astype(bf16)
        u_ref[...] = u
        u_blocks = _dot(perm_ref[...], u).astype(bf16)

        def project(c):
            lhs = u_blocks if conv_cols[0] <= c * tn < conv_cols[1] else u
            p_ref[:, c * tn:(c + 1) * tn] = _dot(lhs, w_ref[:, _weight_cols(c * tn, tn)])

        for c in attn_chunks:
            project(c)
        bd = _head_blockdiag()
        lo = _low_head_lanes()
        for p in range(NQ // 2):
            z = p_ref[:, C_Q * D + 128 * p:C_Q * D + 128 * p + 128]
            rq = lax.rsqrt(_head_sums(z * z, bd) * (1.0 / HD) + EPS)
            qn_ref[:, 128 * p:128 * p + 128] = (z * rq * gq_ref[...] * (HD ** -0.5)).astype(bf16)
        kv0 = C_KV * tn
        for p in range(NKV // 2):
            z = p_ref[:, kv0 + 128 * p:kv0 + 128 * p + 128]
            rk = lax.rsqrt(_head_sums(z * z, bd) * (1.0 / HD) + EPS)
            _split_pair(z * rk * gk_ref[...], kk_ref, p, lo)
            _split_pair(p_ref[:, kv0 + 256 + 128 * p:kv0 + 256 + 128 * p + 128], vv_ref, p, lo)
        for c in range(N // tn):
            if c not in attn_chunks:
                project(c)
        for k in range(NBLK):
            tail_ref[k:k + 1, :] = p_ref[RPB * k + RPB - 1:RPB * (k + 1), conv_cols[0]:conv_cols[1]]

    once = pl.Buffered(1)
    row = pl.BlockSpec((TT, D), lambda i: (i, 0))
    vec = pl.BlockSpec((1, 128), lambda i: (0, 0))
    return _call(
        body, (x, g, w, perm, gq2, gk2), grid=(T // TT,),
        in_specs=[row, pl.BlockSpec((1, D), lambda i: (0, 0)),
                  pl.BlockSpec((D, N), lambda i: (0, 0), pipeline_mode=once),
                  pl.BlockSpec((TT, TT), lambda i: (0, 0), pipeline_mode=once), vec, vec],
        out_specs=[pl.BlockSpec((TT, N), lambda i: (i, 0)), row, pl.BlockSpec((NBLK, 2 * D), lambda i: (i, 0)),
                   row, row, row],
        out_shape=[SDS((T, N), f32), SDS((T, D), bf16), SDS((T // TT * NBLK, 2 * D), f32)] + [SDS((T, D), bf16)] * 3,
        name="rms_inproj", sem=("parallel",), vmem_mb=56, comm=comm, mid_step=(3 * (T // TT)) // 4)


def _split_pair(pair, out_ref, p, lo):
    rolled = pltpu.roll(pair, HD, 1)
    zero = jnp.zeros_like(pair)
    c = 512 * p
    out_ref[:, c:c + 128] = jnp.where(lo, pair, zero).astype(bf16)
    out_ref[:, c + 128:c + 256] = jnp.where(lo, zero, rolled).astype(bf16)
    out_ref[:, c + 256:c + 384] = jnp.where(lo, rolled, zero).astype(bf16)
    out_ref[:, c + 384:c + 512] = jnp.where(lo, zero, pair).astype(bf16)


def _bucket_tile():
    qi = np.arange(BLK)[:, None]
    kj = np.arange(BLK)[None, :]
    n = np.where(kj > qi, qi + BLK - kj, qi - kj)
    max_exact = NBUCKET // 2
    nf = np.maximum(n, 1).astype(np.float32)
    large = max_exact + (np.log(nf / max_exact) / np.float32(np.log(128 / max_exact))
                         * (NBUCKET - max_exact)).astype(np.int32)
    large = np.minimum(large, NBUCKET - 1)
    return np.where(n < max_exact, n, large).astype(np.int32)


def _from_prev_block():
    return lax.broadcasted_iota(jnp.int32, (BLK, BLK), 1) > lax.broadcasted_iota(jnp.int32, (BLK, BLK), 0)


def _bias_tiles(rel_bias):
    def body(rb_ref, bk_ref, out_ref):
        bk = bk_ref[...]
        for h in range(NQ):
            acc = jnp.zeros((BLK, BLK), f32)
            for b in range(NBUCKET):
                acc = jnp.where(bk == b, rb_ref[b, h], acc)
            out_ref[h] = acc

    return pl.pallas_call(
        body,
        in_specs=[pl.BlockSpec(memory_space=pltpu.SMEM), pl.BlockSpec(memory_space=pltpu.VMEM)],
        out_specs=pl.BlockSpec(memory_space=pltpu.VMEM),
        out_shape=SDS((NQ, BLK, BLK), f32),
        name="bias_tiles")(rel_bias, jnp.asarray(_bucket_tile()))


def _rows2(ref, c):
    return jnp.concatenate([ref[:, c:c + 128], ref[:, c + 128:c + 256]], axis=0)


def _attn_fwd(qn, kk, vv, bias, sinks, comm=None):
    T = qn.shape[0]
    nb = T // BLK

    def body(s_ref, q_ref, kc_ref, kp_ref, vc_ref, vp_ref, b_ref, o_ref, lse_ref):
        prev = _from_prev_block()
        no_key = jnp.logical_and(prev, pl.program_id(0) == 0)
        scores = []
        for h in range(NKV):
            qs = _rows2(q_ref, 256 * h)
            scores.append((_dot_nt(qs, _rows2(kc_ref, 256 * h)), _dot_nt(qs, _rows2(kp_ref, 256 * h))))
        for h in range(NKV):
            c = 256 * h
            sc, sp = scores[h]
            vstack = jnp.concatenate([vp_ref[:, c:c + 128], vc_ref[:, c:c + 128],
                                      vp_ref[:, c + 128:c + 256], vc_ref[:, c + 128:c + 256]], axis=0)
            for pr in range(2):
                ps = []
                for e in range(2):
                    hq = 4 * h + 2 * pr + e
                    rows, cols = slice(128 * pr, 128 * pr + 128), slice(128 * e, 128 * e + 128)
                    s = jnp.where(no_key, NEG, jnp.where(prev, sp[rows, cols], sc[rows, cols]) + b_ref[hq])
                    sink = s_ref[0, hq]
                    m = jnp.maximum(jnp.max(s, axis=-1, keepdims=True), sink)
                    ex = jnp.exp(s - m)
                    l = jnp.sum(ex, axis=-1, keepdims=True) + jnp.exp(sink - m)
                    p = ex * (1.0 / l)
                    ps += [jnp.where(prev, p, 0.0).astype(bf16), jnp.where(prev, 0.0, p).astype(bf16)]
                    lse_ref[:, hq:hq + 1] = m + jnp.log(l)
                o_ref[:, c + 128 * pr:c + 128 * pr + 128] = _dot(jnp.concatenate(ps, axis=1), vstack).astype(bf16)

    blk = lambda f: pl.BlockSpec((BLK, D), f)
    cur = lambda n: (n, 0)
    prev = lambda n: (jnp.maximum(n - 1, 0), 0)
    return _call(
        body, (sinks, qn, kk, kk, vv, vv, bias), grid=(nb,),
        in_specs=[pl.BlockSpec(memory_space=pltpu.SMEM), blk(cur), blk(cur), blk(prev), blk(cur), blk(prev),
                  pl.BlockSpec((NQ, BLK, BLK), lambda n: (0, 0, 0))],
        out_specs=[blk(cur), pl.BlockSpec((BLK, NQ), cur)],
        out_shape=[SDS((T, D), bf16), SDS((T, NQ), f32)],
        name="attn_fwd", sem=("parallel",), vmem_mb=32, comm=comm, mid_step=(3 * nb) // 4)


TT = 512
NBLK = 32
RPB = TT // NBLK
CONV_LANES = 128
KBLK = 4
GBLK = 8
TAPG = 8


def _block_perm():
    p = np.arange(TT)
    m = np.zeros((TT, TT), np.float32)
    m[p, NBLK * (p % RPB) + p // RPB] = 1.0
    return jnp.asarray(m, bf16), jnp.asarray(m.T, bf16)


def _lane_groups():
    return [slice(q * CONV_LANES, (q + 1) * CONV_LANES) for q in range(D // CONV_LANES)]


def _fill_time_blocks(z, tile, edge, causal):
    row = lax.broadcasted_iota(jnp.int32, (RPB, 1), 0)
    for k in range(NBLK):
        blk = tile[RPB * k:RPB * (k + 1)]
        if causal:
            z[NBLK + k] = blk
            z[k] = jnp.where(row == 0, edge[k:k + 1], pltpu.roll(blk, 1, 0))
        else:
            z[k] = blk
            z[NBLK + k] = jnp.where(row == RPB - 1, edge[k:k + 1], pltpu.roll(blk, RPB - 1, 0))


def _block_conv(z, w_ref, tap_offset, init, store):
    def step(s, carry):
        k0 = s * KBLK
        for ln in _lane_groups():
            accs = [init(ln) for _ in range(KBLK)]
            for g0 in range(0, CW, TAPG):
                taps = range(g0, min(g0 + TAPG, CW))
                lo = min(tap_offset(j) for j in taps)
                hi = max(tap_offset(j) for j in taps)
                win = [z[k0 + lo + d, :, ln] for d in range(KBLK + hi - lo)]
                for j in taps:
                    wv = w_ref[j:j + 1, ln]
                    for q in range(KBLK):
                        accs[q] = accs[q] + win[q + tap_offset(j) - lo] * wv
            for q in range(KBLK):
                store(k0 + q, ln, accs[q])
        return carry

    lax.fori_loop(0, NBLK // KBLK, step, 0)


def _block_rows(k):
    return pl.ds(pl.multiple_of(k * RPB, RPB), RPB)


def _glu_conv_fwd(proj, tail, w_dw, b_dw, ln_g, ln_b, perm_t, comm=None):
    T = proj.shape[0]

    def body(a_ref, g_ref, ta_ref, tg_ref, w_ref, b_ref, lg_ref, lb_ref, pt_ref, h1_ref, h3_ref, z):
        edge = jnp.where(pl.program_id(0) > 0, ta_ref[...] * _sigmoid(tg_ref[...]), 0.0)
        _fill_time_blocks(z, a_ref[...] * _sigmoid(g_ref[...]), edge, causal=True)

        def store(k, ln, value):
            h1_ref[_block_rows(k), ln] = value

        _block_conv(z, w_ref, lambda j: NBLK - (CW - 1) + j,
                    lambda ln: jnp.broadcast_to(b_ref[:, ln], (RPB, CONV_LANES)), store)
        h1 = h1_ref[...]
        mu = jnp.mean(h1, axis=-1, keepdims=True)
        xc = h1 - mu
        var = jnp.mean(xc * xc, axis=-1, keepdims=True)
        h2 = xc * lax.rsqrt(var + EPS) * lg_ref[...] + lb_ref[...]
        h3_ref[...] = _dot(pt_ref[...], (h2 * _sigmoid(h2)).astype(bf16)).astype(bf16)

    tile = lambda cb: pl.BlockSpec((TT, D), lambda i: (i, cb))
    edge = lambda cb: pl.BlockSpec((NBLK, D), lambda i: (jnp.maximum(i - 1, 0), cb))
    vec = pl.BlockSpec((1, D), lambda i: (0, 0))
    (h1, h3), got = _call(
        body, (proj, proj, tail, tail, w_dw, b_dw, ln_g, ln_b, perm_t), grid=(T // TT,),
        in_specs=[tile(C_A), tile(C_G), edge(0), edge(1), pl.BlockSpec((HALO, D), lambda i: (0, 0)), vec, vec, vec,
                  pl.BlockSpec((TT, TT), lambda i: (0, 0), pipeline_mode=pl.Buffered(1))],
        out_specs=[pl.BlockSpec((TT, D), lambda i: (i, 0))] * 2,
        out_shape=[SDS((T, D), f32), SDS((T, D), bf16)],
        scratch_shapes=[pltpu.VMEM((2 * NBLK, RPB, D), f32)],
        name="glu_conv_fwd", sem=("parallel",), vmem_mb=40, comm=comm, mid_step=(3 * (T // TT)) // 4)
    return h1, h3, got


def _mix_out(o, h3, proj, x, w_attn_o, w_conv_out, w_out, g_mlp):
    T = x.shape[0]
    tm = 512

    def body(o_ref, h3_ref, ga_ref, gc_ref, x_ref, wa_ref, wc_ref, wo_ref, g_ref,
             attn_ref, conv_ref, mg_ref, x1_ref, n2_ref):
        x1 = x_ref[...]
        for j in range(D // MIX_CHUNK):
            cols = slice(j * MIX_CHUNK, (j + 1) * MIX_CHUNK)
            attn = _dot(o_ref[...], wa_ref[:, cols])
            conv = _dot(h3_ref[...], wc_ref[:, cols])
            attn_ref[:, cols] = attn.astype(bf16)
            conv_ref[:, cols] = conv.astype(bf16)
            mg = (_sigmoid(ga_ref[:, cols]) * attn + _sigmoid(gc_ref[:, cols]) * conv).astype(bf16)
            mg_ref[:, cols] = mg
            x1 = x1 + _dot(mg, wo_ref[cols, :])
        x1_ref[...] = x1
        r = lax.rsqrt(jnp.mean(x1 * x1, axis=-1, keepdims=True) + EPS)
        n2_ref[...] = (x1 * r * g_ref[...]).astype(bf16)

    tile = lambda cb=0: pl.BlockSpec((tm, D), lambda i: (i, cb))
    wfull = pl.BlockSpec((D, D), lambda i: (0, 0), pipeline_mode=pl.Buffered(1))
    return pl.pallas_call(
        body, grid=(T // tm,),
        in_specs=[tile(), tile(), tile(C_GA), tile(C_GC), tile(), wfull, wfull, wfull,
                  pl.BlockSpec((1, D), lambda i: (0, 0))],
        out_specs=[tile()] * 5,
        out_shape=[SDS((T, D), bf16), SDS((T, D), bf16), SDS((T, D), bf16), SDS((T, D), f32), SDS((T, D), bf16)],
        name="mix_out", compiler_params=_cparams(("parallel",), 48))(o, h3, proj, proj, x, w_attn_o, w_conv_out, w_out, g_mlp)


def _mlp_fwd(n2, w1, w2, x1, tgt):
    T = n2.shape[0]
    tm, tf = 512, 1024

    def body(n2_ref, w1_ref, w2_ref, x1_ref, t_ref, hm_ref, slope_ref, dy_ref, dyb_ref, loss_ref):
        @pl.when(pl.program_id(0) == 0)
        def _():
            loss_ref[...] = jnp.zeros_like(loss_ref)

        n2v = n2_ref[...]
        for c in range(DFF // tf):
            r = jnp.maximum(_dot(n2v, w1_ref[:, c * tf:(c + 1) * tf]), 0.0)
            hm_ref[:, c * tf:(c + 1) * tf] = (r * r).astype(bf16)
            slope_ref[:, c * tf:(c + 1) * tf] = (2.0 * r).astype(bf16)
        e = x1_ref[...] + _dot(hm_ref[...], w2_ref[...]) - t_ref[...]
        dy = e * (1.0 / D)
        dy_ref[...] = dy
        dyb_ref[...] = dy.astype(bf16)
        loss_ref[...] += 0.5 * jnp.sum(jnp.sum(e * e, axis=-1, keepdims=True) * (1.0 / D))

    row = pl.BlockSpec((tm, D), lambda i: (i, 0))
    once = pl.Buffered(1)
    return pl.pallas_call(
        body, grid=(T // tm,),
        in_specs=[row, pl.BlockSpec((D, DFF), lambda i: (0, 0), pipeline_mode=once),
                  pl.BlockSpec((DFF, D), lambda i: (0, 0), pipeline_mode=once), row, row],
        out_specs=[pl.BlockSpec((tm, DFF), lambda i: (i, 0)), pl.BlockSpec((tm, DFF), lambda i: (i, 0)), row, row,
                   pl.BlockSpec((8, 128), lambda i: (0, 0))],
        out_shape=[SDS((T, DFF), bf16), SDS((T, DFF), bf16), SDS((T, D), f32), SDS((T, D), bf16), SDS((8, 128), f32)],
        name="mlp_fwd", compiler_params=_cparams(("arbitrary",), 60))(n2, w1, w2, x1, tgt)


def _rms_bwd(xv, g, dn, dres):
    r = lax.rsqrt(jnp.mean(xv * xv, axis=-1, keepdims=True) + EPS)
    gd = dn * g
    dx = dres + r * gd - xv * (r * r * r) * jnp.mean(xv * gd, axis=-1, keepdims=True)
    dg = jnp.sum(dn * xv * r, axis=0, keepdims=True)
    return dx, dg


def _mlp_bwd(dy, dyb, slope, w1, w2, x1, g_mlp):
    T = dy.shape[0]
    tm, tf = 512, 1024

    def body(dy_ref, dyb_ref, slope_ref, w1_ref, w2_ref, x1_ref, g_ref, df_ref, dx_ref, dxb_ref, dg_ref):
        @pl.when(pl.program_id(0) == 0)
        def _():
            dg_ref[...] = jnp.zeros_like(dg_ref)

        dyb = dyb_ref[...]
        for c in range(DFF // tf):
            cols = slice(c * tf, (c + 1) * tf)
            d_hm = _dot_nt(dyb, w2_ref[cols, :])
            df_ref[:, cols] = (d_hm * slope_ref[:, cols].astype(f32)).astype(bf16)
        dn = _dot_nt(df_ref[...], w1_ref[...])
        dx, dg = _rms_bwd(x1_ref[...], g_ref[...], dn, dy_ref[...])
        dx_ref[...] = dx
        dxb_ref[...] = dx.astype(bf16)
        dg_ref[...] += dg

    row = pl.BlockSpec((tm, D), lambda i: (i, 0))
    wide = pl.BlockSpec((tm, DFF), lambda i: (i, 0))
    vec = pl.BlockSpec((1, D), lambda i: (0, 0))
    once = pl.Buffered(1)
    return pl.pallas_call(
        body, grid=(T // tm,),
        in_specs=[row, row, wide, pl.BlockSpec((D, DFF), lambda i: (0, 0), pipeline_mode=once),
                  pl.BlockSpec((DFF, D), lambda i: (0, 0), pipeline_mode=once), row, vec],
        out_specs=[wide, row, row, vec],
        out_shape=[SDS((T, DFF), bf16), SDS((T, D), f32), SDS((T, D), bf16), SDS((1, D), f32)],
        name="mlp_bwd", compiler_params=_cparams(("arbitrary",), 56))(dy, dyb, slope, w1, w2, x1, g_mlp)


def _wgrad(a, b, name, tn=1024):
    T, M = a.shape
    N = b.shape[1]
    tmm, tk = min(M, 1024), min(T, 2048)

    def body(a_ref, b_ref, o_ref):
        @pl.when(pl.program_id(2) == 0)
        def _():
            o_ref[...] = jnp.zeros_like(o_ref)

        o_ref[...] += _dot_tn(a_ref[...], b_ref[...])

    return pl.pallas_call(
        body, grid=(M // tmm, N // tn, T // tk),
        in_specs=[pl.BlockSpec((tk, tmm), lambda m, n, t: (t, m)), pl.BlockSpec((tk, tn), lambda m, n, t: (t, n))],
        out_specs=pl.BlockSpec((tmm, tn), lambda m, n, t: (m, n)),
        out_shape=SDS((M, N), f32),
        name=name, compiler_params=_cparams(("parallel", "parallel", "arbitrary"), 40))(a, b)


def _mix_bwd(dx1b, proj, attn, conv, h1, w_attn_o, w_conv_out, w_out, ln_g, ln_b, perm, comm=None):
    T = dx1b.shape[0]
    tm = TT

    def body(dx_ref, ga_ref, gc_ref, attn_ref, conv_ref, h1_ref, wa_ref, wc_ref, wo_ref, lg_ref, lb_ref, perm_ref,
             dat_ref, dcv_ref, do_ref, dh1_ref, dga_ref, dgc_ref, acc_ref, head_ref):
        @pl.when(pl.program_id(0) == 0)
        def _():
            acc_ref[...] = jnp.zeros_like(acc_ref)

        d_o, d_h3 = None, None
        for j in range(D // MIX_CHUNK):
            cols = slice(j * MIX_CHUNK, (j + 1) * MIX_CHUNK)
            dm = _dot_nt(dx_ref[...], wo_ref[cols, :])
            sa = _sigmoid(ga_ref[:, cols])
            sc = _sigmoid(gc_ref[:, cols])
            dat = (dm * sa).astype(bf16)
            dcv = (dm * sc).astype(bf16)
            dat_ref[:, cols] = dat
            dcv_ref[:, cols] = dcv
            dga_ref[:, cols] = (dm * attn_ref[:, cols].astype(f32) * sa * (1.0 - sa)).astype(bf16)
            dgc_ref[:, cols] = (dm * conv_ref[:, cols].astype(f32) * sc * (1.0 - sc)).astype(bf16)
            part_o = _dot_nt(dat, wa_ref[:, cols])
            part_h = _dot_nt(_dot(perm_ref[...], dcv).astype(bf16), wc_ref[:, cols])
            d_o = part_o if d_o is None else d_o + part_o
            d_h3 = part_h if d_h3 is None else d_h3 + part_h
        do_ref[...] = d_o.astype(bf16)
        h1 = h1_ref[...]
        mu = jnp.mean(h1, axis=-1, keepdims=True)
        xc = h1 - mu
        rstd = lax.rsqrt(jnp.mean(xc * xc, axis=-1, keepdims=True) + EPS)
        xh = xc * rstd
        h2 = xh * lg_ref[...] + lb_ref[...]
        sg = _sigmoid(h2)
        dh2 = d_h3 * (sg * (1.0 + h2 * (1.0 - sg)))
        dxh = dh2 * lg_ref[...]
        dh1 = rstd * (dxh - jnp.mean(dxh, axis=-1, keepdims=True) - xh * jnp.mean(dxh * xh, axis=-1, keepdims=True))
        dh1_ref[...] = dh1
        for k in range(NBLK):
            head_ref[k:k + 1, :] = dh1[RPB * k:RPB * k + 1]
        acc_ref[0:1, :] += jnp.sum(dh2 * xh, axis=0, keepdims=True)
        acc_ref[1:2, :] += jnp.sum(dh2, axis=0, keepdims=True)
        acc_ref[2:3, :] += jnp.sum(dh1, axis=0, keepdims=True)

    tile = lambda cb=0: pl.BlockSpec((tm, D), lambda i: (i, cb))
    wfull = pl.BlockSpec((D, D), lambda i: (0, 0), pipeline_mode=pl.Buffered(1))
    vec = pl.BlockSpec((1, D), lambda i: (0, 0))
    return _call(
        body, (dx1b, proj, proj, attn, conv, h1, w_attn_o, w_conv_out, w_out, ln_g, ln_b, perm), grid=(T // tm,),
        in_specs=[tile(), tile(C_GA), tile(C_GC), tile(), tile(), tile(), wfull, wfull, wfull, vec, vec,
                  pl.BlockSpec((TT, TT), lambda i: (0, 0), pipeline_mode=pl.Buffered(1))],
        out_specs=[tile()] * 6 + [pl.BlockSpec((8, D), lambda i: (0, 0)), pl.BlockSpec((NBLK, D), lambda i: (i, 0))],
        out_shape=[SDS((T, D), bf16), SDS((T, D), bf16), SDS((T, D), bf16), SDS((T, D), f32),
                   SDS((T, D), bf16), SDS((T, D), bf16), SDS((8, D), f32), SDS((T // RPB, D), f32)],
        name="mix_bwd", sem=("arbitrary",), vmem_mb=56, comm=comm)


def _conv_bwd(dh1, head, proj, tail, w_dw, perm_t, comm=None):
    T = dh1.shape[0]
    nt = T // TT

    def body(d_ref, hd_ref, a_ref, g_ref, ta_ref, tg_ref, w_ref, pt_ref, da_ref, dg_ref, gw_ref, zd, zh, dh0, gacc):
        i = pl.program_id(0)

        @pl.when(i == 0)
        def _():
            gacc[...] = jnp.zeros_like(gacc)

        a = a_ref[...]
        sg = _sigmoid(g_ref[...])
        _fill_time_blocks(zd, d_ref[...], jnp.where(i < nt - 1, hd_ref[...], 0.0), causal=False)
        _fill_time_blocks(zh, a * sg, jnp.where(i > 0, ta_ref[...] * _sigmoid(tg_ref[...]), 0.0), causal=True)

        def store(k, ln, value):
            dh0[_block_rows(k), ln] = value

        _block_conv(zd, w_ref, lambda j: (CW - 1) - j, lambda ln: jnp.zeros((RPB, CONV_LANES), f32), store)

        for ln in _lane_groups():
            for g0 in range(0, CW, TAPG):
                taps = list(range(g0, min(g0 + TAPG, CW)))

                def add_blocks(s, accs, ln=ln, taps=taps):
                    k0 = s * GBLK
                    first = k0 + NBLK - (CW - 1) + taps[0]
                    win = [zh[first + t, :, ln] for t in range(GBLK + len(taps) - 1)]
                    accs = list(accs)
                    for q in range(GBLK):
                        d = zd[k0 + q, :, ln]
                        for n, j in enumerate(taps):
                            accs[n] = accs[n] + d * win[q + j - taps[0]]
                    return tuple(accs)

                accs = lax.fori_loop(0, NBLK // GBLK, add_blocks,
                                     tuple(jnp.zeros((RPB, CONV_LANES), f32) for _ in taps))
                for j, acc in zip(taps, accs):
                    gacc[j, :, ln] += acc

        d0 = dh0[...]
        da_ref[...] = _dot(pt_ref[...], (d0 * sg).astype(bf16)).astype(bf16)
        dg_ref[...] = _dot(pt_ref[...], (d0 * a * sg * (1.0 - sg)).astype(bf16)).astype(bf16)

        @pl.when(i == nt - 1)
        def _():
            gw_ref[...] = jnp.zeros_like(gw_ref)
            for j in range(CW):
                gw_ref[j:j + 1, :] = jnp.sum(gacc[j], axis=0, keepdims=True)

    tile = lambda cb=0: pl.BlockSpec((TT, D), lambda i: (i, cb))
    prev_edge = lambda cb: pl.BlockSpec((NBLK, D), lambda i: (jnp.maximum(i - 1, 0), cb))
    next_edge = pl.BlockSpec((NBLK, D), lambda i: (jnp.minimum(i + 1, nt - 1), 0))
    wspec = pl.BlockSpec((HALO, D), lambda i: (0, 0))
    return _call(
        body, (dh1, head, proj, proj, tail, tail, w_dw, perm_t), grid=(nt,),
        in_specs=[tile(), next_edge, tile(C_A), tile(C_G), prev_edge(0), prev_edge(1), wspec,
                  pl.BlockSpec((TT, TT), lambda i: (0, 0), pipeline_mode=pl.Buffered(1))],
        out_specs=[tile(), tile(), wspec],
        out_shape=[SDS((T, D), bf16), SDS((T, D), bf16), SDS((HALO, D), f32)],
        scratch_shapes=[pltpu.VMEM((2 * NBLK, RPB, D), f32), pltpu.VMEM((2 * NBLK, RPB, D), f32),
                        pltpu.VMEM((TT, D), f32), pltpu.VMEM((HALO, RPB, D), f32)],
        name="conv_bwd", sem=("arbitrary",), vmem_mb=48, comm=comm)


def _attn_bwd(qn, kk, vv, bias, sinks, o, do, lse, comm=None):
    T = qn.shape[0]
    nb = T // BLK

    def body(s_ref, q_ref, kc_ref, kp_ref, vc_ref, vp_ref, b_ref, o_ref, do_ref, lse_ref,
             dq_ref, dkc_ref, dkp_ref, dvc_ref, dvp_ref, dsk_ref, dsa_ref):
        n = pl.program_id(0)

        @pl.when(n == 0)
        def _():
            dsk_ref[...] = jnp.zeros_like(dsk_ref)
            dsa_ref[...] = jnp.zeros_like(dsa_ref)

        @pl.when(n == nb)
        def _():
            dkp_ref[...] = jnp.zeros_like(dkp_ref)
            dvp_ref[...] = jnp.zeros_like(dvp_ref)

        @pl.when(n < nb)
        def _():
            from_prev = _from_prev_block()
            no_key = jnp.logical_and(from_prev, n == 0)
            lo = _low_head_lanes()
            dups = {"kc": [], "kp": [], "vc": [], "vp": []}
            products = []
            for h in range(NKV):
                qs = _rows2(q_ref, 256 * h)
                dos = _rows2(do_ref, 256 * h)
                products.append((qs, dos, _dot_nt(qs, _rows2(kc_ref, 256 * h)), _dot_nt(qs, _rows2(kp_ref, 256 * h)),
                                 _dot_nt(dos, _rows2(vc_ref, 256 * h)), _dot_nt(dos, _rows2(vp_ref, 256 * h))))
            for h in range(NKV):
                c = 256 * h
                qs, dos, sc, sp, dpc, dpp = products[h]
                kstack = jnp.concatenate([kp_ref[:, c:c + 128], kc_ref[:, c:c + 128],
                                          kp_ref[:, c + 128:c + 256], kc_ref[:, c + 128:c + 256]], axis=0)
                p_c, p_p, ds_c, ds_p = [], [], [], []
                for pr in range(2):
                    cc = c + 128 * pr
                    prod = do_ref[:, cc:cc + 128].astype(f32) * o_ref[:, cc:cc + 128].astype(f32)
                    d_lo = jnp.sum(jnp.where(lo, prod, 0.0), axis=-1, keepdims=True)
                    d_hi = jnp.sum(prod, axis=-1, keepdims=True) - d_lo
                    row_pc, row_pp, row_dc, row_dp = [], [], [], []
                    for e in range(2):
                        hq = 4 * h + 2 * pr + e
                        rows, cols = slice(128 * pr, 128 * pr + 128), slice(128 * e, 128 * e + 128)
                        delta = d_lo if e == 0 else d_hi
                        lse = lse_ref[:, hq:hq + 1]
                        s = jnp.where(from_prev, sp[rows, cols], sc[rows, cols]) + b_ref[hq]
                        p = jnp.where(no_key, 0.0, jnp.exp(s - lse))
                        ds = p * (jnp.where(from_prev, dpp[rows, cols], dpc[rows, cols]) - delta)
                        dsa_ref[hq] += ds
                        dsk_ref[hq] += jnp.broadcast_to(-jnp.sum(jnp.exp(s_ref[0, hq] - lse) * delta), (8, 128))
                        row_pc.append(jnp.where(from_prev, 0.0, p).astype(bf16))
                        row_pp.append(jnp.where(from_prev, p, 0.0).astype(bf16))
                        row_dc.append(jnp.where(from_prev, 0.0, ds).astype(bf16))
                        row_dp.append(jnp.where(from_prev, ds, 0.0).astype(bf16))
                    dq_ref[:, cc:cc + 128] = _dot(jnp.concatenate([row_dp[0], row_dc[0], row_dp[1], row_dc[1]], axis=1),
                                                  kstack).astype(bf16)
                    p_c.append(jnp.concatenate(row_pc, axis=1))
                    p_p.append(jnp.concatenate(row_pp, axis=1))
                    ds_c.append(jnp.concatenate(row_dc, axis=1))
                    ds_p.append(jnp.concatenate(row_dp, axis=1))

                def to_keys(m2, rhs):
                    x2 = _dot_tn(jnp.concatenate(m2, axis=0), rhs)
                    x = jnp.where(lo, x2[0:128], x2[128:256])
                    return x + pltpu.roll(x, HD, 1)

                dups["kc"].append(to_keys(ds_c, qs))
                dups["kp"].append(to_keys(ds_p, qs))
                dups["vc"].append(to_keys(p_c, dos))
                dups["vp"].append(to_keys(p_p, dos))
            for key, ref in (("kc", dkc_ref), ("kp", dkp_ref), ("vc", dvc_ref), ("vp", dvp_ref)):
                d = dups[key]
                ref[:, 0:128] = jnp.where(lo, d[0], d[1]).astype(bf16)
                ref[:, 128:256] = jnp.where(lo, d[2], d[3]).astype(bf16)

    clamp = lambda n: jnp.minimum(n, nb - 1)
    blk = lambda f: pl.BlockSpec((BLK, D), f)
    cur = lambda n: (clamp(n), 0)
    prev = lambda n: (jnp.maximum(clamp(n) - 1, 0), 0)
    back = lambda n: (jnp.maximum(n - 1, 0), 0)
    kvb = lambda f: pl.BlockSpec((BLK, NKV * HD), f)
    return _call(
        body, (sinks, qn, kk, kk, vv, vv, bias, o, do, lse), grid=(nb + 1,),
        in_specs=[pl.BlockSpec(memory_space=pltpu.SMEM), blk(cur), blk(cur), blk(prev), blk(cur), blk(prev),
                  pl.BlockSpec((NQ, BLK, BLK), lambda n: (0, 0, 0)), blk(cur), blk(cur),
                  pl.BlockSpec((BLK, NQ), cur)],
        out_specs=[blk(cur), kvb(cur), kvb(back), kvb(cur), kvb(back),
                   pl.BlockSpec((NQ, 8, 128), lambda n: (0, 0, 0)),
                   pl.BlockSpec((NQ, BLK, BLK), lambda n: (0, 0, 0))],
        out_shape=[SDS((T, D), bf16)] + [SDS((T, NKV * HD), bf16)] * 4 + [SDS((NQ, 8, 128), f32), SDS((NQ, BLK, BLK), f32)],
        name="attn_bwd", sem=("arbitrary",), vmem_mb=40, comm=comm)


def _bias_bwd(dsa):
    def body(bk_ref, ds_ref, out_ref):
        bk = bk_ref[...]
        lane = lax.broadcasted_iota(jnp.int32, (1, 128), 1)
        for h in range(NQ):
            ds = ds_ref[h]
            row = jnp.zeros((1, 128), f32)
            for b in range(NBUCKET):
                row = jnp.where(lane == b, jnp.sum(jnp.where(bk == b, ds, 0.0)), row)
            out_ref[h:h + 1, :] = row

    return pl.pallas_call(body, out_shape=SDS((NQ, 128), f32), name="bias_bwd")(jnp.asarray(_bucket_tile()), dsa)


def _qkv_bwd(proj, gq2, gk2, dqn, dkc, dkp, dvc, dvp, comm=None):
    T = proj.shape[0]
    tm = 512

    def body(q_ref, kv_ref, gq_ref, gk_ref, dq_ref, dkc_ref, dkp_ref, dvc_ref, dvp_ref,
             oq_ref, okv_ref, ggq_ref, ggk_ref):
        @pl.when(pl.program_id(0) == 0)
        def _():
            ggq_ref[...] = jnp.zeros_like(ggq_ref)
            ggk_ref[...] = jnp.zeros_like(ggk_ref)

        bd = _head_blockdiag()

        def norm_bwd(z, dy, g, scale):
            r = lax.rsqrt(_head_sums(z * z, bd) * (1.0 / HD) + EPS)
            gd = dy * g * scale
            dz = r * gd - z * (r * r * r) * _head_sums(z * gd, bd) * (1.0 / HD)
            return dz, jnp.sum(dy * scale * z * r, axis=0, keepdims=True)

        gq = jnp.zeros((1, 128), f32)
        for p in range(NQ // 2):
            ln = slice(128 * p, 128 * p + 128)
            dz, dg = norm_bwd(q_ref[:, ln], dq_ref[:, ln].astype(f32), gq_ref[...], HD ** -0.5)
            oq_ref[:, ln] = dz.astype(bf16)
            gq = gq + dg
        ggq_ref[...] += gq + pltpu.roll(gq, HD, 1)
        gk = jnp.zeros((1, 128), f32)
        for p in range(NKV // 2):
            ln = slice(128 * p, 128 * p + 128)
            dz, dg = norm_bwd(kv_ref[:, ln], dkc_ref[:, ln].astype(f32) + dkp_ref[:, ln].astype(f32), gk_ref[...], 1.0)
            okv_ref[:, ln] = dz.astype(bf16)
            gk = gk + dg
        ggk_ref[...] += gk + pltpu.roll(gk, HD, 1)
        okv_ref[:, 256:512] = (dvc_ref[...].astype(f32) + dvp_ref[...].astype(f32)).astype(bf16)

    vec = pl.BlockSpec((1, 128), lambda i: (0, 0))
    kvb = pl.BlockSpec((tm, NKV * HD), lambda i: (i, 0))
    return _call(
        body, (proj, proj, gq2, gk2, dqn, dkc, dkp, dvc, dvp), grid=(T // tm,),
        in_specs=[pl.BlockSpec((tm, D), lambda i: (i, C_Q)), pl.BlockSpec((tm, 512), lambda i: (i, C_KV)), vec, vec,
                  pl.BlockSpec((tm, D), lambda i: (i, 0)), kvb, kvb, kvb, kvb],
        out_specs=[pl.BlockSpec((tm, D), lambda i: (i, 0)), pl.BlockSpec((tm, 512), lambda i: (i, 0)), vec, vec],
        out_shape=[SDS((T, D), bf16), SDS((T, 512), bf16), SDS((1, 128), f32), SDS((1, 128), f32)],
        name="qkv_bwd", sem=("arbitrary",), vmem_mb=32, comm=comm)


def _inproj_bwd(pieces, w_in, x, dx1, g_mix, comm=None):
    T = x.shape[0]
    tm = 512
    widths = [p.shape[1] for p in pieces]
    offs = [sum(widths[:i]) for i in range(len(widths))]
    assert sum(widths) == INW

    def body(*refs):
        p_refs, (w_ref, x_ref, dx1_ref, g_ref, dx_ref, dg_ref) = refs[:len(pieces)], refs[len(pieces):]

        @pl.when(pl.program_id(0) == 0)
        def _():
            dg_ref[...] = jnp.zeros_like(dg_ref)

        du = None
        for p_ref, off, wd in zip(p_refs, offs, widths):
            part = _dot_nt(p_ref[...], w_ref[:, _weight_cols(off, wd)])
            du = part if du is None else du + part
        dx, dg = _rms_bwd(x_ref[...], g_ref[...], du, dx1_ref[...])
        dx_ref[...] = dx
        dg_ref[...] += dg

    row = pl.BlockSpec((tm, D), lambda i: (i, 0))
    vec = pl.BlockSpec((1, D), lambda i: (0, 0))
    return _call(
        body, (*pieces, w_in, x, dx1, g_mix), grid=(T // tm,),
        in_specs=[pl.BlockSpec((tm, wd), lambda i: (i, 0)) for wd in widths]
        + [pl.BlockSpec((D, INW), lambda i: (0, 0), pipeline_mode=pl.Buffered(1)), row, row, vec],
        out_specs=[row, vec],
        out_shape=[SDS((T, D), f32), SDS((1, D), f32)],
        name="inproj_bwd", sem=("arbitrary",), vmem_mb=48, comm=comm)


def _forward_backward(x, tgt, w, placed, chip_core):
    def sums(names, grads, got):
        res = [_pair_sum(nm, grads[nm], got_nm, chip_core) for nm, got_nm in zip(names, got)]
        return {nm: r[0] for nm, r in zip(names, res)}, {nm: r[1] for nm, r in zip(names, res)}

    first = ["w_in", "w_dw"]
    w_in, w_dw = _run_comm(_gather_comm({nm: placed[nm] for nm in first}), "gather_first")
    gq2 = jnp.tile(w["q_norm_g"], (1, 2))
    gk2 = jnp.tile(w["k_norm_g"], (1, 2))
    def gathered_in(names):
        return names, _gather_comm({nm: placed[nm] for nm in names})

    full = {}
    perm, perm_t = _block_perm()
    names, comm = gathered_in(["w_out", "w_attn_o", "w_conv_out"])
    (proj, u, tail, qn, kk, vv), got = _rms_inproj(x, w["norm_mix_g"], w_in, perm, gq2, gk2, comm=comm)
    full.update(zip(names, got))
    bias = _bias_tiles(w["rel_bias"])
    names, comm = gathered_in(["w_ff1"])
    (o, lse), got = _attn_fwd(qn, kk, vv, bias, w["attn_sinks"], comm=comm)
    full.update(zip(names, got))
    names, comm = gathered_in(["w_ff2"])
    h1, h3, got = _glu_conv_fwd(proj, tail, w_dw, w["b_dw"], w["conv_ln_g"], w["conv_ln_b"], perm_t, comm=comm)
    full.update(zip(names, got))
    attn, conv, merged, x1, n2 = _mix_out(o, h3, proj, x, full["w_attn_o"], full["w_conv_out"], full["w_out"],
                                          w["norm_mlp_g"])
    hmid, slope, dy, dyb, loss = _mlp_fwd(n2, full["w_ff1"], full["w_ff2"], x1, tgt)

    g = {}
    df1, dx1, dx1b, g["norm_mlp_g"] = _mlp_bwd(dy, dyb, slope, full["w_ff1"], full["w_ff2"], x1, w["norm_mlp_g"])
    ff = ["w_ff1", "w_ff2"]
    gff = {"w_ff2": _wgrad(hmid, dyb, "wgrad_ff2"), "w_ff1": _wgrad(n2, df1, "wgrad_ff1")}
    (dat, dcv, do, dh1, dga, dgc, lnacc, head), got = _mix_bwd(
        dx1b, proj, attn, conv, h1, full["w_attn_o"], full["w_conv_out"], full["w_out"], w["conv_ln_g"],
        w["conv_ln_b"], perm, comm=_pair_exchange_comm(gff, ff))
    g["conv_ln_g"], g["conv_ln_b"], g["b_dw"] = lnacc[0:1], lnacc[1:2], lnacc[2:3]
    cp_ff, own_ff = sums(ff, gff, got)
    sq = ["w_out", "w_attn_o", "w_conv_out"]
    gsq = {"w_out": _wgrad(merged, dx1b, "wgrad_out"), "w_attn_o": _wgrad(o, dat, "wgrad_attn_o"),
           "w_conv_out": _wgrad(h3, dcv, "wgrad_conv_out")}
    (da, dg, g["w_dw"]), got = _conv_bwd(dh1, head, proj, tail, w_dw, perm_t, comm=_merge_comms(
        _pair_exchange_comm(gsq, sq), _chip_exchange_comm(cp_ff, ff)))
    cp_sq, own_sq = sums(sq, gsq, got[:len(sq)])
    tot_ff = {nm: _chip_sum(nm, own_ff[nm], rc_nm, chip_core) for nm, rc_nm in zip(ff, got[len(sq):])}
    (dqn, dkc, dkp, dvc, dvp, dsk, dsa), got = _attn_bwd(qn, kk, vv, bias, w["attn_sinks"], o, do, lse, comm=_merge_comms(
        _chip_exchange_comm(cp_sq, sq), _pair_share_comm(tot_ff, ff)))
    tot_sq = {nm: _chip_sum(nm, own_sq[nm], rc_nm, chip_core) for nm, rc_nm in zip(sq, got[:len(sq)])}
    shards = dict(zip(ff, got[len(sq):]))
    g["attn_sinks"] = dsk[:, 0, 0].reshape(1, NQ)
    g["rel_bias"] = _bias_bwd(dsa)[:, 0:NBUCKET].T
    (dq, dkv, ggq, ggk), got = _qkv_bwd(proj, gq2, gk2, dqn, dkc, dkp, dvc, dvp, comm=_pair_share_comm(tot_sq, sq))
    shards.update(zip(sq, got))
    g["q_norm_g"], g["k_norm_g"] = ggq[:, 0:HD], ggk[:, 0:HD]
    pieces = [dq, da, dg, dga, dgc, dkv]
    names = ["q", "a", "g", "ga", "gc", "kv"]
    gw = {nm: _wgrad(u, p, "wgrad_in_" + nm, tn=p.shape[1] if p.shape[1] < 1024 else 1024) for nm, p in zip(names, pieces)}
    gin = {"w_in": jnp.concatenate([gw["q"], gw["kv"], gw["a"], gw["g"], gw["ga"], gw["gc"]], axis=1)}
    got = _run_comm(_pair_exchange_comm(gin, ["w_in"]), "rs_pair_exchange_in")
    cp_in, own_in = sums(["w_in"], gin, got)
    (grad_x, g["norm_mix_g"]), rc = _inproj_bwd(pieces, w_in, x, dx1, w["norm_mix_g"],
                                                comm=_chip_exchange_comm(cp_in, ["w_in"]))
    tot = {"w_in": _chip_sum("w_in", own_in["w_in"], rc[0], chip_core)}
    shards["w_in"] = _run_comm(_pair_share_comm(tot, ["w_in"]), "rs_pair_share_in")[0]
    return loss[0, 0], grad_x, g, shards


BIG = ["w_in", "w_attn_o", "w_conv_out", "w_out", "w_ff1", "w_ff2"]
SHARD_AXIS = {"w_in": 1, "w_attn_o": 0, "w_conv_out": 0, "w_out": 0, "w_ff1": 1, "w_ff2": 0, "w_dw": 1}
SHARD_SHAPE = {"w_in": (D, INW // 4), "w_attn_o": (D // 4, D), "w_conv_out": (D // 4, D), "w_out": (D // 4, D),
               "w_ff1": (D, DFF // 4), "w_ff2": (DFF // 4, D), "w_dw": (HALO, D // 4)}


def _position():
    x, y, c = lax.axis_index("x"), lax.axis_index("y"), lax.axis_index("c")
    other_chips = [(1 - x, y), (x, 1 - y), (1 - x, 1 - y)]
    return x, y, c, 2 * x + y, other_chips


def _shard_window(name, full_ref, s, half=None):
    R, C = SHARD_SHAPE[name]
    r0, nr = (0, R) if half is None else (half * (R // 2), R // 2)
    if SHARD_AXIS[name] == 1:
        return full_ref.at[pl.ds(r0, nr), pl.ds(s * C, C)]
    return full_ref.at[pl.ds(s * R + r0, nr), :]


def _remote(src, dst, send_sems, recv_sems, k, device):
    return pltpu.make_async_remote_copy(src_ref=src, dst_ref=dst, send_sem=send_sems.at[k], recv_sem=recv_sems.at[k],
                                        device_id=device, device_id_type=MESH)


def _full_shape(nm):
    R, C = SHARD_SHAPE[nm]
    return (R, 4 * C) if SHARD_AXIS[nm] == 1 else (4 * R, C)


def _place_shard(nm, shard, chip_arr, dtype):
    R, C = SHARD_SHAPE[nm]
    tr = min(R, 256)
    if SHARD_AXIS[nm] == 1:
        o_map = lambda i, ch: (i, ch[0])
    else:
        o_map = lambda i, ch: (ch[0] * (R // tr) + i, 0)

    def body(ch_ref, s_ref, o_ref):
        o_ref[...] = s_ref[...].astype(dtype)

    return pl.pallas_call(
        body,
        grid_spec=pltpu.PrefetchScalarGridSpec(
            num_scalar_prefetch=1, grid=(R // tr,),
            in_specs=[pl.BlockSpec((tr, C), lambda i, ch: (i, 0))], out_specs=pl.BlockSpec((tr, C), o_map)),
        out_shape=SDS(_full_shape(nm), dtype), name="place_" + nm,
        compiler_params=_cparams(("parallel",), 32))(chip_arr, shard)


def _gather_comm(placed):
    names = list(placed)
    n = len(names)

    def copies(cout, send, recv):
        x, y, c, chip, chips = _position()
        for a, nm in enumerate(names):
            for j, (cx, cy) in enumerate(chips):
                def ici(s, a=a, nm=nm, j=j, cx=cx, cy=cy):
                    w = _shard_window(nm, cout[a], s, c)
                    return _remote(w, w, send, recv, 6 * a + j, (cx, cy, c))

                def d2d(h, a=a, nm=nm, j=j, cx=cx, cy=cy):
                    w = _shard_window(nm, cout[a], 2 * cx + cy, h)
                    return _remote(w, w, send, recv, 6 * a + 3 + j, (x, y, 1 - c))

                yield ici, d2d, chip, 2 * cx + cy, c

    def start(cin, cout, send, recv):
        for ici, d2d, chip, s, c in copies(cout, send, recv):
            ici(chip).start()

    def mid(cin, cout, send, recv):
        for ici, d2d, chip, s, c in copies(cout, send, recv):
            ici(s).wait_recv()
            d2d(c).start()

    def finish(cin, cout, send, recv):
        for ici, d2d, chip, s, c in copies(cout, send, recv):
            d2d(1 - c).wait_recv()
        for ici, d2d, chip, s, c in copies(cout, send, recv):
            ici(chip).wait_send()
            d2d(c).wait_send()

    return _Comm([placed[nm] for nm in names], [SDS(placed[nm].shape, placed[nm].dtype) for nm in names], 6 * n,
                 start, finish, mid, aliases={a: a for a in range(n)})


def _half_rows(nm):
    return SHARD_SHAPE[nm][0] // 2


RS_TILE = 128
PAIR_SUM_TILE = 512


def _exchange_comm(ins, out_shapes, copies, n_sems, aliases=None):
    def start(cin, cout, send, recv):
        for cp in copies(cin, cout, send, recv):
            cp.start()

    def finish(cin, cout, send, recv):
        for cp in copies(cin, cout, send, recv):
            cp.wait()

    return _Comm(ins, out_shapes, n_sems, start, finish, aliases=aliases)


def _pair_exchange_comm(grads, names):
    def copies(cin, cout, send, recv):
        x, y, c, chip, chips = _position()
        return [_remote(_shard_window(nm, cin[a], s, 1 - c), cout[a].at[s], send, recv, 4 * a + s, (x, y, 1 - c))
                for a, nm in enumerate(names) for s in range(4)]

    return _exchange_comm([grads[nm] for nm in names],
                          [SDS((4, _half_rows(nm), SHARD_SHAPE[nm][1]), f32) for nm in names], copies, 4 * len(names))


def _pair_sum(nm, g, got, chip_core):
    R, C = SHARD_SHAPE[nm]
    hr = R // 2
    tile = min(hr, PAIR_SUM_TILE)
    nt = hr // tile
    if SHARD_AXIS[nm] == 1:
        g_map = lambda i, s, sc: (sc[1] * nt + i, s)
    else:
        g_map = lambda i, s, sc: (s * (R // tile) + sc[1] * nt + i, 0)

    def body(sc_ref, g_ref, got_ref, o16_ref, own_ref):
        v = g_ref[...] + got_ref[0]
        o16_ref[0] = v.astype(bf16)

        @pl.when(pl.program_id(1) == sc_ref[0])
        def _():
            own_ref[...] = v

    blk3 = pl.BlockSpec((1, tile, C), lambda i, s, sc: (s, i, 0))
    return pl.pallas_call(
        body,
        grid_spec=pltpu.PrefetchScalarGridSpec(
            num_scalar_prefetch=1, grid=(nt, 4),
            in_specs=[pl.BlockSpec((tile, C), g_map), blk3],
            out_specs=[blk3, pl.BlockSpec((tile, C), lambda i, s, sc: (i, 0))]),
        out_shape=[SDS((4, hr, C), bf16), SDS((hr, C), f32)], name="rs_pair_sum_" + nm,
        compiler_params=_cparams(("parallel", "arbitrary"), 32))(chip_core, g, got)


def _chip_exchange_comm(cp, names):
    def copies(cin, cout, send, recv):
        x, y, c, chip, chips = _position()
        return [_remote(cin[a].at[2 * cx + cy], cout[a].at[j], send, recv, 3 * a + j, (cx, cy, c))
                for a, nm in enumerate(names) for j, (cx, cy) in enumerate(chips)]

    return _exchange_comm([cp[nm] for nm in names],
                          [SDS((3, _half_rows(nm), SHARD_SHAPE[nm][1]), bf16) for nm in names], copies, 3 * len(names))


def _chip_sum(nm, own, rc, chip_core):
    R, C = SHARD_SHAPE[nm]
    RS_TILE = min(R // 2, PAIR_SUM_TILE)
    nt = (R // 2) // RS_TILE

    def body(sc_ref, own_ref, rc_ref, o_ref):
        o_ref[...] = own_ref[...] + rc_ref[0].astype(f32) + rc_ref[1].astype(f32) + rc_ref[2].astype(f32)

    return pl.pallas_call(
        body,
        grid_spec=pltpu.PrefetchScalarGridSpec(
            num_scalar_prefetch=1, grid=(nt,),
            in_specs=[pl.BlockSpec((RS_TILE, C), lambda i, sc: (i, 0)),
                      pl.BlockSpec((3, RS_TILE, C), lambda i, sc: (0, i, 0))],
            out_specs=pl.BlockSpec((RS_TILE, C), lambda i, sc: (sc[1] * nt + i, 0))),
        out_shape=SDS((R, C), f32), name="rs_chip_sum_" + nm,
        compiler_params=_cparams(("parallel",), 32))(chip_core, own, rc)


def _pair_share_comm(tot, names):
    def copies(cin, cout, send, recv):
        x, y, c, chip, chips = _position()
        cps = []
        for a, nm in enumerate(names):
            hr = _half_rows(nm)
            mine = cout[a].at[pl.ds(c * hr, hr), :]
            cps.append(_remote(mine, mine, send, recv, a, (x, y, 1 - c)))
        return cps

    return _exchange_comm([tot[nm] for nm in names], [SDS(SHARD_SHAPE[nm], f32) for nm in names], copies, len(names),
                          aliases={a: a for a in range(len(names))})


SMALL_ROWS = 40


def _allreduce_small(block):
    def body(x_ref, out_ref, buf, send_sems, recv_sems, local_sem):
        x, y, c, chip, chips = _position()
        me, sibling = (x, y, c), (x, y, 1 - c)

        def slot(px, py, pc):
            return buf.at[4 * px + 2 * py + pc]

        def copy(k, block_of, to, src=None):
            return _remote(slot(*block_of) if src is None else src, slot(*block_of), send_sems, recv_sems, k, to)

        mine = pltpu.make_async_copy(x_ref, slot(*me), local_sem)
        mine.start()
        first = [copy(0, me, sibling, src=x_ref)] + [copy(1 + j, me, (*ch, c), src=x_ref) for j, ch in enumerate(chips)]
        for cp in first:
            cp.start()
        passed = [copy(4 + j, (*ch, c), sibling) for j, ch in enumerate(chips)]
        for j, ch in enumerate(chips):
            copy(1 + j, (*ch, c), me).wait_recv()
            passed[j].start()
        copy(0, sibling, me).wait_recv()
        for j, ch in enumerate(chips):
            copy(4 + j, (*ch, 1 - c), me).wait_recv()
        for cp in first + passed:
            cp.wait_send()
        mine.wait()
        acc = buf[0]
        for d in range(1, 8):
            acc = acc + buf[d]
        out_ref[...] = acc

    vm = pl.BlockSpec(memory_space=pltpu.VMEM)
    return pl.pallas_call(
        body, in_specs=[vm], out_specs=vm, out_shape=SDS((SMALL_ROWS, D), f32),
        scratch_shapes=[pltpu.VMEM((8, SMALL_ROWS, D), f32), pltpu.SemaphoreType.DMA((7,)), pltpu.SemaphoreType.DMA((7,)),
                        pltpu.SemaphoreType.DMA],
        name="allreduce_small")(block)


def _adamw(w, g, m, v, name):
    rows, cols = w.shape
    tr = 256 if rows % 256 == 0 else rows

    def body(w_ref, g_ref, m_ref, v_ref, d_ref, nm_ref, nv_ref):
        gv = g_ref[...]
        m2 = ADAM_B1 * m_ref[...] + (1.0 - ADAM_B1) * gv
        v2 = ADAM_B2 * v_ref[...] + (1.0 - ADAM_B2) * jnp.square(gv)
        m_hat = m2 / (1.0 - ADAM_B1 ** ADAM_STEP)
        v_hat = v2 / (1.0 - ADAM_B2 ** ADAM_STEP)
        d_ref[...] = -ADAM_LR * (m_hat / (jnp.sqrt(v_hat) + ADAM_EPS) + ADAM_WD * w_ref[...])
        nm_ref[...] = m2
        nv_ref[...] = v2

    spec = pl.BlockSpec((tr, cols), lambda i: (i, 0))
    return pl.pallas_call(body, grid=(rows // tr,), in_specs=[spec] * 4, out_specs=[spec] * 3,
                          out_shape=[SDS((rows, cols), f32)] * 3, name=name,
                          compiler_params=_cparams(("parallel",), 40))(w, g, m, v)


WEIGHTS = ["norm_mix_g", "w_in", "q_norm_g", "k_norm_g", "attn_sinks", "rel_bias", "w_attn_o", "w_dw", "b_dw",
           "conv_ln_g", "conv_ln_b", "w_conv_out", "w_out", "norm_mlp_g", "w_ff1", "w_ff2"]
ROW_VECS = ["norm_mix_g", "b_dw", "conv_ln_g", "conv_ln_b", "norm_mlp_g"]
MISC_ROW = 5
W_DW_ROW = 8


def _pack_small(vals, loss=None):
    misc = [vals["q_norm_g"].reshape(1, HD), vals["k_norm_g"].reshape(1, HD), vals["attn_sinks"].reshape(1, NQ),
            jnp.zeros((1, 1), f32) if loss is None else loss.reshape(1, 1), jnp.zeros((1, 111), f32),
            vals["rel_bias"].reshape(1, NBUCKET * NQ), jnp.zeros((1, 256), f32)]
    rows = [vals[nm].reshape(1, D) for nm in ROW_VECS] + [jnp.concatenate(misc, axis=1), jnp.zeros((2, D), f32)]
    return jnp.concatenate(rows, axis=0)


def _unpack_small(block):
    out = {nm: block[i:i + 1] for i, nm in enumerate(ROW_VECS)}
    misc = block[MISC_ROW]
    out["q_norm_g"] = misc[0:64].reshape(1, HD)
    out["k_norm_g"] = misc[64:128].reshape(1, HD)
    out["attn_sinks"] = misc[128:144].reshape(1, NQ)
    out["rel_bias"] = misc[256:768].reshape(NBUCKET, NQ)
    return out, misc[144]


def kernel(x, norm_mix_g, w_in, q_norm_g, k_norm_g, attn_sinks, rel_bias, w_attn_o, w_dw, b_dw, conv_ln_g, conv_ln_b, w_conv_out, w_out, norm_mlp_g, w_ff1, w_ff2, loss_target, m_norm_mix_g, m_w_in, m_q_norm_g, m_k_norm_g, m_attn_sinks, m_rel_bias, m_w_attn_o, m_w_dw, m_b_dw, m_conv_ln_g, m_conv_ln_b, m_w_conv_out, m_w_out, m_norm_mlp_g, m_w_ff1, m_w_ff2, v_norm_mix_g, v_w_in, v_q_norm_g, v_k_norm_g, v_attn_sinks, v_rel_bias, v_w_attn_o, v_w_dw, v_b_dw, v_conv_ln_g, v_conv_ln_b, v_w_conv_out, v_w_out, v_norm_mlp_g, v_w_ff1, v_w_ff2):
    args = dict(locals())
    wts = {nm: args[nm] for nm in WEIGHTS}
    mom = {nm: args["m_" + nm] for nm in WEIGHTS}
    var = {nm: args["v_" + nm] for nm in WEIGHTS}
    chip = 2 * lax.axis_index("x") + lax.axis_index("y")

    chip_arr = jnp.reshape(chip, (1,)).astype(jnp.int32)
    chip_core = jnp.stack([chip, lax.axis_index("c")]).astype(jnp.int32)
    placed = {nm: _place_shard(nm, wts[nm][0], chip_arr, bf16) for nm in BIG}
    placed["w_dw"] = _place_shard("w_dw", jnp.pad(w_dw[0], ((0, 1), (0, 0))), chip_arr, f32)

    loss_part, grad_x, g, shards = _forward_backward(x[0], loss_target[0], wts, placed, chip_core)

    small = jnp.concatenate([_pack_small(g, loss_part), g["w_dw"]], axis=0)
    small = _allreduce_small(small)
    grads, loss = _unpack_small(small)
    grads["w_dw"] = lax.dynamic_slice(small[W_DW_ROW:W_DW_ROW + CW], (0, chip * (D // 4)), (CW, D // 4))
    grads.update(shards)

    delta, new_m, new_v = {}, {}, {}
    sd, sm, sv = _adamw(_pack_small(wts), small[0:8], _pack_small(mom), _pack_small(var), "adamw_small")
    for res, blk in ((delta, sd), (new_m, sm), (new_v, sv)):
        res.update(_unpack_small(blk)[0])
    for nm in BIG + ["w_dw"]:
        shp = wts[nm].shape
        two_d = lambda a: a.reshape(shp[-2], shp[-1])
        delta[nm], new_m[nm], new_v[nm] = _adamw(two_d(wts[nm]), grads[nm], two_d(mom[nm]), two_d(var[nm]), "adamw_" + nm)

    def shaped(vals):
        return [vals[nm].reshape(wts[nm].shape) for nm in WEIGHTS]

    return (loss, grad_x[None], *shaped(grads), *shaped(delta), *shaped(new_m), *shaped(new_v))
```
